```python
import math
import functools
import jax
import jax.numpy as jnp
from jax import lax
import numpy as np

D_MODEL = 2048
BATCH = 4
SEQ = 2048
DEPTH = 4

N_EVEN = (DEPTH + 1) // 2
N_ODD = DEPTH // 2
N_MEM = 256
RMS_EPS = 1e-6
D_FF = 4 * D_MODEL

DN_HEADS = 8
DN_HEAD_DIM = 128
DN_WIDTH = DN_HEADS * DN_HEAD_DIM
DN_CONV = 3
DN_CHUNK = 64
SC_WIDTH = D_MODEL - DN_WIDTH
SC_CONV = 3
EV_IN = 4 * DN_WIDTH + 4 * DN_HEADS + 3 * SC_WIDTH
EV_OUT = DN_WIDTH + SC_WIDTH

DSA_PATTERNS = ((128, 1), (512, 4), (2048, 16))
DSA_GROUPS = len(DSA_PATTERNS)
DSA_HEADS_PER_GROUP = 4
DSA_HEAD_DIM = 128
DSA_HEADS = DSA_GROUPS * DSA_HEADS_PER_GROUP
DSA_QKV = DSA_HEADS * DSA_HEAD_DIM
DSA_OUT = DSA_HEADS_PER_GROUP * DSA_HEAD_DIM
REL_BUCKETS = 32
REL_MAX_DIST = 1024
NEG_INF = -1e30

RW_HEADS = 8
RW_HEAD_DIM = 64
RW_WIDTH = RW_HEADS * RW_HEAD_DIM
RW_DECAY_LORA = 64
RW_AAA_LORA = 64
RW_GATE_LORA = 128
RW_SPLITS = (RW_WIDTH, RW_WIDTH, RW_WIDTH, RW_DECAY_LORA, RW_DECAY_LORA, RW_AAA_LORA, RW_AAA_LORA, RW_GATE_LORA)
RW_IN = sum(RW_SPLITS)
RW_GN_EPS = 64e-5
OD_IN = 3 * DSA_QKV + RW_IN
OD_OUT = DSA_OUT + RW_WIDTH

XA_HEADS = 4
XA_HEAD_DIM = 128
XA_WIDTH = XA_HEADS * XA_HEAD_DIM

kernel_name = 'hybrid_deltanet_shortconv_dilated_rwkv7_encoder'


def rms_norm(x, g):
    xf = x.astype(jnp.float32)
    y = xf * lax.rsqrt(jnp.mean(xf * xf, axis=-1, keepdims=True) + RMS_EPS)
    return (y * g.astype(jnp.float32)).astype(x.dtype)


def l2_normalize(x):
    return x * lax.rsqrt(jnp.sum(x * x, axis=-1, keepdims=True) + 1e-6)


def centred_dwconv(x, w):
    k = w.shape[-1]
    rhs = jnp.transpose(w)[:, None, :].astype(x.dtype)
    return lax.conv_general_dilated(x, rhs, window_strides=(1,), padding=[(k // 2, k // 2)],
                                    dimension_numbers=('NWC', 'WIO', 'NWC'),
                                    feature_group_count=x.shape[-1])


def centred_token_shift(t):
    tp = jnp.pad(t, ((0, 0), (1, 1), (0, 0)))
    return 0.5 * (tp[:, :-2] + tp[:, 2:])


def t5_bucket(rel):
    half = REL_BUCKETS // 2
    max_exact = half // 2
    n = jnp.abs(rel)
    scaled = jnp.log(jnp.maximum(n, max_exact).astype(jnp.float32) / max_exact) / math.log(REL_MAX_DIST / max_exact)
    large = jnp.minimum(max_exact + (scaled * (half - max_exact)).astype(jnp.int32), half - 1)
    return jnp.where(rel > 0, half, 0) + jnp.where(n < max_exact, n, large)


def gated_delta_rule(q, k, v, g, beta):
    B, S, H, DK = q.shape
    DV = v.shape[-1]
    C = DN_CHUNK
    N = S // C

    def to_chunks(t):
        return jnp.swapaxes(t.reshape((B, N, C, H) + t.shape[3:]), 2, 3)

    qc, kc, vc = to_chunks(q), to_chunks(k), to_chunks(v)
    gc = jnp.cumsum(to_chunks(g), axis=-1)
    bc = to_chunks(beta)
    kb = kc * bc[..., None]
    vb = vc * bc[..., None]
    lower = jnp.tril(jnp.ones((C, C), dtype=bool))
    strict = jnp.tril(jnp.ones((C, C), dtype=bool), -1)
    diff = gc[..., :, None] - gc[..., None, :]
    decay = jnp.where(lower, jnp.exp(jnp.where(lower, diff, 0.0)), 0.0)
    a_mat = jnp.where(strict, jnp.einsum('bnhid,bnhjd->bnhij', kb, kc) * decay, 0.0)
    unit_lower = a_mat + jnp.eye(C, dtype=a_mat.dtype)
    solve = functools.partial(lax.linalg.triangular_solve, left_side=True, lower=True, unit_diagonal=True)
    u = solve(unit_lower, vb)
    w = solve(unit_lower, kb * jnp.exp(gc)[..., None])
    qk = jnp.where(lower, jnp.einsum('bnhid,bnhjd->bnhij', qc, kc) * decay, 0.0)

    def step(state, inp):
        q_i, k_i, u_i, w_i, g_i, qk_i = inp
        v_new = u_i - jnp.einsum('bhcd,bhde->bhce', w_i, state)
        o = (jnp.einsum('bhcd,bhde->bhce', q_i * jnp.exp(g_i)[..., None], state)
             + jnp.einsum('bhij,bhje->bhie', qk_i, v_new))
        g_last = g_i[..., -1:]
        state = (state * jnp.exp(g_last)[..., None]
                 + jnp.einsum('bhcd,bhce->bhde', k_i * jnp.exp(g_last - g_i)[..., None], v_new))
        return state, o

    xs = tuple(jnp.swapaxes(t, 0, 1) for t in (qc, kc, u, w, gc, qk))
    state0 = jnp.zeros((B, H, DK, DV), jnp.float32)
    _, o = lax.scan(step, state0, xs)
    return jnp.transpose(o, (1, 0, 3, 2, 4)).reshape(B, S, H, DV)


def even_mixer(h, w_in, w_out, conv_qkv, a_log, dt_bias, out_gain, conv_sc):
    B, S, _ = h.shape
    f32 = jnp.float32
    p = h @ w_in
    c1 = 3 * DN_WIDTH
    c2 = c1 + DN_WIDTH
    c3 = c2 + 4 * DN_HEADS
    qkv = jax.nn.silu(centred_dwconv(p[..., :c1], conv_qkv)).astype(f32)
    q, k, v = [t.reshape(B, S, DN_HEADS, DN_HEAD_DIM) for t in jnp.split(qkv, 3, axis=-1)]
    q = l2_normalize(q) * (DN_HEAD_DIM ** -0.5)
    k = l2_normalize(k)
    ab = p[..., c2:c3].astype(f32).reshape(B, S, 4, DN_HEADS)
    beta = jax.nn.sigmoid(ab[:, :, 0:2])
    g = -jnp.exp(a_log.astype(f32)) * jax.nn.softplus(ab[:, :, 2:4] + dt_bias.astype(f32))
    flip = lambda t: jnp.flip(t, axis=1)
    o_fwd = gated_delta_rule(q, k, v, g[:, :, 0], beta[:, :, 0])
    o_bwd = flip(gated_delta_rule(flip(q), flip(k), flip(v), flip(g[:, :, 1]), flip(beta[:, :, 1])))
    o = o_fwd + o_bwd
    o = o * lax.rsqrt(jnp.mean(o * o, axis=-1, keepdims=True) + RMS_EPS) * out_gain.astype(f32)
    gate = p[..., c1:c2].astype(f32).reshape(B, S, DN_HEADS, DN_HEAD_DIM)
    y_dn = (o * jax.nn.silu(gate)).reshape(B, S, DN_WIDTH).astype(h.dtype)
    b_gate, c_gate, u_in = jnp.split(p[..., c3:], 3, axis=-1)
    y_sc = b_gate * centred_dwconv(c_gate * u_in, conv_sc)
    return jnp.concatenate([y_dn, y_sc.astype(h.dtype)], axis=-1) @ w_out


def dilated_window_attention(q, k, v, bias, dil, n_side):
    B, S, H, E = q.shape
    f32 = jnp.float32
    L = S // dil
    blk = n_side
    nb = -(-L // blk)
    Lp = nb * blk

    def sub(t):
        return jnp.swapaxes(t.astype(f32).reshape(B, L, dil, H, E), 1, 2)

    qs, ks, vs = sub(q), sub(k), sub(v)
    qb = jnp.pad(qs, ((0, 0), (0, 0), (0, Lp - L), (0, 0), (0, 0))).reshape(B, dil, nb, blk, H, E)
    pad_kv = ((0, 0), (0, 0), (blk, Lp - L + blk), (0, 0), (0, 0))
    kp = jnp.pad(ks, pad_kv).reshape(B, dil, nb + 2, blk, H, E)
    vp = jnp.pad(vs, pad_kv).reshape(B, dil, nb + 2, blk, H, E)
    kw = jnp.concatenate([kp[:, :, 0:nb], kp[:, :, 1:nb + 1], kp[:, :, 2:nb + 2]], axis=3)
    vw = jnp.concatenate([vp[:, :, 0:nb], vp[:, :, 1:nb + 1], vp[:, :, 2:nb + 2]], axis=3)
    qi = jnp.arange(blk)[:, None]
    kj = jnp.arange(3 * blk)[None, :]
    rel = kj - blk - qi
    key_idx = jnp.arange(nb)[:, None, None] * blk + (kj - blk)[None]
    valid = (jnp.abs(rel) <= n_side)[None] & (key_idx >= 0) & (key_idx < L)
    bias_full = bias[:, jnp.clip(rel + n_side, 0, 2 * n_side)]
    logits = jnp.einsum('bdnqhe,bdnkhe->bdnhqk', qb, kw) * (E ** -0.5) + bias_full
    logits = jnp.where(valid[:, None], logits, NEG_INF)
    m = jnp.max(logits, axis=-1, keepdims=True)
    p = jnp.exp(logits - m)
    s = jnp.sum(p, axis=-1, keepdims=True)
    o = jnp.einsum('bdnhqk,bdnkhe->bdnqhe', p, vw) * jnp.swapaxes(1.0 / s, 3, 4)
    lse = jnp.swapaxes((m + jnp.log(s))[..., 0], 3, 4)
    o = jnp.swapaxes(o.reshape(B, dil, Lp, H, E)[:, :, :L], 1, 2).reshape(B, S, H, E)
    lse = jnp.swapaxes(lse.reshape(B, dil, Lp, H)[:, :, :L], 1, 2).reshape(B, S, H)
    return o, lse


def rwkv7_scan(r, w, k, v, a, b, reverse):
    B, S, H, N = r.shape

    def step(state, inp):
        r_t, w_t, k_t, v_t, a_t, b_t = inp
        sa = jnp.einsum('bhij,bhj->bhi', state, a_t)
        state = (state * w_t[:, :, None, :] + sa[..., None] * b_t[:, :, None, :]
                 + v_t[..., None] * k_t[:, :, None, :])
        return state, jnp.einsum('bhij,bhj->bhi', state, r_t)

    xs = tuple(jnp.moveaxis(t, 1, 0) for t in (r, w, k, v, a, b))
    _, y = lax.scan(step, jnp.zeros((B, H, N, N), jnp.float32), xs, reverse=reverse)
    return jnp.moveaxis(y, 0, 1)


def odd_mixer(h, w_in, w_out, qn, kn, rel_bias, mu, w0, w2, a0, a2, g2, k_k, k_a, r_k, ln_w, ln_b):
    B, S, _ = h.shape
    f32 = jnp.float32
    p = h @ w_in
    q, k, v = jnp.split(p[..., :3 * DSA_QKV], 3, axis=-1)
    q = rms_norm(q.reshape(B, S, DSA_HEADS, DSA_HEAD_DIM), qn)
    k = rms_norm(k.reshape(B, S, DSA_HEADS, DSA_HEAD_DIM), kn)
    v = v.reshape(B, S, DSA_HEADS, DSA_HEAD_DIM)
    outs, lses = [], []
    for gi, (window, dil) in enumerate(DSA_PATTERNS):
        n_side = window // (2 * dil)
        hs = slice(gi * DSA_HEADS_PER_GROUP, (gi + 1) * DSA_HEADS_PER_GROUP)
        offs = jnp.arange(-n_side, n_side + 1, dtype=jnp.int32) * dil
        bias = jnp.transpose(rel_bias[t5_bucket(offs), hs]).astype(f32)
        o_g, lse_g = dilated_window_attention(q[:, :, hs], k[:, :, hs], v[:, :, hs], bias, dil, n_side)
        outs.append(o_g)
        lses.append(lse_g)
    wts = jax.nn.softmax(jnp.stack(lses), axis=0)
    y_c = jnp.sum(wts[..., None] * jnp.stack(outs), axis=0).reshape(B, S, DSA_OUT)

    rw = p[..., 3 * DSA_QKV:]
    rw = (rw + mu * (centred_token_shift(rw) - rw)).astype(f32)
    r, kr, vr, wd_f, wd_b, ad_f, ad_b, gd = jnp.split(rw, np.cumsum(RW_SPLITS)[:-1].tolist(), axis=-1)
    heads = lambda t: t.reshape(B, S, RW_HEADS, RW_HEAD_DIM)
    kk = l2_normalize(heads(kr * k_k.astype(f32)))
    gate = jax.nn.sigmoid(gd) @ g2.astype(f32)
    rh, vh = heads(r), heads(vr)
    ys, bonuses = [], []
    for d, (wd, ad) in enumerate(((wd_f, ad_f), (wd_b, ad_b))):
        w_log = -jax.nn.softplus(-(w0[d].astype(f32) + jnp.tanh(wd) @ w2[d].astype(f32))) - 0.5
        decay = jnp.exp(-jnp.exp(w_log))
        a = jax.nn.sigmoid(a0[d].astype(f32) + ad @ a2[d].astype(f32))
        kdh = heads(kr * (1.0 + (a - 1.0) * k_a.astype(f32)))
        ys.append(rwkv7_scan(rh, heads(decay), kdh, vh, -kk, kk * heads(a), reverse=(d == 1)))
        bonuses.append(jnp.sum(rh * kdh * r_k.astype(f32), axis=-1, keepdims=True) * vh)
    y = ys[0] + ys[1]
    mean = jnp.mean(y, axis=-1, keepdims=True)
    var = jnp.mean(jnp.square(y - mean), axis=-1, keepdims=True)
    yn = ((y - mean) * lax.rsqrt(var + RW_GN_EPS)).reshape(B, S, RW_WIDTH) * ln_w.astype(f32) + ln_b.astype(f32)
    y_d = (yn + (bonuses[0] + bonuses[1]).reshape(B, S, RW_WIDTH)) * gate
    return jnp.concatenate([y_c.astype(h.dtype), y_d.astype(h.dtype)], axis=-1) @ w_out


def memory_cross_attention(h, mem_n, wq, wk, wv, wo, qn, kn):
    B, S, _ = h.shape
    M = mem_n.shape[1]
    q = rms_norm((h @ wq).reshape(B, S, XA_HEADS, XA_HEAD_DIM), qn)
    k = rms_norm((mem_n @ wk).reshape(B, M, XA_HEADS, XA_HEAD_DIM), kn)
    v = (mem_n @ wv).reshape(B, M, XA_HEADS, XA_HEAD_DIM)
    logits = jnp.einsum('bshe,bmhe->bhsm', q, k).astype(jnp.float32) * (XA_HEAD_DIM ** -0.5)
    p = jax.nn.softmax(logits, axis=-1)
    o = jnp.einsum('bhsm,bmhe->bshe', p.astype(v.dtype), v).reshape(B, S, XA_WIDTH)
    return o @ wo


def squared_relu_mlp(h, w1, w2):
    return jnp.square(jax.nn.relu(h @ w1)) @ w2


def setup_inputs(seed: int = 0) -> dict:
    key = jax.random.key(seed)
    ks = iter(jax.random.split(key, 48))
    f32 = jnp.float32
    D = D_MODEL
    NE, NO = N_EVEN, N_ODD

    def nrm(shape, scale):
        return jax.random.normal(next(ks), shape, f32) * scale

    def gain(shape):
        return 1.0 + 0.02 * jax.random.normal(next(ks), shape, f32)

    x = nrm((BATCH, SEQ, D), 1.0)
    mem = nrm((BATCH, N_MEM, D), 1.0)
    rel_bias = nrm((REL_BUCKETS, DSA_HEADS), 0.2)
    norm_mix = gain((DEPTH, D))
    norm_xattn = gain((DEPTH, D))
    norm_mem = gain((DEPTH, D))
    norm_ffn = gain((DEPTH, D))
    xa_wq = nrm((DEPTH, D, XA_WIDTH), D ** -0.5)
    xa_wk = nrm((DEPTH, D, XA_WIDTH), D ** -0.5)
    xa_wv = nrm((DEPTH, D, XA_WIDTH), D ** -0.5)
    xa_wo = nrm((DEPTH, XA_WIDTH, D), XA_WIDTH ** -0.5)
    xa_qn = gain((DEPTH, XA_HEAD_DIM))
    xa_kn = gain((DEPTH, XA_HEAD_DIM))
    ffn_w1 = nrm((DEPTH, D, D_FF), D ** -0.5)
    ffn_w2 = nrm((DEPTH, D_FF, D), D_FF ** -0.5)
    ev_w_in = nrm((NE, D, EV_IN), D ** -0.5)
    ev_w_out = nrm((NE, EV_OUT, D), EV_OUT ** -0.5)
    dn_conv = nrm((NE, 3 * DN_WIDTH, DN_CONV), DN_CONV ** -0.5)
    dn_a_log = jnp.log(jax.random.uniform(next(ks), (NE, 2, DN_HEADS), f32, 1.0, 16.0))
    dt = jnp.exp(jax.random.uniform(next(ks), (NE, 2, DN_HEADS), f32, math.log(1e-3), math.log(1e-1)))
    dn_dt_bias = dt + jnp.log(-jnp.expm1(-dt))
    dn_norm = gain((NE, DN_HEAD_DIM))
    sc_conv = nrm((NE, SC_WIDTH, SC_CONV), SC_CONV ** -0.5)
    od_w_in = nrm((NO, D, OD_IN), D ** -0.5)
    od_w_out = nrm((NO, OD_OUT, D), OD_OUT ** -0.5)
    ca_qn = gain((NO, DSA_HEAD_DIM))
    ca_kn = gain((NO, DSA_HEAD_DIM))
    rw_mu = jax.random.uniform(next(ks), (NO, RW_IN), f32, 0.0, 1.0)
    rw_w0 = -2.0 + nrm((NO, 2, RW_WIDTH), 0.5)
    rw_w2 = nrm((NO, 2, RW_DECAY_LORA, RW_WIDTH), 0.5 * RW_DECAY_LORA ** -0.5)
    rw_a0 = nrm((NO, 2, RW_WIDTH), 0.5)
    rw_a2 = nrm((NO, 2, RW_AAA_LORA, RW_WIDTH), 0.5 * RW_AAA_LORA ** -0.5)
    rw_g2 = nrm((NO, RW_GATE_LORA, RW_WIDTH), RW_GATE_LORA ** -0.5)
    rw_k_k = 0.85 + nrm((NO, RW_WIDTH), 0.02)
    rw_k_a = gain((NO, RW_WIDTH))
    rw_r_k = nrm((NO, RW_HEADS, RW_HEAD_DIM), 0.1)
    rw_ln_w = gain((NO, RW_WIDTH))
    rw_ln_b = nrm((NO, RW_WIDTH), 0.01)
    return {'x': x, 'mem': mem, 'rel_bias': rel_bias, 'norm_mix': norm_mix, 'norm_xattn': norm_xattn,
            'norm_mem': norm_mem, 'norm_ffn': norm_ffn, 'xa_wq': xa_wq, 'xa_wk': xa_wk, 'xa_wv': xa_wv,
            'xa_wo': xa_wo, 'xa_qn': xa_qn, 'xa_kn': xa_kn, 'ffn_w1': ffn_w1, 'ffn_w2': ffn_w2,
            'ev_w_in': ev_w_in, 'ev_w_out': ev_w_out, 'dn_conv': dn_conv, 'dn_a_log': dn_a_log,
            'dn_dt_bias': dn_dt_bias, 'dn_norm': dn_norm, 'sc_conv': sc_conv, 'od_w_in': od_w_in,
            'od_w_out': od_w_out, 'ca_qn': ca_qn, 'ca_kn': ca_kn, 'rw_mu': rw_mu, 'rw_w0': rw_w0,
            'rw_w2': rw_w2, 'rw_a0': rw_a0, 'rw_a2': rw_a2, 'rw_g2': rw_g2, 'rw_k_k': rw_k_k,
            'rw_k_a': rw_k_a, 'rw_r_k': rw_r_k, 'rw_ln_w': rw_ln_w, 'rw_ln_b': rw_ln_b}


def reference(x, mem, rel_bias, norm_mix, norm_xattn, norm_mem, norm_ffn, xa_wq, xa_wk, xa_wv, xa_wo,
              xa_qn, xa_kn, ffn_w1, ffn_w2, ev_w_in, ev_w_out, dn_conv, dn_a_log, dn_dt_bias, dn_norm,
              sc_conv, od_w_in, od_w_out, ca_qn, ca_kn, rw_mu, rw_w0, rw_w2, rw_a0, rw_a2, rw_g2,
              rw_k_k, rw_k_a, rw_r_k, rw_ln_w, rw_ln_b):
    for layer in range(DEPTH):
        i = layer // 2
        h = rms_norm(x, norm_mix[layer])
        if layer % 2 == 0:
            x = x + even_mixer(h, ev_w_in[i], ev_w_out[i], dn_conv[i], dn_a_log[i], dn_dt_bias[i],
                               dn_norm[i], sc_conv[i])
        else:
            x = x + odd_mixer(h, od_w_in[i], od_w_out[i], ca_qn[i], ca_kn[i], rel_bias, rw_mu[i],
                              rw_w0[i], rw_w2[i], rw_a0[i], rw_a2[i], rw_g2[i], rw_k_k[i], rw_k_a[i],
                              rw_r_k[i], rw_ln_w[i], rw_ln_b[i])
        x = x + memory_cross_attention(rms_norm(x, norm_xattn[layer]), rms_norm(mem, norm_mem[layer]),
                                       xa_wq[layer], xa_wk[layer], xa_wv[layer], xa_wo[layer],
                                       xa_qn[layer], xa_kn[layer])
        x = x + squared_relu_mlp(rms_norm(x, norm_ffn[layer]), ffn_w1[layer], ffn_w2[layer])
    return x
```

```python
import functools
import math

import jax
import jax.numpy as jnp
import numpy as np
from jax import lax
from jax.experimental import pallas as pl
from jax.experimental.pallas import tpu as pltpu

F32 = jnp.float32
BF16 = jnp.bfloat16

D_MODEL = 2048
DEPTH = 4
RMS_EPS = 1e-6
L2_EPS = 1e-6

DN_HEADS = 8
DN_HEAD_DIM = 128
DN_WIDTH = DN_HEADS * DN_HEAD_DIM
SC_WIDTH = D_MODEL - DN_WIDTH
CHUNK = 64

DSA_PATTERNS = ((128, 1), (512, 4), (2048, 16))
DSA_GROUPS = len(DSA_PATTERNS)
DSA_HPG = 4
DSA_HEAD_DIM = 128
DSA_HEADS = DSA_GROUPS * DSA_HPG
DSA_QKV = DSA_HEADS * DSA_HEAD_DIM
DSA_SIDE = 64
DSA_QBLK = 128
REL_BUCKETS = 32
REL_MAX_DIST = 1024
NEG_INF = -1e30

RW_HEADS = 8
RW_HEAD_DIM = 64
RW_WIDTH = RW_HEADS * RW_HEAD_DIM
RW_LORA = 64
RW_GATE_LORA = 128
RW_MAIN = 3 * RW_WIDTH
RW_LORA_IN = 4 * RW_LORA + RW_GATE_LORA
RW_GN_EPS = 64e-5

XA_HEADS = 4
XA_HEAD_DIM = 128
XA_WIDTH = XA_HEADS * XA_HEAD_DIM

LANES = 128
VMEM_LIMIT_BYTES = 56 * 1024 * 1024


def _params(*sem):
    return pltpu.CompilerParams(dimension_semantics=sem, vmem_limit_bytes=VMEM_LIMIT_BYTES)


_NN = (((1,), (0,)), ((), ()))
_NT = (((1,), (1,)), ((), ()))


def _dot1(a, b, dims=_NN):
    return lax.dot_general(a.astype(BF16), b.astype(BF16), dims, preferred_element_type=F32)


def _split2(a):
    hi = a.astype(BF16)
    lo = (a - hi.astype(F32)).astype(BF16)
    return hi, lo


def _dot3(a, b, dims=_NN):
    ah, al = _split2(a)
    bh, bl = _split2(b)
    dg = functools.partial(lax.dot_general, dimension_numbers=dims, preferred_element_type=F32)
    return dg(ah, bh) + (dg(al, bh) + dg(ah, bl))


def _dot_exact_lhs(a01, b):
    a = a01.astype(BF16)
    b1 = b.astype(BF16)
    r1 = b - b1.astype(F32)
    b2 = r1.astype(BF16)
    b3 = (r1 - b2.astype(F32)).astype(BF16)
    dg = functools.partial(lax.dot_general, dimension_numbers=_NN, preferred_element_type=F32)
    return dg(a, b1) + (dg(a, b2) + dg(a, b3))


def _dot_exact_rhs(a, b01):
    b = b01.astype(BF16)
    a1 = a.astype(BF16)
    r1 = a - a1.astype(F32)
    a2 = r1.astype(BF16)
    a3 = (r1 - a2.astype(F32)).astype(BF16)
    dg = functools.partial(lax.dot_general, dimension_numbers=_NN, preferred_element_type=F32)
    return dg(a1, b) + (dg(a2, b) + dg(a3, b))


def _neumann_inverse(a, eye_f, dot):
    x = eye_f - a
    p = dot(a, a)
    x = x + dot(x, p)
    for _ in range(4):
        p = dot(p, p)
        x = x + dot(x, p)
    return x


def _sigmoid(x):
    return 1.0 / (1.0 + jnp.exp(-x))


def _softplus(x):
    return jnp.maximum(x, 0.0) + jnp.log1p(jnp.exp(-jnp.abs(x)))


def _shift_down(x, row):
    return jnp.where(row == 0, 0.0, pltpu.roll(x, 1, 0))


def _shift_up(x, row):
    n = x.shape[0]
    return jnp.where(row == n - 1, 0.0, pltpu.roll(x, n - 1, 0))


def _norm_mm_body(x_ref, g_ref, w_ref, *rest, act, precise):
    if precise:
        wp_ref, o_ref, op_ref, xn_ref = rest
    else:
        o_ref, xn_ref = rest

    @pl.when(pl.program_id(1) == 0)
    def _():
        x = x_ref[...]
        xn = x * lax.rsqrt(jnp.mean(x * x, axis=-1, keepdims=True) + RMS_EPS) * g_ref[...]
        xn_ref[...] = xn.astype(BF16)
        if precise:
            op_ref[...] = _dot3(xn, wp_ref[...])

    acc = jnp.dot(xn_ref[...], w_ref[...], preferred_element_type=F32)
    if act == "relu2":
        acc = jnp.square(jnp.maximum(acc, 0.0))
    o_ref[...] = acc.astype(o_ref.dtype)


def norm_mm(x, g, w, wp=None, act=None, out_dtype=F32, tm=512, tn=1024):
    t, d = x.shape
    n = w.shape[1]
    tm = min(tm, t)
    tn = min(tn, n)
    assert t % tm == 0 and n % tn == 0
    precise = wp is not None
    in_specs = [
        pl.BlockSpec((tm, d), lambda i, j: (i, 0)),
        pl.BlockSpec((1, d), lambda i, j: (0, 0)),
        pl.BlockSpec((d, tn), lambda i, j: (0, j)),
    ]
    out_specs = pl.BlockSpec((tm, tn), lambda i, j: (i, j))
    out_shape = jax.ShapeDtypeStruct((t, n), out_dtype)
    args = [x, g.reshape(1, d), w]
    if precise:
        npc = wp.shape[1]
        in_specs.append(pl.BlockSpec((d, npc), lambda i, j: (0, 0)))
        out_specs = [out_specs, pl.BlockSpec((tm, npc), lambda i, j: (i, 0))]
        out_shape = [out_shape, jax.ShapeDtypeStruct((t, npc), F32)]
        args.append(wp)
    return pl.pallas_call(
        functools.partial(_norm_mm_body, act=act, precise=precise),
        grid=(t // tm, n // tn),
        in_specs=in_specs,
        out_specs=out_specs,
        out_shape=out_shape,
        scratch_shapes=[pltpu.VMEM((tm, d), BF16)],
        compiler_params=_params("parallel", "arbitrary"),
        name="norm_mm",
    )(*args)


def _mm_res_body(a_ref, w_ref, r_ref, o_ref):
    acc = jnp.dot(a_ref[...], w_ref[...], preferred_element_type=F32)

    @pl.when(pl.program_id(2) == 0)
    def _():
        o_ref[...] = r_ref[...] + acc

    @pl.when(pl.program_id(2) > 0)
    def _():
        o_ref[...] += acc


def mm_res(a, w, res, tm=1024, tn=1024, tk=2048):
    t, k = a.shape
    n = w.shape[1]
    tm, tn, tk = min(tm, t), min(tn, n), min(tk, k)
    assert t % tm == 0 and n % tn == 0 and k % tk == 0
    return pl.pallas_call(
        _mm_res_body,
        grid=(t // tm, n // tn, k // tk),
        in_specs=[
            pl.BlockSpec((tm, tk), lambda i, j, kk: (i, kk)),
            pl.BlockSpec((tk, tn), lambda i, j, kk: (kk, j)),
            pl.BlockSpec((tm, tn), lambda i, j, kk: (i, j)),
        ],
        out_specs=pl.BlockSpec((tm, tn), lambda i, j, kk: (i, j)),
        out_shape=jax.ShapeDtypeStruct((t, n), F32),
        compiler_params=_params("parallel", "parallel", "arbitrary"),
        name="mm_res",
    )(a, w, res)


def _xattn_body(x_ref, g_ref, wq_ref, kv_ref, wo_ref, qn_ref, kn_ref, o_ref):
    x = x_ref[...]
    xn = x * lax.rsqrt(jnp.mean(x * x, axis=-1, keepdims=True) + RMS_EPS) * g_ref[...]
    q = jnp.dot(xn.astype(BF16), wq_ref[...], preferred_element_type=F32)
    kv = kv_ref[0]
    outs = []
    for h in range(XA_HEADS):
        sl = slice(h * XA_HEAD_DIM, (h + 1) * XA_HEAD_DIM)
        qh = q[:, sl]
        qh = qh * lax.rsqrt(jnp.mean(qh * qh, axis=-1, keepdims=True) + RMS_EPS) * qn_ref[...]
        kh = kv[:, sl]
        kh = kh * lax.rsqrt(jnp.mean(kh * kh, axis=-1, keepdims=True) + RMS_EPS) * kn_ref[...]
        vh = kv[:, XA_WIDTH + h * XA_HEAD_DIM:XA_WIDTH + (h + 1) * XA_HEAD_DIM]
        logits = _dot1(qh, kh, _NT) * (XA_HEAD_DIM ** -0.5)
        m = jnp.max(logits, axis=-1, keepdims=True)
        p = jnp.exp(logits - m)
        s = jnp.sum(p, axis=-1, keepdims=True)
        outs.append(_dot1(p, vh) / s)
    o = jnp.concatenate(outs, axis=-1).astype(BF16)
    o_ref[...] = x + jnp.dot(o, wo_ref[...], preferred_element_type=F32)


def xattn(x, g, wq, kv, wo, qn, kn, batch, ts=512):
    t, d = x.shape
    s = t // batch
    ts = min(ts, s)
    nst = s // ts
    m = kv.shape[1]
    return pl.pallas_call(
        _xattn_body,
        grid=(batch, nst),
        in_specs=[
            pl.BlockSpec((ts, d), lambda b, i: (b * nst + i, 0)),
            pl.BlockSpec((1, d), lambda b, i: (0, 0)),
            pl.BlockSpec((d, XA_WIDTH), lambda b, i: (0, 0)),
            pl.BlockSpec((1, m, 2 * XA_WIDTH), lambda b, i: (b, 0, 0)),
            pl.BlockSpec((XA_WIDTH, d), lambda b, i: (0, 0)),
            pl.BlockSpec((1, XA_HEAD_DIM), lambda b, i: (0, 0)),
            pl.BlockSpec((1, XA_HEAD_DIM), lambda b, i: (0, 0)),
        ],
        out_specs=pl.BlockSpec((ts, d), lambda b, i: (b * nst + i, 0)),
        out_shape=jax.ShapeDtypeStruct((t, d), F32),
        compiler_params=_params("parallel", "parallel"),
        name="xattn",
    )(x, g.reshape(1, d), wq, kv, wo, qn.reshape(1, -1), kn.reshape(1, -1))


def _tri_masks(n, blk):
    ii = lax.broadcasted_iota(jnp.int32, (n, n), 0)
    jj = lax.broadcasted_iota(jnp.int32, (n, n), 1)
    same = (ii // blk) == (jj // blk) if n != blk else None

    def m(c):
        return c if same is None else (c & same)

    return {
        False: (m(ii >= jj), m(ii > jj)),
        True: (m(ii <= jj), m(ii < jj)),
        "eye": ii == jj,
    }


def _dn_chunk(state, qc, kc, vc, ktc, beta, g, masks, reverse, dot):
    c = CHUNK
    incl, strict = masks[reverse]
    eye = masks["eye"]
    eye_f = jnp.where(eye, 1.0, 0.0)
    incl_f = jnp.where(incl, 1.0, 0.0)
    gb = jnp.broadcast_to(g, (c, c))
    gc_col = _dot_exact_lhs(incl_f, gb)
    gc_row = _dot_exact_lhs(jnp.ones((c, c), F32), jnp.where(eye, gc_col, 0.0))
    decay = jnp.where(incl, jnp.exp(jnp.where(incl, gc_col - gc_row, 0.0)), 0.0)
    gcum = gc_col[:, 0:1]
    egc = jnp.exp(gcum)
    kb = kc * beta
    kq = dot(jnp.concatenate([kb, qc], axis=0), kc, _NT)
    a_mat = jnp.where(strict, kq[:c] * decay, 0.0)
    qk = jnp.where(incl, kq[c:] * decay, 0.0)
    t_inv = _neumann_inverse(a_mat, eye_f, dot)
    uw = dot(t_inv, jnp.concatenate([vc * beta, kb * egc], axis=1))
    u, w = uw[:, :DN_HEAD_DIM], uw[:, DN_HEAD_DIM:]
    ws = dot(jnp.concatenate([w, qc * egc], axis=0), state)
    v_new = u - ws[:c]
    o = ws[c:] + dot(qk, v_new)
    last = 0 if reverse else c - 1
    g_row = gc_row[0:1, :]
    g_last = g_row[:, last:last + 1]
    new_state = state * jnp.exp(g_last) + dot(ktc * jnp.exp(g_last - g_row), v_new)
    return new_state, o


def _dn_body(q_ref, k_ref, v_ref, gate_ref, ab_ref, cq_ref, ck_ref, cv_ref, alog_ref, dt_ref, gain_ref,
             y_ref, qs_ref, ks_ref, vs_ref, kt_ref, bg_ref, of_ref, ob_ref):
    s = q_ref.shape[0]
    h = pl.program_id(1)
    row = lax.broadcasted_iota(jnp.int32, (s, 1), 0)

    def conv_silu(x_ref, cw_ref):
        x = x_ref[...]
        w = cw_ref[...]
        y = _shift_down(x, row) * w[0:1] + x * w[1:2] + _shift_up(x, row) * w[2:3]
        return y * _sigmoid(y)

    q = conv_silu(q_ref, cq_ref)
    q = q * lax.rsqrt(jnp.sum(q * q, axis=-1, keepdims=True) + L2_EPS) * (DN_HEAD_DIM ** -0.5)
    qs_ref[...] = q
    k = conv_silu(k_ref, ck_ref)
    k = k * lax.rsqrt(jnp.sum(k * k, axis=-1, keepdims=True) + L2_EPS)
    ks_ref[...] = k
    kt_ref[...] = k.T
    vs_ref[...] = conv_silu(v_ref, cv_ref)

    ab = ab_ref[...]
    beta_all = _sigmoid(ab)
    g_all = -jnp.exp(alog_ref[...]) * _softplus(ab + dt_ref[...])
    lane = lax.broadcasted_iota(jnp.int32, (1, 4 * DN_HEADS), 1)

    def pick(t, idx):
        return jnp.sum(jnp.where(lane == idx, t, 0.0), axis=-1, keepdims=True)

    cols = (pick(beta_all, h), pick(beta_all, DN_HEADS + h),
            pick(g_all, 2 * DN_HEADS + h), pick(g_all, 3 * DN_HEADS + h))
    lane128 = lax.broadcasted_iota(jnp.int32, (1, LANES), 1)
    bg = jnp.zeros((s, LANES), F32)
    for i, col in enumerate(cols):
        bg = jnp.where(lane128 == i, col, bg)
    bg_ref[...] = bg

    masks = _tri_masks(CHUNK, CHUNK)
    nc = s // CHUNK
    npair = nc // 2

    def run_chunk(state, cidx, ktc, reverse):
        rows = pl.ds(pl.multiple_of(cidx * CHUNK, CHUNK), CHUNK)
        bgc = bg_ref[rows, :]
        off = 1 if reverse else 0
        beta = bgc[:, off:off + 1]
        g = bgc[:, 2 + off:3 + off]
        state, o = _dn_chunk(state, qs_ref[rows, :], ks_ref[rows, :], vs_ref[rows, :], ktc, beta, g,
                             masks, reverse, _dot3)
        if reverse:
            ob_ref[rows, :] = o
        else:
            of_ref[rows, :] = o
        return state

    def pair(i, carry):
        sf, sb = carry
        ktf = kt_ref[:, pl.ds(pl.multiple_of(i * 2 * CHUNK, 2 * CHUNK), 2 * CHUNK)]
        sf = run_chunk(sf, 2 * i, ktf[:, :CHUNK], False)
        sf = run_chunk(sf, 2 * i + 1, ktf[:, CHUNK:], False)
        pb = npair - 1 - i
        ktb = kt_ref[:, pl.ds(pl.multiple_of(pb * 2 * CHUNK, 2 * CHUNK), 2 * CHUNK)]
        sb = run_chunk(sb, 2 * pb + 1, ktb[:, CHUNK:], True)
        sb = run_chunk(sb, 2 * pb, ktb[:, :CHUNK], True)
        return sf, sb

    zero = jnp.zeros((DN_HEAD_DIM, DN_HEAD_DIM), F32)
    lax.fori_loop(0, npair, pair, (zero, zero))

    o = of_ref[...] + ob_ref[...]
    o = o * lax.rsqrt(jnp.mean(o * o, axis=-1, keepdims=True) + RMS_EPS) * gain_ref[...]
    gate = gate_ref[...]
    y_ref[...] = (o * (gate * _sigmoid(gate))).astype(y_ref.dtype)


def deltanet(p, ab, conv_t, alog32, dt32, gain, batch):
    t = p.shape[0]
    s = t // batch
    hd = DN_HEAD_DIM
    col = lambda off: pl.BlockSpec((s, hd), lambda b, h: (b, off + h))
    cw = lambda off: pl.BlockSpec((3, hd), lambda b, h: (0, off + h))
    small = lambda n: pl.BlockSpec((1, n), lambda b, h: (0, 0))
    big = pltpu.VMEM((s, hd), F32)
    return pl.pallas_call(
        _dn_body,
        grid=(batch, DN_HEADS),
        in_specs=[col(0), col(DN_HEADS), col(2 * DN_HEADS), col(3 * DN_HEADS),
                  pl.BlockSpec((s, 4 * DN_HEADS), lambda b, h: (b, 0)),
                  cw(0), cw(DN_HEADS), cw(2 * DN_HEADS),
                  small(4 * DN_HEADS), small(4 * DN_HEADS), small(hd)],
        out_specs=pl.BlockSpec((s, hd), lambda b, h: (b, h)),
        out_shape=jax.ShapeDtypeStruct((t, DN_WIDTH), BF16),
        scratch_shapes=[big, big, big, pltpu.VMEM((hd, s), F32), big, big, big],
        compiler_params=_params("parallel", "parallel"),
        name="deltanet",
    )(p, p, p, p, ab, conv_t, conv_t, conv_t, alog32, dt32, gain.reshape(1, hd))


def _sconv_body(b_ref, c_ref, u_ref, w_ref, y_ref):
    s = b_ref.shape[0]
    row = lax.broadcasted_iota(jnp.int32, (s, 1), 0)
    cu = c_ref[...] * u_ref[...]
    w = w_ref[...]
    y = _shift_down(cu, row) * w[0:1] + cu * w[1:2] + _shift_up(cu, row) * w[2:3]
    y_ref[...] = (b_ref[...] * y).astype(y_ref.dtype)


def short_conv(p, conv_t, col0, batch, tc=256):
    t = p.shape[0]
    s = t // batch
    nct = SC_WIDTH // tc
    base = col0 // tc
    col = lambda off: pl.BlockSpec((s, tc), lambda b, c: (b, base + off * nct + c))
    return pl.pallas_call(
        _sconv_body,
        grid=(batch, nct),
        in_specs=[col(0), col(1), col(2), pl.BlockSpec((3, tc), lambda b, c: (0, c))],
        out_specs=pl.BlockSpec((s, tc), lambda b, c: (b, c)),
        out_shape=jax.ShapeDtypeStruct((t, SC_WIDTH), BF16),
        compiler_params=_params("parallel", "parallel"),
        name="short_conv",
    )(p, p, p, conv_t)


def _dsa_body(*refs, seq):
    q_refs = refs[0:3]
    k_refs = refs[3:6]
    v_refs = refs[6:9]
    qn_ref, kn_ref, bias_ref, y_ref, qs_ref, ks_ref, kbuf_ref, vbuf_ref, og_ref, lse_ref = refs[9:]
    s = seq
    qb = DSA_QBLK
    side = DSA_SIDE
    kj = lax.broadcasted_iota(jnp.int32, (1, qb + 2 * side), 1)

    for gi, (_, dil) in enumerate(DSA_PATTERNS):
        sub = s // dil
        nblk = sub // qb
        q = q_refs[gi][...]
        qs_ref[...] = (q * lax.rsqrt(jnp.mean(q * q, axis=-1, keepdims=True) + RMS_EPS) * qn_ref[...]
                       * (DSA_HEAD_DIM ** -0.5))
        k = k_refs[gi][...]
        ks_ref[...] = k * lax.rsqrt(jnp.mean(k * k, axis=-1, keepdims=True) + RMS_EPS) * kn_ref[...]
        zpad = jnp.zeros((side, DSA_HEAD_DIM), F32)
        kbuf_ref[0:side, :] = zpad
        vbuf_ref[0:side, :] = zpad
        kbuf_ref[side + sub:2 * side + sub, :] = zpad
        vbuf_ref[side + sub:2 * side + sub, :] = zpad
        bias = bias_ref[0, gi]

        def residue(r, _, gi=gi, dil=dil, sub=sub, nblk=nblk, bias=bias):
            kbuf_ref[side:side + sub, :] = ks_ref[pl.ds(r, sub, stride=dil), :]
            vbuf_ref[side:side + sub, :] = v_refs[gi][pl.ds(r, sub, stride=dil), :]

            def block(n, _):
                rows = pl.ds(r + n * (qb * dil), qb, stride=dil)
                qblk = qs_ref[rows, :]
                win = pl.ds(pl.multiple_of(n * qb, qb), qb + 2 * side)
                logits = _dot1(qblk, kbuf_ref[win, :], _NT) + bias
                pos = n * qb - side + kj
                logits = jnp.where((pos >= 0) & (pos < sub), logits, NEG_INF)
                m = jnp.max(logits, axis=-1, keepdims=True)
                p = jnp.exp(logits - m)
                ssum = jnp.sum(p, axis=-1, keepdims=True)
                og_ref[gi, rows, :] = _dot1(p, vbuf_ref[win, :]) / ssum
                lse_ref[gi, rows, :] = jnp.broadcast_to(m + jnp.log(ssum), (qb, DSA_HEAD_DIM))
                return 0

            lax.fori_loop(0, nblk, block, 0)
            return 0

        lax.fori_loop(0, dil, residue, 0)

    lse = [lse_ref[gi] for gi in range(DSA_GROUPS)]
    mx = jnp.maximum(jnp.maximum(lse[0], lse[1]), lse[2])
    ws = [jnp.exp(l - mx) for l in lse]
    num = ws[0] * og_ref[0] + ws[1] * og_ref[1] + ws[2] * og_ref[2]
    y_ref[...] = (num / (ws[0] + ws[1] + ws[2])).astype(y_ref.dtype)


def dilated_attention(p, qn, kn, bias_tab, batch):
    t = p.shape[0]
    s = t // batch
    hd = DSA_HEAD_DIM
    nh = DSA_HEADS

    def col(part, gi):
        return pl.BlockSpec((s, hd), lambda b, j: (b, part * nh + gi * DSA_HPG + j))

    in_specs = [col(part, gi) for part in range(3) for gi in range(DSA_GROUPS)]
    in_specs += [pl.BlockSpec((1, hd), lambda b, j: (0, 0)), pl.BlockSpec((1, hd), lambda b, j: (0, 0)),
                 pl.BlockSpec((1, DSA_GROUPS, DSA_QBLK, DSA_QBLK + 2 * DSA_SIDE), lambda b, j: (j, 0, 0, 0))]
    big = pltpu.VMEM((s, hd), F32)
    pad = pltpu.VMEM((s + 2 * DSA_SIDE, hd), F32)
    grp = pltpu.VMEM((DSA_GROUPS, s, hd), F32)
    return pl.pallas_call(
        functools.partial(_dsa_body, seq=s),
        grid=(batch, DSA_HPG),
        in_specs=in_specs,
        out_specs=pl.BlockSpec((s, hd), lambda b, j: (b, j)),
        out_shape=jax.ShapeDtypeStruct((t, DSA_HPG * hd), BF16),
        scratch_shapes=[big, big, pad, pad, grp, grp],
        compiler_params=_params("parallel", "parallel"),
        name="dilated_attention",
    )(*([p] * 9), qn.reshape(1, hd), kn.reshape(1, hd), bias_tab)


def _t5_bucket(rel):
    half = REL_BUCKETS // 2
    max_exact = half // 2
    n = jnp.abs(rel)
    scaled = jnp.log(jnp.maximum(n, max_exact).astype(F32) / max_exact) / math.log(REL_MAX_DIST / max_exact)
    large = jnp.minimum(max_exact + (scaled * (half - max_exact)).astype(jnp.int32), half - 1)
    return jnp.where(rel > 0, half, 0) + jnp.where(n < max_exact, n, large)


def _dsa_bias_table(rel_bias):
    qi = jnp.arange(DSA_QBLK)[:, None]
    kj = jnp.arange(DSA_QBLK + 2 * DSA_SIDE)[None, :]
    rel = kj - DSA_SIDE - qi
    inside = jnp.abs(rel) <= DSA_SIDE
    tabs = []
    for gi, (_, dil) in enumerate(DSA_PATTERNS):
        bucket = _t5_bucket(jnp.clip(rel, -DSA_SIDE, DSA_SIDE) * dil)
        b = rel_bias[bucket][:, :, gi * DSA_HPG:(gi + 1) * DSA_HPG]
        tabs.append(jnp.where(inside[:, :, None], b, NEG_INF))
    return jnp.transpose(jnp.stack(tabs), (3, 0, 1, 2)).astype(F32)


def _head_block_diag():
    ii = lax.broadcasted_iota(jnp.int32, (LANES, LANES), 0)
    jj = lax.broadcasted_iota(jnp.int32, (LANES, LANES), 1)
    return (ii // RW_HEAD_DIM) == (jj // RW_HEAD_DIM)


def _rw_prep_body(r_ref, k_ref, v_ref, lo_ref, mur_ref, muk_ref, muv_ref, mulo_ref, w0_ref, w2_ref, a0_ref,
                  a2_ref, g2_ref, kk_ref, ka_ref, rk_ref,
                  ro_ref, vo_ref, kko_ref, bon_ref, gate_ref, lwf_ref, lwb_ref, kdf_ref, kdb_ref, bbf_ref, bbb_ref):
    s = r_ref.shape[0]
    row = lax.broadcasted_iota(jnp.int32, (s, 1), 0)

    def mix(t, mu):
        return t + mu * (0.5 * (_shift_down(t, row) + _shift_up(t, row)) - t)

    r = mix(r_ref[...], mur_ref[...])
    kr = mix(k_ref[...], muk_ref[...])
    v = mix(v_ref[...], muv_ref[...])
    lo = mix(lo_ref[...], mulo_ref[...])
    bd = jnp.where(_head_block_diag(), 1.0, 0.0)

    def head_sum(t):
        return _dot_exact_rhs(t, bd)

    kk = kr * kk_ref[...]
    kk = kk * lax.rsqrt(head_sum(kk * kk) + L2_EPS)
    gd = lo[:, 4 * RW_LORA:]
    gate_ref[...] = _dot3(_sigmoid(gd), g2_ref[...])
    ro_ref[...] = r
    vo_ref[...] = v
    kko_ref[...] = kk
    bonus = jnp.zeros_like(r)
    outs = ((lwf_ref, kdf_ref, bbf_ref), (lwb_ref, kdb_ref, bbb_ref))
    for d in range(2):
        wd = lo[:, d * RW_LORA:(d + 1) * RW_LORA]
        ad = lo[:, (2 + d) * RW_LORA:(3 + d) * RW_LORA]
        w_log = -_softplus(-(w0_ref[d:d + 1, :] + _dot3(jnp.tanh(wd), w2_ref[d]))) - 0.5
        a = _sigmoid(a0_ref[d:d + 1, :] + _dot3(ad, a2_ref[d]))
        kd = kr * (1.0 + (a - 1.0) * ka_ref[...])
        lw_ref, kd_ref, bb_ref = outs[d]
        lw_ref[...] = -jnp.exp(w_log)
        kd_ref[...] = kd
        bb_ref[...] = kk * a
        bonus = bonus + head_sum(r * kd * rk_ref[...]) * v
    bon_ref[...] = bonus


def rwkv_prep(p, lora_in, mu, w0, w2, a0, a2, g2, k_k, k_a, r_k, col0, batch):
    t = p.shape[0]
    s = t // batch
    nct = RW_WIDTH // LANES
    base = col0 // LANES
    nlo = RW_LORA_IN
    col = lambda off: pl.BlockSpec((s, LANES), lambda b, c: (b, base + off * nct + c))
    vec = lambda off: pl.BlockSpec((1, LANES), lambda b, c: (0, off * nct + c))
    mu_main = mu[:RW_MAIN].reshape(1, RW_MAIN)
    mu_lo = mu[RW_MAIN:].reshape(1, nlo)
    out_spec = pl.BlockSpec((s, LANES), lambda b, c: (b, c))
    n_out = 11
    return pl.pallas_call(
        _rw_prep_body,
        grid=(batch, nct),
        in_specs=[col(0), col(1), col(2),
                  pl.BlockSpec((s, nlo), lambda b, c: (b, 0)),
                  vec(0), vec(1), vec(2),
                  pl.BlockSpec((1, nlo), lambda b, c: (0, 0)),
                  pl.BlockSpec((2, LANES), lambda b, c: (0, c)),
                  pl.BlockSpec((2, RW_LORA, LANES), lambda b, c: (0, 0, c)),
                  pl.BlockSpec((2, LANES), lambda b, c: (0, c)),
                  pl.BlockSpec((2, RW_LORA, LANES), lambda b, c: (0, 0, c)),
                  pl.BlockSpec((RW_GATE_LORA, LANES), lambda b, c: (0, c)),
                  vec(0), vec(0), vec(0)],
        out_specs=[out_spec] * n_out,
        out_shape=[jax.ShapeDtypeStruct((t, RW_WIDTH), F32)] * n_out,
        compiler_params=_params("parallel", "parallel"),
        name="rwkv_prep",
    )(p, p, p, lora_in, mu_main, mu_main, mu_main, mu_lo, w0, w2, a0, a2, g2,
      k_k.reshape(1, RW_WIDTH), k_a.reshape(1, RW_WIDTH), r_k.reshape(1, RW_WIDTH))


def _rw_chunk(state, r, v, kk, lw, kd, bb, lwt, kdt, bbt, masks, consts, reverse, dot):
    c = CHUNK
    incl2, strict2 = masks["2c"][reverse]
    eye2_f, head0, head_bd = consts
    incl_cf = jnp.where(masks["c"][reverse][0], 1.0, 0.0)
    incl_ctf = jnp.where(masks["c"][not reverse][0], 1.0, 0.0)
    cum = _dot_exact_lhs(incl_cf, lw)
    cum_t = _dot_exact_rhs(lwt, incl_ctf)
    last = 0 if reverse else c - 1
    tot_t = cum_t[:, last:last + 1]
    e_neg = jnp.exp(-cum)
    a_t = -kk * jnp.exp(cum - lw)
    r_t = r * jnp.exp(cum)
    b_t = bb * e_neg
    k_t = kd * e_neg

    def stack(x):
        return jnp.concatenate([jnp.where(head0, x, 0.0), jnp.where(head0, 0.0, x)], axis=0)

    ar = jnp.concatenate([stack(a_t), stack(r_t)], axis=0)
    bk = jnp.concatenate([stack(b_t), stack(k_t)], axis=0)
    g = dot(ar, bk, _NT)
    c2 = 2 * c
    a_ab = jnp.where(strict2, g[:c2, :c2], 0.0)
    a_ak = jnp.where(strict2, g[:c2, c2:], 0.0)
    rbk = jnp.concatenate([jnp.where(incl2, g[c2:, :c2], 0.0), jnp.where(incl2, g[c2:, c2:], 0.0)], axis=1)
    t_inv = _neumann_inverse(-a_ab, eye2_f, dot)
    ah = dot(ar, state)
    vs = stack(v)
    ps = dot(t_inv, ah[:c2] + dot(a_ak, vs))
    os_ = ah[c2:] + dot(rbk, jnp.concatenate([ps, vs], axis=0))
    y = os_[:c] + os_[c:]
    e_out_t = jnp.exp(tot_t - cum_t)
    bkt = jnp.concatenate([bbt * e_out_t, kdt * e_out_t], axis=1)
    pv = jnp.concatenate([ps[:c] + ps[c:], v], axis=0)
    new_state = state * jnp.exp(tot_t) + jnp.where(head_bd, dot(bkt, pv), 0.0)
    return new_state, y


def _rw_scan_body(r_ref, v_ref, kk_ref, bon_ref, gate_ref, lwf_ref, lwb_ref, kdf_ref, kdb_ref, bbf_ref, bbb_ref,
                  lnw_ref, lnb_ref, y_ref, lwt_ref, kdt_ref, bbt_ref, yf_ref, yb_ref):
    s = r_ref.shape[0]
    for d, (lw_ref, kd_ref, bb_ref) in enumerate(((lwf_ref, kdf_ref, bbf_ref), (lwb_ref, kdb_ref, bbb_ref))):
        lwt_ref[d] = lw_ref[...].T
        kdt_ref[d] = kd_ref[...].T
        bbt_ref[d] = bb_ref[...].T

    masks = {"c": _tri_masks(CHUNK, CHUNK), "2c": _tri_masks(2 * CHUNK, CHUNK)}
    eye2_f = jnp.where(masks["2c"]["eye"], 1.0, 0.0)
    head0 = lax.broadcasted_iota(jnp.int32, (1, LANES), 1) < RW_HEAD_DIM
    head_bd = _head_block_diag()
    consts = (eye2_f, head0, head_bd)
    nc = s // CHUNK
    npair = nc // 2

    def run_chunk(state, cidx, half, tp, reverse):
        rows = pl.ds(pl.multiple_of(cidx * CHUNK, CHUNK), CHUNK)
        lw_ref, kd_ref, bb_ref = (lwb_ref, kdb_ref, bbb_ref) if reverse else (lwf_ref, kdf_ref, bbf_ref)
        lwt, kdt, bbt = (x[:, half * CHUNK:(half + 1) * CHUNK] for x in tp)
        state, y = _rw_chunk(state, r_ref[rows, :], v_ref[rows, :], kk_ref[rows, :], lw_ref[rows, :],
                             kd_ref[rows, :], bb_ref[rows, :], lwt, kdt, bbt, masks, consts, reverse, _dot3)
        if reverse:
            yb_ref[rows, :] = y
        else:
            yf_ref[rows, :] = y
        return state

    def load_t(d, pidx):
        cols = pl.ds(pl.multiple_of(pidx * 2 * CHUNK, 2 * CHUNK), 2 * CHUNK)
        return lwt_ref[d, :, cols], kdt_ref[d, :, cols], bbt_ref[d, :, cols]

    def pair(i, carry):
        sf, sb = carry
        tf = load_t(0, i)
        sf = run_chunk(sf, 2 * i, 0, tf, False)
        sf = run_chunk(sf, 2 * i + 1, 1, tf, False)
        pb = npair - 1 - i
        tb = load_t(1, pb)
        sb = run_chunk(sb, 2 * pb + 1, 1, tb, True)
        sb = run_chunk(sb, 2 * pb, 0, tb, True)
        return sf, sb

    zero = jnp.zeros((LANES, LANES), F32)
    lax.fori_loop(0, npair, pair, (zero, zero))

    bd = jnp.where(head_bd, 1.0, 0.0)
    y = yf_ref[...] + yb_ref[...]
    mean = _dot_exact_rhs(y, bd) * (1.0 / RW_HEAD_DIM)
    yc = y - mean
    var = _dot_exact_rhs(yc * yc, bd) * (1.0 / RW_HEAD_DIM)
    yn = yc * lax.rsqrt(var + RW_GN_EPS) * lnw_ref[...] + lnb_ref[...]
    y_ref[...] = ((yn + bon_ref[...]) * gate_ref[...]).astype(y_ref.dtype)


def rwkv_scan(prep, ln_w, ln_b, batch):
    t = prep[0].shape[0]
    s = t // batch
    nct = RW_WIDTH // LANES
    blk = pl.BlockSpec((s, LANES), lambda b, c: (b, c))
    vec = pl.BlockSpec((1, LANES), lambda b, c: (0, c))
    tsp = pltpu.VMEM((2, LANES, s), F32)
    big = pltpu.VMEM((s, LANES), F32)
    return pl.pallas_call(
        _rw_scan_body,
        grid=(batch, nct),
        in_specs=[blk] * 11 + [vec, vec],
        out_specs=blk,
        out_shape=jax.ShapeDtypeStruct((t, RW_WIDTH), BF16),
        scratch_shapes=[tsp, tsp, tsp, big, big],
        compiler_params=_params("parallel", "parallel"),
        name="rwkv_scan",
    )(*prep, ln_w.reshape(1, RW_WIDTH), ln_b.reshape(1, RW_WIDTH))


def even_mixer(x, g, w_in, w_out, conv_qkv, a_log, dt_bias, out_gain, conv_sc, batch):
    c_ab = 4 * DN_WIDTH
    c_sc = c_ab + 4 * DN_HEADS
    w_main = jnp.concatenate([w_in[:, :c_ab], w_in[:, c_sc:]], axis=1).astype(BF16)
    p, ab = norm_mm(x, g, w_main, wp=w_in[:, c_ab:c_sc])
    zeros = jnp.zeros((2 * DN_HEADS,), F32)
    alog32 = jnp.concatenate([zeros, a_log.reshape(-1)]).reshape(1, -1)
    dt32 = jnp.concatenate([zeros, dt_bias.reshape(-1)]).reshape(1, -1)
    y_dn = deltanet(p, ab, conv_qkv.T, alog32, dt32, out_gain, batch)
    y_sc = short_conv(p, conv_sc.T, c_ab, batch)
    y = jnp.concatenate([y_dn, y_sc], axis=1)
    return mm_res(y, w_out.astype(BF16), x)


def odd_mixer(x, g, w_in, w_out, qn, kn, bias_tab, mu, w0, w2, a0, a2, g2, k_k, k_a, r_k, ln_w, ln_b, batch):
    c_rw = 3 * DSA_QKV
    c_lo = c_rw + RW_MAIN
    p, lora_in = norm_mm(x, g, w_in[:, :c_lo].astype(BF16), wp=w_in[:, c_lo:])
    y_c = dilated_attention(p, qn, kn, bias_tab, batch)
    prep = rwkv_prep(p, lora_in, mu, w0, w2, a0, a2, g2, k_k, k_a, r_k, c_rw, batch)
    y_d = rwkv_scan(prep, ln_w, ln_b, batch)
    y = jnp.concatenate([y_c, y_d], axis=1)
    return mm_res(y, w_out.astype(BF16), x)


def kernel(x, mem, rel_bias, norm_mix, norm_xattn, norm_mem, norm_ffn, xa_wq, xa_wk, xa_wv, xa_wo, xa_qn, xa_kn, ffn_w1, ffn_w2, ev_w_in, ev_w_out, dn_conv, dn_a_log, dn_dt_bias, dn_norm, sc_conv, od_w_in, od_w_out, ca_qn, ca_kn, rw_mu, rw_w0, rw_w2, rw_a0, rw_a2, rw_g2, rw_k_k, rw_k_a, rw_r_k, rw_ln_w, rw_ln_b):
    batch, seq, d = x.shape
    n_mem = mem.shape[1]
    xf = x.reshape(batch * seq, d)
    memf = mem.reshape(batch * n_mem, d)
    bias_tab = _dsa_bias_table(rel_bias)
    for layer in range(DEPTH):
        i = layer // 2
        if layer % 2 == 0:
            xf = even_mixer(xf, norm_mix[layer], ev_w_in[i], ev_w_out[i], dn_conv[i], dn_a_log[i],
                            dn_dt_bias[i], dn_norm[i], sc_conv[i], batch)
        else:
            xf = odd_mixer(xf, norm_mix[layer], od_w_in[i], od_w_out[i], ca_qn[i], ca_kn[i], bias_tab, rw_mu[i],
                           rw_w0[i], rw_w2[i], rw_a0[i], rw_a2[i], rw_g2[i], rw_k_k[i], rw_k_a[i], rw_r_k[i],
                           rw_ln_w[i], rw_ln_b[i], batch)
        w_kv = jnp.concatenate([xa_wk[layer], xa_wv[layer]], axis=1).astype(BF16)
        kv = norm_mm(memf, norm_mem[layer], w_kv).reshape(batch, n_mem, 2 * XA_WIDTH)
        xf = xattn(xf, norm_xattn[layer], xa_wq[layer].astype(BF16), kv, xa_wo[layer].astype(BF16),
                   xa_qn[layer], xa_kn[layer], batch)
        h1 = norm_mm(xf, norm_ffn[layer], ffn_w1[layer].astype(BF16), act="relu2", out_dtype=BF16)
        xf = mm_res(h1, ffn_w2[layer].astype(BF16), xf)
    return xf.reshape(batch, seq, d)
```

```python
import functools
import math

import jax
import jax.numpy as jnp
import numpy as np
from jax import lax
from jax.experimental import pallas as pl
from jax.experimental.pallas import tpu as pltpu

F32 = jnp.float32
BF16 = jnp.bfloat16

D_MODEL = 2048
DEPTH = 4
RMS_EPS = 1e-6
L2_EPS = 1e-6

DN_HEADS = 8
DN_HEAD_DIM = 128
DN_WIDTH = DN_HEADS * DN_HEAD_DIM
SC_WIDTH = D_MODEL - DN_WIDTH
CHUNK = 64
PAIR = 2 * CHUNK

DSA_PATTERNS = ((128, 1), (512, 4), (2048, 16))
DSA_GROUPS = len(DSA_PATTERNS)
DSA_HPG = 4
DSA_HEAD_DIM = 128
DSA_HEADS = DSA_GROUPS * DSA_HPG
DSA_QKV = DSA_HEADS * DSA_HEAD_DIM
DSA_SIDE = 64
DSA_QBLK = 128
REL_BUCKETS = 32
REL_MAX_DIST = 1024
NEG_INF = -1e30

RW_HEADS = 8
RW_HEAD_DIM = 64
RW_WIDTH = RW_HEADS * RW_HEAD_DIM
RW_LORA = 64
RW_GATE_LORA = 128
RW_MAIN = 3 * RW_WIDTH
RW_LORA_IN = 4 * RW_LORA + RW_GATE_LORA
RW_GN_EPS = 64e-5

XA_HEADS = 4
XA_HEAD_DIM = 128
XA_WIDTH = XA_HEADS * XA_HEAD_DIM

LANES = 128
VMEM_LIMIT_BYTES = 56 * 1024 * 1024


def _params(*sem):
    return pltpu.CompilerParams(dimension_semantics=sem, vmem_limit_bytes=VMEM_LIMIT_BYTES)


_NN = (((1,), (0,)), ((), ()))
_NT = (((1,), (1,)), ((), ()))


def _dot1(a, b, dims=_NN):
    return lax.dot_general(a.astype(BF16), b.astype(BF16), dims, preferred_element_type=F32)


def _split2(a):
    hi = a.astype(BF16)
    lo = (a - hi.astype(F32)).astype(BF16)
    return hi, lo


def _dot3(a, b, dims=_NN):
    ah, al = _split2(a)
    bh, bl = _split2(b)
    dg = functools.partial(lax.dot_general, dimension_numbers=dims, preferred_element_type=F32)
    return dg(ah, bh) + (dg(al, bh) + dg(ah, bl))


def _dot_exact_lhs(a01, b):
    a = a01.astype(BF16)
    b1 = b.astype(BF16)
    r1 = b - b1.astype(F32)
    b2 = r1.astype(BF16)
    b3 = (r1 - b2.astype(F32)).astype(BF16)
    dg = functools.partial(lax.dot_general, dimension_numbers=_NN, preferred_element_type=F32)
    return dg(a, b1) + (dg(a, b2) + dg(a, b3))


def _dot_exact_rhs(a, b01):
    b = b01.astype(BF16)
    a1 = a.astype(BF16)
    r1 = a - a1.astype(F32)
    a2 = r1.astype(BF16)
    a3 = (r1 - a2.astype(F32)).astype(BF16)
    dg = functools.partial(lax.dot_general, dimension_numbers=_NN, preferred_element_type=F32)
    return dg(a1, b) + (dg(a2, b) + dg(a3, b))


_DN_DOT_KQ = _dot1
_DN_DOT_INV = _dot1
_DN_DOT_UW = _dot1
_DN_DOT_SEQ = _dot1
_RW_DOT_G = _dot1
_RW_DOT_INV = _dot1
_RW_DOT_T = _dot1
_RW_DOT_SEQ = _dot1


def _merge_masks(n, reverse):
    ii = lax.broadcasted_iota(jnp.int32, (n, n), 0)
    jj = lax.broadcasted_iota(jnp.int32, (n, n), 1)
    tri = (ii < jj) if reverse else (ii > jj)
    out = []
    size = 1
    while size < CHUNK:
        out.append(tri & ((ii // (2 * size)) == (jj // (2 * size))) & ((ii // size) != (jj // size)))
        size *= 2
    return out


def _tri_inverse(a, eye_f, level_masks, dot):
    t = eye_f - jnp.where(level_masks[0], a, 0.0)
    for m in level_masks[1:]:
        t = t - dot(t, dot(jnp.where(m, a, 0.0), t))
    return t


def _sigmoid(x):
    return 1.0 / (1.0 + jnp.exp(-x))


def _softplus(x):
    return jnp.maximum(x, 0.0) + jnp.log1p(jnp.exp(-jnp.abs(x)))


def _shift_down(x, row):
    return jnp.where(row == 0, 0.0, pltpu.roll(x, 1, 0))


def _shift_up(x, row):
    n = x.shape[0]
    return jnp.where(row == n - 1, 0.0, pltpu.roll(x, n - 1, 0))


def _norm_mm_body(x_ref, g_ref, w_ref, *rest, act, precise):
    if precise:
        wp_ref, o_ref, op_ref, xn_ref = rest
    else:
        o_ref, xn_ref = rest

    @pl.when(pl.program_id(1) == 0)
    def _():
        x = x_ref[...]
        xn = x * lax.rsqrt(jnp.mean(x * x, axis=-1, keepdims=True) + RMS_EPS) * g_ref[...]
        xn_ref[...] = xn.astype(BF16)
        if precise:
            op_ref[...] = _dot3(xn, wp_ref[...])

    acc = jnp.dot(xn_ref[...], w_ref[...], preferred_element_type=F32)
    if act == "relu2":
        acc = jnp.square(jnp.maximum(acc, 0.0))
    o_ref[...] = acc.astype(o_ref.dtype)


def norm_mm(x, g, w, wp=None, act=None, out_dtype=F32, tm=512, tn=1024):
    t, d = x.shape
    n = w.shape[1]
    tm = min(tm, t)
    tn = min(tn, n)
    assert t % tm == 0 and n % tn == 0
    precise = wp is not None
    in_specs = [
        pl.BlockSpec((tm, d), lambda i, j: (i, 0)),
        pl.BlockSpec((1, d), lambda i, j: (0, 0)),
        pl.BlockSpec((d, tn), lambda i, j: (0, j)),
    ]
    out_specs = pl.BlockSpec((tm, tn), lambda i, j: (i, j))
    out_shape = jax.ShapeDtypeStruct((t, n), out_dtype)
    args = [x, g.reshape(1, d), w]
    if precise:
        npc = wp.shape[1]
        in_specs.append(pl.BlockSpec((d, npc), lambda i, j: (0, 0)))
        out_specs = [out_specs, pl.BlockSpec((tm, npc), lambda i, j: (i, 0))]
        out_shape = [out_shape, jax.ShapeDtypeStruct((t, npc), F32)]
        args.append(wp)
    return pl.pallas_call(
        functools.partial(_norm_mm_body, act=act, precise=precise),
        grid=(t // tm, n // tn),
        in_specs=in_specs,
        out_specs=out_specs,
        out_shape=out_shape,
        scratch_shapes=[pltpu.VMEM((tm, d), BF16)],
        compiler_params=_params("parallel", "arbitrary"),
        name="norm_mm",
    )(*args)


def _mm_res_body(a_ref, w_ref, r_ref, o_ref):
    acc = jnp.dot(a_ref[...], w_ref[...], preferred_element_type=F32)

    @pl.when(pl.program_id(2) == 0)
    def _():
        o_ref[...] = r_ref[...] + acc

    @pl.when(pl.program_id(2) > 0)
    def _():
        o_ref[...] += acc


def mm_res(a, w, res, tm=1024, tn=1024, tk=2048):
    t, k = a.shape
    n = w.shape[1]
    tm, tn, tk = min(tm, t), min(tn, n), min(tk, k)
    assert t % tm == 0 and n % tn == 0 and k % tk == 0
    return pl.pallas_call(
        _mm_res_body,
        grid=(t // tm, n // tn, k // tk),
        in_specs=[
            pl.BlockSpec((tm, tk), lambda i, j, kk: (i, kk)),
            pl.BlockSpec((tk, tn), lambda i, j, kk: (kk, j)),
            pl.BlockSpec((tm, tn), lambda i, j, kk: (i, j)),
        ],
        out_specs=pl.BlockSpec((tm, tn), lambda i, j, kk: (i, j)),
        out_shape=jax.ShapeDtypeStruct((t, n), F32),
        compiler_params=_params("parallel", "parallel", "arbitrary"),
        name="mm_res",
    )(a, w, res)


def _xattn_body(x_ref, g_ref, wq_ref, kv_ref, wo_ref, qn_ref, kn_ref, o_ref):
    x = x_ref[...]
    xn = x * lax.rsqrt(jnp.mean(x * x, axis=-1, keepdims=True) + RMS_EPS) * g_ref[...]
    q = jnp.dot(xn.astype(BF16), wq_ref[...], preferred_element_type=F32)
    kv = kv_ref[0]
    outs = []
    for h in range(XA_HEADS):
        sl = slice(h * XA_HEAD_DIM, (h + 1) * XA_HEAD_DIM)
        qh = q[:, sl]
        qh = qh * lax.rsqrt(jnp.mean(qh * qh, axis=-1, keepdims=True) + RMS_EPS) * qn_ref[...]
        kh = kv[:, sl]
        kh = kh * lax.rsqrt(jnp.mean(kh * kh, axis=-1, keepdims=True) + RMS_EPS) * kn_ref[...]
        vh = kv[:, XA_WIDTH + h * XA_HEAD_DIM:XA_WIDTH + (h + 1) * XA_HEAD_DIM]
        logits = _dot1(qh, kh, _NT) * (XA_HEAD_DIM ** -0.5)
        m = jnp.max(logits, axis=-1, keepdims=True)
        p = jnp.exp(logits - m)
        s = jnp.sum(p, axis=-1, keepdims=True)
        outs.append(_dot1(p, vh) / s)
    o = jnp.concatenate(outs, axis=-1).astype(BF16)
    o_ref[...] = x + jnp.dot(o, wo_ref[...], preferred_element_type=F32)


def xattn(x, g, wq, kv, wo, qn, kn, batch, ts=512):
    t, d = x.shape
    s = t // batch
    ts = min(ts, s)
    nst = s // ts
    m = kv.shape[1]
    return pl.pallas_call(
        _xattn_body,
        grid=(batch, nst),
        in_specs=[
            pl.BlockSpec((ts, d), lambda b, i: (b * nst + i, 0)),
            pl.BlockSpec((1, d), lambda b, i: (0, 0)),
            pl.BlockSpec((d, XA_WIDTH), lambda b, i: (0, 0)),
            pl.BlockSpec((1, m, 2 * XA_WIDTH), lambda b, i: (b, 0, 0)),
            pl.BlockSpec((XA_WIDTH, d), lambda b, i: (0, 0)),
            pl.BlockSpec((1, XA_HEAD_DIM), lambda b, i: (0, 0)),
            pl.BlockSpec((1, XA_HEAD_DIM), lambda b, i: (0, 0)),
        ],
        out_specs=pl.BlockSpec((ts, d), lambda b, i: (b * nst + i, 0)),
        out_shape=jax.ShapeDtypeStruct((t, d), F32),
        compiler_params=_params("parallel", "parallel"),
        name="xattn",
    )(x, g.reshape(1, d), wq, kv, wo, qn.reshape(1, -1), kn.reshape(1, -1))


def _tri_masks(n, blk):
    ii = lax.broadcasted_iota(jnp.int32, (n, n), 0)
    jj = lax.broadcasted_iota(jnp.int32, (n, n), 1)
    same = (ii // blk) == (jj // blk) if n != blk else None

    def m(c):
        return c if same is None else (c & same)

    return {
        False: (m(ii >= jj), m(ii > jj)),
        True: (m(ii <= jj), m(ii < jj)),
        "eye": ii == jj,
    }


def _seg_cumsum(x, pos, reverse):
    n = x.shape[0]
    sh = 1
    while sh < CHUNK:
        if reverse:
            x = x + jnp.where(pos < CHUNK - sh, pltpu.roll(x, n - sh, 0), 0.0)
        else:
            x = x + jnp.where(pos >= sh, pltpu.roll(x, sh, 0), 0.0)
        sh *= 2
    return x


def _dn_body(q_ref, k_ref, v_ref, gate_ref, ab_ref, cq_ref, ck_ref, cv_ref, alog_ref, dt_ref, gain_ref,
             y_ref, qs_ref, ks_ref, vs_ref, kt_ref, bg_ref, gt_ref, of_ref, ob_ref, st0_ref, st1_ref, eg0_ref, eg1_ref):
    s = q_ref.shape[0]
    h = pl.program_id(1)
    row = lax.broadcasted_iota(jnp.int32, (s, 1), 0)
    c = CHUNK

    def conv_silu(x_ref, cw_ref):
        x = x_ref[...]
        w = cw_ref[...]
        y = _shift_down(x, row) * w[0:1] + x * w[1:2] + _shift_up(x, row) * w[2:3]
        return y * _sigmoid(y)

    q = conv_silu(q_ref, cq_ref)
    q = q * lax.rsqrt(jnp.sum(q * q, axis=-1, keepdims=True) + L2_EPS) * (DN_HEAD_DIM ** -0.5)
    qs_ref[...] = q
    k = conv_silu(k_ref, ck_ref)
    k = k * lax.rsqrt(jnp.sum(k * k, axis=-1, keepdims=True) + L2_EPS)
    ks_ref[...] = k
    kt_ref[...] = k.T
    vs_ref[...] = conv_silu(v_ref, cv_ref)

    ab = ab_ref[...]
    beta_all = _sigmoid(ab)
    g_all = -jnp.exp(alog_ref[...]) * _softplus(ab + dt_ref[...])
    lane = lax.broadcasted_iota(jnp.int32, (1, 4 * DN_HEADS), 1)

    def pick(t, idx):
        return jnp.sum(jnp.where(lane == idx, t, 0.0), axis=-1, keepdims=True)

    pos = row % c
    lane128 = lax.broadcasted_iota(jnp.int32, (1, LANES), 1)
    cols = (pick(beta_all, h), pick(beta_all, DN_HEADS + h),
            _seg_cumsum(pick(g_all, 2 * DN_HEADS + h), pos, False),
            _seg_cumsum(pick(g_all, 3 * DN_HEADS + h), pos, True))
    bg = jnp.zeros((s, LANES), F32)
    for i, col in enumerate(cols):
        bg = jnp.where(lane128 == i, col, bg)
    bg_ref[...] = bg
    gt_ref[...] = bg.T[0:8, :]

    masks = _tri_masks(PAIR, c)
    merge = {rev: _merge_masks(PAIR, rev) for rev in (False, True)}
    eye_f = jnp.where(masks["eye"], 1.0, 0.0)
    npair = s // PAIR
    slots = ((st0_ref, eg0_ref), (st1_ref, eg1_ref))

    def prep(pidx, slot, d):
        st_ref, eg_ref = slots[slot]
        reverse = d == 1
        rows = pl.ds(pl.multiple_of(pidx * PAIR, PAIR), PAIR)
        qp, kp, vp = qs_ref[rows, :], ks_ref[rows, :], vs_ref[rows, :]
        bgp = bg_ref[rows, :]
        gt = gt_ref[:, rows]
        beta = bgp[:, d:d + 1]
        gcol = bgp[:, 2 + d:3 + d]
        grow = gt[2 + d:3 + d, :]
        incl, strict = masks[reverse]
        decay = jnp.where(incl, jnp.exp(jnp.where(incl, gcol - grow, 0.0)), 0.0)
        kb = kp * beta
        kq = _DN_DOT_KQ(jnp.concatenate([kb, qp], axis=0), kp, _NT)
        a_mat = jnp.where(strict, kq[:PAIR] * decay, 0.0)
        t_inv = _tri_inverse(a_mat, eye_f, merge[reverse], _DN_DOT_INV)
        egc = jnp.exp(gcol)
        uw = _DN_DOT_UW(t_inv, jnp.concatenate([vp * beta, kb * egc], axis=1))
        st_ref[d, 0] = uw[:, :DN_HEAD_DIM]
        st_ref[d, 1] = uw[:, DN_HEAD_DIM:]
        st_ref[d, 2] = qp * egc
        st_ref[d, 3] = jnp.where(incl, kq[PAIR:] * decay, 0.0)
        last0, last1 = (0, c) if reverse else (c - 1, PAIR - 1)
        glast = jnp.where(lane128 < c, grow[:, last0:last0 + 1], grow[:, last1:last1 + 1])
        st_ref[d, 4] = kt_ref[:, rows] * jnp.exp(glast - grow)
        eg_ref[d] = jnp.broadcast_to(jnp.exp(glast), (8, PAIR))

    def seq(state, pidx, slot, d, half):
        st_ref, eg_ref = slots[slot]
        r0 = half * c
        w = st_ref[d, 1, r0:r0 + c, :]
        qg = st_ref[d, 2, r0:r0 + c, :]
        ws = _DN_DOT_SEQ(jnp.concatenate([w, qg], axis=0), state)
        v_new = st_ref[d, 0, r0:r0 + c, :] - ws[:c]
        o = ws[c:] + _DN_DOT_SEQ(st_ref[d, 3, r0:r0 + c, r0:r0 + c], v_new)
        rows = pl.ds(pl.multiple_of(pidx * PAIR + r0, c), c)
        if d:
            ob_ref[rows, :] = o
        else:
            of_ref[rows, :] = o
        return state * eg_ref[d, 0:1, r0:r0 + 1] + _DN_DOT_SEQ(st_ref[d, 4, :, r0:r0 + c], v_new)

    def pair_step(p, slot, carry):
        sf, sb = carry
        nxt = jnp.minimum(p + 1, npair - 1)
        prep(nxt, 1 - slot, 0)
        prep(npair - 1 - nxt, 1 - slot, 1)
        pb = npair - 1 - p
        sf = seq(sf, p, slot, 0, 0)
        sb = seq(sb, pb, slot, 1, 1)
        sf = seq(sf, p, slot, 0, 1)
        sb = seq(sb, pb, slot, 1, 0)
        return sf, sb

    prep(0, 0, 0)
    prep(npair - 1, 0, 1)

    def body(i2, carry):
        return pair_step(2 * i2 + 1, 1, pair_step(2 * i2, 0, carry))

    zero = jnp.zeros((DN_HEAD_DIM, DN_HEAD_DIM), F32)
    lax.fori_loop(0, npair // 2, body, (zero, zero))

    o = of_ref[...] + ob_ref[...]
    o = o * lax.rsqrt(jnp.mean(o * o, axis=-1, keepdims=True) + RMS_EPS) * gain_ref[...]
    gate = gate_ref[...]
    y_ref[...] = (o * (gate * _sigmoid(gate))).astype(y_ref.dtype)


def deltanet(p, ab, conv_t, alog32, dt32, gain, batch):
    t = p.shape[0]
    s = t // batch
    assert s % (2 * PAIR) == 0
    hd = DN_HEAD_DIM
    col = lambda off: pl.BlockSpec((s, hd), lambda b, h: (b, off + h))
    cw = lambda off: pl.BlockSpec((3, hd), lambda b, h: (0, off + h))
    small = lambda n: pl.BlockSpec((1, n), lambda b, h: (0, 0))
    big = pltpu.VMEM((s, hd), F32)
    slot = pltpu.VMEM((2, 5, PAIR, hd), F32)
    egl = pltpu.VMEM((2, 8, PAIR), F32)
    return pl.pallas_call(
        _dn_body,
        grid=(batch, DN_HEADS),
        in_specs=[col(0), col(DN_HEADS), col(2 * DN_HEADS), col(3 * DN_HEADS),
                  pl.BlockSpec((s, 4 * DN_HEADS), lambda b, h: (b, 0)),
                  cw(0), cw(DN_HEADS), cw(2 * DN_HEADS),
                  small(4 * DN_HEADS), small(4 * DN_HEADS), small(hd)],
        out_specs=pl.BlockSpec((s, hd), lambda b, h: (b, h)),
        out_shape=jax.ShapeDtypeStruct((t, DN_WIDTH), BF16),
        scratch_shapes=[big, big, big, pltpu.VMEM((hd, s), F32), big, pltpu.VMEM((8, s), F32), big, big,
                        slot, slot, egl, egl],
        compiler_params=_params("parallel", "parallel"),
        name="deltanet",
    )(p, p, p, p, ab, conv_t, conv_t, conv_t, alog32, dt32, gain.reshape(1, hd))


def _sconv_body(b_ref, c_ref, u_ref, w_ref, y_ref):
    s = b_ref.shape[0]
    row = lax.broadcasted_iota(jnp.int32, (s, 1), 0)
    cu = c_ref[...] * u_ref[...]
    w = w_ref[...]
    y = _shift_down(cu, row) * w[0:1] + cu * w[1:2] + _shift_up(cu, row) * w[2:3]
    y_ref[...] = (b_ref[...] * y).astype(y_ref.dtype)


def short_conv(p, conv_t, col0, batch, tc=256):
    t = p.shape[0]
    s = t // batch
    nct = SC_WIDTH // tc
    base = col0 // tc
    col = lambda off: pl.BlockSpec((s, tc), lambda b, c: (b, base + off * nct + c))
    return pl.pallas_call(
        _sconv_body,
        grid=(batch, nct),
        in_specs=[col(0), col(1), col(2), pl.BlockSpec((3, tc), lambda b, c: (0, c))],
        out_specs=pl.BlockSpec((s, tc), lambda b, c: (b, c)),
        out_shape=jax.ShapeDtypeStruct((t, SC_WIDTH), BF16),
        compiler_params=_params("parallel", "parallel"),
        name="short_conv",
    )(p, p, p, conv_t)


def _dsa_body(*refs, seq):
    q_refs = refs[0:3]
    k_refs = refs[3:6]
    v_refs = refs[6:9]
    qn_ref, kn_ref, bias_ref, y_ref, qs_ref, ks_ref, kbuf_ref, vbuf_ref, og_ref, lse_ref = refs[9:]
    s = seq
    qb = DSA_QBLK
    side = DSA_SIDE
    kj = lax.broadcasted_iota(jnp.int32, (1, qb + 2 * side), 1)

    for gi, (_, dil) in enumerate(DSA_PATTERNS):
        sub = s // dil
        nblk = sub // qb
        q = q_refs[gi][...]
        qs_ref[...] = (q * lax.rsqrt(jnp.mean(q * q, axis=-1, keepdims=True) + RMS_EPS) * qn_ref[...]
                       * (DSA_HEAD_DIM ** -0.5))
        k = k_refs[gi][...]
        ks_ref[...] = k * lax.rsqrt(jnp.mean(k * k, axis=-1, keepdims=True) + RMS_EPS) * kn_ref[...]
        zpad = jnp.zeros((side, DSA_HEAD_DIM), F32)
        kbuf_ref[0:side, :] = zpad
        vbuf_ref[0:side, :] = zpad
        kbuf_ref[side + sub:2 * side + sub, :] = zpad
        vbuf_ref[side + sub:2 * side + sub, :] = zpad
        bias = bias_ref[0, gi]

        def residue(r, _, gi=gi, dil=dil, sub=sub, nblk=nblk, bias=bias):
            kbuf_ref[side:side + sub, :] = ks_ref[pl.ds(r, sub, stride=dil), :]
            vbuf_ref[side:side + sub, :] = v_refs[gi][pl.ds(r, sub, stride=dil), :]

            def block(n, _):
                rows = pl.ds(r + n * (qb * dil), qb, stride=dil)
                qblk = qs_ref[rows, :]
                win = pl.ds(pl.multiple_of(n * qb, qb), qb + 2 * side)
                logits = _dot1(qblk, kbuf_ref[win, :], _NT) + bias
                pos = n * qb - side + kj
                logits = jnp.where((pos >= 0) & (pos < sub), logits, NEG_INF)
                m = jnp.max(logits, axis=-1, keepdims=True)
                p = jnp.exp(logits - m)
                ssum = jnp.sum(p, axis=-1, keepdims=True)
                og_ref[gi, rows, :] = _dot1(p, vbuf_ref[win, :]) / ssum
                lse_ref[gi, rows, :] = jnp.broadcast_to(m + jnp.log(ssum), (qb, DSA_HEAD_DIM))
                return 0

            lax.fori_loop(0, nblk, block, 0)
            return 0

        lax.fori_loop(0, dil, residue, 0)

    lse = [lse_ref[gi] for gi in range(DSA_GROUPS)]
    mx = jnp.maximum(jnp.maximum(lse[0], lse[1]), lse[2])
    ws = [jnp.exp(l - mx) for l in lse]
    num = ws[0] * og_ref[0] + ws[1] * og_ref[1] + ws[2] * og_ref[2]
    y_ref[...] = (num / (ws[0] + ws[1] + ws[2])).astype(y_ref.dtype)


def dilated_attention(p, qn, kn, bias_tab, batch):
    t = p.shape[0]
    s = t // batch
    hd = DSA_HEAD_DIM
    nh = DSA_HEADS

    def col(part, gi):
        return pl.BlockSpec((s, hd), lambda b, j: (b, part * nh + gi * DSA_HPG + j))

    in_specs = [col(part, gi) for part in range(3) for gi in range(DSA_GROUPS)]
    in_specs += [pl.BlockSpec((1, hd), lambda b, j: (0, 0)), pl.BlockSpec((1, hd), lambda b, j: (0, 0)),
                 pl.BlockSpec((1, DSA_GROUPS, DSA_QBLK, DSA_QBLK + 2 * DSA_SIDE), lambda b, j: (j, 0, 0, 0))]
    big = pltpu.VMEM((s, hd), F32)
    pad = pltpu.VMEM((s + 2 * DSA_SIDE, hd), F32)
    grp = pltpu.VMEM((DSA_GROUPS, s, hd), F32)
    return pl.pallas_call(
        functools.partial(_dsa_body, seq=s),
        grid=(batch, DSA_HPG),
        in_specs=in_specs,
        out_specs=pl.BlockSpec((s, hd), lambda b, j: (b, j)),
        out_shape=jax.ShapeDtypeStruct((t, DSA_HPG * hd), BF16),
        scratch_shapes=[big, big, pad, pad, grp, grp],
        compiler_params=_params("parallel", "parallel"),
        name="dilated_attention",
    )(*([p] * 9), qn.reshape(1, hd), kn.reshape(1, hd), bias_tab)


def _t5_bucket(rel):
    half = REL_BUCKETS // 2
    max_exact = half // 2
    n = jnp.abs(rel)
    scaled = jnp.log(jnp.maximum(n, max_exact).astype(F32) / max_exact) / math.log(REL_MAX_DIST / max_exact)
    large = jnp.minimum(max_exact + (scaled * (half - max_exact)).astype(jnp.int32), half - 1)
    return jnp.where(rel > 0, half, 0) + jnp.where(n < max_exact, n, large)


def _dsa_bias_table(rel_bias):
    qi = jnp.arange(DSA_QBLK)[:, None]
    kj = jnp.arange(DSA_QBLK + 2 * DSA_SIDE)[None, :]
    rel = kj - DSA_SIDE - qi
    inside = jnp.abs(rel) <= DSA_SIDE
    tabs = []
    for gi, (_, dil) in enumerate(DSA_PATTERNS):
        bucket = _t5_bucket(jnp.clip(rel, -DSA_SIDE, DSA_SIDE) * dil)
        b = rel_bias[bucket][:, :, gi * DSA_HPG:(gi + 1) * DSA_HPG]
        tabs.append(jnp.where(inside[:, :, None], b, NEG_INF))
    return jnp.transpose(jnp.stack(tabs), (3, 0, 1, 2)).astype(F32)


def _head_block_diag():
    ii = lax.broadcasted_iota(jnp.int32, (LANES, LANES), 0)
    jj = lax.broadcasted_iota(jnp.int32, (LANES, LANES), 1)
    return (ii // RW_HEAD_DIM) == (jj // RW_HEAD_DIM)


def _rw_prep_body(r_ref, k_ref, v_ref, lo_ref, mur_ref, muk_ref, muv_ref, mulo_ref, w0_ref, w2_ref, a0_ref,
                  a2_ref, g2_ref, kk_ref, ka_ref, rk_ref,
                  ro_ref, vo_ref, kko_ref, bon_ref, gate_ref, lwf_ref, lwb_ref, kdf_ref, kdb_ref, bbf_ref, bbb_ref):
    s = r_ref.shape[0]
    row = lax.broadcasted_iota(jnp.int32, (s, 1), 0)

    def mix(t, mu):
        return t + mu * (0.5 * (_shift_down(t, row) + _shift_up(t, row)) - t)

    r = mix(r_ref[...], mur_ref[...])
    kr = mix(k_ref[...], muk_ref[...])
    v = mix(v_ref[...], muv_ref[...])
    lo = mix(lo_ref[...], mulo_ref[...])
    bd = jnp.where(_head_block_diag(), 1.0, 0.0)

    def head_sum(t):
        return _dot_exact_rhs(t, bd)

    kk = kr * kk_ref[...]
    kk = kk * lax.rsqrt(head_sum(kk * kk) + L2_EPS)
    gd = lo[:, 4 * RW_LORA:]
    gate_ref[...] = _dot3(_sigmoid(gd), g2_ref[...])
    ro_ref[...] = r
    vo_ref[...] = v
    kko_ref[...] = kk
    bonus = jnp.zeros_like(r)
    outs = ((lwf_ref, kdf_ref, bbf_ref), (lwb_ref, kdb_ref, bbb_ref))
    for d in range(2):
        wd = lo[:, d * RW_LORA:(d + 1) * RW_LORA]
        ad = lo[:, (2 + d) * RW_LORA:(3 + d) * RW_LORA]
        w_log = -_softplus(-(w0_ref[d:d + 1, :] + _dot3(jnp.tanh(wd), w2_ref[d]))) - 0.5
        a = _sigmoid(a0_ref[d:d + 1, :] + _dot3(ad, a2_ref[d]))
        kd = kr * (1.0 + (a - 1.0) * ka_ref[...])
        lw_ref, kd_ref, bb_ref = outs[d]
        lw_ref[...] = -jnp.exp(w_log)
        kd_ref[...] = kd
        bb_ref[...] = kk * a
        bonus = bonus + head_sum(r * kd * rk_ref[...]) * v
    bon_ref[...] = bonus


def rwkv_prep(p, lora_in, mu, w0, w2, a0, a2, g2, k_k, k_a, r_k, col0, batch):
    t = p.shape[0]
    s = t // batch
    nct = RW_WIDTH // LANES
    base = col0 // LANES
    nlo = RW_LORA_IN
    col = lambda off: pl.BlockSpec((s, LANES), lambda b, c: (b, base + off * nct + c))
    vec = lambda off: pl.BlockSpec((1, LANES), lambda b, c: (0, off * nct + c))
    mu_main = mu[:RW_MAIN].reshape(1, RW_MAIN)
    mu_lo = mu[RW_MAIN:].reshape(1, nlo)
    out_spec = pl.BlockSpec((s, LANES), lambda b, c: (b, c))
    n_out = 11
    return pl.pallas_call(
        _rw_prep_body,
        grid=(batch, nct),
        in_specs=[col(0), col(1), col(2),
                  pl.BlockSpec((s, nlo), lambda b, c: (b, 0)),
                  vec(0), vec(1), vec(2),
                  pl.BlockSpec((1, nlo), lambda b, c: (0, 0)),
                  pl.BlockSpec((2, LANES), lambda b, c: (0, c)),
                  pl.BlockSpec((2, RW_LORA, LANES), lambda b, c: (0, 0, c)),
                  pl.BlockSpec((2, LANES), lambda b, c: (0, c)),
                  pl.BlockSpec((2, RW_LORA, LANES), lambda b, c: (0, 0, c)),
                  pl.BlockSpec((RW_GATE_LORA, LANES), lambda b, c: (0, c)),
                  vec(0), vec(0), vec(0)],
        out_specs=[out_spec] * n_out,
        out_shape=[jax.ShapeDtypeStruct((t, RW_WIDTH), F32)] * n_out,
        compiler_params=_params("parallel", "parallel"),
        name="rwkv_prep",
    )(p, p, p, lora_in, mu_main, mu_main, mu_main, mu_lo, w0, w2, a0, a2, g2,
      k_k.reshape(1, RW_WIDTH), k_a.reshape(1, RW_WIDTH), r_k.reshape(1, RW_WIDTH))


def _rw_scan_body(r_ref, v_ref, kk_ref, bon_ref, gate_ref, lwf_ref, lwb_ref, kdf_ref, kdb_ref, bbf_ref, bbb_ref,
                  lnw_ref, lnb_ref, y_ref, cum_ref, cumt_ref, kdt_ref, bbt_ref, yf_ref, yb_ref, *slot_refs):
    s = r_ref.shape[0]
    c = CHUNK
    c2 = 2 * c
    row = lax.broadcasted_iota(jnp.int32, (s, 1), 0)
    pos = row % c
    lw_refs, kd_refs, bb_refs = (lwf_ref, lwb_ref), (kdf_ref, kdb_ref), (bbf_ref, bbb_ref)
    for d in range(2):
        cum = _seg_cumsum(lw_refs[d][...], pos, d == 1)
        cum_ref[d] = cum
        cumt_ref[d] = cum.T
        kdt_ref[d] = kd_refs[d][...].T
        bbt_ref[d] = bb_refs[d][...].T

    masks = _tri_masks(c2, c)
    merge = {rev: _merge_masks(c2, rev) for rev in (False, True)}
    eye2_f = jnp.where(masks["eye"], 1.0, 0.0)
    head0 = lax.broadcasted_iota(jnp.int32, (1, LANES), 1) < RW_HEAD_DIM
    head_bd = _head_block_diag()
    npair = s // PAIR
    slots = (slot_refs[0:5], slot_refs[5:10])

    def stack(x):
        return jnp.concatenate([jnp.where(head0, x, 0.0), jnp.where(head0, 0.0, x)], axis=0)

    def prep(pidx, slot, d):
        tr_ref, tc_ref, rbk_ref, bkt_ref, et_ref = slots[slot]
        reverse = d == 1
        incl2, strict2 = masks[reverse]
        cols = pl.ds(pl.multiple_of(pidx * PAIR, PAIR), PAIR)
        cum_t2, kdt2, bbt2 = cumt_ref[d, :, cols], kdt_ref[d, :, cols], bbt_ref[d, :, cols]
        for half in range(2):
            rows = pl.ds(pl.multiple_of(pidx * PAIR + half * c, c), c)
            hs = slice(half * c, (half + 1) * c)
            r, v, kk = r_ref[rows, :], v_ref[rows, :], kk_ref[rows, :]
            lw, kd, bb = lw_refs[d][rows, :], kd_refs[d][rows, :], bb_refs[d][rows, :]
            cum = cum_ref[d, rows, :]
            cum_t = cum_t2[:, hs]
            e_neg = jnp.exp(-cum)
            a_s = stack(-kk * jnp.exp(cum - lw))
            r_s = stack(r * jnp.exp(cum))
            ar = jnp.concatenate([a_s, r_s], axis=0)
            bk = jnp.concatenate([stack(bb * e_neg), stack(kd * e_neg)], axis=0)
            g = _RW_DOT_G(ar, bk, _NT)
            a_ab = jnp.where(strict2, g[:c2, :c2], 0.0)
            a_ak = jnp.where(strict2, g[:c2, c2:], 0.0)
            t_inv = _tri_inverse(-a_ab, eye2_f, merge[reverse], _RW_DOT_INV)
            rhs = jnp.concatenate([a_s, _RW_DOT_T(a_ak, stack(v))], axis=1)
            tt = _RW_DOT_T(t_inv, rhs)
            tr_ref[d, half] = jnp.concatenate([tt[:, :LANES], r_s], axis=0)
            tc_ref[d, half] = tt[:, LANES:]
            rbk_ref[d, half] = jnp.concatenate([jnp.where(incl2, g[c2:, :c2], 0.0),
                                                jnp.where(incl2, g[c2:, c2:], 0.0)], axis=1)
            last = 0 if reverse else c - 1
            tot_t = cum_t[:, last:last + 1]
            e_out_t = jnp.exp(tot_t - cum_t)
            bkt_ref[d, half] = jnp.concatenate([bbt2[:, hs] * e_out_t, kdt2[:, hs] * e_out_t], axis=1)
            et_ref[d, half] = jnp.broadcast_to(jnp.exp(tot_t), (LANES, LANES))

    def seq(state, pidx, slot, d, half):
        tr_ref, tc_ref, rbk_ref, bkt_ref, et_ref = slots[slot]
        rows = pl.ds(pl.multiple_of(pidx * PAIR + half * c, c), c)
        v = v_ref[rows, :]
        x = _RW_DOT_SEQ(tr_ref[d, half], state)
        ps = x[:c2] + tc_ref[d, half]
        os_ = x[c2:] + _RW_DOT_SEQ(rbk_ref[d, half], jnp.concatenate([ps, stack(v)], axis=0))
        y = os_[:c] + os_[c:]
        if d:
            yb_ref[rows, :] = y
        else:
            yf_ref[rows, :] = y
        pv = jnp.concatenate([ps[:c] + ps[c:], v], axis=0)
        return state * et_ref[d, half] + jnp.where(head_bd, _RW_DOT_SEQ(bkt_ref[d, half], pv), 0.0)

    def pair_step(p, slot, carry):
        sf, sb = carry
        nxt = jnp.minimum(p + 1, npair - 1)
        prep(nxt, 1 - slot, 0)
        prep(npair - 1 - nxt, 1 - slot, 1)
        pb = npair - 1 - p
        sf = seq(sf, p, slot, 0, 0)
        sb = seq(sb, pb, slot, 1, 1)
        sf = seq(sf, p, slot, 0, 1)
        sb = seq(sb, pb, slot, 1, 0)
        return sf, sb

    prep(0, 0, 0)
    prep(npair - 1, 0, 1)

    def body(i2, carry):
        return pair_step(2 * i2 + 1, 1, pair_step(2 * i2, 0, carry))

    zero = jnp.zeros((LANES, LANES), F32)
    lax.fori_loop(0, npair // 2, body, (zero, zero))

    bd = jnp.where(head_bd, 1.0, 0.0)
    y = yf_ref[...] + yb_ref[...]
    mean = _dot_exact_rhs(y, bd) * (1.0 / RW_HEAD_DIM)
    yc = y - mean
    var = _dot_exact_rhs(yc * yc, bd) * (1.0 / RW_HEAD_DIM)
    yn = yc * lax.rsqrt(var + RW_GN_EPS) * lnw_ref[...] + lnb_ref[...]
    y_ref[...] = ((yn + bon_ref[...]) * gate_ref[...]).astype(y_ref.dtype)


def rwkv_scan(prep, ln_w, ln_b, batch):
    t = prep[0].shape[0]
    s = t // batch
    assert s % (2 * PAIR) == 0
    nct = RW_WIDTH // LANES
    blk = pl.BlockSpec((s, LANES), lambda b, c: (b, c))
    vec = pl.BlockSpec((1, LANES), lambda b, c: (0, c))
    tsp = pltpu.VMEM((2, LANES, s), F32)
    big = pltpu.VMEM((s, LANES), F32)
    sq = pltpu.VMEM((2, 2, PAIR, LANES), F32)
    slot = [pltpu.VMEM((2, 2, 2 * PAIR, LANES), F32), sq, pltpu.VMEM((2, 2, PAIR, 2 * LANES), F32), sq, sq]
    return pl.pallas_call(
        _rw_scan_body,
        grid=(batch, nct),
        in_specs=[blk] * 11 + [vec, vec],
        out_specs=blk,
        out_shape=jax.ShapeDtypeStruct((t, RW_WIDTH), BF16),
        scratch_shapes=[pltpu.VMEM((2, s, LANES), F32), tsp, tsp, tsp, big, big] + slot + slot,
        compiler_params=_params("parallel", "parallel"),
        name="rwkv_scan",
    )(*prep, ln_w.reshape(1, RW_WIDTH), ln_b.reshape(1, RW_WIDTH))


def even_mixer(x, g, w_in, w_out, conv_qkv, a_log, dt_bias, out_gain, conv_sc, batch):
    c_ab = 4 * DN_WIDTH
    c_sc = c_ab + 4 * DN_HEADS
    w_main = jnp.concatenate([w_in[:, :c_ab], w_in[:, c_sc:]], axis=1).astype(BF16)
    p, ab = norm_mm(x, g, w_main, wp=w_in[:, c_ab:c_sc])
    zeros = jnp.zeros((2 * DN_HEADS,), F32)
    alog32 = jnp.concatenate([zeros, a_log.reshape(-1)]).reshape(1, -1)
    dt32 = jnp.concatenate([zeros, dt_bias.reshape(-1)]).reshape(1, -1)
    y_dn = deltanet(p, ab, conv_qkv.T, alog32, dt32, out_gain, batch)
    y_sc = short_conv(p, conv_sc.T, c_ab, batch)
    y = jnp.concatenate([y_dn, y_sc], axis=1)
    return mm_res(y, w_out.astype(BF16), x)


def odd_mixer(x, g, w_in, w_out, qn, kn, bias_tab, mu, w0, w2, a0, a2, g2, k_k, k_a, r_k, ln_w, ln_b, batch):
    c_rw = 3 * DSA_QKV
    c_lo = c_rw + RW_MAIN
    p, lora_in = norm_mm(x, g, w_in[:, :c_lo].astype(BF16), wp=w_in[:, c_lo:])
    y_c = dilated_attention(p, qn, kn, bias_tab, batch)
    prep = rwkv_prep(p, lora_in, mu, w0, w2, a0, a2, g2, k_k, k_a, r_k, c_rw, batch)
    y_d = rwkv_scan(prep, ln_w, ln_b, batch)
    y = jnp.concatenate([y_c, y_d], axis=1)
    return mm_res(y, w_out.astype(BF16), x)


def kernel(x, mem, rel_bias, norm_mix, norm_xattn, norm_mem, norm_ffn, xa_wq, xa_wk, xa_wv, xa_wo, xa_qn, xa_kn, ffn_w1, ffn_w2, ev_w_in, ev_w_out, dn_conv, dn_a_log, dn_dt_bias, dn_norm, sc_conv, od_w_in, od_w_out, ca_qn, ca_kn, rw_mu, rw_w0, rw_w2, rw_a0, rw_a2, rw_g2, rw_k_k, rw_k_a, rw_r_k, rw_ln_w, rw_ln_b):
    batch, seq, d = x.shape
    n_mem = mem.shape[1]
    xf = x.reshape(batch * seq, d)
    memf = mem.reshape(batch * n_mem, d)
    bias_tab = _dsa_bias_table(rel_bias)
    for layer in range(DEPTH):
        i = layer // 2
        if layer % 2 == 0:
            xf = even_mixer(xf, norm_mix[layer], ev_w_in[i], ev_w_out[i], dn_conv[i], dn_a_log[i],
                            dn_dt_bias[i], dn_norm[i], sc_conv[i], batch)
        else:
            xf = odd_mixer(xf, norm_mix[layer], od_w_in[i], od_w_out[i], ca_qn[i], ca_kn[i], bias_tab, rw_mu[i],
                           rw_w0[i], rw_w2[i], rw_a0[i], rw_a2[i], rw_g2[i], rw_k_k[i], rw_k_a[i], rw_r_k[i],
                           rw_ln_w[i], rw_ln_b[i], batch)
        w_kv = jnp.concatenate([xa_wk[layer], xa_wv[layer]], axis=1).astype(BF16)
        kv = norm_mm(memf, norm_mem[layer], w_kv).reshape(batch, n_mem, 2 * XA_WIDTH)
        xf = xattn(xf, norm_xattn[layer], xa_wq[layer].astype(BF16), kv, xa_wo[layer].astype(BF16),
                   xa_qn[layer], xa_kn[layer], batch)
        h1 = norm_mm(xf, norm_ffn[layer], ffn_w1[layer].astype(BF16), act="relu2", out_dtype=BF16)
        xf = mm_res(h1, ffn_w2[layer].astype(BF16), xf)
    return xf.reshape(batch, seq, d)
```

```python
import functools
import math

import jax
import jax.numpy as jnp
import numpy as np
from jax import lax
from jax.experimental import pallas as pl
from jax.experimental.pallas import tpu as pltpu

F32 = jnp.float32
BF16 = jnp.bfloat16

D_MODEL = 2048
DEPTH = 4
RMS_EPS = 1e-6
L2_EPS = 1e-6

DN_HEADS = 8
DN_HEAD_DIM = 128
DN_WIDTH = DN_HEADS * DN_HEAD_DIM
SC_WIDTH = D_MODEL - DN_WIDTH
CHUNK = 64
PAIR = 2 * CHUNK
DN_PREP_PAIRS = 4
RW_PREP_PAIRS = 2

DSA_PATTERNS = ((128, 1), (512, 4), (2048, 16))
DSA_GROUPS = len(DSA_PATTERNS)
DSA_HPG = 4
DSA_HEAD_DIM = 128
DSA_HEADS = DSA_GROUPS * DSA_HPG
DSA_QKV = DSA_HEADS * DSA_HEAD_DIM
DSA_SIDE = 64
DSA_QBLK = 128
REL_BUCKETS = 32
REL_MAX_DIST = 1024
NEG_INF = -1e30

RW_HEADS = 8
RW_HEAD_DIM = 64
RW_WIDTH = RW_HEADS * RW_HEAD_DIM
RW_LORA = 64
RW_GATE_LORA = 128
RW_MAIN = 3 * RW_WIDTH
RW_LORA_IN = 4 * RW_LORA + RW_GATE_LORA
RW_GN_EPS = 64e-5

XA_HEADS = 4
XA_HEAD_DIM = 128
XA_WIDTH = XA_HEADS * XA_HEAD_DIM

LANES = 128
VMEM_LIMIT_BYTES = 56 * 1024 * 1024


def _params(*sem):
    return pltpu.CompilerParams(dimension_semantics=sem, vmem_limit_bytes=VMEM_LIMIT_BYTES)


_NN = (((1,), (0,)), ((), ()))
_NT = (((1,), (1,)), ((), ()))


def _dot1(a, b, dims=_NN):
    return lax.dot_general(a.astype(BF16), b.astype(BF16), dims, preferred_element_type=F32)


def _split2(a):
    hi = a.astype(BF16)
    lo = (a - hi.astype(F32)).astype(BF16)
    return hi, lo


def _dot3(a, b, dims=_NN):
    ah, al = _split2(a)
    bh, bl = _split2(b)
    dg = functools.partial(lax.dot_general, dimension_numbers=dims, preferred_element_type=F32)
    return dg(ah, bh) + (dg(al, bh) + dg(ah, bl))


def _dot_exact_lhs(a01, b):
    a = a01.astype(BF16)
    b1 = b.astype(BF16)
    r1 = b - b1.astype(F32)
    b2 = r1.astype(BF16)
    b3 = (r1 - b2.astype(F32)).astype(BF16)
    dg = functools.partial(lax.dot_general, dimension_numbers=_NN, preferred_element_type=F32)
    return dg(a, b1) + (dg(a, b2) + dg(a, b3))


def _dot_exact_rhs(a, b01):
    b = b01.astype(BF16)
    a1 = a.astype(BF16)
    r1 = a - a1.astype(F32)
    a2 = r1.astype(BF16)
    a3 = (r1 - a2.astype(F32)).astype(BF16)
    dg = functools.partial(lax.dot_general, dimension_numbers=_NN, preferred_element_type=F32)
    return dg(a1, b) + (dg(a2, b) + dg(a3, b))


def _interleave(chains):
    chains = list(chains)
    while chains:
        alive = []
        for ch in chains:
            try:
                next(ch)
                alive.append(ch)
            except StopIteration:
                pass
        chains = alive


def _merge_masks(n, reverse):
    ii = lax.broadcasted_iota(jnp.int32, (n, n), 0)
    jj = lax.broadcasted_iota(jnp.int32, (n, n), 1)
    tri = (ii < jj) if reverse else (ii > jj)
    out = []
    size = 1
    while size < CHUNK:
        out.append(tri & ((ii // (2 * size)) == (jj // (2 * size))) & ((ii // size) != (jj // size)))
        size *= 2
    return out


def _tri_inverse(a, eye_f, level_masks, out):
    t = eye_f - jnp.where(level_masks[0], a, 0.0)
    for m in level_masks[1:]:
        x = _dot1(jnp.where(m, a, 0.0), t)
        yield
        t = t - _dot1(t, x)
        yield
    out.append(t)


def _sigmoid(x):
    return 1.0 / (1.0 + jnp.exp(-x))


def _softplus(x):
    return jnp.maximum(x, 0.0) + jnp.log1p(jnp.exp(-jnp.abs(x)))


def _shift_down(x, row):
    return jnp.where(row == 0, 0.0, pltpu.roll(x, 1, 0))


def _shift_up(x, row):
    n = x.shape[0]
    return jnp.where(row == n - 1, 0.0, pltpu.roll(x, n - 1, 0))


def _norm_mm_body(x_ref, g_ref, w_ref, *rest, act, precise):
    if precise:
        wp_ref, o_ref, op_ref, xn_ref = rest
    else:
        o_ref, xn_ref = rest

    @pl.when(pl.program_id(1) == 0)
    def _():
        x = x_ref[...]
        xn = x * lax.rsqrt(jnp.mean(x * x, axis=-1, keepdims=True) + RMS_EPS) * g_ref[...]
        xn_ref[...] = xn.astype(BF16)
        if precise:
            op_ref[...] = _dot3(xn, wp_ref[...])

    acc = jnp.dot(xn_ref[...], w_ref[...], preferred_element_type=F32)
    if act == "relu2":
        acc = jnp.square(jnp.maximum(acc, 0.0))
    o_ref[...] = acc.astype(o_ref.dtype)


def norm_mm(x, g, w, wp=None, act=None, out_dtype=F32, tm=512, tn=1024):
    t, d = x.shape
    n = w.shape[1]
    tm = min(tm, t)
    tn = min(tn, n)
    assert t % tm == 0 and n % tn == 0
    precise = wp is not None
    in_specs = [
        pl.BlockSpec((tm, d), lambda i, j: (i, 0)),
        pl.BlockSpec((1, d), lambda i, j: (0, 0)),
        pl.BlockSpec((d, tn), lambda i, j: (0, j)),
    ]
    out_specs = pl.BlockSpec((tm, tn), lambda i, j: (i, j))
    out_shape = jax.ShapeDtypeStruct((t, n), out_dtype)
    args = [x, g.reshape(1, d), w]
    if precise:
        npc = wp.shape[1]
        in_specs.append(pl.BlockSpec((d, npc), lambda i, j: (0, 0)))
        out_specs = [out_specs, pl.BlockSpec((tm, npc), lambda i, j: (i, 0))]
        out_shape = [out_shape, jax.ShapeDtypeStruct((t, npc), F32)]
        args.append(wp)
    return pl.pallas_call(
        functools.partial(_norm_mm_body, act=act, precise=precise),
        grid=(t // tm, n // tn),
        in_specs=in_specs,
        out_specs=out_specs,
        out_shape=out_shape,
        scratch_shapes=[pltpu.VMEM((tm, d), BF16)],
        compiler_params=_params("parallel", "arbitrary"),
        name="norm_mm",
    )(*args)


def _mm_res_body(a_ref, w_ref, r_ref, o_ref):
    acc = jnp.dot(a_ref[...], w_ref[...], preferred_element_type=F32)

    @pl.when(pl.program_id(2) == 0)
    def _():
        o_ref[...] = r_ref[...] + acc

    @pl.when(pl.program_id(2) > 0)
    def _():
        o_ref[...] += acc


def mm_res(a, w, res, tm=1024, tn=1024, tk=2048):
    t, k = a.shape
    n = w.shape[1]
    tm, tn, tk = min(tm, t), min(tn, n), min(tk, k)
    assert t % tm == 0 and n % tn == 0 and k % tk == 0
    return pl.pallas_call(
        _mm_res_body,
        grid=(t // tm, n // tn, k // tk),
        in_specs=[
            pl.BlockSpec((tm, tk), lambda i, j, kk: (i, kk)),
            pl.BlockSpec((tk, tn), lambda i, j, kk: (kk, j)),
            pl.BlockSpec((tm, tn), lambda i, j, kk: (i, j)),
        ],
        out_specs=pl.BlockSpec((tm, tn), lambda i, j, kk: (i, j)),
        out_shape=jax.ShapeDtypeStruct((t, n), F32),
        compiler_params=_params("parallel", "parallel", "arbitrary"),
        name="mm_res",
    )(a, w, res)


def _xattn_body(x_ref, g_ref, wq_ref, kv_ref, wo_ref, qn_ref, kn_ref, o_ref):
    x = x_ref[...]
    xn = x * lax.rsqrt(jnp.mean(x * x, axis=-1, keepdims=True) + RMS_EPS) * g_ref[...]
    q = jnp.dot(xn.astype(BF16), wq_ref[...], preferred_element_type=F32)
    kv = kv_ref[0]
    outs = []
    for h in range(XA_HEADS):
        sl = slice(h * XA_HEAD_DIM, (h + 1) * XA_HEAD_DIM)
        qh = q[:, sl]
        qh = qh * lax.rsqrt(jnp.mean(qh * qh, axis=-1, keepdims=True) + RMS_EPS) * qn_ref[...]
        kh = kv[:, sl]
        kh = kh * lax.rsqrt(jnp.mean(kh * kh, axis=-1, keepdims=True) + RMS_EPS) * kn_ref[...]
        vh = kv[:, XA_WIDTH + h * XA_HEAD_DIM:XA_WIDTH + (h + 1) * XA_HEAD_DIM]
        logits = _dot1(qh, kh, _NT) * (XA_HEAD_DIM ** -0.5)
        m = jnp.max(logits, axis=-1, keepdims=True)
        p = jnp.exp(logits - m)
        s = jnp.sum(p, axis=-1, keepdims=True)
        outs.append(_dot1(p, vh) / s)
    o = jnp.concatenate(outs, axis=-1).astype(BF16)
    o_ref[...] = x + jnp.dot(o, wo_ref[...], preferred_element_type=F32)


def xattn(x, g, wq, kv, wo, qn, kn, batch, ts=512):
    t, d = x.shape
    s = t // batch
    ts = min(ts, s)
    nst = s // ts
    m = kv.shape[1]
    return pl.pallas_call(
        _xattn_body,
        grid=(batch, nst),
        in_specs=[
            pl.BlockSpec((ts, d), lambda b, i: (b * nst + i, 0)),
            pl.BlockSpec((1, d), lambda b, i: (0, 0)),
            pl.BlockSpec((d, XA_WIDTH), lambda b, i: (0, 0)),
            pl.BlockSpec((1, m, 2 * XA_WIDTH), lambda b, i: (b, 0, 0)),
            pl.BlockSpec((XA_WIDTH, d), lambda b, i: (0, 0)),
            pl.BlockSpec((1, XA_HEAD_DIM), lambda b, i: (0, 0)),
            pl.BlockSpec((1, XA_HEAD_DIM), lambda b, i: (0, 0)),
        ],
        out_specs=pl.BlockSpec((ts, d), lambda b, i: (b * nst + i, 0)),
        out_shape=jax.ShapeDtypeStruct((t, d), F32),
        compiler_params=_params("parallel", "parallel"),
        name="xattn",
    )(x, g.reshape(1, d), wq, kv, wo, qn.reshape(1, -1), kn.reshape(1, -1))


def _tri_masks(n, blk):
    ii = lax.broadcasted_iota(jnp.int32, (n, n), 0)
    jj = lax.broadcasted_iota(jnp.int32, (n, n), 1)
    same = (ii // blk) == (jj // blk) if n != blk else None

    def m(c):
        return c if same is None else (c & same)

    return {
        False: (m(ii >= jj), m(ii > jj)),
        True: (m(ii <= jj), m(ii < jj)),
        "eye": ii == jj,
    }


def _seg_cumsum(x, pos, reverse):
    n = x.shape[0]
    sh = 1
    while sh < CHUNK:
        if reverse:
            x = x + jnp.where(pos < CHUNK - sh, pltpu.roll(x, n - sh, 0), 0.0)
        else:
            x = x + jnp.where(pos >= sh, pltpu.roll(x, sh, 0), 0.0)
        sh *= 2
    return x


def _dn_body(q_ref, k_ref, v_ref, gate_ref, ab_ref, cq_ref, ck_ref, cv_ref, alog_ref, dt_ref, gain_ref,
             y_ref, qs_ref, ks_ref, vs_ref, kt_ref, bg_ref, gt_ref, of_ref, ob_ref,
             u_ref, w_ref, qg_ref, qk_ref, kdt_ref, eg_ref):
    s = q_ref.shape[0]
    h = pl.program_id(1)
    row = lax.broadcasted_iota(jnp.int32, (s, 1), 0)
    c = CHUNK

    def conv_silu(x_ref, cw_ref):
        x = x_ref[...]
        w = cw_ref[...]
        y = _shift_down(x, row) * w[0:1] + x * w[1:2] + _shift_up(x, row) * w[2:3]
        return y * _sigmoid(y)

    q = conv_silu(q_ref, cq_ref)
    q = q * lax.rsqrt(jnp.sum(q * q, axis=-1, keepdims=True) + L2_EPS) * (DN_HEAD_DIM ** -0.5)
    qs_ref[...] = q
    k = conv_silu(k_ref, ck_ref)
    k = k * lax.rsqrt(jnp.sum(k * k, axis=-1, keepdims=True) + L2_EPS)
    ks_ref[...] = k
    kt_ref[...] = k.T
    vs_ref[...] = conv_silu(v_ref, cv_ref)

    ab = ab_ref[...]
    beta_all = _sigmoid(ab)
    g_all = -jnp.exp(alog_ref[...]) * _softplus(ab + dt_ref[...])
    lane = lax.broadcasted_iota(jnp.int32, (1, 4 * DN_HEADS), 1)

    def pick(t, idx):
        return jnp.sum(jnp.where(lane == idx, t, 0.0), axis=-1, keepdims=True)

    pos = row % c
    lane128 = lax.broadcasted_iota(jnp.int32, (1, LANES), 1)
    cols = (pick(beta_all, h), pick(beta_all, DN_HEADS + h),
            _seg_cumsum(pick(g_all, 2 * DN_HEADS + h), pos, False),
            _seg_cumsum(pick(g_all, 3 * DN_HEADS + h), pos, True))
    bg = jnp.zeros((s, LANES), F32)
    for i, col in enumerate(cols):
        bg = jnp.where(lane128 == i, col, bg)
    bg_ref[...] = bg
    gt_ref[...] = bg.T[0:8, :]

    masks = _tri_masks(PAIR, c)
    merge = {rev: _merge_masks(PAIR, rev) for rev in (False, True)}
    eye_f = jnp.where(masks["eye"], 1.0, 0.0)
    npair = s // PAIR
    group = min(DN_PREP_PAIRS, npair)

    def prep(pidx, d):
        reverse = d == 1
        rows = pl.ds(pl.multiple_of(pidx * PAIR, PAIR), PAIR)
        qp, kp, vp = qs_ref[rows, :], ks_ref[rows, :], vs_ref[rows, :]
        bgp = bg_ref[rows, :]
        beta = bgp[:, d:d + 1]
        gcol = bgp[:, 2 + d:3 + d]
        grow = gt_ref[2 + d:3 + d, rows]
        incl, strict = masks[reverse]
        decay = jnp.where(incl, jnp.exp(jnp.where(incl, gcol - grow, 0.0)), 0.0)
        kb = kp * beta
        kq = _dot1(jnp.concatenate([kb, qp], axis=0), kp, _NT)
        yield
        qk_ref[d, rows, :] = jnp.where(incl, kq[PAIR:] * decay, 0.0).astype(BF16)
        t_inv = []
        yield from _tri_inverse(jnp.where(strict, kq[:PAIR] * decay, 0.0), eye_f, merge[reverse], t_inv)
        egc = jnp.exp(gcol)
        uw = _dot1(t_inv[0], jnp.concatenate([vp * beta, kb * egc], axis=1))
        yield
        u_ref[d, rows, :] = uw[:, :DN_HEAD_DIM]
        w_ref[d, rows, :] = uw[:, DN_HEAD_DIM:].astype(BF16)
        qg_ref[d, rows, :] = (qp * egc).astype(BF16)
        last0, last1 = (0, c) if reverse else (c - 1, PAIR - 1)
        glast = jnp.where(lane128 < c, grow[:, last0:last0 + 1], grow[:, last1:last1 + 1])
        kdt_ref[d, :, rows] = (kt_ref[:, rows] * jnp.exp(glast - grow)).astype(BF16)
        eg_ref[d, :, rows] = jnp.broadcast_to(jnp.exp(glast), (8, PAIR))

    def prep_body(i, _):
        _interleave(prep(i * group + j, d) for j in range(group) for d in range(2))
        return 0

    lax.fori_loop(0, npair // group, prep_body, 0)

    def seq(states, pidx, d):
        pair_rows = pl.ds(pl.multiple_of(pidx * PAIR, PAIR), PAIR)
        kdt = kdt_ref[d, :, pair_rows]
        eg = eg_ref[d, 0:1, pair_rows]
        for half in ((1, 0) if d else (0, 1)):
            r0 = half * c
            rows = pl.ds(pl.multiple_of(pidx * PAIR + r0, c), c)
            ws = _dot1(jnp.concatenate([w_ref[d, rows, :], qg_ref[d, rows, :]], axis=0), states[d])
            yield
            v_new = u_ref[d, rows, :] - ws[:c]
            o = ws[c:] + _dot1(qk_ref[d, rows, r0:r0 + c], v_new)
            if d:
                ob_ref[rows, :] = o
            else:
                of_ref[rows, :] = o
            states[d] = states[d] * eg[:, r0:r0 + 1] + _dot1(kdt[:, r0:r0 + c], v_new)
            yield

    def seq_body(p, carry):
        states = list(carry)
        _interleave([seq(states, p, 0), seq(states, npair - 1 - p, 1)])
        return tuple(states)

    zero = jnp.zeros((DN_HEAD_DIM, DN_HEAD_DIM), F32)
    lax.fori_loop(0, npair, seq_body, (zero, zero))

    o = of_ref[...] + ob_ref[...]
    o = o * lax.rsqrt(jnp.mean(o * o, axis=-1, keepdims=True) + RMS_EPS) * gain_ref[...]
    gate = gate_ref[...]
    y_ref[...] = (o * (gate * _sigmoid(gate))).astype(y_ref.dtype)


def deltanet(p, ab, conv_t, alog32, dt32, gain, batch):
    t = p.shape[0]
    s = t // batch
    assert s % PAIR == 0 and (s // PAIR) % min(DN_PREP_PAIRS, s // PAIR) == 0
    hd = DN_HEAD_DIM
    col = lambda off: pl.BlockSpec((s, hd), lambda b, h: (b, off + h))
    cw = lambda off: pl.BlockSpec((3, hd), lambda b, h: (0, off + h))
    small = lambda n: pl.BlockSpec((1, n), lambda b, h: (0, 0))
    big = pltpu.VMEM((s, hd), F32)
    half = pltpu.VMEM((2, s, hd), BF16)
    return pl.pallas_call(
        _dn_body,
        grid=(batch, DN_HEADS),
        in_specs=[col(0), col(DN_HEADS), col(2 * DN_HEADS), col(3 * DN_HEADS),
                  pl.BlockSpec((s, 4 * DN_HEADS), lambda b, h: (b, 0)),
                  cw(0), cw(DN_HEADS), cw(2 * DN_HEADS),
                  small(4 * DN_HEADS), small(4 * DN_HEADS), small(hd)],
        out_specs=pl.BlockSpec((s, hd), lambda b, h: (b, h)),
        out_shape=jax.ShapeDtypeStruct((t, DN_WIDTH), BF16),
        scratch_shapes=[big, big, big, pltpu.VMEM((hd, s), F32), big, pltpu.VMEM((8, s), F32), big, big,
                        pltpu.VMEM((2, s, hd), F32), half, half, half, pltpu.VMEM((2, hd, s), BF16),
                        pltpu.VMEM((2, 8, s), F32)],
        compiler_params=_params("parallel", "parallel"),
        name="deltanet",
    )(p, p, p, p, ab, conv_t, conv_t, conv_t, alog32, dt32, gain.reshape(1, hd))


def _sconv_body(b_ref, c_ref, u_ref, w_ref, y_ref):
    s = b_ref.shape[0]
    row = lax.broadcasted_iota(jnp.int32, (s, 1), 0)
    cu = c_ref[...] * u_ref[...]
    w = w_ref[...]
    y = _shift_down(cu, row) * w[0:1] + cu * w[1:2] + _shift_up(cu, row) * w[2:3]
    y_ref[...] = (b_ref[...] * y).astype(y_ref.dtype)


def short_conv(p, conv_t, col0, batch, tc=256):
    t = p.shape[0]
    s = t // batch
    nct = SC_WIDTH // tc
    base = col0 // tc
    col = lambda off: pl.BlockSpec((s, tc), lambda b, c: (b, base + off * nct + c))
    return pl.pallas_call(
        _sconv_body,
        grid=(batch, nct),
        in_specs=[col(0), col(1), col(2), pl.BlockSpec((3, tc), lambda b, c: (0, c))],
        out_specs=pl.BlockSpec((s, tc), lambda b, c: (b, c)),
        out_shape=jax.ShapeDtypeStruct((t, SC_WIDTH), BF16),
        compiler_params=_params("parallel", "parallel"),
        name="short_conv",
    )(p, p, p, conv_t)


def _dsa_body(*refs, seq):
    q_refs = refs[0:3]
    k_refs = refs[3:6]
    v_refs = refs[6:9]
    qn_ref, kn_ref, bias_ref, y_ref, qs_ref, ks_ref, kbuf_ref, vbuf_ref, og_ref, lse_ref = refs[9:]
    s = seq
    qb = DSA_QBLK
    side = DSA_SIDE
    kj = lax.broadcasted_iota(jnp.int32, (1, qb + 2 * side), 1)

    for gi, (_, dil) in enumerate(DSA_PATTERNS):
        sub = s // dil
        nblk = sub // qb
        q = q_refs[gi][...]
        qs_ref[...] = (q * lax.rsqrt(jnp.mean(q * q, axis=-1, keepdims=True) + RMS_EPS) * qn_ref[...]
                       * (DSA_HEAD_DIM ** -0.5))
        k = k_refs[gi][...]
        ks_ref[...] = k * lax.rsqrt(jnp.mean(k * k, axis=-1, keepdims=True) + RMS_EPS) * kn_ref[...]
        zpad = jnp.zeros((side, DSA_HEAD_DIM), F32)
        kbuf_ref[0:side, :] = zpad
        vbuf_ref[0:side, :] = zpad
        kbuf_ref[side + sub:2 * side + sub, :] = zpad
        vbuf_ref[side + sub:2 * side + sub, :] = zpad
        bias = bias_ref[0, gi]

        def residue(r, _, gi=gi, dil=dil, sub=sub, nblk=nblk, bias=bias):
            kbuf_ref[side:side + sub, :] = ks_ref[pl.ds(r, sub, stride=dil), :]
            vbuf_ref[side:side + sub, :] = v_refs[gi][pl.ds(r, sub, stride=dil), :]

            def block(n, _):
                rows = pl.ds(r + n * (qb * dil), qb, stride=dil)
                qblk = qs_ref[rows, :]
                win = pl.ds(pl.multiple_of(n * qb, qb), qb + 2 * side)
                logits = _dot1(qblk, kbuf_ref[win, :], _NT) + bias
                pos = n * qb - side + kj
                logits = jnp.where((pos >= 0) & (pos < sub), logits, NEG_INF)
                m = jnp.max(logits, axis=-1, keepdims=True)
                p = jnp.exp(logits - m)
                ssum = jnp.sum(p, axis=-1, keepdims=True)
                og_ref[gi, rows, :] = _dot1(p, vbuf_ref[win, :]) / ssum
                lse_ref[gi, rows, :] = jnp.broadcast_to(m + jnp.log(ssum), (qb, DSA_HEAD_DIM))
                return 0

            lax.fori_loop(0, nblk, block, 0)
            return 0

        lax.fori_loop(0, dil, residue, 0)

    lse = [lse_ref[gi] for gi in range(DSA_GROUPS)]
    mx = jnp.maximum(jnp.maximum(lse[0], lse[1]), lse[2])
    ws = [jnp.exp(l - mx) for l in lse]
    num = ws[0] * og_ref[0] + ws[1] * og_ref[1] + ws[2] * og_ref[2]
    y_ref[...] = (num / (ws[0] + ws[1] + ws[2])).astype(y_ref.dtype)


def dilated_attention(p, qn, kn, bias_tab, batch):
    t = p.shape[0]
    s = t // batch
    hd = DSA_HEAD_DIM
    nh = DSA_HEADS

    def col(part, gi):
        return pl.BlockSpec((s, hd), lambda b, j: (b, part * nh + gi * DSA_HPG + j))

    in_specs = [col(part, gi) for part in range(3) for gi in range(DSA_GROUPS)]
    in_specs += [pl.BlockSpec((1, hd), lambda b, j: (0, 0)), pl.BlockSpec((1, hd), lambda b, j: (0, 0)),
                 pl.BlockSpec((1, DSA_GROUPS, DSA_QBLK, DSA_QBLK + 2 * DSA_SIDE), lambda b, j: (j, 0, 0, 0))]
    big = pltpu.VMEM((s, hd), F32)
    pad = pltpu.VMEM((s + 2 * DSA_SIDE, hd), F32)
    grp = pltpu.VMEM((DSA_GROUPS, s, hd), F32)
    return pl.pallas_call(
        functools.partial(_dsa_body, seq=s),
        grid=(batch, DSA_HPG),
        in_specs=in_specs,
        out_specs=pl.BlockSpec((s, hd), lambda b, j: (b, j)),
        out_shape=jax.ShapeDtypeStruct((t, DSA_HPG * hd), BF16),
        scratch_shapes=[big, big, pad, pad, grp, grp],
        compiler_params=_params("parallel", "parallel"),
        name="dilated_attention",
    )(*([p] * 9), qn.reshape(1, hd), kn.reshape(1, hd), bias_tab)


def _t5_bucket(rel):
    half = REL_BUCKETS // 2
    max_exact = half // 2
    n = jnp.abs(rel)
    scaled = jnp.log(jnp.maximum(n, max_exact).astype(F32) / max_exact) / math.log(REL_MAX_DIST / max_exact)
    large = jnp.minimum(max_exact + (scaled * (half - max_exact)).astype(jnp.int32), half - 1)
    return jnp.where(rel > 0, half, 0) + jnp.where(n < max_exact, n, large)


def _dsa_bias_table(rel_bias):
    qi = jnp.arange(DSA_QBLK)[:, None]
    kj = jnp.arange(DSA_QBLK + 2 * DSA_SIDE)[None, :]
    rel = kj - DSA_SIDE - qi
    inside = jnp.abs(rel) <= DSA_SIDE
    tabs = []
    for gi, (_, dil) in enumerate(DSA_PATTERNS):
        bucket = _t5_bucket(jnp.clip(rel, -DSA_SIDE, DSA_SIDE) * dil)
        b = rel_bias[bucket][:, :, gi * DSA_HPG:(gi + 1) * DSA_HPG]
        tabs.append(jnp.where(inside[:, :, None], b, NEG_INF))
    return jnp.transpose(jnp.stack(tabs), (3, 0, 1, 2)).astype(F32)


def _head_block_diag():
    ii = lax.broadcasted_iota(jnp.int32, (LANES, LANES), 0)
    jj = lax.broadcasted_iota(jnp.int32, (LANES, LANES), 1)
    return (ii // RW_HEAD_DIM) == (jj // RW_HEAD_DIM)


def _rw_prep_body(r_ref, k_ref, v_ref, lo_ref, mur_ref, muk_ref, muv_ref, mulo_ref, w0_ref, w2_ref, a0_ref,
                  a2_ref, g2_ref, kk_ref, ka_ref, rk_ref,
                  ro_ref, vo_ref, kko_ref, bon_ref, gate_ref, lwf_ref, lwb_ref, kdf_ref, kdb_ref, bbf_ref, bbb_ref):
    s = r_ref.shape[0]
    row = lax.broadcasted_iota(jnp.int32, (s, 1), 0)

    def mix(t, mu):
        return t + mu * (0.5 * (_shift_down(t, row) + _shift_up(t, row)) - t)

    r = mix(r_ref[...], mur_ref[...])
    kr = mix(k_ref[...], muk_ref[...])
    v = mix(v_ref[...], muv_ref[...])
    lo = mix(lo_ref[...], mulo_ref[...])
    bd = jnp.where(_head_block_diag(), 1.0, 0.0)

    def head_sum(t):
        return _dot_exact_rhs(t, bd)

    kk = kr * kk_ref[...]
    kk = kk * lax.rsqrt(head_sum(kk * kk) + L2_EPS)
    gd = lo[:, 4 * RW_LORA:]
    gate_ref[...] = _dot3(_sigmoid(gd), g2_ref[...])
    ro_ref[...] = r
    vo_ref[...] = v
    kko_ref[...] = kk
    bonus = jnp.zeros_like(r)
    outs = ((lwf_ref, kdf_ref, bbf_ref), (lwb_ref, kdb_ref, bbb_ref))
    for d in range(2):
        wd = lo[:, d * RW_LORA:(d + 1) * RW_LORA]
        ad = lo[:, (2 + d) * RW_LORA:(3 + d) * RW_LORA]
        w_log = -_softplus(-(w0_ref[d:d + 1, :] + _dot3(jnp.tanh(wd), w2_ref[d]))) - 0.5
        a = _sigmoid(a0_ref[d:d + 1, :] + _dot3(ad, a2_ref[d]))
        kd = kr * (1.0 + (a - 1.0) * ka_ref[...])
        lw_ref, kd_ref, bb_ref = outs[d]
        lw_ref[...] = -jnp.exp(w_log)
        kd_ref[...] = kd
        bb_ref[...] = kk * a
        bonus = bonus + head_sum(r * kd * rk_ref[...]) * v
    bon_ref[...] = bonus


def rwkv_prep(p, lora_in, mu, w0, w2, a0, a2, g2, k_k, k_a, r_k, col0, batch):
    t = p.shape[0]
    s = t // batch
    nct = RW_WIDTH // LANES
    base = col0 // LANES
    nlo = RW_LORA_IN
    col = lambda off: pl.BlockSpec((s, LANES), lambda b, c: (b, base + off * nct + c))
    vec = lambda off: pl.BlockSpec((1, LANES), lambda b, c: (0, off * nct + c))
    mu_main = mu[:RW_MAIN].reshape(1, RW_MAIN)
    mu_lo = mu[RW_MAIN:].reshape(1, nlo)
    out_spec = pl.BlockSpec((s, LANES), lambda b, c: (b, c))
    n_out = 11
    return pl.pallas_call(
        _rw_prep_body,
        grid=(batch, nct),
        in_specs=[col(0), col(1), col(2),
                  pl.BlockSpec((s, nlo), lambda b, c: (b, 0)),
                  vec(0), vec(1), vec(2),
                  pl.BlockSpec((1, nlo), lambda b, c: (0, 0)),
                  pl.BlockSpec((2, LANES), lambda b, c: (0, c)),
                  pl.BlockSpec((2, RW_LORA, LANES), lambda b, c: (0, 0, c)),
                  pl.BlockSpec((2, LANES), lambda b, c: (0, c)),
                  pl.BlockSpec((2, RW_LORA, LANES), lambda b, c: (0, 0, c)),
                  pl.BlockSpec((RW_GATE_LORA, LANES), lambda b, c: (0, c)),
                  vec(0), vec(0), vec(0)],
        out_specs=[out_spec] * n_out,
        out_shape=[jax.ShapeDtypeStruct((t, RW_WIDTH), F32)] * n_out,
        compiler_params=_params("parallel", "parallel"),
        name="rwkv_prep",
    )(p, p, p, lora_in, mu_main, mu_main, mu_main, mu_lo, w0, w2, a0, a2, g2,
      k_k.reshape(1, RW_WIDTH), k_a.reshape(1, RW_WIDTH), r_k.reshape(1, RW_WIDTH))


def _rw_scan_body(r_ref, v_ref, kk_ref, bon_ref, gate_ref, lwf_ref, lwb_ref, kdf_ref, kdb_ref, bbf_ref, bbb_ref,
                  lnw_ref, lnb_ref, y_ref, cum_ref, cumt_ref, kdt_ref, bbt_ref, yf_ref, yb_ref,
                  tr_ref, tc_ref, rbk_ref, bkt_ref):
    s = r_ref.shape[0]
    c = CHUNK
    c2 = 2 * c
    row = lax.broadcasted_iota(jnp.int32, (s, 1), 0)
    pos = row % c
    lw_refs, kd_refs, bb_refs = (lwf_ref, lwb_ref), (kdf_ref, kdb_ref), (bbf_ref, bbb_ref)
    for d in range(2):
        cum = _seg_cumsum(lw_refs[d][...], pos, d == 1)
        cum_ref[d] = cum
        cumt_ref[d] = cum.T
        kdt_ref[d] = kd_refs[d][...].T
        bbt_ref[d] = bb_refs[d][...].T

    masks = _tri_masks(c2, c)
    merge = {rev: _merge_masks(c2, rev) for rev in (False, True)}
    eye2_f = jnp.where(masks["eye"], 1.0, 0.0)
    head0 = lax.broadcasted_iota(jnp.int32, (1, LANES), 1) < RW_HEAD_DIM
    head_bd = _head_block_diag()
    npair = s // PAIR
    group = min(RW_PREP_PAIRS, npair)

    def stack(x):
        return jnp.concatenate([jnp.where(head0, x, 0.0), jnp.where(head0, 0.0, x)], axis=0)

    def prep(pidx, d, half):
        reverse = d == 1
        incl2, strict2 = masks[reverse]
        cidx = 2 * pidx + half
        cols = pl.ds(pl.multiple_of(pidx * PAIR, PAIR), PAIR)
        rows = pl.ds(pl.multiple_of(cidx * c, c), c)
        hs = slice(half * c, (half + 1) * c)
        r, v, kk = r_ref[rows, :], v_ref[rows, :], kk_ref[rows, :]
        lw, kd, bb = lw_refs[d][rows, :], kd_refs[d][rows, :], bb_refs[d][rows, :]
        cum = cum_ref[d, rows, :]
        e_neg = jnp.exp(-cum)
        a_s = stack(-kk * jnp.exp(cum - lw))
        r_s = stack(r * jnp.exp(cum))
        ar = jnp.concatenate([a_s, r_s], axis=0)
        bk = jnp.concatenate([stack(bb * e_neg), stack(kd * e_neg)], axis=0)
        g = _dot1(ar, bk, _NT)
        yield
        rbk_ref[d, cidx] = jnp.concatenate([jnp.where(incl2, g[c2:, :c2], 0.0),
                                            jnp.where(incl2, g[c2:, c2:], 0.0)], axis=1).astype(BF16)
        av = _dot1(jnp.where(strict2, g[:c2, c2:], 0.0), stack(v))
        yield
        t_inv = []
        yield from _tri_inverse(-jnp.where(strict2, g[:c2, :c2], 0.0), eye2_f, merge[reverse], t_inv)
        tt = _dot1(t_inv[0], jnp.concatenate([a_s, av], axis=1))
        yield
        tr_ref[d, cidx] = jnp.concatenate([tt[:, :LANES], r_s], axis=0).astype(BF16)
        tc_ref[d, cidx] = tt[:, LANES:].astype(BF16)
        cum_t = cumt_ref[d, :, cols][:, hs]
        last = 0 if reverse else c - 1
        e_out_t = jnp.exp(cum_t[:, last:last + 1] - cum_t)
        bkt_ref[d, cidx] = jnp.concatenate([bbt_ref[d, :, cols][:, hs] * e_out_t,
                                            kdt_ref[d, :, cols][:, hs] * e_out_t], axis=1).astype(BF16)

    def prep_body(i, _):
        _interleave(prep(i * group + j, d, half) for j in range(group) for d in range(2) for half in range(2))
        return 0

    lax.fori_loop(0, npair // group, prep_body, 0)

    def seq(states, pidx, d):
        cum_t2 = cumt_ref[d, :, pl.ds(pl.multiple_of(pidx * PAIR, PAIR), PAIR)]
        for half in ((1, 0) if d else (0, 1)):
            cidx = 2 * pidx + half
            rows = pl.ds(pl.multiple_of(cidx * c, c), c)
            v = v_ref[rows, :]
            x = _dot1(tr_ref[d, cidx], states[d])
            yield
            ps = x[:c2] + tc_ref[d, cidx].astype(F32)
            os_ = x[c2:] + _dot1(rbk_ref[d, cidx], jnp.concatenate([ps, stack(v)], axis=0))
            y = os_[:c] + os_[c:]
            if d:
                yb_ref[rows, :] = y
            else:
                yf_ref[rows, :] = y
            pv = jnp.concatenate([ps[:c] + ps[c:], v], axis=0)
            last = half * c + (0 if d else c - 1)
            e_tot = jnp.exp(cum_t2[:, last:last + 1])
            states[d] = states[d] * e_tot + jnp.where(head_bd, _dot1(bkt_ref[d, cidx], pv), 0.0)
            yield

    def seq_body(p, carry):
        states = list(carry)
        _interleave([seq(states, p, 0), seq(states, npair - 1 - p, 1)])
        return tuple(states)

    zero = jnp.zeros((LANES, LANES), F32)
    lax.fori_loop(0, npair, seq_body, (zero, zero))

    bd = jnp.where(head_bd, 1.0, 0.0)
    y = yf_ref[...] + yb_ref[...]
    mean = _dot_exact_rhs(y, bd) * (1.0 / RW_HEAD_DIM)
    yc = y - mean
    var = _dot_exact_rhs(yc * yc, bd) * (1.0 / RW_HEAD_DIM)
    yn = yc * lax.rsqrt(var + RW_GN_EPS) * lnw_ref[...] + lnb_ref[...]
    y_ref[...] = ((yn + bon_ref[...]) * gate_ref[...]).astype(y_ref.dtype)


def rwkv_scan(prep, ln_w, ln_b, batch):
    t = prep[0].shape[0]
    s = t // batch
    assert s % PAIR == 0 and (s // PAIR) % min(RW_PREP_PAIRS, s // PAIR) == 0
    nct = RW_WIDTH // LANES
    nchunk = s // CHUNK
    blk = pl.BlockSpec((s, LANES), lambda b, c: (b, c))
    vec = pl.BlockSpec((1, LANES), lambda b, c: (0, c))
    tsp = pltpu.VMEM((2, LANES, s), F32)
    big = pltpu.VMEM((s, LANES), F32)
    per_chunk = lambda rows, cols: pltpu.VMEM((2, nchunk, rows, cols), BF16)
    return pl.pallas_call(
        _rw_scan_body,
        grid=(batch, nct),
        in_specs=[blk] * 11 + [vec, vec],
        out_specs=blk,
        out_shape=jax.ShapeDtypeStruct((t, RW_WIDTH), BF16),
        scratch_shapes=[pltpu.VMEM((2, s, LANES), F32), tsp, tsp, tsp, big, big,
                        per_chunk(2 * PAIR, LANES), per_chunk(PAIR, LANES), per_chunk(PAIR, 2 * LANES),
                        per_chunk(PAIR, LANES)],
        compiler_params=_params("parallel", "parallel"),
        name="rwkv_scan",
    )(*prep, ln_w.reshape(1, RW_WIDTH), ln_b.reshape(1, RW_WIDTH))


def even_mixer(x, g, w_in, w_out, conv_qkv, a_log, dt_bias, out_gain, conv_sc, batch):
    c_ab = 4 * DN_WIDTH
    c_sc = c_ab + 4 * DN_HEADS
    w_main = jnp.concatenate([w_in[:, :c_ab], w_in[:, c_sc:]], axis=1).astype(BF16)
    p, ab = norm_mm(x, g, w_main, wp=w_in[:, c_ab:c_sc])
    zeros = jnp.zeros((2 * DN_HEADS,), F32)
    alog32 = jnp.concatenate([zeros, a_log.reshape(-1)]).reshape(1, -1)
    dt32 = jnp.concatenate([zeros, dt_bias.reshape(-1)]).reshape(1, -1)
    y_dn = deltanet(p, ab, conv_qkv.T, alog32, dt32, out_gain, batch)
    y_sc = short_conv(p, conv_sc.T, c_ab, batch)
    y = jnp.concatenate([y_dn, y_sc], axis=1)
    return mm_res(y, w_out.astype(BF16), x)


def odd_mixer(x, g, w_in, w_out, qn, kn, bias_tab, mu, w0, w2, a0, a2, g2, k_k, k_a, r_k, ln_w, ln_b, batch):
    c_rw = 3 * DSA_QKV
    c_lo = c_rw + RW_MAIN
    p, lora_in = norm_mm(x, g, w_in[:, :c_lo].astype(BF16), wp=w_in[:, c_lo:])
    y_c = dilated_attention(p, qn, kn, bias_tab, batch)
    prep = rwkv_prep(p, lora_in, mu, w0, w2, a0, a2, g2, k_k, k_a, r_k, c_rw, batch)
    y_d = rwkv_scan(prep, ln_w, ln_b, batch)
    y = jnp.concatenate([y_c, y_d], axis=1)
    return mm_res(y, w_out.astype(BF16), x)


def kernel(x, mem, rel_bias, norm_mix, norm_xattn, norm_mem, norm_ffn, xa_wq, xa_wk, xa_wv, xa_wo, xa_qn, xa_kn, ffn_w1, ffn_w2, ev_w_in, ev_w_out, dn_conv, dn_a_log, dn_dt_bias, dn_norm, sc_conv, od_w_in, od_w_out, ca_qn, ca_kn, rw_mu, rw_w0, rw_w2, rw_a0, rw_a2, rw_g2, rw_k_k, rw_k_a, rw_r_k, rw_ln_w, rw_ln_b):
    batch, seq, d = x.shape
    n_mem = mem.shape[1]
    xf = x.reshape(batch * seq, d)
    memf = mem.reshape(batch * n_mem, d)
    bias_tab = _dsa_bias_table(rel_bias)
    for layer in range(DEPTH):
        i = layer // 2
        if layer % 2 == 0:
            xf = even_mixer(xf, norm_mix[layer], ev_w_in[i], ev_w_out[i], dn_conv[i], dn_a_log[i],
                            dn_dt_bias[i], dn_norm[i], sc_conv[i], batch)
        else:
            xf = odd_mixer(xf, norm_mix[layer], od_w_in[i], od_w_out[i], ca_qn[i], ca_kn[i], bias_tab, rw_mu[i],
                           rw_w0[i], rw_w2[i], rw_a0[i], rw_a2[i], rw_g2[i], rw_k_k[i], rw_k_a[i], rw_r_k[i],
                           rw_ln_w[i], rw_ln_b[i], batch)
        w_kv = jnp.concatenate([xa_wk[layer], xa_wv[layer]], axis=1).astype(BF16)
        kv = norm_mm(memf, norm_mem[layer], w_kv).reshape(batch, n_mem, 2 * XA_WIDTH)
        xf = xattn(xf, norm_xattn[layer], xa_wq[layer].astype(BF16), kv, xa_wo[layer].astype(BF16),
                   xa_qn[layer], xa_kn[layer], batch)
        h1 = norm_mm(xf, norm_ffn[layer], ffn_w1[layer].astype(BF16), act="relu2", out_dtype=BF16)
        xf = mm_res(h1, ffn_w2[layer].astype(BF16), xf)
    return xf.reshape(batch, seq, d)
```

```python
import functools
import math

import jax
import jax.numpy as jnp
import numpy as np
from jax import lax
from jax.experimental import pallas as pl
from jax.experimental.pallas import tpu as pltpu

F32 = jnp.float32
BF16 = jnp.bfloat16

D_MODEL = 2048
DEPTH = 4
RMS_EPS = 1e-6
L2_EPS = 1e-6

DN_HEADS = 8
DN_HEAD_DIM = 128
DN_WIDTH = DN_HEADS * DN_HEAD_DIM
SC_WIDTH = D_MODEL - DN_WIDTH
CHUNK = 64
PAIR = 2 * CHUNK
DN_PREP_PAIRS = 4
RW_PREP_PAIRS = 2

DSA_PATTERNS = ((128, 1), (512, 4), (2048, 16))
DSA_GROUPS = len(DSA_PATTERNS)
DSA_HPG = 4
DSA_HEAD_DIM = 128
DSA_HEADS = DSA_GROUPS * DSA_HPG
DSA_QKV = DSA_HEADS * DSA_HEAD_DIM
DSA_SIDE = 64
DSA_QBLK = 128
REL_BUCKETS = 32
REL_MAX_DIST = 1024
NEG_INF = -1e30

RW_HEADS = 8
RW_HEAD_DIM = 64
RW_WIDTH = RW_HEADS * RW_HEAD_DIM
RW_LORA = 64
RW_GATE_LORA = 128
RW_MAIN = 3 * RW_WIDTH
RW_LORA_IN = 4 * RW_LORA + RW_GATE_LORA
RW_GN_EPS = 64e-5

XA_HEADS = 4
XA_HEAD_DIM = 128
XA_WIDTH = XA_HEADS * XA_HEAD_DIM

LANES = 128
NORM_ROWS = 256
VMEM_LIMIT_BYTES = 56 * 1024 * 1024


def _params(*sem):
    return pltpu.CompilerParams(dimension_semantics=sem, vmem_limit_bytes=VMEM_LIMIT_BYTES)


_NN = (((1,), (0,)), ((), ()))
_NT = (((1,), (1,)), ((), ()))


def _dot1(a, b, dims=_NN):
    return lax.dot_general(a.astype(BF16), b.astype(BF16), dims, preferred_element_type=F32)


def _split2(a):
    hi = a.astype(BF16)
    lo = (a - hi.astype(F32)).astype(BF16)
    return hi, lo


def _dot3(a, b, dims=_NN):
    ah, al = _split2(a)
    bh, bl = _split2(b)
    dg = functools.partial(lax.dot_general, dimension_numbers=dims, preferred_element_type=F32)
    return dg(ah, bh) + (dg(al, bh) + dg(ah, bl))


def _dot_exact_lhs(a01, b):
    a = a01.astype(BF16)
    b1 = b.astype(BF16)
    r1 = b - b1.astype(F32)
    b2 = r1.astype(BF16)
    b3 = (r1 - b2.astype(F32)).astype(BF16)
    dg = functools.partial(lax.dot_general, dimension_numbers=_NN, preferred_element_type=F32)
    return dg(a, b1) + (dg(a, b2) + dg(a, b3))


def _dot_exact_rhs(a, b01):
    b = b01.astype(BF16)
    a1 = a.astype(BF16)
    r1 = a - a1.astype(F32)
    a2 = r1.astype(BF16)
    a3 = (r1 - a2.astype(F32)).astype(BF16)
    dg = functools.partial(lax.dot_general, dimension_numbers=_NN, preferred_element_type=F32)
    return dg(a1, b) + (dg(a2, b) + dg(a3, b))


def _interleave(chains):
    chains = list(chains)
    while chains:
        alive = []
        for ch in chains:
            try:
                next(ch)
                alive.append(ch)
            except StopIteration:
                pass
        chains = alive


def _merge_masks(n, reverse):
    ii = lax.broadcasted_iota(jnp.int32, (n, n), 0)
    jj = lax.broadcasted_iota(jnp.int32, (n, n), 1)
    tri = (ii < jj) if reverse else (ii > jj)
    out = []
    size = 1
    while size < CHUNK:
        out.append(tri & ((ii // (2 * size)) == (jj // (2 * size))) & ((ii // size) != (jj // size)))
        size *= 2
    return out


def _tri_inverse(a, eye_f, level_masks, out):
    t = eye_f - jnp.where(level_masks[0], a, 0.0)
    for m in level_masks[1:]:
        x = _dot1(jnp.where(m, a, 0.0), t)
        yield
        t = t - _dot1(t, x)
        yield
    out.append(t)


def _sigmoid(x):
    return 1.0 / (1.0 + jnp.exp(-x))


def _softplus(x):
    return jnp.maximum(x, 0.0) + jnp.log1p(jnp.exp(-jnp.abs(x)))


def _shift_down(x, row):
    return jnp.where(row == 0, 0.0, pltpu.roll(x, 1, 0))


def _shift_up(x, row):
    n = x.shape[0]
    return jnp.where(row == n - 1, 0.0, pltpu.roll(x, n - 1, 0))


def _norm_mm_body(x_ref, g_ref, w_ref, *rest, act, precise):
    if precise:
        wp_ref, o_ref, op_ref, xn_ref = rest
    else:
        o_ref, xn_ref = rest

    @pl.when(pl.program_id(1) == 0)
    def _():
        def rows_step(r, _):
            rows = pl.ds(pl.multiple_of(r * NORM_ROWS, NORM_ROWS), NORM_ROWS)
            x = x_ref[rows, :]
            xn = x * lax.rsqrt(jnp.mean(x * x, axis=-1, keepdims=True) + RMS_EPS) * g_ref[...]
            xn_ref[rows, :] = xn.astype(BF16)
            if precise:
                op_ref[rows, :] = _dot3(xn, wp_ref[...])
            return 0

        lax.fori_loop(0, x_ref.shape[0] // NORM_ROWS, rows_step, 0)

    acc = jnp.dot(xn_ref[...], w_ref[...], preferred_element_type=F32)
    if act == "relu2":
        acc = jnp.square(jnp.maximum(acc, 0.0))
    o_ref[...] = acc.astype(o_ref.dtype)


def norm_mm(x, g, w, layer, wp=None, act=None, out_dtype=F32, tm=1024, tn=1024):
    t, d = x.shape
    n = w.shape[2]
    tm = min(tm, t)
    tn = min(tn, n)
    assert t % tm == 0 and n % tn == 0 and tm % NORM_ROWS == 0
    precise = wp is not None
    in_specs = [
        pl.BlockSpec((tm, d), lambda i, j: (i, 0)),
        pl.BlockSpec((1, d), lambda i, j: (0, 0)),
        pl.BlockSpec((None, d, tn), lambda i, j: (layer, 0, j)),
    ]
    out_specs = pl.BlockSpec((tm, tn), lambda i, j: (i, j))
    out_shape = jax.ShapeDtypeStruct((t, n), out_dtype)
    args = [x, g.reshape(1, d), w]
    if precise:
        npc = wp.shape[1]
        in_specs.append(pl.BlockSpec((d, npc), lambda i, j: (0, 0)))
        out_specs = [out_specs, pl.BlockSpec((tm, npc), lambda i, j: (i, 0))]
        out_shape = [out_shape, jax.ShapeDtypeStruct((t, npc), F32)]
        args.append(wp)
    return pl.pallas_call(
        functools.partial(_norm_mm_body, act=act, precise=precise),
        grid=(t // tm, n // tn),
        in_specs=in_specs,
        out_specs=out_specs,
        out_shape=out_shape,
        scratch_shapes=[pltpu.VMEM((tm, d), BF16)],
        compiler_params=_params("parallel", "arbitrary"),
        name="norm_mm",
    )(*args)


def _mm_res_body(a_ref, w_ref, r_ref, o_ref):
    acc = jnp.dot(a_ref[...], w_ref[...], preferred_element_type=F32)

    @pl.when(pl.program_id(2) == 0)
    def _():
        o_ref[...] = r_ref[...] + acc

    @pl.when(pl.program_id(2) > 0)
    def _():
        o_ref[...] += acc


def mm_res(a, w, layer, res, tm=1024, tn=1024, tk=2048):
    t, k = a.shape
    n = w.shape[2]
    tm, tn, tk = min(tm, t), min(tn, n), min(tk, k)
    assert t % tm == 0 and n % tn == 0 and k % tk == 0
    return pl.pallas_call(
        _mm_res_body,
        grid=(t // tm, n // tn, k // tk),
        in_specs=[
            pl.BlockSpec((tm, tk), lambda i, j, kk: (i, kk)),
            pl.BlockSpec((None, tk, tn), lambda i, j, kk: (layer, kk, j)),
            pl.BlockSpec((tm, tn), lambda i, j, kk: (i, j)),
        ],
        out_specs=pl.BlockSpec((tm, tn), lambda i, j, kk: (i, j)),
        out_shape=jax.ShapeDtypeStruct((t, n), F32),
        compiler_params=_params("parallel", "parallel", "arbitrary"),
        name="mm_res",
    )(a, w, res)


def _xattn_body(x_ref, g_ref, wq_ref, kv_ref, wo_ref, qn_ref, kn_ref, o_ref):
    x = x_ref[...]
    xn = x * lax.rsqrt(jnp.mean(x * x, axis=-1, keepdims=True) + RMS_EPS) * g_ref[...]
    q = jnp.dot(xn.astype(BF16), wq_ref[...], preferred_element_type=F32)
    kv = kv_ref[0]
    outs = []
    for h in range(XA_HEADS):
        sl = slice(h * XA_HEAD_DIM, (h + 1) * XA_HEAD_DIM)
        qh = q[:, sl]
        qh = qh * lax.rsqrt(jnp.mean(qh * qh, axis=-1, keepdims=True) + RMS_EPS) * qn_ref[...]
        kh = kv[:, sl]
        kh = kh * lax.rsqrt(jnp.mean(kh * kh, axis=-1, keepdims=True) + RMS_EPS) * kn_ref[...]
        vh = kv[:, XA_WIDTH + h * XA_HEAD_DIM:XA_WIDTH + (h + 1) * XA_HEAD_DIM]
        logits = _dot1(qh, kh, _NT) * (XA_HEAD_DIM ** -0.5)
        m = jnp.max(logits, axis=-1, keepdims=True)
        p = jnp.exp(logits - m)
        s = jnp.sum(p, axis=-1, keepdims=True)
        outs.append(_dot1(p, vh) / s)
    o = jnp.concatenate(outs, axis=-1).astype(BF16)
    o_ref[...] = x + jnp.dot(o, wo_ref[...], preferred_element_type=F32)


def xattn(x, g, wq, kv, wo, layer, qn, kn, batch, ts=512):
    t, d = x.shape
    s = t // batch
    ts = min(ts, s)
    nst = s // ts
    m = kv.shape[1]
    return pl.pallas_call(
        _xattn_body,
        grid=(batch, nst),
        in_specs=[
            pl.BlockSpec((ts, d), lambda b, i: (b * nst + i, 0)),
            pl.BlockSpec((1, d), lambda b, i: (0, 0)),
            pl.BlockSpec((None, d, XA_WIDTH), lambda b, i: (layer, 0, 0)),
            pl.BlockSpec((1, m, 2 * XA_WIDTH), lambda b, i: (b, 0, 0)),
            pl.BlockSpec((None, XA_WIDTH, d), lambda b, i: (layer, 0, 0)),
            pl.BlockSpec((1, XA_HEAD_DIM), lambda b, i: (0, 0)),
            pl.BlockSpec((1, XA_HEAD_DIM), lambda b, i: (0, 0)),
        ],
        out_specs=pl.BlockSpec((ts, d), lambda b, i: (b * nst + i, 0)),
        out_shape=jax.ShapeDtypeStruct((t, d), F32),
        compiler_params=_params("parallel", "parallel"),
        name="xattn",
    )(x, g.reshape(1, d), wq, kv, wo, qn.reshape(1, -1), kn.reshape(1, -1))


def _tri_masks(n, blk):
    ii = lax.broadcasted_iota(jnp.int32, (n, n), 0)
    jj = lax.broadcasted_iota(jnp.int32, (n, n), 1)
    same = (ii // blk) == (jj // blk) if n != blk else None

    def m(c):
        return c if same is None else (c & same)

    return {
        False: (m(ii >= jj), m(ii > jj)),
        True: (m(ii <= jj), m(ii < jj)),
        "eye": ii == jj,
    }


def _seg_cumsum(x, pos, reverse):
    n = x.shape[0]
    sh = 1
    while sh < CHUNK:
        if reverse:
            x = x + jnp.where(pos < CHUNK - sh, pltpu.roll(x, n - sh, 0), 0.0)
        else:
            x = x + jnp.where(pos >= sh, pltpu.roll(x, sh, 0), 0.0)
        sh *= 2
    return x


def _dn_body(q_ref, k_ref, v_ref, gate_ref, ab_ref, cq_ref, ck_ref, cv_ref, alog_ref, dt_ref, gain_ref,
             y_ref, qs_ref, ks_ref, vs_ref, kt_ref, bg_ref, gt_ref, of_ref, ob_ref,
             u_ref, w_ref, qg_ref, qk_ref, kdt_ref, eg_ref):
    s = q_ref.shape[0]
    h = pl.program_id(1)
    row = lax.broadcasted_iota(jnp.int32, (s, 1), 0)
    c = CHUNK

    def conv_silu(x_ref, cw_ref):
        x = x_ref[...]
        w = cw_ref[...]
        y = _shift_down(x, row) * w[0:1] + x * w[1:2] + _shift_up(x, row) * w[2:3]
        return y * _sigmoid(y)

    q = conv_silu(q_ref, cq_ref)
    q = q * lax.rsqrt(jnp.sum(q * q, axis=-1, keepdims=True) + L2_EPS) * (DN_HEAD_DIM ** -0.5)
    qs_ref[...] = q
    k = conv_silu(k_ref, ck_ref)
    k = k * lax.rsqrt(jnp.sum(k * k, axis=-1, keepdims=True) + L2_EPS)
    ks_ref[...] = k
    kt_ref[...] = k.T
    vs_ref[...] = conv_silu(v_ref, cv_ref)

    ab = ab_ref[...]
    beta_all = _sigmoid(ab)
    g_all = -jnp.exp(alog_ref[...]) * _softplus(ab + dt_ref[...])
    lane = lax.broadcasted_iota(jnp.int32, (1, 4 * DN_HEADS), 1)

    def pick(t, idx):
        return jnp.sum(jnp.where(lane == idx, t, 0.0), axis=-1, keepdims=True)

    pos = row % c
    lane128 = lax.broadcasted_iota(jnp.int32, (1, LANES), 1)
    cols = (pick(beta_all, h), pick(beta_all, DN_HEADS + h),
            _seg_cumsum(pick(g_all, 2 * DN_HEADS + h), pos, False),
            _seg_cumsum(pick(g_all, 3 * DN_HEADS + h), pos, True))
    bg = jnp.zeros((s, LANES), F32)
    for i, col in enumerate(cols):
        bg = jnp.where(lane128 == i, col, bg)
    bg_ref[...] = bg
    gt_ref[...] = bg.T[0:8, :]

    masks = _tri_masks(PAIR, c)
    merge = {rev: _merge_masks(PAIR, rev) for rev in (False, True)}
    eye_f = jnp.where(masks["eye"], 1.0, 0.0)
    npair = s // PAIR
    group = min(DN_PREP_PAIRS, npair)

    def prep(pidx, d):
        reverse = d == 1
        rows = pl.ds(pl.multiple_of(pidx * PAIR, PAIR), PAIR)
        qp, kp, vp = qs_ref[rows, :], ks_ref[rows, :], vs_ref[rows, :]
        bgp = bg_ref[rows, :]
        beta = bgp[:, d:d + 1]
        gcol = bgp[:, 2 + d:3 + d]
        grow = gt_ref[2 + d:3 + d, rows]
        incl, strict = masks[reverse]
        decay = jnp.where(incl, jnp.exp(jnp.where(incl, gcol - grow, 0.0)), 0.0)
        kb = kp * beta
        kq = _dot1(jnp.concatenate([kb, qp], axis=0), kp, _NT)
        yield
        qk_ref[d, rows, :] = jnp.where(incl, kq[PAIR:] * decay, 0.0).astype(BF16)
        t_inv = []
        yield from _tri_inverse(jnp.where(strict, kq[:PAIR] * decay, 0.0), eye_f, merge[reverse], t_inv)
        egc = jnp.exp(gcol)
        uw = _dot1(t_inv[0], jnp.concatenate([vp * beta, kb * egc], axis=1))
        yield
        u_ref[d, rows, :] = uw[:, :DN_HEAD_DIM]
        w_ref[d, rows, :] = uw[:, DN_HEAD_DIM:].astype(BF16)
        qg_ref[d, rows, :] = (qp * egc).astype(BF16)
        last0, last1 = (0, c) if reverse else (c - 1, PAIR - 1)
        glast = jnp.where(lane128 < c, grow[:, last0:last0 + 1], grow[:, last1:last1 + 1])
        kdt_ref[d, :, rows] = (kt_ref[:, rows] * jnp.exp(glast - grow)).astype(BF16)
        eg_ref[d, :, rows] = jnp.broadcast_to(jnp.exp(glast), (8, PAIR))

    def prep_body(i, _):
        _interleave(prep(i * group + j, d) for j in range(group) for d in range(2))
        return 0

    lax.fori_loop(0, npair // group, prep_body, 0)

    def seq(states, pidx, d):
        pair_rows = pl.ds(pl.multiple_of(pidx * PAIR, PAIR), PAIR)
        kdt = kdt_ref[d, :, pair_rows]
        eg = eg_ref[d, 0:1, pair_rows]
        for half in ((1, 0) if d else (0, 1)):
            r0 = half * c
            rows = pl.ds(pl.multiple_of(pidx * PAIR + r0, c), c)
            ws = _dot1(jnp.concatenate([w_ref[d, rows, :], qg_ref[d, rows, :]], axis=0), states[d])
            yield
            v_new = u_ref[d, rows, :] - ws[:c]
            o = ws[c:] + _dot1(qk_ref[d, rows, r0:r0 + c], v_new)
            if d:
                ob_ref[rows, :] = o
            else:
                of_ref[rows, :] = o
            states[d] = states[d] * eg[:, r0:r0 + 1] + _dot1(kdt[:, r0:r0 + c], v_new)
            yield

    def seq_body(p, carry):
        states = list(carry)
        _interleave([seq(states, p, 0), seq(states, npair - 1 - p, 1)])
        return tuple(states)

    zero = jnp.zeros((DN_HEAD_DIM, DN_HEAD_DIM), F32)
    lax.fori_loop(0, npair, seq_body, (zero, zero))

    o = of_ref[...] + ob_ref[...]
    o = o * lax.rsqrt(jnp.mean(o * o, axis=-1, keepdims=True) + RMS_EPS) * gain_ref[...]
    gate = gate_ref[...]
    y_ref[...] = (o * (gate * _sigmoid(gate))).astype(y_ref.dtype)


def deltanet(p, ab, conv_t, alog32, dt32, gain, batch):
    t = p.shape[0]
    s = t // batch
    assert s % PAIR == 0 and (s // PAIR) % min(DN_PREP_PAIRS, s // PAIR) == 0
    hd = DN_HEAD_DIM
    col = lambda off: pl.BlockSpec((s, hd), lambda b, h: (b, off + h))
    cw = lambda off: pl.BlockSpec((3, hd), lambda b, h: (0, off + h))
    small = lambda n: pl.BlockSpec((1, n), lambda b, h: (0, 0))
    big = pltpu.VMEM((s, hd), F32)
    half = pltpu.VMEM((2, s, hd), BF16)
    return pl.pallas_call(
        _dn_body,
        grid=(batch, DN_HEADS),
        in_specs=[col(0), col(DN_HEADS), col(2 * DN_HEADS), col(3 * DN_HEADS),
                  pl.BlockSpec((s, 4 * DN_HEADS), lambda b, h: (b, 0)),
                  cw(0), cw(DN_HEADS), cw(2 * DN_HEADS),
                  small(4 * DN_HEADS), small(4 * DN_HEADS), small(hd)],
        out_specs=pl.BlockSpec((s, hd), lambda b, h: (b, h)),
        out_shape=jax.ShapeDtypeStruct((t, DN_WIDTH), BF16),
        scratch_shapes=[big, big, big, pltpu.VMEM((hd, s), F32), big, pltpu.VMEM((8, s), F32), big, big,
                        pltpu.VMEM((2, s, hd), F32), half, half, half, pltpu.VMEM((2, hd, s), BF16),
                        pltpu.VMEM((2, 8, s), F32)],
        compiler_params=_params("parallel", "parallel"),
        name="deltanet",
    )(p, p, p, p, ab, conv_t, conv_t, conv_t, alog32, dt32, gain.reshape(1, hd))


def _sconv_body(b_ref, c_ref, u_ref, w_ref, y_ref):
    s = b_ref.shape[0]
    row = lax.broadcasted_iota(jnp.int32, (s, 1), 0)
    cu = c_ref[...] * u_ref[...]
    w = w_ref[...]
    y = _shift_down(cu, row) * w[0:1] + cu * w[1:2] + _shift_up(cu, row) * w[2:3]
    y_ref[...] = (b_ref[...] * y).astype(y_ref.dtype)


def short_conv(p, conv_t, col0, batch, tc=256):
    t = p.shape[0]
    s = t // batch
    nct = SC_WIDTH // tc
    base = col0 // tc
    col = lambda off: pl.BlockSpec((s, tc), lambda b, c: (b, base + off * nct + c))
    return pl.pallas_call(
        _sconv_body,
        grid=(batch, nct),
        in_specs=[col(0), col(1), col(2), pl.BlockSpec((3, tc), lambda b, c: (0, c))],
        out_specs=pl.BlockSpec((s, tc), lambda b, c: (b, c)),
        out_shape=jax.ShapeDtypeStruct((t, SC_WIDTH), BF16),
        compiler_params=_params("parallel", "parallel"),
        name="short_conv",
    )(p, p, p, conv_t)


def _dsa_body(*refs, seq):
    q_refs = refs[0:3]
    k_refs = refs[3:6]
    v_refs = refs[6:9]
    qn_ref, kn_ref, bias_ref, y_ref, qs_ref, ks_ref, kbuf_ref, vbuf_ref, og_ref, lse_ref = refs[9:]
    s = seq
    qb = DSA_QBLK
    side = DSA_SIDE
    kj = lax.broadcasted_iota(jnp.int32, (1, qb + 2 * side), 1)

    for gi, (_, dil) in enumerate(DSA_PATTERNS):
        sub = s // dil
        nblk = sub // qb
        q = q_refs[gi][...]
        qs_ref[...] = (q * lax.rsqrt(jnp.mean(q * q, axis=-1, keepdims=True) + RMS_EPS) * qn_ref[...]
                       * (DSA_HEAD_DIM ** -0.5))
        k = k_refs[gi][...]
        ks_ref[...] = k * lax.rsqrt(jnp.mean(k * k, axis=-1, keepdims=True) + RMS_EPS) * kn_ref[...]
        zpad = jnp.zeros((side, DSA_HEAD_DIM), F32)
        kbuf_ref[0:side, :] = zpad
        vbuf_ref[0:side, :] = zpad
        kbuf_ref[side + sub:2 * side + sub, :] = zpad
        vbuf_ref[side + sub:2 * side + sub, :] = zpad
        bias = bias_ref[0, gi]

        def residue(r, _, gi=gi, dil=dil, sub=sub, nblk=nblk, bias=bias):
            kbuf_ref[side:side + sub, :] = ks_ref[pl.ds(r, sub, stride=dil), :]
            vbuf_ref[side:side + sub, :] = v_refs[gi][pl.ds(r, sub, stride=dil), :]

            def block(n, _):
                rows = pl.ds(r + n * (qb * dil), qb, stride=dil)
                qblk = qs_ref[rows, :]
                win = pl.ds(pl.multiple_of(n * qb, qb), qb + 2 * side)
                logits = _dot1(qblk, kbuf_ref[win, :], _NT) + bias
                pos = n * qb - side + kj
                logits = jnp.where((pos >= 0) & (pos < sub), logits, NEG_INF)
                m = jnp.max(logits, axis=-1, keepdims=True)
                p = jnp.exp(logits - m)
                ssum = jnp.sum(p, axis=-1, keepdims=True)
                og_ref[gi, rows, :] = _dot1(p, vbuf_ref[win, :]) / ssum
                lse_ref[gi, rows, :] = jnp.broadcast_to(m + jnp.log(ssum), (qb, DSA_HEAD_DIM))
                return 0

            lax.fori_loop(0, nblk, block, 0)
            return 0

        lax.fori_loop(0, dil, residue, 0)

    lse = [lse_ref[gi] for gi in range(DSA_GROUPS)]
    mx = jnp.maximum(jnp.maximum(lse[0], lse[1]), lse[2])
    ws = [jnp.exp(l - mx) for l in lse]
    num = ws[0] * og_ref[0] + ws[1] * og_ref[1] + ws[2] * og_ref[2]
    y_ref[...] = (num / (ws[0] + ws[1] + ws[2])).astype(y_ref.dtype)


def dilated_attention(p, qn, kn, bias_tab, batch):
    t = p.shape[0]
    s = t // batch
    hd = DSA_HEAD_DIM
    nh = DSA_HEADS

    def col(part, gi):
        return pl.BlockSpec((s, hd), lambda b, j: (b, part * nh + gi * DSA_HPG + j))

    in_specs = [col(part, gi) for part in range(3) for gi in range(DSA_GROUPS)]
    in_specs += [pl.BlockSpec((1, hd), lambda b, j: (0, 0)), pl.BlockSpec((1, hd), lambda b, j: (0, 0)),
                 pl.BlockSpec((1, DSA_GROUPS, DSA_QBLK, DSA_QBLK + 2 * DSA_SIDE), lambda b, j: (j, 0, 0, 0))]
    big = pltpu.VMEM((s, hd), F32)
    pad = pltpu.VMEM((s + 2 * DSA_SIDE, hd), F32)
    grp = pltpu.VMEM((DSA_GROUPS, s, hd), F32)
    return pl.pallas_call(
        functools.partial(_dsa_body, seq=s),
        grid=(batch, DSA_HPG),
        in_specs=in_specs,
        out_specs=pl.BlockSpec((s, hd), lambda b, j: (b, j)),
        out_shape=jax.ShapeDtypeStruct((t, DSA_HPG * hd), BF16),
        scratch_shapes=[big, big, pad, pad, grp, grp],
        compiler_params=_params("parallel", "parallel"),
        name="dilated_attention",
    )(*([p] * 9), qn.reshape(1, hd), kn.reshape(1, hd), bias_tab)


def _t5_bucket(rel):
    half = REL_BUCKETS // 2
    max_exact = half // 2
    n = np.abs(rel)
    scaled = (np.log(np.maximum(n, max_exact).astype(np.float32) / np.float32(max_exact))
              / np.float32(math.log(REL_MAX_DIST / max_exact)))
    large = np.minimum(max_exact + (scaled * np.float32(half - max_exact)).astype(np.int32), half - 1)
    return np.where(rel > 0, half, 0) + np.where(n < max_exact, n, large)


def _dsa_bias_table(rel_bias):
    width = DSA_QBLK + 2 * DSA_SIDE
    tabs = []
    for gi, (_, dil) in enumerate(DSA_PATTERNS):
        offs = np.arange(-DSA_SIDE, DSA_SIDE + 1, dtype=np.int32) * dil
        band = rel_bias[_t5_bucket(offs)][:, gi * DSA_HPG:(gi + 1) * DSA_HPG].astype(F32)
        fill = jnp.full((DSA_QBLK - 1, DSA_HPG), NEG_INF, F32)
        line = jnp.concatenate([fill, band, fill], axis=0)
        tabs.append(jnp.stack([line[DSA_QBLK - 1 - q:DSA_QBLK - 1 - q + width] for q in range(DSA_QBLK)]))
    return jnp.transpose(jnp.stack(tabs), (3, 0, 1, 2))


def _head_block_diag():
    ii = lax.broadcasted_iota(jnp.int32, (LANES, LANES), 0)
    jj = lax.broadcasted_iota(jnp.int32, (LANES, LANES), 1)
    return (ii // RW_HEAD_DIM) == (jj // RW_HEAD_DIM)


def _rw_prep_body(r_ref, k_ref, v_ref, lo_ref, mur_ref, muk_ref, muv_ref, mulo_ref, w0_ref, w2_ref, a0_ref,
                  a2_ref, g2_ref, kk_ref, ka_ref, rk_ref,
                  ro_ref, vo_ref, kko_ref, bon_ref, gate_ref, lwf_ref, lwb_ref, kdf_ref, kdb_ref, bbf_ref, bbb_ref):
    s = r_ref.shape[0]
    row = lax.broadcasted_iota(jnp.int32, (s, 1), 0)

    def mix(t, mu):
        return t + mu * (0.5 * (_shift_down(t, row) + _shift_up(t, row)) - t)

    r = mix(r_ref[...], mur_ref[...])
    kr = mix(k_ref[...], muk_ref[...])
    v = mix(v_ref[...], muv_ref[...])
    lo = mix(lo_ref[...], mulo_ref[...])
    bd = jnp.where(_head_block_diag(), 1.0, 0.0)

    def head_sum(t):
        return _dot_exact_rhs(t, bd)

    kk = kr * kk_ref[...]
    kk = kk * lax.rsqrt(head_sum(kk * kk) + L2_EPS)
    gd = lo[:, 4 * RW_LORA:]
    gate_ref[...] = _dot3(_sigmoid(gd), g2_ref[...])
    ro_ref[...] = r
    vo_ref[...] = v
    kko_ref[...] = kk
    bonus = jnp.zeros_like(r)
    outs = ((lwf_ref, kdf_ref, bbf_ref), (lwb_ref, kdb_ref, bbb_ref))
    for d in range(2):
        wd = lo[:, d * RW_LORA:(d + 1) * RW_LORA]
        ad = lo[:, (2 + d) * RW_LORA:(3 + d) * RW_LORA]
        w_log = -_softplus(-(w0_ref[d:d + 1, :] + _dot3(jnp.tanh(wd), w2_ref[d]))) - 0.5
        a = _sigmoid(a0_ref[d:d + 1, :] + _dot3(ad, a2_ref[d]))
        kd = kr * (1.0 + (a - 1.0) * ka_ref[...])
        lw_ref, kd_ref, bb_ref = outs[d]
        lw_ref[...] = -jnp.exp(w_log)
        kd_ref[...] = kd
        bb_ref[...] = kk * a
        bonus = bonus + head_sum(r * kd * rk_ref[...]) * v
    bon_ref[...] = bonus


def rwkv_prep(p, lora_in, mu, w0, w2, a0, a2, g2, k_k, k_a, r_k, col0, batch):
    t = p.shape[0]
    s = t // batch
    nct = RW_WIDTH // LANES
    base = col0 // LANES
    nlo = RW_LORA_IN
    col = lambda off: pl.BlockSpec((s, LANES), lambda b, c: (b, base + off * nct + c))
    vec = lambda off: pl.BlockSpec((1, LANES), lambda b, c: (0, off * nct + c))
    mu_main = mu[:RW_MAIN].reshape(1, RW_MAIN)
    mu_lo = mu[RW_MAIN:].reshape(1, nlo)
    out_spec = pl.BlockSpec((s, LANES), lambda b, c: (b, c))
    n_out = 11
    return pl.pallas_call(
        _rw_prep_body,
        grid=(batch, nct),
        in_specs=[col(0), col(1), col(2),
                  pl.BlockSpec((s, nlo), lambda b, c: (b, 0)),
                  vec(0), vec(1), vec(2),
                  pl.BlockSpec((1, nlo), lambda b, c: (0, 0)),
                  pl.BlockSpec((2, LANES), lambda b, c: (0, c)),
                  pl.BlockSpec((2, RW_LORA, LANES), lambda b, c: (0, 0, c)),
                  pl.BlockSpec((2, LANES), lambda b, c: (0, c)),
                  pl.BlockSpec((2, RW_LORA, LANES), lambda b, c: (0, 0, c)),
                  pl.BlockSpec((RW_GATE_LORA, LANES), lambda b, c: (0, c)),
                  vec(0), vec(0), vec(0)],
        out_specs=[out_spec] * n_out,
        out_shape=[jax.ShapeDtypeStruct((t, RW_WIDTH), F32)] * n_out,
        compiler_params=_params("parallel", "parallel"),
        name="rwkv_prep",
    )(p, p, p, lora_in, mu_main, mu_main, mu_main, mu_lo, w0, w2, a0, a2, g2,
      k_k.reshape(1, RW_WIDTH), k_a.reshape(1, RW_WIDTH), r_k.reshape(1, RW_WIDTH))


def _rw_scan_body(r_ref, v_ref, kk_ref, bon_ref, gate_ref, lwf_ref, lwb_ref, kdf_ref, kdb_ref, bbf_ref, bbb_ref,
                  lnw_ref, lnb_ref, y_ref, cum_ref, cumt_ref, kdt_ref, bbt_ref, yf_ref, yb_ref,
                  tr_ref, tc_ref, rbk_ref, bkt_ref):
    s = r_ref.shape[0]
    c = CHUNK
    c2 = 2 * c
    row = lax.broadcasted_iota(jnp.int32, (s, 1), 0)
    pos = row % c
    lw_refs, kd_refs, bb_refs = (lwf_ref, lwb_ref), (kdf_ref, kdb_ref), (bbf_ref, bbb_ref)
    for d in range(2):
        cum = _seg_cumsum(lw_refs[d][...], pos, d == 1)
        cum_ref[d] = cum
        cumt_ref[d] = cum.T
        kdt_ref[d] = kd_refs[d][...].T
        bbt_ref[d] = bb_refs[d][...].T

    masks = _tri_masks(c2, c)
    merge = {rev: _merge_masks(c2, rev) for rev in (False, True)}
    eye2_f = jnp.where(masks["eye"], 1.0, 0.0)
    head0 = lax.broadcasted_iota(jnp.int32, (1, LANES), 1) < RW_HEAD_DIM
    head_bd = _head_block_diag()
    npair = s // PAIR
    group = min(RW_PREP_PAIRS, npair)

    def stack(x):
        return jnp.concatenate([jnp.where(head0, x, 0.0), jnp.where(head0, 0.0, x)], axis=0)

    def prep(pidx, d, half):
        reverse = d == 1
        incl2, strict2 = masks[reverse]
        cidx = 2 * pidx + half
        cols = pl.ds(pl.multiple_of(pidx * PAIR, PAIR), PAIR)
        rows = pl.ds(pl.multiple_of(cidx * c, c), c)
        hs = slice(half * c, (half + 1) * c)
        r, v, kk = r_ref[rows, :], v_ref[rows, :], kk_ref[rows, :]
        lw, kd, bb = lw_refs[d][rows, :], kd_refs[d][rows, :], bb_refs[d][rows, :]
        cum = cum_ref[d, rows, :]
        e_neg = jnp.exp(-cum)
        a_s = stack(-kk * jnp.exp(cum - lw))
        r_s = stack(r * jnp.exp(cum))
        ar = jnp.concatenate([a_s, r_s], axis=0)
        bk = jnp.concatenate([stack(bb * e_neg), stack(kd * e_neg)], axis=0)
        g = _dot1(ar, bk, _NT)
        yield
        rbk_ref[d, cidx] = jnp.concatenate([jnp.where(incl2, g[c2:, :c2], 0.0),
                                            jnp.where(incl2, g[c2:, c2:], 0.0)], axis=1).astype(BF16)
        av = _dot1(jnp.where(strict2, g[:c2, c2:], 0.0), stack(v))
        yield
        t_inv = []
        yield from _tri_inverse(-jnp.where(strict2, g[:c2, :c2], 0.0), eye2_f, merge[reverse], t_inv)
        tt = _dot1(t_inv[0], jnp.concatenate([a_s, av], axis=1))
        yield
        tr_ref[d, cidx] = jnp.concatenate([tt[:, :LANES], r_s], axis=0).astype(BF16)
        tc_ref[d, cidx] = tt[:, LANES:].astype(BF16)
        cum_t = cumt_ref[d, :, cols][:, hs]
        last = 0 if reverse else c - 1
        e_out_t = jnp.exp(cum_t[:, last:last + 1] - cum_t)
        bkt_ref[d, cidx] = jnp.concatenate([bbt_ref[d, :, cols][:, hs] * e_out_t,
                                            kdt_ref[d, :, cols][:, hs] * e_out_t], axis=1).astype(BF16)

    def prep_body(i, _):
        _interleave(prep(i * group + j, d, half) for j in range(group) for d in range(2) for half in range(2))
        return 0

    lax.fori_loop(0, npair // group, prep_body, 0)

    def seq(states, pidx, d):
        cum_t2 = cumt_ref[d, :, pl.ds(pl.multiple_of(pidx * PAIR, PAIR), PAIR)]
        for half in ((1, 0) if d else (0, 1)):
            cidx = 2 * pidx + half
            rows = pl.ds(pl.multiple_of(cidx * c, c), c)
            v = v_ref[rows, :]
            x = _dot1(tr_ref[d, cidx], states[d])
            yield
            ps = x[:c2] + tc_ref[d, cidx].astype(F32)
            os_ = x[c2:] + _dot1(rbk_ref[d, cidx], jnp.concatenate([ps, stack(v)], axis=0))
            y = os_[:c] + os_[c:]
            if d:
                yb_ref[rows, :] = y
            else:
                yf_ref[rows, :] = y
            pv = jnp.concatenate([ps[:c] + ps[c:], v], axis=0)
            last = half * c + (0 if d else c - 1)
            e_tot = jnp.exp(cum_t2[:, last:last + 1])
            states[d] = states[d] * e_tot + jnp.where(head_bd, _dot1(bkt_ref[d, cidx], pv), 0.0)
            yield

    def seq_body(p, carry):
        states = list(carry)
        _interleave([seq(states, p, 0), seq(states, npair - 1 - p, 1)])
        return tuple(states)

    zero = jnp.zeros((LANES, LANES), F32)
    lax.fori_loop(0, npair, seq_body, (zero, zero))

    bd = jnp.where(head_bd, 1.0, 0.0)
    y = yf_ref[...] + yb_ref[...]
    mean = _dot_exact_rhs(y, bd) * (1.0 / RW_HEAD_DIM)
    yc = y - mean
    var = _dot_exact_rhs(yc * yc, bd) * (1.0 / RW_HEAD_DIM)
    yn = yc * lax.rsqrt(var + RW_GN_EPS) * lnw_ref[...] + lnb_ref[...]
    y_ref[...] = ((yn + bon_ref[...]) * gate_ref[...]).astype(y_ref.dtype)


def rwkv_scan(prep, ln_w, ln_b, batch):
    t = prep[0].shape[0]
    s = t // batch
    assert s % PAIR == 0 and (s // PAIR) % min(RW_PREP_PAIRS, s // PAIR) == 0
    nct = RW_WIDTH // LANES
    nchunk = s // CHUNK
    blk = pl.BlockSpec((s, LANES), lambda b, c: (b, c))
    vec = pl.BlockSpec((1, LANES), lambda b, c: (0, c))
    tsp = pltpu.VMEM((2, LANES, s), F32)
    big = pltpu.VMEM((s, LANES), F32)
    per_chunk = lambda rows, cols: pltpu.VMEM((2, nchunk, rows, cols), BF16)
    return pl.pallas_call(
        _rw_scan_body,
        grid=(batch, nct),
        in_specs=[blk] * 11 + [vec, vec],
        out_specs=blk,
        out_shape=jax.ShapeDtypeStruct((t, RW_WIDTH), BF16),
        scratch_shapes=[pltpu.VMEM((2, s, LANES), F32), tsp, tsp, tsp, big, big,
                        per_chunk(2 * PAIR, LANES), per_chunk(PAIR, LANES), per_chunk(PAIR, 2 * LANES),
                        per_chunk(PAIR, LANES)],
        compiler_params=_params("parallel", "parallel"),
        name="rwkv_scan",
    )(*prep, ln_w.reshape(1, RW_WIDTH), ln_b.reshape(1, RW_WIDTH))


def even_mixer(x, g, w_main, w_ab, w_out, layer, conv_qkv, a_log, dt_bias, out_gain, conv_sc, batch):
    p, ab = norm_mm(x, g, w_main, layer, wp=w_ab)
    zeros = jnp.zeros((2 * DN_HEADS,), F32)
    alog32 = jnp.concatenate([zeros, a_log.reshape(-1)]).reshape(1, -1)
    dt32 = jnp.concatenate([zeros, dt_bias.reshape(-1)]).reshape(1, -1)
    y_dn = deltanet(p, ab, conv_qkv.T, alog32, dt32, out_gain, batch)
    y_sc = short_conv(p, conv_sc.T, 4 * DN_WIDTH, batch)
    y = jnp.concatenate([y_dn, y_sc], axis=1)
    return mm_res(y, w_out, layer, x)


def odd_mixer(x, g, w_main, w_lora, w_out, layer, qn, kn, bias_tab, mu, w0, w2, a0, a2, g2, k_k, k_a, r_k, ln_w, ln_b,
              batch):
    p, lora_in = norm_mm(x, g, w_main, layer, wp=w_lora)
    y_c = dilated_attention(p, qn, kn, bias_tab, batch)
    prep = rwkv_prep(p, lora_in, mu, w0, w2, a0, a2, g2, k_k, k_a, r_k, 3 * DSA_QKV, batch)
    y_d = rwkv_scan(prep, ln_w, ln_b, batch)
    y = jnp.concatenate([y_c, y_d], axis=1)
    return mm_res(y, w_out, layer, x)


def kernel(x, mem, rel_bias, norm_mix, norm_xattn, norm_mem, norm_ffn, xa_wq, xa_wk, xa_wv, xa_wo, xa_qn, xa_kn, ffn_w1, ffn_w2, ev_w_in, ev_w_out, dn_conv, dn_a_log, dn_dt_bias, dn_norm, sc_conv, od_w_in, od_w_out, ca_qn, ca_kn, rw_mu, rw_w0, rw_w2, rw_a0, rw_a2, rw_g2, rw_k_k, rw_k_a, rw_r_k, rw_ln_w, rw_ln_b):
    batch, seq, d = x.shape
    n_mem = mem.shape[1]
    xf = x.reshape(batch * seq, d)
    memf = mem.reshape(batch * n_mem, d)
    bias_tab = _dsa_bias_table(rel_bias)
    c_ab = 4 * DN_WIDTH
    c_sc = c_ab + 4 * DN_HEADS
    c_lo = 3 * DSA_QKV + RW_MAIN
    ev_main = jnp.concatenate([ev_w_in[:, :, :c_ab], ev_w_in[:, :, c_sc:]], axis=2).astype(BF16)
    od_main = od_w_in[:, :, :c_lo].astype(BF16)
    ev_out, od_out = ev_w_out.astype(BF16), od_w_out.astype(BF16)
    w_q, w_o = xa_wq.astype(BF16), xa_wo.astype(BF16)
    w_kv = jnp.concatenate([xa_wk, xa_wv], axis=2).astype(BF16)
    w_1, w_2 = ffn_w1.astype(BF16), ffn_w2.astype(BF16)
    for layer in range(DEPTH):
        i = layer // 2
        if layer % 2 == 0:
            xf = even_mixer(xf, norm_mix[layer], ev_main, ev_w_in[i, :, c_ab:c_sc], ev_out, i, dn_conv[i], dn_a_log[i],
                            dn_dt_bias[i], dn_norm[i], sc_conv[i], batch)
        else:
            xf = odd_mixer(xf, norm_mix[layer], od_main, od_w_in[i, :, c_lo:], od_out, i, ca_qn[i], ca_kn[i], bias_tab,
                           rw_mu[i], rw_w0[i], rw_w2[i], rw_a0[i], rw_a2[i], rw_g2[i], rw_k_k[i], rw_k_a[i], rw_r_k[i],
                           rw_ln_w[i], rw_ln_b[i], batch)
        kv = norm_mm(memf, norm_mem[layer], w_kv, layer).reshape(batch, n_mem, 2 * XA_WIDTH)
        xf = xattn(xf, norm_xattn[layer], w_q, kv, w_o, layer, xa_qn[layer], xa_kn[layer], batch)
        h1 = norm_mm(xf, norm_ffn[layer], w_1, layer, act="relu2", out_dtype=BF16)
        xf = mm_res(h1, w_2, layer, xf)
    return xf.reshape(batch, seq, d)
```

```python
import functools
import math

import jax
import jax.numpy as jnp
import numpy as np
from jax import lax
from jax.experimental import pallas as pl
from jax.experimental.pallas import tpu as pltpu

F32 = jnp.float32
BF16 = jnp.bfloat16

D_MODEL = 2048
DEPTH = 4
RMS_EPS = 1e-6
L2_EPS = 1e-6

DN_HEADS = 8
DN_HEAD_DIM = 128
DN_WIDTH = DN_HEADS * DN_HEAD_DIM
SC_WIDTH = D_MODEL - DN_WIDTH
CHUNK = 64
PAIR = 2 * CHUNK
DN_PREP_PAIRS = 4
RW_PREP_PAIRS = 2

DSA_PATTERNS = ((128, 1), (512, 4), (2048, 16))
DSA_GROUPS = len(DSA_PATTERNS)
DSA_HPG = 4
DSA_HEAD_DIM = 128
DSA_HEADS = DSA_GROUPS * DSA_HPG
DSA_QKV = DSA_HEADS * DSA_HEAD_DIM
DSA_SIDE = 64
DSA_QBLK = 128
DSA_PAD = DSA_SIDE * max(d for _, d in DSA_PATTERNS)
DSA_INTERLEAVE = 4
REL_BUCKETS = 32
REL_MAX_DIST = 1024
NEG_INF = -1e30

RW_HEADS = 8
RW_HEAD_DIM = 64
RW_WIDTH = RW_HEADS * RW_HEAD_DIM
RW_LORA = 64
RW_GATE_LORA = 128
RW_MAIN = 3 * RW_WIDTH
RW_LORA_IN = 4 * RW_LORA + RW_GATE_LORA
RW_GN_EPS = 64e-5

XA_HEADS = 4
XA_HEAD_DIM = 128
XA_WIDTH = XA_HEADS * XA_HEAD_DIM

LANES = 128
NORM_ROWS = 256
VMEM_LIMIT_BYTES = 56 * 1024 * 1024


def _params(*sem):
    return pltpu.CompilerParams(dimension_semantics=sem, vmem_limit_bytes=VMEM_LIMIT_BYTES)


_NN = (((1,), (0,)), ((), ()))
_NT = (((1,), (1,)), ((), ()))


def _dot1(a, b, dims=_NN):
    return lax.dot_general(a.astype(BF16), b.astype(BF16), dims, preferred_element_type=F32)


def _split2(a):
    hi = a.astype(BF16)
    lo = (a - hi.astype(F32)).astype(BF16)
    return hi, lo


def _dot3(a, b, dims=_NN):
    ah, al = _split2(a)
    bh, bl = _split2(b)
    dg = functools.partial(lax.dot_general, dimension_numbers=dims, preferred_element_type=F32)
    return dg(ah, bh) + (dg(al, bh) + dg(ah, bl))


def _dot_exact_lhs(a01, b):
    a = a01.astype(BF16)
    b1 = b.astype(BF16)
    r1 = b - b1.astype(F32)
    b2 = r1.astype(BF16)
    b3 = (r1 - b2.astype(F32)).astype(BF16)
    dg = functools.partial(lax.dot_general, dimension_numbers=_NN, preferred_element_type=F32)
    return dg(a, b1) + (dg(a, b2) + dg(a, b3))


def _dot_exact_rhs(a, b01):
    b = b01.astype(BF16)
    a1 = a.astype(BF16)
    r1 = a - a1.astype(F32)
    a2 = r1.astype(BF16)
    a3 = (r1 - a2.astype(F32)).astype(BF16)
    dg = functools.partial(lax.dot_general, dimension_numbers=_NN, preferred_element_type=F32)
    return dg(a1, b) + (dg(a2, b) + dg(a3, b))


def _interleave(chains):
    chains = list(chains)
    while chains:
        alive = []
        for ch in chains:
            try:
                next(ch)
                alive.append(ch)
            except StopIteration:
                pass
        chains = alive


def _merge_masks(n, reverse):
    ii = lax.broadcasted_iota(jnp.int32, (n, n), 0)
    jj = lax.broadcasted_iota(jnp.int32, (n, n), 1)
    tri = (ii < jj) if reverse else (ii > jj)
    out = []
    size = 1
    while size < CHUNK:
        out.append(tri & ((ii // (2 * size)) == (jj // (2 * size))) & ((ii // size) != (jj // size)))
        size *= 2
    return out


def _tri_inverse(a, eye_f, level_masks, out):
    t = eye_f - jnp.where(level_masks[0], a, 0.0)
    for m in level_masks[1:]:
        x = _dot1(jnp.where(m, a, 0.0), t)
        yield
        t = t - _dot1(t, x)
        yield
    out.append(t)


def _sigmoid(x):
    return 1.0 / (1.0 + jnp.exp(-x))


def _softplus(x):
    return jnp.maximum(x, 0.0) + jnp.log1p(jnp.exp(-jnp.abs(x)))


def _shift_down(x, row):
    return jnp.where(row == 0, 0.0, pltpu.roll(x, 1, 0))


def _shift_up(x, row):
    n = x.shape[0]
    return jnp.where(row == n - 1, 0.0, pltpu.roll(x, n - 1, 0))


def _norm_mm_body(x_ref, g_ref, w_ref, *rest, act, precise):
    if precise:
        wp_ref, o_ref, op_ref, xn_ref = rest
    else:
        o_ref, xn_ref = rest

    @pl.when(pl.program_id(1) == 0)
    def _():
        def rows_step(r, _):
            rows = pl.ds(pl.multiple_of(r * NORM_ROWS, NORM_ROWS), NORM_ROWS)
            x = x_ref[rows, :]
            xn = x * lax.rsqrt(jnp.mean(x * x, axis=-1, keepdims=True) + RMS_EPS) * g_ref[...]
            xn_ref[rows, :] = xn.astype(BF16)
            if precise:
                op_ref[rows, :] = _dot3(xn, wp_ref[...])
            return 0

        lax.fori_loop(0, x_ref.shape[0] // NORM_ROWS, rows_step, 0)

    acc = jnp.dot(xn_ref[...], w_ref[...], preferred_element_type=F32)
    if act == "relu2":
        acc = jnp.square(jnp.maximum(acc, 0.0))
    o_ref[...] = acc.astype(o_ref.dtype)


def norm_mm(x, g, w, layer, wp=None, act=None, out_dtype=F32, tm=1024, tn=1024):
    t, d = x.shape
    n = w.shape[2]
    tm = min(tm, t)
    tn = min(tn, n)
    assert t % tm == 0 and n % tn == 0 and tm % NORM_ROWS == 0
    precise = wp is not None
    in_specs = [
        pl.BlockSpec((tm, d), lambda i, j: (i, 0)),
        pl.BlockSpec((1, d), lambda i, j: (0, 0)),
        pl.BlockSpec((None, d, tn), lambda i, j: (layer, 0, j)),
    ]
    out_specs = pl.BlockSpec((tm, tn), lambda i, j: (i, j))
    out_shape = jax.ShapeDtypeStruct((t, n), out_dtype)
    args = [x, g.reshape(1, d), w]
    if precise:
        npc = wp.shape[1]
        in_specs.append(pl.BlockSpec((d, npc), lambda i, j: (0, 0)))
        out_specs = [out_specs, pl.BlockSpec((tm, npc), lambda i, j: (i, 0))]
        out_shape = [out_shape, jax.ShapeDtypeStruct((t, npc), F32)]
        args.append(wp)
    return pl.pallas_call(
        functools.partial(_norm_mm_body, act=act, precise=precise),
        grid=(t // tm, n // tn),
        in_specs=in_specs,
        out_specs=out_specs,
        out_shape=out_shape,
        scratch_shapes=[pltpu.VMEM((tm, d), BF16)],
        compiler_params=_params("parallel", "arbitrary"),
        name="norm_mm",
    )(*args)


def _mm_res_body(a_ref, w_ref, r_ref, o_ref):
    acc = jnp.dot(a_ref[...], w_ref[...], preferred_element_type=F32)

    @pl.when(pl.program_id(2) == 0)
    def _():
        o_ref[...] = r_ref[...] + acc

    @pl.when(pl.program_id(2) > 0)
    def _():
        o_ref[...] += acc


def mm_res(a, w, layer, res, tm=1024, tn=1024, tk=2048):
    t, k = a.shape
    n = w.shape[2]
    tm, tn, tk = min(tm, t), min(tn, n), min(tk, k)
    assert t % tm == 0 and n % tn == 0 and k % tk == 0
    return pl.pallas_call(
        _mm_res_body,
        grid=(t // tm, n // tn, k // tk),
        in_specs=[
            pl.BlockSpec((tm, tk), lambda i, j, kk: (i, kk)),
            pl.BlockSpec((None, tk, tn), lambda i, j, kk: (layer, kk, j)),
            pl.BlockSpec((tm, tn), lambda i, j, kk: (i, j)),
        ],
        out_specs=pl.BlockSpec((tm, tn), lambda i, j, kk: (i, j)),
        out_shape=jax.ShapeDtypeStruct((t, n), F32),
        compiler_params=_params("parallel", "parallel", "arbitrary"),
        name="mm_res",
    )(a, w, res)


def _xattn_body(x_ref, g_ref, wq_ref, kv_ref, wo_ref, qn_ref, kn_ref, o_ref):
    x = x_ref[...]
    xn = x * lax.rsqrt(jnp.mean(x * x, axis=-1, keepdims=True) + RMS_EPS) * g_ref[...]
    q = jnp.dot(xn.astype(BF16), wq_ref[...], preferred_element_type=F32)
    kv = kv_ref[0]
    outs = []
    for h in range(XA_HEADS):
        sl = slice(h * XA_HEAD_DIM, (h + 1) * XA_HEAD_DIM)
        qh = q[:, sl]
        qh = qh * lax.rsqrt(jnp.mean(qh * qh, axis=-1, keepdims=True) + RMS_EPS) * qn_ref[...]
        kh = kv[:, sl]
        kh = kh * lax.rsqrt(jnp.mean(kh * kh, axis=-1, keepdims=True) + RMS_EPS) * kn_ref[...]
        vh = kv[:, XA_WIDTH + h * XA_HEAD_DIM:XA_WIDTH + (h + 1) * XA_HEAD_DIM]
        logits = _dot1(qh, kh, _NT) * (XA_HEAD_DIM ** -0.5)
        m = jnp.max(logits, axis=-1, keepdims=True)
        p = jnp.exp(logits - m)
        s = jnp.sum(p, axis=-1, keepdims=True)
        outs.append(_dot1(p, vh) / s)
    o = jnp.concatenate(outs, axis=-1).astype(BF16)
    o_ref[...] = x + jnp.dot(o, wo_ref[...], preferred_element_type=F32)


def xattn(x, g, wq, kv, wo, layer, qn, kn, batch, ts=512):
    t, d = x.shape
    s = t // batch
    ts = min(ts, s)
    nst = s // ts
    m = kv.shape[1]
    return pl.pallas_call(
        _xattn_body,
        grid=(batch, nst),
        in_specs=[
            pl.BlockSpec((ts, d), lambda b, i: (b * nst + i, 0)),
            pl.BlockSpec((1, d), lambda b, i: (0, 0)),
            pl.BlockSpec((None, d, XA_WIDTH), lambda b, i: (layer, 0, 0)),
            pl.BlockSpec((1, m, 2 * XA_WIDTH), lambda b, i: (b, 0, 0)),
            pl.BlockSpec((None, XA_WIDTH, d), lambda b, i: (layer, 0, 0)),
            pl.BlockSpec((1, XA_HEAD_DIM), lambda b, i: (0, 0)),
            pl.BlockSpec((1, XA_HEAD_DIM), lambda b, i: (0, 0)),
        ],
        out_specs=pl.BlockSpec((ts, d), lambda b, i: (b * nst + i, 0)),
        out_shape=jax.ShapeDtypeStruct((t, d), F32),
        compiler_params=_params("parallel", "parallel"),
        name="xattn",
    )(x, g.reshape(1, d), wq, kv, wo, qn.reshape(1, -1), kn.reshape(1, -1))


def _tri_masks(n, blk):
    ii = lax.broadcasted_iota(jnp.int32, (n, n), 0)
    jj = lax.broadcasted_iota(jnp.int32, (n, n), 1)
    same = (ii // blk) == (jj // blk) if n != blk else None

    def m(c):
        return c if same is None else (c & same)

    return {
        False: (m(ii >= jj), m(ii > jj)),
        True: (m(ii <= jj), m(ii < jj)),
        "eye": ii == jj,
    }


def _seg_cumsum(x, pos, reverse):
    n = x.shape[0]
    sh = 1
    while sh < CHUNK:
        if reverse:
            x = x + jnp.where(pos < CHUNK - sh, pltpu.roll(x, n - sh, 0), 0.0)
        else:
            x = x + jnp.where(pos >= sh, pltpu.roll(x, sh, 0), 0.0)
        sh *= 2
    return x


def _dn_body(q_ref, k_ref, v_ref, gate_ref, ab_ref, cq_ref, ck_ref, cv_ref, alog_ref, dt_ref, gain_ref,
             y_ref, qs_ref, ks_ref, vs_ref, kt_ref, bg_ref, gt_ref, of_ref, ob_ref,
             u_ref, w_ref, qg_ref, qk_ref, kdt_ref, eg_ref):
    s = q_ref.shape[0]
    h = pl.program_id(1)
    row = lax.broadcasted_iota(jnp.int32, (s, 1), 0)
    c = CHUNK

    def conv_silu(x_ref, cw_ref):
        x = x_ref[...]
        w = cw_ref[...]
        y = _shift_down(x, row) * w[0:1] + x * w[1:2] + _shift_up(x, row) * w[2:3]
        return y * _sigmoid(y)

    q = conv_silu(q_ref, cq_ref)
    q = q * lax.rsqrt(jnp.sum(q * q, axis=-1, keepdims=True) + L2_EPS) * (DN_HEAD_DIM ** -0.5)
    qs_ref[...] = q
    k = conv_silu(k_ref, ck_ref)
    k = k * lax.rsqrt(jnp.sum(k * k, axis=-1, keepdims=True) + L2_EPS)
    ks_ref[...] = k
    kt_ref[...] = k.T
    vs_ref[...] = conv_silu(v_ref, cv_ref)

    ab = ab_ref[...]
    beta_all = _sigmoid(ab)
    g_all = -jnp.exp(alog_ref[...]) * _softplus(ab + dt_ref[...])
    lane = lax.broadcasted_iota(jnp.int32, (1, 4 * DN_HEADS), 1)

    def pick(t, idx):
        return jnp.sum(jnp.where(lane == idx, t, 0.0), axis=-1, keepdims=True)

    pos = row % c
    lane128 = lax.broadcasted_iota(jnp.int32, (1, LANES), 1)
    cols = (pick(beta_all, h), pick(beta_all, DN_HEADS + h),
            _seg_cumsum(pick(g_all, 2 * DN_HEADS + h), pos, False),
            _seg_cumsum(pick(g_all, 3 * DN_HEADS + h), pos, True))
    bg = jnp.zeros((s, LANES), F32)
    for i, col in enumerate(cols):
        bg = jnp.where(lane128 == i, col, bg)
    bg_ref[...] = bg
    gt_ref[...] = bg.T[0:8, :]

    masks = _tri_masks(PAIR, c)
    merge = {rev: _merge_masks(PAIR, rev) for rev in (False, True)}
    eye_f = jnp.where(masks["eye"], 1.0, 0.0)
    npair = s // PAIR
    group = min(DN_PREP_PAIRS, npair)

    def prep(pidx, d):
        reverse = d == 1
        rows = pl.ds(pl.multiple_of(pidx * PAIR, PAIR), PAIR)
        qp, kp, vp = qs_ref[rows, :], ks_ref[rows, :], vs_ref[rows, :]
        bgp = bg_ref[rows, :]
        beta = bgp[:, d:d + 1]
        gcol = bgp[:, 2 + d:3 + d]
        grow = gt_ref[2 + d:3 + d, rows]
        incl, strict = masks[reverse]
        decay = jnp.where(incl, jnp.exp(jnp.where(incl, gcol - grow, 0.0)), 0.0)
        kb = kp * beta
        kq = _dot1(jnp.concatenate([kb, qp], axis=0), kp, _NT)
        yield
        qk_ref[d, rows, :] = jnp.where(incl, kq[PAIR:] * decay, 0.0).astype(BF16)
        t_inv = []
        yield from _tri_inverse(jnp.where(strict, kq[:PAIR] * decay, 0.0), eye_f, merge[reverse], t_inv)
        egc = jnp.exp(gcol)
        uw = _dot1(t_inv[0], jnp.concatenate([vp * beta, kb * egc], axis=1))
        yield
        u_ref[d, rows, :] = uw[:, :DN_HEAD_DIM]
        w_ref[d, rows, :] = uw[:, DN_HEAD_DIM:].astype(BF16)
        qg_ref[d, rows, :] = (qp * egc).astype(BF16)
        last0, last1 = (0, c) if reverse else (c - 1, PAIR - 1)
        glast = jnp.where(lane128 < c, grow[:, last0:last0 + 1], grow[:, last1:last1 + 1])
        kdt_ref[d, :, rows] = (kt_ref[:, rows] * jnp.exp(glast - grow)).astype(BF16)
        eg_ref[d, :, rows] = jnp.broadcast_to(jnp.exp(glast), (8, PAIR))

    def prep_body(i, _):
        _interleave(prep(i * group + j, d) for j in range(group) for d in range(2))
        return 0

    lax.fori_loop(0, npair // group, prep_body, 0)

    def seq(states, pidx, d):
        pair_rows = pl.ds(pl.multiple_of(pidx * PAIR, PAIR), PAIR)
        kdt = kdt_ref[d, :, pair_rows]
        eg = eg_ref[d, 0:1, pair_rows]
        for half in ((1, 0) if d else (0, 1)):
            r0 = half * c
            rows = pl.ds(pl.multiple_of(pidx * PAIR + r0, c), c)
            ws = _dot1(jnp.concatenate([w_ref[d, rows, :], qg_ref[d, rows, :]], axis=0), states[d])
            yield
            v_new = u_ref[d, rows, :] - ws[:c]
            o = ws[c:] + _dot1(qk_ref[d, rows, r0:r0 + c], v_new)
            if d:
                ob_ref[rows, :] = o
            else:
                of_ref[rows, :] = o
            states[d] = states[d] * eg[:, r0:r0 + 1] + _dot1(kdt[:, r0:r0 + c], v_new)
            yield

    def seq_body(p, carry):
        states = list(carry)
        _interleave([seq(states, p, 0), seq(states, npair - 1 - p, 1)])
        return tuple(states)

    zero = jnp.zeros((DN_HEAD_DIM, DN_HEAD_DIM), F32)
    lax.fori_loop(0, npair, seq_body, (zero, zero))

    o = of_ref[...] + ob_ref[...]
    o = o * lax.rsqrt(jnp.mean(o * o, axis=-1, keepdims=True) + RMS_EPS) * gain_ref[...]
    gate = gate_ref[...]
    y_ref[...] = (o * (gate * _sigmoid(gate))).astype(y_ref.dtype)


def deltanet(p, ab, conv_t, alog32, dt32, gain, batch):
    t = p.shape[0]
    s = t // batch
    assert s % PAIR == 0 and (s // PAIR) % min(DN_PREP_PAIRS, s // PAIR) == 0
    hd = DN_HEAD_DIM
    col = lambda off: pl.BlockSpec((s, hd), lambda b, h: (b, off + h))
    cw = lambda off: pl.BlockSpec((3, hd), lambda b, h: (0, off + h))
    small = lambda n: pl.BlockSpec((1, n), lambda b, h: (0, 0))
    big = pltpu.VMEM((s, hd), F32)
    half = pltpu.VMEM((2, s, hd), BF16)
    return pl.pallas_call(
        _dn_body,
        grid=(batch, DN_HEADS),
        in_specs=[col(0), col(DN_HEADS), col(2 * DN_HEADS), col(3 * DN_HEADS),
                  pl.BlockSpec((s, 4 * DN_HEADS), lambda b, h: (b, 0)),
                  cw(0), cw(DN_HEADS), cw(2 * DN_HEADS),
                  small(4 * DN_HEADS), small(4 * DN_HEADS), small(hd)],
        out_specs=pl.BlockSpec((s, hd), lambda b, h: (b, h)),
        out_shape=jax.ShapeDtypeStruct((t, DN_WIDTH), BF16),
        scratch_shapes=[big, big, big, pltpu.VMEM((hd, s), F32), big, pltpu.VMEM((8, s), F32), big, big,
                        pltpu.VMEM((2, s, hd), F32), half, half, half, pltpu.VMEM((2, hd, s), BF16),
                        pltpu.VMEM((2, 8, s), F32)],
        compiler_params=_params("parallel", "parallel"),
        name="deltanet",
    )(p, p, p, p, ab, conv_t, conv_t, conv_t, alog32, dt32, gain.reshape(1, hd))


def _sconv_body(b_ref, c_ref, u_ref, w_ref, y_ref):
    s = b_ref.shape[0]
    row = lax.broadcasted_iota(jnp.int32, (s, 1), 0)
    cu = c_ref[...] * u_ref[...]
    w = w_ref[...]
    y = _shift_down(cu, row) * w[0:1] + cu * w[1:2] + _shift_up(cu, row) * w[2:3]
    y_ref[...] = (b_ref[...] * y).astype(y_ref.dtype)


def short_conv(p, conv_t, col0, batch, tc=256):
    t = p.shape[0]
    s = t // batch
    nct = SC_WIDTH // tc
    base = col0 // tc
    col = lambda off: pl.BlockSpec((s, tc), lambda b, c: (b, base + off * nct + c))
    return pl.pallas_call(
        _sconv_body,
        grid=(batch, nct),
        in_specs=[col(0), col(1), col(2), pl.BlockSpec((3, tc), lambda b, c: (0, c))],
        out_specs=pl.BlockSpec((s, tc), lambda b, c: (b, c)),
        out_shape=jax.ShapeDtypeStruct((t, SC_WIDTH), BF16),
        compiler_params=_params("parallel", "parallel"),
        name="short_conv",
    )(p, p, p, conv_t)


def _dsa_body(*refs, seq):
    q_refs = refs[0:3]
    k_refs = refs[3:6]
    v_refs = refs[6:9]
    qn_ref, kn_ref, bias_ref, y_ref, qs_ref, kpad_ref, vpad_ref, og_ref, lse_ref = refs[9:]
    s = seq
    qb = DSA_QBLK
    side = DSA_SIDE
    width = qb + 2 * side
    kj = lax.broadcasted_iota(jnp.int32, (1, width), 1)
    zpad = jnp.zeros((DSA_PAD, DSA_HEAD_DIM), F32)
    for ref in (kpad_ref, vpad_ref):
        ref[0:DSA_PAD, :] = zpad
        ref[DSA_PAD + s:2 * DSA_PAD + s, :] = zpad

    for gi, (_, dil) in enumerate(DSA_PATTERNS):
        sub = s // dil
        nblk = sub // qb
        q = q_refs[gi][...]
        qs_ref[...] = (q * lax.rsqrt(jnp.mean(q * q, axis=-1, keepdims=True) + RMS_EPS) * qn_ref[...]
                       * (DSA_HEAD_DIM ** -0.5))
        k = k_refs[gi][...]
        kpad_ref[DSA_PAD:DSA_PAD + s, :] = k * lax.rsqrt(jnp.mean(k * k, axis=-1, keepdims=True) + RMS_EPS) * kn_ref[...]
        vpad_ref[DSA_PAD:DSA_PAD + s, :] = v_refs[gi][...]
        bias = bias_ref[0, gi]

        def block(t, gi=gi, dil=dil, sub=sub, nblk=nblk, bias=bias):
            r = t // nblk
            n = t % nblk
            rows = pl.ds(r + n * (qb * dil), qb, stride=dil)
            win = pl.ds(DSA_PAD + r + (n * qb - side) * dil, width, stride=dil)
            logits = _dot1(qs_ref[rows, :], kpad_ref[win, :], _NT) + bias
            yield
            pos = n * qb - side + kj
            logits = jnp.where((pos >= 0) & (pos < sub), logits, NEG_INF)
            m = jnp.max(logits, axis=-1, keepdims=True)
            yield
            p = jnp.exp(logits - m)
            ssum = jnp.sum(p, axis=-1, keepdims=True)
            o = _dot1(p, vpad_ref[win, :])
            yield
            og_ref[gi, rows, :] = o / ssum
            lse_ref[gi, rows, :] = jnp.broadcast_to(m + jnp.log(ssum), (qb, DSA_HEAD_DIM))

        nblocks = dil * nblk

        def blocks_step(i, _, block=block):
            _interleave(block(i * DSA_INTERLEAVE + u) for u in range(DSA_INTERLEAVE))
            return 0

        lax.fori_loop(0, nblocks // DSA_INTERLEAVE, blocks_step, 0)

    lse = [lse_ref[gi] for gi in range(DSA_GROUPS)]
    mx = jnp.maximum(jnp.maximum(lse[0], lse[1]), lse[2])
    ws = [jnp.exp(l - mx) for l in lse]
    num = ws[0] * og_ref[0] + ws[1] * og_ref[1] + ws[2] * og_ref[2]
    y_ref[...] = (num / (ws[0] + ws[1] + ws[2])).astype(y_ref.dtype)


def dilated_attention(p, qn, kn, bias_tab, batch):
    t = p.shape[0]
    s = t // batch
    assert (s // DSA_QBLK) % DSA_INTERLEAVE == 0
    hd = DSA_HEAD_DIM
    nh = DSA_HEADS

    def col(part, gi):
        return pl.BlockSpec((s, hd), lambda b, j: (b, part * nh + gi * DSA_HPG + j))

    in_specs = [col(part, gi) for part in range(3) for gi in range(DSA_GROUPS)]
    in_specs += [pl.BlockSpec((1, hd), lambda b, j: (0, 0)), pl.BlockSpec((1, hd), lambda b, j: (0, 0)),
                 pl.BlockSpec((1, DSA_GROUPS, DSA_QBLK, DSA_QBLK + 2 * DSA_SIDE), lambda b, j: (j, 0, 0, 0))]
    big = pltpu.VMEM((s, hd), F32)
    pad = pltpu.VMEM((s + 2 * DSA_PAD, hd), F32)
    grp = pltpu.VMEM((DSA_GROUPS, s, hd), F32)
    return pl.pallas_call(
        functools.partial(_dsa_body, seq=s),
        grid=(batch, DSA_HPG),
        in_specs=in_specs,
        out_specs=pl.BlockSpec((s, hd), lambda b, j: (b, j)),
        out_shape=jax.ShapeDtypeStruct((t, DSA_HPG * hd), BF16),
        scratch_shapes=[big, pad, pad, grp, grp],
        compiler_params=_params("parallel", "parallel"),
        name="dilated_attention",
    )(*([p] * 9), qn.reshape(1, hd), kn.reshape(1, hd), bias_tab)


def _t5_bucket(rel):
    half = REL_BUCKETS // 2
    max_exact = half // 2
    n = np.abs(rel)
    scaled = (np.log(np.maximum(n, max_exact).astype(np.float32) / np.float32(max_exact))
              / np.float32(math.log(REL_MAX_DIST / max_exact)))
    large = np.minimum(max_exact + (scaled * np.float32(half - max_exact)).astype(np.int32), half - 1)
    return np.where(rel > 0, half, 0) + np.where(n < max_exact, n, large)


def _dsa_bias_table(rel_bias):
    width = DSA_QBLK + 2 * DSA_SIDE
    tabs = []
    for gi, (_, dil) in enumerate(DSA_PATTERNS):
        offs = np.arange(-DSA_SIDE, DSA_SIDE + 1, dtype=np.int32) * dil
        band = rel_bias[_t5_bucket(offs)][:, gi * DSA_HPG:(gi + 1) * DSA_HPG].astype(F32)
        fill = jnp.full((DSA_QBLK - 1, DSA_HPG), NEG_INF, F32)
        line = jnp.concatenate([fill, band, fill], axis=0)
        tabs.append(jnp.stack([line[DSA_QBLK - 1 - q:DSA_QBLK - 1 - q + width] for q in range(DSA_QBLK)]))
    return jnp.transpose(jnp.stack(tabs), (3, 0, 1, 2))


def _head_block_diag():
    ii = lax.broadcasted_iota(jnp.int32, (LANES, LANES), 0)
    jj = lax.broadcasted_iota(jnp.int32, (LANES, LANES), 1)
    return (ii // RW_HEAD_DIM) == (jj // RW_HEAD_DIM)


def _rw_prep_body(r_ref, k_ref, v_ref, lo_ref, mur_ref, muk_ref, muv_ref, mulo_ref, w0_ref, w2_ref, a0_ref,
                  a2_ref, g2_ref, kk_ref, ka_ref, rk_ref,
                  ro_ref, vo_ref, kko_ref, bon_ref, gate_ref, lwf_ref, lwb_ref, kdf_ref, kdb_ref, bbf_ref, bbb_ref):
    s = r_ref.shape[0]
    row = lax.broadcasted_iota(jnp.int32, (s, 1), 0)

    def mix(t, mu):
        return t + mu * (0.5 * (_shift_down(t, row) + _shift_up(t, row)) - t)

    r = mix(r_ref[...], mur_ref[...])
    kr = mix(k_ref[...], muk_ref[...])
    v = mix(v_ref[...], muv_ref[...])
    lo = mix(lo_ref[...], mulo_ref[...])
    bd = jnp.where(_head_block_diag(), 1.0, 0.0)

    def head_sum(t):
        return _dot_exact_rhs(t, bd)

    kk = kr * kk_ref[...]
    kk = kk * lax.rsqrt(head_sum(kk * kk) + L2_EPS)
    gd = lo[:, 4 * RW_LORA:]
    gate_ref[...] = _dot3(_sigmoid(gd), g2_ref[...])
    ro_ref[...] = r
    vo_ref[...] = v
    kko_ref[...] = kk
    bonus = jnp.zeros_like(r)
    outs = ((lwf_ref, kdf_ref, bbf_ref), (lwb_ref, kdb_ref, bbb_ref))
    for d in range(2):
        wd = lo[:, d * RW_LORA:(d + 1) * RW_LORA]
        ad = lo[:, (2 + d) * RW_LORA:(3 + d) * RW_LORA]
        w_log = -_softplus(-(w0_ref[d:d + 1, :] + _dot3(jnp.tanh(wd), w2_ref[d]))) - 0.5
        a = _sigmoid(a0_ref[d:d + 1, :] + _dot3(ad, a2_ref[d]))
        kd = kr * (1.0 + (a - 1.0) * ka_ref[...])
        lw_ref, kd_ref, bb_ref = outs[d]
        lw_ref[...] = -jnp.exp(w_log)
        kd_ref[...] = kd
        bb_ref[...] = kk * a
        bonus = bonus + head_sum(r * kd * rk_ref[...]) * v
    bon_ref[...] = bonus


def rwkv_prep(p, lora_in, mu, w0, w2, a0, a2, g2, k_k, k_a, r_k, col0, batch):
    t = p.shape[0]
    s = t // batch
    nct = RW_WIDTH // LANES
    base = col0 // LANES
    nlo = RW_LORA_IN
    col = lambda off: pl.BlockSpec((s, LANES), lambda b, c: (b, base + off * nct + c))
    vec = lambda off: pl.BlockSpec((1, LANES), lambda b, c: (0, off * nct + c))
    mu_main = mu[:RW_MAIN].reshape(1, RW_MAIN)
    mu_lo = mu[RW_MAIN:].reshape(1, nlo)
    out_spec = pl.BlockSpec((s, LANES), lambda b, c: (b, c))
    n_out = 11
    return pl.pallas_call(
        _rw_prep_body,
        grid=(batch, nct),
        in_specs=[col(0), col(1), col(2),
                  pl.BlockSpec((s, nlo), lambda b, c: (b, 0)),
                  vec(0), vec(1), vec(2),
                  pl.BlockSpec((1, nlo), lambda b, c: (0, 0)),
                  pl.BlockSpec((2, LANES), lambda b, c: (0, c)),
                  pl.BlockSpec((2, RW_LORA, LANES), lambda b, c: (0, 0, c)),
                  pl.BlockSpec((2, LANES), lambda b, c: (0, c)),
                  pl.BlockSpec((2, RW_LORA, LANES), lambda b, c: (0, 0, c)),
                  pl.BlockSpec((RW_GATE_LORA, LANES), lambda b, c: (0, c)),
                  vec(0), vec(0), vec(0)],
        out_specs=[out_spec] * n_out,
        out_shape=[jax.ShapeDtypeStruct((t, RW_WIDTH), F32)] * n_out,
        compiler_params=_params("parallel", "parallel"),
        name="rwkv_prep",
    )(p, p, p, lora_in, mu_main, mu_main, mu_main, mu_lo, w0, w2, a0, a2, g2,
      k_k.reshape(1, RW_WIDTH), k_a.reshape(1, RW_WIDTH), r_k.reshape(1, RW_WIDTH))


def _rw_scan_body(r_ref, v_ref, kk_ref, bon_ref, gate_ref, lwf_ref, lwb_ref, kdf_ref, kdb_ref, bbf_ref, bbb_ref,
                  lnw_ref, lnb_ref, y_ref, cum_ref, cumt_ref, kdt_ref, bbt_ref, yf_ref, yb_ref,
                  tr_ref, tc_ref, rbk_ref, bkt_ref):
    s = r_ref.shape[0]
    c = CHUNK
    c2 = 2 * c
    row = lax.broadcasted_iota(jnp.int32, (s, 1), 0)
    pos = row % c
    lw_refs, kd_refs, bb_refs = (lwf_ref, lwb_ref), (kdf_ref, kdb_ref), (bbf_ref, bbb_ref)
    for d in range(2):
        cum = _seg_cumsum(lw_refs[d][...], pos, d == 1)
        cum_ref[d] = cum
        cumt_ref[d] = cum.T
        kdt_ref[d] = kd_refs[d][...].T
        bbt_ref[d] = bb_refs[d][...].T

    masks = _tri_masks(c2, c)
    merge = {rev: _merge_masks(c2, rev) for rev in (False, True)}
    eye2_f = jnp.where(masks["eye"], 1.0, 0.0)
    head0 = lax.broadcasted_iota(jnp.int32, (1, LANES), 1) < RW_HEAD_DIM
    head_bd = _head_block_diag()
    npair = s // PAIR
    group = min(RW_PREP_PAIRS, npair)

    def stack(x):
        return jnp.concatenate([jnp.where(head0, x, 0.0), jnp.where(head0, 0.0, x)], axis=0)

    def prep(pidx, d, half):
        reverse = d == 1
        incl2, strict2 = masks[reverse]
        cidx = 2 * pidx + half
        cols = pl.ds(pl.multiple_of(pidx * PAIR, PAIR), PAIR)
        rows = pl.ds(pl.multiple_of(cidx * c, c), c)
        hs = slice(half * c, (half + 1) * c)
        r, v, kk = r_ref[rows, :], v_ref[rows, :], kk_ref[rows, :]
        lw, kd, bb = lw_refs[d][rows, :], kd_refs[d][rows, :], bb_refs[d][rows, :]
        cum = cum_ref[d, rows, :]
        e_neg = jnp.exp(-cum)
        a_s = stack(-kk * jnp.exp(cum - lw))
        r_s = stack(r * jnp.exp(cum))
        ar = jnp.concatenate([a_s, r_s], axis=0)
        bk = jnp.concatenate([stack(bb * e_neg), stack(kd * e_neg)], axis=0)
        g = _dot1(ar, bk, _NT)
        yield
        rbk_ref[d, cidx] = jnp.concatenate([jnp.where(incl2, g[c2:, :c2], 0.0),
                                            jnp.where(incl2, g[c2:, c2:], 0.0)], axis=1).astype(BF16)
        av = _dot1(jnp.where(strict2, g[:c2, c2:], 0.0), stack(v))
        yield
        t_inv = []
        yield from _tri_inverse(-jnp.where(strict2, g[:c2, :c2], 0.0), eye2_f, merge[reverse], t_inv)
        tt = _dot1(t_inv[0], jnp.concatenate([a_s, av], axis=1))
        yield
        tr_ref[d, cidx] = jnp.concatenate([tt[:, :LANES], r_s], axis=0).astype(BF16)
        tc_ref[d, cidx] = tt[:, LANES:].astype(BF16)
        cum_t = cumt_ref[d, :, cols][:, hs]
        last = 0 if reverse else c - 1
        e_out_t = jnp.exp(cum_t[:, last:last + 1] - cum_t)
        bkt_ref[d, cidx] = jnp.concatenate([bbt_ref[d, :, cols][:, hs] * e_out_t,
                                            kdt_ref[d, :, cols][:, hs] * e_out_t], axis=1).astype(BF16)

    def prep_body(i, _):
        _interleave(prep(i * group + j, d, half) for j in range(group) for d in range(2) for half in range(2))
        return 0

    lax.fori_loop(0, npair // group, prep_body, 0)

    def seq(states, pidx, d):
        cum_t2 = cumt_ref[d, :, pl.ds(pl.multiple_of(pidx * PAIR, PAIR), PAIR)]
        for half in ((1, 0) if d else (0, 1)):
            cidx = 2 * pidx + half
            rows = pl.ds(pl.multiple_of(cidx * c, c), c)
            v = v_ref[rows, :]
            x = _dot1(tr_ref[d, cidx], states[d])
            yield
            ps = x[:c2] + tc_ref[d, cidx].astype(F32)
            os_ = x[c2:] + _dot1(rbk_ref[d, cidx], jnp.concatenate([ps, stack(v)], axis=0))
            y = os_[:c] + os_[c:]
            if d:
                yb_ref[rows, :] = y
            else:
                yf_ref[rows, :] = y
            pv = jnp.concatenate([ps[:c] + ps[c:], v], axis=0)
            last = half * c + (0 if d else c - 1)
            e_tot = jnp.exp(cum_t2[:, last:last + 1])
            states[d] = states[d] * e_tot + jnp.where(head_bd, _dot1(bkt_ref[d, cidx], pv), 0.0)
            yield

    def seq_body(p, carry):
        states = list(carry)
        _interleave([seq(states, p, 0), seq(states, npair - 1 - p, 1)])
        return tuple(states)

    zero = jnp.zeros((LANES, LANES), F32)
    lax.fori_loop(0, npair, seq_body, (zero, zero))

    bd = jnp.where(head_bd, 1.0, 0.0)
    y = yf_ref[...] + yb_ref[...]
    mean = _dot_exact_rhs(y, bd) * (1.0 / RW_HEAD_DIM)
    yc = y - mean
    var = _dot_exact_rhs(yc * yc, bd) * (1.0 / RW_HEAD_DIM)
    yn = yc * lax.rsqrt(var + RW_GN_EPS) * lnw_ref[...] + lnb_ref[...]
    y_ref[...] = ((yn + bon_ref[...]) * gate_ref[...]).astype(y_ref.dtype)


def rwkv_scan(prep, ln_w, ln_b, batch):
    t = prep[0].shape[0]
    s = t // batch
    assert s % PAIR == 0 and (s // PAIR) % min(RW_PREP_PAIRS, s // PAIR) == 0
    nct = RW_WIDTH // LANES
    nchunk = s // CHUNK
    blk = pl.BlockSpec((s, LANES), lambda b, c: (b, c))
    vec = pl.BlockSpec((1, LANES), lambda b, c: (0, c))
    tsp = pltpu.VMEM((2, LANES, s), F32)
    big = pltpu.VMEM((s, LANES), F32)
    per_chunk = lambda rows, cols: pltpu.VMEM((2, nchunk, rows, cols), BF16)
    return pl.pallas_call(
        _rw_scan_body,
        grid=(batch, nct),
        in_specs=[blk] * 11 + [vec, vec],
        out_specs=blk,
        out_shape=jax.ShapeDtypeStruct((t, RW_WIDTH), BF16),
        scratch_shapes=[pltpu.VMEM((2, s, LANES), F32), tsp, tsp, tsp, big, big,
                        per_chunk(2 * PAIR, LANES), per_chunk(PAIR, LANES), per_chunk(PAIR, 2 * LANES),
                        per_chunk(PAIR, LANES)],
        compiler_params=_params("parallel", "parallel"),
        name="rwkv_scan",
    )(*prep, ln_w.reshape(1, RW_WIDTH), ln_b.reshape(1, RW_WIDTH))


def even_mixer(x, g, w_main, w_ab, w_out, layer, conv_qkv, a_log, dt_bias, out_gain, conv_sc, batch):
    p, ab = norm_mm(x, g, w_main, layer, wp=w_ab)
    zeros = jnp.zeros((2 * DN_HEADS,), F32)
    alog32 = jnp.concatenate([zeros, a_log.reshape(-1)]).reshape(1, -1)
    dt32 = jnp.concatenate([zeros, dt_bias.reshape(-1)]).reshape(1, -1)
    y_dn = deltanet(p, ab, conv_qkv.T, alog32, dt32, out_gain, batch)
    y_sc = short_conv(p, conv_sc.T, 4 * DN_WIDTH, batch)
    y = jnp.concatenate([y_dn, y_sc], axis=1)
    return mm_res(y, w_out, layer, x)


def odd_mixer(x, g, w_main, w_lora, w_out, layer, qn, kn, bias_tab, mu, w0, w2, a0, a2, g2, k_k, k_a, r_k, ln_w, ln_b,
              batch):
    p, lora_in = norm_mm(x, g, w_main, layer, wp=w_lora)
    y_c = dilated_attention(p, qn, kn, bias_tab, batch)
    prep = rwkv_prep(p, lora_in, mu, w0, w2, a0, a2, g2, k_k, k_a, r_k, 3 * DSA_QKV, batch)
    y_d = rwkv_scan(prep, ln_w, ln_b, batch)
    y = jnp.concatenate([y_c, y_d], axis=1)
    return mm_res(y, w_out, layer, x)


def kernel(x, mem, rel_bias, norm_mix, norm_xattn, norm_mem, norm_ffn, xa_wq, xa_wk, xa_wv, xa_wo, xa_qn, xa_kn, ffn_w1, ffn_w2, ev_w_in, ev_w_out, dn_conv, dn_a_log, dn_dt_bias, dn_norm, sc_conv, od_w_in, od_w_out, ca_qn, ca_kn, rw_mu, rw_w0, rw_w2, rw_a0, rw_a2, rw_g2, rw_k_k, rw_k_a, rw_r_k, rw_ln_w, rw_ln_b):
    batch, seq, d = x.shape
    n_mem = mem.shape[1]
    xf = x.reshape(batch * seq, d)
    memf = mem.reshape(batch * n_mem, d)
    bias_tab = _dsa_bias_table(rel_bias)
    c_ab = 4 * DN_WIDTH
    c_sc = c_ab + 4 * DN_HEADS
    c_lo = 3 * DSA_QKV + RW_MAIN
    ev_main = jnp.concatenate([ev_w_in[:, :, :c_ab], ev_w_in[:, :, c_sc:]], axis=2).astype(BF16)
    od_main = od_w_in[:, :, :c_lo].astype(BF16)
    ev_out, od_out = ev_w_out.astype(BF16), od_w_out.astype(BF16)
    w_q, w_o = xa_wq.astype(BF16), xa_wo.astype(BF16)
    w_kv = jnp.concatenate([xa_wk, xa_wv], axis=2).astype(BF16)
    w_1, w_2 = ffn_w1.astype(BF16), ffn_w2.astype(BF16)
    for layer in range(DEPTH):
        i = layer // 2
        if layer % 2 == 0:
            xf = even_mixer(xf, norm_mix[layer], ev_main, ev_w_in[i, :, c_ab:c_sc], ev_out, i, dn_conv[i], dn_a_log[i],
                            dn_dt_bias[i], dn_norm[i], sc_conv[i], batch)
        else:
            xf = odd_mixer(xf, norm_mix[layer], od_main, od_w_in[i, :, c_lo:], od_out, i, ca_qn[i], ca_kn[i], bias_tab,
                           rw_mu[i], rw_w0[i], rw_w2[i], rw_a0[i], rw_a2[i], rw_g2[i], rw_k_k[i], rw_k_a[i], rw_r_k[i],
                           rw_ln_w[i], rw_ln_b[i], batch)
        kv = norm_mm(memf, norm_mem[layer], w_kv, layer).reshape(batch, n_mem, 2 * XA_WIDTH)
        xf = xattn(xf, norm_xattn[layer], w_q, kv, w_o, layer, xa_qn[layer], xa_kn[layer], batch)
        h1 = norm_mm(xf, norm_ffn[layer], w_1, layer, act="relu2", out_dtype=BF16)
        xf = mm_res(h1, w_2, layer, xf)
    return xf.reshape(batch, seq, d)
```

```python
import functools
import math

import jax
import jax.numpy as jnp
import numpy as np
from jax import lax
from jax.experimental import pallas as pl
from jax.experimental.pallas import tpu as pltpu

F32 = jnp.float32
BF16 = jnp.bfloat16

D_MODEL = 2048
DEPTH = 4
RMS_EPS = 1e-6
L2_EPS = 1e-6

DN_HEADS = 8
DN_HEAD_DIM = 128
DN_WIDTH = DN_HEADS * DN_HEAD_DIM
SC_WIDTH = D_MODEL - DN_WIDTH
CHUNK = 64
PAIR = 2 * CHUNK
DN_PREP_PAIRS = 4
RW_PREP_PAIRS = 2

DSA_PATTERNS = ((128, 1), (512, 4), (2048, 16))
DSA_GROUPS = len(DSA_PATTERNS)
DSA_HPG = 4
DSA_HEAD_DIM = 128
DSA_HEADS = DSA_GROUPS * DSA_HPG
DSA_QKV = DSA_HEADS * DSA_HEAD_DIM
DSA_SIDE = 64
DSA_QBLK = 128
DSA_PAD = DSA_SIDE * max(d for _, d in DSA_PATTERNS)
DSA_INTERLEAVE = 4
REL_BUCKETS = 32
REL_MAX_DIST = 1024
NEG_INF = -1e30

RW_HEADS = 8
RW_HEAD_DIM = 64
RW_WIDTH = RW_HEADS * RW_HEAD_DIM
RW_LORA = 64
RW_GATE_LORA = 128
RW_MAIN = 3 * RW_WIDTH
RW_LORA_IN = 4 * RW_LORA + RW_GATE_LORA
RW_GN_EPS = 64e-5

XA_HEADS = 4
XA_HEAD_DIM = 128
XA_WIDTH = XA_HEADS * XA_HEAD_DIM

LANES = 128
NORM_ROWS = 256
VMEM_LIMIT_BYTES = 56 * 1024 * 1024


def _params(*sem):
    return pltpu.CompilerParams(dimension_semantics=sem, vmem_limit_bytes=VMEM_LIMIT_BYTES)


_NN = (((1,), (0,)), ((), ()))
_NT = (((1,), (1,)), ((), ()))


def _dot1(a, b, dims=_NN):
    return lax.dot_general(a.astype(BF16), b.astype(BF16), dims, preferred_element_type=F32)


def _split2(a):
    hi = a.astype(BF16)
    lo = (a - hi.astype(F32)).astype(BF16)
    return hi, lo


def _dot3(a, b, dims=_NN):
    ah, al = _split2(a)
    bh, bl = _split2(b)
    dg = functools.partial(lax.dot_general, dimension_numbers=dims, preferred_element_type=F32)
    return dg(ah, bh) + (dg(al, bh) + dg(ah, bl))


def _dot_exact_lhs(a01, b):
    a = a01.astype(BF16)
    b1 = b.astype(BF16)
    r1 = b - b1.astype(F32)
    b2 = r1.astype(BF16)
    b3 = (r1 - b2.astype(F32)).astype(BF16)
    dg = functools.partial(lax.dot_general, dimension_numbers=_NN, preferred_element_type=F32)
    return dg(a, b1) + (dg(a, b2) + dg(a, b3))


def _dot_exact_rhs(a, b01):
    b = b01.astype(BF16)
    a1 = a.astype(BF16)
    r1 = a - a1.astype(F32)
    a2 = r1.astype(BF16)
    a3 = (r1 - a2.astype(F32)).astype(BF16)
    dg = functools.partial(lax.dot_general, dimension_numbers=_NN, preferred_element_type=F32)
    return dg(a1, b) + (dg(a2, b) + dg(a3, b))


def _interleave(chains):
    chains = list(chains)
    while chains:
        alive = []
        for ch in chains:
            try:
                next(ch)
                alive.append(ch)
            except StopIteration:
                pass
        chains = alive


def _merge_masks(n, reverse):
    ii = lax.broadcasted_iota(jnp.int32, (n, n), 0)
    jj = lax.broadcasted_iota(jnp.int32, (n, n), 1)
    tri = (ii < jj) if reverse else (ii > jj)
    out = []
    size = 1
    while size < CHUNK:
        out.append(tri & ((ii // (2 * size)) == (jj // (2 * size))) & ((ii // size) != (jj // size)))
        size *= 2
    return out


def _tri_inverse(a, eye_f, level_masks, out):
    t = eye_f - jnp.where(level_masks[0], a, 0.0)
    for m in level_masks[1:]:
        x = _dot1(jnp.where(m, a, 0.0), t)
        yield
        t = t - _dot1(t, x)
        yield
    out.append(t)


def _sigmoid(x):
    return 1.0 / (1.0 + jnp.exp(-x))


def _softplus(x):
    return jnp.maximum(x, 0.0) + jnp.log1p(jnp.exp(-jnp.abs(x)))


def _shift_down(x, row):
    return jnp.where(row == 0, 0.0, pltpu.roll(x, 1, 0))


def _shift_up(x, row):
    n = x.shape[0]
    return jnp.where(row == n - 1, 0.0, pltpu.roll(x, n - 1, 0))


def _norm_mm_body(x_ref, g_ref, w_ref, *rest, act, precise, precise_t, n_first):
    if n_first is not None:
        w2_ref, rest = rest[0], rest[1:]
    if precise:
        wp_ref, o_ref, op_ref, xn_ref = rest
    else:
        o_ref, xn_ref = rest

    @pl.when(pl.program_id(1) == 0)
    def _():
        def rows_step(r, _):
            rows = pl.ds(pl.multiple_of(r * NORM_ROWS, NORM_ROWS), NORM_ROWS)
            x = x_ref[rows, :]
            xn = x * lax.rsqrt(jnp.mean(x * x, axis=-1, keepdims=True) + RMS_EPS) * g_ref[...]
            xn_ref[rows, :] = xn.astype(BF16)
            if precise and precise_t:
                op_ref[:, rows] = _dot3(wp_ref[...], xn, _NT)
            elif precise:
                op_ref[rows, :] = _dot3(xn, wp_ref[...])
            return 0

        lax.fori_loop(0, x_ref.shape[0] // NORM_ROWS, rows_step, 0)

    def project(wt_ref):
        acc = jnp.dot(xn_ref[...], wt_ref[...], preferred_element_type=F32)
        if act == "relu2":
            acc = jnp.square(jnp.maximum(acc, 0.0))
        o_ref[...] = acc.astype(o_ref.dtype)

    if n_first is None:
        project(w_ref)
    else:
        pl.when(pl.program_id(1) < n_first)(lambda: project(w_ref))
        pl.when(pl.program_id(1) >= n_first)(lambda: project(w2_ref))


def norm_mm(x, g, w, layer, w2=None, wp=None, wp_t=None, act=None, out_dtype=F32, tm=1024, tn=1024):
    t, d = x.shape
    n1 = w.shape[2]
    n = n1 + (0 if w2 is None else w2.shape[2])
    tm = min(tm, t)
    tn = min(tn, n1)
    assert t % tm == 0 and n1 % tn == 0 and n % tn == 0 and tm % NORM_ROWS == 0
    precise = wp is not None or wp_t is not None
    n_first = None if w2 is None else n1 // tn
    in_specs = [
        pl.BlockSpec((tm, d), lambda i, j: (i, 0)),
        pl.BlockSpec((1, d), lambda i, j: (0, 0)),
    ]
    args = [x, g.reshape(1, d), w]
    if w2 is None:
        in_specs.append(pl.BlockSpec((None, d, tn), lambda i, j: (layer, 0, j)))
    else:
        in_specs.append(pl.BlockSpec((None, d, tn), lambda i, j: (layer, 0, jnp.minimum(j, n_first - 1))))
        in_specs.append(pl.BlockSpec((None, d, tn), lambda i, j: (layer, 0, jnp.maximum(j - n_first, 0))))
        args.append(w2)
    out_specs = pl.BlockSpec((tm, tn), lambda i, j: (i, j))
    out_shape = jax.ShapeDtypeStruct((t, n), out_dtype)
    if wp is not None:
        npc = wp.shape[1]
        in_specs.append(pl.BlockSpec((d, npc), lambda i, j: (0, 0)))
        out_specs = [out_specs, pl.BlockSpec((tm, npc), lambda i, j: (i, 0))]
        out_shape = [out_shape, jax.ShapeDtypeStruct((t, npc), F32)]
        args.append(wp)
    elif wp_t is not None:
        npc = wp_t.shape[0]
        in_specs.append(pl.BlockSpec((npc, d), lambda i, j: (0, 0)))
        out_specs = [out_specs, pl.BlockSpec((npc, tm), lambda i, j: (0, i))]
        out_shape = [out_shape, jax.ShapeDtypeStruct((npc, t), F32)]
        args.append(wp_t)
    return pl.pallas_call(
        functools.partial(_norm_mm_body, act=act, precise=precise, precise_t=wp_t is not None, n_first=n_first),
        grid=(t // tm, n // tn),
        in_specs=in_specs,
        out_specs=out_specs,
        out_shape=out_shape,
        scratch_shapes=[pltpu.VMEM((tm, d), BF16)],
        compiler_params=_params("parallel", "arbitrary"),
        name="norm_mm",
    )(*args)


def _mm_res_body(a_ref, w_ref, r_ref, o_ref):
    acc = jnp.dot(a_ref[...], w_ref[...], preferred_element_type=F32)

    @pl.when(pl.program_id(2) == 0)
    def _():
        o_ref[...] = r_ref[...] + acc

    @pl.when(pl.program_id(2) > 0)
    def _():
        o_ref[...] += acc


def mm_res(a, w, layer, res, tm=1024, tn=1024, tk=2048):
    t, k = a.shape
    n = w.shape[2]
    tm, tn, tk = min(tm, t), min(tn, n), min(tk, k)
    assert t % tm == 0 and n % tn == 0 and k % tk == 0
    return pl.pallas_call(
        _mm_res_body,
        grid=(t // tm, n // tn, k // tk),
        in_specs=[
            pl.BlockSpec((tm, tk), lambda i, j, kk: (i, kk)),
            pl.BlockSpec((None, tk, tn), lambda i, j, kk: (layer, kk, j)),
            pl.BlockSpec((tm, tn), lambda i, j, kk: (i, j)),
        ],
        out_specs=pl.BlockSpec((tm, tn), lambda i, j, kk: (i, j)),
        out_shape=jax.ShapeDtypeStruct((t, n), F32),
        compiler_params=_params("parallel", "parallel", "arbitrary"),
        name="mm_res",
    )(a, w, res)


def _xattn_body(x_ref, g_ref, wq_ref, kv_ref, wo_ref, qn_ref, kn_ref, o_ref):
    x = x_ref[...]
    xn = x * lax.rsqrt(jnp.mean(x * x, axis=-1, keepdims=True) + RMS_EPS) * g_ref[...]
    q = jnp.dot(xn.astype(BF16), wq_ref[...], preferred_element_type=F32)
    kv = kv_ref[0]
    outs = []
    for h in range(XA_HEADS):
        sl = slice(h * XA_HEAD_DIM, (h + 1) * XA_HEAD_DIM)
        qh = q[:, sl]
        qh = qh * lax.rsqrt(jnp.mean(qh * qh, axis=-1, keepdims=True) + RMS_EPS) * qn_ref[...]
        kh = kv[:, sl]
        kh = kh * lax.rsqrt(jnp.mean(kh * kh, axis=-1, keepdims=True) + RMS_EPS) * kn_ref[...]
        vh = kv[:, XA_WIDTH + h * XA_HEAD_DIM:XA_WIDTH + (h + 1) * XA_HEAD_DIM]
        logits = _dot1(qh, kh, _NT) * (XA_HEAD_DIM ** -0.5)
        m = jnp.max(logits, axis=-1, keepdims=True)
        p = jnp.exp(logits - m)
        s = jnp.sum(p, axis=-1, keepdims=True)
        outs.append(_dot1(p, vh) / s)
    o = jnp.concatenate(outs, axis=-1).astype(BF16)
    o_ref[...] = x + jnp.dot(o, wo_ref[...], preferred_element_type=F32)


def xattn(x, g, wq, kv, wo, layer, qn, kn, batch, ts=512):
    t, d = x.shape
    s = t // batch
    ts = min(ts, s)
    nst = s // ts
    m = kv.shape[1]
    return pl.pallas_call(
        _xattn_body,
        grid=(batch, nst),
        in_specs=[
            pl.BlockSpec((ts, d), lambda b, i: (b * nst + i, 0)),
            pl.BlockSpec((1, d), lambda b, i: (0, 0)),
            pl.BlockSpec((None, d, XA_WIDTH), lambda b, i: (layer, 0, 0)),
            pl.BlockSpec((1, m, 2 * XA_WIDTH), lambda b, i: (b, 0, 0)),
            pl.BlockSpec((None, XA_WIDTH, d), lambda b, i: (layer, 0, 0)),
            pl.BlockSpec((1, XA_HEAD_DIM), lambda b, i: (0, 0)),
            pl.BlockSpec((1, XA_HEAD_DIM), lambda b, i: (0, 0)),
        ],
        out_specs=pl.BlockSpec((ts, d), lambda b, i: (b * nst + i, 0)),
        out_shape=jax.ShapeDtypeStruct((t, d), F32),
        compiler_params=_params("parallel", "parallel"),
        name="xattn",
    )(x, g.reshape(1, d), wq, kv, wo, qn.reshape(1, -1), kn.reshape(1, -1))


def _tri_masks(n, blk):
    ii = lax.broadcasted_iota(jnp.int32, (n, n), 0)
    jj = lax.broadcasted_iota(jnp.int32, (n, n), 1)
    same = (ii // blk) == (jj // blk) if n != blk else None

    def m(c):
        return c if same is None else (c & same)

    return {
        False: (m(ii >= jj), m(ii > jj)),
        True: (m(ii <= jj), m(ii < jj)),
        "eye": ii == jj,
    }


def _seg_cumsum(x, pos, reverse):
    n = x.shape[0]
    sh = 1
    while sh < CHUNK:
        if reverse:
            x = x + jnp.where(pos < CHUNK - sh, pltpu.roll(x, n - sh, 0), 0.0)
        else:
            x = x + jnp.where(pos >= sh, pltpu.roll(x, sh, 0), 0.0)
        sh *= 2
    return x


def _dn_body(q_ref, k_ref, v_ref, gate_ref, ab_ref, cq_ref, ck_ref, cv_ref, alog_ref, dt_ref, gain_ref,
             y_ref, qs_ref, ks_ref, vs_ref, kt_ref, bg_ref, gt_ref, of_ref, ob_ref,
             u_ref, w_ref, qg_ref, qk_ref, kdt_ref, eg_ref):
    s = q_ref.shape[0]
    h = pl.program_id(1)
    row = lax.broadcasted_iota(jnp.int32, (s, 1), 0)
    c = CHUNK

    def conv_silu(x_ref, cw_ref):
        x = x_ref[...]
        w = cw_ref[...]
        y = _shift_down(x, row) * w[0:1] + x * w[1:2] + _shift_up(x, row) * w[2:3]
        return y * _sigmoid(y)

    q = conv_silu(q_ref, cq_ref)
    q = q * lax.rsqrt(jnp.sum(q * q, axis=-1, keepdims=True) + L2_EPS) * (DN_HEAD_DIM ** -0.5)
    qs_ref[...] = q
    k = conv_silu(k_ref, ck_ref)
    k = k * lax.rsqrt(jnp.sum(k * k, axis=-1, keepdims=True) + L2_EPS)
    ks_ref[...] = k
    kt_ref[...] = k.T
    vs_ref[...] = conv_silu(v_ref, cv_ref)

    lane32 = lax.broadcasted_iota(jnp.int32, (1, 4 * DN_HEADS), 1)
    pos_t = lax.broadcasted_iota(jnp.int32, (1, s), 1) % c

    def decay_row(d):
        idx = (2 + d) * DN_HEADS + h
        a_log = jnp.sum(jnp.where(lane32 == idx, alog_ref[...], 0.0), axis=-1, keepdims=True)
        dt = jnp.sum(jnp.where(lane32 == idx, dt_ref[...], 0.0), axis=-1, keepdims=True)
        g = -jnp.exp(a_log) * _softplus(ab_ref[pl.ds(idx, 1), :] + dt)
        sh = 1
        while sh < c:
            if d:
                g = g + jnp.where(pos_t < c - sh, pltpu.roll(g, s - sh, 1), 0.0)
            else:
                g = g + jnp.where(pos_t >= sh, pltpu.roll(g, sh, 1), 0.0)
            sh *= 2
        return g

    rows = [_sigmoid(ab_ref[pl.ds(h, 1), :]), _sigmoid(ab_ref[pl.ds(DN_HEADS + h, 1), :]), decay_row(0), decay_row(1)]
    gt = jnp.concatenate(rows + [jnp.zeros((LANES - len(rows), s), F32)], axis=0)
    gt_ref[...] = gt[0:8, :]
    bg_ref[...] = gt.T
    lane128 = lax.broadcasted_iota(jnp.int32, (1, LANES), 1)

    masks = _tri_masks(PAIR, c)
    merge = {rev: _merge_masks(PAIR, rev) for rev in (False, True)}
    eye_f = jnp.where(masks["eye"], 1.0, 0.0)
    npair = s // PAIR
    group = min(DN_PREP_PAIRS, npair)

    def prep(pidx, d):
        reverse = d == 1
        rows = pl.ds(pl.multiple_of(pidx * PAIR, PAIR), PAIR)
        qp, kp, vp = qs_ref[rows, :], ks_ref[rows, :], vs_ref[rows, :]
        bgp = bg_ref[rows, :]
        beta = bgp[:, d:d + 1]
        gcol = bgp[:, 2 + d:3 + d]
        grow = gt_ref[2 + d:3 + d, rows]
        incl, strict = masks[reverse]
        decay = jnp.where(incl, jnp.exp(jnp.where(incl, gcol - grow, 0.0)), 0.0)
        kb = kp * beta
        kq = _dot1(jnp.concatenate([kb, qp], axis=0), kp, _NT)
        yield
        qk_ref[d, rows, :] = jnp.where(incl, kq[PAIR:] * decay, 0.0).astype(BF16)
        t_inv = []
        yield from _tri_inverse(jnp.where(strict, kq[:PAIR] * decay, 0.0), eye_f, merge[reverse], t_inv)
        egc = jnp.exp(gcol)
        uw = _dot1(t_inv[0], jnp.concatenate([vp * beta, kb * egc], axis=1))
        yield
        u_ref[d, rows, :] = uw[:, :DN_HEAD_DIM]
        w_ref[d, rows, :] = uw[:, DN_HEAD_DIM:].astype(BF16)
        qg_ref[d, rows, :] = (qp * egc).astype(BF16)
        last0, last1 = (0, c) if reverse else (c - 1, PAIR - 1)
        glast = jnp.where(lane128 < c, grow[:, last0:last0 + 1], grow[:, last1:last1 + 1])
        kdt_ref[d, :, rows] = (kt_ref[:, rows] * jnp.exp(glast - grow)).astype(BF16)
        eg_ref[d, :, rows] = jnp.broadcast_to(jnp.exp(glast), (8, PAIR))

    def prep_body(i, _):
        _interleave(prep(i * group + j, d) for j in range(group) for d in range(2))
        return 0

    lax.fori_loop(0, npair // group, prep_body, 0)

    def seq(states, pidx, d):
        pair_rows = pl.ds(pl.multiple_of(pidx * PAIR, PAIR), PAIR)
        kdt = kdt_ref[d, :, pair_rows]
        eg = eg_ref[d, 0:1, pair_rows]
        for half in ((1, 0) if d else (0, 1)):
            r0 = half * c
            rows = pl.ds(pl.multiple_of(pidx * PAIR + r0, c), c)
            ws = _dot1(jnp.concatenate([w_ref[d, rows, :], qg_ref[d, rows, :]], axis=0), states[d])
            yield
            v_new = u_ref[d, rows, :] - ws[:c]
            o = ws[c:] + _dot1(qk_ref[d, rows, r0:r0 + c], v_new)
            if d:
                ob_ref[rows, :] = o
            else:
                of_ref[rows, :] = o
            states[d] = states[d] * eg[:, r0:r0 + 1] + _dot1(kdt[:, r0:r0 + c], v_new)
            yield

    def seq_body(p, carry):
        states = list(carry)
        _interleave([seq(states, p, 0), seq(states, npair - 1 - p, 1)])
        return tuple(states)

    zero = jnp.zeros((DN_HEAD_DIM, DN_HEAD_DIM), F32)
    lax.fori_loop(0, npair, seq_body, (zero, zero))

    o = of_ref[...] + ob_ref[...]
    o = o * lax.rsqrt(jnp.mean(o * o, axis=-1, keepdims=True) + RMS_EPS) * gain_ref[...]
    gate = gate_ref[...]
    y_ref[...] = (o * (gate * _sigmoid(gate))).astype(y_ref.dtype)


def deltanet(p, ab, conv_t, alog32, dt32, gain, batch):
    t = p.shape[0]
    s = t // batch
    assert s % PAIR == 0 and (s // PAIR) % min(DN_PREP_PAIRS, s // PAIR) == 0
    hd = DN_HEAD_DIM
    col = lambda off: pl.BlockSpec((s, hd), lambda b, h: (b, off + h))
    cw = lambda off: pl.BlockSpec((3, hd), lambda b, h: (0, off + h))
    small = lambda n: pl.BlockSpec((1, n), lambda b, h: (0, 0))
    big = pltpu.VMEM((s, hd), F32)
    half = pltpu.VMEM((2, s, hd), BF16)
    return pl.pallas_call(
        _dn_body,
        grid=(batch, DN_HEADS),
        in_specs=[col(0), col(DN_HEADS), col(2 * DN_HEADS), col(3 * DN_HEADS),
                  pl.BlockSpec((4 * DN_HEADS, s), lambda b, h: (0, b)),
                  cw(0), cw(DN_HEADS), cw(2 * DN_HEADS),
                  small(4 * DN_HEADS), small(4 * DN_HEADS), small(hd)],
        out_specs=pl.BlockSpec((s, hd), lambda b, h: (b, h)),
        out_shape=jax.ShapeDtypeStruct((t, DN_WIDTH), BF16),
        scratch_shapes=[big, big, big, pltpu.VMEM((hd, s), F32), big, pltpu.VMEM((8, s), F32), big, big,
                        pltpu.VMEM((2, s, hd), F32), half, half, half, pltpu.VMEM((2, hd, s), BF16),
                        pltpu.VMEM((2, 8, s), F32)],
        compiler_params=_params("parallel", "parallel"),
        name="deltanet",
    )(p, p, p, p, ab, conv_t, conv_t, conv_t, alog32, dt32, gain.reshape(1, hd))


def _sconv_body(b_ref, c_ref, u_ref, w_ref, y_ref):
    s = b_ref.shape[0]
    row = lax.broadcasted_iota(jnp.int32, (s, 1), 0)
    cu = c_ref[...] * u_ref[...]
    w = w_ref[...]
    y = _shift_down(cu, row) * w[0:1] + cu * w[1:2] + _shift_up(cu, row) * w[2:3]
    y_ref[...] = (b_ref[...] * y).astype(y_ref.dtype)


def short_conv(p, conv_t, col0, batch, tc=256):
    t = p.shape[0]
    s = t // batch
    nct = SC_WIDTH // tc
    base = col0 // tc
    col = lambda off: pl.BlockSpec((s, tc), lambda b, c: (b, base + off * nct + c))
    return pl.pallas_call(
        _sconv_body,
        grid=(batch, nct),
        in_specs=[col(0), col(1), col(2), pl.BlockSpec((3, tc), lambda b, c: (0, c))],
        out_specs=pl.BlockSpec((s, tc), lambda b, c: (b, c)),
        out_shape=jax.ShapeDtypeStruct((t, SC_WIDTH), BF16),
        compiler_params=_params("parallel", "parallel"),
        name="short_conv",
    )(p, p, p, conv_t)


def _dsa_body(*refs, seq):
    q_refs = refs[0:3]
    k_refs = refs[3:6]
    v_refs = refs[6:9]
    qn_ref, kn_ref, bias_ref, y_ref, qs_ref, kpad_ref, vpad_ref, og_ref, lse_ref = refs[9:]
    s = seq
    qb = DSA_QBLK
    side = DSA_SIDE
    width = qb + 2 * side
    kj = lax.broadcasted_iota(jnp.int32, (1, width), 1)
    zpad = jnp.zeros((DSA_PAD, DSA_HEAD_DIM), F32)
    for ref in (kpad_ref, vpad_ref):
        ref[0:DSA_PAD, :] = zpad
        ref[DSA_PAD + s:2 * DSA_PAD + s, :] = zpad

    for gi, (_, dil) in enumerate(DSA_PATTERNS):
        sub = s // dil
        nblk = sub // qb
        q = q_refs[gi][...]
        qs_ref[...] = (q * lax.rsqrt(jnp.mean(q * q, axis=-1, keepdims=True) + RMS_EPS) * qn_ref[...]
                       * (DSA_HEAD_DIM ** -0.5))
        k = k_refs[gi][...]
        kpad_ref[DSA_PAD:DSA_PAD + s, :] = k * lax.rsqrt(jnp.mean(k * k, axis=-1, keepdims=True) + RMS_EPS) * kn_ref[...]
        vpad_ref[DSA_PAD:DSA_PAD + s, :] = v_refs[gi][...]
        bias = bias_ref[0, gi]

        def block(t, gi=gi, dil=dil, sub=sub, nblk=nblk, bias=bias):
            r = t // nblk
            n = t % nblk
            rows = pl.ds(r + n * (qb * dil), qb, stride=dil)
            win = pl.ds(DSA_PAD + r + (n * qb - side) * dil, width, stride=dil)
            logits = _dot1(qs_ref[rows, :], kpad_ref[win, :], _NT) + bias
            yield
            pos = n * qb - side + kj
            logits = jnp.where((pos >= 0) & (pos < sub), logits, NEG_INF)
            m = jnp.max(logits, axis=-1, keepdims=True)
            yield
            p = jnp.exp(logits - m)
            ssum = jnp.sum(p, axis=-1, keepdims=True)
            o = _dot1(p, vpad_ref[win, :])
            yield
            og_ref[gi, rows, :] = o / ssum
            lse_ref[gi, rows, :] = jnp.broadcast_to(m + jnp.log(ssum), (qb, DSA_HEAD_DIM))

        nblocks = dil * nblk

        def blocks_step(i, _, block=block):
            _interleave(block(i * DSA_INTERLEAVE + u) for u in range(DSA_INTERLEAVE))
            return 0

        lax.fori_loop(0, nblocks // DSA_INTERLEAVE, blocks_step, 0)

    lse = [lse_ref[gi] for gi in range(DSA_GROUPS)]
    mx = jnp.maximum(jnp.maximum(lse[0], lse[1]), lse[2])
    ws = [jnp.exp(l - mx) for l in lse]
    num = ws[0] * og_ref[0] + ws[1] * og_ref[1] + ws[2] * og_ref[2]
    y_ref[...] = (num / (ws[0] + ws[1] + ws[2])).astype(y_ref.dtype)


def dilated_attention(p, qn, kn, bias_tab, batch):
    t = p.shape[0]
    s = t // batch
    assert (s // DSA_QBLK) % DSA_INTERLEAVE == 0
    hd = DSA_HEAD_DIM
    nh = DSA_HEADS

    def col(part, gi):
        return pl.BlockSpec((s, hd), lambda b, j: (b, part * nh + gi * DSA_HPG + j))

    in_specs = [col(part, gi) for part in range(3) for gi in range(DSA_GROUPS)]
    in_specs += [pl.BlockSpec((1, hd), lambda b, j: (0, 0)), pl.BlockSpec((1, hd), lambda b, j: (0, 0)),
                 pl.BlockSpec((1, DSA_GROUPS, DSA_QBLK, DSA_QBLK + 2 * DSA_SIDE), lambda b, j: (j, 0, 0, 0))]
    big = pltpu.VMEM((s, hd), F32)
    pad = pltpu.VMEM((s + 2 * DSA_PAD, hd), F32)
    grp = pltpu.VMEM((DSA_GROUPS, s, hd), F32)
    return pl.pallas_call(
        functools.partial(_dsa_body, seq=s),
        grid=(batch, DSA_HPG),
        in_specs=in_specs,
        out_specs=pl.BlockSpec((s, hd), lambda b, j: (b, j)),
        out_shape=jax.ShapeDtypeStruct((t, DSA_HPG * hd), BF16),
        scratch_shapes=[big, pad, pad, grp, grp],
        compiler_params=_params("parallel", "parallel"),
        name="dilated_attention",
    )(*([p] * 9), qn.reshape(1, hd), kn.reshape(1, hd), bias_tab)


def _t5_bucket(rel):
    half = REL_BUCKETS // 2
    max_exact = half // 2
    n = np.abs(rel)
    scaled = (np.log(np.maximum(n, max_exact).astype(np.float32) / np.float32(max_exact))
              / np.float32(math.log(REL_MAX_DIST / max_exact)))
    large = np.minimum(max_exact + (scaled * np.float32(half - max_exact)).astype(np.int32), half - 1)
    return np.where(rel > 0, half, 0) + np.where(n < max_exact, n, large)


def _dsa_bias_table(rel_bias):
    width = DSA_QBLK + 2 * DSA_SIDE
    tabs = []
    for gi, (_, dil) in enumerate(DSA_PATTERNS):
        offs = np.arange(-DSA_SIDE, DSA_SIDE + 1, dtype=np.int32) * dil
        band = rel_bias[_t5_bucket(offs)][:, gi * DSA_HPG:(gi + 1) * DSA_HPG].astype(F32)
        fill = jnp.full((DSA_QBLK - 1, DSA_HPG), NEG_INF, F32)
        line = jnp.concatenate([fill, band, fill], axis=0)
        tabs.append(jnp.stack([line[DSA_QBLK - 1 - q:DSA_QBLK - 1 - q + width] for q in range(DSA_QBLK)]))
    return jnp.transpose(jnp.stack(tabs), (3, 0, 1, 2))


def _head_block_diag():
    ii = lax.broadcasted_iota(jnp.int32, (LANES, LANES), 0)
    jj = lax.broadcasted_iota(jnp.int32, (LANES, LANES), 1)
    return (ii // RW_HEAD_DIM) == (jj // RW_HEAD_DIM)


def _rw_prep_body(r_ref, k_ref, v_ref, lo_ref, mur_ref, muk_ref, muv_ref, mulo_ref, w0_ref, w2_ref, a0_ref,
                  a2_ref, g2_ref, kk_ref, ka_ref, rk_ref,
                  ro_ref, vo_ref, kko_ref, bon_ref, gate_ref, lwf_ref, lwb_ref, kdf_ref, kdb_ref, bbf_ref, bbb_ref):
    s = r_ref.shape[0]
    row = lax.broadcasted_iota(jnp.int32, (s, 1), 0)

    def mix(t, mu):
        return t + mu * (0.5 * (_shift_down(t, row) + _shift_up(t, row)) - t)

    r = mix(r_ref[...], mur_ref[...])
    kr = mix(k_ref[...], muk_ref[...])
    v = mix(v_ref[...], muv_ref[...])
    lo = mix(lo_ref[...], mulo_ref[...])
    bd = jnp.where(_head_block_diag(), 1.0, 0.0)

    def head_sum(t):
        return _dot_exact_rhs(t, bd)

    kk = kr * kk_ref[...]
    kk = kk * lax.rsqrt(head_sum(kk * kk) + L2_EPS)
    gd = lo[:, 4 * RW_LORA:]
    gate_ref[...] = _dot3(_sigmoid(gd), g2_ref[...])
    ro_ref[...] = r
    vo_ref[...] = v
    kko_ref[...] = kk
    bonus = jnp.zeros_like(r)
    outs = ((lwf_ref, kdf_ref, bbf_ref), (lwb_ref, kdb_ref, bbb_ref))
    for d in range(2):
        wd = lo[:, d * RW_LORA:(d + 1) * RW_LORA]
        ad = lo[:, (2 + d) * RW_LORA:(3 + d) * RW_LORA]
        w_log = -_softplus(-(w0_ref[d:d + 1, :] + _dot3(jnp.tanh(wd), w2_ref[d]))) - 0.5
        a = _sigmoid(a0_ref[d:d + 1, :] + _dot3(ad, a2_ref[d]))
        kd = kr * (1.0 + (a - 1.0) * ka_ref[...])
        lw_ref, kd_ref, bb_ref = outs[d]
        lw_ref[...] = -jnp.exp(w_log)
        kd_ref[...] = kd
        bb_ref[...] = kk * a
        bonus = bonus + head_sum(r * kd * rk_ref[...]) * v
    bon_ref[...] = bonus


def rwkv_prep(p, lora_in, mu, w0, w2, a0, a2, g2, k_k, k_a, r_k, col0, batch):
    t = p.shape[0]
    s = t // batch
    nct = RW_WIDTH // LANES
    base = col0 // LANES
    nlo = RW_LORA_IN
    col = lambda off: pl.BlockSpec((s, LANES), lambda b, c: (b, base + off * nct + c))
    vec = lambda off: pl.BlockSpec((1, LANES), lambda b, c: (0, off * nct + c))
    mu_main = mu[:RW_MAIN].reshape(1, RW_MAIN)
    mu_lo = mu[RW_MAIN:].reshape(1, nlo)
    out_spec = pl.BlockSpec((s, LANES), lambda b, c: (b, c))
    n_out = 11
    return pl.pallas_call(
        _rw_prep_body,
        grid=(batch, nct),
        in_specs=[col(0), col(1), col(2),
                  pl.BlockSpec((s, nlo), lambda b, c: (b, 0)),
                  vec(0), vec(1), vec(2),
                  pl.BlockSpec((1, nlo), lambda b, c: (0, 0)),
                  pl.BlockSpec((2, LANES), lambda b, c: (0, c)),
                  pl.BlockSpec((2, RW_LORA, LANES), lambda b, c: (0, 0, c)),
                  pl.BlockSpec((2, LANES), lambda b, c: (0, c)),
                  pl.BlockSpec((2, RW_LORA, LANES), lambda b, c: (0, 0, c)),
                  pl.BlockSpec((RW_GATE_LORA, LANES), lambda b, c: (0, c)),
                  vec(0), vec(0), vec(0)],
        out_specs=[out_spec] * n_out,
        out_shape=[jax.ShapeDtypeStruct((t, RW_WIDTH), F32)] * n_out,
        compiler_params=_params("parallel", "parallel"),
        name="rwkv_prep",
    )(p, p, p, lora_in, mu_main, mu_main, mu_main, mu_lo, w0, w2, a0, a2, g2,
      k_k.reshape(1, RW_WIDTH), k_a.reshape(1, RW_WIDTH), r_k.reshape(1, RW_WIDTH))


def _rw_scan_body(r_ref, v_ref, kk_ref, bon_ref, gate_ref, lwf_ref, lwb_ref, kdf_ref, kdb_ref, bbf_ref, bbb_ref,
                  lnw_ref, lnb_ref, y_ref, cum_ref, cumt_ref, kdt_ref, bbt_ref, yf_ref, yb_ref,
                  tr_ref, tc_ref, rbk_ref, bkt_ref):
    s = r_ref.shape[0]
    c = CHUNK
    c2 = 2 * c
    row = lax.broadcasted_iota(jnp.int32, (s, 1), 0)
    pos = row % c
    lw_refs, kd_refs, bb_refs = (lwf_ref, lwb_ref), (kdf_ref, kdb_ref), (bbf_ref, bbb_ref)
    for d in range(2):
        cum = _seg_cumsum(lw_refs[d][...], pos, d == 1)
        cum_ref[d] = cum
        cumt_ref[d] = cum.T
        kdt_ref[d] = kd_refs[d][...].T
        bbt_ref[d] = bb_refs[d][...].T

    masks = _tri_masks(c2, c)
    merge = {rev: _merge_masks(c2, rev) for rev in (False, True)}
    eye2_f = jnp.where(masks["eye"], 1.0, 0.0)
    head0 = lax.broadcasted_iota(jnp.int32, (1, LANES), 1) < RW_HEAD_DIM
    head_bd = _head_block_diag()
    npair = s // PAIR
    group = min(RW_PREP_PAIRS, npair)

    def stack(x):
        return jnp.concatenate([jnp.where(head0, x, 0.0), jnp.where(head0, 0.0, x)], axis=0)

    def prep(pidx, d, half):
        reverse = d == 1
        incl2, strict2 = masks[reverse]
        cidx = 2 * pidx + half
        cols = pl.ds(pl.multiple_of(pidx * PAIR, PAIR), PAIR)
        rows = pl.ds(pl.multiple_of(cidx * c, c), c)
        hs = slice(half * c, (half + 1) * c)
        r, v, kk = r_ref[rows, :], v_ref[rows, :], kk_ref[rows, :]
        lw, kd, bb = lw_refs[d][rows, :], kd_refs[d][rows, :], bb_refs[d][rows, :]
        cum = cum_ref[d, rows, :]
        e_neg = jnp.exp(-cum)
        a_s = stack(-kk * jnp.exp(cum - lw))
        r_s = stack(r * jnp.exp(cum))
        ar = jnp.concatenate([a_s, r_s], axis=0)
        bk = jnp.concatenate([stack(bb * e_neg), stack(kd * e_neg)], axis=0)
        g = _dot1(ar, bk, _NT)
        yield
        rbk_ref[d, cidx] = jnp.concatenate([jnp.where(incl2, g[c2:, :c2], 0.0),
                                            jnp.where(incl2, g[c2:, c2:], 0.0)], axis=1).astype(BF16)
        av = _dot1(jnp.where(strict2, g[:c2, c2:], 0.0), stack(v))
        yield
        t_inv = []
        yield from _tri_inverse(-jnp.where(strict2, g[:c2, :c2], 0.0), eye2_f, merge[reverse], t_inv)
        tt = _dot1(t_inv[0], jnp.concatenate([a_s, av], axis=1))
        yield
        tr_ref[d, cidx] = jnp.concatenate([tt[:, :LANES], r_s], axis=0).astype(BF16)
        tc_ref[d, cidx] = tt[:, LANES:].astype(BF16)
        cum_t = cumt_ref[d, :, cols][:, hs]
        last = 0 if reverse else c - 1
        e_out_t = jnp.exp(cum_t[:, last:last + 1] - cum_t)
        bkt_ref[d, cidx] = jnp.concatenate([bbt_ref[d, :, cols][:, hs] * e_out_t,
                                            kdt_ref[d, :, cols][:, hs] * e_out_t], axis=1).astype(BF16)

    def prep_body(i, _):
        _interleave(prep(i * group + j, d, half) for j in range(group) for d in range(2) for half in range(2))
        return 0

    lax.fori_loop(0, npair // group, prep_body, 0)

    def seq(states, pidx, d):
        cum_t2 = cumt_ref[d, :, pl.ds(pl.multiple_of(pidx * PAIR, PAIR), PAIR)]
        for half in ((1, 0) if d else (0, 1)):
            cidx = 2 * pidx + half
            rows = pl.ds(pl.multiple_of(cidx * c, c), c)
            v = v_ref[rows, :]
            x = _dot1(tr_ref[d, cidx], states[d])
            yield
            ps = x[:c2] + tc_ref[d, cidx].astype(F32)
            os_ = x[c2:] + _dot1(rbk_ref[d, cidx], jnp.concatenate([ps, stack(v)], axis=0))
            y = os_[:c] + os_[c:]
            if d:
                yb_ref[rows, :] = y
            else:
                yf_ref[rows, :] = y
            pv = jnp.concatenate([ps[:c] + ps[c:], v], axis=0)
            last = half * c + (0 if d else c - 1)
            e_tot = jnp.exp(cum_t2[:, last:last + 1])
            states[d] = states[d] * e_tot + jnp.where(head_bd, _dot1(bkt_ref[d, cidx], pv), 0.0)
            yield

    def seq_body(p, carry):
        states = list(carry)
        _interleave([seq(states, p, 0), seq(states, npair - 1 - p, 1)])
        return tuple(states)

    zero = jnp.zeros((LANES, LANES), F32)
    lax.fori_loop(0, npair, seq_body, (zero, zero))

    bd = jnp.where(head_bd, 1.0, 0.0)
    y = yf_ref[...] + yb_ref[...]
    mean = _dot_exact_rhs(y, bd) * (1.0 / RW_HEAD_DIM)
    yc = y - mean
    var = _dot_exact_rhs(yc * yc, bd) * (1.0 / RW_HEAD_DIM)
    yn = yc * lax.rsqrt(var + RW_GN_EPS) * lnw_ref[...] + lnb_ref[...]
    y_ref[...] = ((yn + bon_ref[...]) * gate_ref[...]).astype(y_ref.dtype)


def rwkv_scan(prep, ln_w, ln_b, batch):
    t = prep[0].shape[0]
    s = t // batch
    assert s % PAIR == 0 and (s // PAIR) % min(RW_PREP_PAIRS, s // PAIR) == 0
    nct = RW_WIDTH // LANES
    nchunk = s // CHUNK
    blk = pl.BlockSpec((s, LANES), lambda b, c: (b, c))
    vec = pl.BlockSpec((1, LANES), lambda b, c: (0, c))
    tsp = pltpu.VMEM((2, LANES, s), F32)
    big = pltpu.VMEM((s, LANES), F32)
    per_chunk = lambda rows, cols: pltpu.VMEM((2, nchunk, rows, cols), BF16)
    return pl.pallas_call(
        _rw_scan_body,
        grid=(batch, nct),
        in_specs=[blk] * 11 + [vec, vec],
        out_specs=blk,
        out_shape=jax.ShapeDtypeStruct((t, RW_WIDTH), BF16),
        scratch_shapes=[pltpu.VMEM((2, s, LANES), F32), tsp, tsp, tsp, big, big,
                        per_chunk(2 * PAIR, LANES), per_chunk(PAIR, LANES), per_chunk(PAIR, 2 * LANES),
                        per_chunk(PAIR, LANES)],
        compiler_params=_params("parallel", "parallel"),
        name="rwkv_scan",
    )(*prep, ln_w.reshape(1, RW_WIDTH), ln_b.reshape(1, RW_WIDTH))


def even_mixer(x, g, w_dn, w_sc, w_ab, w_out, layer, conv_qkv, a_log, dt_bias, out_gain, conv_sc, batch):
    p, ab = norm_mm(x, g, w_dn, layer, w2=w_sc, wp_t=w_ab.T)
    zeros = jnp.zeros((2 * DN_HEADS,), F32)
    alog32 = jnp.concatenate([zeros, a_log.reshape(-1)]).reshape(1, -1)
    dt32 = jnp.concatenate([zeros, dt_bias.reshape(-1)]).reshape(1, -1)
    y_dn = deltanet(p, ab, conv_qkv.T, alog32, dt32, out_gain, batch)
    y_sc = short_conv(p, conv_sc.T, 4 * DN_WIDTH, batch)
    y = jnp.concatenate([y_dn, y_sc], axis=1)
    return mm_res(y, w_out, layer, x)


def odd_mixer(x, g, w_main, w_lora, w_out, layer, qn, kn, bias_tab, mu, w0, w2, a0, a2, g2, k_k, k_a, r_k, ln_w, ln_b,
              batch):
    p, lora_in = norm_mm(x, g, w_main, layer, wp=w_lora)
    y_c = dilated_attention(p, qn, kn, bias_tab, batch)
    prep = rwkv_prep(p, lora_in, mu, w0, w2, a0, a2, g2, k_k, k_a, r_k, 3 * DSA_QKV, batch)
    y_d = rwkv_scan(prep, ln_w, ln_b, batch)
    y = jnp.concatenate([y_c, y_d], axis=1)
    return mm_res(y, w_out, layer, x)


def kernel(x, mem, rel_bias, norm_mix, norm_xattn, norm_mem, norm_ffn, xa_wq, xa_wk, xa_wv, xa_wo, xa_qn, xa_kn, ffn_w1, ffn_w2, ev_w_in, ev_w_out, dn_conv, dn_a_log, dn_dt_bias, dn_norm, sc_conv, od_w_in, od_w_out, ca_qn, ca_kn, rw_mu, rw_w0, rw_w2, rw_a0, rw_a2, rw_g2, rw_k_k, rw_k_a, rw_r_k, rw_ln_w, rw_ln_b):
    batch, seq, d = x.shape
    n_mem = mem.shape[1]
    xf = x.reshape(batch * seq, d)
    memf = mem.reshape(batch * n_mem, d)
    bias_tab = _dsa_bias_table(rel_bias)
    c_ab = 4 * DN_WIDTH
    c_sc = c_ab + 4 * DN_HEADS
    c_lo = 3 * DSA_QKV + RW_MAIN
    ev_dn, ev_sc = ev_w_in[:, :, :c_ab].astype(BF16), ev_w_in[:, :, c_sc:].astype(BF16)
    od_main = od_w_in[:, :, :c_lo].astype(BF16)
    ev_out, od_out = ev_w_out.astype(BF16), od_w_out.astype(BF16)
    w_q, w_o = xa_wq.astype(BF16), xa_wo.astype(BF16)
    w_kv = jnp.concatenate([xa_wk, xa_wv], axis=2).astype(BF16)
    w_1, w_2 = ffn_w1.astype(BF16), ffn_w2.astype(BF16)
    for layer in range(DEPTH):
        i = layer // 2
        if layer % 2 == 0:
            xf = even_mixer(xf, norm_mix[layer], ev_dn, ev_sc, ev_w_in[i, :, c_ab:c_sc], ev_out, i, dn_conv[i],
                            dn_a_log[i], dn_dt_bias[i], dn_norm[i], sc_conv[i], batch)
        else:
            xf = odd_mixer(xf, norm_mix[layer], od_main, od_w_in[i, :, c_lo:], od_out, i, ca_qn[i], ca_kn[i], bias_tab,
                           rw_mu[i], rw_w0[i], rw_w2[i], rw_a0[i], rw_a2[i], rw_g2[i], rw_k_k[i], rw_k_a[i], rw_r_k[i],
                           rw_ln_w[i], rw_ln_b[i], batch)
        kv = norm_mm(memf, norm_mem[layer], w_kv, layer).reshape(batch, n_mem, 2 * XA_WIDTH)
        xf = xattn(xf, norm_xattn[layer], w_q, kv, w_o, layer, xa_qn[layer], xa_kn[layer], batch)
        h1 = norm_mm(xf, norm_ffn[layer], w_1, layer, act="relu2", out_dtype=BF16)
        xf = mm_res(h1, w_2, layer, xf)
    return xf.reshape(batch, seq, d)
```

```python
import functools
import math

import jax
import jax.numpy as jnp
import numpy as np
from jax import lax
from jax.experimental import pallas as pl
from jax.experimental.pallas import tpu as pltpu

F32 = jnp.float32
BF16 = jnp.bfloat16

D_MODEL = 2048
DEPTH = 4
RMS_EPS = 1e-6
L2_EPS = 1e-6

DN_HEADS = 8
DN_HEAD_DIM = 128
DN_WIDTH = DN_HEADS * DN_HEAD_DIM
SC_WIDTH = D_MODEL - DN_WIDTH
CHUNK = 64
PAIR = 2 * CHUNK
DN_PREP_PAIRS = 4
RW_PREP_PAIRS = 2

DSA_PATTERNS = ((128, 1), (512, 4), (2048, 16))
DSA_GROUPS = len(DSA_PATTERNS)
DSA_HPG = 4
DSA_HEAD_DIM = 128
DSA_HEADS = DSA_GROUPS * DSA_HPG
DSA_QKV = DSA_HEADS * DSA_HEAD_DIM
DSA_SIDE = 64
DSA_QBLK = 128
DSA_PAD = DSA_SIDE * max(d for _, d in DSA_PATTERNS)
DSA_INTERLEAVE = 4
REL_BUCKETS = 32
REL_MAX_DIST = 1024
NEG_INF = -1e30

RW_HEADS = 8
RW_HEAD_DIM = 64
RW_WIDTH = RW_HEADS * RW_HEAD_DIM
RW_LORA = 64
RW_GATE_LORA = 128
RW_MAIN = 3 * RW_WIDTH
RW_LORA_IN = 4 * RW_LORA + RW_GATE_LORA
RW_GN_EPS = 64e-5

XA_HEADS = 4
XA_HEAD_DIM = 128
XA_WIDTH = XA_HEADS * XA_HEAD_DIM

LANES = 128
NORM_ROWS = 256
VMEM_LIMIT_BYTES = 56 * 1024 * 1024


def _params(*sem):
    return pltpu.CompilerParams(dimension_semantics=sem, vmem_limit_bytes=VMEM_LIMIT_BYTES)


_NN = (((1,), (0,)), ((), ()))
_NT = (((1,), (1,)), ((), ()))


def _dot1(a, b, dims=_NN):
    return lax.dot_general(a.astype(BF16), b.astype(BF16), dims, preferred_element_type=F32)


def _split2(a):
    hi = a.astype(BF16)
    lo = (a - hi.astype(F32)).astype(BF16)
    return hi, lo


def _dot3(a, b, dims=_NN):
    ah, al = _split2(a)
    bh, bl = _split2(b)
    dg = functools.partial(lax.dot_general, dimension_numbers=dims, preferred_element_type=F32)
    return dg(ah, bh) + (dg(al, bh) + dg(ah, bl))


def _dot_exact_lhs(a01, b):
    a = a01.astype(BF16)
    b1 = b.astype(BF16)
    r1 = b - b1.astype(F32)
    b2 = r1.astype(BF16)
    b3 = (r1 - b2.astype(F32)).astype(BF16)
    dg = functools.partial(lax.dot_general, dimension_numbers=_NN, preferred_element_type=F32)
    return dg(a, b1) + (dg(a, b2) + dg(a, b3))


def _dot_exact_rhs(a, b01):
    b = b01.astype(BF16)
    a1 = a.astype(BF16)
    r1 = a - a1.astype(F32)
    a2 = r1.astype(BF16)
    a3 = (r1 - a2.astype(F32)).astype(BF16)
    dg = functools.partial(lax.dot_general, dimension_numbers=_NN, preferred_element_type=F32)
    return dg(a1, b) + (dg(a2, b) + dg(a3, b))


def _interleave(chains):
    chains = list(chains)
    while chains:
        alive = []
        for ch in chains:
            try:
                next(ch)
                alive.append(ch)
            except StopIteration:
                pass
        chains = alive


def _merge_masks(n, reverse):
    ii = lax.broadcasted_iota(jnp.int32, (n, n), 0)
    jj = lax.broadcasted_iota(jnp.int32, (n, n), 1)
    tri = (ii < jj) if reverse else (ii > jj)
    out = []
    size = 1
    while size < CHUNK:
        out.append(tri & ((ii // (2 * size)) == (jj // (2 * size))) & ((ii // size) != (jj // size)))
        size *= 2
    return out


def _tri_inverse(a, eye_f, level_masks, out):
    t = eye_f - jnp.where(level_masks[0], a, 0.0)
    for m in level_masks[1:]:
        x = _dot1(jnp.where(m, a, 0.0), t)
        yield
        t = t - _dot1(t, x)
        yield
    out.append(t)


def _sigmoid(x):
    return 1.0 / (1.0 + jnp.exp(-x))


def _softplus(x):
    return jnp.maximum(x, 0.0) + jnp.log1p(jnp.exp(-jnp.abs(x)))


def _shift_down(x, row):
    return jnp.where(row == 0, 0.0, pltpu.roll(x, 1, 0))


def _shift_up(x, row):
    n = x.shape[0]
    return jnp.where(row == n - 1, 0.0, pltpu.roll(x, n - 1, 0))


def _norm_mm_body(x_ref, g_ref, w_ref, *rest, act, precise, precise_t, n_first):
    if n_first is not None:
        w2_ref, rest = rest[0], rest[1:]
    if precise:
        wp_ref, o_ref, op_ref, xn_ref = rest
    else:
        o_ref, xn_ref = rest

    @pl.when(pl.program_id(1) == 0)
    def _():
        def rows_step(r, _):
            rows = pl.ds(pl.multiple_of(r * NORM_ROWS, NORM_ROWS), NORM_ROWS)
            x = x_ref[rows, :]
            xn = x * lax.rsqrt(jnp.mean(x * x, axis=-1, keepdims=True) + RMS_EPS) * g_ref[...]
            xn_ref[rows, :] = xn.astype(BF16)
            if precise and precise_t:
                op_ref[:, rows] = _dot3(wp_ref[...], xn, _NT)
            elif precise:
                op_ref[rows, :] = _dot3(xn, wp_ref[...])
            return 0

        lax.fori_loop(0, x_ref.shape[0] // NORM_ROWS, rows_step, 0)

    def project(wt_ref):
        acc = jnp.dot(xn_ref[...], wt_ref[...], preferred_element_type=F32)
        if act == "relu2":
            acc = jnp.square(jnp.maximum(acc, 0.0))
        o_ref[...] = acc.astype(o_ref.dtype)

    if n_first is None:
        project(w_ref)
    else:
        pl.when(pl.program_id(1) < n_first)(lambda: project(w_ref))
        pl.when(pl.program_id(1) >= n_first)(lambda: project(w2_ref))


def norm_mm(x, g, w, layer, w2=None, wp=None, wp_t=None, act=None, out_dtype=F32, tm=1024, tn=1024):
    t, d = x.shape
    n1 = w.shape[2]
    n = n1 + (0 if w2 is None else w2.shape[2])
    tm = min(tm, t)
    tn = min(tn, n1)
    assert t % tm == 0 and n1 % tn == 0 and n % tn == 0 and tm % NORM_ROWS == 0
    precise = wp is not None or wp_t is not None
    n_first = None if w2 is None else n1 // tn
    in_specs = [
        pl.BlockSpec((tm, d), lambda i, j: (i, 0)),
        pl.BlockSpec((1, d), lambda i, j: (0, 0)),
    ]
    args = [x, g.reshape(1, d), w]
    if w2 is None:
        in_specs.append(pl.BlockSpec((None, d, tn), lambda i, j: (layer, 0, j)))
    else:
        in_specs.append(pl.BlockSpec((None, d, tn), lambda i, j: (layer, 0, jnp.minimum(j, n_first - 1))))
        in_specs.append(pl.BlockSpec((None, d, tn), lambda i, j: (layer, 0, jnp.maximum(j - n_first, 0))))
        args.append(w2)
    out_specs = pl.BlockSpec((tm, tn), lambda i, j: (i, j))
    out_shape = jax.ShapeDtypeStruct((t, n), out_dtype)
    if wp is not None:
        npc = wp.shape[1]
        in_specs.append(pl.BlockSpec((d, npc), lambda i, j: (0, 0)))
        out_specs = [out_specs, pl.BlockSpec((tm, npc), lambda i, j: (i, 0))]
        out_shape = [out_shape, jax.ShapeDtypeStruct((t, npc), F32)]
        args.append(wp)
    elif wp_t is not None:
        npc = wp_t.shape[0]
        in_specs.append(pl.BlockSpec((npc, d), lambda i, j: (0, 0)))
        out_specs = [out_specs, pl.BlockSpec((npc, tm), lambda i, j: (0, i))]
        out_shape = [out_shape, jax.ShapeDtypeStruct((npc, t), F32)]
        args.append(wp_t)
    return pl.pallas_call(
        functools.partial(_norm_mm_body, act=act, precise=precise, precise_t=wp_t is not None, n_first=n_first),
        grid=(t // tm, n // tn),
        in_specs=in_specs,
        out_specs=out_specs,
        out_shape=out_shape,
        scratch_shapes=[pltpu.VMEM((tm, d), BF16)],
        compiler_params=_params("parallel", "arbitrary"),
        name="norm_mm",
    )(*args)


def _mm_res_body(a_ref, w_ref, r_ref, o_ref):
    acc = jnp.dot(a_ref[...], w_ref[...], preferred_element_type=F32)

    @pl.when(pl.program_id(2) == 0)
    def _():
        o_ref[...] = r_ref[...] + acc

    @pl.when(pl.program_id(2) > 0)
    def _():
        o_ref[...] += acc


def mm_res(a, w, layer, res, tm=1024, tn=1024, tk=2048):
    t, k = a.shape
    n = w.shape[2]
    tm, tn, tk = min(tm, t), min(tn, n), min(tk, k)
    assert t % tm == 0 and n % tn == 0 and k % tk == 0
    return pl.pallas_call(
        _mm_res_body,
        grid=(t // tm, n // tn, k // tk),
        in_specs=[
            pl.BlockSpec((tm, tk), lambda i, j, kk: (i, kk)),
            pl.BlockSpec((None, tk, tn), lambda i, j, kk: (layer, kk, j)),
            pl.BlockSpec((tm, tn), lambda i, j, kk: (i, j)),
        ],
        out_specs=pl.BlockSpec((tm, tn), lambda i, j, kk: (i, j)),
        out_shape=jax.ShapeDtypeStruct((t, n), F32),
        compiler_params=_params("parallel", "parallel", "arbitrary"),
        name="mm_res",
    )(a, w, res)


def _xattn_body(x_ref, g_ref, wq_ref, kv_ref, wo_ref, qn_ref, kn_ref, o_ref):
    x = x_ref[...]
    xn = x * lax.rsqrt(jnp.mean(x * x, axis=-1, keepdims=True) + RMS_EPS) * g_ref[...]
    q = jnp.dot(xn.astype(BF16), wq_ref[...], preferred_element_type=F32)
    kv = kv_ref[0]
    outs = []
    for h in range(XA_HEADS):
        sl = slice(h * XA_HEAD_DIM, (h + 1) * XA_HEAD_DIM)
        qh = q[:, sl]
        qh = qh * lax.rsqrt(jnp.mean(qh * qh, axis=-1, keepdims=True) + RMS_EPS) * qn_ref[...]
        kh = kv[:, sl]
        kh = kh * lax.rsqrt(jnp.mean(kh * kh, axis=-1, keepdims=True) + RMS_EPS) * kn_ref[...]
        vh = kv[:, XA_WIDTH + h * XA_HEAD_DIM:XA_WIDTH + (h + 1) * XA_HEAD_DIM]
        logits = _dot1(qh, kh, _NT) * (XA_HEAD_DIM ** -0.5)
        m = jnp.max(logits, axis=-1, keepdims=True)
        p = jnp.exp(logits - m)
        s = jnp.sum(p, axis=-1, keepdims=True)
        outs.append(_dot1(p, vh) / s)
    o = jnp.concatenate(outs, axis=-1).astype(BF16)
    o_ref[...] = x + jnp.dot(o, wo_ref[...], preferred_element_type=F32)


def xattn(x, g, wq, kv, wo, layer, qn, kn, batch, ts=512):
    t, d = x.shape
    s = t // batch
    ts = min(ts, s)
    nst = s // ts
    m = kv.shape[1]
    return pl.pallas_call(
        _xattn_body,
        grid=(batch, nst),
        in_specs=[
            pl.BlockSpec((ts, d), lambda b, i: (b * nst + i, 0)),
            pl.BlockSpec((1, d), lambda b, i: (0, 0)),
            pl.BlockSpec((None, d, XA_WIDTH), lambda b, i: (layer, 0, 0)),
            pl.BlockSpec((1, m, 2 * XA_WIDTH), lambda b, i: (b, 0, 0)),
            pl.BlockSpec((None, XA_WIDTH, d), lambda b, i: (layer, 0, 0)),
            pl.BlockSpec((1, XA_HEAD_DIM), lambda b, i: (0, 0)),
            pl.BlockSpec((1, XA_HEAD_DIM), lambda b, i: (0, 0)),
        ],
        out_specs=pl.BlockSpec((ts, d), lambda b, i: (b * nst + i, 0)),
        out_shape=jax.ShapeDtypeStruct((t, d), F32),
        compiler_params=_params("parallel", "parallel"),
        name="xattn",
    )(x, g.reshape(1, d), wq, kv, wo, qn.reshape(1, -1), kn.reshape(1, -1))


def _tri_masks(n, blk):
    ii = lax.broadcasted_iota(jnp.int32, (n, n), 0)
    jj = lax.broadcasted_iota(jnp.int32, (n, n), 1)
    same = (ii // blk) == (jj // blk) if n != blk else None

    def m(c):
        return c if same is None else (c & same)

    return {
        False: (m(ii >= jj), m(ii > jj)),
        True: (m(ii <= jj), m(ii < jj)),
        "eye": ii == jj,
    }


def _seg_cumsum(x, pos, reverse):
    n = x.shape[0]
    sh = 1
    while sh < CHUNK:
        if reverse:
            x = x + jnp.where(pos < CHUNK - sh, pltpu.roll(x, n - sh, 0), 0.0)
        else:
            x = x + jnp.where(pos >= sh, pltpu.roll(x, sh, 0), 0.0)
        sh *= 2
    return x


def _dn_body(q_ref, k_ref, v_ref, gate_ref, ab_ref, cq_ref, ck_ref, cv_ref, alog_ref, dt_ref, gain_ref,
             y_ref, qs_ref, ks_ref, vs_ref, kt_ref, bg_ref, gt_ref, of_ref, ob_ref,
             u_ref, w_ref, qg_ref, qk_ref, kdt_ref, eg_ref):
    s = q_ref.shape[0]
    h = pl.program_id(1)
    row = lax.broadcasted_iota(jnp.int32, (s, 1), 0)
    c = CHUNK

    def conv_silu(x_ref, cw_ref):
        x = x_ref[...]
        w = cw_ref[...]
        y = _shift_down(x, row) * w[0:1] + x * w[1:2] + _shift_up(x, row) * w[2:3]
        return y * _sigmoid(y)

    q = conv_silu(q_ref, cq_ref)
    q = q * lax.rsqrt(jnp.sum(q * q, axis=-1, keepdims=True) + L2_EPS) * (DN_HEAD_DIM ** -0.5)
    qs_ref[...] = q
    k = conv_silu(k_ref, ck_ref)
    k = k * lax.rsqrt(jnp.sum(k * k, axis=-1, keepdims=True) + L2_EPS)
    ks_ref[...] = k
    kt_ref[...] = k.T
    vs_ref[...] = conv_silu(v_ref, cv_ref)

    lane32 = lax.broadcasted_iota(jnp.int32, (1, 4 * DN_HEADS), 1)
    pos_t = lax.broadcasted_iota(jnp.int32, (1, s), 1) % c

    def decay_row(d):
        idx = (2 + d) * DN_HEADS + h
        a_log = jnp.sum(jnp.where(lane32 == idx, alog_ref[...], 0.0), axis=-1, keepdims=True)
        dt = jnp.sum(jnp.where(lane32 == idx, dt_ref[...], 0.0), axis=-1, keepdims=True)
        g = -jnp.exp(a_log) * _softplus(ab_ref[pl.ds(idx, 1), :] + dt)
        sh = 1
        while sh < c:
            if d:
                g = g + jnp.where(pos_t < c - sh, pltpu.roll(g, s - sh, 1), 0.0)
            else:
                g = g + jnp.where(pos_t >= sh, pltpu.roll(g, sh, 1), 0.0)
            sh *= 2
        return g

    rows = [_sigmoid(ab_ref[pl.ds(h, 1), :]), _sigmoid(ab_ref[pl.ds(DN_HEADS + h, 1), :]), decay_row(0), decay_row(1)]
    gt = jnp.concatenate(rows + [jnp.zeros((LANES - len(rows), s), F32)], axis=0)
    gt_ref[...] = gt[0:8, :]
    bg_ref[...] = gt.T
    lane128 = lax.broadcasted_iota(jnp.int32, (1, LANES), 1)

    masks = _tri_masks(PAIR, c)
    merge = {rev: _merge_masks(PAIR, rev) for rev in (False, True)}
    eye_f = jnp.where(masks["eye"], 1.0, 0.0)
    npair = s // PAIR
    group = min(DN_PREP_PAIRS, npair)

    def prep(pidx, d):
        reverse = d == 1
        rows = pl.ds(pl.multiple_of(pidx * PAIR, PAIR), PAIR)
        qp, kp, vp = qs_ref[rows, :], ks_ref[rows, :], vs_ref[rows, :]
        bgp = bg_ref[rows, :]
        beta = bgp[:, d:d + 1]
        gcol = bgp[:, 2 + d:3 + d]
        grow = gt_ref[2 + d:3 + d, rows]
        incl, strict = masks[reverse]
        decay = jnp.where(incl, jnp.exp(jnp.where(incl, gcol - grow, 0.0)), 0.0)
        kb = kp * beta
        kq = _dot1(jnp.concatenate([kb, qp], axis=0), kp, _NT)
        yield
        qk_ref[d, rows, :] = jnp.where(incl, kq[PAIR:] * decay, 0.0).astype(BF16)
        t_inv = []
        yield from _tri_inverse(jnp.where(strict, kq[:PAIR] * decay, 0.0), eye_f, merge[reverse], t_inv)
        egc = jnp.exp(gcol)
        uw = _dot1(t_inv[0], jnp.concatenate([vp * beta, kb * egc], axis=1))
        yield
        u_ref[d, rows, :] = uw[:, :DN_HEAD_DIM]
        w_ref[d, rows, :] = uw[:, DN_HEAD_DIM:].astype(BF16)
        qg_ref[d, rows, :] = (qp * egc).astype(BF16)
        last0, last1 = (0, c) if reverse else (c - 1, PAIR - 1)
        glast = jnp.where(lane128 < c, grow[:, last0:last0 + 1], grow[:, last1:last1 + 1])
        kdt_ref[d, :, rows] = (kt_ref[:, rows] * jnp.exp(glast - grow)).astype(BF16)
        eg_ref[d, :, rows] = jnp.broadcast_to(jnp.exp(glast), (8, PAIR))

    def seq(states, pidx, d):
        pair_rows = pl.ds(pl.multiple_of(pidx * PAIR, PAIR), PAIR)
        kdt = kdt_ref[d, :, pair_rows]
        eg = eg_ref[d, 0:1, pair_rows]
        for half in ((1, 0) if d else (0, 1)):
            r0 = half * c
            rows = pl.ds(pl.multiple_of(pidx * PAIR + r0, c), c)
            ws = _dot1(jnp.concatenate([w_ref[d, rows, :], qg_ref[d, rows, :]], axis=0), states[d])
            yield
            v_new = u_ref[d, rows, :] - ws[:c]
            o = ws[c:] + _dot1(qk_ref[d, rows, r0:r0 + c], v_new)
            if d:
                ob_ref[rows, :] = o
            else:
                of_ref[rows, :] = o
            states[d] = states[d] * eg[:, r0:r0 + 1] + _dot1(kdt[:, r0:r0 + c], v_new)
            yield

    ngroups = npair // group

    def prep_chains(g):
        return [prep(npair - 1 - (g * group + j) if d else g * group + j, d) for j in range(group) for d in range(2)]

    def seq_chain(states, g, d):
        for j in range(group):
            yield from seq(states, npair - 1 - (g * group + j) if d else g * group + j, d)

    def step(g, carry):
        states = list(carry)
        _interleave(prep_chains(g + 1) + [seq_chain(states, g, 0), seq_chain(states, g, 1)])
        return tuple(states)

    zero = jnp.zeros((DN_HEAD_DIM, DN_HEAD_DIM), F32)
    _interleave(prep_chains(0))
    states = list(lax.fori_loop(0, ngroups - 1, step, (zero, zero)))
    _interleave([seq_chain(states, ngroups - 1, 0), seq_chain(states, ngroups - 1, 1)])

    o = of_ref[...] + ob_ref[...]
    o = o * lax.rsqrt(jnp.mean(o * o, axis=-1, keepdims=True) + RMS_EPS) * gain_ref[...]
    gate = gate_ref[...]
    y_ref[...] = (o * (gate * _sigmoid(gate))).astype(y_ref.dtype)


def deltanet(p, ab, conv_t, alog32, dt32, gain, batch):
    t = p.shape[0]
    s = t // batch
    assert s % PAIR == 0 and (s // PAIR) % min(DN_PREP_PAIRS, s // PAIR) == 0
    hd = DN_HEAD_DIM
    col = lambda off: pl.BlockSpec((s, hd), lambda b, h: (b, off + h))
    cw = lambda off: pl.BlockSpec((3, hd), lambda b, h: (0, off + h))
    small = lambda n: pl.BlockSpec((1, n), lambda b, h: (0, 0))
    big = pltpu.VMEM((s, hd), F32)
    half = pltpu.VMEM((2, s, hd), BF16)
    return pl.pallas_call(
        _dn_body,
        grid=(batch, DN_HEADS),
        in_specs=[col(0), col(DN_HEADS), col(2 * DN_HEADS), col(3 * DN_HEADS),
                  pl.BlockSpec((4 * DN_HEADS, s), lambda b, h: (0, b)),
                  cw(0), cw(DN_HEADS), cw(2 * DN_HEADS),
                  small(4 * DN_HEADS), small(4 * DN_HEADS), small(hd)],
        out_specs=pl.BlockSpec((s, hd), lambda b, h: (b, h)),
        out_shape=jax.ShapeDtypeStruct((t, DN_WIDTH), BF16),
        scratch_shapes=[big, big, big, pltpu.VMEM((hd, s), F32), big, pltpu.VMEM((8, s), F32), big, big,
                        pltpu.VMEM((2, s, hd), F32), half, half, half, pltpu.VMEM((2, hd, s), BF16),
                        pltpu.VMEM((2, 8, s), F32)],
        compiler_params=_params("parallel", "parallel"),
        name="deltanet",
    )(p, p, p, p, ab, conv_t, conv_t, conv_t, alog32, dt32, gain.reshape(1, hd))


def _sconv_body(b_ref, c_ref, u_ref, w_ref, y_ref):
    s = b_ref.shape[0]
    row = lax.broadcasted_iota(jnp.int32, (s, 1), 0)
    cu = c_ref[...] * u_ref[...]
    w = w_ref[...]
    y = _shift_down(cu, row) * w[0:1] + cu * w[1:2] + _shift_up(cu, row) * w[2:3]
    y_ref[...] = (b_ref[...] * y).astype(y_ref.dtype)


def short_conv(p, conv_t, col0, batch, tc=256):
    t = p.shape[0]
    s = t // batch
    nct = SC_WIDTH // tc
    base = col0 // tc
    col = lambda off: pl.BlockSpec((s, tc), lambda b, c: (b, base + off * nct + c))
    return pl.pallas_call(
        _sconv_body,
        grid=(batch, nct),
        in_specs=[col(0), col(1), col(2), pl.BlockSpec((3, tc), lambda b, c: (0, c))],
        out_specs=pl.BlockSpec((s, tc), lambda b, c: (b, c)),
        out_shape=jax.ShapeDtypeStruct((t, SC_WIDTH), BF16),
        compiler_params=_params("parallel", "parallel"),
        name="short_conv",
    )(p, p, p, conv_t)


def _dsa_body(*refs, seq):
    q_refs = refs[0:3]
    k_refs = refs[3:6]
    v_refs = refs[6:9]
    qn_ref, kn_ref, bias_ref, y_ref, qs_ref, kpad_ref, vpad_ref, og_ref, lse_ref = refs[9:]
    s = seq
    qb = DSA_QBLK
    side = DSA_SIDE
    width = qb + 2 * side
    kj = lax.broadcasted_iota(jnp.int32, (1, width), 1)
    zpad = jnp.zeros((DSA_PAD, DSA_HEAD_DIM), F32)
    for ref in (kpad_ref, vpad_ref):
        ref[0:DSA_PAD, :] = zpad
        ref[DSA_PAD + s:2 * DSA_PAD + s, :] = zpad

    for gi, (_, dil) in enumerate(DSA_PATTERNS):
        sub = s // dil
        nblk = sub // qb
        q = q_refs[gi][...]
        qs_ref[...] = (q * lax.rsqrt(jnp.mean(q * q, axis=-1, keepdims=True) + RMS_EPS) * qn_ref[...]
                       * (DSA_HEAD_DIM ** -0.5))
        k = k_refs[gi][...]
        kpad_ref[DSA_PAD:DSA_PAD + s, :] = k * lax.rsqrt(jnp.mean(k * k, axis=-1, keepdims=True) + RMS_EPS) * kn_ref[...]
        vpad_ref[DSA_PAD:DSA_PAD + s, :] = v_refs[gi][...]
        bias = bias_ref[0, gi]

        def block(t, gi=gi, dil=dil, sub=sub, nblk=nblk, bias=bias):
            r = t // nblk
            n = t % nblk
            rows = pl.ds(r + n * (qb * dil), qb, stride=dil)
            win = pl.ds(DSA_PAD + r + (n * qb - side) * dil, width, stride=dil)
            logits = _dot1(qs_ref[rows, :], kpad_ref[win, :], _NT) + bias
            yield
            pos = n * qb - side + kj
            logits = jnp.where((pos >= 0) & (pos < sub), logits, NEG_INF)
            m = jnp.max(logits, axis=-1, keepdims=True)
            yield
            p = jnp.exp(logits - m)
            ssum = jnp.sum(p, axis=-1, keepdims=True)
            o = _dot1(p, vpad_ref[win, :])
            yield
            og_ref[gi, rows, :] = o / ssum
            lse_ref[gi, rows, :] = jnp.broadcast_to(m + jnp.log(ssum), (qb, DSA_HEAD_DIM))

        nblocks = dil * nblk

        def blocks_step(i, _, block=block):
            _interleave(block(i * DSA_INTERLEAVE + u) for u in range(DSA_INTERLEAVE))
            return 0

        lax.fori_loop(0, nblocks // DSA_INTERLEAVE, blocks_step, 0)

    lse = [lse_ref[gi] for gi in range(DSA_GROUPS)]
    mx = jnp.maximum(jnp.maximum(lse[0], lse[1]), lse[2])
    ws = [jnp.exp(l - mx) for l in lse]
    num = ws[0] * og_ref[0] + ws[1] * og_ref[1] + ws[2] * og_ref[2]
    y_ref[...] = (num / (ws[0] + ws[1] + ws[2])).astype(y_ref.dtype)


def dilated_attention(p, qn, kn, bias_tab, batch):
    t = p.shape[0]
    s = t // batch
    assert (s // DSA_QBLK) % DSA_INTERLEAVE == 0
    hd = DSA_HEAD_DIM
    nh = DSA_HEADS

    def col(part, gi):
        return pl.BlockSpec((s, hd), lambda b, j: (b, part * nh + gi * DSA_HPG + j))

    in_specs = [col(part, gi) for part in range(3) for gi in range(DSA_GROUPS)]
    in_specs += [pl.BlockSpec((1, hd), lambda b, j: (0, 0)), pl.BlockSpec((1, hd), lambda b, j: (0, 0)),
                 pl.BlockSpec((1, DSA_GROUPS, DSA_QBLK, DSA_QBLK + 2 * DSA_SIDE), lambda b, j: (j, 0, 0, 0))]
    big = pltpu.VMEM((s, hd), F32)
    pad = pltpu.VMEM((s + 2 * DSA_PAD, hd), F32)
    grp = pltpu.VMEM((DSA_GROUPS, s, hd), F32)
    return pl.pallas_call(
        functools.partial(_dsa_body, seq=s),
        grid=(batch, DSA_HPG),
        in_specs=in_specs,
        out_specs=pl.BlockSpec((s, hd), lambda b, j: (b, j)),
        out_shape=jax.ShapeDtypeStruct((t, DSA_HPG * hd), BF16),
        scratch_shapes=[big, pad, pad, grp, grp],
        compiler_params=_params("parallel", "parallel"),
        name="dilated_attention",
    )(*([p] * 9), qn.reshape(1, hd), kn.reshape(1, hd), bias_tab)


def _t5_bucket(rel):
    half = REL_BUCKETS // 2
    max_exact = half // 2
    n = np.abs(rel)
    scaled = (np.log(np.maximum(n, max_exact).astype(np.float32) / np.float32(max_exact))
              / np.float32(math.log(REL_MAX_DIST / max_exact)))
    large = np.minimum(max_exact + (scaled * np.float32(half - max_exact)).astype(np.int32), half - 1)
    return np.where(rel > 0, half, 0) + np.where(n < max_exact, n, large)


def _dsa_bias_table(rel_bias):
    width = DSA_QBLK + 2 * DSA_SIDE
    tabs = []
    for gi, (_, dil) in enumerate(DSA_PATTERNS):
        offs = np.arange(-DSA_SIDE, DSA_SIDE + 1, dtype=np.int32) * dil
        band = rel_bias[_t5_bucket(offs)][:, gi * DSA_HPG:(gi + 1) * DSA_HPG].astype(F32)
        fill = jnp.full((DSA_QBLK - 1, DSA_HPG), NEG_INF, F32)
        line = jnp.concatenate([fill, band, fill], axis=0)
        tabs.append(jnp.stack([line[DSA_QBLK - 1 - q:DSA_QBLK - 1 - q + width] for q in range(DSA_QBLK)]))
    return jnp.transpose(jnp.stack(tabs), (3, 0, 1, 2))


def _head_block_diag():
    ii = lax.broadcasted_iota(jnp.int32, (LANES, LANES), 0)
    jj = lax.broadcasted_iota(jnp.int32, (LANES, LANES), 1)
    return (ii // RW_HEAD_DIM) == (jj // RW_HEAD_DIM)


def _rw_prep_body(r_ref, k_ref, v_ref, lo_ref, mur_ref, muk_ref, muv_ref, mulo_ref, w0_ref, w2_ref, a0_ref,
                  a2_ref, g2_ref, kk_ref, ka_ref, rk_ref,
                  ro_ref, vo_ref, kko_ref, bon_ref, gate_ref, lwf_ref, lwb_ref, kdf_ref, kdb_ref, bbf_ref, bbb_ref):
    s = r_ref.shape[0]
    row = lax.broadcasted_iota(jnp.int32, (s, 1), 0)

    def mix(t, mu):
        return t + mu * (0.5 * (_shift_down(t, row) + _shift_up(t, row)) - t)

    r = mix(r_ref[...], mur_ref[...])
    kr = mix(k_ref[...], muk_ref[...])
    v = mix(v_ref[...], muv_ref[...])
    lo = mix(lo_ref[...], mulo_ref[...])
    bd = jnp.where(_head_block_diag(), 1.0, 0.0)

    def head_sum(t):
        return _dot_exact_rhs(t, bd)

    kk = kr * kk_ref[...]
    kk = kk * lax.rsqrt(head_sum(kk * kk) + L2_EPS)
    gd = lo[:, 4 * RW_LORA:]
    gate_ref[...] = _dot3(_sigmoid(gd), g2_ref[...])
    ro_ref[...] = r
    vo_ref[...] = v
    kko_ref[...] = kk
    bonus = jnp.zeros_like(r)
    outs = ((lwf_ref, kdf_ref, bbf_ref), (lwb_ref, kdb_ref, bbb_ref))
    for d in range(2):
        wd = lo[:, d * RW_LORA:(d + 1) * RW_LORA]
        ad = lo[:, (2 + d) * RW_LORA:(3 + d) * RW_LORA]
        w_log = -_softplus(-(w0_ref[d:d + 1, :] + _dot3(jnp.tanh(wd), w2_ref[d]))) - 0.5
        a = _sigmoid(a0_ref[d:d + 1, :] + _dot3(ad, a2_ref[d]))
        kd = kr * (1.0 + (a - 1.0) * ka_ref[...])
        lw_ref, kd_ref, bb_ref = outs[d]
        lw_ref[...] = -jnp.exp(w_log)
        kd_ref[...] = kd
        bb_ref[...] = kk * a
        bonus = bonus + head_sum(r * kd * rk_ref[...]) * v
    bon_ref[...] = bonus


def rwkv_prep(p, lora_in, mu, w0, w2, a0, a2, g2, k_k, k_a, r_k, col0, batch):
    t = p.shape[0]
    s = t // batch
    nct = RW_WIDTH // LANES
    base = col0 // LANES
    nlo = RW_LORA_IN
    col = lambda off: pl.BlockSpec((s, LANES), lambda b, c: (b, base + off * nct + c))
    vec = lambda off: pl.BlockSpec((1, LANES), lambda b, c: (0, off * nct + c))
    mu_main = mu[:RW_MAIN].reshape(1, RW_MAIN)
    mu_lo = mu[RW_MAIN:].reshape(1, nlo)
    out_spec = pl.BlockSpec((s, LANES), lambda b, c: (b, c))
    n_out = 11
    return pl.pallas_call(
        _rw_prep_body,
        grid=(batch, nct),
        in_specs=[col(0), col(1), col(2),
                  pl.BlockSpec((s, nlo), lambda b, c: (b, 0)),
                  vec(0), vec(1), vec(2),
                  pl.BlockSpec((1, nlo), lambda b, c: (0, 0)),
                  pl.BlockSpec((2, LANES), lambda b, c: (0, c)),
                  pl.BlockSpec((2, RW_LORA, LANES), lambda b, c: (0, 0, c)),
                  pl.BlockSpec((2, LANES), lambda b, c: (0, c)),
                  pl.BlockSpec((2, RW_LORA, LANES), lambda b, c: (0, 0, c)),
                  pl.BlockSpec((RW_GATE_LORA, LANES), lambda b, c: (0, c)),
                  vec(0), vec(0), vec(0)],
        out_specs=[out_spec] * n_out,
        out_shape=[jax.ShapeDtypeStruct((t, RW_WIDTH), F32)] * n_out,
        compiler_params=_params("parallel", "parallel"),
        name="rwkv_prep",
    )(p, p, p, lora_in, mu_main, mu_main, mu_main, mu_lo, w0, w2, a0, a2, g2,
      k_k.reshape(1, RW_WIDTH), k_a.reshape(1, RW_WIDTH), r_k.reshape(1, RW_WIDTH))


def _rw_scan_body(r_ref, v_ref, kk_ref, bon_ref, gate_ref, lwf_ref, lwb_ref, kdf_ref, kdb_ref, bbf_ref, bbb_ref,
                  lnw_ref, lnb_ref, y_ref, cum_ref, cumt_ref, kdt_ref, bbt_ref, yf_ref, yb_ref,
                  tr_ref, tc_ref, rbk_ref, bkt_ref):
    s = r_ref.shape[0]
    c = CHUNK
    c2 = 2 * c
    row = lax.broadcasted_iota(jnp.int32, (s, 1), 0)
    pos = row % c
    lw_refs, kd_refs, bb_refs = (lwf_ref, lwb_ref), (kdf_ref, kdb_ref), (bbf_ref, bbb_ref)
    for d in range(2):
        cum = _seg_cumsum(lw_refs[d][...], pos, d == 1)
        cum_ref[d] = cum
        cumt_ref[d] = cum.T
        kdt_ref[d] = kd_refs[d][...].T
        bbt_ref[d] = bb_refs[d][...].T

    masks = _tri_masks(c2, c)
    merge = {rev: _merge_masks(c2, rev) for rev in (False, True)}
    eye2_f = jnp.where(masks["eye"], 1.0, 0.0)
    head0 = lax.broadcasted_iota(jnp.int32, (1, LANES), 1) < RW_HEAD_DIM
    head_bd = _head_block_diag()
    npair = s // PAIR
    group = min(RW_PREP_PAIRS, npair)

    def stack(x):
        return jnp.concatenate([jnp.where(head0, x, 0.0), jnp.where(head0, 0.0, x)], axis=0)

    def prep(pidx, d, half):
        reverse = d == 1
        incl2, strict2 = masks[reverse]
        cidx = 2 * pidx + half
        cols = pl.ds(pl.multiple_of(pidx * PAIR, PAIR), PAIR)
        rows = pl.ds(pl.multiple_of(cidx * c, c), c)
        hs = slice(half * c, (half + 1) * c)
        r, v, kk = r_ref[rows, :], v_ref[rows, :], kk_ref[rows, :]
        lw, kd, bb = lw_refs[d][rows, :], kd_refs[d][rows, :], bb_refs[d][rows, :]
        cum = cum_ref[d, rows, :]
        e_neg = jnp.exp(-cum)
        a_s = stack(-kk * jnp.exp(cum - lw))
        r_s = stack(r * jnp.exp(cum))
        ar = jnp.concatenate([a_s, r_s], axis=0)
        bk = jnp.concatenate([stack(bb * e_neg), stack(kd * e_neg)], axis=0)
        g = _dot1(ar, bk, _NT)
        yield
        rbk_ref[d, cidx] = jnp.concatenate([jnp.where(incl2, g[c2:, :c2], 0.0),
                                            jnp.where(incl2, g[c2:, c2:], 0.0)], axis=1).astype(BF16)
        av = _dot1(jnp.where(strict2, g[:c2, c2:], 0.0), stack(v))
        yield
        t_inv = []
        yield from _tri_inverse(-jnp.where(strict2, g[:c2, :c2], 0.0), eye2_f, merge[reverse], t_inv)
        tt = _dot1(t_inv[0], jnp.concatenate([a_s, av], axis=1))
        yield
        tr_ref[d, cidx] = jnp.concatenate([tt[:, :LANES], r_s], axis=0).astype(BF16)
        tc_ref[d, cidx] = tt[:, LANES:].astype(BF16)
        cum_t = cumt_ref[d, :, cols][:, hs]
        last = 0 if reverse else c - 1
        e_out_t = jnp.exp(cum_t[:, last:last + 1] - cum_t)
        bkt_ref[d, cidx] = jnp.concatenate([bbt_ref[d, :, cols][:, hs] * e_out_t,
                                            kdt_ref[d, :, cols][:, hs] * e_out_t], axis=1).astype(BF16)

    def seq(states, pidx, d):
        cum_t2 = cumt_ref[d, :, pl.ds(pl.multiple_of(pidx * PAIR, PAIR), PAIR)]
        for half in ((1, 0) if d else (0, 1)):
            cidx = 2 * pidx + half
            rows = pl.ds(pl.multiple_of(cidx * c, c), c)
            v = v_ref[rows, :]
            x = _dot1(tr_ref[d, cidx], states[d])
            yield
            ps = x[:c2] + tc_ref[d, cidx].astype(F32)
            os_ = x[c2:] + _dot1(rbk_ref[d, cidx], jnp.concatenate([ps, stack(v)], axis=0))
            y = os_[:c] + os_[c:]
            if d:
                yb_ref[rows, :] = y
            else:
                yf_ref[rows, :] = y
            pv = jnp.concatenate([ps[:c] + ps[c:], v], axis=0)
            last = half * c + (0 if d else c - 1)
            e_tot = jnp.exp(cum_t2[:, last:last + 1])
            states[d] = states[d] * e_tot + jnp.where(head_bd, _dot1(bkt_ref[d, cidx], pv), 0.0)
            yield

    ngroups = npair // group

    def prep_chains(g):
        return [prep(npair - 1 - (g * group + j) if d else g * group + j, d, half)
                for j in range(group) for d in range(2) for half in range(2)]

    def seq_chain(states, g, d):
        for j in range(group):
            yield from seq(states, npair - 1 - (g * group + j) if d else g * group + j, d)

    def step(g, carry):
        states = list(carry)
        _interleave(prep_chains(g + 1) + [seq_chain(states, g, 0), seq_chain(states, g, 1)])
        return tuple(states)

    zero = jnp.zeros((LANES, LANES), F32)
    _interleave(prep_chains(0))
    states = list(lax.fori_loop(0, ngroups - 1, step, (zero, zero)))
    _interleave([seq_chain(states, ngroups - 1, 0), seq_chain(states, ngroups - 1, 1)])

    bd = jnp.where(head_bd, 1.0, 0.0)
    y = yf_ref[...] + yb_ref[...]
    mean = _dot_exact_rhs(y, bd) * (1.0 / RW_HEAD_DIM)
    yc = y - mean
    var = _dot_exact_rhs(yc * yc, bd) * (1.0 / RW_HEAD_DIM)
    yn = yc * lax.rsqrt(var + RW_GN_EPS) * lnw_ref[...] + lnb_ref[...]
    y_ref[...] = ((yn + bon_ref[...]) * gate_ref[...]).astype(y_ref.dtype)


def rwkv_scan(prep, ln_w, ln_b, batch):
    t = prep[0].shape[0]
    s = t // batch
    assert s % PAIR == 0 and (s // PAIR) % min(RW_PREP_PAIRS, s // PAIR) == 0
    nct = RW_WIDTH // LANES
    nchunk = s // CHUNK
    blk = pl.BlockSpec((s, LANES), lambda b, c: (b, c))
    vec = pl.BlockSpec((1, LANES), lambda b, c: (0, c))
    tsp = pltpu.VMEM((2, LANES, s), F32)
    big = pltpu.VMEM((s, LANES), F32)
    per_chunk = lambda rows, cols: pltpu.VMEM((2, nchunk, rows, cols), BF16)
    return pl.pallas_call(
        _rw_scan_body,
        grid=(batch, nct),
        in_specs=[blk] * 11 + [vec, vec],
        out_specs=blk,
        out_shape=jax.ShapeDtypeStruct((t, RW_WIDTH), BF16),
        scratch_shapes=[pltpu.VMEM((2, s, LANES), F32), tsp, tsp, tsp, big, big,
                        per_chunk(2 * PAIR, LANES), per_chunk(PAIR, LANES), per_chunk(PAIR, 2 * LANES),
                        per_chunk(PAIR, LANES)],
        compiler_params=_params("parallel", "parallel"),
        name="rwkv_scan",
    )(*prep, ln_w.reshape(1, RW_WIDTH), ln_b.reshape(1, RW_WIDTH))


def even_mixer(x, g, w_dn, w_sc, w_ab, w_out, layer, conv_qkv, a_log, dt_bias, out_gain, conv_sc, batch):
    p, ab = norm_mm(x, g, w_dn, layer, w2=w_sc, wp_t=w_ab.T)
    zeros = jnp.zeros((2 * DN_HEADS,), F32)
    alog32 = jnp.concatenate([zeros, a_log.reshape(-1)]).reshape(1, -1)
    dt32 = jnp.concatenate([zeros, dt_bias.reshape(-1)]).reshape(1, -1)
    y_dn = deltanet(p, ab, conv_qkv.T, alog32, dt32, out_gain, batch)
    y_sc = short_conv(p, conv_sc.T, 4 * DN_WIDTH, batch)
    y = jnp.concatenate([y_dn, y_sc], axis=1)
    return mm_res(y, w_out, layer, x)


def odd_mixer(x, g, w_main, w_lora, w_out, layer, qn, kn, bias_tab, mu, w0, w2, a0, a2, g2, k_k, k_a, r_k, ln_w, ln_b,
              batch):
    p, lora_in = norm_mm(x, g, w_main, layer, wp=w_lora)
    y_c = dilated_attention(p, qn, kn, bias_tab, batch)
    prep = rwkv_prep(p, lora_in, mu, w0, w2, a0, a2, g2, k_k, k_a, r_k, 3 * DSA_QKV, batch)
    y_d = rwkv_scan(prep, ln_w, ln_b, batch)
    y = jnp.concatenate([y_c, y_d], axis=1)
    return mm_res(y, w_out, layer, x)


def kernel(x, mem, rel_bias, norm_mix, norm_xattn, norm_mem, norm_ffn, xa_wq, xa_wk, xa_wv, xa_wo, xa_qn, xa_kn, ffn_w1, ffn_w2, ev_w_in, ev_w_out, dn_conv, dn_a_log, dn_dt_bias, dn_norm, sc_conv, od_w_in, od_w_out, ca_qn, ca_kn, rw_mu, rw_w0, rw_w2, rw_a0, rw_a2, rw_g2, rw_k_k, rw_k_a, rw_r_k, rw_ln_w, rw_ln_b):
    batch, seq, d = x.shape
    n_mem = mem.shape[1]
    xf = x.reshape(batch * seq, d)
    memf = mem.reshape(batch * n_mem, d)
    bias_tab = _dsa_bias_table(rel_bias)
    c_ab = 4 * DN_WIDTH
    c_sc = c_ab + 4 * DN_HEADS
    c_lo = 3 * DSA_QKV + RW_MAIN
    ev_dn, ev_sc = ev_w_in[:, :, :c_ab].astype(BF16), ev_w_in[:, :, c_sc:].astype(BF16)
    od_main = od_w_in[:, :, :c_lo].astype(BF16)
    ev_out, od_out = ev_w_out.astype(BF16), od_w_out.astype(BF16)
    w_q, w_o = xa_wq.astype(BF16), xa_wo.astype(BF16)
    w_kv = jnp.concatenate([xa_wk, xa_wv], axis=2).astype(BF16)
    w_1, w_2 = ffn_w1.astype(BF16), ffn_w2.astype(BF16)
    for layer in range(DEPTH):
        i = layer // 2
        if layer % 2 == 0:
            xf = even_mixer(xf, norm_mix[layer], ev_dn, ev_sc, ev_w_in[i, :, c_ab:c_sc], ev_out, i, dn_conv[i],
                            dn_a_log[i], dn_dt_bias[i], dn_norm[i], sc_conv[i], batch)
        else:
            xf = odd_mixer(xf, norm_mix[layer], od_main, od_w_in[i, :, c_lo:], od_out, i, ca_qn[i], ca_kn[i], bias_tab,
                           rw_mu[i], rw_w0[i], rw_w2[i], rw_a0[i], rw_a2[i], rw_g2[i], rw_k_k[i], rw_k_a[i], rw_r_k[i],
                           rw_ln_w[i], rw_ln_b[i], batch)
        kv = norm_mm(memf, norm_mem[layer], w_kv, layer).reshape(batch, n_mem, 2 * XA_WIDTH)
        xf = xattn(xf, norm_xattn[layer], w_q, kv, w_o, layer, xa_qn[layer], xa_kn[layer], batch)
        h1 = norm_mm(xf, norm_ffn[layer], w_1, layer, act="relu2", out_dtype=BF16)
        xf = mm_res(h1, w_2, layer, xf)
    return xf.reshape(batch, seq, d)
```

```python
import functools
import math

import jax
import jax.numpy as jnp
import numpy as np
from jax import lax
from jax.experimental import pallas as pl
from jax.experimental.pallas import tpu as pltpu

F32 = jnp.float32
BF16 = jnp.bfloat16

D_MODEL = 2048
DEPTH = 4
RMS_EPS = 1e-6
L2_EPS = 1e-6

DN_HEADS = 8
DN_HEAD_DIM = 128
DN_WIDTH = DN_HEADS * DN_HEAD_DIM
SC_WIDTH = D_MODEL - DN_WIDTH
CHUNK = 64
PAIR = 2 * CHUNK
DN_PREP_PAIRS = 4
RW_PREP_PAIRS = 2

DSA_PATTERNS = ((128, 1), (512, 4), (2048, 16))
DSA_GROUPS = len(DSA_PATTERNS)
DSA_HPG = 4
DSA_HEAD_DIM = 128
DSA_HEADS = DSA_GROUPS * DSA_HPG
DSA_QKV = DSA_HEADS * DSA_HEAD_DIM
DSA_SIDE = 64
DSA_QBLK = 128
DSA_PAD = DSA_SIDE * max(d for _, d in DSA_PATTERNS)
DSA_INTERLEAVE = 4
REL_BUCKETS = 32
REL_MAX_DIST = 1024
NEG_INF = -1e30

RW_HEADS = 8
RW_HEAD_DIM = 64
RW_WIDTH = RW_HEADS * RW_HEAD_DIM
RW_LORA = 64
RW_GATE_LORA = 128
RW_MAIN = 3 * RW_WIDTH
RW_LORA_IN = 4 * RW_LORA + RW_GATE_LORA
RW_GN_EPS = 64e-5

XA_HEADS = 4
XA_HEAD_DIM = 128
XA_WIDTH = XA_HEADS * XA_HEAD_DIM

LANES = 128
NORM_ROWS = 256
VMEM_LIMIT_BYTES = 56 * 1024 * 1024


def _params(*sem):
    return pltpu.CompilerParams(dimension_semantics=sem, vmem_limit_bytes=VMEM_LIMIT_BYTES)


_NN = (((1,), (0,)), ((), ()))
_NT = (((1,), (1,)), ((), ()))


def _dot1(a, b, dims=_NN):
    return lax.dot_general(a.astype(BF16), b.astype(BF16), dims, preferred_element_type=F32)


def _split2(a):
    hi = a.astype(BF16)
    lo = (a - hi.astype(F32)).astype(BF16)
    return hi, lo


def _dot3(a, b, dims=_NN):
    ah, al = _split2(a)
    bh, bl = _split2(b)
    dg = functools.partial(lax.dot_general, dimension_numbers=dims, preferred_element_type=F32)
    return dg(ah, bh) + (dg(al, bh) + dg(ah, bl))


def _dot_exact_lhs(a01, b):
    a = a01.astype(BF16)
    b1 = b.astype(BF16)
    r1 = b - b1.astype(F32)
    b2 = r1.astype(BF16)
    b3 = (r1 - b2.astype(F32)).astype(BF16)
    dg = functools.partial(lax.dot_general, dimension_numbers=_NN, preferred_element_type=F32)
    return dg(a, b1) + (dg(a, b2) + dg(a, b3))


def _dot_exact_rhs(a, b01):
    b = b01.astype(BF16)
    a1 = a.astype(BF16)
    r1 = a - a1.astype(F32)
    a2 = r1.astype(BF16)
    a3 = (r1 - a2.astype(F32)).astype(BF16)
    dg = functools.partial(lax.dot_general, dimension_numbers=_NN, preferred_element_type=F32)
    return dg(a1, b) + (dg(a2, b) + dg(a3, b))


def _interleave(chains):
    chains = list(chains)
    while chains:
        alive = []
        for ch in chains:
            try:
                next(ch)
                alive.append(ch)
            except StopIteration:
                pass
        chains = alive


def _merge_masks(n, reverse):
    ii = lax.broadcasted_iota(jnp.int32, (n, n), 0)
    jj = lax.broadcasted_iota(jnp.int32, (n, n), 1)
    tri = (ii < jj) if reverse else (ii > jj)
    out = []
    size = 1
    while size < CHUNK:
        out.append(tri & ((ii // (2 * size)) == (jj // (2 * size))) & ((ii // size) != (jj // size)))
        size *= 2
    return out


def _tri_inverse(a, eye_f, level_masks, out):
    t = eye_f - jnp.where(level_masks[0], a, 0.0)
    for m in level_masks[1:]:
        x = _dot1(jnp.where(m, a, 0.0), t)
        yield
        t = t - _dot1(t, x)
        yield
    out.append(t)


def _sigmoid(x):
    return 1.0 / (1.0 + jnp.exp(-x))


def _softplus(x):
    return jnp.maximum(x, 0.0) + jnp.log1p(jnp.exp(-jnp.abs(x)))


def _shift_down(x, row):
    return jnp.where(row == 0, 0.0, pltpu.roll(x, 1, 0))


def _shift_up(x, row):
    n = x.shape[0]
    return jnp.where(row == n - 1, 0.0, pltpu.roll(x, n - 1, 0))


def _norm_mm_body(x_ref, g_ref, w_ref, *rest, act, precise, precise_t, n_first):
    if n_first is not None:
        w2_ref, rest = rest[0], rest[1:]
    if precise:
        wp_ref, o_ref, op_ref, xn_ref = rest
    else:
        o_ref, xn_ref = rest

    @pl.when(pl.program_id(1) == 0)
    def _():
        def rows_step(r, _):
            rows = pl.ds(pl.multiple_of(r * NORM_ROWS, NORM_ROWS), NORM_ROWS)
            x = x_ref[rows, :]
            xn = x * lax.rsqrt(jnp.mean(x * x, axis=-1, keepdims=True) + RMS_EPS) * g_ref[...]
            xn_ref[rows, :] = xn.astype(BF16)
            if precise and precise_t:
                op_ref[:, rows] = _dot1(wp_ref[...], xn, _NT)
            elif precise:
                op_ref[rows, :] = _dot1(xn, wp_ref[...])
            return 0

        lax.fori_loop(0, x_ref.shape[0] // NORM_ROWS, rows_step, 0)

    def project(wt_ref):
        acc = jnp.dot(xn_ref[...], wt_ref[...], preferred_element_type=F32)
        if act == "relu2":
            acc = jnp.square(jnp.maximum(acc, 0.0))
        o_ref[...] = acc.astype(o_ref.dtype)

    if n_first is None:
        project(w_ref)
    else:
        pl.when(pl.program_id(1) < n_first)(lambda: project(w_ref))
        pl.when(pl.program_id(1) >= n_first)(lambda: project(w2_ref))


def norm_mm(x, g, w, layer, ncols=None, w2=None, wp=None, wp_t=None, act=None, out_dtype=F32, tm=1024, tn=1024):
    t, d = x.shape
    n1 = w.shape[2] if ncols is None else ncols
    n = n1 + (0 if w2 is None else w2.shape[2])
    tm = min(tm, t)
    tn = min(tn, n1)
    assert t % tm == 0 and n1 % tn == 0 and n % tn == 0 and tm % NORM_ROWS == 0
    precise = wp is not None or wp_t is not None
    n_first = None if w2 is None else n1 // tn
    in_specs = [
        pl.BlockSpec((tm, d), lambda i, j: (i, 0)),
        pl.BlockSpec((1, d), lambda i, j: (0, 0)),
    ]
    args = [x, g.reshape(1, d), w]
    if w2 is None:
        in_specs.append(pl.BlockSpec((None, d, tn), lambda i, j: (layer, 0, j)))
    else:
        in_specs.append(pl.BlockSpec((None, d, tn), lambda i, j: (layer, 0, jnp.minimum(j, n_first - 1))))
        in_specs.append(pl.BlockSpec((None, d, tn), lambda i, j: (layer, 0, jnp.maximum(j - n_first, 0))))
        args.append(w2)
    out_specs = pl.BlockSpec((tm, tn), lambda i, j: (i, j))
    out_shape = jax.ShapeDtypeStruct((t, n), out_dtype)
    if wp is not None:
        npc = wp.shape[1]
        in_specs.append(pl.BlockSpec((d, npc), lambda i, j: (0, 0)))
        out_specs = [out_specs, pl.BlockSpec((tm, npc), lambda i, j: (i, 0))]
        out_shape = [out_shape, jax.ShapeDtypeStruct((t, npc), F32)]
        args.append(wp)
    elif wp_t is not None:
        npc = wp_t.shape[0]
        in_specs.append(pl.BlockSpec((npc, d), lambda i, j: (0, 0)))
        out_specs = [out_specs, pl.BlockSpec((npc, tm), lambda i, j: (0, i))]
        out_shape = [out_shape, jax.ShapeDtypeStruct((npc, t), F32)]
        args.append(wp_t)
    return pl.pallas_call(
        functools.partial(_norm_mm_body, act=act, precise=precise, precise_t=wp_t is not None, n_first=n_first),
        grid=(t // tm, n // tn),
        in_specs=in_specs,
        out_specs=out_specs,
        out_shape=out_shape,
        scratch_shapes=[pltpu.VMEM((tm, d), BF16)],
        compiler_params=_params("parallel", "arbitrary"),
        name="norm_mm",
    )(*args)


def _mm_res_body(a_ref, w_ref, r_ref, o_ref):
    acc = jnp.dot(a_ref[...], w_ref[...], preferred_element_type=F32)

    @pl.when(pl.program_id(2) == 0)
    def _():
        o_ref[...] = r_ref[...] + acc

    @pl.when(pl.program_id(2) > 0)
    def _():
        o_ref[...] += acc


def mm_res(a, w, layer, res, tm=1024, tn=1024, tk=2048):
    t, k = a.shape
    n = w.shape[2]
    tm, tn, tk = min(tm, t), min(tn, n), min(tk, k)
    assert t % tm == 0 and n % tn == 0 and k % tk == 0
    return pl.pallas_call(
        _mm_res_body,
        grid=(t // tm, n // tn, k // tk),
        in_specs=[
            pl.BlockSpec((tm, tk), lambda i, j, kk: (i, kk)),
            pl.BlockSpec((None, tk, tn), lambda i, j, kk: (layer, kk, j)),
            pl.BlockSpec((tm, tn), lambda i, j, kk: (i, j)),
        ],
        out_specs=pl.BlockSpec((tm, tn), lambda i, j, kk: (i, j)),
        out_shape=jax.ShapeDtypeStruct((t, n), F32),
        compiler_params=_params("parallel", "parallel", "arbitrary"),
        name="mm_res",
    )(a, w, res)


def _xattn_body(x_ref, g_ref, wq_ref, kv_ref, wo_ref, qn_ref, kn_ref, o_ref):
    x = x_ref[...]
    xn = x * lax.rsqrt(jnp.mean(x * x, axis=-1, keepdims=True) + RMS_EPS) * g_ref[...]
    q = jnp.dot(xn.astype(BF16), wq_ref[...], preferred_element_type=F32)
    kv = kv_ref[0]
    outs = []
    for h in range(XA_HEADS):
        sl = slice(h * XA_HEAD_DIM, (h + 1) * XA_HEAD_DIM)
        qh = q[:, sl]
        qh = qh * lax.rsqrt(jnp.mean(qh * qh, axis=-1, keepdims=True) + RMS_EPS) * qn_ref[...]
        kh = kv[:, sl]
        kh = kh * lax.rsqrt(jnp.mean(kh * kh, axis=-1, keepdims=True) + RMS_EPS) * kn_ref[...]
        vh = kv[:, XA_WIDTH + h * XA_HEAD_DIM:XA_WIDTH + (h + 1) * XA_HEAD_DIM]
        logits = _dot1(qh, kh, _NT) * (XA_HEAD_DIM ** -0.5)
        m = jnp.max(logits, axis=-1, keepdims=True)
        p = jnp.exp(logits - m)
        s = jnp.sum(p, axis=-1, keepdims=True)
        outs.append(_dot1(p, vh) / s)
    o = jnp.concatenate(outs, axis=-1).astype(BF16)
    o_ref[...] = x + jnp.dot(o, wo_ref[...], preferred_element_type=F32)


def xattn(x, g, wq, kv, wo, layer, qn, kn, batch, ts=512):
    t, d = x.shape
    s = t // batch
    ts = min(ts, s)
    nst = s // ts
    m = kv.shape[1]
    return pl.pallas_call(
        _xattn_body,
        grid=(batch, nst),
        in_specs=[
            pl.BlockSpec((ts, d), lambda b, i: (b * nst + i, 0)),
            pl.BlockSpec((1, d), lambda b, i: (0, 0)),
            pl.BlockSpec((None, d, XA_WIDTH), lambda b, i: (layer, 0, 0)),
            pl.BlockSpec((1, m, 2 * XA_WIDTH), lambda b, i: (b, 0, 0)),
            pl.BlockSpec((None, XA_WIDTH, d), lambda b, i: (layer, 0, 0)),
            pl.BlockSpec((1, XA_HEAD_DIM), lambda b, i: (0, 0)),
            pl.BlockSpec((1, XA_HEAD_DIM), lambda b, i: (0, 0)),
        ],
        out_specs=pl.BlockSpec((ts, d), lambda b, i: (b * nst + i, 0)),
        out_shape=jax.ShapeDtypeStruct((t, d), F32),
        compiler_params=_params("parallel", "parallel"),
        name="xattn",
    )(x, g.reshape(1, d), wq, kv, wo, qn.reshape(1, -1), kn.reshape(1, -1))


def _tri_masks(n, blk):
    ii = lax.broadcasted_iota(jnp.int32, (n, n), 0)
    jj = lax.broadcasted_iota(jnp.int32, (n, n), 1)
    same = (ii // blk) == (jj // blk) if n != blk else None

    def m(c):
        return c if same is None else (c & same)

    return {
        False: (m(ii >= jj), m(ii > jj)),
        True: (m(ii <= jj), m(ii < jj)),
        "eye": ii == jj,
    }


def _seg_cumsum(x, pos, reverse):
    n = x.shape[0]
    sh = 1
    while sh < CHUNK:
        if reverse:
            x = x + jnp.where(pos < CHUNK - sh, pltpu.roll(x, n - sh, 0), 0.0)
        else:
            x = x + jnp.where(pos >= sh, pltpu.roll(x, sh, 0), 0.0)
        sh *= 2
    return x


def _dn_body(q_ref, k_ref, v_ref, gate_ref, ab_ref, cq_ref, ck_ref, cv_ref, alog_ref, dt_ref, gain_ref,
             y_ref, qs_ref, ks_ref, vs_ref, kt_ref, bg_ref, gt_ref, of_ref, ob_ref,
             u_ref, w_ref, qg_ref, qk_ref, kdt_ref, eg_ref):
    s = q_ref.shape[0]
    h = pl.program_id(1)
    row = lax.broadcasted_iota(jnp.int32, (s, 1), 0)
    c = CHUNK

    def conv_silu(x_ref, cw_ref):
        x = x_ref[...]
        w = cw_ref[...]
        y = _shift_down(x, row) * w[0:1] + x * w[1:2] + _shift_up(x, row) * w[2:3]
        return y * _sigmoid(y)

    q = conv_silu(q_ref, cq_ref)
    q = q * lax.rsqrt(jnp.sum(q * q, axis=-1, keepdims=True) + L2_EPS) * (DN_HEAD_DIM ** -0.5)
    qs_ref[...] = q
    k = conv_silu(k_ref, ck_ref)
    k = k * lax.rsqrt(jnp.sum(k * k, axis=-1, keepdims=True) + L2_EPS)
    ks_ref[...] = k
    kt_ref[...] = k.T
    vs_ref[...] = conv_silu(v_ref, cv_ref)

    lane32 = lax.broadcasted_iota(jnp.int32, (1, 4 * DN_HEADS), 1)
    pos_t = lax.broadcasted_iota(jnp.int32, (1, s), 1) % c

    def decay_row(d):
        idx = (2 + d) * DN_HEADS + h
        a_log = jnp.sum(jnp.where(lane32 == idx, alog_ref[...], 0.0), axis=-1, keepdims=True)
        dt = jnp.sum(jnp.where(lane32 == idx, dt_ref[...], 0.0), axis=-1, keepdims=True)
        g = -jnp.exp(a_log) * _softplus(ab_ref[pl.ds(idx, 1), :] + dt)
        sh = 1
        while sh < c:
            if d:
                g = g + jnp.where(pos_t < c - sh, pltpu.roll(g, s - sh, 1), 0.0)
            else:
                g = g + jnp.where(pos_t >= sh, pltpu.roll(g, sh, 1), 0.0)
            sh *= 2
        return g

    rows = [_sigmoid(ab_ref[pl.ds(h, 1), :]), _sigmoid(ab_ref[pl.ds(DN_HEADS + h, 1), :]), decay_row(0), decay_row(1)]
    gt = jnp.concatenate(rows + [jnp.zeros((LANES - len(rows), s), F32)], axis=0)
    gt_ref[...] = gt[0:8, :]
    bg_ref[...] = gt.T
    lane128 = lax.broadcasted_iota(jnp.int32, (1, LANES), 1)

    masks = _tri_masks(PAIR, c)
    merge = {rev: _merge_masks(PAIR, rev) for rev in (False, True)}
    eye_f = jnp.where(masks["eye"], 1.0, 0.0)
    npair = s // PAIR
    group = min(DN_PREP_PAIRS, npair)

    def prep(pidx, d):
        reverse = d == 1
        rows = pl.ds(pl.multiple_of(pidx * PAIR, PAIR), PAIR)
        qp, kp, vp = qs_ref[rows, :], ks_ref[rows, :], vs_ref[rows, :]
        bgp = bg_ref[rows, :]
        beta = bgp[:, d:d + 1]
        gcol = bgp[:, 2 + d:3 + d]
        grow = gt_ref[2 + d:3 + d, rows]
        incl, strict = masks[reverse]
        decay = jnp.where(incl, jnp.exp(jnp.where(incl, gcol - grow, 0.0)), 0.0)
        kb = kp * beta
        kq = _dot1(jnp.concatenate([kb, qp], axis=0), kp, _NT)
        yield
        qk_ref[d, rows, :] = jnp.where(incl, kq[PAIR:] * decay, 0.0).astype(BF16)
        t_inv = []
        yield from _tri_inverse(jnp.where(strict, kq[:PAIR] * decay, 0.0), eye_f, merge[reverse], t_inv)
        egc = jnp.exp(gcol)
        uw = _dot1(t_inv[0], jnp.concatenate([vp * beta, kb * egc], axis=1))
        yield
        u_ref[d, rows, :] = uw[:, :DN_HEAD_DIM]
        w_ref[d, rows, :] = uw[:, DN_HEAD_DIM:].astype(BF16)
        qg_ref[d, rows, :] = (qp * egc).astype(BF16)
        last0, last1 = (0, c) if reverse else (c - 1, PAIR - 1)
        glast = jnp.where(lane128 < c, grow[:, last0:last0 + 1], grow[:, last1:last1 + 1])
        kdt_ref[d, :, rows] = (kt_ref[:, rows] * jnp.exp(glast - grow)).astype(BF16)
        eg_ref[d, :, rows] = jnp.broadcast_to(jnp.exp(glast), (8, PAIR))

    def seq(states, pidx, d):
        pair_rows = pl.ds(pl.multiple_of(pidx * PAIR, PAIR), PAIR)
        kdt = kdt_ref[d, :, pair_rows]
        eg = eg_ref[d, 0:1, pair_rows]
        for half in ((1, 0) if d else (0, 1)):
            r0 = half * c
            rows = pl.ds(pl.multiple_of(pidx * PAIR + r0, c), c)
            ws = _dot1(jnp.concatenate([w_ref[d, rows, :], qg_ref[d, rows, :]], axis=0), states[d])
            yield
            v_new = u_ref[d, rows, :] - ws[:c]
            o = ws[c:] + _dot1(qk_ref[d, rows, r0:r0 + c], v_new)
            if d:
                ob_ref[rows, :] = o
            else:
                of_ref[rows, :] = o
            states[d] = states[d] * eg[:, r0:r0 + 1] + _dot1(kdt[:, r0:r0 + c], v_new)
            yield

    ngroups = npair // group

    def prep_chains(g):
        return [prep(npair - 1 - (g * group + j) if d else g * group + j, d) for j in range(group) for d in range(2)]

    def seq_chain(states, g, d):
        for j in range(group):
            yield from seq(states, npair - 1 - (g * group + j) if d else g * group + j, d)

    def step(g, carry):
        states = list(carry)
        _interleave(prep_chains(g + 1) + [seq_chain(states, g, 0), seq_chain(states, g, 1)])
        return tuple(states)

    zero = jnp.zeros((DN_HEAD_DIM, DN_HEAD_DIM), F32)
    _interleave(prep_chains(0))
    states = list(lax.fori_loop(0, ngroups - 1, step, (zero, zero)))
    _interleave([seq_chain(states, ngroups - 1, 0), seq_chain(states, ngroups - 1, 1)])

    o = of_ref[...] + ob_ref[...]
    o = o * lax.rsqrt(jnp.mean(o * o, axis=-1, keepdims=True) + RMS_EPS) * gain_ref[...]
    gate = gate_ref[...]
    y_ref[...] = (o * (gate * _sigmoid(gate))).astype(y_ref.dtype)


def deltanet(p, ab, conv_t, alog32, dt32, gain, batch):
    t = p.shape[0]
    s = t // batch
    assert s % PAIR == 0 and (s // PAIR) % min(DN_PREP_PAIRS, s // PAIR) == 0
    hd = DN_HEAD_DIM
    col = lambda off: pl.BlockSpec((s, hd), lambda b, h: (b, off + h))
    cw = lambda off: pl.BlockSpec((3, hd), lambda b, h: (0, off + h))
    small = lambda n: pl.BlockSpec((1, n), lambda b, h: (0, 0))
    big = pltpu.VMEM((s, hd), F32)
    half = pltpu.VMEM((2, s, hd), BF16)
    return pl.pallas_call(
        _dn_body,
        grid=(batch, DN_HEADS),
        in_specs=[col(0), col(DN_HEADS), col(2 * DN_HEADS), col(3 * DN_HEADS),
                  pl.BlockSpec((4 * DN_HEADS, s), lambda b, h: (0, b)),
                  cw(0), cw(DN_HEADS), cw(2 * DN_HEADS),
                  small(4 * DN_HEADS), small(4 * DN_HEADS), small(hd)],
        out_specs=pl.BlockSpec((s, hd), lambda b, h: (b, h)),
        out_shape=jax.ShapeDtypeStruct((t, DN_WIDTH + SC_WIDTH), BF16),
        scratch_shapes=[big, big, big, pltpu.VMEM((hd, s), F32), big, pltpu.VMEM((8, s), F32), big, big,
                        pltpu.VMEM((2, s, hd), F32), half, half, half, pltpu.VMEM((2, hd, s), BF16),
                        pltpu.VMEM((2, 8, s), F32)],
        compiler_params=_params("parallel", "parallel"),
        name="deltanet",
    )(p, p, p, p, ab, conv_t, conv_t, conv_t, alog32, dt32, gain.reshape(1, hd))


def _sconv_body(b_ref, c_ref, u_ref, w_ref, y_in_ref, y_ref):
    s = b_ref.shape[0]
    row = lax.broadcasted_iota(jnp.int32, (s, 1), 0)
    cu = c_ref[...] * u_ref[...]
    w = w_ref[...]
    y = _shift_down(cu, row) * w[0:1] + cu * w[1:2] + _shift_up(cu, row) * w[2:3]
    y_ref[...] = (b_ref[...] * y).astype(y_ref.dtype)


def short_conv(p, conv_t, col0, y, batch, tc=256):
    t = p.shape[0]
    s = t // batch
    nct = SC_WIDTH // tc
    base = col0 // tc
    col = lambda off: pl.BlockSpec((s, tc), lambda b, c: (b, base + off * nct + c))
    return pl.pallas_call(
        _sconv_body,
        grid=(batch, nct),
        in_specs=[col(0), col(1), col(2), pl.BlockSpec((3, tc), lambda b, c: (0, c)),
                  pl.BlockSpec(memory_space=pl.ANY)],
        out_specs=pl.BlockSpec((s, tc), lambda b, c: (b, DN_WIDTH // tc + c)),
        out_shape=jax.ShapeDtypeStruct(y.shape, y.dtype),
        input_output_aliases={4: 0},
        compiler_params=_params("parallel", "parallel"),
        name="short_conv",
    )(p, p, p, conv_t, y)


def _dsa_body(*refs, seq):
    q_refs = refs[0:3]
    k_refs = refs[3:6]
    v_refs = refs[6:9]
    qn_ref, kn_ref, bias_ref, y_ref, qs_ref, kpad_ref, vpad_ref, og_ref, lse_ref = refs[9:]
    s = seq
    qb = DSA_QBLK
    side = DSA_SIDE
    width = qb + 2 * side
    kj = lax.broadcasted_iota(jnp.int32, (1, width), 1)
    zpad = jnp.zeros((DSA_PAD, DSA_HEAD_DIM), F32)
    for ref in (kpad_ref, vpad_ref):
        ref[0:DSA_PAD, :] = zpad
        ref[DSA_PAD + s:2 * DSA_PAD + s, :] = zpad

    for gi, (_, dil) in enumerate(DSA_PATTERNS):
        sub = s // dil
        nblk = sub // qb
        q = q_refs[gi][...]
        qs_ref[...] = (q * lax.rsqrt(jnp.mean(q * q, axis=-1, keepdims=True) + RMS_EPS) * qn_ref[...]
                       * (DSA_HEAD_DIM ** -0.5))
        k = k_refs[gi][...]
        kpad_ref[DSA_PAD:DSA_PAD + s, :] = k * lax.rsqrt(jnp.mean(k * k, axis=-1, keepdims=True) + RMS_EPS) * kn_ref[...]
        vpad_ref[DSA_PAD:DSA_PAD + s, :] = v_refs[gi][...]
        bias = bias_ref[0, gi]

        def block(t, gi=gi, dil=dil, sub=sub, nblk=nblk, bias=bias):
            r = t // nblk
            n = t % nblk
            rows = pl.ds(r + n * (qb * dil), qb, stride=dil)
            win = pl.ds(DSA_PAD + r + (n * qb - side) * dil, width, stride=dil)
            logits = _dot1(qs_ref[rows, :], kpad_ref[win, :], _NT) + bias
            yield
            pos = n * qb - side + kj
            logits = jnp.where((pos >= 0) & (pos < sub), logits, NEG_INF)
            m = jnp.max(logits, axis=-1, keepdims=True)
            yield
            p = jnp.exp(logits - m)
            ssum = jnp.sum(p, axis=-1, keepdims=True)
            o = _dot1(p, vpad_ref[win, :])
            yield
            og_ref[gi, rows, :] = o / ssum
            lse_ref[gi, rows, :] = jnp.broadcast_to(m + jnp.log(ssum), (qb, DSA_HEAD_DIM))

        nblocks = dil * nblk

        def blocks_step(i, _, block=block):
            _interleave(block(i * DSA_INTERLEAVE + u) for u in range(DSA_INTERLEAVE))
            return 0

        lax.fori_loop(0, nblocks // DSA_INTERLEAVE, blocks_step, 0)

    lse = [lse_ref[gi] for gi in range(DSA_GROUPS)]
    mx = jnp.maximum(jnp.maximum(lse[0], lse[1]), lse[2])
    ws = [jnp.exp(l - mx) for l in lse]
    num = ws[0] * og_ref[0] + ws[1] * og_ref[1] + ws[2] * og_ref[2]
    y_ref[...] = (num / (ws[0] + ws[1] + ws[2])).astype(y_ref.dtype)


def dilated_attention(p, qn, kn, bias_tab, batch):
    t = p.shape[0]
    s = t // batch
    assert (s // DSA_QBLK) % DSA_INTERLEAVE == 0
    hd = DSA_HEAD_DIM
    nh = DSA_HEADS

    def col(part, gi):
        return pl.BlockSpec((s, hd), lambda b, j: (b, part * nh + gi * DSA_HPG + j))

    in_specs = [col(part, gi) for part in range(3) for gi in range(DSA_GROUPS)]
    in_specs += [pl.BlockSpec((1, hd), lambda b, j: (0, 0)), pl.BlockSpec((1, hd), lambda b, j: (0, 0)),
                 pl.BlockSpec((1, DSA_GROUPS, DSA_QBLK, DSA_QBLK + 2 * DSA_SIDE), lambda b, j: (j, 0, 0, 0))]
    big = pltpu.VMEM((s, hd), F32)
    pad = pltpu.VMEM((s + 2 * DSA_PAD, hd), F32)
    grp = pltpu.VMEM((DSA_GROUPS, s, hd), F32)
    return pl.pallas_call(
        functools.partial(_dsa_body, seq=s),
        grid=(batch, DSA_HPG),
        in_specs=in_specs,
        out_specs=pl.BlockSpec((s, hd), lambda b, j: (b, j)),
        out_shape=jax.ShapeDtypeStruct((t, DSA_HPG * hd + RW_WIDTH), BF16),
        scratch_shapes=[big, pad, pad, grp, grp],
        compiler_params=_params("parallel", "parallel"),
        name="dilated_attention",
    )(*([p] * 9), qn.reshape(1, hd), kn.reshape(1, hd), bias_tab)


def _t5_bucket(rel):
    half = REL_BUCKETS // 2
    max_exact = half // 2
    n = np.abs(rel)
    scaled = (np.log(np.maximum(n, max_exact).astype(np.float32) / np.float32(max_exact))
              / np.float32(math.log(REL_MAX_DIST / max_exact)))
    large = np.minimum(max_exact + (scaled * np.float32(half - max_exact)).astype(np.int32), half - 1)
    return np.where(rel > 0, half, 0) + np.where(n < max_exact, n, large)


def _dsa_bias_table(rel_bias):
    width = DSA_QBLK + 2 * DSA_SIDE
    tabs = []
    for gi, (_, dil) in enumerate(DSA_PATTERNS):
        offs = np.arange(-DSA_SIDE, DSA_SIDE + 1, dtype=np.int32) * dil
        band = rel_bias[_t5_bucket(offs)][:, gi * DSA_HPG:(gi + 1) * DSA_HPG].astype(F32)
        fill = jnp.full((DSA_QBLK - 1, DSA_HPG), NEG_INF, F32)
        line = jnp.concatenate([fill, band, fill], axis=0)
        tabs.append(jnp.stack([line[DSA_QBLK - 1 - q:DSA_QBLK - 1 - q + width] for q in range(DSA_QBLK)]))
    return jnp.transpose(jnp.stack(tabs), (3, 0, 1, 2))


def _head_block_diag():
    ii = lax.broadcasted_iota(jnp.int32, (LANES, LANES), 0)
    jj = lax.broadcasted_iota(jnp.int32, (LANES, LANES), 1)
    return (ii // RW_HEAD_DIM) == (jj // RW_HEAD_DIM)


def _rw_prep_body(r_ref, k_ref, v_ref, lo_ref, mur_ref, muk_ref, muv_ref, mulo_ref, w0_ref, w2_ref, a0_ref,
                  a2_ref, g2_ref, kk_ref, ka_ref, rk_ref,
                  ro_ref, vo_ref, kko_ref, bon_ref, gate_ref, lwf_ref, lwb_ref, kdf_ref, kdb_ref, bbf_ref, bbb_ref):
    s = r_ref.shape[0]
    row = lax.broadcasted_iota(jnp.int32, (s, 1), 0)

    def mix(t, mu):
        return t + mu * (0.5 * (_shift_down(t, row) + _shift_up(t, row)) - t)

    r = mix(r_ref[...], mur_ref[...])
    kr = mix(k_ref[...], muk_ref[...])
    v = mix(v_ref[...], muv_ref[...])
    lo = mix(lo_ref[...], mulo_ref[...])
    bd = jnp.where(_head_block_diag(), 1.0, 0.0)

    def head_sum(t):
        return _dot_exact_rhs(t, bd)

    kk = kr * kk_ref[...]
    kk = kk * lax.rsqrt(head_sum(kk * kk) + L2_EPS)
    gd = lo[:, 4 * RW_LORA:]
    gate_ref[...] = _dot3(_sigmoid(gd), g2_ref[...])
    ro_ref[...] = r
    vo_ref[...] = v
    kko_ref[...] = kk
    bonus = jnp.zeros_like(r)
    outs = ((lwf_ref, kdf_ref, bbf_ref), (lwb_ref, kdb_ref, bbb_ref))
    for d in range(2):
        wd = lo[:, d * RW_LORA:(d + 1) * RW_LORA]
        ad = lo[:, (2 + d) * RW_LORA:(3 + d) * RW_LORA]
        w_log = -_softplus(-(w0_ref[d:d + 1, :] + _dot3(jnp.tanh(wd), w2_ref[d]))) - 0.5
        a = _sigmoid(a0_ref[d:d + 1, :] + _dot3(ad, a2_ref[d]))
        kd = kr * (1.0 + (a - 1.0) * ka_ref[...])
        lw_ref, kd_ref, bb_ref = outs[d]
        lw_ref[...] = -jnp.exp(w_log)
        kd_ref[...] = kd
        bb_ref[...] = kk * a
        bonus = bonus + head_sum(r * kd * rk_ref[...]) * v
    bon_ref[...] = bonus


def rwkv_prep(p, lora_in, mu, w0, w2, a0, a2, g2, k_k, k_a, r_k, col0, batch):
    t = p.shape[0]
    s = t // batch
    nct = RW_WIDTH // LANES
    base = col0 // LANES
    nlo = RW_LORA_IN
    col = lambda off: pl.BlockSpec((s, LANES), lambda b, c: (b, base + off * nct + c))
    vec = lambda off: pl.BlockSpec((1, LANES), lambda b, c: (0, off * nct + c))
    mu_main = mu[:RW_MAIN].reshape(1, RW_MAIN)
    mu_lo = mu[RW_MAIN:].reshape(1, nlo)
    out_spec = pl.BlockSpec((s, LANES), lambda b, c: (b, c))
    n_out = 11
    return pl.pallas_call(
        _rw_prep_body,
        grid=(batch, nct),
        in_specs=[col(0), col(1), col(2),
                  pl.BlockSpec((s, nlo), lambda b, c: (b, 0)),
                  vec(0), vec(1), vec(2),
                  pl.BlockSpec((1, nlo), lambda b, c: (0, 0)),
                  pl.BlockSpec((2, LANES), lambda b, c: (0, c)),
                  pl.BlockSpec((2, RW_LORA, LANES), lambda b, c: (0, 0, c)),
                  pl.BlockSpec((2, LANES), lambda b, c: (0, c)),
                  pl.BlockSpec((2, RW_LORA, LANES), lambda b, c: (0, 0, c)),
                  pl.BlockSpec((RW_GATE_LORA, LANES), lambda b, c: (0, c)),
                  vec(0), vec(0), vec(0)],
        out_specs=[out_spec] * n_out,
        out_shape=[jax.ShapeDtypeStruct((t, RW_WIDTH), F32)] * n_out,
        compiler_params=_params("parallel", "parallel"),
        name="rwkv_prep",
    )(p, p, p, lora_in, mu_main, mu_main, mu_main, mu_lo, w0, w2, a0, a2, g2,
      k_k.reshape(1, RW_WIDTH), k_a.reshape(1, RW_WIDTH), r_k.reshape(1, RW_WIDTH))


def _rw_scan_body(r_ref, v_ref, kk_ref, bon_ref, gate_ref, lwf_ref, lwb_ref, kdf_ref, kdb_ref, bbf_ref, bbb_ref,
                  lnw_ref, lnb_ref, y_in_ref, y_ref, cum_ref, cumt_ref, kdt_ref, bbt_ref, yf_ref, yb_ref,
                  tr_ref, tc_ref, rbk_ref, bkt_ref):
    s = r_ref.shape[0]
    c = CHUNK
    c2 = 2 * c
    row = lax.broadcasted_iota(jnp.int32, (s, 1), 0)
    pos = row % c
    lw_refs, kd_refs, bb_refs = (lwf_ref, lwb_ref), (kdf_ref, kdb_ref), (bbf_ref, bbb_ref)
    for d in range(2):
        cum = _seg_cumsum(lw_refs[d][...], pos, d == 1)
        cum_ref[d] = cum
        cumt_ref[d] = cum.T
        kdt_ref[d] = kd_refs[d][...].T
        bbt_ref[d] = bb_refs[d][...].T

    masks = _tri_masks(c2, c)
    merge = {rev: _merge_masks(c2, rev) for rev in (False, True)}
    eye2_f = jnp.where(masks["eye"], 1.0, 0.0)
    head0 = lax.broadcasted_iota(jnp.int32, (1, LANES), 1) < RW_HEAD_DIM
    head_bd = _head_block_diag()
    npair = s // PAIR
    group = min(RW_PREP_PAIRS, npair)

    def stack(x):
        return jnp.concatenate([jnp.where(head0, x, 0.0), jnp.where(head0, 0.0, x)], axis=0)

    def prep(pidx, d, half):
        reverse = d == 1
        incl2, strict2 = masks[reverse]
        cidx = 2 * pidx + half
        cols = pl.ds(pl.multiple_of(pidx * PAIR, PAIR), PAIR)
        rows = pl.ds(pl.multiple_of(cidx * c, c), c)
        hs = slice(half * c, (half + 1) * c)
        r, v, kk = r_ref[rows, :], v_ref[rows, :], kk_ref[rows, :]
        lw, kd, bb = lw_refs[d][rows, :], kd_refs[d][rows, :], bb_refs[d][rows, :]
        cum = cum_ref[d, rows, :]
        e_neg = jnp.exp(-cum)
        a_s = stack(-kk * jnp.exp(cum - lw))
        r_s = stack(r * jnp.exp(cum))
        ar = jnp.concatenate([a_s, r_s], axis=0)
        bk = jnp.concatenate([stack(bb * e_neg), stack(kd * e_neg)], axis=0)
        g = _dot1(ar, bk, _NT)
        yield
        rbk_ref[d, cidx] = jnp.concatenate([jnp.where(incl2, g[c2:, :c2], 0.0),
                                            jnp.where(incl2, g[c2:, c2:], 0.0)], axis=1).astype(BF16)
        av = _dot1(jnp.where(strict2, g[:c2, c2:], 0.0), stack(v))
        yield
        t_inv = []
        yield from _tri_inverse(-jnp.where(strict2, g[:c2, :c2], 0.0), eye2_f, merge[reverse], t_inv)
        tt = _dot1(t_inv[0], jnp.concatenate([a_s, av], axis=1))
        yield
        tr_ref[d, cidx] = jnp.concatenate([tt[:, :LANES], r_s], axis=0).astype(BF16)
        tc_ref[d, cidx] = tt[:, LANES:].astype(BF16)
        cum_t = cumt_ref[d, :, cols][:, hs]
        last = 0 if reverse else c - 1
        e_out_t = jnp.exp(cum_t[:, last:last + 1] - cum_t)
        bkt_ref[d, cidx] = jnp.concatenate([bbt_ref[d, :, cols][:, hs] * e_out_t,
                                            kdt_ref[d, :, cols][:, hs] * e_out_t], axis=1).astype(BF16)

    def seq(states, pidx, d):
        cum_t2 = cumt_ref[d, :, pl.ds(pl.multiple_of(pidx * PAIR, PAIR), PAIR)]
        for half in ((1, 0) if d else (0, 1)):
            cidx = 2 * pidx + half
            rows = pl.ds(pl.multiple_of(cidx * c, c), c)
            v = v_ref[rows, :]
            x = _dot1(tr_ref[d, cidx], states[d])
            yield
            ps = x[:c2] + tc_ref[d, cidx].astype(F32)
            os_ = x[c2:] + _dot1(rbk_ref[d, cidx], jnp.concatenate([ps, stack(v)], axis=0))
            y = os_[:c] + os_[c:]
            if d:
                yb_ref[rows, :] = y
            else:
                yf_ref[rows, :] = y
            pv = jnp.concatenate([ps[:c] + ps[c:], v], axis=0)
            last = half * c + (0 if d else c - 1)
            e_tot = jnp.exp(cum_t2[:, last:last + 1])
            states[d] = states[d] * e_tot + jnp.where(head_bd, _dot1(bkt_ref[d, cidx], pv), 0.0)
            yield

    ngroups = npair // group

    def prep_chains(g):
        return [prep(npair - 1 - (g * group + j) if d else g * group + j, d, half)
                for j in range(group) for d in range(2) for half in range(2)]

    def seq_chain(states, g, d):
        for j in range(group):
            yield from seq(states, npair - 1 - (g * group + j) if d else g * group + j, d)

    def step(g, carry):
        states = list(carry)
        _interleave(prep_chains(g + 1) + [seq_chain(states, g, 0), seq_chain(states, g, 1)])
        return tuple(states)

    zero = jnp.zeros((LANES, LANES), F32)
    _interleave(prep_chains(0))
    states = list(lax.fori_loop(0, ngroups - 1, step, (zero, zero)))
    _interleave([seq_chain(states, ngroups - 1, 0), seq_chain(states, ngroups - 1, 1)])

    bd = jnp.where(head_bd, 1.0, 0.0)
    y = yf_ref[...] + yb_ref[...]
    mean = _dot_exact_rhs(y, bd) * (1.0 / RW_HEAD_DIM)
    yc = y - mean
    var = _dot_exact_rhs(yc * yc, bd) * (1.0 / RW_HEAD_DIM)
    yn = yc * lax.rsqrt(var + RW_GN_EPS) * lnw_ref[...] + lnb_ref[...]
    y_ref[...] = ((yn + bon_ref[...]) * gate_ref[...]).astype(y_ref.dtype)


def rwkv_scan(prep, ln_w, ln_b, y, batch):
    t = prep[0].shape[0]
    s = t // batch
    assert s % PAIR == 0 and (s // PAIR) % min(RW_PREP_PAIRS, s // PAIR) == 0
    nct = RW_WIDTH // LANES
    nchunk = s // CHUNK
    blk = pl.BlockSpec((s, LANES), lambda b, c: (b, c))
    vec = pl.BlockSpec((1, LANES), lambda b, c: (0, c))
    tsp = pltpu.VMEM((2, LANES, s), F32)
    big = pltpu.VMEM((s, LANES), F32)
    per_chunk = lambda rows, cols: pltpu.VMEM((2, nchunk, rows, cols), BF16)
    return pl.pallas_call(
        _rw_scan_body,
        grid=(batch, nct),
        in_specs=[blk] * 11 + [vec, vec, pl.BlockSpec(memory_space=pl.ANY)],
        out_specs=pl.BlockSpec((s, LANES), lambda b, c: (b, (y.shape[1] - RW_WIDTH) // LANES + c)),
        out_shape=jax.ShapeDtypeStruct(y.shape, y.dtype),
        input_output_aliases={13: 0},
        scratch_shapes=[pltpu.VMEM((2, s, LANES), F32), tsp, tsp, tsp, big, big,
                        per_chunk(2 * PAIR, LANES), per_chunk(PAIR, LANES), per_chunk(PAIR, 2 * LANES),
                        per_chunk(PAIR, LANES)],
        compiler_params=_params("parallel", "parallel"),
        name="rwkv_scan",
    )(*prep, ln_w.reshape(1, RW_WIDTH), ln_b.reshape(1, RW_WIDTH), y)


def even_mixer(x, g, w_in, w_sc, w_ab, w_out, layer, conv_qkv, a_log, dt_bias, out_gain, conv_sc, batch):
    p, ab = norm_mm(x, g, w_in, layer, ncols=4 * DN_WIDTH, w2=w_sc, wp_t=w_ab.T)
    zeros = jnp.zeros((2 * DN_HEADS,), F32)
    alog32 = jnp.concatenate([zeros, a_log.reshape(-1)]).reshape(1, -1)
    dt32 = jnp.concatenate([zeros, dt_bias.reshape(-1)]).reshape(1, -1)
    y = deltanet(p, ab, conv_qkv.T, alog32, dt32, out_gain, batch)
    y = short_conv(p, conv_sc.T, 4 * DN_WIDTH, y, batch)
    return mm_res(y, w_out, layer, x)


def odd_mixer(x, g, w_in, w_lora, w_out, layer, qn, kn, bias_tab, mu, w0, w2, a0, a2, g2, k_k, k_a, r_k, ln_w, ln_b,
              batch):
    p, lora_in = norm_mm(x, g, w_in, layer, ncols=3 * DSA_QKV + RW_MAIN, wp=w_lora)
    y = dilated_attention(p, qn, kn, bias_tab, batch)
    prep = rwkv_prep(p, lora_in, mu, w0, w2, a0, a2, g2, k_k, k_a, r_k, 3 * DSA_QKV, batch)
    y = rwkv_scan(prep, ln_w, ln_b, y, batch)
    return mm_res(y, w_out, layer, x)


def kernel(x, mem, rel_bias, norm_mix, norm_xattn, norm_mem, norm_ffn, xa_wq, xa_wk, xa_wv, xa_wo, xa_qn, xa_kn, ffn_w1, ffn_w2, ev_w_in, ev_w_out, dn_conv, dn_a_log, dn_dt_bias, dn_norm, sc_conv, od_w_in, od_w_out, ca_qn, ca_kn, rw_mu, rw_w0, rw_w2, rw_a0, rw_a2, rw_g2, rw_k_k, rw_k_a, rw_r_k, rw_ln_w, rw_ln_b):
    batch, seq, d = x.shape
    n_mem = mem.shape[1]
    xf = x.reshape(batch * seq, d)
    memf = mem.reshape(batch * n_mem, d)
    bias_tab = _dsa_bias_table(rel_bias)
    c_ab = 4 * DN_WIDTH
    c_sc = c_ab + 4 * DN_HEADS
    c_lo = 3 * DSA_QKV + RW_MAIN
    ev_in, od_in = ev_w_in.astype(BF16), od_w_in.astype(BF16)
    ev_sc = ev_in[:, :, c_sc:]
    ev_out, od_out = ev_w_out.astype(BF16), od_w_out.astype(BF16)
    w_q, w_o = xa_wq.astype(BF16), xa_wo.astype(BF16)
    w_kv = jnp.concatenate([xa_wk, xa_wv], axis=2).astype(BF16)
    w_1, w_2 = ffn_w1.astype(BF16), ffn_w2.astype(BF16)
    for layer in range(DEPTH):
        i = layer // 2
        if layer % 2 == 0:
            xf = even_mixer(xf, norm_mix[layer], ev_in, ev_sc, ev_w_in[i, :, c_ab:c_sc], ev_out, i, dn_conv[i],
                            dn_a_log[i], dn_dt_bias[i], dn_norm[i], sc_conv[i], batch)
        else:
            xf = odd_mixer(xf, norm_mix[layer], od_in, od_w_in[i, :, c_lo:], od_out, i, ca_qn[i], ca_kn[i], bias_tab,
                           rw_mu[i], rw_w0[i], rw_w2[i], rw_a0[i], rw_a2[i], rw_g2[i], rw_k_k[i], rw_k_a[i], rw_r_k[i],
                           rw_ln_w[i], rw_ln_b[i], batch)
        kv = norm_mm(memf, norm_mem[layer], w_kv, layer).reshape(batch, n_mem, 2 * XA_WIDTH)
        xf = xattn(xf, norm_xattn[layer], w_q, kv, w_o, layer, xa_qn[layer], xa_kn[layer], batch)
        h1 = norm_mm(xf, norm_ffn[layer], w_1, layer, act="relu2", out_dtype=BF16)
        xf = mm_res(h1, w_2, layer, xf)
    return xf.reshape(batch, seq, d)
```

```python
import functools
import math

import jax
import jax.numpy as jnp
import numpy as np
from jax import lax
from jax.experimental import pallas as pl
from jax.experimental.pallas import tpu as pltpu

F32 = jnp.float32
BF16 = jnp.bfloat16

D_MODEL = 2048
DEPTH = 4
RMS_EPS = 1e-6
L2_EPS = 1e-6

DN_HEADS = 8
DN_HEAD_DIM = 128
DN_WIDTH = DN_HEADS * DN_HEAD_DIM
SC_WIDTH = D_MODEL - DN_WIDTH
CHUNK = 64
PAIR = 2 * CHUNK
DN_PREP_PAIRS = 4
RW_PREP_PAIRS = 2

DSA_PATTERNS = ((128, 1), (512, 4), (2048, 16))
DSA_GROUPS = len(DSA_PATTERNS)
DSA_HPG = 4
DSA_HEAD_DIM = 128
DSA_HEADS = DSA_GROUPS * DSA_HPG
DSA_QKV = DSA_HEADS * DSA_HEAD_DIM
DSA_SIDE = 64
DSA_QBLK = 128
DSA_PAD = DSA_SIDE * max(d for _, d in DSA_PATTERNS)
DSA_INTERLEAVE = 4
REL_BUCKETS = 32
REL_MAX_DIST = 1024
NEG_INF = -1e30

RW_HEADS = 8
RW_HEAD_DIM = 64
RW_WIDTH = RW_HEADS * RW_HEAD_DIM
RW_LORA = 64
RW_GATE_LORA = 128
RW_MAIN = 3 * RW_WIDTH
RW_LORA_IN = 4 * RW_LORA + RW_GATE_LORA
RW_GN_EPS = 64e-5

XA_HEADS = 4
XA_HEAD_DIM = 128
XA_WIDTH = XA_HEADS * XA_HEAD_DIM

LANES = 128
NORM_ROWS = 256
VMEM_LIMIT_BYTES = 56 * 1024 * 1024


def _params(*sem):
    return pltpu.CompilerParams(dimension_semantics=sem, vmem_limit_bytes=VMEM_LIMIT_BYTES)


_NN = (((1,), (0,)), ((), ()))
_NT = (((1,), (1,)), ((), ()))


def _dot1(a, b, dims=_NN):
    return lax.dot_general(a.astype(BF16), b.astype(BF16), dims, preferred_element_type=F32)


def _dot_sum(a, b01):
    b = b01.astype(BF16)
    hi = a.astype(BF16)
    lo = (a - hi.astype(F32)).astype(BF16)
    dg = functools.partial(lax.dot_general, dimension_numbers=_NN, preferred_element_type=F32)
    return dg(hi, b) + dg(lo, b)


def _interleave(chains):
    chains = list(chains)
    while chains:
        alive = []
        for ch in chains:
            try:
                next(ch)
                alive.append(ch)
            except StopIteration:
                pass
        chains = alive


def _merge_masks(n, reverse):
    ii = lax.broadcasted_iota(jnp.int32, (n, n), 0)
    jj = lax.broadcasted_iota(jnp.int32, (n, n), 1)
    tri = (ii < jj) if reverse else (ii > jj)
    out = []
    size = 1
    while size < CHUNK:
        out.append(tri & ((ii // (2 * size)) == (jj // (2 * size))) & ((ii // size) != (jj // size)))
        size *= 2
    return out


def _tri_inverse(a, eye_f, level_masks, out):
    t = eye_f - jnp.where(level_masks[0], a, 0.0)
    for m in level_masks[1:]:
        x = _dot1(jnp.where(m, a, 0.0), t)
        yield
        t = t - _dot1(t, x)
        yield
    out.append(t)


def _sigmoid(x):
    return 1.0 / (1.0 + jnp.exp(-x))


def _softplus(x):
    return jnp.maximum(x, 0.0) + jnp.log1p(jnp.exp(-jnp.abs(x)))


def _shift_down(x, row):
    return jnp.where(row == 0, 0.0, pltpu.roll(x, 1, 0))


def _shift_up(x, row):
    n = x.shape[0]
    return jnp.where(row == n - 1, 0.0, pltpu.roll(x, n - 1, 0))


def _norm_mm_body(x_ref, g_ref, w_ref, *rest, act, precise, precise_t, n_first):
    if n_first is not None:
        w2_ref, rest = rest[0], rest[1:]
    if precise:
        wp_ref, o_ref, op_ref, xn_ref = rest
    else:
        o_ref, xn_ref = rest

    @pl.when(pl.program_id(1) == 0)
    def _():
        def rows_step(r, _):
            rows = pl.ds(pl.multiple_of(r * NORM_ROWS, NORM_ROWS), NORM_ROWS)
            x = x_ref[rows, :]
            xn = x * lax.rsqrt(jnp.mean(x * x, axis=-1, keepdims=True) + RMS_EPS) * g_ref[...]
            xn_ref[rows, :] = xn.astype(BF16)
            if precise and precise_t:
                op_ref[:, rows] = _dot1(wp_ref[...], xn, _NT)
            elif precise:
                op_ref[rows, :] = _dot1(xn, wp_ref[...])
            return 0

        lax.fori_loop(0, x_ref.shape[0] // NORM_ROWS, rows_step, 0)

    def project(wt_ref):
        acc = jnp.dot(xn_ref[...], wt_ref[...].astype(BF16), preferred_element_type=F32)
        if act == "relu2":
            acc = jnp.square(jnp.maximum(acc, 0.0))
        o_ref[...] = acc.astype(o_ref.dtype)

    if n_first is None:
        project(w_ref)
    else:
        pl.when(pl.program_id(1) < n_first)(lambda: project(w_ref))
        pl.when(pl.program_id(1) >= n_first)(lambda: project(w2_ref))


def norm_mm(x, g, w, layer, ncols=None, w2=None, wp=None, wp_t=None, act=None, out_dtype=F32, tm=1024, tn=1024):
    t, d = x.shape
    n1 = w.shape[2] if ncols is None else ncols
    n = n1 + (0 if w2 is None else w2.shape[2])
    tm = min(tm, t)
    tn = min(tn, n1)
    assert t % tm == 0 and n1 % tn == 0 and n % tn == 0 and tm % NORM_ROWS == 0
    precise = wp is not None or wp_t is not None
    n_first = None if w2 is None else n1 // tn
    in_specs = [
        pl.BlockSpec((tm, d), lambda i, j: (i, 0)),
        pl.BlockSpec((1, d), lambda i, j: (0, 0)),
    ]
    args = [x, g.reshape(1, d), w]
    if w2 is None:
        in_specs.append(pl.BlockSpec((None, d, tn), lambda i, j: (layer, 0, j)))
    else:
        in_specs.append(pl.BlockSpec((None, d, tn), lambda i, j: (layer, 0, jnp.minimum(j, n_first - 1))))
        in_specs.append(pl.BlockSpec((None, d, tn), lambda i, j: (layer, 0, jnp.maximum(j - n_first, 0))))
        args.append(w2)
    out_specs = pl.BlockSpec((tm, tn), lambda i, j: (i, j))
    out_shape = jax.ShapeDtypeStruct((t, n), out_dtype)
    if wp is not None:
        npc = wp.shape[1]
        in_specs.append(pl.BlockSpec((d, npc), lambda i, j: (0, 0)))
        out_specs = [out_specs, pl.BlockSpec((tm, npc), lambda i, j: (i, 0))]
        out_shape = [out_shape, jax.ShapeDtypeStruct((t, npc), F32)]
        args.append(wp)
    elif wp_t is not None:
        npc = wp_t.shape[0]
        in_specs.append(pl.BlockSpec((npc, d), lambda i, j: (0, 0)))
        out_specs = [out_specs, pl.BlockSpec((npc, tm), lambda i, j: (0, i))]
        out_shape = [out_shape, jax.ShapeDtypeStruct((npc, t), F32)]
        args.append(wp_t)
    return pl.pallas_call(
        functools.partial(_norm_mm_body, act=act, precise=precise, precise_t=wp_t is not None, n_first=n_first),
        grid=(t // tm, n // tn),
        in_specs=in_specs,
        out_specs=out_specs,
        out_shape=out_shape,
        scratch_shapes=[pltpu.VMEM((tm, d), BF16)],
        compiler_params=_params("parallel", "arbitrary"),
        name="norm_mm",
    )(*args)


def _mm_res_body(a_ref, w_ref, r_ref, o_ref):
    acc = jnp.dot(a_ref[...], w_ref[...].astype(BF16), preferred_element_type=F32)

    @pl.when(pl.program_id(2) == 0)
    def _():
        o_ref[...] = r_ref[...] + acc

    @pl.when(pl.program_id(2) > 0)
    def _():
        o_ref[...] += acc


def mm_res(a, w, layer, res, tm=1024, tn=1024, tk=2048):
    t, k = a.shape
    n = w.shape[2]
    tm, tn, tk = min(tm, t), min(tn, n), min(tk, k)
    assert t % tm == 0 and n % tn == 0 and k % tk == 0
    return pl.pallas_call(
        _mm_res_body,
        grid=(t // tm, n // tn, k // tk),
        in_specs=[
            pl.BlockSpec((tm, tk), lambda i, j, kk: (i, kk)),
            pl.BlockSpec((None, tk, tn), lambda i, j, kk: (layer, kk, j)),
            pl.BlockSpec((tm, tn), lambda i, j, kk: (i, j)),
        ],
        out_specs=pl.BlockSpec((tm, tn), lambda i, j, kk: (i, j)),
        out_shape=jax.ShapeDtypeStruct((t, n), F32),
        compiler_params=_params("parallel", "parallel", "arbitrary"),
        name="mm_res",
    )(a, w, res)


def _xattn_body(x_ref, g_ref, wq_ref, kv_ref, wo_ref, qn_ref, kn_ref, o_ref):
    x = x_ref[...]
    xn = x * lax.rsqrt(jnp.mean(x * x, axis=-1, keepdims=True) + RMS_EPS) * g_ref[...]
    q = jnp.dot(xn.astype(BF16), wq_ref[...], preferred_element_type=F32)
    kv = kv_ref[0]
    outs = []
    for h in range(XA_HEADS):
        sl = slice(h * XA_HEAD_DIM, (h + 1) * XA_HEAD_DIM)
        qh = q[:, sl]
        qh = qh * lax.rsqrt(jnp.mean(qh * qh, axis=-1, keepdims=True) + RMS_EPS) * qn_ref[...]
        kh = kv[:, sl]
        kh = kh * lax.rsqrt(jnp.mean(kh * kh, axis=-1, keepdims=True) + RMS_EPS) * kn_ref[...]
        vh = kv[:, XA_WIDTH + h * XA_HEAD_DIM:XA_WIDTH + (h + 1) * XA_HEAD_DIM]
        logits = _dot1(qh, kh, _NT) * (XA_HEAD_DIM ** -0.5)
        m = jnp.max(logits, axis=-1, keepdims=True)
        p = jnp.exp(logits - m)
        s = jnp.sum(p, axis=-1, keepdims=True)
        outs.append(_dot1(p, vh) / s)
    o = jnp.concatenate(outs, axis=-1).astype(BF16)
    o_ref[...] = x + jnp.dot(o, wo_ref[...], preferred_element_type=F32)


def xattn(x, g, wq, kv, wo, layer, qn, kn, batch, ts=512):
    t, d = x.shape
    s = t // batch
    ts = min(ts, s)
    nst = s // ts
    m = kv.shape[1]
    return pl.pallas_call(
        _xattn_body,
        grid=(batch, nst),
        in_specs=[
            pl.BlockSpec((ts, d), lambda b, i: (b * nst + i, 0)),
            pl.BlockSpec((1, d), lambda b, i: (0, 0)),
            pl.BlockSpec((None, d, XA_WIDTH), lambda b, i: (layer, 0, 0)),
            pl.BlockSpec((1, m, 2 * XA_WIDTH), lambda b, i: (b, 0, 0)),
            pl.BlockSpec((None, XA_WIDTH, d), lambda b, i: (layer, 0, 0)),
            pl.BlockSpec((1, XA_HEAD_DIM), lambda b, i: (0, 0)),
            pl.BlockSpec((1, XA_HEAD_DIM), lambda b, i: (0, 0)),
        ],
        out_specs=pl.BlockSpec((ts, d), lambda b, i: (b * nst + i, 0)),
        out_shape=jax.ShapeDtypeStruct((t, d), F32),
        compiler_params=_params("parallel", "parallel"),
        name="xattn",
    )(x, g.reshape(1, d), wq, kv, wo, qn.reshape(1, -1), kn.reshape(1, -1))


def _tri_masks(n, blk):
    ii = lax.broadcasted_iota(jnp.int32, (n, n), 0)
    jj = lax.broadcasted_iota(jnp.int32, (n, n), 1)
    same = (ii // blk) == (jj // blk) if n != blk else None

    def m(c):
        return c if same is None else (c & same)

    return {
        False: (m(ii >= jj), m(ii > jj)),
        True: (m(ii <= jj), m(ii < jj)),
        "eye": ii == jj,
    }


def _seg_cumsum(x, pos, reverse):
    n = x.shape[0]
    sh = 1
    while sh < CHUNK:
        if reverse:
            x = x + jnp.where(pos < CHUNK - sh, pltpu.roll(x, n - sh, 0), 0.0)
        else:
            x = x + jnp.where(pos >= sh, pltpu.roll(x, sh, 0), 0.0)
        sh *= 2
    return x


def _dn_body(q_ref, k_ref, v_ref, gate_ref, ab_ref, cq_ref, ck_ref, cv_ref, alog_ref, dt_ref, gain_ref,
             y_ref, qs_ref, ks_ref, vs_ref, kt_ref, bg_ref, gt_ref, of_ref, ob_ref,
             u_ref, w_ref, qg_ref, qk_ref, kdt_ref, eg_ref):
    s = q_ref.shape[0]
    h = pl.program_id(1)
    row = lax.broadcasted_iota(jnp.int32, (s, 1), 0)
    c = CHUNK

    def conv_silu(x_ref, cw_ref):
        x = x_ref[...]
        w = cw_ref[...]
        y = _shift_down(x, row) * w[0:1] + x * w[1:2] + _shift_up(x, row) * w[2:3]
        return y * _sigmoid(y)

    q = conv_silu(q_ref, cq_ref)
    q = q * lax.rsqrt(jnp.sum(q * q, axis=-1, keepdims=True) + L2_EPS) * (DN_HEAD_DIM ** -0.5)
    qs_ref[...] = q
    k = conv_silu(k_ref, ck_ref)
    k = k * lax.rsqrt(jnp.sum(k * k, axis=-1, keepdims=True) + L2_EPS)
    ks_ref[...] = k
    kt_ref[...] = k.T
    vs_ref[...] = conv_silu(v_ref, cv_ref)

    lane32 = lax.broadcasted_iota(jnp.int32, (1, 4 * DN_HEADS), 1)
    pos_t = lax.broadcasted_iota(jnp.int32, (1, s), 1) % c

    def decay_row(d):
        idx = (2 + d) * DN_HEADS + h
        a_log = jnp.sum(jnp.where(lane32 == idx, alog_ref[...], 0.0), axis=-1, keepdims=True)
        dt = jnp.sum(jnp.where(lane32 == idx, dt_ref[...], 0.0), axis=-1, keepdims=True)
        g = -jnp.exp(a_log) * _softplus(ab_ref[pl.ds(idx, 1), :] + dt)
        sh = 1
        while sh < c:
            if d:
                g = g + jnp.where(pos_t < c - sh, pltpu.roll(g, s - sh, 1), 0.0)
            else:
                g = g + jnp.where(pos_t >= sh, pltpu.roll(g, sh, 1), 0.0)
            sh *= 2
        return g

    rows = [_sigmoid(ab_ref[pl.ds(h, 1), :]), _sigmoid(ab_ref[pl.ds(DN_HEADS + h, 1), :]), decay_row(0), decay_row(1)]
    gt = jnp.concatenate(rows + [jnp.zeros((LANES - len(rows), s), F32)], axis=0)
    gt_ref[...] = gt[0:8, :]
    bg_ref[...] = gt.T
    lane128 = lax.broadcasted_iota(jnp.int32, (1, LANES), 1)

    masks = _tri_masks(PAIR, c)
    merge = {rev: _merge_masks(PAIR, rev) for rev in (False, True)}
    eye_f = jnp.where(masks["eye"], 1.0, 0.0)
    npair = s // PAIR
    group = min(DN_PREP_PAIRS, npair)

    def prep(pidx, d):
        reverse = d == 1
        rows = pl.ds(pl.multiple_of(pidx * PAIR, PAIR), PAIR)
        qp, kp, vp = qs_ref[rows, :], ks_ref[rows, :], vs_ref[rows, :]
        bgp = bg_ref[rows, :]
        beta = bgp[:, d:d + 1]
        gcol = bgp[:, 2 + d:3 + d]
        grow = gt_ref[2 + d:3 + d, rows]
        incl, strict = masks[reverse]
        decay = jnp.where(incl, jnp.exp(jnp.where(incl, gcol - grow, 0.0)), 0.0)
        kb = kp * beta
        kq = _dot1(jnp.concatenate([kb, qp], axis=0), kp, _NT)
        yield
        qk_ref[d, rows, :] = jnp.where(incl, kq[PAIR:] * decay, 0.0).astype(BF16)
        t_inv = []
        yield from _tri_inverse(jnp.where(strict, kq[:PAIR] * decay, 0.0), eye_f, merge[reverse], t_inv)
        egc = jnp.exp(gcol)
        uw = _dot1(t_inv[0], jnp.concatenate([vp * beta, kb * egc], axis=1))
        yield
        u_ref[d, rows, :] = uw[:, :DN_HEAD_DIM]
        w_ref[d, rows, :] = uw[:, DN_HEAD_DIM:].astype(BF16)
        qg_ref[d, rows, :] = (qp * egc).astype(BF16)
        last0, last1 = (0, c) if reverse else (c - 1, PAIR - 1)
        glast = jnp.where(lane128 < c, grow[:, last0:last0 + 1], grow[:, last1:last1 + 1])
        kdt_ref[d, :, rows] = (kt_ref[:, rows] * jnp.exp(glast - grow)).astype(BF16)
        eg_ref[d, :, rows] = jnp.broadcast_to(jnp.exp(glast), (8, PAIR))

    def seq(states, pidx, d):
        pair_rows = pl.ds(pl.multiple_of(pidx * PAIR, PAIR), PAIR)
        kdt = kdt_ref[d, :, pair_rows]
        eg = eg_ref[d, 0:1, pair_rows]
        for half in ((1, 0) if d else (0, 1)):
            r0 = half * c
            rows = pl.ds(pl.multiple_of(pidx * PAIR + r0, c), c)
            ws = _dot1(jnp.concatenate([w_ref[d, rows, :], qg_ref[d, rows, :]], axis=0), states[d])
            yield
            v_new = u_ref[d, rows, :] - ws[:c]
            o = ws[c:] + _dot1(qk_ref[d, rows, r0:r0 + c], v_new)
            if d:
                ob_ref[rows, :] = o
            else:
                of_ref[rows, :] = o
            states[d] = states[d] * eg[:, r0:r0 + 1] + _dot1(kdt[:, r0:r0 + c], v_new)
            yield

    ngroups = npair // group

    def prep_chains(g):
        return [prep(npair - 1 - (g * group + j) if d else g * group + j, d) for j in range(group) for d in range(2)]

    def seq_chain(states, g, d):
        for j in range(group):
            yield from seq(states, npair - 1 - (g * group + j) if d else g * group + j, d)

    def step(g, carry):
        states = list(carry)
        _interleave(prep_chains(g + 1) + [seq_chain(states, g, 0), seq_chain(states, g, 1)])
        return tuple(states)

    zero = jnp.zeros((DN_HEAD_DIM, DN_HEAD_DIM), F32)
    _interleave(prep_chains(0))
    states = list(lax.fori_loop(0, ngroups - 1, step, (zero, zero)))
    _interleave([seq_chain(states, ngroups - 1, 0), seq_chain(states, ngroups - 1, 1)])

    o = of_ref[...] + ob_ref[...]
    o = o * lax.rsqrt(jnp.mean(o * o, axis=-1, keepdims=True) + RMS_EPS) * gain_ref[...]
    gate = gate_ref[...]
    y_ref[...] = (o * (gate * _sigmoid(gate))).astype(y_ref.dtype)


def deltanet(p, ab, conv_t, alog32, dt32, gain, batch):
    t = p.shape[0]
    s = t // batch
    assert s % PAIR == 0 and (s // PAIR) % min(DN_PREP_PAIRS, s // PAIR) == 0
    hd = DN_HEAD_DIM
    col = lambda off: pl.BlockSpec((s, hd), lambda b, h: (b, off + h))
    cw = lambda off: pl.BlockSpec((3, hd), lambda b, h: (0, off + h))
    small = lambda n: pl.BlockSpec((1, n), lambda b, h: (0, 0))
    big = pltpu.VMEM((s, hd), F32)
    half = pltpu.VMEM((2, s, hd), BF16)
    return pl.pallas_call(
        _dn_body,
        grid=(batch, DN_HEADS),
        in_specs=[col(0), col(DN_HEADS), col(2 * DN_HEADS), col(3 * DN_HEADS),
                  pl.BlockSpec((4 * DN_HEADS, s), lambda b, h: (0, b)),
                  cw(0), cw(DN_HEADS), cw(2 * DN_HEADS),
                  small(4 * DN_HEADS), small(4 * DN_HEADS), small(hd)],
        out_specs=pl.BlockSpec((s, hd), lambda b, h: (b, h)),
        out_shape=jax.ShapeDtypeStruct((t, DN_WIDTH + SC_WIDTH), BF16),
        scratch_shapes=[big, big, big, pltpu.VMEM((hd, s), F32), big, pltpu.VMEM((8, s), F32), big, big,
                        pltpu.VMEM((2, s, hd), F32), half, half, half, pltpu.VMEM((2, hd, s), BF16),
                        pltpu.VMEM((2, 8, s), F32)],
        compiler_params=_params("parallel", "parallel"),
        name="deltanet",
    )(p, p, p, p, ab, conv_t, conv_t, conv_t, alog32, dt32, gain.reshape(1, hd))


def _sconv_body(b_ref, c_ref, u_ref, w_ref, y_in_ref, y_ref):
    s = b_ref.shape[0]
    row = lax.broadcasted_iota(jnp.int32, (s, 1), 0)
    cu = c_ref[...] * u_ref[...]
    w = w_ref[...]
    y = _shift_down(cu, row) * w[0:1] + cu * w[1:2] + _shift_up(cu, row) * w[2:3]
    y_ref[...] = (b_ref[...] * y).astype(y_ref.dtype)


def short_conv(p, conv_t, col0, y, batch, tc=256):
    t = p.shape[0]
    s = t // batch
    nct = SC_WIDTH // tc
    base = col0 // tc
    col = lambda off: pl.BlockSpec((s, tc), lambda b, c: (b, base + off * nct + c))
    return pl.pallas_call(
        _sconv_body,
        grid=(batch, nct),
        in_specs=[col(0), col(1), col(2), pl.BlockSpec((3, tc), lambda b, c: (0, c)),
                  pl.BlockSpec(memory_space=pl.ANY)],
        out_specs=pl.BlockSpec((s, tc), lambda b, c: (b, DN_WIDTH // tc + c)),
        out_shape=jax.ShapeDtypeStruct(y.shape, y.dtype),
        input_output_aliases={4: 0},
        compiler_params=_params("parallel", "parallel"),
        name="short_conv",
    )(p, p, p, conv_t, y)


def _dsa_body(*refs, seq):
    q_refs = refs[0:3]
    k_refs = refs[3:6]
    v_refs = refs[6:9]
    qn_ref, kn_ref, bias_ref, y_ref, qs_ref, kpad_ref, vpad_ref, og_ref, lse_ref = refs[9:]
    s = seq
    qb = DSA_QBLK
    side = DSA_SIDE
    width = qb + 2 * side
    kj = lax.broadcasted_iota(jnp.int32, (1, width), 1)
    zpad = jnp.zeros((DSA_PAD, DSA_HEAD_DIM), F32)
    for ref in (kpad_ref, vpad_ref):
        ref[0:DSA_PAD, :] = zpad
        ref[DSA_PAD + s:2 * DSA_PAD + s, :] = zpad

    for gi, (_, dil) in enumerate(DSA_PATTERNS):
        sub = s // dil
        nblk = sub // qb
        q = q_refs[gi][...]
        qs_ref[...] = (q * lax.rsqrt(jnp.mean(q * q, axis=-1, keepdims=True) + RMS_EPS) * qn_ref[...]
                       * (DSA_HEAD_DIM ** -0.5))
        k = k_refs[gi][...]
        kpad_ref[DSA_PAD:DSA_PAD + s, :] = k * lax.rsqrt(jnp.mean(k * k, axis=-1, keepdims=True) + RMS_EPS) * kn_ref[...]
        vpad_ref[DSA_PAD:DSA_PAD + s, :] = v_refs[gi][...]
        bias = bias_ref[0, gi]

        def block(t, gi=gi, dil=dil, sub=sub, nblk=nblk, bias=bias):
            r = t // nblk
            n = t % nblk
            rows = pl.ds(r + n * (qb * dil), qb, stride=dil)
            win = pl.ds(DSA_PAD + r + (n * qb - side) * dil, width, stride=dil)
            logits = _dot1(qs_ref[rows, :], kpad_ref[win, :], _NT) + bias
            yield
            pos = n * qb - side + kj
            logits = jnp.where((pos >= 0) & (pos < sub), logits, NEG_INF)
            m = jnp.max(logits, axis=-1, keepdims=True)
            yield
            p = jnp.exp(logits - m)
            ssum = jnp.sum(p, axis=-1, keepdims=True)
            o = _dot1(p, vpad_ref[win, :])
            yield
            og_ref[gi, rows, :] = o / ssum
            lse_ref[gi, rows, :] = jnp.broadcast_to(m + jnp.log(ssum), (qb, DSA_HEAD_DIM))

        nblocks = dil * nblk

        def blocks_step(i, _, block=block):
            _interleave(block(i * DSA_INTERLEAVE + u) for u in range(DSA_INTERLEAVE))
            return 0

        lax.fori_loop(0, nblocks // DSA_INTERLEAVE, blocks_step, 0)

    lse = [lse_ref[gi] for gi in range(DSA_GROUPS)]
    mx = jnp.maximum(jnp.maximum(lse[0], lse[1]), lse[2])
    ws = [jnp.exp(l - mx) for l in lse]
    num = ws[0] * og_ref[0] + ws[1] * og_ref[1] + ws[2] * og_ref[2]
    y_ref[...] = (num / (ws[0] + ws[1] + ws[2])).astype(y_ref.dtype)


def dilated_attention(p, qn, kn, bias_tab, batch):
    t = p.shape[0]
    s = t // batch
    assert (s // DSA_QBLK) % DSA_INTERLEAVE == 0
    hd = DSA_HEAD_DIM
    nh = DSA_HEADS

    def col(part, gi):
        return pl.BlockSpec((s, hd), lambda b, j: (b, part * nh + gi * DSA_HPG + j))

    in_specs = [col(part, gi) for part in range(3) for gi in range(DSA_GROUPS)]
    in_specs += [pl.BlockSpec((1, hd), lambda b, j: (0, 0)), pl.BlockSpec((1, hd), lambda b, j: (0, 0)),
                 pl.BlockSpec((1, DSA_GROUPS, DSA_QBLK, DSA_QBLK + 2 * DSA_SIDE), lambda b, j: (j, 0, 0, 0))]
    big = pltpu.VMEM((s, hd), F32)
    pad = pltpu.VMEM((s + 2 * DSA_PAD, hd), F32)
    grp = pltpu.VMEM((DSA_GROUPS, s, hd), F32)
    return pl.pallas_call(
        functools.partial(_dsa_body, seq=s),
        grid=(batch, DSA_HPG),
        in_specs=in_specs,
        out_specs=pl.BlockSpec((s, hd), lambda b, j: (b, j)),
        out_shape=jax.ShapeDtypeStruct((t, DSA_HPG * hd + RW_WIDTH), BF16),
        scratch_shapes=[big, pad, pad, grp, grp],
        compiler_params=_params("parallel", "parallel"),
        name="dilated_attention",
    )(*([p] * 9), qn.reshape(1, hd), kn.reshape(1, hd), bias_tab)


def _t5_bucket(rel):
    half = REL_BUCKETS // 2
    max_exact = half // 2
    n = np.abs(rel)
    scaled = (np.log(np.maximum(n, max_exact).astype(np.float32) / np.float32(max_exact))
              / np.float32(math.log(REL_MAX_DIST / max_exact)))
    large = np.minimum(max_exact + (scaled * np.float32(half - max_exact)).astype(np.int32), half - 1)
    return np.where(rel > 0, half, 0) + np.where(n < max_exact, n, large)


def _dsa_bias_table(rel_bias):
    width = DSA_QBLK + 2 * DSA_SIDE
    tabs = []
    for gi, (_, dil) in enumerate(DSA_PATTERNS):
        offs = np.arange(-DSA_SIDE, DSA_SIDE + 1, dtype=np.int32) * dil
        band = rel_bias[_t5_bucket(offs)][:, gi * DSA_HPG:(gi + 1) * DSA_HPG].astype(F32)
        fill = jnp.full((DSA_QBLK - 1, DSA_HPG), NEG_INF, F32)
        line = jnp.concatenate([fill, band, fill], axis=0)
        tabs.append(jnp.stack([line[DSA_QBLK - 1 - q:DSA_QBLK - 1 - q + width] for q in range(DSA_QBLK)]))
    return jnp.transpose(jnp.stack(tabs), (3, 0, 1, 2))


def _head_block_diag():
    ii = lax.broadcasted_iota(jnp.int32, (LANES, LANES), 0)
    jj = lax.broadcasted_iota(jnp.int32, (LANES, LANES), 1)
    return (ii // RW_HEAD_DIM) == (jj // RW_HEAD_DIM)


def _rw_prep_body(r_ref, k_ref, v_ref, lo_ref, mur_ref, muk_ref, muv_ref, mulo_ref, w0_ref, w2_ref, a0_ref,
                  a2_ref, g2_ref, kk_ref, ka_ref, rk_ref,
                  ro_ref, vo_ref, kko_ref, bon_ref, gate_ref, lwf_ref, lwb_ref, kdf_ref, kdb_ref, bbf_ref, bbb_ref):
    s = r_ref.shape[0]
    row = lax.broadcasted_iota(jnp.int32, (s, 1), 0)

    def mix(t, mu):
        return t + mu * (0.5 * (_shift_down(t, row) + _shift_up(t, row)) - t)

    r = mix(r_ref[...], mur_ref[...])
    kr = mix(k_ref[...], muk_ref[...])
    v = mix(v_ref[...], muv_ref[...])
    lo = mix(lo_ref[...], mulo_ref[...])
    bd = jnp.where(_head_block_diag(), 1.0, 0.0)

    def head_sum(t):
        return _dot_sum(t, bd)

    kk = kr * kk_ref[...]
    kk = kk * lax.rsqrt(head_sum(kk * kk) + L2_EPS)
    gd = lo[:, 4 * RW_LORA:]
    gate_ref[...] = _dot1(_sigmoid(gd), g2_ref[...])
    ro_ref[...] = r
    vo_ref[...] = v
    kko_ref[...] = kk
    bonus = jnp.zeros_like(r)
    outs = ((lwf_ref, kdf_ref, bbf_ref), (lwb_ref, kdb_ref, bbb_ref))
    for d in range(2):
        wd = lo[:, d * RW_LORA:(d + 1) * RW_LORA]
        ad = lo[:, (2 + d) * RW_LORA:(3 + d) * RW_LORA]
        w_log = -_softplus(-(w0_ref[d:d + 1, :] + _dot1(jnp.tanh(wd), w2_ref[d]))) - 0.5
        a = _sigmoid(a0_ref[d:d + 1, :] + _dot1(ad, a2_ref[d]))
        kd = kr * (1.0 + (a - 1.0) * ka_ref[...])
        lw_ref, kd_ref, bb_ref = outs[d]
        lw_ref[...] = -jnp.exp(w_log)
        kd_ref[...] = kd
        bb_ref[...] = kk * a
        bonus = bonus + head_sum(r * kd * rk_ref[...]) * v
    bon_ref[...] = bonus


def rwkv_prep(p, lora_in, mu, w0, w2, a0, a2, g2, k_k, k_a, r_k, col0, batch):
    t = p.shape[0]
    s = t // batch
    nct = RW_WIDTH // LANES
    base = col0 // LANES
    nlo = RW_LORA_IN
    col = lambda off: pl.BlockSpec((s, LANES), lambda b, c: (b, base + off * nct + c))
    vec = lambda off: pl.BlockSpec((1, LANES), lambda b, c: (0, off * nct + c))
    mu_main = mu[:RW_MAIN].reshape(1, RW_MAIN)
    mu_lo = mu[RW_MAIN:].reshape(1, nlo)
    out_spec = pl.BlockSpec((s, LANES), lambda b, c: (b, c))
    n_out = 11
    return pl.pallas_call(
        _rw_prep_body,
        grid=(batch, nct),
        in_specs=[col(0), col(1), col(2),
                  pl.BlockSpec((s, nlo), lambda b, c: (b, 0)),
                  vec(0), vec(1), vec(2),
                  pl.BlockSpec((1, nlo), lambda b, c: (0, 0)),
                  pl.BlockSpec((2, LANES), lambda b, c: (0, c)),
                  pl.BlockSpec((2, RW_LORA, LANES), lambda b, c: (0, 0, c)),
                  pl.BlockSpec((2, LANES), lambda b, c: (0, c)),
                  pl.BlockSpec((2, RW_LORA, LANES), lambda b, c: (0, 0, c)),
                  pl.BlockSpec((RW_GATE_LORA, LANES), lambda b, c: (0, c)),
                  vec(0), vec(0), vec(0)],
        out_specs=[out_spec] * n_out,
        out_shape=[jax.ShapeDtypeStruct((t, RW_WIDTH), F32)] * n_out,
        compiler_params=_params("parallel", "parallel"),
        name="rwkv_prep",
    )(p, p, p, lora_in, mu_main, mu_main, mu_main, mu_lo, w0, w2, a0, a2, g2,
      k_k.reshape(1, RW_WIDTH), k_a.reshape(1, RW_WIDTH), r_k.reshape(1, RW_WIDTH))


def _rw_scan_body(r_ref, v_ref, kk_ref, bon_ref, gate_ref, lwf_ref, lwb_ref, kdf_ref, kdb_ref, bbf_ref, bbb_ref,
                  lnw_ref, lnb_ref, y_in_ref, y_ref, cum_ref, cumt_ref, kdt_ref, bbt_ref, yf_ref, yb_ref,
                  tr_ref, tc_ref, rbk_ref, bkt_ref):
    s = r_ref.shape[0]
    c = CHUNK
    c2 = 2 * c
    row = lax.broadcasted_iota(jnp.int32, (s, 1), 0)
    pos = row % c
    lw_refs, kd_refs, bb_refs = (lwf_ref, lwb_ref), (kdf_ref, kdb_ref), (bbf_ref, bbb_ref)
    for d in range(2):
        cum = _seg_cumsum(lw_refs[d][...], pos, d == 1)
        cum_ref[d] = cum
        cumt_ref[d] = cum.T
        kdt_ref[d] = kd_refs[d][...].T
        bbt_ref[d] = bb_refs[d][...].T

    masks = _tri_masks(c2, c)
    merge = {rev: _merge_masks(c2, rev) for rev in (False, True)}
    eye2_f = jnp.where(masks["eye"], 1.0, 0.0)
    head0 = lax.broadcasted_iota(jnp.int32, (1, LANES), 1) < RW_HEAD_DIM
    head_bd = _head_block_diag()
    npair = s // PAIR
    group = min(RW_PREP_PAIRS, npair)

    def stack(x):
        return jnp.concatenate([jnp.where(head0, x, 0.0), jnp.where(head0, 0.0, x)], axis=0)

    def prep(pidx, d, half):
        reverse = d == 1
        incl2, strict2 = masks[reverse]
        cidx = 2 * pidx + half
        cols = pl.ds(pl.multiple_of(pidx * PAIR, PAIR), PAIR)
        rows = pl.ds(pl.multiple_of(cidx * c, c), c)
        hs = slice(half * c, (half + 1) * c)
        r, v, kk = r_ref[rows, :], v_ref[rows, :], kk_ref[rows, :]
        lw, kd, bb = lw_refs[d][rows, :], kd_refs[d][rows, :], bb_refs[d][rows, :]
        cum = cum_ref[d, rows, :]
        e_neg = jnp.exp(-cum)
        a_s = stack(-kk * jnp.exp(cum - lw))
        r_s = stack(r * jnp.exp(cum))
        ar = jnp.concatenate([a_s, r_s], axis=0)
        bk = jnp.concatenate([stack(bb * e_neg), stack(kd * e_neg)], axis=0)
        g = _dot1(ar, bk, _NT)
        yield
        rbk_ref[d, cidx] = jnp.concatenate([jnp.where(incl2, g[c2:, :c2], 0.0),
                                            jnp.where(incl2, g[c2:, c2:], 0.0)], axis=1).astype(BF16)
        av = _dot1(jnp.where(strict2, g[:c2, c2:], 0.0), stack(v))
        yield
        t_inv = []
        yield from _tri_inverse(-jnp.where(strict2, g[:c2, :c2], 0.0), eye2_f, merge[reverse], t_inv)
        tt = _dot1(t_inv[0], jnp.concatenate([a_s, av], axis=1))
        yield
        tr_ref[d, cidx] = jnp.concatenate([tt[:, :LANES], r_s], axis=0).astype(BF16)
        tc_ref[d, cidx] = tt[:, LANES:].astype(BF16)
        cum_t = cumt_ref[d, :, cols][:, hs]
        last = 0 if reverse else c - 1
        e_out_t = jnp.exp(cum_t[:, last:last + 1] - cum_t)
        bkt_ref[d, cidx] = jnp.concatenate([bbt_ref[d, :, cols][:, hs] * e_out_t,
                                            kdt_ref[d, :, cols][:, hs] * e_out_t], axis=1).astype(BF16)

    def seq(states, pidx, d):
        cum_t2 = cumt_ref[d, :, pl.ds(pl.multiple_of(pidx * PAIR, PAIR), PAIR)]
        for half in ((1, 0) if d else (0, 1)):
            cidx = 2 * pidx + half
            rows = pl.ds(pl.multiple_of(cidx * c, c), c)
            v = v_ref[rows, :]
            x = _dot1(tr_ref[d, cidx], states[d])
            yield
            ps = x[:c2] + tc_ref[d, cidx].astype(F32)
            os_ = x[c2:] + _dot1(rbk_ref[d, cidx], jnp.concatenate([ps, stack(v)], axis=0))
            y = os_[:c] + os_[c:]
            if d:
                yb_ref[rows, :] = y
            else:
                yf_ref[rows, :] = y
            pv = jnp.concatenate([ps[:c] + ps[c:], v], axis=0)
            last = half * c + (0 if d else c - 1)
            e_tot = jnp.exp(cum_t2[:, last:last + 1])
            states[d] = states[d] * e_tot + jnp.where(head_bd, _dot1(bkt_ref[d, cidx], pv), 0.0)
            yield

    ngroups = npair // group

    def prep_chains(g):
        return [prep(npair - 1 - (g * group + j) if d else g * group + j, d, half)
                for j in range(group) for d in range(2) for half in range(2)]

    def seq_chain(states, g, d):
        for j in range(group):
            yield from seq(states, npair - 1 - (g * group + j) if d else g * group + j, d)

    def step(g, carry):
        states = list(carry)
        _interleave(prep_chains(g + 1) + [seq_chain(states, g, 0), seq_chain(states, g, 1)])
        return tuple(states)

    zero = jnp.zeros((LANES, LANES), F32)
    _interleave(prep_chains(0))
    states = list(lax.fori_loop(0, ngroups - 1, step, (zero, zero)))
    _interleave([seq_chain(states, ngroups - 1, 0), seq_chain(states, ngroups - 1, 1)])

    bd = jnp.where(head_bd, 1.0, 0.0)
    y = yf_ref[...] + yb_ref[...]
    mean = _dot_sum(y, bd) * (1.0 / RW_HEAD_DIM)
    yc = y - mean
    var = _dot_sum(yc * yc, bd) * (1.0 / RW_HEAD_DIM)
    yn = yc * lax.rsqrt(var + RW_GN_EPS) * lnw_ref[...] + lnb_ref[...]
    y_ref[...] = ((yn + bon_ref[...]) * gate_ref[...]).astype(y_ref.dtype)


def rwkv_scan(prep, ln_w, ln_b, y, batch):
    t = prep[0].shape[0]
    s = t // batch
    assert s % PAIR == 0 and (s // PAIR) % min(RW_PREP_PAIRS, s // PAIR) == 0
    nct = RW_WIDTH // LANES
    nchunk = s // CHUNK
    blk = pl.BlockSpec((s, LANES), lambda b, c: (b, c))
    vec = pl.BlockSpec((1, LANES), lambda b, c: (0, c))
    tsp = pltpu.VMEM((2, LANES, s), F32)
    big = pltpu.VMEM((s, LANES), F32)
    per_chunk = lambda rows, cols: pltpu.VMEM((2, nchunk, rows, cols), BF16)
    return pl.pallas_call(
        _rw_scan_body,
        grid=(batch, nct),
        in_specs=[blk] * 11 + [vec, vec, pl.BlockSpec(memory_space=pl.ANY)],
        out_specs=pl.BlockSpec((s, LANES), lambda b, c: (b, (y.shape[1] - RW_WIDTH) // LANES + c)),
        out_shape=jax.ShapeDtypeStruct(y.shape, y.dtype),
        input_output_aliases={13: 0},
        scratch_shapes=[pltpu.VMEM((2, s, LANES), F32), tsp, tsp, tsp, big, big,
                        per_chunk(2 * PAIR, LANES), per_chunk(PAIR, LANES), per_chunk(PAIR, 2 * LANES),
                        per_chunk(PAIR, LANES)],
        compiler_params=_params("parallel", "parallel"),
        name="rwkv_scan",
    )(*prep, ln_w.reshape(1, RW_WIDTH), ln_b.reshape(1, RW_WIDTH), y)


def even_mixer(x, g, w_in, w_sc, w_ab, w_out, layer, conv_qkv, a_log, dt_bias, out_gain, conv_sc, batch):
    p, ab = norm_mm(x, g, w_in, layer, ncols=4 * DN_WIDTH, w2=w_sc, wp_t=w_ab.T)
    zeros = jnp.zeros((2 * DN_HEADS,), F32)
    alog32 = jnp.concatenate([zeros, a_log.reshape(-1)]).reshape(1, -1)
    dt32 = jnp.concatenate([zeros, dt_bias.reshape(-1)]).reshape(1, -1)
    y = deltanet(p, ab, conv_qkv.T, alog32, dt32, out_gain, batch)
    y = short_conv(p, conv_sc.T, 4 * DN_WIDTH, y, batch)
    return mm_res(y, w_out, layer, x)


def odd_mixer(x, g, w_in, w_lora, w_out, layer, qn, kn, bias_tab, mu, w0, w2, a0, a2, g2, k_k, k_a, r_k, ln_w, ln_b,
              batch):
    p, lora_in = norm_mm(x, g, w_in, layer, ncols=3 * DSA_QKV + RW_MAIN, wp=w_lora)
    y = dilated_attention(p, qn, kn, bias_tab, batch)
    prep = rwkv_prep(p, lora_in, mu, w0, w2, a0, a2, g2, k_k, k_a, r_k, 3 * DSA_QKV, batch)
    y = rwkv_scan(prep, ln_w, ln_b, y, batch)
    return mm_res(y, w_out, layer, x)


def kernel(x, mem, rel_bias, norm_mix, norm_xattn, norm_mem, norm_ffn, xa_wq, xa_wk, xa_wv, xa_wo, xa_qn, xa_kn, ffn_w1, ffn_w2, ev_w_in, ev_w_out, dn_conv, dn_a_log, dn_dt_bias, dn_norm, sc_conv, od_w_in, od_w_out, ca_qn, ca_kn, rw_mu, rw_w0, rw_w2, rw_a0, rw_a2, rw_g2, rw_k_k, rw_k_a, rw_r_k, rw_ln_w, rw_ln_b):
    batch, seq, d = x.shape
    n_mem = mem.shape[1]
    xf = x.reshape(batch * seq, d)
    memf = mem.reshape(batch * n_mem, d)
    bias_tab = _dsa_bias_table(rel_bias)
    c_ab = 4 * DN_WIDTH
    c_sc = c_ab + 4 * DN_HEADS
    c_lo = 3 * DSA_QKV + RW_MAIN
    ev_in, od_in = ev_w_in.astype(BF16), od_w_in.astype(BF16)
    ev_sc = ev_in[:, :, c_sc:]
    ev_out, od_out = ev_w_out.astype(BF16), od_w_out.astype(BF16)
    w_q, w_o = xa_wq.astype(BF16), xa_wo.astype(BF16)
    w_kv = jnp.concatenate([xa_wk, xa_wv], axis=2).astype(BF16)
    for layer in range(DEPTH):
        i = layer // 2
        if layer % 2 == 0:
            xf = even_mixer(xf, norm_mix[layer], ev_in, ev_sc, ev_w_in[i, :, c_ab:c_sc], ev_out, i, dn_conv[i],
                            dn_a_log[i], dn_dt_bias[i], dn_norm[i], sc_conv[i], batch)
        else:
            xf = odd_mixer(xf, norm_mix[layer], od_in, od_w_in[i, :, c_lo:], od_out, i, ca_qn[i], ca_kn[i], bias_tab,
                           rw_mu[i], rw_w0[i], rw_w2[i], rw_a0[i], rw_a2[i], rw_g2[i], rw_k_k[i], rw_k_a[i], rw_r_k[i],
                           rw_ln_w[i], rw_ln_b[i], batch)
        kv = norm_mm(memf, norm_mem[layer], w_kv, layer).reshape(batch, n_mem, 2 * XA_WIDTH)
        xf = xattn(xf, norm_xattn[layer], w_q, kv, w_o, layer, xa_qn[layer], xa_kn[layer], batch)
        h1 = norm_mm(xf, norm_ffn[layer], ffn_w1, layer, act="relu2", out_dtype=BF16)
        xf = mm_res(h1, ffn_w2, layer, xf)
    return xf.reshape(batch, seq, d)
```

```python
import functools
import math

import jax
import jax.numpy as jnp
import numpy as np
from jax import lax
from jax.experimental import pallas as pl
from jax.experimental.pallas import tpu as pltpu

F32 = jnp.float32
BF16 = jnp.bfloat16

D_MODEL = 2048
DEPTH = 4
RMS_EPS = 1e-6
L2_EPS = 1e-6

DN_HEADS = 8
DN_HEAD_DIM = 128
DN_WIDTH = DN_HEADS * DN_HEAD_DIM
SC_WIDTH = D_MODEL - DN_WIDTH
CHUNK = 64
PAIR = 2 * CHUNK
DN_PREP_PAIRS = 4
DN_HEADS_PER_STEP = 2
RW_PREP_PAIRS = 2

DSA_PATTERNS = ((128, 1), (512, 4), (2048, 16))
DSA_GROUPS = len(DSA_PATTERNS)
DSA_HPG = 4
DSA_HEAD_DIM = 128
DSA_HEADS = DSA_GROUPS * DSA_HPG
DSA_QKV = DSA_HEADS * DSA_HEAD_DIM
DSA_SIDE = 64
DSA_QBLK = 128
DSA_PAD = DSA_SIDE * max(d for _, d in DSA_PATTERNS)
DSA_INTERLEAVE = 4
REL_BUCKETS = 32
REL_MAX_DIST = 1024
NEG_INF = -1e30

RW_HEADS = 8
RW_HEAD_DIM = 64
RW_WIDTH = RW_HEADS * RW_HEAD_DIM
RW_LORA = 64
RW_GATE_LORA = 128
RW_MAIN = 3 * RW_WIDTH
RW_LORA_IN = 4 * RW_LORA + RW_GATE_LORA
RW_GN_EPS = 64e-5

XA_HEADS = 4
XA_HEAD_DIM = 128
XA_WIDTH = XA_HEADS * XA_HEAD_DIM

LANES = 128
NORM_ROWS = 256
VMEM_LIMIT_BYTES = 56 * 1024 * 1024


def _params(*sem):
    return pltpu.CompilerParams(dimension_semantics=sem, vmem_limit_bytes=VMEM_LIMIT_BYTES)


_NN = (((1,), (0,)), ((), ()))
_NT = (((1,), (1,)), ((), ()))


def _dot1(a, b, dims=_NN):
    return lax.dot_general(a.astype(BF16), b.astype(BF16), dims, preferred_element_type=F32)


def _dot_sum(a, b01):
    b = b01.astype(BF16)
    hi = a.astype(BF16)
    lo = (a - hi.astype(F32)).astype(BF16)
    dg = functools.partial(lax.dot_general, dimension_numbers=_NN, preferred_element_type=F32)
    return dg(hi, b) + dg(lo, b)


def _interleave(chains):
    chains = list(chains)
    while chains:
        alive = []
        for ch in chains:
            try:
                next(ch)
                alive.append(ch)
            except StopIteration:
                pass
        chains = alive


def _merge_masks(n, reverse):
    ii = lax.broadcasted_iota(jnp.int32, (n, n), 0)
    jj = lax.broadcasted_iota(jnp.int32, (n, n), 1)
    tri = (ii < jj) if reverse else (ii > jj)
    out = []
    size = 1
    while size < CHUNK:
        out.append(tri & ((ii // (2 * size)) == (jj // (2 * size))) & ((ii // size) != (jj // size)))
        size *= 2
    return out


def _tri_inverse(a, eye_f, level_masks, out):
    t = eye_f - jnp.where(level_masks[0], a, 0.0)
    for m in level_masks[1:]:
        x = _dot1(jnp.where(m, a, 0.0), t)
        yield
        t = t - _dot1(t, x)
        yield
    out.append(t)


def _sigmoid(x):
    return 1.0 / (1.0 + jnp.exp(-x))


def _softplus(x):
    return jnp.maximum(x, 0.0) + jnp.log1p(jnp.exp(-jnp.abs(x)))


def _shift_down(x, row):
    return jnp.where(row == 0, 0.0, pltpu.roll(x, 1, 0))


def _shift_up(x, row):
    n = x.shape[0]
    return jnp.where(row == n - 1, 0.0, pltpu.roll(x, n - 1, 0))


def _norm_mm_body(x_ref, g_ref, w_ref, *rest, act, precise, precise_t, n_first):
    if n_first is not None:
        w2_ref, rest = rest[0], rest[1:]
    if precise:
        wp_ref, o_ref, op_ref, xn_ref = rest
    else:
        o_ref, xn_ref = rest

    @pl.when(pl.program_id(1) == 0)
    def _():
        def rows_step(r, _):
            rows = pl.ds(pl.multiple_of(r * NORM_ROWS, NORM_ROWS), NORM_ROWS)
            x = x_ref[rows, :]
            xn = x * lax.rsqrt(jnp.mean(x * x, axis=-1, keepdims=True) + RMS_EPS) * g_ref[...]
            xn_ref[rows, :] = xn.astype(BF16)
            if precise and precise_t:
                op_ref[:, rows] = _dot1(wp_ref[...], xn, _NT)
            elif precise:
                op_ref[rows, :] = _dot1(xn, wp_ref[...])
            return 0

        lax.fori_loop(0, x_ref.shape[0] // NORM_ROWS, rows_step, 0)

    def project(wt_ref):
        acc = jnp.dot(xn_ref[...], wt_ref[...].astype(BF16), preferred_element_type=F32)
        if act == "relu2":
            acc = jnp.square(jnp.maximum(acc, 0.0))
        o_ref[...] = acc.astype(o_ref.dtype)

    if n_first is None:
        project(w_ref)
    else:
        pl.when(pl.program_id(1) < n_first)(lambda: project(w_ref))
        pl.when(pl.program_id(1) >= n_first)(lambda: project(w2_ref))


def norm_mm(x, g, w, layer, ncols=None, w2=None, wp=None, wp_t=None, act=None, out_dtype=F32, tm=1024, tn=1024):
    t, d = x.shape
    n1 = w.shape[2] if ncols is None else ncols
    n = n1 + (0 if w2 is None else w2.shape[2])
    tm = min(tm, t)
    tn = min(tn, n1)
    assert t % tm == 0 and n1 % tn == 0 and n % tn == 0 and tm % NORM_ROWS == 0
    precise = wp is not None or wp_t is not None
    n_first = None if w2 is None else n1 // tn
    in_specs = [
        pl.BlockSpec((tm, d), lambda i, j: (i, 0)),
        pl.BlockSpec((1, d), lambda i, j: (0, 0)),
    ]
    args = [x, g.reshape(1, d), w]
    if w2 is None:
        in_specs.append(pl.BlockSpec((None, d, tn), lambda i, j: (layer, 0, j)))
    else:
        in_specs.append(pl.BlockSpec((None, d, tn), lambda i, j: (layer, 0, jnp.minimum(j, n_first - 1))))
        in_specs.append(pl.BlockSpec((None, d, tn), lambda i, j: (layer, 0, jnp.maximum(j - n_first, 0))))
        args.append(w2)
    out_specs = pl.BlockSpec((tm, tn), lambda i, j: (i, j))
    out_shape = jax.ShapeDtypeStruct((t, n), out_dtype)
    if wp is not None:
        npc = wp.shape[1]
        in_specs.append(pl.BlockSpec((d, npc), lambda i, j: (0, 0)))
        out_specs = [out_specs, pl.BlockSpec((tm, npc), lambda i, j: (i, 0))]
        out_shape = [out_shape, jax.ShapeDtypeStruct((t, npc), F32)]
        args.append(wp)
    elif wp_t is not None:
        npc = wp_t.shape[0]
        in_specs.append(pl.BlockSpec((npc, d), lambda i, j: (0, 0)))
        out_specs = [out_specs, pl.BlockSpec((npc, tm), lambda i, j: (0, i))]
        out_shape = [out_shape, jax.ShapeDtypeStruct((npc, t), F32)]
        args.append(wp_t)
    return pl.pallas_call(
        functools.partial(_norm_mm_body, act=act, precise=precise, precise_t=wp_t is not None, n_first=n_first),
        grid=(t // tm, n // tn),
        in_specs=in_specs,
        out_specs=out_specs,
        out_shape=out_shape,
        scratch_shapes=[pltpu.VMEM((tm, d), BF16)],
        compiler_params=_params("parallel", "arbitrary"),
        name="norm_mm",
    )(*args)


def _mm_res_body(a_ref, *rest, n_first):
    if n_first is not None:
        a2_ref, rest = rest[0], rest[1:]
    w_ref, r_ref, o_ref = rest
    kk = pl.program_id(2)

    def accumulate(lhs_ref):
        acc = jnp.dot(lhs_ref[...], w_ref[...].astype(BF16), preferred_element_type=F32)

        @pl.when(kk == 0)
        def _():
            o_ref[...] = r_ref[...] + acc

        @pl.when(kk > 0)
        def _():
            o_ref[...] += acc

    if n_first is None:
        accumulate(a_ref)
    else:
        pl.when(kk < n_first)(lambda: accumulate(a_ref))
        pl.when(kk >= n_first)(lambda: accumulate(a2_ref))


def mm_res(a, w, layer, res, a2=None, tm=1024, tn=1024, tk=2048):
    t, k1 = a.shape
    k = k1 + (0 if a2 is None else a2.shape[1])
    n = w.shape[2]
    tm, tn, tk = min(tm, t), min(tn, n), min(tk, k1)
    assert t % tm == 0 and n % tn == 0 and k1 % tk == 0 and k % tk == 0
    n_first = None if a2 is None else k1 // tk
    if a2 is None:
        in_specs = [pl.BlockSpec((tm, tk), lambda i, j, kk: (i, kk))]
        args = [a]
    else:
        in_specs = [pl.BlockSpec((tm, tk), lambda i, j, kk: (i, jnp.minimum(kk, n_first - 1))),
                    pl.BlockSpec((tm, tk), lambda i, j, kk: (i, jnp.maximum(kk - n_first, 0)))]
        args = [a, a2]
    in_specs += [
        pl.BlockSpec((None, tk, tn), lambda i, j, kk: (layer, kk, j)),
        pl.BlockSpec((tm, tn), lambda i, j, kk: (i, j)),
    ]
    return pl.pallas_call(
        functools.partial(_mm_res_body, n_first=n_first),
        grid=(t // tm, n // tn, k // tk),
        in_specs=in_specs,
        out_specs=pl.BlockSpec((tm, tn), lambda i, j, kk: (i, j)),
        out_shape=jax.ShapeDtypeStruct((t, n), F32),
        compiler_params=_params("parallel", "parallel", "arbitrary"),
        name="mm_res",
    )(*args, w, res)


def _xattn_body(x_ref, g_ref, wq_ref, kv_ref, wo_ref, qn_ref, kn_ref, o_ref):
    x = x_ref[...]
    xn = x * lax.rsqrt(jnp.mean(x * x, axis=-1, keepdims=True) + RMS_EPS) * g_ref[...]
    q = jnp.dot(xn.astype(BF16), wq_ref[...], preferred_element_type=F32)
    kv = kv_ref[0]
    outs = []
    for h in range(XA_HEADS):
        sl = slice(h * XA_HEAD_DIM, (h + 1) * XA_HEAD_DIM)
        qh = q[:, sl]
        qh = qh * lax.rsqrt(jnp.mean(qh * qh, axis=-1, keepdims=True) + RMS_EPS) * qn_ref[...]
        kh = kv[:, sl]
        kh = kh * lax.rsqrt(jnp.mean(kh * kh, axis=-1, keepdims=True) + RMS_EPS) * kn_ref[...]
        vh = kv[:, XA_WIDTH + h * XA_HEAD_DIM:XA_WIDTH + (h + 1) * XA_HEAD_DIM]
        logits = _dot1(qh, kh, _NT) * (XA_HEAD_DIM ** -0.5)
        m = jnp.max(logits, axis=-1, keepdims=True)
        p = jnp.exp(logits - m)
        s = jnp.sum(p, axis=-1, keepdims=True)
        outs.append(_dot1(p, vh) / s)
    o = jnp.concatenate(outs, axis=-1).astype(BF16)
    o_ref[...] = x + jnp.dot(o, wo_ref[...], preferred_element_type=F32)


def xattn(x, g, wq, kv, wo, layer, qn, kn, batch, ts=512):
    t, d = x.shape
    s = t // batch
    ts = min(ts, s)
    nst = s // ts
    m = kv.shape[1]
    return pl.pallas_call(
        _xattn_body,
        grid=(batch, nst),
        in_specs=[
            pl.BlockSpec((ts, d), lambda b, i: (b * nst + i, 0)),
            pl.BlockSpec((1, d), lambda b, i: (0, 0)),
            pl.BlockSpec((None, d, XA_WIDTH), lambda b, i: (layer, 0, 0)),
            pl.BlockSpec((1, m, 2 * XA_WIDTH), lambda b, i: (b, 0, 0)),
            pl.BlockSpec((None, XA_WIDTH, d), lambda b, i: (layer, 0, 0)),
            pl.BlockSpec((1, XA_HEAD_DIM), lambda b, i: (0, 0)),
            pl.BlockSpec((1, XA_HEAD_DIM), lambda b, i: (0, 0)),
        ],
        out_specs=pl.BlockSpec((ts, d), lambda b, i: (b * nst + i, 0)),
        out_shape=jax.ShapeDtypeStruct((t, d), F32),
        compiler_params=_params("parallel", "parallel"),
        name="xattn",
    )(x, g.reshape(1, d), wq, kv, wo, qn.reshape(1, -1), kn.reshape(1, -1))


def _tri_masks(n, blk):
    ii = lax.broadcasted_iota(jnp.int32, (n, n), 0)
    jj = lax.broadcasted_iota(jnp.int32, (n, n), 1)
    same = (ii // blk) == (jj // blk) if n != blk else None

    def m(c):
        return c if same is None else (c & same)

    return {
        False: (m(ii >= jj), m(ii > jj)),
        True: (m(ii <= jj), m(ii < jj)),
        "eye": ii == jj,
    }


def _seg_cumsum(x, pos, reverse):
    n = x.shape[0]
    sh = 1
    while sh < CHUNK:
        if reverse:
            x = x + jnp.where(pos < CHUNK - sh, pltpu.roll(x, n - sh, 0), 0.0)
        else:
            x = x + jnp.where(pos >= sh, pltpu.roll(x, sh, 0), 0.0)
        sh *= 2
    return x


def _dn_body(q_ref, k_ref, v_ref, gate_ref, ab_ref, cq_ref, ck_ref, cv_ref, alog_ref, dt_ref, gain_ref,
             y_ref, qs_ref, ks_ref, vs_ref, kt_ref, bg_ref, gt_ref, of_ref, ob_ref,
             u_ref, w_ref, qg_ref, qk_ref, kdt_ref, eg_ref, st_ref):
    s = q_ref.shape[0]
    hd = DN_HEAD_DIM
    nh = DN_HEADS_PER_STEP
    row = lax.broadcasted_iota(jnp.int32, (s, 1), 0)
    c = CHUNK
    lane32 = lax.broadcasted_iota(jnp.int32, (1, 4 * DN_HEADS), 1)
    lane128 = lax.broadcasted_iota(jnp.int32, (1, LANES), 1)
    pos_t = lax.broadcasted_iota(jnp.int32, (1, s), 1) % c

    def conv_silu(x_ref, cw_ref, lanes):
        x = x_ref[:, lanes]
        w = cw_ref[:, lanes]
        y = _shift_down(x, row) * w[0:1] + x * w[1:2] + _shift_up(x, row) * w[2:3]
        return y * _sigmoid(y)

    for hh in range(nh):
        h = pl.program_id(1) * nh + hh
        lanes = slice(hh * hd, (hh + 1) * hd)
        q = conv_silu(q_ref, cq_ref, lanes)
        q = q * lax.rsqrt(jnp.sum(q * q, axis=-1, keepdims=True) + L2_EPS) * (hd ** -0.5)
        qs_ref[hh] = q
        k = conv_silu(k_ref, ck_ref, lanes)
        k = k * lax.rsqrt(jnp.sum(k * k, axis=-1, keepdims=True) + L2_EPS)
        ks_ref[hh] = k
        kt_ref[hh] = k.T
        vs_ref[hh] = conv_silu(v_ref, cv_ref, lanes)

        def decay_row(d, h=h):
            idx = (2 + d) * DN_HEADS + h
            a_log = jnp.sum(jnp.where(lane32 == idx, alog_ref[...], 0.0), axis=-1, keepdims=True)
            dt = jnp.sum(jnp.where(lane32 == idx, dt_ref[...], 0.0), axis=-1, keepdims=True)
            g = -jnp.exp(a_log) * _softplus(ab_ref[pl.ds(idx, 1), :] + dt)
            sh = 1
            while sh < c:
                if d:
                    g = g + jnp.where(pos_t < c - sh, pltpu.roll(g, s - sh, 1), 0.0)
                else:
                    g = g + jnp.where(pos_t >= sh, pltpu.roll(g, sh, 1), 0.0)
                sh *= 2
            return g

        rows = [_sigmoid(ab_ref[pl.ds(h, 1), :]), _sigmoid(ab_ref[pl.ds(DN_HEADS + h, 1), :]),
                decay_row(0), decay_row(1)]
        gt = jnp.concatenate(rows + [jnp.zeros((LANES - len(rows), s), F32)], axis=0)
        gt_ref[hh] = gt[0:8, :]
        bg_ref[hh] = gt.T
        for d in range(2):
            st_ref[hh, d] = jnp.zeros((hd, hd), F32)

    masks = _tri_masks(PAIR, c)
    merge = {rev: _merge_masks(PAIR, rev) for rev in (False, True)}
    eye_f = jnp.where(masks["eye"], 1.0, 0.0)
    npair = s // PAIR
    group = min(DN_PREP_PAIRS, npair)

    def prep(hh, pidx, d):
        reverse = d == 1
        rows = pl.ds(pl.multiple_of(pidx * PAIR, PAIR), PAIR)
        qp, kp, vp = qs_ref[hh, rows, :], ks_ref[hh, rows, :], vs_ref[hh, rows, :]
        bgp = bg_ref[hh, rows, :]
        beta = bgp[:, d:d + 1]
        gcol = bgp[:, 2 + d:3 + d]
        grow = gt_ref[hh, 2 + d:3 + d, rows]
        incl, strict = masks[reverse]
        decay = jnp.where(incl, jnp.exp(jnp.where(incl, gcol - grow, 0.0)), 0.0)
        kb = kp * beta
        kq = _dot1(jnp.concatenate([kb, qp], axis=0), kp, _NT)
        yield
        qk_ref[hh, d, rows, :] = jnp.where(incl, kq[PAIR:] * decay, 0.0).astype(BF16)
        t_inv = []
        yield from _tri_inverse(jnp.where(strict, kq[:PAIR] * decay, 0.0), eye_f, merge[reverse], t_inv)
        egc = jnp.exp(gcol)
        uw = _dot1(t_inv[0], jnp.concatenate([vp * beta, kb * egc], axis=1))
        yield
        u_ref[hh, d, rows, :] = uw[:, :hd]
        w_ref[hh, d, rows, :] = uw[:, hd:].astype(BF16)
        qg_ref[hh, d, rows, :] = (qp * egc).astype(BF16)
        last0, last1 = (0, c) if reverse else (c - 1, PAIR - 1)
        glast = jnp.where(lane128 < c, grow[:, last0:last0 + 1], grow[:, last1:last1 + 1])
        kdt_ref[hh, d, :, rows] = (kt_ref[hh, :, rows] * jnp.exp(glast - grow)).astype(BF16)
        eg_ref[hh, d, :, rows] = jnp.broadcast_to(jnp.exp(glast), (8, PAIR))

    def seq(hh, pidx, d):
        pair_rows = pl.ds(pl.multiple_of(pidx * PAIR, PAIR), PAIR)
        kdt = kdt_ref[hh, d, :, pair_rows]
        eg = eg_ref[hh, d, 0:1, pair_rows]
        for half in ((1, 0) if d else (0, 1)):
            r0 = half * c
            rows = pl.ds(pl.multiple_of(pidx * PAIR + r0, c), c)
            state = st_ref[hh, d]
            ws = _dot1(jnp.concatenate([w_ref[hh, d, rows, :], qg_ref[hh, d, rows, :]], axis=0), state)
            yield
            v_new = u_ref[hh, d, rows, :] - ws[:c]
            o = ws[c:] + _dot1(qk_ref[hh, d, rows, r0:r0 + c], v_new)
            if d:
                ob_ref[hh, rows, :] = o
            else:
                of_ref[hh, rows, :] = o
            st_ref[hh, d] = state * eg[:, r0:r0 + 1] + _dot1(kdt[:, r0:r0 + c], v_new)
            yield

    ngroups = npair // group

    def pair_of(g, j, d):
        return npair - 1 - (g * group + j) if d else g * group + j

    def prep_chains(g):
        return [prep(hh, pair_of(g, j, d), d) for j in range(group) for hh in range(nh) for d in range(2)]

    def seq_chain(g, hh, d):
        for j in range(group):
            yield from seq(hh, pair_of(g, j, d), d)

    def seq_chains(g):
        return [seq_chain(g, hh, d) for hh in range(nh) for d in range(2)]

    def step(g, _):
        _interleave(prep_chains(g + 1) + seq_chains(g))
        return 0

    _interleave(prep_chains(0))
    lax.fori_loop(0, ngroups - 1, step, 0)
    _interleave(seq_chains(ngroups - 1))

    for hh in range(nh):
        lanes = slice(hh * hd, (hh + 1) * hd)
        o = of_ref[hh] + ob_ref[hh]
        o = o * lax.rsqrt(jnp.mean(o * o, axis=-1, keepdims=True) + RMS_EPS) * gain_ref[...]
        gate = gate_ref[:, lanes]
        y_ref[:, lanes] = (o * (gate * _sigmoid(gate))).astype(y_ref.dtype)


def deltanet(p, ab, conv_t, alog32, dt32, gain, batch):
    t = p.shape[0]
    s = t // batch
    assert s % PAIR == 0 and (s // PAIR) % min(DN_PREP_PAIRS, s // PAIR) == 0
    hd = DN_HEAD_DIM
    nh = DN_HEADS_PER_STEP
    nsteps = DN_HEADS // nh
    col = lambda off: pl.BlockSpec((s, nh * hd), lambda b, h: (b, off * nsteps + h))
    cw = lambda off: pl.BlockSpec((3, nh * hd), lambda b, h: (0, off * nsteps + h))
    gate_col = pl.BlockSpec((s, nh * hd), lambda b, h: (b, 3 * nsteps + h), pipeline_mode=pl.Buffered(1))
    small = lambda n: pl.BlockSpec((1, n), lambda b, h: (0, 0))
    tok = pltpu.VMEM((nh, s, hd), F32)
    per_dir = pltpu.VMEM((nh, 2, s, hd), BF16)
    return pl.pallas_call(
        _dn_body,
        grid=(batch, nsteps),
        in_specs=[col(0), col(1), col(2), gate_col,
                  pl.BlockSpec((4 * DN_HEADS, s), lambda b, h: (0, b)),
                  cw(0), cw(1), cw(2),
                  small(4 * DN_HEADS), small(4 * DN_HEADS), small(hd)],
        out_specs=pl.BlockSpec((s, nh * hd), lambda b, h: (b, h)),
        out_shape=jax.ShapeDtypeStruct((t, DN_WIDTH), BF16),
        scratch_shapes=[tok, tok, tok, pltpu.VMEM((nh, hd, s), F32), tok, pltpu.VMEM((nh, 8, s), F32), tok, tok,
                        pltpu.VMEM((nh, 2, s, hd), F32), per_dir, per_dir, per_dir,
                        pltpu.VMEM((nh, 2, hd, s), BF16), pltpu.VMEM((nh, 2, 8, s), F32),
                        pltpu.VMEM((nh, 2, hd, hd), F32)],
        compiler_params=_params("parallel", "parallel"),
        name="deltanet",
    )(p, p, p, p, ab, conv_t, conv_t, conv_t, alog32, dt32, gain.reshape(1, hd))


def _sconv_body(b_ref, c_ref, u_ref, w_ref, y_ref):
    s = b_ref.shape[0]
    row = lax.broadcasted_iota(jnp.int32, (s, 1), 0)
    cu = c_ref[...] * u_ref[...]
    w = w_ref[...]
    y = _shift_down(cu, row) * w[0:1] + cu * w[1:2] + _shift_up(cu, row) * w[2:3]
    y_ref[...] = (b_ref[...] * y).astype(y_ref.dtype)


def short_conv(p, conv_t, col0, batch, tc=256):
    t = p.shape[0]
    s = t // batch
    nct = SC_WIDTH // tc
    base = col0 // tc
    col = lambda off: pl.BlockSpec((s, tc), lambda b, c: (b, base + off * nct + c))
    return pl.pallas_call(
        _sconv_body,
        grid=(batch, nct),
        in_specs=[col(0), col(1), col(2), pl.BlockSpec((3, tc), lambda b, c: (0, c))],
        out_specs=pl.BlockSpec((s, tc), lambda b, c: (b, c)),
        out_shape=jax.ShapeDtypeStruct((t, SC_WIDTH), BF16),
        compiler_params=_params("parallel", "parallel"),
        name="short_conv",
    )(p, p, p, conv_t)


def _dsa_body(*refs, seq):
    q_refs = refs[0:3]
    k_refs = refs[3:6]
    v_refs = refs[6:9]
    qn_ref, kn_ref, bias_ref, y_ref, qs_ref, kpad_ref, vpad_ref, og_ref, lse_ref = refs[9:]
    s = seq
    qb = DSA_QBLK
    side = DSA_SIDE
    width = qb + 2 * side
    kj = lax.broadcasted_iota(jnp.int32, (1, width), 1)
    zpad = jnp.zeros((DSA_PAD, DSA_HEAD_DIM), F32)
    for ref in (kpad_ref, vpad_ref):
        ref[0:DSA_PAD, :] = zpad
        ref[DSA_PAD + s:2 * DSA_PAD + s, :] = zpad

    for gi, (_, dil) in enumerate(DSA_PATTERNS):
        sub = s // dil
        nblk = sub // qb
        q = q_refs[gi][...]
        qs_ref[...] = (q * lax.rsqrt(jnp.mean(q * q, axis=-1, keepdims=True) + RMS_EPS) * qn_ref[...]
                       * (DSA_HEAD_DIM ** -0.5))
        k = k_refs[gi][...]
        kpad_ref[DSA_PAD:DSA_PAD + s, :] = k * lax.rsqrt(jnp.mean(k * k, axis=-1, keepdims=True) + RMS_EPS) * kn_ref[...]
        vpad_ref[DSA_PAD:DSA_PAD + s, :] = v_refs[gi][...]
        bias = bias_ref[0, gi]

        def block(t, gi=gi, dil=dil, sub=sub, nblk=nblk, bias=bias):
            r = t // nblk
            n = t % nblk
            rows = pl.ds(r + n * (qb * dil), qb, stride=dil)
            win = pl.ds(DSA_PAD + r + (n * qb - side) * dil, width, stride=dil)
            logits = _dot1(qs_ref[rows, :], kpad_ref[win, :], _NT) + bias
            yield
            pos = n * qb - side + kj
            logits = jnp.where((pos >= 0) & (pos < sub), logits, NEG_INF)
            m = jnp.max(logits, axis=-1, keepdims=True)
            yield
            p = jnp.exp(logits - m)
            ssum = jnp.sum(p, axis=-1, keepdims=True)
            o = _dot1(p, vpad_ref[win, :])
            yield
            og_ref[gi, rows, :] = o / ssum
            lse_ref[gi, rows, :] = jnp.broadcast_to(m + jnp.log(ssum), (qb, DSA_HEAD_DIM))

        nblocks = dil * nblk

        def blocks_step(i, _, block=block):
            _interleave(block(i * DSA_INTERLEAVE + u) for u in range(DSA_INTERLEAVE))
            return 0

        lax.fori_loop(0, nblocks // DSA_INTERLEAVE, blocks_step, 0)

    lse = [lse_ref[gi] for gi in range(DSA_GROUPS)]
    mx = jnp.maximum(jnp.maximum(lse[0], lse[1]), lse[2])
    ws = [jnp.exp(l - mx) for l in lse]
    num = ws[0] * og_ref[0] + ws[1] * og_ref[1] + ws[2] * og_ref[2]
    y_ref[...] = (num / (ws[0] + ws[1] + ws[2])).astype(y_ref.dtype)


def dilated_attention(p, qn, kn, bias_tab, batch):
    t = p.shape[0]
    s = t // batch
    assert (s // DSA_QBLK) % DSA_INTERLEAVE == 0
    hd = DSA_HEAD_DIM
    nh = DSA_HEADS

    def col(part, gi):
        return pl.BlockSpec((s, hd), lambda b, j: (b, part * nh + gi * DSA_HPG + j))

    in_specs = [col(part, gi) for part in range(3) for gi in range(DSA_GROUPS)]
    in_specs += [pl.BlockSpec((1, hd), lambda b, j: (0, 0)), pl.BlockSpec((1, hd), lambda b, j: (0, 0)),
                 pl.BlockSpec((1, DSA_GROUPS, DSA_QBLK, DSA_QBLK + 2 * DSA_SIDE), lambda b, j: (j, 0, 0, 0))]
    big = pltpu.VMEM((s, hd), F32)
    pad = pltpu.VMEM((s + 2 * DSA_PAD, hd), F32)
    grp = pltpu.VMEM((DSA_GROUPS, s, hd), F32)
    return pl.pallas_call(
        functools.partial(_dsa_body, seq=s),
        grid=(batch, DSA_HPG),
        in_specs=in_specs,
        out_specs=pl.BlockSpec((s, hd), lambda b, j: (b, j)),
        out_shape=jax.ShapeDtypeStruct((t, DSA_HPG * hd), BF16),
        scratch_shapes=[big, pad, pad, grp, grp],
        compiler_params=_params("parallel", "parallel"),
        name="dilated_attention",
    )(*([p] * 9), qn.reshape(1, hd), kn.reshape(1, hd), bias_tab)


def _t5_bucket(rel):
    half = REL_BUCKETS // 2
    max_exact = half // 2
    n = np.abs(rel)
    scaled = (np.log(np.maximum(n, max_exact).astype(np.float32) / np.float32(max_exact))
              / np.float32(math.log(REL_MAX_DIST / max_exact)))
    large = np.minimum(max_exact + (scaled * np.float32(half - max_exact)).astype(np.int32), half - 1)
    return np.where(rel > 0, half, 0) + np.where(n < max_exact, n, large)


def _dsa_bias_table(rel_bias):
    width = DSA_QBLK + 2 * DSA_SIDE
    tabs = []
    for gi, (_, dil) in enumerate(DSA_PATTERNS):
        offs = np.arange(-DSA_SIDE, DSA_SIDE + 1, dtype=np.int32) * dil
        band = rel_bias[_t5_bucket(offs)][:, gi * DSA_HPG:(gi + 1) * DSA_HPG].astype(F32)
        fill = jnp.full((DSA_QBLK - 1, DSA_HPG), NEG_INF, F32)
        line = jnp.concatenate([fill, band, fill], axis=0)
        tabs.append(jnp.stack([line[DSA_QBLK - 1 - q:DSA_QBLK - 1 - q + width] for q in range(DSA_QBLK)]))
    return jnp.transpose(jnp.stack(tabs), (3, 0, 1, 2))


def _head_block_diag():
    ii = lax.broadcasted_iota(jnp.int32, (LANES, LANES), 0)
    jj = lax.broadcasted_iota(jnp.int32, (LANES, LANES), 1)
    return (ii // RW_HEAD_DIM) == (jj // RW_HEAD_DIM)


def _rw_prep_body(r_ref, k_ref, v_ref, lo_ref, mur_ref, muk_ref, muv_ref, mulo_ref, w0_ref, w2_ref, a0_ref,
                  a2_ref, g2_ref, kk_ref, ka_ref, rk_ref,
                  ro_ref, vo_ref, kko_ref, bon_ref, gate_ref, lwf_ref, lwb_ref, kdf_ref, kdb_ref, bbf_ref, bbb_ref):
    s = r_ref.shape[0]
    row = lax.broadcasted_iota(jnp.int32, (s, 1), 0)

    def mix(t, mu):
        return t + mu * (0.5 * (_shift_down(t, row) + _shift_up(t, row)) - t)

    r = mix(r_ref[...], mur_ref[...])
    kr = mix(k_ref[...], muk_ref[...])
    v = mix(v_ref[...], muv_ref[...])
    lo = mix(lo_ref[...], mulo_ref[...])
    bd = jnp.where(_head_block_diag(), 1.0, 0.0)

    def head_sum(t):
        return _dot_sum(t, bd)

    kk = kr * kk_ref[...]
    kk = kk * lax.rsqrt(head_sum(kk * kk) + L2_EPS)
    gd = lo[:, 4 * RW_LORA:]
    gate_ref[...] = _dot1(_sigmoid(gd), g2_ref[...])
    ro_ref[...] = r
    vo_ref[...] = v
    kko_ref[...] = kk
    bonus = jnp.zeros_like(r)
    outs = ((lwf_ref, kdf_ref, bbf_ref), (lwb_ref, kdb_ref, bbb_ref))
    for d in range(2):
        wd = lo[:, d * RW_LORA:(d + 1) * RW_LORA]
        ad = lo[:, (2 + d) * RW_LORA:(3 + d) * RW_LORA]
        w_log = -_softplus(-(w0_ref[d:d + 1, :] + _dot1(jnp.tanh(wd), w2_ref[d]))) - 0.5
        a = _sigmoid(a0_ref[d:d + 1, :] + _dot1(ad, a2_ref[d]))
        kd = kr * (1.0 + (a - 1.0) * ka_ref[...])
        lw_ref, kd_ref, bb_ref = outs[d]
        lw_ref[...] = -jnp.exp(w_log)
        kd_ref[...] = kd
        bb_ref[...] = kk * a
        bonus = bonus + head_sum(r * kd * rk_ref[...]) * v
    bon_ref[...] = bonus


def rwkv_prep(p, lora_in, mu, w0, w2, a0, a2, g2, k_k, k_a, r_k, col0, batch):
    t = p.shape[0]
    s = t // batch
    nct = RW_WIDTH // LANES
    base = col0 // LANES
    nlo = RW_LORA_IN
    col = lambda off: pl.BlockSpec((s, LANES), lambda b, c: (b, base + off * nct + c))
    vec = lambda off: pl.BlockSpec((1, LANES), lambda b, c: (0, off * nct + c))
    mu_main = mu[:RW_MAIN].reshape(1, RW_MAIN)
    mu_lo = mu[RW_MAIN:].reshape(1, nlo)
    out_spec = pl.BlockSpec((s, LANES), lambda b, c: (b, c))
    n_out = 11
    return pl.pallas_call(
        _rw_prep_body,
        grid=(batch, nct),
        in_specs=[col(0), col(1), col(2),
                  pl.BlockSpec((s, nlo), lambda b, c: (b, 0)),
                  vec(0), vec(1), vec(2),
                  pl.BlockSpec((1, nlo), lambda b, c: (0, 0)),
                  pl.BlockSpec((2, LANES), lambda b, c: (0, c)),
                  pl.BlockSpec((2, RW_LORA, LANES), lambda b, c: (0, 0, c)),
                  pl.BlockSpec((2, LANES), lambda b, c: (0, c)),
                  pl.BlockSpec((2, RW_LORA, LANES), lambda b, c: (0, 0, c)),
                  pl.BlockSpec((RW_GATE_LORA, LANES), lambda b, c: (0, c)),
                  vec(0), vec(0), vec(0)],
        out_specs=[out_spec] * n_out,
        out_shape=[jax.ShapeDtypeStruct((t, RW_WIDTH), F32)] * n_out,
        compiler_params=_params("parallel", "parallel"),
        name="rwkv_prep",
    )(p, p, p, lora_in, mu_main, mu_main, mu_main, mu_lo, w0, w2, a0, a2, g2,
      k_k.reshape(1, RW_WIDTH), k_a.reshape(1, RW_WIDTH), r_k.reshape(1, RW_WIDTH))


def _rw_scan_body(r_ref, v_ref, kk_ref, bon_ref, gate_ref, lwf_ref, lwb_ref, kdf_ref, kdb_ref, bbf_ref, bbb_ref,
                  lnw_ref, lnb_ref, y_ref, cum_ref, cumt_ref, kdt_ref, bbt_ref, yf_ref, yb_ref,
                  tr_ref, tc_ref, rbk_ref, bkt_ref):
    s = r_ref.shape[0]
    c = CHUNK
    c2 = 2 * c
    row = lax.broadcasted_iota(jnp.int32, (s, 1), 0)
    pos = row % c
    lw_refs, kd_refs, bb_refs = (lwf_ref, lwb_ref), (kdf_ref, kdb_ref), (bbf_ref, bbb_ref)
    for d in range(2):
        cum = _seg_cumsum(lw_refs[d][...], pos, d == 1)
        cum_ref[d] = cum
        cumt_ref[d] = cum.T
        kdt_ref[d] = kd_refs[d][...].T
        bbt_ref[d] = bb_refs[d][...].T

    masks = _tri_masks(c2, c)
    merge = {rev: _merge_masks(c2, rev) for rev in (False, True)}
    eye2_f = jnp.where(masks["eye"], 1.0, 0.0)
    head0 = lax.broadcasted_iota(jnp.int32, (1, LANES), 1) < RW_HEAD_DIM
    head_bd = _head_block_diag()
    npair = s // PAIR
    group = min(RW_PREP_PAIRS, npair)

    def stack(x):
        return jnp.concatenate([jnp.where(head0, x, 0.0), jnp.where(head0, 0.0, x)], axis=0)

    def prep(pidx, d, half):
        reverse = d == 1
        incl2, strict2 = masks[reverse]
        cidx = 2 * pidx + half
        cols = pl.ds(pl.multiple_of(pidx * PAIR, PAIR), PAIR)
        rows = pl.ds(pl.multiple_of(cidx * c, c), c)
        hs = slice(half * c, (half + 1) * c)
        r, v, kk = r_ref[rows, :], v_ref[rows, :], kk_ref[rows, :]
        lw, kd, bb = lw_refs[d][rows, :], kd_refs[d][rows, :], bb_refs[d][rows, :]
        cum = cum_ref[d, rows, :]
        e_neg = jnp.exp(-cum)
        a_s = stack(-kk * jnp.exp(cum - lw))
        r_s = stack(r * jnp.exp(cum))
        ar = jnp.concatenate([a_s, r_s], axis=0)
        bk = jnp.concatenate([stack(bb * e_neg), stack(kd * e_neg)], axis=0)
        g = _dot1(ar, bk, _NT)
        yield
        rbk_ref[d, cidx] = jnp.concatenate([jnp.where(incl2, g[c2:, :c2], 0.0),
                                            jnp.where(incl2, g[c2:, c2:], 0.0)], axis=1).astype(BF16)
        av = _dot1(jnp.where(strict2, g[:c2, c2:], 0.0), stack(v))
        yield
        t_inv = []
        yield from _tri_inverse(-jnp.where(strict2, g[:c2, :c2], 0.0), eye2_f, merge[reverse], t_inv)
        tt = _dot1(t_inv[0], jnp.concatenate([a_s, av], axis=1))
        yield
        tr_ref[d, cidx] = jnp.concatenate([tt[:, :LANES], r_s], axis=0).astype(BF16)
        tc_ref[d, cidx] = tt[:, LANES:].astype(BF16)
        cum_t = cumt_ref[d, :, cols][:, hs]
        last = 0 if reverse else c - 1
        e_out_t = jnp.exp(cum_t[:, last:last + 1] - cum_t)
        bkt_ref[d, cidx] = jnp.concatenate([bbt_ref[d, :, cols][:, hs] * e_out_t,
                                            kdt_ref[d, :, cols][:, hs] * e_out_t], axis=1).astype(BF16)

    def seq(states, pidx, d):
        cum_t2 = cumt_ref[d, :, pl.ds(pl.multiple_of(pidx * PAIR, PAIR), PAIR)]
        for half in ((1, 0) if d else (0, 1)):
            cidx = 2 * pidx + half
            rows = pl.ds(pl.multiple_of(cidx * c, c), c)
            v = v_ref[rows, :]
            x = _dot1(tr_ref[d, cidx], states[d])
            yield
            ps = x[:c2] + tc_ref[d, cidx].astype(F32)
            os_ = x[c2:] + _dot1(rbk_ref[d, cidx], jnp.concatenate([ps, stack(v)], axis=0))
            y = os_[:c] + os_[c:]
            if d:
                yb_ref[rows, :] = y
            else:
                yf_ref[rows, :] = y
            pv = jnp.concatenate([ps[:c] + ps[c:], v], axis=0)
            last = half * c + (0 if d else c - 1)
            e_tot = jnp.exp(cum_t2[:, last:last + 1])
            states[d] = states[d] * e_tot + jnp.where(head_bd, _dot1(bkt_ref[d, cidx], pv), 0.0)
            yield

    ngroups = npair // group

    def prep_chains(g):
        return [prep(npair - 1 - (g * group + j) if d else g * group + j, d, half)
                for j in range(group) for d in range(2) for half in range(2)]

    def seq_chain(states, g, d):
        for j in range(group):
            yield from seq(states, npair - 1 - (g * group + j) if d else g * group + j, d)

    def step(g, carry):
        states = list(carry)
        _interleave(prep_chains(g + 1) + [seq_chain(states, g, 0), seq_chain(states, g, 1)])
        return tuple(states)

    zero = jnp.zeros((LANES, LANES), F32)
    _interleave(prep_chains(0))
    states = list(lax.fori_loop(0, ngroups - 1, step, (zero, zero)))
    _interleave([seq_chain(states, ngroups - 1, 0), seq_chain(states, ngroups - 1, 1)])

    bd = jnp.where(head_bd, 1.0, 0.0)
    y = yf_ref[...] + yb_ref[...]
    mean = _dot_sum(y, bd) * (1.0 / RW_HEAD_DIM)
    yc = y - mean
    var = _dot_sum(yc * yc, bd) * (1.0 / RW_HEAD_DIM)
    yn = yc * lax.rsqrt(var + RW_GN_EPS) * lnw_ref[...] + lnb_ref[...]
    y_ref[...] = ((yn + bon_ref[...]) * gate_ref[...]).astype(y_ref.dtype)


def rwkv_scan(prep, ln_w, ln_b, batch):
    t = prep[0].shape[0]
    s = t // batch
    assert s % PAIR == 0 and (s // PAIR) % min(RW_PREP_PAIRS, s // PAIR) == 0
    nct = RW_WIDTH // LANES
    nchunk = s // CHUNK
    blk = pl.BlockSpec((s, LANES), lambda b, c: (b, c))
    vec = pl.BlockSpec((1, LANES), lambda b, c: (0, c))
    tsp = pltpu.VMEM((2, LANES, s), F32)
    big = pltpu.VMEM((s, LANES), F32)
    per_chunk = lambda rows, cols: pltpu.VMEM((2, nchunk, rows, cols), BF16)
    return pl.pallas_call(
        _rw_scan_body,
        grid=(batch, nct),
        in_specs=[blk] * 11 + [vec, vec],
        out_specs=blk,
        out_shape=jax.ShapeDtypeStruct((t, RW_WIDTH), BF16),
        scratch_shapes=[pltpu.VMEM((2, s, LANES), F32), tsp, tsp, tsp, big, big,
                        per_chunk(2 * PAIR, LANES), per_chunk(PAIR, LANES), per_chunk(PAIR, 2 * LANES),
                        per_chunk(PAIR, LANES)],
        compiler_params=_params("parallel", "parallel"),
        name="rwkv_scan",
    )(*prep, ln_w.reshape(1, RW_WIDTH), ln_b.reshape(1, RW_WIDTH))


def even_mixer(x, g, w_in, w_sc, w_ab, w_out, layer, conv_qkv, a_log, dt_bias, out_gain, conv_sc, batch):
    p, ab = norm_mm(x, g, w_in, layer, ncols=4 * DN_WIDTH, w2=w_sc, wp_t=w_ab.T)
    zeros = jnp.zeros((2 * DN_HEADS,), F32)
    alog32 = jnp.concatenate([zeros, a_log.reshape(-1)]).reshape(1, -1)
    dt32 = jnp.concatenate([zeros, dt_bias.reshape(-1)]).reshape(1, -1)
    y_dn = deltanet(p, ab, conv_qkv.T, alog32, dt32, out_gain, batch)
    y_sc = short_conv(p, conv_sc.T, 4 * DN_WIDTH, batch)
    return mm_res(y_dn, w_out, layer, x, a2=y_sc)


def odd_mixer(x, g, w_in, w_lora, w_out, layer, qn, kn, bias_tab, mu, w0, w2, a0, a2, g2, k_k, k_a, r_k, ln_w, ln_b,
              batch):
    p, lora_in = norm_mm(x, g, w_in, layer, ncols=3 * DSA_QKV + RW_MAIN, wp=w_lora)
    y_c = dilated_attention(p, qn, kn, bias_tab, batch)
    prep = rwkv_prep(p, lora_in, mu, w0, w2, a0, a2, g2, k_k, k_a, r_k, 3 * DSA_QKV, batch)
    y_d = rwkv_scan(prep, ln_w, ln_b, batch)
    return mm_res(y_c, w_out, layer, x, a2=y_d)


def kernel(x, mem, rel_bias, norm_mix, norm_xattn, norm_mem, norm_ffn, xa_wq, xa_wk, xa_wv, xa_wo, xa_qn, xa_kn, ffn_w1, ffn_w2, ev_w_in, ev_w_out, dn_conv, dn_a_log, dn_dt_bias, dn_norm, sc_conv, od_w_in, od_w_out, ca_qn, ca_kn, rw_mu, rw_w0, rw_w2, rw_a0, rw_a2, rw_g2, rw_k_k, rw_k_a, rw_r_k, rw_ln_w, rw_ln_b):
    batch, seq, d = x.shape
    n_mem = mem.shape[1]
    xf = x.reshape(batch * seq, d)
    memf = mem.reshape(batch * n_mem, d)
    bias_tab = _dsa_bias_table(rel_bias)
    c_ab = 4 * DN_WIDTH
    c_sc = c_ab + 4 * DN_HEADS
    c_lo = 3 * DSA_QKV + RW_MAIN
    ev_in, od_in = ev_w_in.astype(BF16), od_w_in.astype(BF16)
    ev_sc = ev_in[:, :, c_sc:]
    ev_out, od_out = ev_w_out.astype(BF16), od_w_out.astype(BF16)
    w_q, w_o = xa_wq.astype(BF16), xa_wo.astype(BF16)
    w_kv = jnp.concatenate([xa_wk, xa_wv], axis=2).astype(BF16)
    for layer in range(DEPTH):
        i = layer // 2
        if layer % 2 == 0:
            xf = even_mixer(xf, norm_mix[layer], ev_in, ev_sc, ev_w_in[i, :, c_ab:c_sc], ev_out, i, dn_conv[i],
                            dn_a_log[i], dn_dt_bias[i], dn_norm[i], sc_conv[i], batch)
        else:
            xf = odd_mixer(xf, norm_mix[layer], od_in, od_w_in[i, :, c_lo:], od_out, i, ca_qn[i], ca_kn[i], bias_tab,
                           rw_mu[i], rw_w0[i], rw_w2[i], rw_a0[i], rw_a2[i], rw_g2[i], rw_k_k[i], rw_k_a[i], rw_r_k[i],
                           rw_ln_w[i], rw_ln_b[i], batch)
        kv = norm_mm(memf, norm_mem[layer], w_kv, layer).reshape(batch, n_mem, 2 * XA_WIDTH)
        xf = xattn(xf, norm_xattn[layer], w_q, kv, w_o, layer, xa_qn[layer], xa_kn[layer], batch)
        h1 = norm_mm(xf, norm_ffn[layer], ffn_w1, layer, act="relu2", out_dtype=BF16)
        xf = mm_res(h1, ffn_w2, layer, xf)
    return xf.reshape(batch, seq, d)
```

```python
import functools
import math

import jax
import jax.numpy as jnp
import numpy as np
from jax import lax
from jax.experimental import pallas as pl
from jax.experimental.pallas import tpu as pltpu

F32 = jnp.float32
BF16 = jnp.bfloat16

D_MODEL = 2048
DEPTH = 4
RMS_EPS = 1e-6
L2_EPS = 1e-6

DN_HEADS = 8
DN_HEAD_DIM = 128
DN_WIDTH = DN_HEADS * DN_HEAD_DIM
SC_WIDTH = D_MODEL - DN_WIDTH
CHUNK = 64
PAIR = 2 * CHUNK
DN_PREP_PAIRS = 4
DN_HEADS_PER_STEP = 2
RW_PREP_PAIRS = 2

DSA_PATTERNS = ((128, 1), (512, 4), (2048, 16))
DSA_GROUPS = len(DSA_PATTERNS)
DSA_HPG = 4
DSA_HEAD_DIM = 128
DSA_HEADS = DSA_GROUPS * DSA_HPG
DSA_QKV = DSA_HEADS * DSA_HEAD_DIM
DSA_SIDE = 64
DSA_QBLK = 128
DSA_PAD = DSA_SIDE * max(d for _, d in DSA_PATTERNS)
DSA_INTERLEAVE = 8
REL_BUCKETS = 32
REL_MAX_DIST = 1024
NEG_INF = -1e30

RW_HEADS = 8
RW_HEAD_DIM = 64
RW_WIDTH = RW_HEADS * RW_HEAD_DIM
RW_LORA = 64
RW_GATE_LORA = 128
RW_MAIN = 3 * RW_WIDTH
RW_LORA_IN = 4 * RW_LORA + RW_GATE_LORA
RW_GN_EPS = 64e-5

XA_HEADS = 4
XA_HEAD_DIM = 128
XA_WIDTH = XA_HEADS * XA_HEAD_DIM

LANES = 128
NORM_ROWS = 256
VMEM_LIMIT_BYTES = 58 * 1024 * 1024


def _params(*sem):
    return pltpu.CompilerParams(dimension_semantics=sem, vmem_limit_bytes=VMEM_LIMIT_BYTES)


_NN = (((1,), (0,)), ((), ()))
_NT = (((1,), (1,)), ((), ()))


def _dot1(a, b, dims=_NN):
    return lax.dot_general(a.astype(BF16), b.astype(BF16), dims, preferred_element_type=F32)


def _dot_sum(a, b01):
    b = b01.astype(BF16)
    hi = a.astype(BF16)
    lo = (a - hi.astype(F32)).astype(BF16)
    dg = functools.partial(lax.dot_general, dimension_numbers=_NN, preferred_element_type=F32)
    return dg(hi, b) + dg(lo, b)


def _interleave(chains):
    chains = list(chains)
    while chains:
        alive = []
        for ch in chains:
            try:
                next(ch)
                alive.append(ch)
            except StopIteration:
                pass
        chains = alive


def _merge_masks(n, reverse):
    ii = lax.broadcasted_iota(jnp.int32, (n, n), 0)
    jj = lax.broadcasted_iota(jnp.int32, (n, n), 1)
    tri = (ii < jj) if reverse else (ii > jj)
    out = []
    size = 1
    while size < CHUNK:
        out.append(tri & ((ii // (2 * size)) == (jj // (2 * size))) & ((ii // size) != (jj // size)))
        size *= 2
    return out


def _tri_inverse(a, eye_f, level_masks, out):
    t = eye_f - jnp.where(level_masks[0], a, 0.0)
    for m in level_masks[1:]:
        x = _dot1(jnp.where(m, a, 0.0), t)
        yield
        t = t - _dot1(t, x)
        yield
    out.append(t)


def _sigmoid(x):
    return 1.0 / (1.0 + jnp.exp(-x))


def _softplus(x):
    return jnp.maximum(x, 0.0) + jnp.log1p(jnp.exp(-jnp.abs(x)))


def _shift_down(x, row):
    return jnp.where(row == 0, 0.0, pltpu.roll(x, 1, 0))


def _shift_up(x, row):
    n = x.shape[0]
    return jnp.where(row == n - 1, 0.0, pltpu.roll(x, n - 1, 0))


def _norm_mm_body(x_ref, g_ref, w_ref, *rest, act, precise, precise_t, n_first):
    if n_first is not None:
        w2_ref, rest = rest[0], rest[1:]
    if precise:
        wp_ref, o_ref, op_ref, xn_ref = rest
    else:
        o_ref, xn_ref = rest

    @pl.when(pl.program_id(1) == 0)
    def _():
        def rows_step(r, _):
            rows = pl.ds(pl.multiple_of(r * NORM_ROWS, NORM_ROWS), NORM_ROWS)
            x = x_ref[rows, :]
            xn = x * lax.rsqrt(jnp.mean(x * x, axis=-1, keepdims=True) + RMS_EPS) * g_ref[...]
            xn_ref[rows, :] = xn.astype(BF16)
            if precise and precise_t:
                op_ref[:, rows] = _dot1(wp_ref[...], xn, _NT)
            elif precise:
                op_ref[rows, :] = _dot1(xn, wp_ref[...])
            return 0

        lax.fori_loop(0, x_ref.shape[0] // NORM_ROWS, rows_step, 0)

    def project(wt_ref):
        acc = jnp.dot(xn_ref[...], wt_ref[...].astype(BF16), preferred_element_type=F32)
        if act == "relu2":
            acc = jnp.square(jnp.maximum(acc, 0.0))
        o_ref[...] = acc.astype(o_ref.dtype)

    if n_first is None:
        project(w_ref)
    else:
        pl.when(pl.program_id(1) < n_first)(lambda: project(w_ref))
        pl.when(pl.program_id(1) >= n_first)(lambda: project(w2_ref))


def norm_mm(x, g, w, layer, ncols=None, w2=None, wp=None, wp_t=None, act=None, out_dtype=F32, tm=1024, tn=1024):
    t, d = x.shape
    n1 = w.shape[2] if ncols is None else ncols
    n = n1 + (0 if w2 is None else w2.shape[2])
    tm = min(tm, t)
    tn = min(tn, n1)
    assert t % tm == 0 and n1 % tn == 0 and n % tn == 0 and tm % NORM_ROWS == 0
    precise = wp is not None or wp_t is not None
    n_first = None if w2 is None else n1 // tn
    in_specs = [
        pl.BlockSpec((tm, d), lambda i, j: (i, 0)),
        pl.BlockSpec((1, d), lambda i, j: (0, 0)),
    ]
    args = [x, g.reshape(1, d), w]
    if w2 is None:
        in_specs.append(pl.BlockSpec((None, d, tn), lambda i, j: (layer, 0, j)))
    else:
        in_specs.append(pl.BlockSpec((None, d, tn), lambda i, j: (layer, 0, jnp.minimum(j, n_first - 1))))
        in_specs.append(pl.BlockSpec((None, d, tn), lambda i, j: (layer, 0, jnp.maximum(j - n_first, 0))))
        args.append(w2)
    out_specs = pl.BlockSpec((tm, tn), lambda i, j: (i, j))
    out_shape = jax.ShapeDtypeStruct((t, n), out_dtype)
    if wp is not None:
        npc = wp.shape[1]
        in_specs.append(pl.BlockSpec((d, npc), lambda i, j: (0, 0)))
        out_specs = [out_specs, pl.BlockSpec((tm, npc), lambda i, j: (i, 0))]
        out_shape = [out_shape, jax.ShapeDtypeStruct((t, npc), F32)]
        args.append(wp)
    elif wp_t is not None:
        npc = wp_t.shape[0]
        in_specs.append(pl.BlockSpec((npc, d), lambda i, j: (0, 0)))
        out_specs = [out_specs, pl.BlockSpec((npc, tm), lambda i, j: (0, i))]
        out_shape = [out_shape, jax.ShapeDtypeStruct((npc, t), F32)]
        args.append(wp_t)
    return pl.pallas_call(
        functools.partial(_norm_mm_body, act=act, precise=precise, precise_t=wp_t is not None, n_first=n_first),
        grid=(t // tm, n // tn),
        in_specs=in_specs,
        out_specs=out_specs,
        out_shape=out_shape,
        scratch_shapes=[pltpu.VMEM((tm, d), BF16)],
        compiler_params=_params("parallel", "arbitrary"),
        name="norm_mm",
    )(*args)


def _mm_res_body(a_ref, *rest, n_first):
    if n_first is not None:
        a2_ref, rest = rest[0], rest[1:]
    w_ref, r_ref, o_ref = rest
    kk = pl.program_id(2)

    def accumulate(lhs_ref):
        acc = jnp.dot(lhs_ref[...], w_ref[...].astype(BF16), preferred_element_type=F32)

        @pl.when(kk == 0)
        def _():
            o_ref[...] = r_ref[...] + acc

        @pl.when(kk > 0)
        def _():
            o_ref[...] += acc

    if n_first is None:
        accumulate(a_ref)
    else:
        pl.when(kk < n_first)(lambda: accumulate(a_ref))
        pl.when(kk >= n_first)(lambda: accumulate(a2_ref))


def mm_res(a, w, layer, res, a2=None, tm=1024, tn=1024, tk=2048):
    t, k1 = a.shape
    k = k1 + (0 if a2 is None else a2.shape[1])
    n = w.shape[2]
    tm, tn, tk = min(tm, t), min(tn, n), min(tk, k1)
    assert t % tm == 0 and n % tn == 0 and k1 % tk == 0 and k % tk == 0
    n_first = None if a2 is None else k1 // tk
    if a2 is None:
        in_specs = [pl.BlockSpec((tm, tk), lambda i, j, kk: (i, kk))]
        args = [a]
    else:
        in_specs = [pl.BlockSpec((tm, tk), lambda i, j, kk: (i, jnp.minimum(kk, n_first - 1))),
                    pl.BlockSpec((tm, tk), lambda i, j, kk: (i, jnp.maximum(kk - n_first, 0)))]
        args = [a, a2]
    in_specs += [
        pl.BlockSpec((None, tk, tn), lambda i, j, kk: (layer, kk, j)),
        pl.BlockSpec((tm, tn), lambda i, j, kk: (i, j)),
    ]
    return pl.pallas_call(
        functools.partial(_mm_res_body, n_first=n_first),
        grid=(t // tm, n // tn, k // tk),
        in_specs=in_specs,
        out_specs=pl.BlockSpec((tm, tn), lambda i, j, kk: (i, j)),
        out_shape=jax.ShapeDtypeStruct((t, n), F32),
        compiler_params=_params("parallel", "parallel", "arbitrary"),
        name="mm_res",
    )(*args, w, res)


def _xattn_body(x_ref, g_ref, wq_ref, kv_ref, wo_ref, qn_ref, kn_ref, o_ref):
    x = x_ref[...]
    xn = x * lax.rsqrt(jnp.mean(x * x, axis=-1, keepdims=True) + RMS_EPS) * g_ref[...]
    q = jnp.dot(xn.astype(BF16), wq_ref[...], preferred_element_type=F32)
    kv = kv_ref[0]
    outs = []
    for h in range(XA_HEADS):
        sl = slice(h * XA_HEAD_DIM, (h + 1) * XA_HEAD_DIM)
        qh = q[:, sl]
        qh = qh * lax.rsqrt(jnp.mean(qh * qh, axis=-1, keepdims=True) + RMS_EPS) * qn_ref[...]
        kh = kv[:, sl]
        kh = kh * lax.rsqrt(jnp.mean(kh * kh, axis=-1, keepdims=True) + RMS_EPS) * kn_ref[...]
        vh = kv[:, XA_WIDTH + h * XA_HEAD_DIM:XA_WIDTH + (h + 1) * XA_HEAD_DIM]
        logits = _dot1(qh, kh, _NT) * (XA_HEAD_DIM ** -0.5)
        m = jnp.max(logits, axis=-1, keepdims=True)
        p = jnp.exp(logits - m)
        s = jnp.sum(p, axis=-1, keepdims=True)
        outs.append(_dot1(p, vh) / s)
    o = jnp.concatenate(outs, axis=-1).astype(BF16)
    o_ref[...] = x + jnp.dot(o, wo_ref[...], preferred_element_type=F32)


def xattn(x, g, wq, kv, wo, layer, qn, kn, batch, ts=512):
    t, d = x.shape
    s = t // batch
    ts = min(ts, s)
    nst = s // ts
    m = kv.shape[1]
    return pl.pallas_call(
        _xattn_body,
        grid=(batch, nst),
        in_specs=[
            pl.BlockSpec((ts, d), lambda b, i: (b * nst + i, 0)),
            pl.BlockSpec((1, d), lambda b, i: (0, 0)),
            pl.BlockSpec((None, d, XA_WIDTH), lambda b, i: (layer, 0, 0)),
            pl.BlockSpec((1, m, 2 * XA_WIDTH), lambda b, i: (b, 0, 0)),
            pl.BlockSpec((None, XA_WIDTH, d), lambda b, i: (layer, 0, 0)),
            pl.BlockSpec((1, XA_HEAD_DIM), lambda b, i: (0, 0)),
            pl.BlockSpec((1, XA_HEAD_DIM), lambda b, i: (0, 0)),
        ],
        out_specs=pl.BlockSpec((ts, d), lambda b, i: (b * nst + i, 0)),
        out_shape=jax.ShapeDtypeStruct((t, d), F32),
        compiler_params=_params("parallel", "parallel"),
        name="xattn",
    )(x, g.reshape(1, d), wq, kv, wo, qn.reshape(1, -1), kn.reshape(1, -1))


def _tri_masks(n, blk):
    ii = lax.broadcasted_iota(jnp.int32, (n, n), 0)
    jj = lax.broadcasted_iota(jnp.int32, (n, n), 1)
    same = (ii // blk) == (jj // blk) if n != blk else None

    def m(c):
        return c if same is None else (c & same)

    return {
        False: (m(ii >= jj), m(ii > jj)),
        True: (m(ii <= jj), m(ii < jj)),
        "eye": ii == jj,
    }


def _seg_cumsum(x, pos, reverse):
    n = x.shape[0]
    sh = 1
    while sh < CHUNK:
        if reverse:
            x = x + jnp.where(pos < CHUNK - sh, pltpu.roll(x, n - sh, 0), 0.0)
        else:
            x = x + jnp.where(pos >= sh, pltpu.roll(x, sh, 0), 0.0)
        sh *= 2
    return x


def _dn_body(q_ref, k_ref, v_ref, gate_ref, ab_ref, cq_ref, ck_ref, cv_ref, alog_ref, dt_ref, gain_ref,
             y_ref, qs_ref, ks_ref, vs_ref, kt_ref, bg_ref, gt_ref, of_ref, ob_ref,
             u_ref, w_ref, qg_ref, qk_ref, kdt_ref, eg_ref, st_ref):
    s = q_ref.shape[0]
    hd = DN_HEAD_DIM
    nh = DN_HEADS_PER_STEP
    row = lax.broadcasted_iota(jnp.int32, (s, 1), 0)
    c = CHUNK
    lane32 = lax.broadcasted_iota(jnp.int32, (1, 4 * DN_HEADS), 1)
    lane128 = lax.broadcasted_iota(jnp.int32, (1, LANES), 1)
    pos_t = lax.broadcasted_iota(jnp.int32, (1, s), 1) % c

    def conv_silu(x_ref, cw_ref, lanes):
        x = x_ref[:, lanes]
        w = cw_ref[:, lanes]
        y = _shift_down(x, row) * w[0:1] + x * w[1:2] + _shift_up(x, row) * w[2:3]
        return y * _sigmoid(y)

    for hh in range(nh):
        h = pl.program_id(1) * nh + hh
        lanes = slice(hh * hd, (hh + 1) * hd)
        q = conv_silu(q_ref, cq_ref, lanes)
        q = q * lax.rsqrt(jnp.sum(q * q, axis=-1, keepdims=True) + L2_EPS) * (hd ** -0.5)
        qs_ref[hh] = q
        k = conv_silu(k_ref, ck_ref, lanes)
        k = k * lax.rsqrt(jnp.sum(k * k, axis=-1, keepdims=True) + L2_EPS)
        ks_ref[hh] = k
        kt_ref[hh] = k.T
        vs_ref[hh] = conv_silu(v_ref, cv_ref, lanes)

        def decay_row(d, h=h):
            idx = (2 + d) * DN_HEADS + h
            a_log = jnp.sum(jnp.where(lane32 == idx, alog_ref[...], 0.0), axis=-1, keepdims=True)
            dt = jnp.sum(jnp.where(lane32 == idx, dt_ref[...], 0.0), axis=-1, keepdims=True)
            g = -jnp.exp(a_log) * _softplus(ab_ref[pl.ds(idx, 1), :] + dt)
            sh = 1
            while sh < c:
                if d:
                    g = g + jnp.where(pos_t < c - sh, pltpu.roll(g, s - sh, 1), 0.0)
                else:
                    g = g + jnp.where(pos_t >= sh, pltpu.roll(g, sh, 1), 0.0)
                sh *= 2
            return g

        rows = [_sigmoid(ab_ref[pl.ds(h, 1), :]), _sigmoid(ab_ref[pl.ds(DN_HEADS + h, 1), :]),
                decay_row(0), decay_row(1)]
        gt = jnp.concatenate(rows + [jnp.zeros((LANES - len(rows), s), F32)], axis=0)
        gt_ref[hh] = gt[0:8, :]
        bg_ref[hh] = gt.T
        for d in range(2):
            st_ref[hh, d] = jnp.zeros((hd, hd), F32)

    masks = _tri_masks(PAIR, c)
    merge = {rev: _merge_masks(PAIR, rev) for rev in (False, True)}
    eye_f = jnp.where(masks["eye"], 1.0, 0.0)
    npair = s // PAIR
    group = min(DN_PREP_PAIRS, npair)

    def prep(hh, pidx, d):
        reverse = d == 1
        rows = pl.ds(pl.multiple_of(pidx * PAIR, PAIR), PAIR)
        qp, kp, vp = qs_ref[hh, rows, :], ks_ref[hh, rows, :], vs_ref[hh, rows, :]
        bgp = bg_ref[hh, rows, :]
        beta = bgp[:, d:d + 1]
        gcol = bgp[:, 2 + d:3 + d]
        grow = gt_ref[hh, 2 + d:3 + d, rows]
        incl, strict = masks[reverse]
        decay = jnp.where(incl, jnp.exp(jnp.where(incl, gcol - grow, 0.0)), 0.0)
        kb = kp * beta
        kq = _dot1(jnp.concatenate([kb, qp], axis=0), kp, _NT)
        yield
        qk_ref[hh, d, rows, :] = jnp.where(incl, kq[PAIR:] * decay, 0.0).astype(BF16)
        t_inv = []
        yield from _tri_inverse(jnp.where(strict, kq[:PAIR] * decay, 0.0), eye_f, merge[reverse], t_inv)
        egc = jnp.exp(gcol)
        uw = _dot1(t_inv[0], jnp.concatenate([vp * beta, kb * egc], axis=1))
        yield
        u_ref[hh, d, rows, :] = uw[:, :hd]
        w_ref[hh, d, rows, :] = uw[:, hd:].astype(BF16)
        qg_ref[hh, d, rows, :] = (qp * egc).astype(BF16)
        last0, last1 = (0, c) if reverse else (c - 1, PAIR - 1)
        glast = jnp.where(lane128 < c, grow[:, last0:last0 + 1], grow[:, last1:last1 + 1])
        kdt_ref[hh, d, :, rows] = (kt_ref[hh, :, rows] * jnp.exp(glast - grow)).astype(BF16)
        eg_ref[hh, d, :, rows] = jnp.broadcast_to(jnp.exp(glast), (8, PAIR))

    def seq(hh, pidx, d):
        pair_rows = pl.ds(pl.multiple_of(pidx * PAIR, PAIR), PAIR)
        kdt = kdt_ref[hh, d, :, pair_rows]
        eg = eg_ref[hh, d, 0:1, pair_rows]
        for half in ((1, 0) if d else (0, 1)):
            r0 = half * c
            rows = pl.ds(pl.multiple_of(pidx * PAIR + r0, c), c)
            state = st_ref[hh, d]
            ws = _dot1(jnp.concatenate([w_ref[hh, d, rows, :], qg_ref[hh, d, rows, :]], axis=0), state)
            yield
            v_new = u_ref[hh, d, rows, :] - ws[:c]
            o = ws[c:] + _dot1(qk_ref[hh, d, rows, r0:r0 + c], v_new)
            if d:
                ob_ref[hh, rows, :] = o
            else:
                of_ref[hh, rows, :] = o
            st_ref[hh, d] = state * eg[:, r0:r0 + 1] + _dot1(kdt[:, r0:r0 + c], v_new)
            yield

    ngroups = npair // group

    def pair_of(g, j, d):
        return npair - 1 - (g * group + j) if d else g * group + j

    def prep_chains(g):
        return [prep(hh, pair_of(g, j, d), d) for j in range(group) for hh in range(nh) for d in range(2)]

    def seq_chain(g, hh, d):
        for j in range(group):
            yield from seq(hh, pair_of(g, j, d), d)

    def seq_chains(g):
        return [seq_chain(g, hh, d) for hh in range(nh) for d in range(2)]

    def step(g, _):
        _interleave(prep_chains(g + 1) + seq_chains(g))
        return 0

    _interleave(prep_chains(0))
    lax.fori_loop(0, ngroups - 1, step, 0)
    _interleave(seq_chains(ngroups - 1))

    for hh in range(nh):
        lanes = slice(hh * hd, (hh + 1) * hd)
        o = of_ref[hh] + ob_ref[hh]
        o = o * lax.rsqrt(jnp.mean(o * o, axis=-1, keepdims=True) + RMS_EPS) * gain_ref[...]
        gate = gate_ref[:, lanes]
        y_ref[:, lanes] = (o * (gate * _sigmoid(gate))).astype(y_ref.dtype)


def deltanet(p, ab, conv_t, alog32, dt32, gain, batch):
    t = p.shape[0]
    s = t // batch
    assert s % PAIR == 0 and (s // PAIR) % min(DN_PREP_PAIRS, s // PAIR) == 0
    hd = DN_HEAD_DIM
    nh = DN_HEADS_PER_STEP
    nsteps = DN_HEADS // nh
    col = lambda off: pl.BlockSpec((s, nh * hd), lambda b, h: (b, off * nsteps + h))
    cw = lambda off: pl.BlockSpec((3, nh * hd), lambda b, h: (0, off * nsteps + h))
    gate_col = pl.BlockSpec((s, nh * hd), lambda b, h: (b, 3 * nsteps + h), pipeline_mode=pl.Buffered(1))
    small = lambda n: pl.BlockSpec((1, n), lambda b, h: (0, 0))
    tok = pltpu.VMEM((nh, s, hd), F32)
    per_dir = pltpu.VMEM((nh, 2, s, hd), BF16)
    return pl.pallas_call(
        _dn_body,
        grid=(batch, nsteps),
        in_specs=[col(0), col(1), col(2), gate_col,
                  pl.BlockSpec((4 * DN_HEADS, s), lambda b, h: (0, b)),
                  cw(0), cw(1), cw(2),
                  small(4 * DN_HEADS), small(4 * DN_HEADS), small(hd)],
        out_specs=pl.BlockSpec((s, nh * hd), lambda b, h: (b, h)),
        out_shape=jax.ShapeDtypeStruct((t, DN_WIDTH), BF16),
        scratch_shapes=[tok, tok, tok, pltpu.VMEM((nh, hd, s), F32), tok, pltpu.VMEM((nh, 8, s), F32), tok, tok,
                        pltpu.VMEM((nh, 2, s, hd), F32), per_dir, per_dir, per_dir,
                        pltpu.VMEM((nh, 2, hd, s), BF16), pltpu.VMEM((nh, 2, 8, s), F32),
                        pltpu.VMEM((nh, 2, hd, hd), F32)],
        compiler_params=_params("parallel", "parallel"),
        name="deltanet",
    )(p, p, p, p, ab, conv_t, conv_t, conv_t, alog32, dt32, gain.reshape(1, hd))


def _sconv_body(b_ref, c_ref, u_ref, w_ref, y_ref):
    s = b_ref.shape[0]
    row = lax.broadcasted_iota(jnp.int32, (s, 1), 0)
    cu = c_ref[...] * u_ref[...]
    w = w_ref[...]
    y = _shift_down(cu, row) * w[0:1] + cu * w[1:2] + _shift_up(cu, row) * w[2:3]
    y_ref[...] = (b_ref[...] * y).astype(y_ref.dtype)


def short_conv(p, conv_t, col0, batch, tc=256):
    t = p.shape[0]
    s = t // batch
    nct = SC_WIDTH // tc
    base = col0 // tc
    col = lambda off: pl.BlockSpec((s, tc), lambda b, c: (b, base + off * nct + c))
    return pl.pallas_call(
        _sconv_body,
        grid=(batch, nct),
        in_specs=[col(0), col(1), col(2), pl.BlockSpec((3, tc), lambda b, c: (0, c))],
        out_specs=pl.BlockSpec((s, tc), lambda b, c: (b, c)),
        out_shape=jax.ShapeDtypeStruct((t, SC_WIDTH), BF16),
        compiler_params=_params("parallel", "parallel"),
        name="short_conv",
    )(p, p, p, conv_t)


def _dsa_body(*refs, seq):
    q_refs = refs[0:3]
    k_refs = refs[3:6]
    v_refs = refs[6:9]
    qn_ref, kn_ref, bias_ref, y_ref, qs_ref, kpad_ref, vpad_ref, og_ref, lse_ref = refs[9:]
    s = seq
    qb = DSA_QBLK
    side = DSA_SIDE
    width = qb + 2 * side
    kj = lax.broadcasted_iota(jnp.int32, (1, width), 1)
    zpad = jnp.zeros((DSA_PAD, DSA_HEAD_DIM), F32)
    for ref in (kpad_ref, vpad_ref):
        ref[0:DSA_PAD, :] = zpad
        ref[DSA_PAD + s:2 * DSA_PAD + s, :] = zpad

    for gi, (_, dil) in enumerate(DSA_PATTERNS):
        sub = s // dil
        nblk = sub // qb
        q = q_refs[gi][...]
        qs_ref[...] = (q * lax.rsqrt(jnp.mean(q * q, axis=-1, keepdims=True) + RMS_EPS) * qn_ref[...]
                       * (DSA_HEAD_DIM ** -0.5))
        k = k_refs[gi][...]
        kpad_ref[DSA_PAD:DSA_PAD + s, :] = k * lax.rsqrt(jnp.mean(k * k, axis=-1, keepdims=True) + RMS_EPS) * kn_ref[...]
        vpad_ref[DSA_PAD:DSA_PAD + s, :] = v_refs[gi][...]
        bias = bias_ref[0, gi]

        def block(t, gi=gi, dil=dil, sub=sub, nblk=nblk, bias=bias):
            r = t // nblk
            n = t % nblk
            rows = pl.ds(r + n * (qb * dil), qb, stride=dil)
            win = pl.ds(DSA_PAD + r + (n * qb - side) * dil, width, stride=dil)
            logits = _dot1(qs_ref[rows, :], kpad_ref[win, :], _NT) + bias
            yield
            pos = n * qb - side + kj
            logits = jnp.where((pos >= 0) & (pos < sub), logits, NEG_INF)
            m = jnp.max(logits, axis=-1, keepdims=True)
            yield
            p = jnp.exp(logits - m)
            ssum = jnp.sum(p, axis=-1, keepdims=True)
            o = _dot1(p, vpad_ref[win, :])
            yield
            og_ref[gi, rows, :] = o / ssum
            lse_ref[gi, rows, :] = jnp.broadcast_to(m + jnp.log(ssum), (qb, DSA_HEAD_DIM))

        nblocks = dil * nblk

        def blocks_step(i, _, block=block):
            _interleave(block(i * DSA_INTERLEAVE + u) for u in range(DSA_INTERLEAVE))
            return 0

        lax.fori_loop(0, nblocks // DSA_INTERLEAVE, blocks_step, 0)

    lse = [lse_ref[gi] for gi in range(DSA_GROUPS)]
    mx = jnp.maximum(jnp.maximum(lse[0], lse[1]), lse[2])
    ws = [jnp.exp(l - mx) for l in lse]
    num = ws[0] * og_ref[0] + ws[1] * og_ref[1] + ws[2] * og_ref[2]
    y_ref[...] = (num / (ws[0] + ws[1] + ws[2])).astype(y_ref.dtype)


def dilated_attention(p, qn, kn, bias_tab, batch):
    t = p.shape[0]
    s = t // batch
    assert (s // DSA_QBLK) % DSA_INTERLEAVE == 0
    hd = DSA_HEAD_DIM
    nh = DSA_HEADS

    def col(part, gi):
        return pl.BlockSpec((s, hd), lambda b, j: (b, part * nh + gi * DSA_HPG + j))

    in_specs = [col(part, gi) for part in range(3) for gi in range(DSA_GROUPS)]
    in_specs += [pl.BlockSpec((1, hd), lambda b, j: (0, 0)), pl.BlockSpec((1, hd), lambda b, j: (0, 0)),
                 pl.BlockSpec((1, DSA_GROUPS, DSA_QBLK, DSA_QBLK + 2 * DSA_SIDE), lambda b, j: (j, 0, 0, 0))]
    big = pltpu.VMEM((s, hd), F32)
    pad = pltpu.VMEM((s + 2 * DSA_PAD, hd), F32)
    grp = pltpu.VMEM((DSA_GROUPS, s, hd), F32)
    return pl.pallas_call(
        functools.partial(_dsa_body, seq=s),
        grid=(batch, DSA_HPG),
        in_specs=in_specs,
        out_specs=pl.BlockSpec((s, hd), lambda b, j: (b, j)),
        out_shape=jax.ShapeDtypeStruct((t, DSA_HPG * hd), BF16),
        scratch_shapes=[big, pad, pad, grp, grp],
        compiler_params=_params("parallel", "parallel"),
        name="dilated_attention",
    )(*([p] * 9), qn.reshape(1, hd), kn.reshape(1, hd), bias_tab)


def _t5_bucket(rel):
    half = REL_BUCKETS // 2
    max_exact = half // 2
    n = np.abs(rel)
    scaled = (np.log(np.maximum(n, max_exact).astype(np.float32) / np.float32(max_exact))
              / np.float32(math.log(REL_MAX_DIST / max_exact)))
    large = np.minimum(max_exact + (scaled * np.float32(half - max_exact)).astype(np.int32), half - 1)
    return np.where(rel > 0, half, 0) + np.where(n < max_exact, n, large)


def _dsa_bias_table(rel_bias):
    width = DSA_QBLK + 2 * DSA_SIDE
    tabs = []
    for gi, (_, dil) in enumerate(DSA_PATTERNS):
        offs = np.arange(-DSA_SIDE, DSA_SIDE + 1, dtype=np.int32) * dil
        band = rel_bias[_t5_bucket(offs)][:, gi * DSA_HPG:(gi + 1) * DSA_HPG].astype(F32)
        fill = jnp.full((DSA_QBLK - 1, DSA_HPG), NEG_INF, F32)
        line = jnp.concatenate([fill, band, fill], axis=0)
        tabs.append(jnp.stack([line[DSA_QBLK - 1 - q:DSA_QBLK - 1 - q + width] for q in range(DSA_QBLK)]))
    return jnp.transpose(jnp.stack(tabs), (3, 0, 1, 2))


def _head_block_diag():
    ii = lax.broadcasted_iota(jnp.int32, (LANES, LANES), 0)
    jj = lax.broadcasted_iota(jnp.int32, (LANES, LANES), 1)
    return (ii // RW_HEAD_DIM) == (jj // RW_HEAD_DIM)


def _rw_prep_body(r_ref, k_ref, v_ref, lo_ref, mur_ref, muk_ref, muv_ref, mulo_ref, w0_ref, w2_ref, a0_ref,
                  a2_ref, g2_ref, kk_ref, ka_ref, rk_ref,
                  ro_ref, vo_ref, kko_ref, bon_ref, gate_ref, lwf_ref, lwb_ref, kdf_ref, kdb_ref, bbf_ref, bbb_ref):
    s = r_ref.shape[0]
    row = lax.broadcasted_iota(jnp.int32, (s, 1), 0)

    def mix(t, mu):
        return t + mu * (0.5 * (_shift_down(t, row) + _shift_up(t, row)) - t)

    r = mix(r_ref[...], mur_ref[...])
    kr = mix(k_ref[...], muk_ref[...])
    v = mix(v_ref[...], muv_ref[...])
    lo = mix(lo_ref[...], mulo_ref[...])
    bd = jnp.where(_head_block_diag(), 1.0, 0.0)

    def head_sum(t):
        return _dot_sum(t, bd)

    kk = kr * kk_ref[...]
    kk = kk * lax.rsqrt(head_sum(kk * kk) + L2_EPS)
    gd = lo[:, 4 * RW_LORA:]
    gate_ref[...] = _dot1(_sigmoid(gd), g2_ref[...])
    ro_ref[...] = r
    vo_ref[...] = v
    kko_ref[...] = kk
    bonus = jnp.zeros_like(r)
    outs = ((lwf_ref, kdf_ref, bbf_ref), (lwb_ref, kdb_ref, bbb_ref))
    for d in range(2):
        wd = lo[:, d * RW_LORA:(d + 1) * RW_LORA]
        ad = lo[:, (2 + d) * RW_LORA:(3 + d) * RW_LORA]
        w_log = -_softplus(-(w0_ref[d:d + 1, :] + _dot1(jnp.tanh(wd), w2_ref[d]))) - 0.5
        a = _sigmoid(a0_ref[d:d + 1, :] + _dot1(ad, a2_ref[d]))
        kd = kr * (1.0 + (a - 1.0) * ka_ref[...])
        lw_ref, kd_ref, bb_ref = outs[d]
        lw_ref[...] = -jnp.exp(w_log)
        kd_ref[...] = kd
        bb_ref[...] = kk * a
        bonus = bonus + head_sum(r * kd * rk_ref[...]) * v
    bon_ref[...] = bonus


def rwkv_prep(p, lora_in, mu, w0, w2, a0, a2, g2, k_k, k_a, r_k, col0, batch):
    t = p.shape[0]
    s = t // batch
    nct = RW_WIDTH // LANES
    base = col0 // LANES
    nlo = RW_LORA_IN
    col = lambda off: pl.BlockSpec((s, LANES), lambda b, c: (b, base + off * nct + c))
    vec = lambda off: pl.BlockSpec((1, LANES), lambda b, c: (0, off * nct + c))
    mu_main = mu[:RW_MAIN].reshape(1, RW_MAIN)
    mu_lo = mu[RW_MAIN:].reshape(1, nlo)
    out_spec = pl.BlockSpec((s, LANES), lambda b, c: (b, c))
    n_out = 11
    return pl.pallas_call(
        _rw_prep_body,
        grid=(batch, nct),
        in_specs=[col(0), col(1), col(2),
                  pl.BlockSpec((s, nlo), lambda b, c: (b, 0)),
                  vec(0), vec(1), vec(2),
                  pl.BlockSpec((1, nlo), lambda b, c: (0, 0)),
                  pl.BlockSpec((2, LANES), lambda b, c: (0, c)),
                  pl.BlockSpec((2, RW_LORA, LANES), lambda b, c: (0, 0, c)),
                  pl.BlockSpec((2, LANES), lambda b, c: (0, c)),
                  pl.BlockSpec((2, RW_LORA, LANES), lambda b, c: (0, 0, c)),
                  pl.BlockSpec((RW_GATE_LORA, LANES), lambda b, c: (0, c)),
                  vec(0), vec(0), vec(0)],
        out_specs=[out_spec] * n_out,
        out_shape=[jax.ShapeDtypeStruct((t, RW_WIDTH), F32)] * n_out,
        compiler_params=_params("parallel", "parallel"),
        name="rwkv_prep",
    )(p, p, p, lora_in, mu_main, mu_main, mu_main, mu_lo, w0, w2, a0, a2, g2,
      k_k.reshape(1, RW_WIDTH), k_a.reshape(1, RW_WIDTH), r_k.reshape(1, RW_WIDTH))


def _rw_scan_body(r_ref, v_ref, kk_ref, bon_ref, gate_ref, lwf_ref, lwb_ref, kdf_ref, kdb_ref, bbf_ref, bbb_ref,
                  lnw_ref, lnb_ref, y_ref, cum_ref, cumt_ref, kdt_ref, bbt_ref, yf_ref, yb_ref,
                  tr_ref, tc_ref, rbk_ref, bkt_ref):
    s = r_ref.shape[0]
    c = CHUNK
    c2 = 2 * c
    row = lax.broadcasted_iota(jnp.int32, (s, 1), 0)
    pos = row % c
    lw_refs, kd_refs, bb_refs = (lwf_ref, lwb_ref), (kdf_ref, kdb_ref), (bbf_ref, bbb_ref)
    for d in range(2):
        cum = _seg_cumsum(lw_refs[d][...], pos, d == 1)
        cum_ref[d] = cum
        cumt_ref[d] = cum.T
        kdt_ref[d] = kd_refs[d][...].T
        bbt_ref[d] = bb_refs[d][...].T

    masks = _tri_masks(c2, c)
    merge = {rev: _merge_masks(c2, rev) for rev in (False, True)}
    eye2_f = jnp.where(masks["eye"], 1.0, 0.0)
    head0 = lax.broadcasted_iota(jnp.int32, (1, LANES), 1) < RW_HEAD_DIM
    head_bd = _head_block_diag()
    npair = s // PAIR
    group = min(RW_PREP_PAIRS, npair)

    def stack(x):
        return jnp.concatenate([jnp.where(head0, x, 0.0), jnp.where(head0, 0.0, x)], axis=0)

    def prep(pidx, d, half):
        reverse = d == 1
        incl2, strict2 = masks[reverse]
        cidx = 2 * pidx + half
        cols = pl.ds(pl.multiple_of(pidx * PAIR, PAIR), PAIR)
        rows = pl.ds(pl.multiple_of(cidx * c, c), c)
        hs = slice(half * c, (half + 1) * c)
        r, v, kk = r_ref[rows, :], v_ref[rows, :], kk_ref[rows, :]
        lw, kd, bb = lw_refs[d][rows, :], kd_refs[d][rows, :], bb_refs[d][rows, :]
        cum = cum_ref[d, rows, :]
        e_neg = jnp.exp(-cum)
        a_s = stack(-kk * jnp.exp(cum - lw))
        r_s = stack(r * jnp.exp(cum))
        ar = jnp.concatenate([a_s, r_s], axis=0)
        bk = jnp.concatenate([stack(bb * e_neg), stack(kd * e_neg)], axis=0)
        g = _dot1(ar, bk, _NT)
        yield
        rbk_ref[d, cidx] = jnp.concatenate([jnp.where(incl2, g[c2:, :c2], 0.0),
                                            jnp.where(incl2, g[c2:, c2:], 0.0)], axis=1).astype(BF16)
        av = _dot1(jnp.where(strict2, g[:c2, c2:], 0.0), stack(v))
        yield
        t_inv = []
        yield from _tri_inverse(-jnp.where(strict2, g[:c2, :c2], 0.0), eye2_f, merge[reverse], t_inv)
        tt = _dot1(t_inv[0], jnp.concatenate([a_s, av], axis=1))
        yield
        tr_ref[d, cidx] = jnp.concatenate([tt[:, :LANES], r_s], axis=0).astype(BF16)
        tc_ref[d, cidx] = tt[:, LANES:].astype(BF16)
        cum_t = cumt_ref[d, :, cols][:, hs]
        last = 0 if reverse else c - 1
        e_out_t = jnp.exp(cum_t[:, last:last + 1] - cum_t)
        bkt_ref[d, cidx] = jnp.concatenate([bbt_ref[d, :, cols][:, hs] * e_out_t,
                                            kdt_ref[d, :, cols][:, hs] * e_out_t], axis=1).astype(BF16)

    def seq(states, pidx, d):
        cum_t2 = cumt_ref[d, :, pl.ds(pl.multiple_of(pidx * PAIR, PAIR), PAIR)]
        for half in ((1, 0) if d else (0, 1)):
            cidx = 2 * pidx + half
            rows = pl.ds(pl.multiple_of(cidx * c, c), c)
            v = v_ref[rows, :]
            x = _dot1(tr_ref[d, cidx], states[d])
            yield
            ps = x[:c2] + tc_ref[d, cidx].astype(F32)
            os_ = x[c2:] + _dot1(rbk_ref[d, cidx], jnp.concatenate([ps, stack(v)], axis=0))
            y = os_[:c] + os_[c:]
            if d:
                yb_ref[rows, :] = y
            else:
                yf_ref[rows, :] = y
            pv = jnp.concatenate([ps[:c] + ps[c:], v], axis=0)
            last = half * c + (0 if d else c - 1)
            e_tot = jnp.exp(cum_t2[:, last:last + 1])
            states[d] = states[d] * e_tot + jnp.where(head_bd, _dot1(bkt_ref[d, cidx], pv), 0.0)
            yield

    ngroups = npair // group

    def prep_chains(g):
        return [prep(npair - 1 - (g * group + j) if d else g * group + j, d, half)
                for j in range(group) for d in range(2) for half in range(2)]

    def seq_chain(states, g, d):
        for j in range(group):
            yield from seq(states, npair - 1 - (g * group + j) if d else g * group + j, d)

    def step(g, carry):
        states = list(carry)
        _interleave(prep_chains(g + 1) + [seq_chain(states, g, 0), seq_chain(states, g, 1)])
        return tuple(states)

    zero = jnp.zeros((LANES, LANES), F32)
    _interleave(prep_chains(0))
    states = list(lax.fori_loop(0, ngroups - 1, step, (zero, zero)))
    _interleave([seq_chain(states, ngroups - 1, 0), seq_chain(states, ngroups - 1, 1)])

    bd = jnp.where(head_bd, 1.0, 0.0)
    y = yf_ref[...] + yb_ref[...]
    mean = _dot_sum(y, bd) * (1.0 / RW_HEAD_DIM)
    yc = y - mean
    var = _dot_sum(yc * yc, bd) * (1.0 / RW_HEAD_DIM)
    yn = yc * lax.rsqrt(var + RW_GN_EPS) * lnw_ref[...] + lnb_ref[...]
    y_ref[...] = ((yn + bon_ref[...]) * gate_ref[...]).astype(y_ref.dtype)


def rwkv_scan(prep, ln_w, ln_b, batch):
    t = prep[0].shape[0]
    s = t // batch
    assert s % PAIR == 0 and (s // PAIR) % min(RW_PREP_PAIRS, s // PAIR) == 0
    nct = RW_WIDTH // LANES
    nchunk = s // CHUNK
    blk = pl.BlockSpec((s, LANES), lambda b, c: (b, c))
    vec = pl.BlockSpec((1, LANES), lambda b, c: (0, c))
    tsp = pltpu.VMEM((2, LANES, s), F32)
    big = pltpu.VMEM((s, LANES), F32)
    per_chunk = lambda rows, cols: pltpu.VMEM((2, nchunk, rows, cols), BF16)
    return pl.pallas_call(
        _rw_scan_body,
        grid=(batch, nct),
        in_specs=[blk] * 11 + [vec, vec],
        out_specs=blk,
        out_shape=jax.ShapeDtypeStruct((t, RW_WIDTH), BF16),
        scratch_shapes=[pltpu.VMEM((2, s, LANES), F32), tsp, tsp, tsp, big, big,
                        per_chunk(2 * PAIR, LANES), per_chunk(PAIR, LANES), per_chunk(PAIR, 2 * LANES),
                        per_chunk(PAIR, LANES)],
        compiler_params=_params("parallel", "parallel"),
        name="rwkv_scan",
    )(*prep, ln_w.reshape(1, RW_WIDTH), ln_b.reshape(1, RW_WIDTH))


def even_mixer(x, g, w_in, w_sc, w_ab, w_out, layer, conv_qkv, a_log, dt_bias, out_gain, conv_sc, batch):
    p, ab = norm_mm(x, g, w_in, layer, ncols=4 * DN_WIDTH, w2=w_sc, wp_t=w_ab.T)
    zeros = jnp.zeros((2 * DN_HEADS,), F32)
    alog32 = jnp.concatenate([zeros, a_log.reshape(-1)]).reshape(1, -1)
    dt32 = jnp.concatenate([zeros, dt_bias.reshape(-1)]).reshape(1, -1)
    y_dn = deltanet(p, ab, conv_qkv.T, alog32, dt32, out_gain, batch)
    y_sc = short_conv(p, conv_sc.T, 4 * DN_WIDTH, batch)
    return mm_res(y_dn, w_out, layer, x, a2=y_sc)


def odd_mixer(x, g, w_in, w_lora, w_out, layer, qn, kn, bias_tab, mu, w0, w2, a0, a2, g2, k_k, k_a, r_k, ln_w, ln_b,
              batch):
    p, lora_in = norm_mm(x, g, w_in, layer, ncols=3 * DSA_QKV + RW_MAIN, wp=w_lora)
    y_c = dilated_attention(p, qn, kn, bias_tab, batch)
    prep = rwkv_prep(p, lora_in, mu, w0, w2, a0, a2, g2, k_k, k_a, r_k, 3 * DSA_QKV, batch)
    y_d = rwkv_scan(prep, ln_w, ln_b, batch)
    return mm_res(y_c, w_out, layer, x, a2=y_d)


def kernel(x, mem, rel_bias, norm_mix, norm_xattn, norm_mem, norm_ffn, xa_wq, xa_wk, xa_wv, xa_wo, xa_qn, xa_kn, ffn_w1, ffn_w2, ev_w_in, ev_w_out, dn_conv, dn_a_log, dn_dt_bias, dn_norm, sc_conv, od_w_in, od_w_out, ca_qn, ca_kn, rw_mu, rw_w0, rw_w2, rw_a0, rw_a2, rw_g2, rw_k_k, rw_k_a, rw_r_k, rw_ln_w, rw_ln_b):
    batch, seq, d = x.shape
    n_mem = mem.shape[1]
    xf = x.reshape(batch * seq, d)
    memf = mem.reshape(batch * n_mem, d)
    bias_tab = _dsa_bias_table(rel_bias)
    c_ab = 4 * DN_WIDTH
    c_sc = c_ab + 4 * DN_HEADS
    c_lo = 3 * DSA_QKV + RW_MAIN
    ev_in, od_in = ev_w_in, od_w_in
    ev_sc = ev_w_in[:, :, c_sc:].astype(BF16)
    ev_out, od_out = ev_w_out.astype(BF16), od_w_out.astype(BF16)
    w_q, w_o = xa_wq.astype(BF16), xa_wo.astype(BF16)
    w_kv = jnp.concatenate([xa_wk, xa_wv], axis=2).astype(BF16)
    for layer in range(DEPTH):
        i = layer // 2
        if layer % 2 == 0:
            xf = even_mixer(xf, norm_mix[layer], ev_in, ev_sc, ev_w_in[i, :, c_ab:c_sc], ev_out, i, dn_conv[i],
                            dn_a_log[i], dn_dt_bias[i], dn_norm[i], sc_conv[i], batch)
        else:
            xf = odd_mixer(xf, norm_mix[layer], od_in, od_w_in[i, :, c_lo:], od_out, i, ca_qn[i], ca_kn[i], bias_tab,
                           rw_mu[i], rw_w0[i], rw_w2[i], rw_a0[i], rw_a2[i], rw_g2[i], rw_k_k[i], rw_k_a[i], rw_r_k[i],
                           rw_ln_w[i], rw_ln_b[i], batch)
        kv = norm_mm(memf, norm_mem[layer], w_kv, layer).reshape(batch, n_mem, 2 * XA_WIDTH)
        xf = xattn(xf, norm_xattn[layer], w_q, kv, w_o, layer, xa_qn[layer], xa_kn[layer], batch)
        h1 = norm_mm(xf, norm_ffn[layer], ffn_w1, layer, act="relu2", out_dtype=BF16)
        xf = mm_res(h1, ffn_w2, layer, xf)
    return xf.reshape(batch, seq, d)
```

```python
import functools
import math

import jax
import jax.numpy as jnp
import numpy as np
from jax import lax
from jax.experimental import pallas as pl
from jax.experimental.pallas import tpu as pltpu

F32 = jnp.float32
BF16 = jnp.bfloat16

D_MODEL = 2048
DEPTH = 4
RMS_EPS = 1e-6
L2_EPS = 1e-6

DN_HEADS = 8
DN_HEAD_DIM = 128
DN_WIDTH = DN_HEADS * DN_HEAD_DIM
SC_WIDTH = D_MODEL - DN_WIDTH
CHUNK = 64
PAIR = 2 * CHUNK
DN_PREP_PAIRS = 4
DN_HEADS_PER_STEP = 2
RW_PREP_PAIRS = 4

DSA_PATTERNS = ((128, 1), (512, 4), (2048, 16))
DSA_GROUPS = len(DSA_PATTERNS)
DSA_HPG = 4
DSA_HEAD_DIM = 128
DSA_HEADS = DSA_GROUPS * DSA_HPG
DSA_QKV = DSA_HEADS * DSA_HEAD_DIM
DSA_SIDE = 64
DSA_QBLK = 128
DSA_PAD = DSA_SIDE * max(d for _, d in DSA_PATTERNS)
DSA_INTERLEAVE = 8
REL_BUCKETS = 32
REL_MAX_DIST = 1024
NEG_INF = -1e30

RW_HEADS = 8
RW_HEAD_DIM = 64
RW_WIDTH = RW_HEADS * RW_HEAD_DIM
RW_LORA = 64
RW_GATE_LORA = 128
RW_MAIN = 3 * RW_WIDTH
RW_LORA_IN = 4 * RW_LORA + RW_GATE_LORA
RW_GN_EPS = 64e-5

XA_HEADS = 4
XA_HEAD_DIM = 128
XA_WIDTH = XA_HEADS * XA_HEAD_DIM

LANES = 128
NORM_ROWS = 256
VMEM_LIMIT_BYTES = 58 * 1024 * 1024


def _params(*sem):
    return pltpu.CompilerParams(dimension_semantics=sem, vmem_limit_bytes=VMEM_LIMIT_BYTES)


_NN = (((1,), (0,)), ((), ()))
_NT = (((1,), (1,)), ((), ()))


def _dot1(a, b, dims=_NN):
    return lax.dot_general(a.astype(BF16), b.astype(BF16), dims, preferred_element_type=F32)


def _dot_sum(a, b01):
    b = b01.astype(BF16)
    hi = a.astype(BF16)
    lo = (a - hi.astype(F32)).astype(BF16)
    dg = functools.partial(lax.dot_general, dimension_numbers=_NN, preferred_element_type=F32)
    return dg(hi, b) + dg(lo, b)


def _interleave(chains):
    chains = list(chains)
    while chains:
        alive = []
        for ch in chains:
            try:
                next(ch)
                alive.append(ch)
            except StopIteration:
                pass
        chains = alive


def _merge_masks(n, reverse):
    ii = lax.broadcasted_iota(jnp.int32, (n, n), 0)
    jj = lax.broadcasted_iota(jnp.int32, (n, n), 1)
    tri = (ii < jj) if reverse else (ii > jj)
    out = []
    size = 1
    while size < CHUNK:
        out.append(tri & ((ii // (2 * size)) == (jj // (2 * size))) & ((ii // size) != (jj // size)))
        size *= 2
    return out


def _tri_inverse(a, eye_f, level_masks, out):
    t = eye_f - jnp.where(level_masks[0], a, 0.0)
    for m in level_masks[1:]:
        x = _dot1(jnp.where(m, a, 0.0), t)
        yield
        t = t - _dot1(t, x)
        yield
    out.append(t)


def _sigmoid(x):
    return 1.0 / (1.0 + jnp.exp(-x))


def _softplus(x):
    return jnp.maximum(x, 0.0) + jnp.log1p(jnp.exp(-jnp.abs(x)))


def _shift_down(x, row):
    return jnp.where(row == 0, 0.0, pltpu.roll(x, 1, 0))


def _shift_up(x, row):
    n = x.shape[0]
    return jnp.where(row == n - 1, 0.0, pltpu.roll(x, n - 1, 0))


def _norm_mm_body(x_ref, g_ref, w_ref, *rest, act, precise, precise_t, n_first):
    if n_first is not None:
        w2_ref, rest = rest[0], rest[1:]
    if precise:
        wp_ref, o_ref, op_ref, xn_ref = rest
    else:
        o_ref, xn_ref = rest

    @pl.when(pl.program_id(1) == 0)
    def _():
        def rows_step(r, _):
            rows = pl.ds(pl.multiple_of(r * NORM_ROWS, NORM_ROWS), NORM_ROWS)
            x = x_ref[rows, :]
            xn = x * lax.rsqrt(jnp.mean(x * x, axis=-1, keepdims=True) + RMS_EPS) * g_ref[...]
            xn_ref[rows, :] = xn.astype(BF16)
            if precise and precise_t:
                op_ref[:, rows] = _dot1(wp_ref[...], xn, _NT)
            elif precise:
                op_ref[rows, :] = _dot1(xn, wp_ref[...])
            return 0

        lax.fori_loop(0, x_ref.shape[0] // NORM_ROWS, rows_step, 0)

    def project(wt_ref):
        acc = jnp.dot(xn_ref[...], wt_ref[...].astype(BF16), preferred_element_type=F32)
        if act == "relu2":
            acc = jnp.square(jnp.maximum(acc, 0.0))
        o_ref[...] = acc.astype(o_ref.dtype)

    if n_first is None:
        project(w_ref)
    else:
        pl.when(pl.program_id(1) < n_first)(lambda: project(w_ref))
        pl.when(pl.program_id(1) >= n_first)(lambda: project(w2_ref))


def norm_mm(x, g, w, layer, ncols=None, w2=None, wp=None, wp_t=None, act=None, out_dtype=F32, tm=1024, tn=1024):
    t, d = x.shape
    n1 = w.shape[2] if ncols is None else ncols
    n = n1 + (0 if w2 is None else w2.shape[2])
    tm = min(tm, t)
    tn = min(tn, n1)
    assert t % tm == 0 and n1 % tn == 0 and n % tn == 0 and tm % NORM_ROWS == 0
    precise = wp is not None or wp_t is not None
    n_first = None if w2 is None else n1 // tn
    in_specs = [
        pl.BlockSpec((tm, d), lambda i, j: (i, 0)),
        pl.BlockSpec((1, d), lambda i, j: (0, 0)),
    ]
    args = [x, g.reshape(1, d), w]
    if w2 is None:
        in_specs.append(pl.BlockSpec((None, d, tn), lambda i, j: (layer, 0, j)))
    else:
        in_specs.append(pl.BlockSpec((None, d, tn), lambda i, j: (layer, 0, jnp.minimum(j, n_first - 1))))
        in_specs.append(pl.BlockSpec((None, d, tn), lambda i, j: (layer, 0, jnp.maximum(j - n_first, 0))))
        args.append(w2)
    out_specs = pl.BlockSpec((tm, tn), lambda i, j: (i, j))
    out_shape = jax.ShapeDtypeStruct((t, n), out_dtype)
    if wp is not None:
        npc = wp.shape[1]
        in_specs.append(pl.BlockSpec((d, npc), lambda i, j: (0, 0)))
        out_specs = [out_specs, pl.BlockSpec((tm, npc), lambda i, j: (i, 0))]
        out_shape = [out_shape, jax.ShapeDtypeStruct((t, npc), F32)]
        args.append(wp)
    elif wp_t is not None:
        npc = wp_t.shape[0]
        in_specs.append(pl.BlockSpec((npc, d), lambda i, j: (0, 0)))
        out_specs = [out_specs, pl.BlockSpec((npc, tm), lambda i, j: (0, i))]
        out_shape = [out_shape, jax.ShapeDtypeStruct((npc, t), F32)]
        args.append(wp_t)
    return pl.pallas_call(
        functools.partial(_norm_mm_body, act=act, precise=precise, precise_t=wp_t is not None, n_first=n_first),
        grid=(t // tm, n // tn),
        in_specs=in_specs,
        out_specs=out_specs,
        out_shape=out_shape,
        scratch_shapes=[pltpu.VMEM((tm, d), BF16)],
        compiler_params=_params("parallel", "arbitrary"),
        name="norm_mm",
    )(*args)


def _mm_res_body(a_ref, *rest, n_first):
    if n_first is not None:
        a2_ref, rest = rest[0], rest[1:]
    w_ref, r_ref, o_ref = rest
    kk = pl.program_id(2)

    def accumulate(lhs_ref):
        acc = jnp.dot(lhs_ref[...], w_ref[...].astype(BF16), preferred_element_type=F32)

        @pl.when(kk == 0)
        def _():
            o_ref[...] = r_ref[...] + acc

        @pl.when(kk > 0)
        def _():
            o_ref[...] += acc

    if n_first is None:
        accumulate(a_ref)
    else:
        pl.when(kk < n_first)(lambda: accumulate(a_ref))
        pl.when(kk >= n_first)(lambda: accumulate(a2_ref))


def mm_res(a, w, layer, res, a2=None, tm=1024, tn=1024, tk=2048):
    t, k1 = a.shape
    k = k1 + (0 if a2 is None else a2.shape[1])
    n = w.shape[2]
    tm, tn, tk = min(tm, t), min(tn, n), min(tk, k1)
    assert t % tm == 0 and n % tn == 0 and k1 % tk == 0 and k % tk == 0
    n_first = None if a2 is None else k1 // tk
    if a2 is None:
        in_specs = [pl.BlockSpec((tm, tk), lambda i, j, kk: (i, kk))]
        args = [a]
    else:
        in_specs = [pl.BlockSpec((tm, tk), lambda i, j, kk: (i, jnp.minimum(kk, n_first - 1))),
                    pl.BlockSpec((tm, tk), lambda i, j, kk: (i, jnp.maximum(kk - n_first, 0)))]
        args = [a, a2]
    in_specs += [
        pl.BlockSpec((None, tk, tn), lambda i, j, kk: (layer, kk, j)),
        pl.BlockSpec((tm, tn), lambda i, j, kk: (i, j)),
    ]
    return pl.pallas_call(
        functools.partial(_mm_res_body, n_first=n_first),
        grid=(t // tm, n // tn, k // tk),
        in_specs=in_specs,
        out_specs=pl.BlockSpec((tm, tn), lambda i, j, kk: (i, j)),
        out_shape=jax.ShapeDtypeStruct((t, n), F32),
        compiler_params=_params("parallel", "parallel", "arbitrary"),
        name="mm_res",
    )(*args, w, res)


def _xattn_body(x_ref, g_ref, wq_ref, kv_ref, wo_ref, qn_ref, kn_ref, o_ref):
    x = x_ref[...]
    xn = x * lax.rsqrt(jnp.mean(x * x, axis=-1, keepdims=True) + RMS_EPS) * g_ref[...]
    q = jnp.dot(xn.astype(BF16), wq_ref[...], preferred_element_type=F32)
    kv = kv_ref[0]
    outs = []
    for h in range(XA_HEADS):
        sl = slice(h * XA_HEAD_DIM, (h + 1) * XA_HEAD_DIM)
        qh = q[:, sl]
        qh = qh * lax.rsqrt(jnp.mean(qh * qh, axis=-1, keepdims=True) + RMS_EPS) * qn_ref[...]
        kh = kv[:, sl]
        kh = kh * lax.rsqrt(jnp.mean(kh * kh, axis=-1, keepdims=True) + RMS_EPS) * kn_ref[...]
        vh = kv[:, XA_WIDTH + h * XA_HEAD_DIM:XA_WIDTH + (h + 1) * XA_HEAD_DIM]
        logits = _dot1(qh, kh, _NT) * (XA_HEAD_DIM ** -0.5)
        m = jnp.max(logits, axis=-1, keepdims=True)
        p = jnp.exp(logits - m)
        s = jnp.sum(p, axis=-1, keepdims=True)
        outs.append(_dot1(p, vh) / s)
    o = jnp.concatenate(outs, axis=-1).astype(BF16)
    o_ref[...] = x + jnp.dot(o, wo_ref[...], preferred_element_type=F32)


def xattn(x, g, wq, kv, wo, layer, qn, kn, batch, ts=512):
    t, d = x.shape
    s = t // batch
    ts = min(ts, s)
    nst = s // ts
    m = kv.shape[1]
    return pl.pallas_call(
        _xattn_body,
        grid=(batch, nst),
        in_specs=[
            pl.BlockSpec((ts, d), lambda b, i: (b * nst + i, 0)),
            pl.BlockSpec((1, d), lambda b, i: (0, 0)),
            pl.BlockSpec((None, d, XA_WIDTH), lambda b, i: (layer, 0, 0)),
            pl.BlockSpec((1, m, 2 * XA_WIDTH), lambda b, i: (b, 0, 0)),
            pl.BlockSpec((None, XA_WIDTH, d), lambda b, i: (layer, 0, 0)),
            pl.BlockSpec((1, XA_HEAD_DIM), lambda b, i: (0, 0)),
            pl.BlockSpec((1, XA_HEAD_DIM), lambda b, i: (0, 0)),
        ],
        out_specs=pl.BlockSpec((ts, d), lambda b, i: (b * nst + i, 0)),
        out_shape=jax.ShapeDtypeStruct((t, d), F32),
        compiler_params=_params("parallel", "parallel"),
        name="xattn",
    )(x, g.reshape(1, d), wq, kv, wo, qn.reshape(1, -1), kn.reshape(1, -1))


def _tri_masks(n, blk):
    ii = lax.broadcasted_iota(jnp.int32, (n, n), 0)
    jj = lax.broadcasted_iota(jnp.int32, (n, n), 1)
    same = (ii // blk) == (jj // blk) if n != blk else None

    def m(c):
        return c if same is None else (c & same)

    return {
        False: (m(ii >= jj), m(ii > jj)),
        True: (m(ii <= jj), m(ii < jj)),
        "eye": ii == jj,
    }


def _seg_cumsum(x, pos, reverse):
    n = x.shape[0]
    sh = 1
    while sh < CHUNK:
        if reverse:
            x = x + jnp.where(pos < CHUNK - sh, pltpu.roll(x, n - sh, 0), 0.0)
        else:
            x = x + jnp.where(pos >= sh, pltpu.roll(x, sh, 0), 0.0)
        sh *= 2
    return x


def _dn_body(q_ref, k_ref, v_ref, gate_ref, ab_ref, cq_ref, ck_ref, cv_ref, alog_ref, dt_ref, gain_ref,
             y_ref, qs_ref, ks_ref, vs_ref, kt_ref, bg_ref, gt_ref, of_ref, ob_ref,
             u_ref, w_ref, qg_ref, qk_ref, kdt_ref, eg_ref, st_ref):
    s = q_ref.shape[0]
    hd = DN_HEAD_DIM
    nh = DN_HEADS_PER_STEP
    row = lax.broadcasted_iota(jnp.int32, (s, 1), 0)
    c = CHUNK
    lane32 = lax.broadcasted_iota(jnp.int32, (1, 4 * DN_HEADS), 1)
    lane128 = lax.broadcasted_iota(jnp.int32, (1, LANES), 1)
    pos_t = lax.broadcasted_iota(jnp.int32, (1, s), 1) % c

    def conv_silu(x_ref, cw_ref, lanes):
        x = x_ref[:, lanes]
        w = cw_ref[:, lanes]
        y = _shift_down(x, row) * w[0:1] + x * w[1:2] + _shift_up(x, row) * w[2:3]
        return y * _sigmoid(y)

    for hh in range(nh):
        h = pl.program_id(1) * nh + hh
        lanes = slice(hh * hd, (hh + 1) * hd)
        q = conv_silu(q_ref, cq_ref, lanes)
        q = q * lax.rsqrt(jnp.sum(q * q, axis=-1, keepdims=True) + L2_EPS) * (hd ** -0.5)
        qs_ref[hh] = q
        k = conv_silu(k_ref, ck_ref, lanes)
        k = k * lax.rsqrt(jnp.sum(k * k, axis=-1, keepdims=True) + L2_EPS)
        ks_ref[hh] = k
        kt_ref[hh] = k.T
        vs_ref[hh] = conv_silu(v_ref, cv_ref, lanes)

        def decay_row(d, h=h):
            idx = (2 + d) * DN_HEADS + h
            a_log = jnp.sum(jnp.where(lane32 == idx, alog_ref[...], 0.0), axis=-1, keepdims=True)
            dt = jnp.sum(jnp.where(lane32 == idx, dt_ref[...], 0.0), axis=-1, keepdims=True)
            g = -jnp.exp(a_log) * _softplus(ab_ref[pl.ds(idx, 1), :] + dt)
            sh = 1
            while sh < c:
                if d:
                    g = g + jnp.where(pos_t < c - sh, pltpu.roll(g, s - sh, 1), 0.0)
                else:
                    g = g + jnp.where(pos_t >= sh, pltpu.roll(g, sh, 1), 0.0)
                sh *= 2
            return g

        rows = [_sigmoid(ab_ref[pl.ds(h, 1), :]), _sigmoid(ab_ref[pl.ds(DN_HEADS + h, 1), :]),
                decay_row(0), decay_row(1)]
        gt = jnp.concatenate(rows + [jnp.zeros((LANES - len(rows), s), F32)], axis=0)
        gt_ref[hh] = gt[0:8, :]
        bg_ref[hh] = gt.T
        for d in range(2):
            st_ref[hh, d] = jnp.zeros((hd, hd), F32)

    masks = _tri_masks(PAIR, c)
    merge = {rev: _merge_masks(PAIR, rev) for rev in (False, True)}
    eye_f = jnp.where(masks["eye"], 1.0, 0.0)
    npair = s // PAIR
    group = min(DN_PREP_PAIRS, npair)

    def prep(hh, pidx, d):
        reverse = d == 1
        rows = pl.ds(pl.multiple_of(pidx * PAIR, PAIR), PAIR)
        qp, kp, vp = qs_ref[hh, rows, :], ks_ref[hh, rows, :], vs_ref[hh, rows, :]
        bgp = bg_ref[hh, rows, :]
        beta = bgp[:, d:d + 1]
        gcol = bgp[:, 2 + d:3 + d]
        grow = gt_ref[hh, 2 + d:3 + d, rows]
        incl, strict = masks[reverse]
        decay = jnp.where(incl, jnp.exp(jnp.where(incl, gcol - grow, 0.0)), 0.0)
        kb = kp * beta
        kq = _dot1(jnp.concatenate([kb, qp], axis=0), kp, _NT)
        yield
        qk_ref[hh, d, rows, :] = jnp.where(incl, kq[PAIR:] * decay, 0.0).astype(BF16)
        t_inv = []
        yield from _tri_inverse(jnp.where(strict, kq[:PAIR] * decay, 0.0), eye_f, merge[reverse], t_inv)
        egc = jnp.exp(gcol)
        uw = _dot1(t_inv[0], jnp.concatenate([vp * beta, kb * egc], axis=1))
        yield
        u_ref[hh, d, rows, :] = uw[:, :hd]
        w_ref[hh, d, rows, :] = uw[:, hd:].astype(BF16)
        qg_ref[hh, d, rows, :] = (qp * egc).astype(BF16)
        last0, last1 = (0, c) if reverse else (c - 1, PAIR - 1)
        glast = jnp.where(lane128 < c, grow[:, last0:last0 + 1], grow[:, last1:last1 + 1])
        kdt_ref[hh, d, :, rows] = (kt_ref[hh, :, rows] * jnp.exp(glast - grow)).astype(BF16)
        eg_ref[hh, d, :, rows] = jnp.broadcast_to(jnp.exp(glast), (8, PAIR))

    def seq(hh, pidx, d):
        pair_rows = pl.ds(pl.multiple_of(pidx * PAIR, PAIR), PAIR)
        kdt = kdt_ref[hh, d, :, pair_rows]
        eg = eg_ref[hh, d, 0:1, pair_rows]
        for half in ((1, 0) if d else (0, 1)):
            r0 = half * c
            rows = pl.ds(pl.multiple_of(pidx * PAIR + r0, c), c)
            state = st_ref[hh, d]
            ws = _dot1(jnp.concatenate([w_ref[hh, d, rows, :], qg_ref[hh, d, rows, :]], axis=0), state)
            yield
            v_new = u_ref[hh, d, rows, :] - ws[:c]
            o = ws[c:] + _dot1(qk_ref[hh, d, rows, r0:r0 + c], v_new)
            if d:
                ob_ref[hh, rows, :] = o
            else:
                of_ref[hh, rows, :] = o
            st_ref[hh, d] = state * eg[:, r0:r0 + 1] + _dot1(kdt[:, r0:r0 + c], v_new)
            yield

    ngroups = npair // group

    def pair_of(g, j, d):
        return npair - 1 - (g * group + j) if d else g * group + j

    def prep_chains(g):
        return [prep(hh, pair_of(g, j, d), d) for j in range(group) for hh in range(nh) for d in range(2)]

    def seq_chain(g, hh, d):
        for j in range(group):
            yield from seq(hh, pair_of(g, j, d), d)

    def seq_chains(g):
        return [seq_chain(g, hh, d) for hh in range(nh) for d in range(2)]

    def step(g, _):
        _interleave(prep_chains(g + 1) + seq_chains(g))
        return 0

    _interleave(prep_chains(0))
    lax.fori_loop(0, ngroups - 1, step, 0)
    _interleave(seq_chains(ngroups - 1))

    for hh in range(nh):
        lanes = slice(hh * hd, (hh + 1) * hd)
        o = of_ref[hh] + ob_ref[hh]
        o = o * lax.rsqrt(jnp.mean(o * o, axis=-1, keepdims=True) + RMS_EPS) * gain_ref[...]
        gate = gate_ref[:, lanes]
        y_ref[:, lanes] = (o * (gate * _sigmoid(gate))).astype(y_ref.dtype)


def deltanet(p, ab, conv_t, alog32, dt32, gain, batch):
    t = p.shape[0]
    s = t // batch
    assert s % PAIR == 0 and (s // PAIR) % min(DN_PREP_PAIRS, s // PAIR) == 0
    hd = DN_HEAD_DIM
    nh = DN_HEADS_PER_STEP
    nsteps = DN_HEADS // nh
    col = lambda off: pl.BlockSpec((s, nh * hd), lambda b, h: (b, off * nsteps + h))
    cw = lambda off: pl.BlockSpec((3, nh * hd), lambda b, h: (0, off * nsteps + h))
    gate_col = pl.BlockSpec((s, nh * hd), lambda b, h: (b, 3 * nsteps + h), pipeline_mode=pl.Buffered(1))
    small = lambda n: pl.BlockSpec((1, n), lambda b, h: (0, 0))
    tok = pltpu.VMEM((nh, s, hd), F32)
    per_dir = pltpu.VMEM((nh, 2, s, hd), BF16)
    return pl.pallas_call(
        _dn_body,
        grid=(batch, nsteps),
        in_specs=[col(0), col(1), col(2), gate_col,
                  pl.BlockSpec((4 * DN_HEADS, s), lambda b, h: (0, b)),
                  cw(0), cw(1), cw(2),
                  small(4 * DN_HEADS), small(4 * DN_HEADS), small(hd)],
        out_specs=pl.BlockSpec((s, nh * hd), lambda b, h: (b, h)),
        out_shape=jax.ShapeDtypeStruct((t, DN_WIDTH), BF16),
        scratch_shapes=[tok, tok, tok, pltpu.VMEM((nh, hd, s), F32), tok, pltpu.VMEM((nh, 8, s), F32), tok, tok,
                        pltpu.VMEM((nh, 2, s, hd), F32), per_dir, per_dir, per_dir,
                        pltpu.VMEM((nh, 2, hd, s), BF16), pltpu.VMEM((nh, 2, 8, s), F32),
                        pltpu.VMEM((nh, 2, hd, hd), F32)],
        compiler_params=_params("parallel", "parallel"),
        name="deltanet",
    )(p, p, p, p, ab, conv_t, conv_t, conv_t, alog32, dt32, gain.reshape(1, hd))


def _sconv_body(b_ref, c_ref, u_ref, w_ref, y_ref):
    s = b_ref.shape[0]
    row = lax.broadcasted_iota(jnp.int32, (s, 1), 0)
    cu = c_ref[...] * u_ref[...]
    w = w_ref[...]
    y = _shift_down(cu, row) * w[0:1] + cu * w[1:2] + _shift_up(cu, row) * w[2:3]
    y_ref[...] = (b_ref[...] * y).astype(y_ref.dtype)


def short_conv(p, conv_t, col0, batch, tc=256):
    t = p.shape[0]
    s = t // batch
    nct = SC_WIDTH // tc
    base = col0 // tc
    col = lambda off: pl.BlockSpec((s, tc), lambda b, c: (b, base + off * nct + c))
    return pl.pallas_call(
        _sconv_body,
        grid=(batch, nct),
        in_specs=[col(0), col(1), col(2), pl.BlockSpec((3, tc), lambda b, c: (0, c))],
        out_specs=pl.BlockSpec((s, tc), lambda b, c: (b, c)),
        out_shape=jax.ShapeDtypeStruct((t, SC_WIDTH), BF16),
        compiler_params=_params("parallel", "parallel"),
        name="short_conv",
    )(p, p, p, conv_t)


def _dsa_body(*refs, seq):
    q_refs = refs[0:3]
    k_refs = refs[3:6]
    v_refs = refs[6:9]
    qn_ref, kn_ref, bias_ref, y_ref, qs_ref, kpad_ref, vpad_ref, og_ref, lse_ref = refs[9:]
    s = seq
    qb = DSA_QBLK
    side = DSA_SIDE
    width = qb + 2 * side
    kj = lax.broadcasted_iota(jnp.int32, (1, width), 1)
    zpad = jnp.zeros((DSA_PAD, DSA_HEAD_DIM), F32)
    for ref in (kpad_ref, vpad_ref):
        ref[0:DSA_PAD, :] = zpad
        ref[DSA_PAD + s:2 * DSA_PAD + s, :] = zpad

    for gi, (_, dil) in enumerate(DSA_PATTERNS):
        sub = s // dil
        nblk = sub // qb
        q = q_refs[gi][...]
        qs_ref[...] = (q * lax.rsqrt(jnp.mean(q * q, axis=-1, keepdims=True) + RMS_EPS) * qn_ref[...]
                       * (DSA_HEAD_DIM ** -0.5))
        k = k_refs[gi][...]
        kpad_ref[DSA_PAD:DSA_PAD + s, :] = k * lax.rsqrt(jnp.mean(k * k, axis=-1, keepdims=True) + RMS_EPS) * kn_ref[...]
        vpad_ref[DSA_PAD:DSA_PAD + s, :] = v_refs[gi][...]
        bias = bias_ref[0, gi]

        def block(t, gi=gi, dil=dil, sub=sub, nblk=nblk, bias=bias):
            r = t // nblk
            n = t % nblk
            rows = pl.ds(r + n * (qb * dil), qb, stride=dil)
            win = pl.ds(DSA_PAD + r + (n * qb - side) * dil, width, stride=dil)
            logits = _dot1(qs_ref[rows, :], kpad_ref[win, :], _NT) + bias
            yield
            pos = n * qb - side + kj
            logits = jnp.where((pos >= 0) & (pos < sub), logits, NEG_INF)
            m = jnp.max(logits, axis=-1, keepdims=True)
            yield
            p = jnp.exp(logits - m)
            ssum = jnp.sum(p, axis=-1, keepdims=True)
            o = _dot1(p, vpad_ref[win, :])
            yield
            og_ref[gi, rows, :] = o / ssum
            lse_ref[gi, rows, :] = jnp.broadcast_to(m + jnp.log(ssum), (qb, DSA_HEAD_DIM))

        nblocks = dil * nblk

        def blocks_step(i, _, block=block):
            _interleave(block(i * DSA_INTERLEAVE + u) for u in range(DSA_INTERLEAVE))
            return 0

        lax.fori_loop(0, nblocks // DSA_INTERLEAVE, blocks_step, 0)

    lse = [lse_ref[gi] for gi in range(DSA_GROUPS)]
    mx = jnp.maximum(jnp.maximum(lse[0], lse[1]), lse[2])
    ws = [jnp.exp(l - mx) for l in lse]
    num = ws[0] * og_ref[0] + ws[1] * og_ref[1] + ws[2] * og_ref[2]
    y_ref[...] = (num / (ws[0] + ws[1] + ws[2])).astype(y_ref.dtype)


def dilated_attention(p, qn, kn, bias_tab, batch):
    t = p.shape[0]
    s = t // batch
    assert (s // DSA_QBLK) % DSA_INTERLEAVE == 0
    hd = DSA_HEAD_DIM
    nh = DSA_HEADS

    def col(part, gi):
        return pl.BlockSpec((s, hd), lambda b, j: (b, part * nh + gi * DSA_HPG + j))

    in_specs = [col(part, gi) for part in range(3) for gi in range(DSA_GROUPS)]
    in_specs += [pl.BlockSpec((1, hd), lambda b, j: (0, 0)), pl.BlockSpec((1, hd), lambda b, j: (0, 0)),
                 pl.BlockSpec((1, DSA_GROUPS, DSA_QBLK, DSA_QBLK + 2 * DSA_SIDE), lambda b, j: (j, 0, 0, 0))]
    big = pltpu.VMEM((s, hd), F32)
    pad = pltpu.VMEM((s + 2 * DSA_PAD, hd), F32)
    grp = pltpu.VMEM((DSA_GROUPS, s, hd), F32)
    return pl.pallas_call(
        functools.partial(_dsa_body, seq=s),
        grid=(batch, DSA_HPG),
        in_specs=in_specs,
        out_specs=pl.BlockSpec((s, hd), lambda b, j: (b, j)),
        out_shape=jax.ShapeDtypeStruct((t, DSA_HPG * hd), BF16),
        scratch_shapes=[big, pad, pad, grp, grp],
        compiler_params=_params("parallel", "parallel"),
        name="dilated_attention",
    )(*([p] * 9), qn.reshape(1, hd), kn.reshape(1, hd), bias_tab)


def _t5_bucket(rel):
    half = REL_BUCKETS // 2
    max_exact = half // 2
    n = np.abs(rel)
    scaled = (np.log(np.maximum(n, max_exact).astype(np.float32) / np.float32(max_exact))
              / np.float32(math.log(REL_MAX_DIST / max_exact)))
    large = np.minimum(max_exact + (scaled * np.float32(half - max_exact)).astype(np.int32), half - 1)
    return np.where(rel > 0, half, 0) + np.where(n < max_exact, n, large)


def _dsa_bias_table(rel_bias):
    width = DSA_QBLK + 2 * DSA_SIDE
    tabs = []
    for gi, (_, dil) in enumerate(DSA_PATTERNS):
        offs = np.arange(-DSA_SIDE, DSA_SIDE + 1, dtype=np.int32) * dil
        band = rel_bias[_t5_bucket(offs)][:, gi * DSA_HPG:(gi + 1) * DSA_HPG].astype(F32)
        fill = jnp.full((DSA_QBLK - 1, DSA_HPG), NEG_INF, F32)
        line = jnp.concatenate([fill, band, fill], axis=0)
        tabs.append(jnp.stack([line[DSA_QBLK - 1 - q:DSA_QBLK - 1 - q + width] for q in range(DSA_QBLK)]))
    return jnp.transpose(jnp.stack(tabs), (3, 0, 1, 2))


def _head_block_diag():
    ii = lax.broadcasted_iota(jnp.int32, (LANES, LANES), 0)
    jj = lax.broadcasted_iota(jnp.int32, (LANES, LANES), 1)
    return (ii // RW_HEAD_DIM) == (jj // RW_HEAD_DIM)


def _rw_prep_body(r_ref, k_ref, v_ref, lo_ref, mur_ref, muk_ref, muv_ref, mulo_ref, w0_ref, w2_ref, a0_ref,
                  a2_ref, g2_ref, kk_ref, ka_ref, rk_ref,
                  ro_ref, vo_ref, kko_ref, bon_ref, gate_ref, lwf_ref, lwb_ref, kdf_ref, kdb_ref, bbf_ref, bbb_ref):
    s = r_ref.shape[0]
    row = lax.broadcasted_iota(jnp.int32, (s, 1), 0)

    def mix(t, mu):
        return t + mu * (0.5 * (_shift_down(t, row) + _shift_up(t, row)) - t)

    r = mix(r_ref[...], mur_ref[...])
    kr = mix(k_ref[...], muk_ref[...])
    v = mix(v_ref[...], muv_ref[...])
    lo = mix(lo_ref[...], mulo_ref[...])
    bd = jnp.where(_head_block_diag(), 1.0, 0.0)

    def head_sum(t):
        return _dot_sum(t, bd)

    kk = kr * kk_ref[...]
    kk = kk * lax.rsqrt(head_sum(kk * kk) + L2_EPS)
    gd = lo[:, 4 * RW_LORA:]
    gate_ref[...] = _dot1(_sigmoid(gd), g2_ref[...])
    ro_ref[...] = r
    vo_ref[...] = v
    kko_ref[...] = kk
    bonus = jnp.zeros_like(r)
    outs = ((lwf_ref, kdf_ref, bbf_ref), (lwb_ref, kdb_ref, bbb_ref))
    for d in range(2):
        wd = lo[:, d * RW_LORA:(d + 1) * RW_LORA]
        ad = lo[:, (2 + d) * RW_LORA:(3 + d) * RW_LORA]
        w_log = -_softplus(-(w0_ref[d:d + 1, :] + _dot1(jnp.tanh(wd), w2_ref[d]))) - 0.5
        a = _sigmoid(a0_ref[d:d + 1, :] + _dot1(ad, a2_ref[d]))
        kd = kr * (1.0 + (a - 1.0) * ka_ref[...])
        lw_ref, kd_ref, bb_ref = outs[d]
        lw_ref[...] = -jnp.exp(w_log)
        kd_ref[...] = kd
        bb_ref[...] = kk * a
        bonus = bonus + head_sum(r * kd * rk_ref[...]) * v
    bon_ref[...] = bonus


def rwkv_prep(p, lora_in, mu, w0, w2, a0, a2, g2, k_k, k_a, r_k, col0, batch):
    t = p.shape[0]
    s = t // batch
    nct = RW_WIDTH // LANES
    base = col0 // LANES
    nlo = RW_LORA_IN
    col = lambda off: pl.BlockSpec((s, LANES), lambda b, c: (b, base + off * nct + c))
    vec = lambda off: pl.BlockSpec((1, LANES), lambda b, c: (0, off * nct + c))
    mu_main = mu[:RW_MAIN].reshape(1, RW_MAIN)
    mu_lo = mu[RW_MAIN:].reshape(1, nlo)
    out_spec = pl.BlockSpec((s, LANES), lambda b, c: (b, c))
    n_out = 11
    return pl.pallas_call(
        _rw_prep_body,
        grid=(batch, nct),
        in_specs=[col(0), col(1), col(2),
                  pl.BlockSpec((s, nlo), lambda b, c: (b, 0)),
                  vec(0), vec(1), vec(2),
                  pl.BlockSpec((1, nlo), lambda b, c: (0, 0)),
                  pl.BlockSpec((2, LANES), lambda b, c: (0, c)),
                  pl.BlockSpec((2, RW_LORA, LANES), lambda b, c: (0, 0, c)),
                  pl.BlockSpec((2, LANES), lambda b, c: (0, c)),
                  pl.BlockSpec((2, RW_LORA, LANES), lambda b, c: (0, 0, c)),
                  pl.BlockSpec((RW_GATE_LORA, LANES), lambda b, c: (0, c)),
                  vec(0), vec(0), vec(0)],
        out_specs=[out_spec] * n_out,
        out_shape=[jax.ShapeDtypeStruct((t, RW_WIDTH), F32)] * n_out,
        compiler_params=_params("parallel", "parallel"),
        name="rwkv_prep",
    )(p, p, p, lora_in, mu_main, mu_main, mu_main, mu_lo, w0, w2, a0, a2, g2,
      k_k.reshape(1, RW_WIDTH), k_a.reshape(1, RW_WIDTH), r_k.reshape(1, RW_WIDTH))


def _rw_scan_body(r_ref, v_ref, kk_ref, bon_ref, gate_ref, lwf_ref, lwb_ref, kdf_ref, kdb_ref, bbf_ref, bbb_ref,
                  lnw_ref, lnb_ref, y_ref, cum_ref, cumt_ref, kdt_ref, bbt_ref, yf_ref, yb_ref,
                  tr_ref, tc_ref, rbk_ref, bkt_ref):
    s = r_ref.shape[0]
    c = CHUNK
    c2 = 2 * c
    row = lax.broadcasted_iota(jnp.int32, (s, 1), 0)
    pos = row % c
    lw_refs, kd_refs, bb_refs = (lwf_ref, lwb_ref), (kdf_ref, kdb_ref), (bbf_ref, bbb_ref)
    for d in range(2):
        cum = _seg_cumsum(lw_refs[d][...], pos, d == 1)
        cum_ref[d] = cum
        cumt_ref[d] = cum.T
        kdt_ref[d] = kd_refs[d][...].T
        bbt_ref[d] = bb_refs[d][...].T

    masks = _tri_masks(c2, c)
    merge = {rev: _merge_masks(c2, rev) for rev in (False, True)}
    eye2_f = jnp.where(masks["eye"], 1.0, 0.0)
    head0 = lax.broadcasted_iota(jnp.int32, (1, LANES), 1) < RW_HEAD_DIM
    head_bd = _head_block_diag()
    npair = s // PAIR
    group = min(RW_PREP_PAIRS, npair)

    def stack(x):
        return jnp.concatenate([jnp.where(head0, x, 0.0), jnp.where(head0, 0.0, x)], axis=0)

    def prep(pidx, d, half):
        reverse = d == 1
        incl2, strict2 = masks[reverse]
        cidx = 2 * pidx + half
        cols = pl.ds(pl.multiple_of(pidx * PAIR, PAIR), PAIR)
        rows = pl.ds(pl.multiple_of(cidx * c, c), c)
        hs = slice(half * c, (half + 1) * c)
        r, v, kk = r_ref[rows, :], v_ref[rows, :], kk_ref[rows, :]
        lw, kd, bb = lw_refs[d][rows, :], kd_refs[d][rows, :], bb_refs[d][rows, :]
        cum = cum_ref[d, rows, :]
        e_neg = jnp.exp(-cum)
        a_s = stack(-kk * jnp.exp(cum - lw))
        r_s = stack(r * jnp.exp(cum))
        ar = jnp.concatenate([a_s, r_s], axis=0)
        bk = jnp.concatenate([stack(bb * e_neg), stack(kd * e_neg)], axis=0)
        g = _dot1(ar, bk, _NT)
        yield
        rbk_ref[d, cidx] = jnp.concatenate([jnp.where(incl2, g[c2:, :c2], 0.0),
                                            jnp.where(incl2, g[c2:, c2:], 0.0)], axis=1).astype(BF16)
        av = _dot1(jnp.where(strict2, g[:c2, c2:], 0.0), stack(v))
        yield
        t_inv = []
        yield from _tri_inverse(-jnp.where(strict2, g[:c2, :c2], 0.0), eye2_f, merge[reverse], t_inv)
        tt = _dot1(t_inv[0], jnp.concatenate([a_s, av], axis=1))
        yield
        tr_ref[d, cidx] = jnp.concatenate([tt[:, :LANES], r_s], axis=0).astype(BF16)
        tc_ref[d, cidx] = tt[:, LANES:].astype(BF16)
        cum_t = cumt_ref[d, :, cols][:, hs]
        last = 0 if reverse else c - 1
        e_out_t = jnp.exp(cum_t[:, last:last + 1] - cum_t)
        bkt_ref[d, cidx] = jnp.concatenate([bbt_ref[d, :, cols][:, hs] * e_out_t,
                                            kdt_ref[d, :, cols][:, hs] * e_out_t], axis=1).astype(BF16)

    def seq(states, pidx, d):
        cum_t2 = cumt_ref[d, :, pl.ds(pl.multiple_of(pidx * PAIR, PAIR), PAIR)]
        for half in ((1, 0) if d else (0, 1)):
            cidx = 2 * pidx + half
            rows = pl.ds(pl.multiple_of(cidx * c, c), c)
            v = v_ref[rows, :]
            x = _dot1(tr_ref[d, cidx], states[d])
            yield
            ps = x[:c2] + tc_ref[d, cidx].astype(F32)
            os_ = x[c2:] + _dot1(rbk_ref[d, cidx], jnp.concatenate([ps, stack(v)], axis=0))
            y = os_[:c] + os_[c:]
            if d:
                yb_ref[rows, :] = y
            else:
                yf_ref[rows, :] = y
            pv = jnp.concatenate([ps[:c] + ps[c:], v], axis=0)
            last = half * c + (0 if d else c - 1)
            e_tot = jnp.exp(cum_t2[:, last:last + 1])
            states[d] = states[d] * e_tot + jnp.where(head_bd, _dot1(bkt_ref[d, cidx], pv), 0.0)
            yield

    ngroups = npair // group

    def prep_chains(g):
        return [prep(npair - 1 - (g * group + j) if d else g * group + j, d, half)
                for j in range(group) for d in range(2) for half in range(2)]

    def seq_chain(states, g, d):
        for j in range(group):
            yield from seq(states, npair - 1 - (g * group + j) if d else g * group + j, d)

    def step(g, carry):
        states = list(carry)
        _interleave(prep_chains(g + 1) + [seq_chain(states, g, 0), seq_chain(states, g, 1)])
        return tuple(states)

    zero = jnp.zeros((LANES, LANES), F32)
    _interleave(prep_chains(0))
    states = list(lax.fori_loop(0, ngroups - 1, step, (zero, zero)))
    _interleave([seq_chain(states, ngroups - 1, 0), seq_chain(states, ngroups - 1, 1)])

    bd = jnp.where(head_bd, 1.0, 0.0)
    y = yf_ref[...] + yb_ref[...]
    mean = _dot_sum(y, bd) * (1.0 / RW_HEAD_DIM)
    yc = y - mean
    var = _dot_sum(yc * yc, bd) * (1.0 / RW_HEAD_DIM)
    yn = yc * lax.rsqrt(var + RW_GN_EPS) * lnw_ref[...] + lnb_ref[...]
    y_ref[...] = ((yn + bon_ref[...]) * gate_ref[...]).astype(y_ref.dtype)


def rwkv_scan(prep, ln_w, ln_b, batch):
    t = prep[0].shape[0]
    s = t // batch
    assert s % PAIR == 0 and (s // PAIR) % min(RW_PREP_PAIRS, s // PAIR) == 0
    nct = RW_WIDTH // LANES
    nchunk = s // CHUNK
    blk = pl.BlockSpec((s, LANES), lambda b, c: (b, c))
    vec = pl.BlockSpec((1, LANES), lambda b, c: (0, c))
    tsp = pltpu.VMEM((2, LANES, s), F32)
    big = pltpu.VMEM((s, LANES), F32)
    per_chunk = lambda rows, cols: pltpu.VMEM((2, nchunk, rows, cols), BF16)
    return pl.pallas_call(
        _rw_scan_body,
        grid=(batch, nct),
        in_specs=[blk] * 11 + [vec, vec],
        out_specs=blk,
        out_shape=jax.ShapeDtypeStruct((t, RW_WIDTH), BF16),
        scratch_shapes=[pltpu.VMEM((2, s, LANES), F32), tsp, tsp, tsp, big, big,
                        per_chunk(2 * PAIR, LANES), per_chunk(PAIR, LANES), per_chunk(PAIR, 2 * LANES),
                        per_chunk(PAIR, LANES)],
        compiler_params=_params("parallel", "parallel"),
        name="rwkv_scan",
    )(*prep, ln_w.reshape(1, RW_WIDTH), ln_b.reshape(1, RW_WIDTH))


def even_mixer(x, g, w_in, w_sc, w_ab, w_out, layer, conv_qkv, a_log, dt_bias, out_gain, conv_sc, batch):
    p, ab = norm_mm(x, g, w_in, layer, ncols=4 * DN_WIDTH, w2=w_sc, wp_t=w_ab.T)
    zeros = jnp.zeros((2 * DN_HEADS,), F32)
    alog32 = jnp.concatenate([zeros, a_log.reshape(-1)]).reshape(1, -1)
    dt32 = jnp.concatenate([zeros, dt_bias.reshape(-1)]).reshape(1, -1)
    y_dn = deltanet(p, ab, conv_qkv.T, alog32, dt32, out_gain, batch)
    y_sc = short_conv(p, conv_sc.T, 4 * DN_WIDTH, batch)
    return mm_res(y_dn, w_out, layer, x, a2=y_sc, tn=D_MODEL)


def odd_mixer(x, g, w_in, w_lora, w_out, layer, qn, kn, bias_tab, mu, w0, w2, a0, a2, g2, k_k, k_a, r_k, ln_w, ln_b,
              batch):
    p, lora_in = norm_mm(x, g, w_in, layer, ncols=3 * DSA_QKV + RW_MAIN, wp=w_lora)
    y_c = dilated_attention(p, qn, kn, bias_tab, batch)
    prep = rwkv_prep(p, lora_in, mu, w0, w2, a0, a2, g2, k_k, k_a, r_k, 3 * DSA_QKV, batch)
    y_d = rwkv_scan(prep, ln_w, ln_b, batch)
    return mm_res(y_c, w_out, layer, x, a2=y_d, tn=D_MODEL)


def kernel(x, mem, rel_bias, norm_mix, norm_xattn, norm_mem, norm_ffn, xa_wq, xa_wk, xa_wv, xa_wo, xa_qn, xa_kn, ffn_w1, ffn_w2, ev_w_in, ev_w_out, dn_conv, dn_a_log, dn_dt_bias, dn_norm, sc_conv, od_w_in, od_w_out, ca_qn, ca_kn, rw_mu, rw_w0, rw_w2, rw_a0, rw_a2, rw_g2, rw_k_k, rw_k_a, rw_r_k, rw_ln_w, rw_ln_b):
    batch, seq, d = x.shape
    n_mem = mem.shape[1]
    xf = x.reshape(batch * seq, d)
    memf = mem.reshape(batch * n_mem, d)
    bias_tab = _dsa_bias_table(rel_bias)
    c_ab = 4 * DN_WIDTH
    c_sc = c_ab + 4 * DN_HEADS
    c_lo = 3 * DSA_QKV + RW_MAIN
    ev_in, od_in = ev_w_in.astype(BF16), od_w_in.astype(BF16)
    ev_sc = ev_in[:, :, c_sc:]
    ev_out, od_out = ev_w_out.astype(BF16), od_w_out.astype(BF16)
    w_q, w_o = xa_wq.astype(BF16), xa_wo.astype(BF16)
    w_kv = jnp.concatenate([xa_wk, xa_wv], axis=2).astype(BF16)
    for layer in range(DEPTH):
        i = layer // 2
        if layer % 2 == 0:
            xf = even_mixer(xf, norm_mix[layer], ev_in, ev_sc, ev_w_in[i, :, c_ab:c_sc], ev_out, i, dn_conv[i],
                            dn_a_log[i], dn_dt_bias[i], dn_norm[i], sc_conv[i], batch)
        else:
            xf = odd_mixer(xf, norm_mix[layer], od_in, od_w_in[i, :, c_lo:], od_out, i, ca_qn[i], ca_kn[i], bias_tab,
                           rw_mu[i], rw_w0[i], rw_w2[i], rw_a0[i], rw_a2[i], rw_g2[i], rw_k_k[i], rw_k_a[i], rw_r_k[i],
                           rw_ln_w[i], rw_ln_b[i], batch)
        kv = norm_mm(memf, norm_mem[layer], w_kv, layer).reshape(batch, n_mem, 2 * XA_WIDTH)
        xf = xattn(xf, norm_xattn[layer], w_q, kv, w_o, layer, xa_qn[layer], xa_kn[layer], batch)
        h1 = norm_mm(xf, norm_ffn[layer], ffn_w1, layer, act="relu2", out_dtype=BF16)
        xf = mm_res(h1, ffn_w2, layer, xf)
    return xf.reshape(batch, seq, d)
```

```python
import functools
import math

import jax
import jax.numpy as jnp
import numpy as np
from jax import lax
from jax.experimental import pallas as pl
from jax.experimental.pallas import tpu as pltpu

F32 = jnp.float32
BF16 = jnp.bfloat16

D_MODEL = 2048
DEPTH = 4
RMS_EPS = 1e-6
L2_EPS = 1e-6

DN_HEADS = 8
DN_HEAD_DIM = 128
DN_WIDTH = DN_HEADS * DN_HEAD_DIM
SC_WIDTH = D_MODEL - DN_WIDTH
CHUNK = 64
PAIR = 2 * CHUNK
DN_PREP_PAIRS = 4
DN_HEADS_PER_STEP = 2
RW_PREP_PAIRS = 4

DSA_PATTERNS = ((128, 1), (512, 4), (2048, 16))
DSA_GROUPS = len(DSA_PATTERNS)
DSA_HPG = 4
DSA_HEAD_DIM = 128
DSA_HEADS = DSA_GROUPS * DSA_HPG
DSA_QKV = DSA_HEADS * DSA_HEAD_DIM
DSA_SIDE = 64
DSA_QBLK = 128
DSA_PAD = DSA_SIDE * max(d for _, d in DSA_PATTERNS)
DSA_INTERLEAVE = 8
REL_BUCKETS = 32
REL_MAX_DIST = 1024
NEG_INF = -1e30

RW_HEADS = 8
RW_HEAD_DIM = 64
RW_WIDTH = RW_HEADS * RW_HEAD_DIM
RW_LORA = 64
RW_GATE_LORA = 128
RW_MAIN = 3 * RW_WIDTH
RW_LORA_IN = 4 * RW_LORA + RW_GATE_LORA
RW_GN_EPS = 64e-5

XA_HEADS = 4
XA_HEAD_DIM = 128
XA_WIDTH = XA_HEADS * XA_HEAD_DIM

LANES = 128
SUBLANES = 8
NORM_ROWS = 256
VMEM_LIMIT_BYTES = 58 * 1024 * 1024


def _params(*sem):
    return pltpu.CompilerParams(dimension_semantics=sem, vmem_limit_bytes=VMEM_LIMIT_BYTES)


_NN = (((1,), (0,)), ((), ()))
_NT = (((1,), (1,)), ((), ()))


def _dot1(a, b, dims=_NN):
    return lax.dot_general(a.astype(BF16), b.astype(BF16), dims, preferred_element_type=F32)


def _dot_sum(a, b01):
    b = b01.astype(BF16)
    hi = a.astype(BF16)
    lo = (a - hi.astype(F32)).astype(BF16)
    dg = functools.partial(lax.dot_general, dimension_numbers=_NN, preferred_element_type=F32)
    return dg(hi, b) + dg(lo, b)


def _interleave(chains):
    chains = list(chains)
    while chains:
        alive = []
        for ch in chains:
            try:
                next(ch)
                alive.append(ch)
            except StopIteration:
                pass
        chains = alive


def _merge_masks(n, reverse):
    ii = lax.broadcasted_iota(jnp.int32, (n, n), 0)
    jj = lax.broadcasted_iota(jnp.int32, (n, n), 1)
    tri = (ii < jj) if reverse else (ii > jj)
    out = []
    size = 1
    while size < CHUNK:
        out.append(tri & ((ii // (2 * size)) == (jj // (2 * size))) & ((ii // size) != (jj // size)))
        size *= 2
    return out


def _tri_inverse(a, eye_f, level_masks, out):
    t = eye_f - jnp.where(level_masks[0], a, 0.0)
    for m in level_masks[1:]:
        x = _dot1(jnp.where(m, a, 0.0), t)
        yield
        t = t - _dot1(t, x)
        yield
    out.append(t)


def _sigmoid(x):
    return 1.0 / (1.0 + jnp.exp(-x))


def _softplus(x):
    return jnp.maximum(x, 0.0) + jnp.log1p(jnp.exp(-jnp.abs(x)))


def _shift_down(x, row):
    return jnp.where(row == 0, 0.0, pltpu.roll(x, 1, 0))


def _shift_up(x, row):
    n = x.shape[0]
    return jnp.where(row == n - 1, 0.0, pltpu.roll(x, n - 1, 0))


def _norm_mm_body(x_ref, g_ref, w_ref, *rest, act, extra, extra_t, n_first):
    if n_first is not None:
        w2_ref, rest = rest[0], rest[1:]
    if extra:
        wp_ref, o_ref, op_ref, xn_ref = rest
    else:
        o_ref, xn_ref = rest

    @pl.when(pl.program_id(1) == 0)
    def _():
        def rows_step(r, _):
            rows = pl.ds(pl.multiple_of(r * NORM_ROWS, NORM_ROWS), NORM_ROWS)
            x = x_ref[rows, :]
            xn = x * lax.rsqrt(jnp.mean(x * x, axis=-1, keepdims=True) + RMS_EPS) * g_ref[...]
            xn_ref[rows, :] = xn.astype(BF16)
            if extra and extra_t:
                op_ref[:, rows] = _dot1(wp_ref[...], xn, _NT)
            elif extra:
                op_ref[rows, :] = _dot1(xn, wp_ref[...])
            return 0

        lax.fori_loop(0, x_ref.shape[0] // NORM_ROWS, rows_step, 0)

    def project(wt_ref):
        acc = jnp.dot(xn_ref[...], wt_ref[...].astype(BF16), preferred_element_type=F32)
        if act == "relu2":
            acc = jnp.square(jnp.maximum(acc, 0.0))
        o_ref[...] = acc.astype(o_ref.dtype)

    if n_first is None:
        project(w_ref)
    else:
        pl.when(pl.program_id(1) < n_first)(lambda: project(w_ref))
        pl.when(pl.program_id(1) >= n_first)(lambda: project(w2_ref))


def norm_mm(x, g, w, layer, ncols=None, w2=None, wp=None, wp_t=None, act=None, out_dtype=F32, tm=1024, tn=1024):
    t, d = x.shape
    n1 = w.shape[2] if ncols is None else ncols
    n = n1 + (0 if w2 is None else w2.shape[2])
    tm = min(tm, t)
    tn = min(tn, n1)
    assert t % tm == 0 and n1 % tn == 0 and n % tn == 0 and tm % NORM_ROWS == 0
    extra = wp is not None or wp_t is not None
    n_first = None if w2 is None else n1 // tn
    in_specs = [
        pl.BlockSpec((tm, d), lambda i, j: (i, 0)),
        pl.BlockSpec((1, d), lambda i, j: (0, 0)),
    ]
    args = [x, g.reshape(1, d), w]
    if w2 is None:
        in_specs.append(pl.BlockSpec((None, d, tn), lambda i, j: (layer, 0, j)))
    else:
        in_specs.append(pl.BlockSpec((None, d, tn), lambda i, j: (layer, 0, jnp.minimum(j, n_first - 1))))
        in_specs.append(pl.BlockSpec((None, d, tn), lambda i, j: (layer, 0, jnp.maximum(j - n_first, 0))))
        args.append(w2)
    out_specs = pl.BlockSpec((tm, tn), lambda i, j: (i, j))
    out_shape = jax.ShapeDtypeStruct((t, n), out_dtype)
    if wp is not None:
        npc = wp.shape[1]
        in_specs.append(pl.BlockSpec((d, npc), lambda i, j: (0, 0)))
        out_specs = [out_specs, pl.BlockSpec((tm, npc), lambda i, j: (i, 0))]
        out_shape = [out_shape, jax.ShapeDtypeStruct((t, npc), F32)]
        args.append(wp)
    elif wp_t is not None:
        npc = wp_t.shape[0]
        in_specs.append(pl.BlockSpec((npc, d), lambda i, j: (0, 0)))
        out_specs = [out_specs, pl.BlockSpec((npc, tm), lambda i, j: (0, i))]
        out_shape = [out_shape, jax.ShapeDtypeStruct((npc, t), F32)]
        args.append(wp_t)
    return pl.pallas_call(
        functools.partial(_norm_mm_body, act=act, extra=extra, extra_t=wp_t is not None, n_first=n_first),
        grid=(t // tm, n // tn),
        in_specs=in_specs,
        out_specs=out_specs,
        out_shape=out_shape,
        scratch_shapes=[pltpu.VMEM((tm, d), BF16)],
        compiler_params=_params("parallel", "arbitrary"),
        name="norm_mm",
    )(*args)


def _mm_res_body(a_ref, *rest, n_first):
    if n_first is not None:
        a2_ref, rest = rest[0], rest[1:]
    w_ref, r_ref, o_ref = rest
    kk = pl.program_id(2)

    @pl.when(kk == 0)
    def _():
        o_ref[...] = r_ref[...]

    def accumulate(lhs_ref):
        o_ref[...] += jnp.dot(lhs_ref[...], w_ref[...].astype(BF16), preferred_element_type=F32)

    if n_first is None:
        accumulate(a_ref)
    else:
        pl.when(kk < n_first)(lambda: accumulate(a_ref))
        pl.when(kk >= n_first)(lambda: accumulate(a2_ref))


def mm_res(a, w, layer, res, a2=None, tm=1024, tn=1024, tk=2048):
    t, k1 = a.shape
    k = k1 + (0 if a2 is None else a2.shape[1])
    n = w.shape[2]
    tm, tn, tk = min(tm, t), min(tn, n), min(tk, k1)
    assert t % tm == 0 and n % tn == 0 and k1 % tk == 0 and k % tk == 0
    n_first = None if a2 is None else k1 // tk
    if a2 is None:
        in_specs = [pl.BlockSpec((tm, tk), lambda i, j, kk: (i, kk))]
        args = [a]
    else:
        in_specs = [pl.BlockSpec((tm, tk), lambda i, j, kk: (i, jnp.minimum(kk, n_first - 1))),
                    pl.BlockSpec((tm, tk), lambda i, j, kk: (i, jnp.maximum(kk - n_first, 0)))]
        args = [a, a2]
    in_specs += [
        pl.BlockSpec((None, tk, tn), lambda i, j, kk: (layer, kk, j)),
        pl.BlockSpec((tm, tn), lambda i, j, kk: (i, j)),
    ]
    return pl.pallas_call(
        functools.partial(_mm_res_body, n_first=n_first),
        grid=(t // tm, n // tn, k // tk),
        in_specs=in_specs,
        out_specs=pl.BlockSpec((tm, tn), lambda i, j, kk: (i, j)),
        out_shape=jax.ShapeDtypeStruct((t, n), F32),
        compiler_params=_params("parallel", "parallel", "arbitrary"),
        name="mm_res",
    )(*args, w, res)


def _xattn_body(x_ref, g_ref, wq_ref, kv_ref, wo_ref, qn_ref, kn_ref, o_ref):
    x = x_ref[...]
    xn = x * lax.rsqrt(jnp.mean(x * x, axis=-1, keepdims=True) + RMS_EPS) * g_ref[...]
    q = jnp.dot(xn.astype(BF16), wq_ref[...], preferred_element_type=F32)
    kv = kv_ref[0]
    outs = []
    for h in range(XA_HEADS):
        sl = slice(h * XA_HEAD_DIM, (h + 1) * XA_HEAD_DIM)
        qh = q[:, sl]
        qh = qh * lax.rsqrt(jnp.mean(qh * qh, axis=-1, keepdims=True) + RMS_EPS) * qn_ref[...]
        kh = kv[:, sl]
        kh = kh * lax.rsqrt(jnp.mean(kh * kh, axis=-1, keepdims=True) + RMS_EPS) * kn_ref[...]
        vh = kv[:, XA_WIDTH + h * XA_HEAD_DIM:XA_WIDTH + (h + 1) * XA_HEAD_DIM]
        logits = _dot1(qh, kh, _NT) * (XA_HEAD_DIM ** -0.5)
        m = jnp.max(logits, axis=-1, keepdims=True)
        p = jnp.exp(logits - m)
        s = jnp.sum(p, axis=-1, keepdims=True)
        outs.append(_dot1(p, vh) / s)
    o = jnp.concatenate(outs, axis=-1).astype(BF16)
    o_ref[...] = x + jnp.dot(o, wo_ref[...], preferred_element_type=F32)


def xattn(x, g, wq, kv, wo, layer, qn, kn, batch, ts=1024):
    t, d = x.shape
    s = t // batch
    ts = min(ts, s)
    nst = s // ts
    m = kv.shape[1]
    return pl.pallas_call(
        _xattn_body,
        grid=(batch, nst),
        in_specs=[
            pl.BlockSpec((ts, d), lambda b, i: (b * nst + i, 0)),
            pl.BlockSpec((1, d), lambda b, i: (0, 0)),
            pl.BlockSpec((None, d, XA_WIDTH), lambda b, i: (layer, 0, 0)),
            pl.BlockSpec((1, m, 2 * XA_WIDTH), lambda b, i: (b, 0, 0)),
            pl.BlockSpec((None, XA_WIDTH, d), lambda b, i: (layer, 0, 0)),
            pl.BlockSpec((1, XA_HEAD_DIM), lambda b, i: (0, 0)),
            pl.BlockSpec((1, XA_HEAD_DIM), lambda b, i: (0, 0)),
        ],
        out_specs=pl.BlockSpec((ts, d), lambda b, i: (b * nst + i, 0)),
        out_shape=jax.ShapeDtypeStruct((t, d), F32),
        compiler_params=_params("parallel", "parallel"),
        name="xattn",
    )(x, g.reshape(1, d), wq, kv, wo, qn.reshape(1, -1), kn.reshape(1, -1))


def _tri_masks(n, blk):
    ii = lax.broadcasted_iota(jnp.int32, (n, n), 0)
    jj = lax.broadcasted_iota(jnp.int32, (n, n), 1)
    same = (ii // blk) == (jj // blk) if n != blk else None

    def m(c):
        return c if same is None else (c & same)

    return {
        False: (m(ii >= jj), m(ii > jj)),
        True: (m(ii <= jj), m(ii < jj)),
        "eye": ii == jj,
    }


def _seg_cumsum(x, pos, reverse):
    n = x.shape[0]
    sh = 1
    while sh < CHUNK:
        if reverse:
            x = x + jnp.where(pos < CHUNK - sh, pltpu.roll(x, n - sh, 0), 0.0)
        else:
            x = x + jnp.where(pos >= sh, pltpu.roll(x, sh, 0), 0.0)
        sh *= 2
    return x


def _dn_body(q_ref, k_ref, v_ref, gate_ref, ab_ref, cq_ref, ck_ref, cv_ref, alog_ref, dt_ref, gain_ref,
             y_ref, qs_ref, ks_ref, vs_ref, kt_ref, bg_ref, gt_ref, of_ref, ob_ref,
             u_ref, w_ref, qg_ref, qk_ref, kdt_ref, eg_ref, st_ref):
    s = q_ref.shape[0]
    hd = DN_HEAD_DIM
    nh = DN_HEADS_PER_STEP
    row = lax.broadcasted_iota(jnp.int32, (s, 1), 0)
    c = CHUNK
    lane32 = lax.broadcasted_iota(jnp.int32, (1, 4 * DN_HEADS), 1)
    lane128 = lax.broadcasted_iota(jnp.int32, (1, LANES), 1)
    pos_t = lax.broadcasted_iota(jnp.int32, (1, s), 1) % c

    def conv_silu(x_ref, cw_ref, lanes):
        x = x_ref[:, lanes]
        w = cw_ref[:, lanes]
        y = _shift_down(x, row) * w[0:1] + x * w[1:2] + _shift_up(x, row) * w[2:3]
        return y * _sigmoid(y)

    for hh in range(nh):
        h = pl.program_id(1) * nh + hh
        lanes = slice(hh * hd, (hh + 1) * hd)
        q = conv_silu(q_ref, cq_ref, lanes)
        q = q * lax.rsqrt(jnp.sum(q * q, axis=-1, keepdims=True) + L2_EPS) * (hd ** -0.5)
        qs_ref[hh] = q
        k = conv_silu(k_ref, ck_ref, lanes)
        k = k * lax.rsqrt(jnp.sum(k * k, axis=-1, keepdims=True) + L2_EPS)
        ks_ref[hh] = k
        kt_ref[hh] = k.T
        vs_ref[hh] = conv_silu(v_ref, cv_ref, lanes)

        def decay_row(d, h=h):
            idx = (2 + d) * DN_HEADS + h
            a_log = jnp.sum(jnp.where(lane32 == idx, alog_ref[...], 0.0), axis=-1, keepdims=True)
            dt = jnp.sum(jnp.where(lane32 == idx, dt_ref[...], 0.0), axis=-1, keepdims=True)
            g = -jnp.exp(a_log) * _softplus(ab_ref[pl.ds(idx, 1), :] + dt)
            sh = 1
            while sh < c:
                if d:
                    g = g + jnp.where(pos_t < c - sh, pltpu.roll(g, s - sh, 1), 0.0)
                else:
                    g = g + jnp.where(pos_t >= sh, pltpu.roll(g, sh, 1), 0.0)
                sh *= 2
            return g

        rows = [_sigmoid(ab_ref[pl.ds(h, 1), :]), _sigmoid(ab_ref[pl.ds(DN_HEADS + h, 1), :]),
                decay_row(0), decay_row(1)]
        gt = jnp.concatenate(rows + [jnp.zeros((LANES - len(rows), s), F32)], axis=0)
        gt_ref[hh] = gt[0:SUBLANES, :]
        bg_ref[hh] = gt.T
        for d in range(2):
            st_ref[hh, d] = jnp.zeros((hd, hd), F32)

    masks = _tri_masks(PAIR, c)
    merge = {rev: _merge_masks(PAIR, rev) for rev in (False, True)}
    eye_f = jnp.where(masks["eye"], 1.0, 0.0)
    npair = s // PAIR
    group = min(DN_PREP_PAIRS, npair)

    def prep(hh, pidx, d):
        reverse = d == 1
        rows = pl.ds(pl.multiple_of(pidx * PAIR, PAIR), PAIR)
        qp, kp, vp = qs_ref[hh, rows, :], ks_ref[hh, rows, :], vs_ref[hh, rows, :]
        bgp = bg_ref[hh, rows, :]
        beta = bgp[:, d:d + 1]
        gcol = bgp[:, 2 + d:3 + d]
        grow = gt_ref[hh, 2 + d:3 + d, rows]
        incl, strict = masks[reverse]
        decay = jnp.where(incl, jnp.exp(jnp.where(incl, gcol - grow, 0.0)), 0.0)
        kb = kp * beta
        kq = _dot1(jnp.concatenate([kb, qp], axis=0), kp, _NT)
        yield
        qk_ref[hh, d, rows, :] = jnp.where(incl, kq[PAIR:] * decay, 0.0).astype(BF16)
        t_inv = []
        yield from _tri_inverse(jnp.where(strict, kq[:PAIR] * decay, 0.0), eye_f, merge[reverse], t_inv)
        egc = jnp.exp(gcol)
        uw = _dot1(t_inv[0], jnp.concatenate([vp * beta, kb * egc], axis=1))
        yield
        u_ref[hh, d, rows, :] = uw[:, :hd]
        w_ref[hh, d, rows, :] = uw[:, hd:].astype(BF16)
        qg_ref[hh, d, rows, :] = (qp * egc).astype(BF16)
        last0, last1 = (0, c) if reverse else (c - 1, PAIR - 1)
        glast = jnp.where(lane128 < c, grow[:, last0:last0 + 1], grow[:, last1:last1 + 1])
        kdt_ref[hh, d, :, rows] = (kt_ref[hh, :, rows] * jnp.exp(glast - grow)).astype(BF16)
        eg_ref[hh, d, :, rows] = jnp.broadcast_to(jnp.exp(glast), (SUBLANES, PAIR))

    def seq(hh, pidx, d):
        pair_rows = pl.ds(pl.multiple_of(pidx * PAIR, PAIR), PAIR)
        kdt = kdt_ref[hh, d, :, pair_rows]
        eg = eg_ref[hh, d, 0:1, pair_rows]
        for half in ((1, 0) if d else (0, 1)):
            r0 = half * c
            rows = pl.ds(pl.multiple_of(pidx * PAIR + r0, c), c)
            state = st_ref[hh, d]
            ws = _dot1(jnp.concatenate([w_ref[hh, d, rows, :], qg_ref[hh, d, rows, :]], axis=0), state)
            yield
            v_new = u_ref[hh, d, rows, :] - ws[:c]
            o = ws[c:] + _dot1(qk_ref[hh, d, rows, r0:r0 + c], v_new)
            if d:
                ob_ref[hh, rows, :] = o
            else:
                of_ref[hh, rows, :] = o
            st_ref[hh, d] = state * eg[:, r0:r0 + 1] + _dot1(kdt[:, r0:r0 + c], v_new)
            yield

    ngroups = npair // group

    def pair_of(g, j, d):
        return npair - 1 - (g * group + j) if d else g * group + j

    def prep_chains(g):
        return [prep(hh, pair_of(g, j, d), d) for j in range(group) for hh in range(nh) for d in range(2)]

    def seq_chain(g, hh, d):
        for j in range(group):
            yield from seq(hh, pair_of(g, j, d), d)

    def seq_chains(g):
        return [seq_chain(g, hh, d) for hh in range(nh) for d in range(2)]

    def step(g, _):
        _interleave(prep_chains(g + 1) + seq_chains(g))
        return 0

    _interleave(prep_chains(0))
    lax.fori_loop(0, ngroups - 1, step, 0)
    _interleave(seq_chains(ngroups - 1))

    for hh in range(nh):
        lanes = slice(hh * hd, (hh + 1) * hd)
        o = of_ref[hh] + ob_ref[hh]
        o = o * lax.rsqrt(jnp.mean(o * o, axis=-1, keepdims=True) + RMS_EPS) * gain_ref[...]
        gate = gate_ref[:, lanes]
        y_ref[:, lanes] = (o * (gate * _sigmoid(gate))).astype(y_ref.dtype)


def deltanet(p, ab, conv_t, alog32, dt32, gain, batch):
    t = p.shape[0]
    s = t // batch
    assert s % PAIR == 0 and (s // PAIR) % min(DN_PREP_PAIRS, s // PAIR) == 0
    hd = DN_HEAD_DIM
    nh = DN_HEADS_PER_STEP
    nsteps = DN_HEADS // nh
    col = lambda off: pl.BlockSpec((s, nh * hd), lambda b, h: (b, off * nsteps + h))
    cw = lambda off: pl.BlockSpec((3, nh * hd), lambda b, h: (0, off * nsteps + h))
    gate_col = pl.BlockSpec((s, nh * hd), lambda b, h: (b, 3 * nsteps + h), pipeline_mode=pl.Buffered(1))
    small = lambda n: pl.BlockSpec((1, n), lambda b, h: (0, 0))
    tok = pltpu.VMEM((nh, s, hd), F32)
    per_dir = pltpu.VMEM((nh, 2, s, hd), BF16)
    return pl.pallas_call(
        _dn_body,
        grid=(batch, nsteps),
        in_specs=[col(0), col(1), col(2), gate_col,
                  pl.BlockSpec((4 * DN_HEADS, s), lambda b, h: (0, b)),
                  cw(0), cw(1), cw(2),
                  small(4 * DN_HEADS), small(4 * DN_HEADS), small(hd)],
        out_specs=pl.BlockSpec((s, nh * hd), lambda b, h: (b, h)),
        out_shape=jax.ShapeDtypeStruct((t, DN_WIDTH), BF16),
        scratch_shapes=[tok, tok, tok, pltpu.VMEM((nh, hd, s), F32), tok, pltpu.VMEM((nh, SUBLANES, s), F32), tok, tok,
                        pltpu.VMEM((nh, 2, s, hd), F32), per_dir, per_dir, per_dir,
                        pltpu.VMEM((nh, 2, hd, s), BF16), pltpu.VMEM((nh, 2, SUBLANES, s), F32),
                        pltpu.VMEM((nh, 2, hd, hd), F32)],
        compiler_params=_params("parallel", "parallel"),
        name="deltanet",
    )(p, p, p, p, ab, conv_t, conv_t, conv_t, alog32, dt32, gain.reshape(1, hd))


def _sconv_body(b_ref, c_ref, u_ref, w_ref, y_ref):
    s = b_ref.shape[0]
    row = lax.broadcasted_iota(jnp.int32, (s, 1), 0)
    cu = c_ref[...] * u_ref[...]
    w = w_ref[...]
    y = _shift_down(cu, row) * w[0:1] + cu * w[1:2] + _shift_up(cu, row) * w[2:3]
    y_ref[...] = (b_ref[...] * y).astype(y_ref.dtype)


def short_conv(p, conv_t, col0, batch, tc=256):
    t = p.shape[0]
    s = t // batch
    nct = SC_WIDTH // tc
    base = col0 // tc
    col = lambda off: pl.BlockSpec((s, tc), lambda b, c: (b, base + off * nct + c))
    return pl.pallas_call(
        _sconv_body,
        grid=(batch, nct),
        in_specs=[col(0), col(1), col(2), pl.BlockSpec((3, tc), lambda b, c: (0, c))],
        out_specs=pl.BlockSpec((s, tc), lambda b, c: (b, c)),
        out_shape=jax.ShapeDtypeStruct((t, SC_WIDTH), BF16),
        compiler_params=_params("parallel", "parallel"),
        name="short_conv",
    )(p, p, p, conv_t)


def _dsa_body(*refs, seq):
    q_refs = refs[0:3]
    k_refs = refs[3:6]
    v_refs = refs[6:9]
    qn_ref, kn_ref, bias_ref, y_ref, qs_ref, kpad_ref, vpad_ref, og_ref, lse_ref = refs[9:]
    s = seq
    qb = DSA_QBLK
    side = DSA_SIDE
    width = qb + 2 * side
    kj = lax.broadcasted_iota(jnp.int32, (1, width), 1)
    zpad = jnp.zeros((DSA_PAD, DSA_HEAD_DIM), F32)
    for ref in (kpad_ref, vpad_ref):
        ref[0:DSA_PAD, :] = zpad
        ref[DSA_PAD + s:2 * DSA_PAD + s, :] = zpad

    for gi, (_, dil) in enumerate(DSA_PATTERNS):
        sub = s // dil
        nblk = sub // qb
        q = q_refs[gi][...]
        qs_ref[...] = (q * lax.rsqrt(jnp.mean(q * q, axis=-1, keepdims=True) + RMS_EPS) * qn_ref[...]
                       * (DSA_HEAD_DIM ** -0.5))
        k = k_refs[gi][...]
        kpad_ref[DSA_PAD:DSA_PAD + s, :] = k * lax.rsqrt(jnp.mean(k * k, axis=-1, keepdims=True) + RMS_EPS) * kn_ref[...]
        vpad_ref[DSA_PAD:DSA_PAD + s, :] = v_refs[gi][...]
        bias = bias_ref[0, gi]

        def block(t, gi=gi, dil=dil, sub=sub, nblk=nblk, bias=bias):
            r = t // nblk
            n = t % nblk
            rows = pl.ds(r + n * (qb * dil), qb, stride=dil)
            win = pl.ds(DSA_PAD + r + (n * qb - side) * dil, width, stride=dil)
            logits = _dot1(qs_ref[rows, :], kpad_ref[win, :], _NT) + bias
            yield
            pos = n * qb - side + kj
            logits = jnp.where((pos >= 0) & (pos < sub), logits, NEG_INF)
            m = jnp.max(logits, axis=-1, keepdims=True)
            yield
            p = jnp.exp(logits - m)
            ssum = jnp.sum(p, axis=-1, keepdims=True)
            o = _dot1(p, vpad_ref[win, :])
            yield
            og_ref[gi, rows, :] = o / ssum
            lse_ref[gi, rows, :] = jnp.broadcast_to(m + jnp.log(ssum), (qb, DSA_HEAD_DIM))

        nblocks = dil * nblk

        def blocks_step(i, _, block=block):
            _interleave(block(i * DSA_INTERLEAVE + u) for u in range(DSA_INTERLEAVE))
            return 0

        lax.fori_loop(0, nblocks // DSA_INTERLEAVE, blocks_step, 0)

    lse = [lse_ref[gi] for gi in range(DSA_GROUPS)]
    mx = jnp.maximum(jnp.maximum(lse[0], lse[1]), lse[2])
    ws = [jnp.exp(l - mx) for l in lse]
    num = ws[0] * og_ref[0] + ws[1] * og_ref[1] + ws[2] * og_ref[2]
    y_ref[...] = (num / (ws[0] + ws[1] + ws[2])).astype(y_ref.dtype)


def dilated_attention(p, qn, kn, bias_tab, batch):
    t = p.shape[0]
    s = t // batch
    assert (s // DSA_QBLK) % DSA_INTERLEAVE == 0
    hd = DSA_HEAD_DIM
    nh = DSA_HEADS

    def col(part, gi):
        return pl.BlockSpec((s, hd), lambda b, j: (b, part * nh + gi * DSA_HPG + j))

    in_specs = [col(part, gi) for part in range(3) for gi in range(DSA_GROUPS)]
    in_specs += [pl.BlockSpec((1, hd), lambda b, j: (0, 0)), pl.BlockSpec((1, hd), lambda b, j: (0, 0)),
                 pl.BlockSpec((1, DSA_GROUPS, DSA_QBLK, DSA_QBLK + 2 * DSA_SIDE), lambda b, j: (j, 0, 0, 0))]
    big = pltpu.VMEM((s, hd), F32)
    pad = pltpu.VMEM((s + 2 * DSA_PAD, hd), F32)
    grp = pltpu.VMEM((DSA_GROUPS, s, hd), F32)
    return pl.pallas_call(
        functools.partial(_dsa_body, seq=s),
        grid=(batch, DSA_HPG),
        in_specs=in_specs,
        out_specs=pl.BlockSpec((s, hd), lambda b, j: (b, j)),
        out_shape=jax.ShapeDtypeStruct((t, DSA_HPG * hd), BF16),
        scratch_shapes=[big, pad, pad, grp, grp],
        compiler_params=_params("parallel", "parallel"),
        name="dilated_attention",
    )(*([p] * 9), qn.reshape(1, hd), kn.reshape(1, hd), bias_tab)


def _t5_bucket(rel):
    half = REL_BUCKETS // 2
    max_exact = half // 2
    n = np.abs(rel)
    scaled = (np.log(np.maximum(n, max_exact).astype(np.float32) / np.float32(max_exact))
              / np.float32(math.log(REL_MAX_DIST / max_exact)))
    large = np.minimum(max_exact + (scaled * np.float32(half - max_exact)).astype(np.int32), half - 1)
    return np.where(rel > 0, half, 0) + np.where(n < max_exact, n, large)


def _dsa_bias_table(rel_bias):
    width = DSA_QBLK + 2 * DSA_SIDE
    tabs = []
    for gi, (_, dil) in enumerate(DSA_PATTERNS):
        offs = np.arange(-DSA_SIDE, DSA_SIDE + 1, dtype=np.int32) * dil
        band = rel_bias[_t5_bucket(offs)][:, gi * DSA_HPG:(gi + 1) * DSA_HPG].astype(F32)
        fill = jnp.full((DSA_QBLK - 1, DSA_HPG), NEG_INF, F32)
        line = jnp.concatenate([fill, band, fill], axis=0)
        tabs.append(jnp.stack([line[DSA_QBLK - 1 - q:DSA_QBLK - 1 - q + width] for q in range(DSA_QBLK)]))
    return jnp.transpose(jnp.stack(tabs), (3, 0, 1, 2))


def _head_block_diag():
    ii = lax.broadcasted_iota(jnp.int32, (LANES, LANES), 0)
    jj = lax.broadcasted_iota(jnp.int32, (LANES, LANES), 1)
    return (ii // RW_HEAD_DIM) == (jj // RW_HEAD_DIM)


def _rw_prep_body(r_ref, k_ref, v_ref, lo_ref, mur_ref, muk_ref, muv_ref, mulo_ref, w0_ref, w2_ref, a0_ref,
                  a2_ref, g2_ref, kk_ref, ka_ref, rk_ref,
                  ro_ref, vo_ref, kko_ref, bon_ref, gate_ref, lwf_ref, lwb_ref, kdf_ref, kdb_ref, bbf_ref, bbb_ref):
    s = r_ref.shape[0]
    row = lax.broadcasted_iota(jnp.int32, (s, 1), 0)

    def mix(t, mu):
        return t + mu * (0.5 * (_shift_down(t, row) + _shift_up(t, row)) - t)

    r = mix(r_ref[...], mur_ref[...])
    kr = mix(k_ref[...], muk_ref[...])
    v = mix(v_ref[...], muv_ref[...])
    lo = mix(lo_ref[...], mulo_ref[...])
    bd = jnp.where(_head_block_diag(), 1.0, 0.0)

    def head_sum(t):
        return _dot_sum(t, bd)

    kk = kr * kk_ref[...]
    kk = kk * lax.rsqrt(head_sum(kk * kk) + L2_EPS)
    gd = lo[:, 4 * RW_LORA:]
    gate_ref[...] = _dot1(_sigmoid(gd), g2_ref[...])
    ro_ref[...] = r
    vo_ref[...] = v
    kko_ref[...] = kk
    bonus = jnp.zeros_like(r)
    outs = ((lwf_ref, kdf_ref, bbf_ref), (lwb_ref, kdb_ref, bbb_ref))
    for d in range(2):
        wd = lo[:, d * RW_LORA:(d + 1) * RW_LORA]
        ad = lo[:, (2 + d) * RW_LORA:(3 + d) * RW_LORA]
        w_log = -_softplus(-(w0_ref[d:d + 1, :] + _dot1(jnp.tanh(wd), w2_ref[d]))) - 0.5
        a = _sigmoid(a0_ref[d:d + 1, :] + _dot1(ad, a2_ref[d]))
        kd = kr * (1.0 + (a - 1.0) * ka_ref[...])
        lw_ref, kd_ref, bb_ref = outs[d]
        lw_ref[...] = -jnp.exp(w_log)
        kd_ref[...] = kd
        bb_ref[...] = kk * a
        bonus = bonus + head_sum(r * kd * rk_ref[...]) * v
    bon_ref[...] = bonus


def rwkv_prep(p, lora_in, mu, w0, w2, a0, a2, g2, k_k, k_a, r_k, col0, batch):
    t = p.shape[0]
    s = t // batch
    nct = RW_WIDTH // LANES
    base = col0 // LANES
    nlo = RW_LORA_IN
    col = lambda off: pl.BlockSpec((s, LANES), lambda b, c: (b, base + off * nct + c))
    vec = lambda off: pl.BlockSpec((1, LANES), lambda b, c: (0, off * nct + c))
    mu_main = mu[:RW_MAIN].reshape(1, RW_MAIN)
    mu_lo = mu[RW_MAIN:].reshape(1, nlo)
    out_spec = pl.BlockSpec((s, LANES), lambda b, c: (b, c))
    n_out = 11
    return pl.pallas_call(
        _rw_prep_body,
        grid=(batch, nct),
        in_specs=[col(0), col(1), col(2),
                  pl.BlockSpec((s, nlo), lambda b, c: (b, 0)),
                  vec(0), vec(1), vec(2),
                  pl.BlockSpec((1, nlo), lambda b, c: (0, 0)),
                  pl.BlockSpec((2, LANES), lambda b, c: (0, c)),
                  pl.BlockSpec((2, RW_LORA, LANES), lambda b, c: (0, 0, c)),
                  pl.BlockSpec((2, LANES), lambda b, c: (0, c)),
                  pl.BlockSpec((2, RW_LORA, LANES), lambda b, c: (0, 0, c)),
                  pl.BlockSpec((RW_GATE_LORA, LANES), lambda b, c: (0, c)),
                  vec(0), vec(0), vec(0)],
        out_specs=[out_spec] * n_out,
        out_shape=[jax.ShapeDtypeStruct((t, RW_WIDTH), F32)] * n_out,
        compiler_params=_params("parallel", "parallel"),
        name="rwkv_prep",
    )(p, p, p, lora_in, mu_main, mu_main, mu_main, mu_lo, w0, w2, a0, a2, g2,
      k_k.reshape(1, RW_WIDTH), k_a.reshape(1, RW_WIDTH), r_k.reshape(1, RW_WIDTH))


def _rw_scan_body(r_ref, v_ref, kk_ref, bon_ref, gate_ref, lwf_ref, lwb_ref, kdf_ref, kdb_ref, bbf_ref, bbb_ref,
                  lnw_ref, lnb_ref, y_ref, cum_ref, cumt_ref, kdt_ref, bbt_ref, yf_ref, yb_ref,
                  tr_ref, tc_ref, rbk_ref, bkt_ref):
    s = r_ref.shape[0]
    c = CHUNK
    c2 = 2 * c
    row = lax.broadcasted_iota(jnp.int32, (s, 1), 0)
    pos = row % c
    lw_refs, kd_refs, bb_refs = (lwf_ref, lwb_ref), (kdf_ref, kdb_ref), (bbf_ref, bbb_ref)
    for d in range(2):
        cum = _seg_cumsum(lw_refs[d][...], pos, d == 1)
        cum_ref[d] = cum
        cumt_ref[d] = cum.T
        kdt_ref[d] = kd_refs[d][...].T
        bbt_ref[d] = bb_refs[d][...].T

    masks = _tri_masks(c2, c)
    merge = {rev: _merge_masks(c2, rev) for rev in (False, True)}
    eye2_f = jnp.where(masks["eye"], 1.0, 0.0)
    head0 = lax.broadcasted_iota(jnp.int32, (1, LANES), 1) < RW_HEAD_DIM
    head_bd = _head_block_diag()
    npair = s // PAIR
    group = min(RW_PREP_PAIRS, npair)

    def stack(x):
        return jnp.concatenate([jnp.where(head0, x, 0.0), jnp.where(head0, 0.0, x)], axis=0)

    def prep(pidx, d, half):
        reverse = d == 1
        incl2, strict2 = masks[reverse]
        cidx = 2 * pidx + half
        cols = pl.ds(pl.multiple_of(pidx * PAIR, PAIR), PAIR)
        rows = pl.ds(pl.multiple_of(cidx * c, c), c)
        hs = slice(half * c, (half + 1) * c)
        r, v, kk = r_ref[rows, :], v_ref[rows, :], kk_ref[rows, :]
        lw, kd, bb = lw_refs[d][rows, :], kd_refs[d][rows, :], bb_refs[d][rows, :]
        cum = cum_ref[d, rows, :]
        e_neg = jnp.exp(-cum)
        a_s = stack(-kk * jnp.exp(cum - lw))
        r_s = stack(r * jnp.exp(cum))
        ar = jnp.concatenate([a_s, r_s], axis=0)
        bk = jnp.concatenate([stack(bb * e_neg), stack(kd * e_neg)], axis=0)
        g = _dot1(ar, bk, _NT)
        yield
        rbk_ref[d, cidx] = jnp.concatenate([jnp.where(incl2, g[c2:, :c2], 0.0),
                                            jnp.where(incl2, g[c2:, c2:], 0.0)], axis=1).astype(BF16)
        av = _dot1(jnp.where(strict2, g[:c2, c2:], 0.0), stack(v))
        yield
        t_inv = []
        yield from _tri_inverse(-jnp.where(strict2, g[:c2, :c2], 0.0), eye2_f, merge[reverse], t_inv)
        tt = _dot1(t_inv[0], jnp.concatenate([a_s, av], axis=1))
        yield
        tr_ref[d, cidx] = jnp.concatenate([tt[:, :LANES], r_s], axis=0).astype(BF16)
        tc_ref[d, cidx] = tt[:, LANES:].astype(BF16)
        cum_t = cumt_ref[d, :, cols][:, hs]
        last = 0 if reverse else c - 1
        e_out_t = jnp.exp(cum_t[:, last:last + 1] - cum_t)
        bkt_ref[d, cidx] = jnp.concatenate([bbt_ref[d, :, cols][:, hs] * e_out_t,
                                            kdt_ref[d, :, cols][:, hs] * e_out_t], axis=1).astype(BF16)

    def seq(states, pidx, d):
        cum_t2 = cumt_ref[d, :, pl.ds(pl.multiple_of(pidx * PAIR, PAIR), PAIR)]
        for half in ((1, 0) if d else (0, 1)):
            cidx = 2 * pidx + half
            rows = pl.ds(pl.multiple_of(cidx * c, c), c)
            v = v_ref[rows, :]
            x = _dot1(tr_ref[d, cidx], states[d])
            yield
            ps = x[:c2] + tc_ref[d, cidx].astype(F32)
            os_ = x[c2:] + _dot1(rbk_ref[d, cidx], jnp.concatenate([ps, stack(v)], axis=0))
            y = os_[:c] + os_[c:]
            if d:
                yb_ref[rows, :] = y
            else:
                yf_ref[rows, :] = y
            pv = jnp.concatenate([ps[:c] + ps[c:], v], axis=0)
            last = half * c + (0 if d else c - 1)
            e_tot = jnp.exp(cum_t2[:, last:last + 1])
            states[d] = states[d] * e_tot + jnp.where(head_bd, _dot1(bkt_ref[d, cidx], pv), 0.0)
            yield

    ngroups = npair // group

    def prep_chains(g):
        return [prep(npair - 1 - (g * group + j) if d else g * group + j, d, half)
                for j in range(group) for d in range(2) for half in range(2)]

    def seq_chain(states, g, d):
        for j in range(group):
            yield from seq(states, npair - 1 - (g * group + j) if d else g * group + j, d)

    def step(g, carry):
        states = list(carry)
        _interleave(prep_chains(g + 1) + [seq_chain(states, g, 0), seq_chain(states, g, 1)])
        return tuple(states)

    zero = jnp.zeros((LANES, LANES), F32)
    _interleave(prep_chains(0))
    states = list(lax.fori_loop(0, ngroups - 1, step, (zero, zero)))
    _interleave([seq_chain(states, ngroups - 1, 0), seq_chain(states, ngroups - 1, 1)])

    bd = jnp.where(head_bd, 1.0, 0.0)
    y = yf_ref[...] + yb_ref[...]
    mean = _dot_sum(y, bd) * (1.0 / RW_HEAD_DIM)
    yc = y - mean
    var = _dot_sum(yc * yc, bd) * (1.0 / RW_HEAD_DIM)
    yn = yc * lax.rsqrt(var + RW_GN_EPS) * lnw_ref[...] + lnb_ref[...]
    y_ref[...] = ((yn + bon_ref[...]) * gate_ref[...]).astype(y_ref.dtype)


def rwkv_scan(prep, ln_w, ln_b, batch):
    t = prep[0].shape[0]
    s = t // batch
    assert s % PAIR == 0 and (s // PAIR) % min(RW_PREP_PAIRS, s // PAIR) == 0
    nct = RW_WIDTH // LANES
    nchunk = s // CHUNK
    blk = pl.BlockSpec((s, LANES), lambda b, c: (b, c))
    vec = pl.BlockSpec((1, LANES), lambda b, c: (0, c))
    tsp = pltpu.VMEM((2, LANES, s), F32)
    big = pltpu.VMEM((s, LANES), F32)
    per_chunk = lambda rows, cols: pltpu.VMEM((2, nchunk, rows, cols), BF16)
    return pl.pallas_call(
        _rw_scan_body,
        grid=(batch, nct),
        in_specs=[blk] * 11 + [vec, vec],
        out_specs=blk,
        out_shape=jax.ShapeDtypeStruct((t, RW_WIDTH), BF16),
        scratch_shapes=[pltpu.VMEM((2, s, LANES), F32), tsp, tsp, tsp, big, big,
                        per_chunk(2 * PAIR, LANES), per_chunk(PAIR, LANES), per_chunk(PAIR, 2 * LANES),
                        per_chunk(PAIR, LANES)],
        compiler_params=_params("parallel", "parallel"),
        name="rwkv_scan",
    )(*prep, ln_w.reshape(1, RW_WIDTH), ln_b.reshape(1, RW_WIDTH))


def even_mixer(x, g, w_in, w_sc, w_ab, w_out, layer, conv_qkv, a_log, dt_bias, out_gain, conv_sc, batch):
    p, ab = norm_mm(x, g, w_in, layer, ncols=4 * DN_WIDTH, w2=w_sc, wp_t=w_ab.T)
    zeros = jnp.zeros((2 * DN_HEADS,), F32)
    alog32 = jnp.concatenate([zeros, a_log.reshape(-1)]).reshape(1, -1)
    dt32 = jnp.concatenate([zeros, dt_bias.reshape(-1)]).reshape(1, -1)
    y_dn = deltanet(p, ab, conv_qkv.T, alog32, dt32, out_gain, batch)
    y_sc = short_conv(p, conv_sc.T, 4 * DN_WIDTH, batch)
    return mm_res(y_dn, w_out, layer, x, a2=y_sc, tn=D_MODEL)


def odd_mixer(x, g, w_in, w_lora, w_out, layer, qn, kn, bias_tab, mu, w0, w2, a0, a2, g2, k_k, k_a, r_k, ln_w, ln_b,
              batch):
    p, lora_in = norm_mm(x, g, w_in, layer, ncols=3 * DSA_QKV + RW_MAIN, wp=w_lora)
    y_c = dilated_attention(p, qn, kn, bias_tab, batch)
    prep = rwkv_prep(p, lora_in, mu, w0, w2, a0, a2, g2, k_k, k_a, r_k, 3 * DSA_QKV, batch)
    y_d = rwkv_scan(prep, ln_w, ln_b, batch)
    return mm_res(y_c, w_out, layer, x, a2=y_d, tn=D_MODEL)


def kernel(x, mem, rel_bias, norm_mix, norm_xattn, norm_mem, norm_ffn, xa_wq, xa_wk, xa_wv, xa_wo, xa_qn, xa_kn, ffn_w1, ffn_w2, ev_w_in, ev_w_out, dn_conv, dn_a_log, dn_dt_bias, dn_norm, sc_conv, od_w_in, od_w_out, ca_qn, ca_kn, rw_mu, rw_w0, rw_w2, rw_a0, rw_a2, rw_g2, rw_k_k, rw_k_a, rw_r_k, rw_ln_w, rw_ln_b):
    batch, seq, d = x.shape
    n_mem = mem.shape[1]
    xf = x.reshape(batch * seq, d)
    memf = mem.reshape(batch * n_mem, d)
    bias_tab = _dsa_bias_table(rel_bias)
    c_ab = 4 * DN_WIDTH
    c_sc = c_ab + 4 * DN_HEADS
    c_lo = 3 * DSA_QKV + RW_MAIN
    ev_in, od_in = ev_w_in.astype(BF16), od_w_in.astype(BF16)
    ev_sc = ev_in[:, :, c_sc:]
    ev_out, od_out = ev_w_out.astype(BF16), od_w_out.astype(BF16)
    w_q, w_o = xa_wq.astype(BF16), xa_wo.astype(BF16)
    w_kv = jnp.concatenate([xa_wk, xa_wv], axis=2).astype(BF16)
    for layer in range(DEPTH):
        i = layer // 2
        if layer % 2 == 0:
            xf = even_mixer(xf, norm_mix[layer], ev_in, ev_sc, ev_w_in[i, :, c_ab:c_sc], ev_out, i, dn_conv[i],
                            dn_a_log[i], dn_dt_bias[i], dn_norm[i], sc_conv[i], batch)
        else:
            xf = odd_mixer(xf, norm_mix[layer], od_in, od_w_in[i, :, c_lo:], od_out, i, ca_qn[i], ca_kn[i], bias_tab,
                           rw_mu[i], rw_w0[i], rw_w2[i], rw_a0[i], rw_a2[i], rw_g2[i], rw_k_k[i], rw_k_a[i], rw_r_k[i],
                           rw_ln_w[i], rw_ln_b[i], batch)
        kv = norm_mm(memf, norm_mem[layer], w_kv, layer).reshape(batch, n_mem, 2 * XA_WIDTH)
        xf = xattn(xf, norm_xattn[layer], w_q, kv, w_o, layer, xa_qn[layer], xa_kn[layer], batch)
        h1 = norm_mm(xf, norm_ffn[layer], ffn_w1, layer, act="relu2", out_dtype=BF16)
        xf = mm_res(h1, ffn_w2, layer, xf)
    return xf.reshape(batch, seq, d)
```

```python
import functools
import math

import jax
import jax.numpy as jnp
import numpy as np
from jax import lax
from jax.experimental import pallas as pl
from jax.experimental.pallas import tpu as pltpu

F32 = jnp.float32
BF16 = jnp.bfloat16

D_MODEL = 2048
DEPTH = 4
RMS_EPS = 1e-6
L2_EPS = 1e-6

DN_HEADS = 8
DN_HEAD_DIM = 128
DN_WIDTH = DN_HEADS * DN_HEAD_DIM
SC_WIDTH = D_MODEL - DN_WIDTH
CHUNK = 64
PAIR = 2 * CHUNK
DN_PREP_PAIRS = 4
DN_HEADS_PER_STEP = 2
RW_PREP_PAIRS = 4

DSA_PATTERNS = ((128, 1), (512, 4), (2048, 16))
DSA_GROUPS = len(DSA_PATTERNS)
DSA_HPG = 4
DSA_HEAD_DIM = 128
DSA_HEADS = DSA_GROUPS * DSA_HPG
DSA_QKV = DSA_HEADS * DSA_HEAD_DIM
DSA_SIDE = 64
DSA_QBLK = 128
DSA_PAD = DSA_SIDE * max(d for _, d in DSA_PATTERNS)
DSA_INTERLEAVE = 8
REL_BUCKETS = 32
REL_MAX_DIST = 1024
NEG_INF = -1e30

RW_HEADS = 8
RW_HEAD_DIM = 64
RW_WIDTH = RW_HEADS * RW_HEAD_DIM
RW_LORA = 64
RW_GATE_LORA = 128
RW_MAIN = 3 * RW_WIDTH
RW_LORA_IN = 4 * RW_LORA + RW_GATE_LORA
RW_GN_EPS = 64e-5

XA_HEADS = 4
XA_HEAD_DIM = 128
XA_WIDTH = XA_HEADS * XA_HEAD_DIM

LANES = 128
SUBLANES = 8
NORM_ROWS = 256
VMEM_LIMIT_BYTES = 58 * 1024 * 1024


def _params(*sem):
    return pltpu.CompilerParams(dimension_semantics=sem, vmem_limit_bytes=VMEM_LIMIT_BYTES)


_NN = (((1,), (0,)), ((), ()))
_NT = (((1,), (1,)), ((), ()))


def _dot1(a, b, dims=_NN):
    return lax.dot_general(a.astype(BF16), b.astype(BF16), dims, preferred_element_type=F32)


def _dot_sum(a, b01):
    b = b01.astype(BF16)
    hi = a.astype(BF16)
    lo = (a - hi.astype(F32)).astype(BF16)
    dg = functools.partial(lax.dot_general, dimension_numbers=_NN, preferred_element_type=F32)
    return dg(hi, b) + dg(lo, b)


def _interleave(chains):
    chains = list(chains)
    while chains:
        alive = []
        for ch in chains:
            try:
                next(ch)
                alive.append(ch)
            except StopIteration:
                pass
        chains = alive


def _merge_masks(n, reverse):
    ii = lax.broadcasted_iota(jnp.int32, (n, n), 0)
    jj = lax.broadcasted_iota(jnp.int32, (n, n), 1)
    tri = (ii < jj) if reverse else (ii > jj)
    out = []
    size = 1
    while size < CHUNK:
        out.append(tri & ((ii // (2 * size)) == (jj // (2 * size))) & ((ii // size) != (jj // size)))
        size *= 2
    return out


def _tri_inverse(a, eye_f, level_masks, out):
    t = eye_f - jnp.where(level_masks[0], a, 0.0)
    for m in level_masks[1:]:
        x = _dot1(jnp.where(m, a, 0.0), t)
        yield
        t = t - _dot1(t, x)
        yield
    out.append(t)


def _sigmoid(x):
    return 1.0 / (1.0 + jnp.exp(-x))


def _softplus(x):
    return jnp.maximum(x, 0.0) + jnp.log1p(jnp.exp(-jnp.abs(x)))


def _shift_down(x, row):
    return jnp.where(row == 0, 0.0, pltpu.roll(x, 1, 0))


def _shift_up(x, row):
    n = x.shape[0]
    return jnp.where(row == n - 1, 0.0, pltpu.roll(x, n - 1, 0))


def _norm_mm_body(x_ref, g_ref, w_ref, *rest, act, extra, extra_t, n_first):
    if n_first is not None:
        w2_ref, rest = rest[0], rest[1:]
    if extra:
        wp_ref, o_ref, op_ref, xn_ref = rest
    else:
        o_ref, xn_ref = rest

    @pl.when(pl.program_id(1) == 0)
    def _():
        def rows_step(r, _):
            rows = pl.ds(pl.multiple_of(r * NORM_ROWS, NORM_ROWS), NORM_ROWS)
            x = x_ref[rows, :]
            xn = x * lax.rsqrt(jnp.mean(x * x, axis=-1, keepdims=True) + RMS_EPS) * g_ref[...]
            xn_ref[rows, :] = xn.astype(BF16)
            if extra and extra_t:
                op_ref[:, rows] = _dot1(wp_ref[...], xn, _NT)
            elif extra:
                op_ref[rows, :] = _dot1(xn, wp_ref[...])
            return 0

        lax.fori_loop(0, x_ref.shape[0] // NORM_ROWS, rows_step, 0)

    def project(wt_ref):
        acc = jnp.dot(xn_ref[...], wt_ref[...].astype(BF16), preferred_element_type=F32)
        if act == "relu2":
            acc = jnp.square(jnp.maximum(acc, 0.0))
        o_ref[...] = acc.astype(o_ref.dtype)

    if n_first is None:
        project(w_ref)
    else:
        pl.when(pl.program_id(1) < n_first)(lambda: project(w_ref))
        pl.when(pl.program_id(1) >= n_first)(lambda: project(w2_ref))


def norm_mm(x, g, w, layer, ncols=None, w2=None, wp=None, wp_t=None, act=None, out_dtype=F32, tm=1024, tn=1024):
    t, d = x.shape
    n1 = w.shape[2] if ncols is None else ncols
    n = n1 + (0 if w2 is None else w2.shape[2])
    tm = min(tm, t)
    tn = min(tn, n1)
    assert t % tm == 0 and n1 % tn == 0 and n % tn == 0 and tm % NORM_ROWS == 0
    extra = wp is not None or wp_t is not None
    n_first = None if w2 is None else n1 // tn
    in_specs = [
        pl.BlockSpec((tm, d), lambda i, j: (i, 0)),
        pl.BlockSpec((1, d), lambda i, j: (0, 0)),
    ]
    args = [x, g.reshape(1, d), w]
    if w2 is None:
        in_specs.append(pl.BlockSpec((None, d, tn), lambda i, j: (layer, 0, j)))
    else:
        in_specs.append(pl.BlockSpec((None, d, tn), lambda i, j: (layer, 0, jnp.minimum(j, n_first - 1))))
        in_specs.append(pl.BlockSpec((None, d, tn), lambda i, j: (layer, 0, jnp.maximum(j - n_first, 0))))
        args.append(w2)
    out_specs = pl.BlockSpec((tm, tn), lambda i, j: (i, j))
    out_shape = jax.ShapeDtypeStruct((t, n), out_dtype)
    if wp is not None:
        npc = wp.shape[1]
        in_specs.append(pl.BlockSpec((d, npc), lambda i, j: (0, 0)))
        out_specs = [out_specs, pl.BlockSpec((tm, npc), lambda i, j: (i, 0))]
        out_shape = [out_shape, jax.ShapeDtypeStruct((t, npc), F32)]
        args.append(wp)
    elif wp_t is not None:
        npc = wp_t.shape[0]
        in_specs.append(pl.BlockSpec((npc, d), lambda i, j: (0, 0)))
        out_specs = [out_specs, pl.BlockSpec((npc, tm), lambda i, j: (0, i))]
        out_shape = [out_shape, jax.ShapeDtypeStruct((npc, t), F32)]
        args.append(wp_t)
    return pl.pallas_call(
        functools.partial(_norm_mm_body, act=act, extra=extra, extra_t=wp_t is not None, n_first=n_first),
        grid=(t // tm, n // tn),
        in_specs=in_specs,
        out_specs=out_specs,
        out_shape=out_shape,
        scratch_shapes=[pltpu.VMEM((tm, d), BF16)],
        compiler_params=_params("parallel", "arbitrary"),
        name="norm_mm",
    )(*args)


def _mm_res_body(a_ref, *rest, n_first):
    if n_first is not None:
        a2_ref, rest = rest[0], rest[1:]
    w_ref, r_ref, o_ref = rest
    kk = pl.program_id(2)

    @pl.when(kk == 0)
    def _():
        o_ref[...] = r_ref[...]

    def accumulate(lhs_ref):
        o_ref[...] += jnp.dot(lhs_ref[...], w_ref[...].astype(BF16), preferred_element_type=F32)

    if n_first is None:
        accumulate(a_ref)
    else:
        pl.when(kk < n_first)(lambda: accumulate(a_ref))
        pl.when(kk >= n_first)(lambda: accumulate(a2_ref))


def mm_res(a, w, layer, res, a2=None, tm=1024, tn=1024, tk=2048):
    t, k1 = a.shape
    k = k1 + (0 if a2 is None else a2.shape[1])
    n = w.shape[2]
    tm, tn, tk = min(tm, t), min(tn, n), min(tk, k1)
    assert t % tm == 0 and n % tn == 0 and k1 % tk == 0 and k % tk == 0
    n_first = None if a2 is None else k1 // tk
    if a2 is None:
        in_specs = [pl.BlockSpec((tm, tk), lambda i, j, kk: (i, kk))]
        args = [a]
    else:
        in_specs = [pl.BlockSpec((tm, tk), lambda i, j, kk: (i, jnp.minimum(kk, n_first - 1))),
                    pl.BlockSpec((tm, tk), lambda i, j, kk: (i, jnp.maximum(kk - n_first, 0)))]
        args = [a, a2]
    in_specs += [
        pl.BlockSpec((None, tk, tn), lambda i, j, kk: (layer, kk, j)),
        pl.BlockSpec((tm, tn), lambda i, j, kk: (i, j)),
    ]
    return pl.pallas_call(
        functools.partial(_mm_res_body, n_first=n_first),
        grid=(t // tm, n // tn, k // tk),
        in_specs=in_specs,
        out_specs=pl.BlockSpec((tm, tn), lambda i, j, kk: (i, j)),
        out_shape=jax.ShapeDtypeStruct((t, n), F32),
        compiler_params=_params("parallel", "parallel", "arbitrary"),
        name="mm_res",
    )(*args, w, res)


def _xattn_body(x_ref, g_ref, wq_ref, kv_ref, wo_ref, qn_ref, kn_ref, o_ref):
    x = x_ref[...]
    xn = x * lax.rsqrt(jnp.mean(x * x, axis=-1, keepdims=True) + RMS_EPS) * g_ref[...]
    q = jnp.dot(xn.astype(BF16), wq_ref[...], preferred_element_type=F32)
    kv = kv_ref[0]
    outs = []
    for h in range(XA_HEADS):
        sl = slice(h * XA_HEAD_DIM, (h + 1) * XA_HEAD_DIM)
        qh = q[:, sl]
        qh = qh * lax.rsqrt(jnp.mean(qh * qh, axis=-1, keepdims=True) + RMS_EPS) * qn_ref[...]
        kh = kv[:, sl]
        kh = kh * lax.rsqrt(jnp.mean(kh * kh, axis=-1, keepdims=True) + RMS_EPS) * kn_ref[...]
        vh = kv[:, XA_WIDTH + h * XA_HEAD_DIM:XA_WIDTH + (h + 1) * XA_HEAD_DIM]
        logits = _dot1(qh, kh, _NT) * (XA_HEAD_DIM ** -0.5)
        m = jnp.max(logits, axis=-1, keepdims=True)
        p = jnp.exp(logits - m)
        s = jnp.sum(p, axis=-1, keepdims=True)
        outs.append(_dot1(p, vh) / s)
    o = jnp.concatenate(outs, axis=-1).astype(BF16)
    o_ref[...] = x + jnp.dot(o, wo_ref[...], preferred_element_type=F32)


def xattn(x, g, wq, kv, wo, layer, qn, kn, batch, ts=1024):
    t, d = x.shape
    s = t // batch
    ts = min(ts, s)
    nst = s // ts
    m = kv.shape[1]
    return pl.pallas_call(
        _xattn_body,
        grid=(batch, nst),
        in_specs=[
            pl.BlockSpec((ts, d), lambda b, i: (b * nst + i, 0)),
            pl.BlockSpec((1, d), lambda b, i: (0, 0)),
            pl.BlockSpec((None, d, XA_WIDTH), lambda b, i: (layer, 0, 0)),
            pl.BlockSpec((1, m, 2 * XA_WIDTH), lambda b, i: (b, 0, 0)),
            pl.BlockSpec((None, XA_WIDTH, d), lambda b, i: (layer, 0, 0)),
            pl.BlockSpec((1, XA_HEAD_DIM), lambda b, i: (0, 0)),
            pl.BlockSpec((1, XA_HEAD_DIM), lambda b, i: (0, 0)),
        ],
        out_specs=pl.BlockSpec((ts, d), lambda b, i: (b * nst + i, 0)),
        out_shape=jax.ShapeDtypeStruct((t, d), F32),
        compiler_params=_params("parallel", "parallel"),
        name="xattn",
    )(x, g.reshape(1, d), wq, kv, wo, qn.reshape(1, -1), kn.reshape(1, -1))


def _tri_masks(n, blk):
    ii = lax.broadcasted_iota(jnp.int32, (n, n), 0)
    jj = lax.broadcasted_iota(jnp.int32, (n, n), 1)
    same = (ii // blk) == (jj // blk) if n != blk else None

    def m(c):
        return c if same is None else (c & same)

    return {
        False: (m(ii >= jj), m(ii > jj)),
        True: (m(ii <= jj), m(ii < jj)),
        "eye": ii == jj,
    }


def _seg_cumsum(x, pos, reverse):
    n = x.shape[0]
    sh = 1
    while sh < CHUNK:
        if reverse:
            x = x + jnp.where(pos < CHUNK - sh, pltpu.roll(x, n - sh, 0), 0.0)
        else:
            x = x + jnp.where(pos >= sh, pltpu.roll(x, sh, 0), 0.0)
        sh *= 2
    return x


def _dn_body(q_ref, k_ref, v_ref, gate_ref, ab_ref, cq_ref, ck_ref, cv_ref, alog_ref, dt_ref, gain_ref,
             y_ref, qs_ref, ks_ref, vs_ref, kt_ref, bg_ref, gt_ref, of_ref, ob_ref,
             u_ref, w_ref, qg_ref, qk_ref, kdt_ref, eg_ref, st_ref):
    s = q_ref.shape[0]
    hd = DN_HEAD_DIM
    nh = DN_HEADS_PER_STEP
    row = lax.broadcasted_iota(jnp.int32, (s, 1), 0)
    c = CHUNK
    lane32 = lax.broadcasted_iota(jnp.int32, (1, 4 * DN_HEADS), 1)
    lane128 = lax.broadcasted_iota(jnp.int32, (1, LANES), 1)
    pos_t = lax.broadcasted_iota(jnp.int32, (1, s), 1) % c

    def conv_silu(x_ref, cw_ref, lanes):
        x = x_ref[:, lanes].astype(F32)
        w = cw_ref[:, lanes]
        y = _shift_down(x, row) * w[0:1] + x * w[1:2] + _shift_up(x, row) * w[2:3]
        return y * _sigmoid(y)

    for hh in range(nh):
        h = pl.program_id(1) * nh + hh
        lanes = slice(hh * hd, (hh + 1) * hd)
        q = conv_silu(q_ref, cq_ref, lanes)
        q = q * lax.rsqrt(jnp.sum(q * q, axis=-1, keepdims=True) + L2_EPS) * (hd ** -0.5)
        qs_ref[hh] = q
        k = conv_silu(k_ref, ck_ref, lanes)
        k = k * lax.rsqrt(jnp.sum(k * k, axis=-1, keepdims=True) + L2_EPS)
        ks_ref[hh] = k
        kt_ref[hh] = k.T
        vs_ref[hh] = conv_silu(v_ref, cv_ref, lanes)

        def decay_row(d, h=h):
            idx = (2 + d) * DN_HEADS + h
            a_log = jnp.sum(jnp.where(lane32 == idx, alog_ref[...], 0.0), axis=-1, keepdims=True)
            dt = jnp.sum(jnp.where(lane32 == idx, dt_ref[...], 0.0), axis=-1, keepdims=True)
            g = -jnp.exp(a_log) * _softplus(ab_ref[pl.ds(idx, 1), :] + dt)
            sh = 1
            while sh < c:
                if d:
                    g = g + jnp.where(pos_t < c - sh, pltpu.roll(g, s - sh, 1), 0.0)
                else:
                    g = g + jnp.where(pos_t >= sh, pltpu.roll(g, sh, 1), 0.0)
                sh *= 2
            return g

        rows = [_sigmoid(ab_ref[pl.ds(h, 1), :]), _sigmoid(ab_ref[pl.ds(DN_HEADS + h, 1), :]),
                decay_row(0), decay_row(1)]
        gt = jnp.concatenate(rows + [jnp.zeros((LANES - len(rows), s), F32)], axis=0)
        gt_ref[hh] = gt[0:SUBLANES, :]
        bg_ref[hh] = gt.T
        for d in range(2):
            st_ref[hh, d] = jnp.zeros((hd, hd), F32)

    masks = _tri_masks(PAIR, c)
    merge = {rev: _merge_masks(PAIR, rev) for rev in (False, True)}
    eye_f = jnp.where(masks["eye"], 1.0, 0.0)
    npair = s // PAIR
    group = min(DN_PREP_PAIRS, npair)

    def prep(hh, pidx, d):
        reverse = d == 1
        rows = pl.ds(pl.multiple_of(pidx * PAIR, PAIR), PAIR)
        qp, kp, vp = qs_ref[hh, rows, :], ks_ref[hh, rows, :], vs_ref[hh, rows, :]
        bgp = bg_ref[hh, rows, :]
        beta = bgp[:, d:d + 1]
        gcol = bgp[:, 2 + d:3 + d]
        grow = gt_ref[hh, 2 + d:3 + d, rows]
        incl, strict = masks[reverse]
        decay = jnp.where(incl, jnp.exp(jnp.where(incl, gcol - grow, 0.0)), 0.0)
        kb = kp * beta
        kq = _dot1(jnp.concatenate([kb, qp], axis=0), kp, _NT)
        yield
        qk_ref[hh, d, rows, :] = jnp.where(incl, kq[PAIR:] * decay, 0.0).astype(BF16)
        t_inv = []
        yield from _tri_inverse(jnp.where(strict, kq[:PAIR] * decay, 0.0), eye_f, merge[reverse], t_inv)
        egc = jnp.exp(gcol)
        uw = _dot1(t_inv[0], jnp.concatenate([vp * beta, kb * egc], axis=1))
        yield
        u_ref[hh, d, rows, :] = uw[:, :hd]
        w_ref[hh, d, rows, :] = uw[:, hd:].astype(BF16)
        qg_ref[hh, d, rows, :] = (qp * egc).astype(BF16)
        last0, last1 = (0, c) if reverse else (c - 1, PAIR - 1)
        glast = jnp.where(lane128 < c, grow[:, last0:last0 + 1], grow[:, last1:last1 + 1])
        kdt_ref[hh, d, :, rows] = (kt_ref[hh, :, rows] * jnp.exp(glast - grow)).astype(BF16)
        eg_ref[hh, d, :, rows] = jnp.broadcast_to(jnp.exp(glast), (SUBLANES, PAIR))

    def seq(hh, pidx, d):
        pair_rows = pl.ds(pl.multiple_of(pidx * PAIR, PAIR), PAIR)
        kdt = kdt_ref[hh, d, :, pair_rows]
        eg = eg_ref[hh, d, 0:1, pair_rows]
        for half in ((1, 0) if d else (0, 1)):
            r0 = half * c
            rows = pl.ds(pl.multiple_of(pidx * PAIR + r0, c), c)
            state = st_ref[hh, d]
            ws = _dot1(jnp.concatenate([w_ref[hh, d, rows, :], qg_ref[hh, d, rows, :]], axis=0), state)
            yield
            v_new = u_ref[hh, d, rows, :] - ws[:c]
            o = ws[c:] + _dot1(qk_ref[hh, d, rows, r0:r0 + c], v_new)
            if d:
                ob_ref[hh, rows, :] = o
            else:
                of_ref[hh, rows, :] = o
            st_ref[hh, d] = state * eg[:, r0:r0 + 1] + _dot1(kdt[:, r0:r0 + c], v_new)
            yield

    ngroups = npair // group

    def pair_of(g, j, d):
        return npair - 1 - (g * group + j) if d else g * group + j

    def prep_chains(g):
        return [prep(hh, pair_of(g, j, d), d) for j in range(group) for hh in range(nh) for d in range(2)]

    def seq_chain(g, hh, d):
        for j in range(group):
            yield from seq(hh, pair_of(g, j, d), d)

    def seq_chains(g):
        return [seq_chain(g, hh, d) for hh in range(nh) for d in range(2)]

    def step(g, _):
        _interleave(prep_chains(g + 1) + seq_chains(g))
        return 0

    _interleave(prep_chains(0))
    lax.fori_loop(0, ngroups - 1, step, 0)
    _interleave(seq_chains(ngroups - 1))

    for hh in range(nh):
        lanes = slice(hh * hd, (hh + 1) * hd)
        o = of_ref[hh] + ob_ref[hh]
        o = o * lax.rsqrt(jnp.mean(o * o, axis=-1, keepdims=True) + RMS_EPS) * gain_ref[...]
        gate = gate_ref[:, lanes].astype(F32)
        y_ref[:, lanes] = (o * (gate * _sigmoid(gate))).astype(y_ref.dtype)


def deltanet(p, ab, conv_t, alog32, dt32, gain, batch):
    t = p.shape[0]
    s = t // batch
    assert s % PAIR == 0 and (s // PAIR) % min(DN_PREP_PAIRS, s // PAIR) == 0
    hd = DN_HEAD_DIM
    nh = DN_HEADS_PER_STEP
    nsteps = DN_HEADS // nh
    col = lambda off: pl.BlockSpec((s, nh * hd), lambda b, h: (b, off * nsteps + h))
    cw = lambda off: pl.BlockSpec((3, nh * hd), lambda b, h: (0, off * nsteps + h))
    gate_col = pl.BlockSpec((s, nh * hd), lambda b, h: (b, 3 * nsteps + h), pipeline_mode=pl.Buffered(1))
    small = lambda n: pl.BlockSpec((1, n), lambda b, h: (0, 0))
    tok = pltpu.VMEM((nh, s, hd), F32)
    per_dir = pltpu.VMEM((nh, 2, s, hd), BF16)
    return pl.pallas_call(
        _dn_body,
        grid=(batch, nsteps),
        in_specs=[col(0), col(1), col(2), gate_col,
                  pl.BlockSpec((4 * DN_HEADS, s), lambda b, h: (0, b)),
                  cw(0), cw(1), cw(2),
                  small(4 * DN_HEADS), small(4 * DN_HEADS), small(hd)],
        out_specs=pl.BlockSpec((s, nh * hd), lambda b, h: (b, h)),
        out_shape=jax.ShapeDtypeStruct((t, DN_WIDTH), BF16),
        scratch_shapes=[tok, tok, tok, pltpu.VMEM((nh, hd, s), F32), tok, pltpu.VMEM((nh, SUBLANES, s), F32), tok, tok,
                        pltpu.VMEM((nh, 2, s, hd), F32), per_dir, per_dir, per_dir,
                        pltpu.VMEM((nh, 2, hd, s), BF16), pltpu.VMEM((nh, 2, SUBLANES, s), F32),
                        pltpu.VMEM((nh, 2, hd, hd), F32)],
        compiler_params=_params("parallel", "parallel"),
        name="deltanet",
    )(p, p, p, p, ab, conv_t, conv_t, conv_t, alog32, dt32, gain.reshape(1, hd))


def _sconv_body(b_ref, c_ref, u_ref, w_ref, y_ref):
    s = b_ref.shape[0]
    row = lax.broadcasted_iota(jnp.int32, (s, 1), 0)
    cu = c_ref[...].astype(F32) * u_ref[...].astype(F32)
    w = w_ref[...]
    y = _shift_down(cu, row) * w[0:1] + cu * w[1:2] + _shift_up(cu, row) * w[2:3]
    y_ref[...] = (b_ref[...].astype(F32) * y).astype(y_ref.dtype)


def short_conv(p, conv_t, col0, batch, tc=256):
    t = p.shape[0]
    s = t // batch
    nct = SC_WIDTH // tc
    base = col0 // tc
    col = lambda off: pl.BlockSpec((s, tc), lambda b, c: (b, base + off * nct + c))
    return pl.pallas_call(
        _sconv_body,
        grid=(batch, nct),
        in_specs=[col(0), col(1), col(2), pl.BlockSpec((3, tc), lambda b, c: (0, c))],
        out_specs=pl.BlockSpec((s, tc), lambda b, c: (b, c)),
        out_shape=jax.ShapeDtypeStruct((t, SC_WIDTH), BF16),
        compiler_params=_params("parallel", "parallel"),
        name="short_conv",
    )(p, p, p, conv_t)


def _dsa_body(*refs, seq):
    q_refs = refs[0:3]
    k_refs = refs[3:6]
    v_refs = refs[6:9]
    qn_ref, kn_ref, bias_ref, y_ref, qs_ref, kpad_ref, vpad_ref, og_ref, lse_ref = refs[9:]
    s = seq
    qb = DSA_QBLK
    side = DSA_SIDE
    width = qb + 2 * side
    kj = lax.broadcasted_iota(jnp.int32, (1, width), 1)
    zpad = jnp.zeros((DSA_PAD, DSA_HEAD_DIM), F32)
    for ref in (kpad_ref, vpad_ref):
        ref[0:DSA_PAD, :] = zpad
        ref[DSA_PAD + s:2 * DSA_PAD + s, :] = zpad

    for gi, (_, dil) in enumerate(DSA_PATTERNS):
        sub = s // dil
        nblk = sub // qb
        q = q_refs[gi][...].astype(F32)
        qs_ref[...] = (q * lax.rsqrt(jnp.mean(q * q, axis=-1, keepdims=True) + RMS_EPS) * qn_ref[...]
                       * (DSA_HEAD_DIM ** -0.5))
        k = k_refs[gi][...].astype(F32)
        kpad_ref[DSA_PAD:DSA_PAD + s, :] = k * lax.rsqrt(jnp.mean(k * k, axis=-1, keepdims=True) + RMS_EPS) * kn_ref[...]
        vpad_ref[DSA_PAD:DSA_PAD + s, :] = v_refs[gi][...].astype(F32)
        bias = bias_ref[0, gi]

        def block(t, gi=gi, dil=dil, sub=sub, nblk=nblk, bias=bias):
            r = t // nblk
            n = t % nblk
            rows = pl.ds(r + n * (qb * dil), qb, stride=dil)
            win = pl.ds(DSA_PAD + r + (n * qb - side) * dil, width, stride=dil)
            logits = _dot1(qs_ref[rows, :], kpad_ref[win, :], _NT) + bias
            yield
            pos = n * qb - side + kj
            logits = jnp.where((pos >= 0) & (pos < sub), logits, NEG_INF)
            m = jnp.max(logits, axis=-1, keepdims=True)
            yield
            p = jnp.exp(logits - m)
            ssum = jnp.sum(p, axis=-1, keepdims=True)
            o = _dot1(p, vpad_ref[win, :])
            yield
            og_ref[gi, rows, :] = o / ssum
            lse_ref[gi, rows, :] = jnp.broadcast_to(m + jnp.log(ssum), (qb, DSA_HEAD_DIM))

        nblocks = dil * nblk

        def blocks_step(i, _, block=block):
            _interleave(block(i * DSA_INTERLEAVE + u) for u in range(DSA_INTERLEAVE))
            return 0

        lax.fori_loop(0, nblocks // DSA_INTERLEAVE, blocks_step, 0)

    lse = [lse_ref[gi] for gi in range(DSA_GROUPS)]
    mx = jnp.maximum(jnp.maximum(lse[0], lse[1]), lse[2])
    ws = [jnp.exp(l - mx) for l in lse]
    num = ws[0] * og_ref[0] + ws[1] * og_ref[1] + ws[2] * og_ref[2]
    y_ref[...] = (num / (ws[0] + ws[1] + ws[2])).astype(y_ref.dtype)


def dilated_attention(p, qn, kn, bias_tab, batch):
    t = p.shape[0]
    s = t // batch
    assert (s // DSA_QBLK) % DSA_INTERLEAVE == 0
    hd = DSA_HEAD_DIM
    nh = DSA_HEADS

    def col(part, gi):
        return pl.BlockSpec((s, hd), lambda b, j: (b, part * nh + gi * DSA_HPG + j))

    in_specs = [col(part, gi) for part in range(3) for gi in range(DSA_GROUPS)]
    in_specs += [pl.BlockSpec((1, hd), lambda b, j: (0, 0)), pl.BlockSpec((1, hd), lambda b, j: (0, 0)),
                 pl.BlockSpec((1, DSA_GROUPS, DSA_QBLK, DSA_QBLK + 2 * DSA_SIDE), lambda b, j: (j, 0, 0, 0))]
    big = pltpu.VMEM((s, hd), F32)
    pad = pltpu.VMEM((s + 2 * DSA_PAD, hd), F32)
    grp = pltpu.VMEM((DSA_GROUPS, s, hd), F32)
    return pl.pallas_call(
        functools.partial(_dsa_body, seq=s),
        grid=(batch, DSA_HPG),
        in_specs=in_specs,
        out_specs=pl.BlockSpec((s, hd), lambda b, j: (b, j)),
        out_shape=jax.ShapeDtypeStruct((t, DSA_HPG * hd), BF16),
        scratch_shapes=[big, pad, pad, grp, grp],
        compiler_params=_params("parallel", "parallel"),
        name="dilated_attention",
    )(*([p] * 9), qn.reshape(1, hd), kn.reshape(1, hd), bias_tab)


def _t5_bucket(rel):
    half = REL_BUCKETS // 2
    max_exact = half // 2
    n = np.abs(rel)
    scaled = (np.log(np.maximum(n, max_exact).astype(np.float32) / np.float32(max_exact))
              / np.float32(math.log(REL_MAX_DIST / max_exact)))
    large = np.minimum(max_exact + (scaled * np.float32(half - max_exact)).astype(np.int32), half - 1)
    return np.where(rel > 0, half, 0) + np.where(n < max_exact, n, large)


def _dsa_bias_table(rel_bias):
    width = DSA_QBLK + 2 * DSA_SIDE
    tabs = []
    for gi, (_, dil) in enumerate(DSA_PATTERNS):
        offs = np.arange(-DSA_SIDE, DSA_SIDE + 1, dtype=np.int32) * dil
        band = rel_bias[_t5_bucket(offs)][:, gi * DSA_HPG:(gi + 1) * DSA_HPG].astype(F32)
        fill = jnp.full((DSA_QBLK - 1, DSA_HPG), NEG_INF, F32)
        line = jnp.concatenate([fill, band, fill], axis=0)
        tabs.append(jnp.stack([line[DSA_QBLK - 1 - q:DSA_QBLK - 1 - q + width] for q in range(DSA_QBLK)]))
    return jnp.transpose(jnp.stack(tabs), (3, 0, 1, 2))


def _head_block_diag():
    ii = lax.broadcasted_iota(jnp.int32, (LANES, LANES), 0)
    jj = lax.broadcasted_iota(jnp.int32, (LANES, LANES), 1)
    return (ii // RW_HEAD_DIM) == (jj // RW_HEAD_DIM)


def _rw_prep_body(r_ref, k_ref, v_ref, lo_ref, mur_ref, muk_ref, muv_ref, mulo_ref, w0_ref, w2_ref, a0_ref,
                  a2_ref, g2_ref, kk_ref, ka_ref, rk_ref,
                  ro_ref, vo_ref, kko_ref, bon_ref, gate_ref, lwf_ref, lwb_ref, kdf_ref, kdb_ref, bbf_ref, bbb_ref):
    s = r_ref.shape[0]
    row = lax.broadcasted_iota(jnp.int32, (s, 1), 0)

    def mix(t, mu):
        return t + mu * (0.5 * (_shift_down(t, row) + _shift_up(t, row)) - t)

    r = mix(r_ref[...].astype(F32), mur_ref[...])
    kr = mix(k_ref[...].astype(F32), muk_ref[...])
    v = mix(v_ref[...].astype(F32), muv_ref[...])
    lo = mix(lo_ref[...], mulo_ref[...])
    bd = jnp.where(_head_block_diag(), 1.0, 0.0)

    def head_sum(t):
        return _dot_sum(t, bd)

    kk = kr * kk_ref[...]
    kk = kk * lax.rsqrt(head_sum(kk * kk) + L2_EPS)
    gd = lo[:, 4 * RW_LORA:]
    gate_ref[...] = _dot1(_sigmoid(gd), g2_ref[...])
    ro_ref[...] = r
    vo_ref[...] = v
    kko_ref[...] = kk
    bonus = jnp.zeros_like(r)
    outs = ((lwf_ref, kdf_ref, bbf_ref), (lwb_ref, kdb_ref, bbb_ref))
    for d in range(2):
        wd = lo[:, d * RW_LORA:(d + 1) * RW_LORA]
        ad = lo[:, (2 + d) * RW_LORA:(3 + d) * RW_LORA]
        w_log = -_softplus(-(w0_ref[d:d + 1, :] + _dot1(jnp.tanh(wd), w2_ref[d]))) - 0.5
        a = _sigmoid(a0_ref[d:d + 1, :] + _dot1(ad, a2_ref[d]))
        kd = kr * (1.0 + (a - 1.0) * ka_ref[...])
        lw_ref, kd_ref, bb_ref = outs[d]
        lw_ref[...] = -jnp.exp(w_log)
        kd_ref[...] = kd
        bb_ref[...] = kk * a
        bonus = bonus + head_sum(r * kd * rk_ref[...]) * v
    bon_ref[...] = bonus


def rwkv_prep(p, lora_in, mu, w0, w2, a0, a2, g2, k_k, k_a, r_k, col0, batch):
    t = p.shape[0]
    s = t // batch
    nct = RW_WIDTH // LANES
    base = col0 // LANES
    nlo = RW_LORA_IN
    col = lambda off: pl.BlockSpec((s, LANES), lambda b, c: (b, base + off * nct + c))
    vec = lambda off: pl.BlockSpec((1, LANES), lambda b, c: (0, off * nct + c))
    mu_main = mu[:RW_MAIN].reshape(1, RW_MAIN)
    mu_lo = mu[RW_MAIN:].reshape(1, nlo)
    out_spec = pl.BlockSpec((s, LANES), lambda b, c: (b, c))
    n_out = 11
    return pl.pallas_call(
        _rw_prep_body,
        grid=(batch, nct),
        in_specs=[col(0), col(1), col(2),
                  pl.BlockSpec((s, nlo), lambda b, c: (b, 0)),
                  vec(0), vec(1), vec(2),
                  pl.BlockSpec((1, nlo), lambda b, c: (0, 0)),
                  pl.BlockSpec((2, LANES), lambda b, c: (0, c)),
                  pl.BlockSpec((2, RW_LORA, LANES), lambda b, c: (0, 0, c)),
                  pl.BlockSpec((2, LANES), lambda b, c: (0, c)),
                  pl.BlockSpec((2, RW_LORA, LANES), lambda b, c: (0, 0, c)),
                  pl.BlockSpec((RW_GATE_LORA, LANES), lambda b, c: (0, c)),
                  vec(0), vec(0), vec(0)],
        out_specs=[out_spec] * n_out,
        out_shape=[jax.ShapeDtypeStruct((t, RW_WIDTH), F32)] * n_out,
        compiler_params=_params("parallel", "parallel"),
        name="rwkv_prep",
    )(p, p, p, lora_in, mu_main, mu_main, mu_main, mu_lo, w0, w2, a0, a2, g2,
      k_k.reshape(1, RW_WIDTH), k_a.reshape(1, RW_WIDTH), r_k.reshape(1, RW_WIDTH))


def _rw_scan_body(r_ref, v_ref, kk_ref, bon_ref, gate_ref, lwf_ref, lwb_ref, kdf_ref, kdb_ref, bbf_ref, bbb_ref,
                  lnw_ref, lnb_ref, y_ref, cum_ref, cumt_ref, kdt_ref, bbt_ref, yf_ref, yb_ref,
                  tr_ref, tc_ref, rbk_ref, bkt_ref):
    s = r_ref.shape[0]
    c = CHUNK
    c2 = 2 * c
    row = lax.broadcasted_iota(jnp.int32, (s, 1), 0)
    pos = row % c
    lw_refs, kd_refs, bb_refs = (lwf_ref, lwb_ref), (kdf_ref, kdb_ref), (bbf_ref, bbb_ref)
    for d in range(2):
        cum = _seg_cumsum(lw_refs[d][...], pos, d == 1)
        cum_ref[d] = cum
        cumt_ref[d] = cum.T
        kdt_ref[d] = kd_refs[d][...].T
        bbt_ref[d] = bb_refs[d][...].T

    masks = _tri_masks(c2, c)
    merge = {rev: _merge_masks(c2, rev) for rev in (False, True)}
    eye2_f = jnp.where(masks["eye"], 1.0, 0.0)
    head0 = lax.broadcasted_iota(jnp.int32, (1, LANES), 1) < RW_HEAD_DIM
    head_bd = _head_block_diag()
    npair = s // PAIR
    group = min(RW_PREP_PAIRS, npair)

    def stack(x):
        return jnp.concatenate([jnp.where(head0, x, 0.0), jnp.where(head0, 0.0, x)], axis=0)

    def prep(pidx, d, half):
        reverse = d == 1
        incl2, strict2 = masks[reverse]
        cidx = 2 * pidx + half
        cols = pl.ds(pl.multiple_of(pidx * PAIR, PAIR), PAIR)
        rows = pl.ds(pl.multiple_of(cidx * c, c), c)
        hs = slice(half * c, (half + 1) * c)
        r, v, kk = r_ref[rows, :], v_ref[rows, :], kk_ref[rows, :]
        lw, kd, bb = lw_refs[d][rows, :], kd_refs[d][rows, :], bb_refs[d][rows, :]
        cum = cum_ref[d, rows, :]
        e_neg = jnp.exp(-cum)
        a_s = stack(-kk * jnp.exp(cum - lw))
        r_s = stack(r * jnp.exp(cum))
        ar = jnp.concatenate([a_s, r_s], axis=0)
        bk = jnp.concatenate([stack(bb * e_neg), stack(kd * e_neg)], axis=0)
        g = _dot1(ar, bk, _NT)
        yield
        rbk_ref[d, cidx] = jnp.concatenate([jnp.where(incl2, g[c2:, :c2], 0.0),
                                            jnp.where(incl2, g[c2:, c2:], 0.0)], axis=1).astype(BF16)
        av = _dot1(jnp.where(strict2, g[:c2, c2:], 0.0), stack(v))
        yield
        t_inv = []
        yield from _tri_inverse(-jnp.where(strict2, g[:c2, :c2], 0.0), eye2_f, merge[reverse], t_inv)
        tt = _dot1(t_inv[0], jnp.concatenate([a_s, av], axis=1))
        yield
        tr_ref[d, cidx] = jnp.concatenate([tt[:, :LANES], r_s], axis=0).astype(BF16)
        tc_ref[d, cidx] = tt[:, LANES:].astype(BF16)
        cum_t = cumt_ref[d, :, cols][:, hs]
        last = 0 if reverse else c - 1
        e_out_t = jnp.exp(cum_t[:, last:last + 1] - cum_t)
        bkt_ref[d, cidx] = jnp.concatenate([bbt_ref[d, :, cols][:, hs] * e_out_t,
                                            kdt_ref[d, :, cols][:, hs] * e_out_t], axis=1).astype(BF16)

    def seq(states, pidx, d):
        cum_t2 = cumt_ref[d, :, pl.ds(pl.multiple_of(pidx * PAIR, PAIR), PAIR)]
        for half in ((1, 0) if d else (0, 1)):
            cidx = 2 * pidx + half
            rows = pl.ds(pl.multiple_of(cidx * c, c), c)
            v = v_ref[rows, :]
            x = _dot1(tr_ref[d, cidx], states[d])
            yield
            ps = x[:c2] + tc_ref[d, cidx].astype(F32)
            os_ = x[c2:] + _dot1(rbk_ref[d, cidx], jnp.concatenate([ps, stack(v)], axis=0))
            y = os_[:c] + os_[c:]
            if d:
                yb_ref[rows, :] = y
            else:
                yf_ref[rows, :] = y
            pv = jnp.concatenate([ps[:c] + ps[c:], v], axis=0)
            last = half * c + (0 if d else c - 1)
            e_tot = jnp.exp(cum_t2[:, last:last + 1])
            states[d] = states[d] * e_tot + jnp.where(head_bd, _dot1(bkt_ref[d, cidx], pv), 0.0)
            yield

    ngroups = npair // group

    def prep_chains(g):
        return [prep(npair - 1 - (g * group + j) if d else g * group + j, d, half)
                for j in range(group) for d in range(2) for half in range(2)]

    def seq_chain(states, g, d):
        for j in range(group):
            yield from seq(states, npair - 1 - (g * group + j) if d else g * group + j, d)

    def step(g, carry):
        states = list(carry)
        _interleave(prep_chains(g + 1) + [seq_chain(states, g, 0), seq_chain(states, g, 1)])
        return tuple(states)

    zero = jnp.zeros((LANES, LANES), F32)
    _interleave(prep_chains(0))
    states = list(lax.fori_loop(0, ngroups - 1, step, (zero, zero)))
    _interleave([seq_chain(states, ngroups - 1, 0), seq_chain(states, ngroups - 1, 1)])

    bd = jnp.where(head_bd, 1.0, 0.0)
    y = yf_ref[...] + yb_ref[...]
    mean = _dot_sum(y, bd) * (1.0 / RW_HEAD_DIM)
    yc = y - mean
    var = _dot_sum(yc * yc, bd) * (1.0 / RW_HEAD_DIM)
    yn = yc * lax.rsqrt(var + RW_GN_EPS) * lnw_ref[...] + lnb_ref[...]
    y_ref[...] = ((yn + bon_ref[...]) * gate_ref[...]).astype(y_ref.dtype)


def rwkv_scan(prep, ln_w, ln_b, batch):
    t = prep[0].shape[0]
    s = t // batch
    assert s % PAIR == 0 and (s // PAIR) % min(RW_PREP_PAIRS, s // PAIR) == 0
    nct = RW_WIDTH // LANES
    nchunk = s // CHUNK
    blk = pl.BlockSpec((s, LANES), lambda b, c: (b, c))
    vec = pl.BlockSpec((1, LANES), lambda b, c: (0, c))
    tsp = pltpu.VMEM((2, LANES, s), F32)
    big = pltpu.VMEM((s, LANES), F32)
    per_chunk = lambda rows, cols: pltpu.VMEM((2, nchunk, rows, cols), BF16)
    return pl.pallas_call(
        _rw_scan_body,
        grid=(batch, nct),
        in_specs=[blk] * 11 + [vec, vec],
        out_specs=blk,
        out_shape=jax.ShapeDtypeStruct((t, RW_WIDTH), BF16),
        scratch_shapes=[pltpu.VMEM((2, s, LANES), F32), tsp, tsp, tsp, big, big,
                        per_chunk(2 * PAIR, LANES), per_chunk(PAIR, LANES), per_chunk(PAIR, 2 * LANES),
                        per_chunk(PAIR, LANES)],
        compiler_params=_params("parallel", "parallel"),
        name="rwkv_scan",
    )(*prep, ln_w.reshape(1, RW_WIDTH), ln_b.reshape(1, RW_WIDTH))


def even_mixer(x, g, w_in, w_sc, w_ab, w_out, layer, conv_qkv, a_log, dt_bias, out_gain, conv_sc, batch):
    p, ab = norm_mm(x, g, w_in, layer, ncols=4 * DN_WIDTH, w2=w_sc, wp_t=w_ab.T, out_dtype=BF16)
    zeros = jnp.zeros((2 * DN_HEADS,), F32)
    alog32 = jnp.concatenate([zeros, a_log.reshape(-1)]).reshape(1, -1)
    dt32 = jnp.concatenate([zeros, dt_bias.reshape(-1)]).reshape(1, -1)
    y_dn = deltanet(p, ab, conv_qkv.T, alog32, dt32, out_gain, batch)
    y_sc = short_conv(p, conv_sc.T, 4 * DN_WIDTH, batch)
    return mm_res(y_dn, w_out, layer, x, a2=y_sc, tn=D_MODEL)


def odd_mixer(x, g, w_in, w_lora, w_out, layer, qn, kn, bias_tab, mu, w0, w2, a0, a2, g2, k_k, k_a, r_k, ln_w, ln_b,
              batch):
    p, lora_in = norm_mm(x, g, w_in, layer, ncols=3 * DSA_QKV + RW_MAIN, wp=w_lora, out_dtype=BF16)
    y_c = dilated_attention(p, qn, kn, bias_tab, batch)
    prep = rwkv_prep(p, lora_in, mu, w0, w2, a0, a2, g2, k_k, k_a, r_k, 3 * DSA_QKV, batch)
    y_d = rwkv_scan(prep, ln_w, ln_b, batch)
    return mm_res(y_c, w_out, layer, x, a2=y_d, tn=D_MODEL)


def kernel(x, mem, rel_bias, norm_mix, norm_xattn, norm_mem, norm_ffn, xa_wq, xa_wk, xa_wv, xa_wo, xa_qn, xa_kn, ffn_w1, ffn_w2, ev_w_in, ev_w_out, dn_conv, dn_a_log, dn_dt_bias, dn_norm, sc_conv, od_w_in, od_w_out, ca_qn, ca_kn, rw_mu, rw_w0, rw_w2, rw_a0, rw_a2, rw_g2, rw_k_k, rw_k_a, rw_r_k, rw_ln_w, rw_ln_b):
    batch, seq, d = x.shape
    n_mem = mem.shape[1]
    xf = x.reshape(batch * seq, d)
    memf = mem.reshape(batch * n_mem, d)
    bias_tab = _dsa_bias_table(rel_bias)
    c_ab = 4 * DN_WIDTH
    c_sc = c_ab + 4 * DN_HEADS
    c_lo = 3 * DSA_QKV + RW_MAIN
    ev_in, od_in = ev_w_in.astype(BF16), od_w_in.astype(BF16)
    ev_sc = ev_in[:, :, c_sc:]
    ev_out, od_out = ev_w_out.astype(BF16), od_w_out.astype(BF16)
    w_q, w_o = xa_wq.astype(BF16), xa_wo.astype(BF16)
    w_kv = jnp.concatenate([xa_wk, xa_wv], axis=2).astype(BF16)
    for layer in range(DEPTH):
        i = layer // 2
        if layer % 2 == 0:
            xf = even_mixer(xf, norm_mix[layer], ev_in, ev_sc, ev_w_in[i, :, c_ab:c_sc], ev_out, i, dn_conv[i],
                            dn_a_log[i], dn_dt_bias[i], dn_norm[i], sc_conv[i], batch)
        else:
            xf = odd_mixer(xf, norm_mix[layer], od_in, od_w_in[i, :, c_lo:], od_out, i, ca_qn[i], ca_kn[i], bias_tab,
                           rw_mu[i], rw_w0[i], rw_w2[i], rw_a0[i], rw_a2[i], rw_g2[i], rw_k_k[i], rw_k_a[i], rw_r_k[i],
                           rw_ln_w[i], rw_ln_b[i], batch)
        kv = norm_mm(memf, norm_mem[layer], w_kv, layer).reshape(batch, n_mem, 2 * XA_WIDTH)
        xf = xattn(xf, norm_xattn[layer], w_q, kv, w_o, layer, xa_qn[layer], xa_kn[layer], batch)
        h1 = norm_mm(xf, norm_ffn[layer], ffn_w1, layer, act="relu2", out_dtype=BF16)
        xf = mm_res(h1, ffn_w2, layer, xf)
    return xf.reshape(batch, seq, d)
```

```python
import functools
import math

import jax
import jax.numpy as jnp
import numpy as np
from jax import lax
from jax.experimental import pallas as pl
from jax.experimental.pallas import tpu as pltpu

F32 = jnp.float32
BF16 = jnp.bfloat16

D_MODEL = 2048
DEPTH = 4
RMS_EPS = 1e-6
L2_EPS = 1e-6

DN_HEADS = 8
DN_HEAD_DIM = 128
DN_WIDTH = DN_HEADS * DN_HEAD_DIM
SC_WIDTH = D_MODEL - DN_WIDTH
CHUNK = 64
PAIR = 2 * CHUNK
DN_PREP_PAIRS = 4
DN_HEADS_PER_STEP = 2
RW_PREP_PAIRS = 4

DSA_PATTERNS = ((128, 1), (512, 4), (2048, 16))
DSA_GROUPS = len(DSA_PATTERNS)
DSA_HPG = 4
DSA_HEAD_DIM = 128
DSA_HEADS = DSA_GROUPS * DSA_HPG
DSA_QKV = DSA_HEADS * DSA_HEAD_DIM
DSA_SIDE = 64
DSA_QBLK = 128
DSA_PAD = DSA_SIDE * max(d for _, d in DSA_PATTERNS)
DSA_INTERLEAVE = 8
REL_BUCKETS = 32
REL_MAX_DIST = 1024
NEG_INF = -1e30

RW_HEADS = 8
RW_HEAD_DIM = 64
RW_WIDTH = RW_HEADS * RW_HEAD_DIM
RW_LORA = 64
RW_GATE_LORA = 128
RW_MAIN = 3 * RW_WIDTH
RW_LORA_IN = 4 * RW_LORA + RW_GATE_LORA
RW_GN_EPS = 64e-5

XA_HEADS = 4
XA_HEAD_DIM = 128
XA_WIDTH = XA_HEADS * XA_HEAD_DIM

LANES = 128
SUBLANES = 8
NORM_ROWS = 256
VMEM_LIMIT_BYTES = 58 * 1024 * 1024


def _params(*sem):
    return pltpu.CompilerParams(dimension_semantics=sem, vmem_limit_bytes=VMEM_LIMIT_BYTES)


_NN = (((1,), (0,)), ((), ()))
_NT = (((1,), (1,)), ((), ()))


def _dot1(a, b, dims=_NN):
    return lax.dot_general(a.astype(BF16), b.astype(BF16), dims, preferred_element_type=F32)


def _dot_sum(a, b01):
    b = b01.astype(BF16)
    hi = a.astype(BF16)
    lo = (a - hi.astype(F32)).astype(BF16)
    dg = functools.partial(lax.dot_general, dimension_numbers=_NN, preferred_element_type=F32)
    return dg(hi, b) + dg(lo, b)


def _interleave(chains):
    chains = list(chains)
    while chains:
        alive = []
        for ch in chains:
            try:
                next(ch)
                alive.append(ch)
            except StopIteration:
                pass
        chains = alive


def _merge_masks(n, reverse):
    ii = lax.broadcasted_iota(jnp.int32, (n, n), 0)
    jj = lax.broadcasted_iota(jnp.int32, (n, n), 1)
    tri = (ii < jj) if reverse else (ii > jj)
    out = []
    size = 1
    while size < CHUNK:
        out.append(tri & ((ii // (2 * size)) == (jj // (2 * size))) & ((ii // size) != (jj // size)))
        size *= 2
    return out


def _tri_inverse(a, eye_f, level_masks, out):
    t = eye_f - jnp.where(level_masks[0], a, 0.0)
    for m in level_masks[1:]:
        x = _dot1(jnp.where(m, a, 0.0), t)
        yield
        t = t - _dot1(t, x)
        yield
    out.append(t)


def _sigmoid(x):
    return 1.0 / (1.0 + jnp.exp(-x))


def _softplus(x):
    return jnp.maximum(x, 0.0) + jnp.log1p(jnp.exp(-jnp.abs(x)))


def _shift_down(x, row):
    return jnp.where(row == 0, 0.0, pltpu.roll(x, 1, 0))


def _shift_up(x, row):
    n = x.shape[0]
    return jnp.where(row == n - 1, 0.0, pltpu.roll(x, n - 1, 0))


def _norm_mm_body(x_ref, g_ref, w_ref, *rest, act, extra, extra_t):
    if extra:
        wp_ref, o_ref, op_ref, xn_ref = rest
    else:
        o_ref, xn_ref = rest

    @pl.when(pl.program_id(1) == 0)
    def _():
        def rows_step(r, _):
            rows = pl.ds(pl.multiple_of(r * NORM_ROWS, NORM_ROWS), NORM_ROWS)
            x = x_ref[rows, :]
            xn = x * lax.rsqrt(jnp.mean(x * x, axis=-1, keepdims=True) + RMS_EPS) * g_ref[...]
            xn_ref[rows, :] = xn.astype(BF16)
            if extra and extra_t:
                op_ref[:, rows] = _dot1(wp_ref[...], xn, _NT)
            elif extra:
                op_ref[rows, :] = _dot1(xn, wp_ref[...])
            return 0

        lax.fori_loop(0, x_ref.shape[0] // NORM_ROWS, rows_step, 0)

    acc = jnp.dot(xn_ref[...], w_ref[...].astype(BF16), preferred_element_type=F32)
    if act == "relu2":
        acc = jnp.square(jnp.maximum(acc, 0.0))
    o_ref[...] = acc.astype(o_ref.dtype)


def norm_mm(x, g, w, layer, ncols=None, wp=None, wp_t=None, act=None, out_dtype=F32, tm=1024, tn=1024):
    t, d = x.shape
    n = w.shape[2] if ncols is None else ncols
    tm = min(tm, t)
    tn = min(tn, n)
    assert t % tm == 0 and n % tn == 0 and tm % NORM_ROWS == 0
    extra = wp is not None or wp_t is not None
    in_specs = [
        pl.BlockSpec((tm, d), lambda i, j: (i, 0)),
        pl.BlockSpec((1, d), lambda i, j: (0, 0)),
        pl.BlockSpec((None, d, tn), lambda i, j: (layer, 0, j)),
    ]
    args = [x, g.reshape(1, d), w]
    out_specs = pl.BlockSpec((tm, tn), lambda i, j: (i, j))
    out_shape = jax.ShapeDtypeStruct((t, n), out_dtype)
    if wp is not None:
        npc = wp.shape[1]
        in_specs.append(pl.BlockSpec((d, npc), lambda i, j: (0, 0)))
        out_specs = [out_specs, pl.BlockSpec((tm, npc), lambda i, j: (i, 0))]
        out_shape = [out_shape, jax.ShapeDtypeStruct((t, npc), F32)]
        args.append(wp)
    elif wp_t is not None:
        npc = wp_t.shape[0]
        in_specs.append(pl.BlockSpec((npc, d), lambda i, j: (0, 0)))
        out_specs = [out_specs, pl.BlockSpec((npc, tm), lambda i, j: (0, i))]
        out_shape = [out_shape, jax.ShapeDtypeStruct((npc, t), F32)]
        args.append(wp_t)
    return pl.pallas_call(
        functools.partial(_norm_mm_body, act=act, extra=extra, extra_t=wp_t is not None),
        grid=(t // tm, n // tn),
        in_specs=in_specs,
        out_specs=out_specs,
        out_shape=out_shape,
        scratch_shapes=[pltpu.VMEM((tm, d), BF16)],
        compiler_params=_params("parallel", "arbitrary"),
        name="norm_mm",
    )(*args)


def _mm_res_body(a_ref, *rest, n_first):
    if n_first is not None:
        a2_ref, rest = rest[0], rest[1:]
    w_ref, r_ref, o_ref = rest
    kk = pl.program_id(2)

    @pl.when(kk == 0)
    def _():
        o_ref[...] = r_ref[...]

    def accumulate(lhs_ref):
        o_ref[...] += jnp.dot(lhs_ref[...], w_ref[...].astype(BF16), preferred_element_type=F32)

    if n_first is None:
        accumulate(a_ref)
    else:
        pl.when(kk < n_first)(lambda: accumulate(a_ref))
        pl.when(kk >= n_first)(lambda: accumulate(a2_ref))


def mm_res(a, w, layer, res, a2=None, tm=1024, tn=1024, tk=2048):
    t, k1 = a.shape
    k = k1 + (0 if a2 is None else a2.shape[1])
    n = w.shape[2]
    tm, tn, tk = min(tm, t), min(tn, n), min(tk, k1)
    assert t % tm == 0 and n % tn == 0 and k1 % tk == 0 and k % tk == 0
    n_first = None if a2 is None else k1 // tk
    if a2 is None:
        in_specs = [pl.BlockSpec((tm, tk), lambda i, j, kk: (i, kk))]
        args = [a]
    else:
        in_specs = [pl.BlockSpec((tm, tk), lambda i, j, kk: (i, jnp.minimum(kk, n_first - 1))),
                    pl.BlockSpec((tm, tk), lambda i, j, kk: (i, jnp.maximum(kk - n_first, 0)))]
        args = [a, a2]
    in_specs += [
        pl.BlockSpec((None, tk, tn), lambda i, j, kk: (layer, kk, j)),
        pl.BlockSpec((tm, tn), lambda i, j, kk: (i, j)),
    ]
    return pl.pallas_call(
        functools.partial(_mm_res_body, n_first=n_first),
        grid=(t // tm, n // tn, k // tk),
        in_specs=in_specs,
        out_specs=pl.BlockSpec((tm, tn), lambda i, j, kk: (i, j)),
        out_shape=jax.ShapeDtypeStruct((t, n), F32),
        compiler_params=_params("parallel", "parallel", "arbitrary"),
        name="mm_res",
    )(*args, w, res)


def _xattn_body(x_ref, g_ref, wq_ref, kv_ref, wo_ref, qn_ref, kn_ref, o_ref):
    x = x_ref[...]
    xn = x * lax.rsqrt(jnp.mean(x * x, axis=-1, keepdims=True) + RMS_EPS) * g_ref[...]
    q = jnp.dot(xn.astype(BF16), wq_ref[...], preferred_element_type=F32)
    kv = kv_ref[0]
    outs = []
    for h in range(XA_HEADS):
        sl = slice(h * XA_HEAD_DIM, (h + 1) * XA_HEAD_DIM)
        qh = q[:, sl]
        qh = qh * lax.rsqrt(jnp.mean(qh * qh, axis=-1, keepdims=True) + RMS_EPS) * qn_ref[...]
        kh = kv[:, sl]
        kh = kh * lax.rsqrt(jnp.mean(kh * kh, axis=-1, keepdims=True) + RMS_EPS) * kn_ref[...]
        vh = kv[:, XA_WIDTH + h * XA_HEAD_DIM:XA_WIDTH + (h + 1) * XA_HEAD_DIM]
        logits = _dot1(qh, kh, _NT) * (XA_HEAD_DIM ** -0.5)
        m = jnp.max(logits, axis=-1, keepdims=True)
        p = jnp.exp(logits - m)
        s = jnp.sum(p, axis=-1, keepdims=True)
        outs.append(_dot1(p, vh) / s)
    o = jnp.concatenate(outs, axis=-1).astype(BF16)
    o_ref[...] = x + jnp.dot(o, wo_ref[...], preferred_element_type=F32)


def xattn(x, g, wq, kv, wo, layer, qn, kn, batch, ts=1024):
    t, d = x.shape
    s = t // batch
    ts = min(ts, s)
    nst = s // ts
    m = kv.shape[1]
    return pl.pallas_call(
        _xattn_body,
        grid=(batch, nst),
        in_specs=[
            pl.BlockSpec((ts, d), lambda b, i: (b * nst + i, 0)),
            pl.BlockSpec((1, d), lambda b, i: (0, 0)),
            pl.BlockSpec((None, d, XA_WIDTH), lambda b, i: (layer, 0, 0)),
            pl.BlockSpec((1, m, 2 * XA_WIDTH), lambda b, i: (b, 0, 0)),
            pl.BlockSpec((None, XA_WIDTH, d), lambda b, i: (layer, 0, 0)),
            pl.BlockSpec((1, XA_HEAD_DIM), lambda b, i: (0, 0)),
            pl.BlockSpec((1, XA_HEAD_DIM), lambda b, i: (0, 0)),
        ],
        out_specs=pl.BlockSpec((ts, d), lambda b, i: (b * nst + i, 0)),
        out_shape=jax.ShapeDtypeStruct((t, d), F32),
        compiler_params=_params("parallel", "parallel"),
        name="xattn",
    )(x, g.reshape(1, d), wq, kv, wo, qn.reshape(1, -1), kn.reshape(1, -1))


def _tri_masks(n, blk):
    ii = lax.broadcasted_iota(jnp.int32, (n, n), 0)
    jj = lax.broadcasted_iota(jnp.int32, (n, n), 1)
    same = (ii // blk) == (jj // blk) if n != blk else None

    def m(c):
        return c if same is None else (c & same)

    return {
        False: (m(ii >= jj), m(ii > jj)),
        True: (m(ii <= jj), m(ii < jj)),
        "eye": ii == jj,
    }


def _seg_cumsum(x, pos, reverse):
    n = x.shape[0]
    sh = 1
    while sh < CHUNK:
        if reverse:
            x = x + jnp.where(pos < CHUNK - sh, pltpu.roll(x, n - sh, 0), 0.0)
        else:
            x = x + jnp.where(pos >= sh, pltpu.roll(x, sh, 0), 0.0)
        sh *= 2
    return x


def _dn_body(q_ref, k_ref, v_ref, gate_ref, ab_ref, cq_ref, ck_ref, cv_ref, alog_ref, dt_ref, gain_ref,
             y_ref, qs_ref, ks_ref, vs_ref, kt_ref, bg_ref, gt_ref, of_ref, ob_ref,
             u_ref, w_ref, qg_ref, qk_ref, kdt_ref, eg_ref, st_ref):
    s = q_ref.shape[0]
    hd = DN_HEAD_DIM
    nh = DN_HEADS_PER_STEP
    row = lax.broadcasted_iota(jnp.int32, (s, 1), 0)
    c = CHUNK
    lane32 = lax.broadcasted_iota(jnp.int32, (1, 4 * DN_HEADS), 1)
    lane128 = lax.broadcasted_iota(jnp.int32, (1, LANES), 1)
    pos_t = lax.broadcasted_iota(jnp.int32, (1, s), 1) % c

    def conv_silu(x_ref, cw_ref, lanes):
        x = x_ref[:, lanes].astype(F32)
        w = cw_ref[:, lanes]
        y = _shift_down(x, row) * w[0:1] + x * w[1:2] + _shift_up(x, row) * w[2:3]
        return y * _sigmoid(y)

    for hh in range(nh):
        h = pl.program_id(1) * nh + hh
        lanes = slice(hh * hd, (hh + 1) * hd)
        q = conv_silu(q_ref, cq_ref, lanes)
        q = q * lax.rsqrt(jnp.sum(q * q, axis=-1, keepdims=True) + L2_EPS) * (hd ** -0.5)
        qs_ref[hh] = q
        k = conv_silu(k_ref, ck_ref, lanes)
        k = k * lax.rsqrt(jnp.sum(k * k, axis=-1, keepdims=True) + L2_EPS)
        ks_ref[hh] = k
        kt_ref[hh] = k.T
        vs_ref[hh] = conv_silu(v_ref, cv_ref, lanes)

        def decay_row(d, h=h):
            idx = (2 + d) * DN_HEADS + h
            a_log = jnp.sum(jnp.where(lane32 == idx, alog_ref[...], 0.0), axis=-1, keepdims=True)
            dt = jnp.sum(jnp.where(lane32 == idx, dt_ref[...], 0.0), axis=-1, keepdims=True)
            g = -jnp.exp(a_log) * _softplus(ab_ref[pl.ds(idx, 1), :] + dt)
            sh = 1
            while sh < c:
                if d:
                    g = g + jnp.where(pos_t < c - sh, pltpu.roll(g, s - sh, 1), 0.0)
                else:
                    g = g + jnp.where(pos_t >= sh, pltpu.roll(g, sh, 1), 0.0)
                sh *= 2
            return g

        rows = [_sigmoid(ab_ref[pl.ds(h, 1), :]), _sigmoid(ab_ref[pl.ds(DN_HEADS + h, 1), :]),
                decay_row(0), decay_row(1)]
        gt = jnp.concatenate(rows + [jnp.zeros((LANES - len(rows), s), F32)], axis=0)
        gt_ref[hh] = gt[0:SUBLANES, :]
        bg_ref[hh] = gt.T
        for d in range(2):
            st_ref[hh, d] = jnp.zeros((hd, hd), F32)

    masks = _tri_masks(PAIR, c)
    merge = {rev: _merge_masks(PAIR, rev) for rev in (False, True)}
    eye_f = jnp.where(masks["eye"], 1.0, 0.0)
    npair = s // PAIR
    group = min(DN_PREP_PAIRS, npair)

    def prep(hh, pidx, d):
        reverse = d == 1
        rows = pl.ds(pl.multiple_of(pidx * PAIR, PAIR), PAIR)
        qp, kp, vp = qs_ref[hh, rows, :], ks_ref[hh, rows, :], vs_ref[hh, rows, :]
        bgp = bg_ref[hh, rows, :]
        beta = bgp[:, d:d + 1]
        gcol = bgp[:, 2 + d:3 + d]
        grow = gt_ref[hh, 2 + d:3 + d, rows]
        incl, strict = masks[reverse]
        decay = jnp.where(incl, jnp.exp(jnp.where(incl, gcol - grow, 0.0)), 0.0)
        kb = kp * beta
        kq = _dot1(jnp.concatenate([kb, qp], axis=0), kp, _NT)
        yield
        qk_ref[hh, d, rows, :] = jnp.where(incl, kq[PAIR:] * decay, 0.0).astype(BF16)
        t_inv = []
        yield from _tri_inverse(jnp.where(strict, kq[:PAIR] * decay, 0.0), eye_f, merge[reverse], t_inv)
        egc = jnp.exp(gcol)
        uw = _dot1(t_inv[0], jnp.concatenate([vp * beta, kb * egc], axis=1))
        yield
        u_ref[hh, d, rows, :] = uw[:, :hd]
        w_ref[hh, d, rows, :] = uw[:, hd:].astype(BF16)
        qg_ref[hh, d, rows, :] = (qp * egc).astype(BF16)
        last0, last1 = (0, c) if reverse else (c - 1, PAIR - 1)
        glast = jnp.where(lane128 < c, grow[:, last0:last0 + 1], grow[:, last1:last1 + 1])
        kdt_ref[hh, d, :, rows] = (kt_ref[hh, :, rows] * jnp.exp(glast - grow)).astype(BF16)
        eg_ref[hh, d, :, rows] = jnp.broadcast_to(jnp.exp(glast), (SUBLANES, PAIR))

    def seq(hh, pidx, d):
        pair_rows = pl.ds(pl.multiple_of(pidx * PAIR, PAIR), PAIR)
        kdt = kdt_ref[hh, d, :, pair_rows]
        eg = eg_ref[hh, d, 0:1, pair_rows]
        for half in ((1, 0) if d else (0, 1)):
            r0 = half * c
            rows = pl.ds(pl.multiple_of(pidx * PAIR + r0, c), c)
            state = st_ref[hh, d]
            ws = _dot1(jnp.concatenate([w_ref[hh, d, rows, :], qg_ref[hh, d, rows, :]], axis=0), state)
            yield
            v_new = u_ref[hh, d, rows, :] - ws[:c]
            o = ws[c:] + _dot1(qk_ref[hh, d, rows, r0:r0 + c], v_new)
            if d:
                ob_ref[hh, rows, :] = o
            else:
                of_ref[hh, rows, :] = o
            st_ref[hh, d] = state * eg[:, r0:r0 + 1] + _dot1(kdt[:, r0:r0 + c], v_new)
            yield

    ngroups = npair // group

    def pair_of(g, j, d):
        return npair - 1 - (g * group + j) if d else g * group + j

    def prep_chains(g):
        return [prep(hh, pair_of(g, j, d), d) for j in range(group) for hh in range(nh) for d in range(2)]

    def seq_chain(g, hh, d):
        for j in range(group):
            yield from seq(hh, pair_of(g, j, d), d)

    def seq_chains(g):
        return [seq_chain(g, hh, d) for hh in range(nh) for d in range(2)]

    def step(g, _):
        _interleave(prep_chains(g + 1) + seq_chains(g))
        return 0

    _interleave(prep_chains(0))
    lax.fori_loop(0, ngroups - 1, step, 0)
    _interleave(seq_chains(ngroups - 1))

    for hh in range(nh):
        lanes = slice(hh * hd, (hh + 1) * hd)
        o = of_ref[hh] + ob_ref[hh]
        o = o * lax.rsqrt(jnp.mean(o * o, axis=-1, keepdims=True) + RMS_EPS) * gain_ref[...]
        gate = gate_ref[:, lanes].astype(F32)
        y_ref[:, lanes] = (o * (gate * _sigmoid(gate))).astype(y_ref.dtype)


def deltanet(p, ab, conv_t, alog32, dt32, gain, batch):
    t = p.shape[0]
    s = t // batch
    assert s % PAIR == 0 and (s // PAIR) % min(DN_PREP_PAIRS, s // PAIR) == 0
    hd = DN_HEAD_DIM
    nh = DN_HEADS_PER_STEP
    nsteps = DN_HEADS // nh
    col = lambda off: pl.BlockSpec((s, nh * hd), lambda b, h: (b, off * nsteps + h))
    cw = lambda off: pl.BlockSpec((3, nh * hd), lambda b, h: (0, off * nsteps + h))
    gate_col = pl.BlockSpec((s, nh * hd), lambda b, h: (b, 3 * nsteps + h), pipeline_mode=pl.Buffered(1))
    small = lambda n: pl.BlockSpec((1, n), lambda b, h: (0, 0))
    tok = pltpu.VMEM((nh, s, hd), F32)
    per_dir = pltpu.VMEM((nh, 2, s, hd), BF16)
    return pl.pallas_call(
        _dn_body,
        grid=(batch, nsteps),
        in_specs=[col(0), col(1), col(2), gate_col,
                  pl.BlockSpec((4 * DN_HEADS, s), lambda b, h: (0, b)),
                  cw(0), cw(1), cw(2),
                  small(4 * DN_HEADS), small(4 * DN_HEADS), small(hd)],
        out_specs=pl.BlockSpec((s, nh * hd), lambda b, h: (b, h)),
        out_shape=jax.ShapeDtypeStruct((t, DN_WIDTH), BF16),
        scratch_shapes=[tok, tok, tok, pltpu.VMEM((nh, hd, s), F32), tok, pltpu.VMEM((nh, SUBLANES, s), F32), tok, tok,
                        pltpu.VMEM((nh, 2, s, hd), F32), per_dir, per_dir, per_dir,
                        pltpu.VMEM((nh, 2, hd, s), BF16), pltpu.VMEM((nh, 2, SUBLANES, s), F32),
                        pltpu.VMEM((nh, 2, hd, hd), F32)],
        compiler_params=_params("parallel", "parallel"),
        name="deltanet",
    )(p, p, p, p, ab, conv_t, conv_t, conv_t, alog32, dt32, gain.reshape(1, hd))


def _sconv_body(b_ref, c_ref, u_ref, w_ref, y_ref):
    s = b_ref.shape[0]
    row = lax.broadcasted_iota(jnp.int32, (s, 1), 0)
    cu = c_ref[...].astype(F32) * u_ref[...].astype(F32)
    w = w_ref[...]
    y = _shift_down(cu, row) * w[0:1] + cu * w[1:2] + _shift_up(cu, row) * w[2:3]
    y_ref[...] = (b_ref[...].astype(F32) * y).astype(y_ref.dtype)


def short_conv(p, conv_t, col0, batch, tc=256):
    t = p.shape[0]
    s = t // batch
    nct = SC_WIDTH // tc
    base = col0 // tc
    col = lambda off: pl.BlockSpec((s, tc), lambda b, c: (b, base + off * nct + c))
    return pl.pallas_call(
        _sconv_body,
        grid=(batch, nct),
        in_specs=[col(0), col(1), col(2), pl.BlockSpec((3, tc), lambda b, c: (0, c))],
        out_specs=pl.BlockSpec((s, tc), lambda b, c: (b, c)),
        out_shape=jax.ShapeDtypeStruct((t, SC_WIDTH), BF16),
        compiler_params=_params("parallel", "parallel"),
        name="short_conv",
    )(p, p, p, conv_t)


def _dsa_body(*refs, seq):
    q_refs = refs[0:3]
    k_refs = refs[3:6]
    v_refs = refs[6:9]
    qn_ref, kn_ref, bias_ref, y_ref, qs_ref, kpad_ref, vpad_ref, og_ref, lse_ref = refs[9:]
    s = seq
    qb = DSA_QBLK
    side = DSA_SIDE
    width = qb + 2 * side
    kj = lax.broadcasted_iota(jnp.int32, (1, width), 1)
    zpad = jnp.zeros((DSA_PAD, DSA_HEAD_DIM), F32)
    for ref in (kpad_ref, vpad_ref):
        ref[0:DSA_PAD, :] = zpad
        ref[DSA_PAD + s:2 * DSA_PAD + s, :] = zpad

    for gi, (_, dil) in enumerate(DSA_PATTERNS):
        sub = s // dil
        nblk = sub // qb
        q = q_refs[gi][...].astype(F32)
        qs_ref[...] = (q * lax.rsqrt(jnp.mean(q * q, axis=-1, keepdims=True) + RMS_EPS) * qn_ref[...]
                       * (DSA_HEAD_DIM ** -0.5))
        k = k_refs[gi][...].astype(F32)
        kpad_ref[DSA_PAD:DSA_PAD + s, :] = k * lax.rsqrt(jnp.mean(k * k, axis=-1, keepdims=True) + RMS_EPS) * kn_ref[...]
        vpad_ref[DSA_PAD:DSA_PAD + s, :] = v_refs[gi][...].astype(F32)
        bias = bias_ref[0, gi]

        def block(t, gi=gi, dil=dil, sub=sub, nblk=nblk, bias=bias):
            r = t // nblk
            n = t % nblk
            rows = pl.ds(r + n * (qb * dil), qb, stride=dil)
            win = pl.ds(DSA_PAD + r + (n * qb - side) * dil, width, stride=dil)
            logits = _dot1(qs_ref[rows, :], kpad_ref[win, :], _NT) + bias
            yield
            pos = n * qb - side + kj
            logits = jnp.where((pos >= 0) & (pos < sub), logits, NEG_INF)
            m = jnp.max(logits, axis=-1, keepdims=True)
            yield
            p = jnp.exp(logits - m)
            ssum = jnp.sum(p, axis=-1, keepdims=True)
            o = _dot1(p, vpad_ref[win, :])
            yield
            og_ref[gi, rows, :] = o / ssum
            lse_ref[gi, rows, :] = jnp.broadcast_to(m + jnp.log(ssum), (qb, DSA_HEAD_DIM))

        nblocks = dil * nblk

        def blocks_step(i, _, block=block):
            _interleave(block(i * DSA_INTERLEAVE + u) for u in range(DSA_INTERLEAVE))
            return 0

        lax.fori_loop(0, nblocks // DSA_INTERLEAVE, blocks_step, 0)

    lse = [lse_ref[gi] for gi in range(DSA_GROUPS)]
    mx = jnp.maximum(jnp.maximum(lse[0], lse[1]), lse[2])
    ws = [jnp.exp(l - mx) for l in lse]
    num = ws[0] * og_ref[0] + ws[1] * og_ref[1] + ws[2] * og_ref[2]
    y_ref[...] = (num / (ws[0] + ws[1] + ws[2])).astype(y_ref.dtype)


def dilated_attention(p, qn, kn, bias_tab, batch):
    t = p.shape[0]
    s = t // batch
    assert (s // DSA_QBLK) % DSA_INTERLEAVE == 0
    hd = DSA_HEAD_DIM
    nh = DSA_HEADS

    def col(part, gi):
        return pl.BlockSpec((s, hd), lambda b, j: (b, part * nh + gi * DSA_HPG + j))

    in_specs = [col(part, gi) for part in range(3) for gi in range(DSA_GROUPS)]
    in_specs += [pl.BlockSpec((1, hd), lambda b, j: (0, 0)), pl.BlockSpec((1, hd), lambda b, j: (0, 0)),
                 pl.BlockSpec((1, DSA_GROUPS, DSA_QBLK, DSA_QBLK + 2 * DSA_SIDE), lambda b, j: (j, 0, 0, 0))]
    big = pltpu.VMEM((s, hd), F32)
    pad = pltpu.VMEM((s + 2 * DSA_PAD, hd), F32)
    grp = pltpu.VMEM((DSA_GROUPS, s, hd), F32)
    return pl.pallas_call(
        functools.partial(_dsa_body, seq=s),
        grid=(batch, DSA_HPG),
        in_specs=in_specs,
        out_specs=pl.BlockSpec((s, hd), lambda b, j: (b, j)),
        out_shape=jax.ShapeDtypeStruct((t, DSA_HPG * hd), BF16),
        scratch_shapes=[big, pad, pad, grp, grp],
        compiler_params=_params("parallel", "parallel"),
        name="dilated_attention",
    )(*([p] * 9), qn.reshape(1, hd), kn.reshape(1, hd), bias_tab)


def _t5_bucket(rel):
    half = REL_BUCKETS // 2
    max_exact = half // 2
    n = np.abs(rel)
    scaled = (np.log(np.maximum(n, max_exact).astype(np.float32) / np.float32(max_exact))
              / np.float32(math.log(REL_MAX_DIST / max_exact)))
    large = np.minimum(max_exact + (scaled * np.float32(half - max_exact)).astype(np.int32), half - 1)
    return np.where(rel > 0, half, 0) + np.where(n < max_exact, n, large)


def _dsa_bias_table(rel_bias):
    width = DSA_QBLK + 2 * DSA_SIDE
    tabs = []
    for gi, (_, dil) in enumerate(DSA_PATTERNS):
        offs = np.arange(-DSA_SIDE, DSA_SIDE + 1, dtype=np.int32) * dil
        band = rel_bias[_t5_bucket(offs)][:, gi * DSA_HPG:(gi + 1) * DSA_HPG].astype(F32)
        fill = jnp.full((DSA_QBLK - 1, DSA_HPG), NEG_INF, F32)
        line = jnp.concatenate([fill, band, fill], axis=0)
        tabs.append(jnp.stack([line[DSA_QBLK - 1 - q:DSA_QBLK - 1 - q + width] for q in range(DSA_QBLK)]))
    return jnp.transpose(jnp.stack(tabs), (3, 0, 1, 2))


def _head_block_diag():
    ii = lax.broadcasted_iota(jnp.int32, (LANES, LANES), 0)
    jj = lax.broadcasted_iota(jnp.int32, (LANES, LANES), 1)
    return (ii // RW_HEAD_DIM) == (jj // RW_HEAD_DIM)


def _rw_prep_body(r_ref, k_ref, v_ref, lo_ref, mur_ref, muk_ref, muv_ref, mulo_ref, w0_ref, w2_ref, a0_ref,
                  a2_ref, g2_ref, kk_ref, ka_ref, rk_ref,
                  ro_ref, vo_ref, kko_ref, bon_ref, gate_ref, lwf_ref, lwb_ref, kdf_ref, kdb_ref, bbf_ref, bbb_ref):
    s = r_ref.shape[0]
    row = lax.broadcasted_iota(jnp.int32, (s, 1), 0)

    def mix(t, mu):
        return t + mu * (0.5 * (_shift_down(t, row) + _shift_up(t, row)) - t)

    r = mix(r_ref[...].astype(F32), mur_ref[...])
    kr = mix(k_ref[...].astype(F32), muk_ref[...])
    v = mix(v_ref[...].astype(F32), muv_ref[...])
    lo = mix(lo_ref[...], mulo_ref[...])
    bd = jnp.where(_head_block_diag(), 1.0, 0.0)

    def head_sum(t):
        return _dot_sum(t, bd)

    kk = kr * kk_ref[...]
    kk = kk * lax.rsqrt(head_sum(kk * kk) + L2_EPS)
    gd = lo[:, 4 * RW_LORA:]
    gate_ref[...] = _dot1(_sigmoid(gd), g2_ref[...])
    ro_ref[...] = r
    vo_ref[...] = v
    kko_ref[...] = kk
    bonus = jnp.zeros_like(r)
    outs = ((lwf_ref, kdf_ref, bbf_ref), (lwb_ref, kdb_ref, bbb_ref))
    for d in range(2):
        wd = lo[:, d * RW_LORA:(d + 1) * RW_LORA]
        ad = lo[:, (2 + d) * RW_LORA:(3 + d) * RW_LORA]
        w_log = -_softplus(-(w0_ref[d:d + 1, :] + _dot1(jnp.tanh(wd), w2_ref[d]))) - 0.5
        a = _sigmoid(a0_ref[d:d + 1, :] + _dot1(ad, a2_ref[d]))
        kd = kr * (1.0 + (a - 1.0) * ka_ref[...])
        lw_ref, kd_ref, bb_ref = outs[d]
        lw_ref[...] = -jnp.exp(w_log)
        kd_ref[...] = kd
        bb_ref[...] = kk * a
        bonus = bonus + head_sum(r * kd * rk_ref[...]) * v
    bon_ref[...] = bonus


def rwkv_prep(p, lora_in, mu, w0, w2, a0, a2, g2, k_k, k_a, r_k, col0, batch):
    t = p.shape[0]
    s = t // batch
    nct = RW_WIDTH // LANES
    base = col0 // LANES
    nlo = RW_LORA_IN
    col = lambda off: pl.BlockSpec((s, LANES), lambda b, c: (b, base + off * nct + c))
    vec = lambda off: pl.BlockSpec((1, LANES), lambda b, c: (0, off * nct + c))
    mu_main = mu[:RW_MAIN].reshape(1, RW_MAIN)
    mu_lo = mu[RW_MAIN:].reshape(1, nlo)
    out_spec = pl.BlockSpec((s, LANES), lambda b, c: (b, c))
    n_out = 11
    return pl.pallas_call(
        _rw_prep_body,
        grid=(batch, nct),
        in_specs=[col(0), col(1), col(2),
                  pl.BlockSpec((s, nlo), lambda b, c: (b, 0)),
                  vec(0), vec(1), vec(2),
                  pl.BlockSpec((1, nlo), lambda b, c: (0, 0)),
                  pl.BlockSpec((2, LANES), lambda b, c: (0, c)),
                  pl.BlockSpec((2, RW_LORA, LANES), lambda b, c: (0, 0, c)),
                  pl.BlockSpec((2, LANES), lambda b, c: (0, c)),
                  pl.BlockSpec((2, RW_LORA, LANES), lambda b, c: (0, 0, c)),
                  pl.BlockSpec((RW_GATE_LORA, LANES), lambda b, c: (0, c)),
                  vec(0), vec(0), vec(0)],
        out_specs=[out_spec] * n_out,
        out_shape=[jax.ShapeDtypeStruct((t, RW_WIDTH), F32)] * n_out,
        compiler_params=_params("parallel", "parallel"),
        name="rwkv_prep",
    )(p, p, p, lora_in, mu_main, mu_main, mu_main, mu_lo, w0, w2, a0, a2, g2,
      k_k.reshape(1, RW_WIDTH), k_a.reshape(1, RW_WIDTH), r_k.reshape(1, RW_WIDTH))


def _rw_scan_body(r_ref, v_ref, kk_ref, bon_ref, gate_ref, lwf_ref, lwb_ref, kdf_ref, kdb_ref, bbf_ref, bbb_ref,
                  lnw_ref, lnb_ref, y_ref, cum_ref, cumt_ref, kdt_ref, bbt_ref, yf_ref, yb_ref,
                  tr_ref, tc_ref, rbk_ref, bkt_ref):
    s = r_ref.shape[0]
    c = CHUNK
    c2 = 2 * c
    row = lax.broadcasted_iota(jnp.int32, (s, 1), 0)
    pos = row % c
    lw_refs, kd_refs, bb_refs = (lwf_ref, lwb_ref), (kdf_ref, kdb_ref), (bbf_ref, bbb_ref)
    for d in range(2):
        cum = _seg_cumsum(lw_refs[d][...], pos, d == 1)
        cum_ref[d] = cum
        cumt_ref[d] = cum.T
        kdt_ref[d] = kd_refs[d][...].T
        bbt_ref[d] = bb_refs[d][...].T

    masks = _tri_masks(c2, c)
    merge = {rev: _merge_masks(c2, rev) for rev in (False, True)}
    eye2_f = jnp.where(masks["eye"], 1.0, 0.0)
    head0 = lax.broadcasted_iota(jnp.int32, (1, LANES), 1) < RW_HEAD_DIM
    head_bd = _head_block_diag()
    npair = s // PAIR
    group = min(RW_PREP_PAIRS, npair)

    def stack(x):
        return jnp.concatenate([jnp.where(head0, x, 0.0), jnp.where(head0, 0.0, x)], axis=0)

    def prep(pidx, d, half):
        reverse = d == 1
        incl2, strict2 = masks[reverse]
        cidx = 2 * pidx + half
        cols = pl.ds(pl.multiple_of(pidx * PAIR, PAIR), PAIR)
        rows = pl.ds(pl.multiple_of(cidx * c, c), c)
        hs = slice(half * c, (half + 1) * c)
        r, v, kk = r_ref[rows, :], v_ref[rows, :], kk_ref[rows, :]
        lw, kd, bb = lw_refs[d][rows, :], kd_refs[d][rows, :], bb_refs[d][rows, :]
        cum = cum_ref[d, rows, :]
        e_neg = jnp.exp(-cum)
        a_s = stack(-kk * jnp.exp(cum - lw))
        r_s = stack(r * jnp.exp(cum))
        ar = jnp.concatenate([a_s, r_s], axis=0)
        bk = jnp.concatenate([stack(bb * e_neg), stack(kd * e_neg)], axis=0)
        g = _dot1(ar, bk, _NT)
        yield
        rbk_ref[d, cidx] = jnp.concatenate([jnp.where(incl2, g[c2:, :c2], 0.0),
                                            jnp.where(incl2, g[c2:, c2:], 0.0)], axis=1).astype(BF16)
        av = _dot1(jnp.where(strict2, g[:c2, c2:], 0.0), stack(v))
        yield
        t_inv = []
        yield from _tri_inverse(-jnp.where(strict2, g[:c2, :c2], 0.0), eye2_f, merge[reverse], t_inv)
        tt = _dot1(t_inv[0], jnp.concatenate([a_s, av], axis=1))
        yield
        tr_ref[d, cidx] = jnp.concatenate([tt[:, :LANES], r_s], axis=0).astype(BF16)
        tc_ref[d, cidx] = tt[:, LANES:].astype(BF16)
        cum_t = cumt_ref[d, :, cols][:, hs]
        last = 0 if reverse else c - 1
        e_out_t = jnp.exp(cum_t[:, last:last + 1] - cum_t)
        bkt_ref[d, cidx] = jnp.concatenate([bbt_ref[d, :, cols][:, hs] * e_out_t,
                                            kdt_ref[d, :, cols][:, hs] * e_out_t], axis=1).astype(BF16)

    def seq(states, pidx, d):
        cum_t2 = cumt_ref[d, :, pl.ds(pl.multiple_of(pidx * PAIR, PAIR), PAIR)]
        for half in ((1, 0) if d else (0, 1)):
            cidx = 2 * pidx + half
            rows = pl.ds(pl.multiple_of(cidx * c, c), c)
            v = v_ref[rows, :]
            x = _dot1(tr_ref[d, cidx], states[d])
            yield
            ps = x[:c2] + tc_ref[d, cidx].astype(F32)
            os_ = x[c2:] + _dot1(rbk_ref[d, cidx], jnp.concatenate([ps, stack(v)], axis=0))
            y = os_[:c] + os_[c:]
            if d:
                yb_ref[rows, :] = y
            else:
                yf_ref[rows, :] = y
            pv = jnp.concatenate([ps[:c] + ps[c:], v], axis=0)
            last = half * c + (0 if d else c - 1)
            e_tot = jnp.exp(cum_t2[:, last:last + 1])
            states[d] = states[d] * e_tot + jnp.where(head_bd, _dot1(bkt_ref[d, cidx], pv), 0.0)
            yield

    ngroups = npair // group

    def prep_chains(g):
        return [prep(npair - 1 - (g * group + j) if d else g * group + j, d, half)
                for j in range(group) for d in range(2) for half in range(2)]

    def seq_chain(states, g, d):
        for j in range(group):
            yield from seq(states, npair - 1 - (g * group + j) if d else g * group + j, d)

    def step(g, carry):
        states = list(carry)
        _interleave(prep_chains(g + 1) + [seq_chain(states, g, 0), seq_chain(states, g, 1)])
        return tuple(states)

    zero = jnp.zeros((LANES, LANES), F32)
    _interleave(prep_chains(0))
    states = list(lax.fori_loop(0, ngroups - 1, step, (zero, zero)))
    _interleave([seq_chain(states, ngroups - 1, 0), seq_chain(states, ngroups - 1, 1)])

    bd = jnp.where(head_bd, 1.0, 0.0)
    y = yf_ref[...] + yb_ref[...]
    mean = _dot_sum(y, bd) * (1.0 / RW_HEAD_DIM)
    yc = y - mean
    var = _dot_sum(yc * yc, bd) * (1.0 / RW_HEAD_DIM)
    yn = yc * lax.rsqrt(var + RW_GN_EPS) * lnw_ref[...] + lnb_ref[...]
    y_ref[...] = ((yn + bon_ref[...]) * gate_ref[...]).astype(y_ref.dtype)


def rwkv_scan(prep, ln_w, ln_b, batch):
    t = prep[0].shape[0]
    s = t // batch
    assert s % PAIR == 0 and (s // PAIR) % min(RW_PREP_PAIRS, s // PAIR) == 0
    nct = RW_WIDTH // LANES
    nchunk = s // CHUNK
    blk = pl.BlockSpec((s, LANES), lambda b, c: (b, c))
    vec = pl.BlockSpec((1, LANES), lambda b, c: (0, c))
    tsp = pltpu.VMEM((2, LANES, s), F32)
    big = pltpu.VMEM((s, LANES), F32)
    per_chunk = lambda rows, cols: pltpu.VMEM((2, nchunk, rows, cols), BF16)
    return pl.pallas_call(
        _rw_scan_body,
        grid=(batch, nct),
        in_specs=[blk] * 11 + [vec, vec],
        out_specs=blk,
        out_shape=jax.ShapeDtypeStruct((t, RW_WIDTH), BF16),
        scratch_shapes=[pltpu.VMEM((2, s, LANES), F32), tsp, tsp, tsp, big, big,
                        per_chunk(2 * PAIR, LANES), per_chunk(PAIR, LANES), per_chunk(PAIR, 2 * LANES),
                        per_chunk(PAIR, LANES)],
        compiler_params=_params("parallel", "parallel"),
        name="rwkv_scan",
    )(*prep, ln_w.reshape(1, RW_WIDTH), ln_b.reshape(1, RW_WIDTH))


def _even_weights_body(w_ref, o_ref):
    c_ab = 4 * DN_WIDTH
    o_ref[:, :c_ab] = w_ref[:, :c_ab].astype(BF16)
    o_ref[:, c_ab:] = w_ref[:, c_ab + 4 * DN_HEADS:].astype(BF16)


def even_in_weights(w_in, rows=256):
    nl, d, n = w_in.shape
    n_out = n - 4 * DN_HEADS
    return pl.pallas_call(
        _even_weights_body,
        grid=(nl, d // rows),
        in_specs=[pl.BlockSpec((None, rows, n), lambda i, r: (i, r, 0))],
        out_specs=pl.BlockSpec((None, rows, n_out), lambda i, r: (i, r, 0)),
        out_shape=jax.ShapeDtypeStruct((nl, d, n_out), BF16),
        compiler_params=_params("parallel", "parallel"),
        name="even_in_weights",
    )(w_in)


def even_mixer(x, g, w_in, w_ab, w_out, layer, conv_qkv, a_log, dt_bias, out_gain, conv_sc, batch):
    p, ab = norm_mm(x, g, w_in, layer, wp_t=w_ab.T, out_dtype=BF16)
    zeros = jnp.zeros((2 * DN_HEADS,), F32)
    alog32 = jnp.concatenate([zeros, a_log.reshape(-1)]).reshape(1, -1)
    dt32 = jnp.concatenate([zeros, dt_bias.reshape(-1)]).reshape(1, -1)
    y_dn = deltanet(p, ab, conv_qkv.T, alog32, dt32, out_gain, batch)
    y_sc = short_conv(p, conv_sc.T, 4 * DN_WIDTH, batch)
    return mm_res(y_dn, w_out, layer, x, a2=y_sc, tn=D_MODEL)


def odd_mixer(x, g, w_in, w_lora, w_out, layer, qn, kn, bias_tab, mu, w0, w2, a0, a2, g2, k_k, k_a, r_k, ln_w, ln_b,
              batch):
    p, lora_in = norm_mm(x, g, w_in, layer, ncols=3 * DSA_QKV + RW_MAIN, wp=w_lora, out_dtype=BF16)
    y_c = dilated_attention(p, qn, kn, bias_tab, batch)
    prep = rwkv_prep(p, lora_in, mu, w0, w2, a0, a2, g2, k_k, k_a, r_k, 3 * DSA_QKV, batch)
    y_d = rwkv_scan(prep, ln_w, ln_b, batch)
    return mm_res(y_c, w_out, layer, x, a2=y_d, tn=D_MODEL)


def kernel(x, mem, rel_bias, norm_mix, norm_xattn, norm_mem, norm_ffn, xa_wq, xa_wk, xa_wv, xa_wo, xa_qn, xa_kn, ffn_w1, ffn_w2, ev_w_in, ev_w_out, dn_conv, dn_a_log, dn_dt_bias, dn_norm, sc_conv, od_w_in, od_w_out, ca_qn, ca_kn, rw_mu, rw_w0, rw_w2, rw_a0, rw_a2, rw_g2, rw_k_k, rw_k_a, rw_r_k, rw_ln_w, rw_ln_b):
    batch, seq, d = x.shape
    n_mem = mem.shape[1]
    xf = x.reshape(batch * seq, d)
    memf = mem.reshape(batch * n_mem, d)
    bias_tab = _dsa_bias_table(rel_bias)
    c_ab = 4 * DN_WIDTH
    c_sc = c_ab + 4 * DN_HEADS
    c_lo = 3 * DSA_QKV + RW_MAIN
    ev_in, od_in = even_in_weights(ev_w_in), od_w_in.astype(BF16)
    ev_out, od_out = ev_w_out.astype(BF16), od_w_out.astype(BF16)
    w_q, w_o = xa_wq.astype(BF16), xa_wo.astype(BF16)
    w_kv = jnp.concatenate([xa_wk, xa_wv], axis=2).astype(BF16)
    for layer in range(DEPTH):
        i = layer // 2
        if layer % 2 == 0:
            xf = even_mixer(xf, norm_mix[layer], ev_in, ev_w_in[i, :, c_ab:c_sc], ev_out, i, dn_conv[i],
                            dn_a_log[i], dn_dt_bias[i], dn_norm[i], sc_conv[i], batch)
        else:
            xf = odd_mixer(xf, norm_mix[layer], od_in, od_w_in[i, :, c_lo:], od_out, i, ca_qn[i], ca_kn[i], bias_tab,
                           rw_mu[i], rw_w0[i], rw_w2[i], rw_a0[i], rw_a2[i], rw_g2[i], rw_k_k[i], rw_k_a[i], rw_r_k[i],
                           rw_ln_w[i], rw_ln_b[i], batch)
        kv = norm_mm(memf, norm_mem[layer], w_kv, layer).reshape(batch, n_mem, 2 * XA_WIDTH)
        xf = xattn(xf, norm_xattn[layer], w_q, kv, w_o, layer, xa_qn[layer], xa_kn[layer], batch)
        h1 = norm_mm(xf, norm_ffn[layer], ffn_w1, layer, act="relu2", out_dtype=BF16)
        xf = mm_res(h1, ffn_w2, layer, xf)
    return xf.reshape(batch, seq, d)
```

```python
import functools
import math

import jax
import jax.numpy as jnp
import numpy as np
from jax import lax
from jax.experimental import pallas as pl
from jax.experimental.pallas import tpu as pltpu

F32 = jnp.float32
BF16 = jnp.bfloat16

D_MODEL = 2048
DEPTH = 4
RMS_EPS = 1e-6
L2_EPS = 1e-6

DN_HEADS = 8
DN_HEAD_DIM = 128
DN_WIDTH = DN_HEADS * DN_HEAD_DIM
SC_WIDTH = D_MODEL - DN_WIDTH
CHUNK = 64
PAIR = 2 * CHUNK
DN_PREP_PAIRS = 4
DN_HEADS_PER_STEP = 2
RW_PREP_PAIRS = 4

DSA_PATTERNS = ((128, 1), (512, 4), (2048, 16))
DSA_GROUPS = len(DSA_PATTERNS)
DSA_HPG = 4
DSA_HEAD_DIM = 128
DSA_HEADS = DSA_GROUPS * DSA_HPG
DSA_QKV = DSA_HEADS * DSA_HEAD_DIM
DSA_SIDE = 64
DSA_QBLK = 128
DSA_PAD = DSA_SIDE * max(d for _, d in DSA_PATTERNS)
DSA_INTERLEAVE = 8
REL_BUCKETS = 32
REL_MAX_DIST = 1024
NEG_INF = -1e30

RW_HEADS = 8
RW_HEAD_DIM = 64
RW_WIDTH = RW_HEADS * RW_HEAD_DIM
RW_LORA = 64
RW_GATE_LORA = 128
RW_MAIN = 3 * RW_WIDTH
RW_LORA_IN = 4 * RW_LORA + RW_GATE_LORA
RW_GN_EPS = 64e-5

XA_HEADS = 4
XA_HEAD_DIM = 128
XA_WIDTH = XA_HEADS * XA_HEAD_DIM

LANES = 128
SUBLANES = 8
NORM_ROWS = 256
VMEM_LIMIT_BYTES = 58 * 1024 * 1024


def _params(*sem):
    return pltpu.CompilerParams(dimension_semantics=sem, vmem_limit_bytes=VMEM_LIMIT_BYTES)


_NN = (((1,), (0,)), ((), ()))
_NT = (((1,), (1,)), ((), ()))


def _dot1(a, b, dims=_NN):
    return lax.dot_general(a.astype(BF16), b.astype(BF16), dims, preferred_element_type=F32)


def _dot_sum(a, b01):
    b = b01.astype(BF16)
    hi = a.astype(BF16)
    lo = (a - hi.astype(F32)).astype(BF16)
    dg = functools.partial(lax.dot_general, dimension_numbers=_NN, preferred_element_type=F32)
    return dg(hi, b) + dg(lo, b)


def _interleave(chains):
    chains = list(chains)
    while chains:
        alive = []
        for ch in chains:
            try:
                next(ch)
                alive.append(ch)
            except StopIteration:
                pass
        chains = alive


def _merge_masks(n, reverse):
    ii = lax.broadcasted_iota(jnp.int32, (n, n), 0)
    jj = lax.broadcasted_iota(jnp.int32, (n, n), 1)
    tri = (ii < jj) if reverse else (ii > jj)
    out = []
    size = 1
    while size < CHUNK:
        out.append(tri & ((ii // (2 * size)) == (jj // (2 * size))) & ((ii // size) != (jj // size)))
        size *= 2
    return out


def _tri_inverse(a, eye_f, level_masks, out):
    t = eye_f - jnp.where(level_masks[0], a, 0.0)
    for m in level_masks[1:]:
        x = _dot1(jnp.where(m, a, 0.0), t)
        yield
        t = t - _dot1(t, x)
        yield
    out.append(t)


def _sigmoid(x):
    return 1.0 / (1.0 + jnp.exp(-x))


def _softplus(x):
    return jnp.maximum(x, 0.0) + jnp.log1p(jnp.exp(-jnp.abs(x)))


def _shift_down(x, row):
    return jnp.where(row == 0, 0.0, pltpu.roll(x, 1, 0))


def _shift_up(x, row):
    n = x.shape[0]
    return jnp.where(row == n - 1, 0.0, pltpu.roll(x, n - 1, 0))


def _norm_mm_body(x_ref, g_ref, w_ref, *rest, act, extra, extra_t, n_first):
    if n_first is not None:
        w2_ref, rest = rest[0], rest[1:]
    if extra:
        wp_ref, o_ref, op_ref, xn_ref = rest
    else:
        o_ref, xn_ref = rest

    @pl.when(pl.program_id(1) == 0)
    def _():
        def rows_step(r, _):
            rows = pl.ds(pl.multiple_of(r * NORM_ROWS, NORM_ROWS), NORM_ROWS)
            x = x_ref[rows, :]
            xn = x * lax.rsqrt(jnp.mean(x * x, axis=-1, keepdims=True) + RMS_EPS) * g_ref[...]
            xn_ref[rows, :] = xn.astype(BF16)
            if extra and extra_t:
                op_ref[:, rows] = _dot1(wp_ref[...], xn, _NT)
            elif extra:
                op_ref[rows, :] = _dot1(xn, wp_ref[...])
            return 0

        lax.fori_loop(0, x_ref.shape[0] // NORM_ROWS, rows_step, 0)

    def project(wt_ref):
        acc = jnp.dot(xn_ref[...], wt_ref[...].astype(BF16), preferred_element_type=F32)
        if act == "relu2":
            acc = jnp.square(jnp.maximum(acc, 0.0))
        o_ref[...] = acc.astype(o_ref.dtype)

    if n_first is None:
        project(w_ref)
    else:
        pl.when(pl.program_id(1) < n_first)(lambda: project(w_ref))
        pl.when(pl.program_id(1) >= n_first)(lambda: project(w2_ref))


def norm_mm(x, g, w, layer, ncols=None, w2=None, wp=None, wp_t=None, act=None, out_dtype=F32, tm=1024, tn=1024):
    t, d = x.shape
    n1 = w.shape[2] if ncols is None else ncols
    n = n1 + (0 if w2 is None else w2.shape[2])
    tm = min(tm, t)
    tn = min(tn, n1)
    assert t % tm == 0 and n1 % tn == 0 and n % tn == 0 and tm % NORM_ROWS == 0
    extra = wp is not None or wp_t is not None
    n_first = None if w2 is None else n1 // tn
    in_specs = [
        pl.BlockSpec((tm, d), lambda i, j: (i, 0)),
        pl.BlockSpec((1, d), lambda i, j: (0, 0)),
    ]
    args = [x, g.reshape(1, d), w]
    if w2 is None:
        in_specs.append(pl.BlockSpec((None, d, tn), lambda i, j: (layer, 0, j)))
    else:
        in_specs.append(pl.BlockSpec((None, d, tn), lambda i, j: (layer, 0, jnp.minimum(j, n_first - 1))))
        in_specs.append(pl.BlockSpec((None, d, tn), lambda i, j: (layer, 0, jnp.maximum(j - n_first, 0))))
        args.append(w2)
    out_specs = pl.BlockSpec((tm, tn), lambda i, j: (i, j))
    out_shape = jax.ShapeDtypeStruct((t, n), out_dtype)
    if wp is not None:
        npc = wp.shape[1]
        in_specs.append(pl.BlockSpec((d, npc), lambda i, j: (0, 0)))
        out_specs = [out_specs, pl.BlockSpec((tm, npc), lambda i, j: (i, 0))]
        out_shape = [out_shape, jax.ShapeDtypeStruct((t, npc), F32)]
        args.append(wp)
    elif wp_t is not None:
        npc = wp_t.shape[0]
        in_specs.append(pl.BlockSpec((npc, d), lambda i, j: (0, 0)))
        out_specs = [out_specs, pl.BlockSpec((npc, tm), lambda i, j: (0, i))]
        out_shape = [out_shape, jax.ShapeDtypeStruct((npc, t), F32)]
        args.append(wp_t)
    return pl.pallas_call(
        functools.partial(_norm_mm_body, act=act, extra=extra, extra_t=wp_t is not None, n_first=n_first),
        grid=(t // tm, n // tn),
        in_specs=in_specs,
        out_specs=out_specs,
        out_shape=out_shape,
        scratch_shapes=[pltpu.VMEM((tm, d), BF16)],
        compiler_params=_params("parallel", "arbitrary"),
        name="norm_mm",
    )(*args)


def _act_mm_body(h_ref, w_ref, o_ref, *, act):
    acc = jnp.dot(h_ref[...], w_ref[...].astype(BF16), preferred_element_type=F32)
    if act == "relu2":
        acc = jnp.square(jnp.maximum(acc, 0.0))
    o_ref[...] = acc.astype(o_ref.dtype)


def act_mm(h, w, layer, act=None, out_dtype=F32, tm=2048, tn=1024):
    t, d = h.shape
    n = w.shape[2]
    tm, tn = min(tm, t), min(tn, n)
    assert t % tm == 0 and n % tn == 0
    return pl.pallas_call(
        functools.partial(_act_mm_body, act=act),
        grid=(t // tm, n // tn),
        in_specs=[pl.BlockSpec((tm, d), lambda i, j: (i, 0)),
                  pl.BlockSpec((None, d, tn), lambda i, j: (layer, 0, j))],
        out_specs=pl.BlockSpec((tm, tn), lambda i, j: (i, j)),
        out_shape=jax.ShapeDtypeStruct((t, n), out_dtype),
        compiler_params=_params("parallel", "arbitrary"),
        name="act_mm",
    )(h, w)


def _mm_res_body(a_ref, *rest, n_first):
    if n_first is not None:
        a2_ref, rest = rest[0], rest[1:]
    w_ref, r_ref, o_ref = rest
    kk = pl.program_id(2)

    @pl.when(kk == 0)
    def _():
        o_ref[...] = r_ref[...]

    def accumulate(lhs_ref):
        o_ref[...] += jnp.dot(lhs_ref[...], w_ref[...].astype(BF16), preferred_element_type=F32)

    if n_first is None:
        accumulate(a_ref)
    else:
        pl.when(kk < n_first)(lambda: accumulate(a_ref))
        pl.when(kk >= n_first)(lambda: accumulate(a2_ref))


def mm_res(a, w, layer, res, a2=None, tm=1024, tn=1024, tk=2048):
    t, k1 = a.shape
    k = k1 + (0 if a2 is None else a2.shape[1])
    n = w.shape[2]
    tm, tn, tk = min(tm, t), min(tn, n), min(tk, k1)
    assert t % tm == 0 and n % tn == 0 and k1 % tk == 0 and k % tk == 0
    n_first = None if a2 is None else k1 // tk
    if a2 is None:
        in_specs = [pl.BlockSpec((tm, tk), lambda i, j, kk: (i, kk))]
        args = [a]
    else:
        in_specs = [pl.BlockSpec((tm, tk), lambda i, j, kk: (i, jnp.minimum(kk, n_first - 1))),
                    pl.BlockSpec((tm, tk), lambda i, j, kk: (i, jnp.maximum(kk - n_first, 0)))]
        args = [a, a2]
    in_specs += [
        pl.BlockSpec((None, tk, tn), lambda i, j, kk: (layer, kk, j)),
        pl.BlockSpec((tm, tn), lambda i, j, kk: (i, j)),
    ]
    return pl.pallas_call(
        functools.partial(_mm_res_body, n_first=n_first),
        grid=(t // tm, n // tn, k // tk),
        in_specs=in_specs,
        out_specs=pl.BlockSpec((tm, tn), lambda i, j, kk: (i, j)),
        out_shape=jax.ShapeDtypeStruct((t, n), F32),
        compiler_params=_params("parallel", "parallel", "arbitrary"),
        name="mm_res",
    )(*args, w, res)


def _xattn_body(x_ref, g_ref, wq_ref, kv_ref, wo_ref, qn_ref, kn_ref, g2_ref, o_ref, hn_ref):
    x = x_ref[...]
    xn = x * lax.rsqrt(jnp.mean(x * x, axis=-1, keepdims=True) + RMS_EPS) * g_ref[...]
    q = jnp.dot(xn.astype(BF16), wq_ref[...], preferred_element_type=F32)
    kv = kv_ref[0]
    outs = []
    for h in range(XA_HEADS):
        sl = slice(h * XA_HEAD_DIM, (h + 1) * XA_HEAD_DIM)
        qh = q[:, sl]
        qh = qh * lax.rsqrt(jnp.mean(qh * qh, axis=-1, keepdims=True) + RMS_EPS) * qn_ref[...]
        kh = kv[:, sl]
        kh = kh * lax.rsqrt(jnp.mean(kh * kh, axis=-1, keepdims=True) + RMS_EPS) * kn_ref[...]
        vh = kv[:, XA_WIDTH + h * XA_HEAD_DIM:XA_WIDTH + (h + 1) * XA_HEAD_DIM]
        logits = _dot1(qh, kh, _NT) * (XA_HEAD_DIM ** -0.5)
        m = jnp.max(logits, axis=-1, keepdims=True)
        p = jnp.exp(logits - m)
        s = jnp.sum(p, axis=-1, keepdims=True)
        outs.append(_dot1(p, vh) / s)
    o = jnp.concatenate(outs, axis=-1).astype(BF16)
    xo = x + jnp.dot(o, wo_ref[...], preferred_element_type=F32)
    o_ref[...] = xo
    hn_ref[...] = (xo * lax.rsqrt(jnp.mean(xo * xo, axis=-1, keepdims=True) + RMS_EPS) * g2_ref[...]).astype(BF16)


def xattn(x, g, wq, kv, wo, layer, qn, kn, g_next, batch, ts=1024):
    t, d = x.shape
    s = t // batch
    ts = min(ts, s)
    nst = s // ts
    m = kv.shape[1]
    return pl.pallas_call(
        _xattn_body,
        grid=(batch, nst),
        in_specs=[
            pl.BlockSpec((ts, d), lambda b, i: (b * nst + i, 0)),
            pl.BlockSpec((1, d), lambda b, i: (0, 0)),
            pl.BlockSpec((None, d, XA_WIDTH), lambda b, i: (layer, 0, 0)),
            pl.BlockSpec((1, m, 2 * XA_WIDTH), lambda b, i: (b, 0, 0)),
            pl.BlockSpec((None, XA_WIDTH, d), lambda b, i: (layer, 0, 0)),
            pl.BlockSpec((1, XA_HEAD_DIM), lambda b, i: (0, 0)),
            pl.BlockSpec((1, XA_HEAD_DIM), lambda b, i: (0, 0)),
            pl.BlockSpec((1, d), lambda b, i: (0, 0)),
        ],
        out_specs=[pl.BlockSpec((ts, d), lambda b, i: (b * nst + i, 0))] * 2,
        out_shape=[jax.ShapeDtypeStruct((t, d), F32), jax.ShapeDtypeStruct((t, d), BF16)],
        compiler_params=_params("parallel", "parallel"),
        name="xattn",
    )(x, g.reshape(1, d), wq, kv, wo, qn.reshape(1, -1), kn.reshape(1, -1), g_next.reshape(1, d))


def _tri_masks(n, blk):
    ii = lax.broadcasted_iota(jnp.int32, (n, n), 0)
    jj = lax.broadcasted_iota(jnp.int32, (n, n), 1)
    same = (ii // blk) == (jj // blk) if n != blk else None

    def m(c):
        return c if same is None else (c & same)

    return {
        False: (m(ii >= jj), m(ii > jj)),
        True: (m(ii <= jj), m(ii < jj)),
        "eye": ii == jj,
    }


def _seg_cumsum(x, pos, reverse):
    n = x.shape[0]
    sh = 1
    while sh < CHUNK:
        if reverse:
            x = x + jnp.where(pos < CHUNK - sh, pltpu.roll(x, n - sh, 0), 0.0)
        else:
            x = x + jnp.where(pos >= sh, pltpu.roll(x, sh, 0), 0.0)
        sh *= 2
    return x


def _dn_body(q_ref, k_ref, v_ref, gate_ref, ab_ref, cq_ref, ck_ref, cv_ref, alog_ref, dt_ref, gain_ref,
             y_ref, qs_ref, ks_ref, vs_ref, kt_ref, bg_ref, gt_ref, of_ref, ob_ref,
             u_ref, w_ref, qg_ref, qk_ref, kdt_ref, eg_ref, st_ref):
    s = q_ref.shape[0]
    hd = DN_HEAD_DIM
    nh = DN_HEADS_PER_STEP
    row = lax.broadcasted_iota(jnp.int32, (s, 1), 0)
    c = CHUNK
    lane32 = lax.broadcasted_iota(jnp.int32, (1, 4 * DN_HEADS), 1)
    lane128 = lax.broadcasted_iota(jnp.int32, (1, LANES), 1)
    pos_t = lax.broadcasted_iota(jnp.int32, (1, s), 1) % c

    def conv_silu(x_ref, cw_ref, lanes):
        x = x_ref[:, lanes].astype(F32)
        w = cw_ref[:, lanes]
        y = _shift_down(x, row) * w[0:1] + x * w[1:2] + _shift_up(x, row) * w[2:3]
        return y * _sigmoid(y)

    for hh in range(nh):
        h = pl.program_id(1) * nh + hh
        lanes = slice(hh * hd, (hh + 1) * hd)
        q = conv_silu(q_ref, cq_ref, lanes)
        q = q * lax.rsqrt(jnp.sum(q * q, axis=-1, keepdims=True) + L2_EPS) * (hd ** -0.5)
        qs_ref[hh] = q
        k = conv_silu(k_ref, ck_ref, lanes)
        k = k * lax.rsqrt(jnp.sum(k * k, axis=-1, keepdims=True) + L2_EPS)
        ks_ref[hh] = k
        kt_ref[hh] = k.T
        vs_ref[hh] = conv_silu(v_ref, cv_ref, lanes)

        def decay_row(d, h=h):
            idx = (2 + d) * DN_HEADS + h
            a_log = jnp.sum(jnp.where(lane32 == idx, alog_ref[...], 0.0), axis=-1, keepdims=True)
            dt = jnp.sum(jnp.where(lane32 == idx, dt_ref[...], 0.0), axis=-1, keepdims=True)
            g = -jnp.exp(a_log) * _softplus(ab_ref[pl.ds(idx, 1), :] + dt)
            sh = 1
            while sh < c:
                if d:
                    g = g + jnp.where(pos_t < c - sh, pltpu.roll(g, s - sh, 1), 0.0)
                else:
                    g = g + jnp.where(pos_t >= sh, pltpu.roll(g, sh, 1), 0.0)
                sh *= 2
            return g

        rows = [_sigmoid(ab_ref[pl.ds(h, 1), :]), _sigmoid(ab_ref[pl.ds(DN_HEADS + h, 1), :]),
                decay_row(0), decay_row(1)]
        gt = jnp.concatenate(rows + [jnp.zeros((LANES - len(rows), s), F32)], axis=0)
        gt_ref[hh] = gt[0:SUBLANES, :]
        bg_ref[hh] = gt.T
        for d in range(2):
            st_ref[hh, d] = jnp.zeros((hd, hd), F32)

    masks = _tri_masks(PAIR, c)
    merge = {rev: _merge_masks(PAIR, rev) for rev in (False, True)}
    eye_f = jnp.where(masks["eye"], 1.0, 0.0)
    npair = s // PAIR
    group = min(DN_PREP_PAIRS, npair)

    def prep(hh, pidx, d):
        reverse = d == 1
        rows = pl.ds(pl.multiple_of(pidx * PAIR, PAIR), PAIR)
        qp, kp, vp = qs_ref[hh, rows, :], ks_ref[hh, rows, :], vs_ref[hh, rows, :]
        bgp = bg_ref[hh, rows, :]
        beta = bgp[:, d:d + 1]
        gcol = bgp[:, 2 + d:3 + d]
        grow = gt_ref[hh, 2 + d:3 + d, rows]
        incl, strict = masks[reverse]
        decay = jnp.where(incl, jnp.exp(jnp.where(incl, gcol - grow, 0.0)), 0.0)
        kb = kp * beta
        kq = _dot1(jnp.concatenate([kb, qp], axis=0), kp, _NT)
        yield
        qk_ref[hh, d, rows, :] = jnp.where(incl, kq[PAIR:] * decay, 0.0).astype(BF16)
        t_inv = []
        yield from _tri_inverse(jnp.where(strict, kq[:PAIR] * decay, 0.0), eye_f, merge[reverse], t_inv)
        egc = jnp.exp(gcol)
        uw = _dot1(t_inv[0], jnp.concatenate([vp * beta, kb * egc], axis=1))
        yield
        u_ref[hh, d, rows, :] = uw[:, :hd]
        w_ref[hh, d, rows, :] = uw[:, hd:].astype(BF16)
        qg_ref[hh, d, rows, :] = (qp * egc).astype(BF16)
        last0, last1 = (0, c) if reverse else (c - 1, PAIR - 1)
        glast = jnp.where(lane128 < c, grow[:, last0:last0 + 1], grow[:, last1:last1 + 1])
        kdt_ref[hh, d, :, rows] = (kt_ref[hh, :, rows] * jnp.exp(glast - grow)).astype(BF16)
        eg_ref[hh, d, :, rows] = jnp.broadcast_to(jnp.exp(glast), (SUBLANES, PAIR))

    def seq(hh, pidx, d):
        pair_rows = pl.ds(pl.multiple_of(pidx * PAIR, PAIR), PAIR)
        kdt = kdt_ref[hh, d, :, pair_rows]
        eg = eg_ref[hh, d, 0:1, pair_rows]
        for half in ((1, 0) if d else (0, 1)):
            r0 = half * c
            rows = pl.ds(pl.multiple_of(pidx * PAIR + r0, c), c)
            state = st_ref[hh, d]
            ws = _dot1(jnp.concatenate([w_ref[hh, d, rows, :], qg_ref[hh, d, rows, :]], axis=0), state)
            yield
            v_new = u_ref[hh, d, rows, :] - ws[:c]
            o = ws[c:] + _dot1(qk_ref[hh, d, rows, r0:r0 + c], v_new)
            if d:
                ob_ref[hh, rows, :] = o
            else:
                of_ref[hh, rows, :] = o
            st_ref[hh, d] = state * eg[:, r0:r0 + 1] + _dot1(kdt[:, r0:r0 + c], v_new)
            yield

    ngroups = npair // group

    def pair_of(g, j, d):
        return npair - 1 - (g * group + j) if d else g * group + j

    def prep_chains(g):
        return [prep(hh, pair_of(g, j, d), d) for j in range(group) for hh in range(nh) for d in range(2)]

    def seq_chain(g, hh, d):
        for j in range(group):
            yield from seq(hh, pair_of(g, j, d), d)

    def seq_chains(g):
        return [seq_chain(g, hh, d) for hh in range(nh) for d in range(2)]

    def step(g, _):
        _interleave(prep_chains(g + 1) + seq_chains(g))
        return 0

    _interleave(prep_chains(0))
    lax.fori_loop(0, ngroups - 1, step, 0)
    _interleave(seq_chains(ngroups - 1))

    for hh in range(nh):
        lanes = slice(hh * hd, (hh + 1) * hd)
        o = of_ref[hh] + ob_ref[hh]
        o = o * lax.rsqrt(jnp.mean(o * o, axis=-1, keepdims=True) + RMS_EPS) * gain_ref[...]
        gate = gate_ref[:, lanes].astype(F32)
        y_ref[:, lanes] = (o * (gate * _sigmoid(gate))).astype(y_ref.dtype)


def deltanet(p, ab, conv_t, alog32, dt32, gain, batch):
    t = p.shape[0]
    s = t // batch
    assert s % PAIR == 0 and (s // PAIR) % min(DN_PREP_PAIRS, s // PAIR) == 0
    hd = DN_HEAD_DIM
    nh = DN_HEADS_PER_STEP
    nsteps = DN_HEADS // nh
    col = lambda off: pl.BlockSpec((s, nh * hd), lambda b, h: (b, off * nsteps + h))
    cw = lambda off: pl.BlockSpec((3, nh * hd), lambda b, h: (0, off * nsteps + h))
    gate_col = pl.BlockSpec((s, nh * hd), lambda b, h: (b, 3 * nsteps + h), pipeline_mode=pl.Buffered(1))
    small = lambda n: pl.BlockSpec((1, n), lambda b, h: (0, 0))
    tok = pltpu.VMEM((nh, s, hd), F32)
    per_dir = pltpu.VMEM((nh, 2, s, hd), BF16)
    return pl.pallas_call(
        _dn_body,
        grid=(batch, nsteps),
        in_specs=[col(0), col(1), col(2), gate_col,
                  pl.BlockSpec((4 * DN_HEADS, s), lambda b, h: (0, b)),
                  cw(0), cw(1), cw(2),
                  small(4 * DN_HEADS), small(4 * DN_HEADS), small(hd)],
        out_specs=pl.BlockSpec((s, nh * hd), lambda b, h: (b, h)),
        out_shape=jax.ShapeDtypeStruct((t, DN_WIDTH), BF16),
        scratch_shapes=[tok, tok, tok, pltpu.VMEM((nh, hd, s), F32), tok, pltpu.VMEM((nh, SUBLANES, s), F32), tok, tok,
                        pltpu.VMEM((nh, 2, s, hd), F32), per_dir, per_dir, per_dir,
                        pltpu.VMEM((nh, 2, hd, s), BF16), pltpu.VMEM((nh, 2, SUBLANES, s), F32),
                        pltpu.VMEM((nh, 2, hd, hd), F32)],
        compiler_params=_params("parallel", "parallel"),
        name="deltanet",
    )(p, p, p, p, ab, conv_t, conv_t, conv_t, alog32, dt32, gain.reshape(1, hd))


def _sconv_body(b_ref, c_ref, u_ref, w_ref, y_ref):
    s = b_ref.shape[0]
    row = lax.broadcasted_iota(jnp.int32, (s, 1), 0)
    cu = c_ref[...].astype(F32) * u_ref[...].astype(F32)
    w = w_ref[...]
    y = _shift_down(cu, row) * w[0:1] + cu * w[1:2] + _shift_up(cu, row) * w[2:3]
    y_ref[...] = (b_ref[...].astype(F32) * y).astype(y_ref.dtype)


def short_conv(p, conv_t, col0, batch, tc=256):
    t = p.shape[0]
    s = t // batch
    nct = SC_WIDTH // tc
    base = col0 // tc
    col = lambda off: pl.BlockSpec((s, tc), lambda b, c: (b, base + off * nct + c))
    return pl.pallas_call(
        _sconv_body,
        grid=(batch, nct),
        in_specs=[col(0), col(1), col(2), pl.BlockSpec((3, tc), lambda b, c: (0, c))],
        out_specs=pl.BlockSpec((s, tc), lambda b, c: (b, c)),
        out_shape=jax.ShapeDtypeStruct((t, SC_WIDTH), BF16),
        compiler_params=_params("parallel", "parallel"),
        name="short_conv",
    )(p, p, p, conv_t)


def _dsa_body(*refs, seq):
    q_refs = refs[0:3]
    k_refs = refs[3:6]
    v_refs = refs[6:9]
    qn_ref, kn_ref, bias_ref, y_ref, qs_ref, kpad_ref, vpad_ref, og_ref, lse_ref = refs[9:]
    s = seq
    qb = DSA_QBLK
    side = DSA_SIDE
    width = qb + 2 * side
    kj = lax.broadcasted_iota(jnp.int32, (1, width), 1)
    zpad = jnp.zeros((DSA_PAD, DSA_HEAD_DIM), F32)
    for ref in (kpad_ref, vpad_ref):
        ref[0:DSA_PAD, :] = zpad
        ref[DSA_PAD + s:2 * DSA_PAD + s, :] = zpad

    for gi, (_, dil) in enumerate(DSA_PATTERNS):
        sub = s // dil
        nblk = sub // qb
        q = q_refs[gi][...].astype(F32)
        qs_ref[...] = (q * lax.rsqrt(jnp.mean(q * q, axis=-1, keepdims=True) + RMS_EPS) * qn_ref[...]
                       * (DSA_HEAD_DIM ** -0.5))
        k = k_refs[gi][...].astype(F32)
        kpad_ref[DSA_PAD:DSA_PAD + s, :] = k * lax.rsqrt(jnp.mean(k * k, axis=-1, keepdims=True) + RMS_EPS) * kn_ref[...]
        vpad_ref[DSA_PAD:DSA_PAD + s, :] = v_refs[gi][...].astype(F32)
        bias = bias_ref[0, gi]

        def block(t, gi=gi, dil=dil, sub=sub, nblk=nblk, bias=bias):
            r = t // nblk
            n = t % nblk
            rows = pl.ds(r + n * (qb * dil), qb, stride=dil)
            win = pl.ds(DSA_PAD + r + (n * qb - side) * dil, width, stride=dil)
            logits = _dot1(qs_ref[rows, :], kpad_ref[win, :], _NT) + bias
            yield
            pos = n * qb - side + kj
            logits = jnp.where((pos >= 0) & (pos < sub), logits, NEG_INF)
            m = jnp.max(logits, axis=-1, keepdims=True)
            yield
            p = jnp.exp(logits - m)
            ssum = jnp.sum(p, axis=-1, keepdims=True)
            o = _dot1(p, vpad_ref[win, :])
            yield
            og_ref[gi, rows, :] = o / ssum
            lse_ref[gi, rows, :] = jnp.broadcast_to(m + jnp.log(ssum), (qb, DSA_HEAD_DIM))

        nblocks = dil * nblk

        def blocks_step(i, _, block=block):
            _interleave(block(i * DSA_INTERLEAVE + u) for u in range(DSA_INTERLEAVE))
            return 0

        lax.fori_loop(0, nblocks // DSA_INTERLEAVE, blocks_step, 0)

    lse = [lse_ref[gi] for gi in range(DSA_GROUPS)]
    mx = jnp.maximum(jnp.maximum(lse[0], lse[1]), lse[2])
    ws = [jnp.exp(l - mx) for l in lse]
    num = ws[0] * og_ref[0] + ws[1] * og_ref[1] + ws[2] * og_ref[2]
    y_ref[...] = (num / (ws[0] + ws[1] + ws[2])).astype(y_ref.dtype)


def dilated_attention(p, qn, kn, bias_tab, batch):
    t = p.shape[0]
    s = t // batch
    assert (s // DSA_QBLK) % DSA_INTERLEAVE == 0
    hd = DSA_HEAD_DIM
    nh = DSA_HEADS

    def col(part, gi):
        return pl.BlockSpec((s, hd), lambda b, j: (b, part * nh + gi * DSA_HPG + j))

    in_specs = [col(part, gi) for part in range(3) for gi in range(DSA_GROUPS)]
    in_specs += [pl.BlockSpec((1, hd), lambda b, j: (0, 0)), pl.BlockSpec((1, hd), lambda b, j: (0, 0)),
                 pl.BlockSpec((1, DSA_GROUPS, DSA_QBLK, DSA_QBLK + 2 * DSA_SIDE), lambda b, j: (j, 0, 0, 0))]
    big = pltpu.VMEM((s, hd), F32)
    pad = pltpu.VMEM((s + 2 * DSA_PAD, hd), F32)
    grp = pltpu.VMEM((DSA_GROUPS, s, hd), F32)
    return pl.pallas_call(
        functools.partial(_dsa_body, seq=s),
        grid=(batch, DSA_HPG),
        in_specs=in_specs,
        out_specs=pl.BlockSpec((s, hd), lambda b, j: (b, j)),
        out_shape=jax.ShapeDtypeStruct((t, DSA_HPG * hd), BF16),
        scratch_shapes=[big, pad, pad, grp, grp],
        compiler_params=_params("parallel", "parallel"),
        name="dilated_attention",
    )(*([p] * 9), qn.reshape(1, hd), kn.reshape(1, hd), bias_tab)


def _t5_bucket(rel):
    half = REL_BUCKETS // 2
    max_exact = half // 2
    n = np.abs(rel)
    scaled = (np.log(np.maximum(n, max_exact).astype(np.float32) / np.float32(max_exact))
              / np.float32(math.log(REL_MAX_DIST / max_exact)))
    large = np.minimum(max_exact + (scaled * np.float32(half - max_exact)).astype(np.int32), half - 1)
    return np.where(rel > 0, half, 0) + np.where(n < max_exact, n, large)


def _dsa_bias_table(rel_bias):
    width = DSA_QBLK + 2 * DSA_SIDE
    tabs = []
    for gi, (_, dil) in enumerate(DSA_PATTERNS):
        offs = np.arange(-DSA_SIDE, DSA_SIDE + 1, dtype=np.int32) * dil
        band = rel_bias[_t5_bucket(offs)][:, gi * DSA_HPG:(gi + 1) * DSA_HPG].astype(F32)
        fill = jnp.full((DSA_QBLK - 1, DSA_HPG), NEG_INF, F32)
        line = jnp.concatenate([fill, band, fill], axis=0)
        tabs.append(jnp.stack([line[DSA_QBLK - 1 - q:DSA_QBLK - 1 - q + width] for q in range(DSA_QBLK)]))
    return jnp.transpose(jnp.stack(tabs), (3, 0, 1, 2))


def _head_block_diag():
    ii = lax.broadcasted_iota(jnp.int32, (LANES, LANES), 0)
    jj = lax.broadcasted_iota(jnp.int32, (LANES, LANES), 1)
    return (ii // RW_HEAD_DIM) == (jj // RW_HEAD_DIM)


def _rw_prep_body(r_ref, k_ref, v_ref, lo_ref, mur_ref, muk_ref, muv_ref, mulo_ref, w0_ref, w2_ref, a0_ref,
                  a2_ref, g2_ref, kk_ref, ka_ref, rk_ref,
                  ro_ref, vo_ref, kko_ref, bon_ref, gate_ref, lwf_ref, lwb_ref, kdf_ref, kdb_ref, bbf_ref, bbb_ref):
    s = r_ref.shape[0]
    row = lax.broadcasted_iota(jnp.int32, (s, 1), 0)

    def mix(t, mu):
        return t + mu * (0.5 * (_shift_down(t, row) + _shift_up(t, row)) - t)

    r = mix(r_ref[...].astype(F32), mur_ref[...])
    kr = mix(k_ref[...].astype(F32), muk_ref[...])
    v = mix(v_ref[...].astype(F32), muv_ref[...])
    lo = mix(lo_ref[...], mulo_ref[...])
    bd = jnp.where(_head_block_diag(), 1.0, 0.0)

    def head_sum(t):
        return _dot_sum(t, bd)

    kk = kr * kk_ref[...]
    kk = kk * lax.rsqrt(head_sum(kk * kk) + L2_EPS)
    gd = lo[:, 4 * RW_LORA:]
    gate_ref[...] = _dot1(_sigmoid(gd), g2_ref[...])
    ro_ref[...] = r
    vo_ref[...] = v
    kko_ref[...] = kk
    bonus = jnp.zeros_like(r)
    outs = ((lwf_ref, kdf_ref, bbf_ref), (lwb_ref, kdb_ref, bbb_ref))
    for d in range(2):
        wd = lo[:, d * RW_LORA:(d + 1) * RW_LORA]
        ad = lo[:, (2 + d) * RW_LORA:(3 + d) * RW_LORA]
        w_log = -_softplus(-(w0_ref[d:d + 1, :] + _dot1(jnp.tanh(wd), w2_ref[d]))) - 0.5
        a = _sigmoid(a0_ref[d:d + 1, :] + _dot1(ad, a2_ref[d]))
        kd = kr * (1.0 + (a - 1.0) * ka_ref[...])
        lw_ref, kd_ref, bb_ref = outs[d]
        lw_ref[...] = -jnp.exp(w_log)
        kd_ref[...] = kd
        bb_ref[...] = kk * a
        bonus = bonus + head_sum(r * kd * rk_ref[...]) * v
    bon_ref[...] = bonus


def rwkv_prep(p, lora_in, mu, w0, w2, a0, a2, g2, k_k, k_a, r_k, col0, batch):
    t = p.shape[0]
    s = t // batch
    nct = RW_WIDTH // LANES
    base = col0 // LANES
    nlo = RW_LORA_IN
    col = lambda off: pl.BlockSpec((s, LANES), lambda b, c: (b, base + off * nct + c))
    vec = lambda off: pl.BlockSpec((1, LANES), lambda b, c: (0, off * nct + c))
    mu_main = mu[:RW_MAIN].reshape(1, RW_MAIN)
    mu_lo = mu[RW_MAIN:].reshape(1, nlo)
    out_spec = pl.BlockSpec((s, LANES), lambda b, c: (b, c))
    n_out = 11
    return pl.pallas_call(
        _rw_prep_body,
        grid=(batch, nct),
        in_specs=[col(0), col(1), col(2),
                  pl.BlockSpec((s, nlo), lambda b, c: (b, 0)),
                  vec(0), vec(1), vec(2),
                  pl.BlockSpec((1, nlo), lambda b, c: (0, 0)),
                  pl.BlockSpec((2, LANES), lambda b, c: (0, c)),
                  pl.BlockSpec((2, RW_LORA, LANES), lambda b, c: (0, 0, c)),
                  pl.BlockSpec((2, LANES), lambda b, c: (0, c)),
                  pl.BlockSpec((2, RW_LORA, LANES), lambda b, c: (0, 0, c)),
                  pl.BlockSpec((RW_GATE_LORA, LANES), lambda b, c: (0, c)),
                  vec(0), vec(0), vec(0)],
        out_specs=[out_spec] * n_out,
        out_shape=[jax.ShapeDtypeStruct((t, RW_WIDTH), F32)] * n_out,
        compiler_params=_params("parallel", "parallel"),
        name="rwkv_prep",
    )(p, p, p, lora_in, mu_main, mu_main, mu_main, mu_lo, w0, w2, a0, a2, g2,
      k_k.reshape(1, RW_WIDTH), k_a.reshape(1, RW_WIDTH), r_k.reshape(1, RW_WIDTH))


def _rw_scan_body(r_ref, v_ref, kk_ref, bon_ref, gate_ref, lwf_ref, lwb_ref, kdf_ref, kdb_ref, bbf_ref, bbb_ref,
                  lnw_ref, lnb_ref, y_ref, cum_ref, cumt_ref, kdt_ref, bbt_ref, yf_ref, yb_ref,
                  tr_ref, tc_ref, rbk_ref, bkt_ref):
    s = r_ref.shape[0]
    c = CHUNK
    c2 = 2 * c
    row = lax.broadcasted_iota(jnp.int32, (s, 1), 0)
    pos = row % c
    lw_refs, kd_refs, bb_refs = (lwf_ref, lwb_ref), (kdf_ref, kdb_ref), (bbf_ref, bbb_ref)
    for d in range(2):
        cum = _seg_cumsum(lw_refs[d][...], pos, d == 1)
        cum_ref[d] = cum
        cumt_ref[d] = cum.T
        kdt_ref[d] = kd_refs[d][...].T
        bbt_ref[d] = bb_refs[d][...].T

    masks = _tri_masks(c2, c)
    merge = {rev: _merge_masks(c2, rev) for rev in (False, True)}
    eye2_f = jnp.where(masks["eye"], 1.0, 0.0)
    head0 = lax.broadcasted_iota(jnp.int32, (1, LANES), 1) < RW_HEAD_DIM
    head_bd = _head_block_diag()
    npair = s // PAIR
    group = min(RW_PREP_PAIRS, npair)

    def stack(x):
        return jnp.concatenate([jnp.where(head0, x, 0.0), jnp.where(head0, 0.0, x)], axis=0)

    def prep(pidx, d, half):
        reverse = d == 1
        incl2, strict2 = masks[reverse]
        cidx = 2 * pidx + half
        cols = pl.ds(pl.multiple_of(pidx * PAIR, PAIR), PAIR)
        rows = pl.ds(pl.multiple_of(cidx * c, c), c)
        hs = slice(half * c, (half + 1) * c)
        r, v, kk = r_ref[rows, :], v_ref[rows, :], kk_ref[rows, :]
        lw, kd, bb = lw_refs[d][rows, :], kd_refs[d][rows, :], bb_refs[d][rows, :]
        cum = cum_ref[d, rows, :]
        e_neg = jnp.exp(-cum)
        a_s = stack(-kk * jnp.exp(cum - lw))
        r_s = stack(r * jnp.exp(cum))
        ar = jnp.concatenate([a_s, r_s], axis=0)
        bk = jnp.concatenate([stack(bb * e_neg), stack(kd * e_neg)], axis=0)
        g = _dot1(ar, bk, _NT)
        yield
        rbk_ref[d, cidx] = jnp.concatenate([jnp.where(incl2, g[c2:, :c2], 0.0),
                                            jnp.where(incl2, g[c2:, c2:], 0.0)], axis=1).astype(BF16)
        av = _dot1(jnp.where(strict2, g[:c2, c2:], 0.0), stack(v))
        yield
        t_inv = []
        yield from _tri_inverse(-jnp.where(strict2, g[:c2, :c2], 0.0), eye2_f, merge[reverse], t_inv)
        tt = _dot1(t_inv[0], jnp.concatenate([a_s, av], axis=1))
        yield
        tr_ref[d, cidx] = jnp.concatenate([tt[:, :LANES], r_s], axis=0).astype(BF16)
        tc_ref[d, cidx] = tt[:, LANES:].astype(BF16)
        cum_t = cumt_ref[d, :, cols][:, hs]
        last = 0 if reverse else c - 1
        e_out_t = jnp.exp(cum_t[:, last:last + 1] - cum_t)
        bkt_ref[d, cidx] = jnp.concatenate([bbt_ref[d, :, cols][:, hs] * e_out_t,
                                            kdt_ref[d, :, cols][:, hs] * e_out_t], axis=1).astype(BF16)

    def seq(states, pidx, d):
        cum_t2 = cumt_ref[d, :, pl.ds(pl.multiple_of(pidx * PAIR, PAIR), PAIR)]
        for half in ((1, 0) if d else (0, 1)):
            cidx = 2 * pidx + half
            rows = pl.ds(pl.multiple_of(cidx * c, c), c)
            v = v_ref[rows, :]
            x = _dot1(tr_ref[d, cidx], states[d])
            yield
            ps = x[:c2] + tc_ref[d, cidx].astype(F32)
            os_ = x[c2:] + _dot1(rbk_ref[d, cidx], jnp.concatenate([ps, stack(v)], axis=0))
            y = os_[:c] + os_[c:]
            if d:
                yb_ref[rows, :] = y
            else:
                yf_ref[rows, :] = y
            pv = jnp.concatenate([ps[:c] + ps[c:], v], axis=0)
            last = half * c + (0 if d else c - 1)
            e_tot = jnp.exp(cum_t2[:, last:last + 1])
            states[d] = states[d] * e_tot + jnp.where(head_bd, _dot1(bkt_ref[d, cidx], pv), 0.0)
            yield

    ngroups = npair // group

    def prep_chains(g):
        return [prep(npair - 1 - (g * group + j) if d else g * group + j, d, half)
                for j in range(group) for d in range(2) for half in range(2)]

    def seq_chain(states, g, d):
        for j in range(group):
            yield from seq(states, npair - 1 - (g * group + j) if d else g * group + j, d)

    def step(g, carry):
        states = list(carry)
        _interleave(prep_chains(g + 1) + [seq_chain(states, g, 0), seq_chain(states, g, 1)])
        return tuple(states)

    zero = jnp.zeros((LANES, LANES), F32)
    _interleave(prep_chains(0))
    states = list(lax.fori_loop(0, ngroups - 1, step, (zero, zero)))
    _interleave([seq_chain(states, ngroups - 1, 0), seq_chain(states, ngroups - 1, 1)])

    bd = jnp.where(head_bd, 1.0, 0.0)
    y = yf_ref[...] + yb_ref[...]
    mean = _dot_sum(y, bd) * (1.0 / RW_HEAD_DIM)
    yc = y - mean
    var = _dot_sum(yc * yc, bd) * (1.0 / RW_HEAD_DIM)
    yn = yc * lax.rsqrt(var + RW_GN_EPS) * lnw_ref[...] + lnb_ref[...]
    y_ref[...] = ((yn + bon_ref[...]) * gate_ref[...]).astype(y_ref.dtype)


def rwkv_scan(prep, ln_w, ln_b, batch):
    t = prep[0].shape[0]
    s = t // batch
    assert s % PAIR == 0 and (s // PAIR) % min(RW_PREP_PAIRS, s // PAIR) == 0
    nct = RW_WIDTH // LANES
    nchunk = s // CHUNK
    blk = pl.BlockSpec((s, LANES), lambda b, c: (b, c))
    vec = pl.BlockSpec((1, LANES), lambda b, c: (0, c))
    tsp = pltpu.VMEM((2, LANES, s), F32)
    big = pltpu.VMEM((s, LANES), F32)
    per_chunk = lambda rows, cols: pltpu.VMEM((2, nchunk, rows, cols), BF16)
    return pl.pallas_call(
        _rw_scan_body,
        grid=(batch, nct),
        in_specs=[blk] * 11 + [vec, vec],
        out_specs=blk,
        out_shape=jax.ShapeDtypeStruct((t, RW_WIDTH), BF16),
        scratch_shapes=[pltpu.VMEM((2, s, LANES), F32), tsp, tsp, tsp, big, big,
                        per_chunk(2 * PAIR, LANES), per_chunk(PAIR, LANES), per_chunk(PAIR, 2 * LANES),
                        per_chunk(PAIR, LANES)],
        compiler_params=_params("parallel", "parallel"),
        name="rwkv_scan",
    )(*prep, ln_w.reshape(1, RW_WIDTH), ln_b.reshape(1, RW_WIDTH))


def even_mixer(x, g, w_in, w_sc, w_ab, w_out, layer, conv_qkv, a_log, dt_bias, out_gain, conv_sc, batch):
    p, ab = norm_mm(x, g, w_in, layer, ncols=4 * DN_WIDTH, w2=w_sc, wp_t=w_ab.T, out_dtype=BF16)
    zeros = jnp.zeros((2 * DN_HEADS,), F32)
    alog32 = jnp.concatenate([zeros, a_log.reshape(-1)]).reshape(1, -1)
    dt32 = jnp.concatenate([zeros, dt_bias.reshape(-1)]).reshape(1, -1)
    y_dn = deltanet(p, ab, conv_qkv.T, alog32, dt32, out_gain, batch)
    y_sc = short_conv(p, conv_sc.T, 4 * DN_WIDTH, batch)
    return mm_res(y_dn, w_out, layer, x, a2=y_sc, tn=D_MODEL)


def odd_mixer(x, g, w_in, w_lora, w_out, layer, qn, kn, bias_tab, mu, w0, w2, a0, a2, g2, k_k, k_a, r_k, ln_w, ln_b,
              batch):
    p, lora_in = norm_mm(x, g, w_in, layer, ncols=3 * DSA_QKV + RW_MAIN, wp=w_lora, out_dtype=BF16)
    y_c = dilated_attention(p, qn, kn, bias_tab, batch)
    prep = rwkv_prep(p, lora_in, mu, w0, w2, a0, a2, g2, k_k, k_a, r_k, 3 * DSA_QKV, batch)
    y_d = rwkv_scan(prep, ln_w, ln_b, batch)
    return mm_res(y_c, w_out, layer, x, a2=y_d, tn=D_MODEL)


def kernel(x, mem, rel_bias, norm_mix, norm_xattn, norm_mem, norm_ffn, xa_wq, xa_wk, xa_wv, xa_wo, xa_qn, xa_kn, ffn_w1, ffn_w2, ev_w_in, ev_w_out, dn_conv, dn_a_log, dn_dt_bias, dn_norm, sc_conv, od_w_in, od_w_out, ca_qn, ca_kn, rw_mu, rw_w0, rw_w2, rw_a0, rw_a2, rw_g2, rw_k_k, rw_k_a, rw_r_k, rw_ln_w, rw_ln_b):
    batch, seq, d = x.shape
    n_mem = mem.shape[1]
    xf = x.reshape(batch * seq, d)
    memf = mem.reshape(batch * n_mem, d)
    bias_tab = _dsa_bias_table(rel_bias)
    c_ab = 4 * DN_WIDTH
    c_sc = c_ab + 4 * DN_HEADS
    c_lo = 3 * DSA_QKV + RW_MAIN
    ev_in, od_in = ev_w_in.astype(BF16), od_w_in.astype(BF16)
    ev_sc = ev_in[:, :, c_sc:]
    ev_out, od_out = ev_w_out.astype(BF16), od_w_out.astype(BF16)
    w_q, w_o = xa_wq.astype(BF16), xa_wo.astype(BF16)
    w_kv = jnp.concatenate([xa_wk, xa_wv], axis=2).astype(BF16)
    for layer in range(DEPTH):
        i = layer // 2
        if layer % 2 == 0:
            xf = even_mixer(xf, norm_mix[layer], ev_in, ev_sc, ev_w_in[i, :, c_ab:c_sc], ev_out, i, dn_conv[i],
                            dn_a_log[i], dn_dt_bias[i], dn_norm[i], sc_conv[i], batch)
        else:
            xf = odd_mixer(xf, norm_mix[layer], od_in, od_w_in[i, :, c_lo:], od_out, i, ca_qn[i], ca_kn[i], bias_tab,
                           rw_mu[i], rw_w0[i], rw_w2[i], rw_a0[i], rw_a2[i], rw_g2[i], rw_k_k[i], rw_k_a[i], rw_r_k[i],
                           rw_ln_w[i], rw_ln_b[i], batch)
        kv = norm_mm(memf, norm_mem[layer], w_kv, layer).reshape(batch, n_mem, 2 * XA_WIDTH)
        xf, hn = xattn(xf, norm_xattn[layer], w_q, kv, w_o, layer, xa_qn[layer], xa_kn[layer], norm_ffn[layer], batch)
        h1 = act_mm(hn, ffn_w1, layer, act="relu2", out_dtype=BF16)
        xf = mm_res(h1, ffn_w2, layer, xf)
    return xf.reshape(batch, seq, d)
```

```python
import functools
import math

import jax
import jax.numpy as jnp
import numpy as np
from jax import lax
from jax.experimental import pallas as pl
from jax.experimental.pallas import tpu as pltpu

F32 = jnp.float32
BF16 = jnp.bfloat16

D_MODEL = 2048
DEPTH = 4
RMS_EPS = 1e-6
L2_EPS = 1e-6

DN_HEADS = 8
DN_HEAD_DIM = 128
DN_WIDTH = DN_HEADS * DN_HEAD_DIM
SC_WIDTH = D_MODEL - DN_WIDTH
CHUNK = 64
PAIR = 2 * CHUNK
DN_PREP_PAIRS = 4
DN_HEADS_PER_STEP = 2
RW_PREP_PAIRS = 4

DSA_PATTERNS = ((128, 1), (512, 4), (2048, 16))
DSA_GROUPS = len(DSA_PATTERNS)
DSA_HPG = 4
DSA_HEAD_DIM = 128
DSA_HEADS = DSA_GROUPS * DSA_HPG
DSA_QKV = DSA_HEADS * DSA_HEAD_DIM
DSA_SIDE = 64
DSA_QBLK = 128
DSA_PAD = DSA_SIDE * max(d for _, d in DSA_PATTERNS)
DSA_INTERLEAVE = 8
REL_BUCKETS = 32
REL_MAX_DIST = 1024
NEG_INF = -1e30

RW_HEADS = 8
RW_HEAD_DIM = 64
RW_WIDTH = RW_HEADS * RW_HEAD_DIM
RW_LORA = 64
RW_GATE_LORA = 128
RW_MAIN = 3 * RW_WIDTH
RW_LORA_IN = 4 * RW_LORA + RW_GATE_LORA
RW_GN_EPS = 64e-5

XA_HEADS = 4
XA_HEAD_DIM = 128
XA_WIDTH = XA_HEADS * XA_HEAD_DIM

LANES = 128
SUBLANES = 8
NORM_ROWS = 256
VMEM_LIMIT_BYTES = 58 * 1024 * 1024


def _params(*sem):
    return pltpu.CompilerParams(dimension_semantics=sem, vmem_limit_bytes=VMEM_LIMIT_BYTES)


_NN = (((1,), (0,)), ((), ()))
_NT = (((1,), (1,)), ((), ()))


def _dot1(a, b, dims=_NN):
    return lax.dot_general(a.astype(BF16), b.astype(BF16), dims, preferred_element_type=F32)


def _dot_sum(a, b01):
    b = b01.astype(BF16)
    hi = a.astype(BF16)
    lo = (a - hi.astype(F32)).astype(BF16)
    dg = functools.partial(lax.dot_general, dimension_numbers=_NN, preferred_element_type=F32)
    return dg(hi, b) + dg(lo, b)


def _interleave(chains):
    chains = list(chains)
    while chains:
        alive = []
        for ch in chains:
            try:
                next(ch)
                alive.append(ch)
            except StopIteration:
                pass
        chains = alive


def _merge_masks(n, reverse):
    ii = lax.broadcasted_iota(jnp.int32, (n, n), 0)
    jj = lax.broadcasted_iota(jnp.int32, (n, n), 1)
    tri = (ii < jj) if reverse else (ii > jj)
    out = []
    size = 1
    while size < CHUNK:
        out.append(tri & ((ii // (2 * size)) == (jj // (2 * size))) & ((ii // size) != (jj // size)))
        size *= 2
    return out


def _tri_inverse(a, eye_f, level_masks, out):
    t = eye_f - jnp.where(level_masks[0], a, 0.0)
    for m in level_masks[1:]:
        x = _dot1(jnp.where(m, a, 0.0), t)
        yield
        t = t - _dot1(t, x)
        yield
    out.append(t)


def _sigmoid(x):
    return 1.0 / (1.0 + jnp.exp(-x))


def _softplus(x):
    return jnp.maximum(x, 0.0) + jnp.log1p(jnp.exp(-jnp.abs(x)))


def _shift_down(x, row):
    return jnp.where(row == 0, 0.0, pltpu.roll(x, 1, 0))


def _shift_up(x, row):
    n = x.shape[0]
    return jnp.where(row == n - 1, 0.0, pltpu.roll(x, n - 1, 0))


def _norm_mm_body(x_ref, g_ref, w_ref, *rest, act, extra, extra_t, n_first):
    if n_first is not None:
        w2_ref, rest = rest[0], rest[1:]
    if extra:
        wp_ref, o_ref, op_ref, xn_ref = rest
    else:
        o_ref, xn_ref = rest

    j = pl.program_id(1)

    def project(wt_ref, rows=slice(None)):
        acc = jnp.dot(xn_ref[rows, :], wt_ref[...].astype(BF16), preferred_element_type=F32)
        if act == "relu2":
            acc = jnp.square(jnp.maximum(acc, 0.0))
        o_ref[rows, :] = acc.astype(o_ref.dtype)

    @pl.when(j == 0)
    def _():
        for r in range(x_ref.shape[0] // NORM_ROWS):
            rows = slice(r * NORM_ROWS, (r + 1) * NORM_ROWS)
            x = x_ref[rows, :]
            xn = x * lax.rsqrt(jnp.mean(x * x, axis=-1, keepdims=True) + RMS_EPS) * g_ref[...]
            xn_ref[rows, :] = xn.astype(BF16)
            if extra and extra_t:
                op_ref[:, rows] = _dot1(wp_ref[...], xn, _NT)
            elif extra:
                op_ref[rows, :] = _dot1(xn, wp_ref[...])
            project(w_ref, rows)

    if n_first is None:
        pl.when(j > 0)(lambda: project(w_ref))
    else:
        pl.when((j > 0) & (j < n_first))(lambda: project(w_ref))
        pl.when(j >= n_first)(lambda: project(w2_ref))


def norm_mm(x, g, w, layer, ncols=None, w2=None, wp=None, wp_t=None, act=None, out_dtype=F32, tm=1024, tn=1024):
    t, d = x.shape
    n1 = w.shape[2] if ncols is None else ncols
    n = n1 + (0 if w2 is None else w2.shape[2])
    tm = min(tm, t)
    tn = min(tn, n1)
    assert t % tm == 0 and n1 % tn == 0 and n % tn == 0 and tm % NORM_ROWS == 0
    extra = wp is not None or wp_t is not None
    n_first = None if w2 is None else n1 // tn
    in_specs = [
        pl.BlockSpec((tm, d), lambda i, j: (i, 0)),
        pl.BlockSpec((1, d), lambda i, j: (0, 0)),
    ]
    args = [x, g.reshape(1, d), w]
    if w2 is None:
        in_specs.append(pl.BlockSpec((None, d, tn), lambda i, j: (layer, 0, j)))
    else:
        in_specs.append(pl.BlockSpec((None, d, tn), lambda i, j: (layer, 0, jnp.minimum(j, n_first - 1))))
        in_specs.append(pl.BlockSpec((None, d, tn), lambda i, j: (layer, 0, jnp.maximum(j - n_first, 0))))
        args.append(w2)
    out_specs = pl.BlockSpec((tm, tn), lambda i, j: (i, j))
    out_shape = jax.ShapeDtypeStruct((t, n), out_dtype)
    if wp is not None:
        npc = wp.shape[1]
        in_specs.append(pl.BlockSpec((d, npc), lambda i, j: (0, 0)))
        out_specs = [out_specs, pl.BlockSpec((tm, npc), lambda i, j: (i, 0))]
        out_shape = [out_shape, jax.ShapeDtypeStruct((t, npc), F32)]
        args.append(wp)
    elif wp_t is not None:
        npc = wp_t.shape[0]
        in_specs.append(pl.BlockSpec((npc, d), lambda i, j: (0, 0)))
        out_specs = [out_specs, pl.BlockSpec((npc, tm), lambda i, j: (0, i))]
        out_shape = [out_shape, jax.ShapeDtypeStruct((npc, t), F32)]
        args.append(wp_t)
    return pl.pallas_call(
        functools.partial(_norm_mm_body, act=act, extra=extra, extra_t=wp_t is not None, n_first=n_first),
        grid=(t // tm, n // tn),
        in_specs=in_specs,
        out_specs=out_specs,
        out_shape=out_shape,
        scratch_shapes=[pltpu.VMEM((tm, d), BF16)],
        compiler_params=_params("parallel", "arbitrary"),
        name="norm_mm",
    )(*args)


def _act_mm_body(h_ref, w_ref, o_ref, *, act):
    acc = jnp.dot(h_ref[...], w_ref[...].astype(BF16), preferred_element_type=F32)
    if act == "relu2":
        acc = jnp.square(jnp.maximum(acc, 0.0))
    o_ref[...] = acc.astype(o_ref.dtype)


def act_mm(h, w, layer, act=None, out_dtype=F32, tm=2048, tn=1024):
    t, d = h.shape
    n = w.shape[2]
    tm, tn = min(tm, t), min(tn, n)
    assert t % tm == 0 and n % tn == 0
    return pl.pallas_call(
        functools.partial(_act_mm_body, act=act),
        grid=(t // tm, n // tn),
        in_specs=[pl.BlockSpec((tm, d), lambda i, j: (i, 0)),
                  pl.BlockSpec((None, d, tn), lambda i, j: (layer, 0, j))],
        out_specs=pl.BlockSpec((tm, tn), lambda i, j: (i, j)),
        out_shape=jax.ShapeDtypeStruct((t, n), out_dtype),
        compiler_params=_params("parallel", "arbitrary"),
        name="act_mm",
    )(h, w)


def _mm_res_body(a_ref, *rest, n_first):
    if n_first is not None:
        a2_ref, rest = rest[0], rest[1:]
    w_ref, r_ref, o_ref = rest
    kk = pl.program_id(2)

    @pl.when(kk == 0)
    def _():
        o_ref[...] = r_ref[...]

    def accumulate(lhs_ref):
        o_ref[...] += jnp.dot(lhs_ref[...], w_ref[...].astype(BF16), preferred_element_type=F32)

    if n_first is None:
        accumulate(a_ref)
    else:
        pl.when(kk < n_first)(lambda: accumulate(a_ref))
        pl.when(kk >= n_first)(lambda: accumulate(a2_ref))


def mm_res(a, w, layer, res, a2=None, tm=1024, tn=1024, tk=2048):
    t, k1 = a.shape
    k = k1 + (0 if a2 is None else a2.shape[1])
    n = w.shape[2]
    tm, tn, tk = min(tm, t), min(tn, n), min(tk, k1)
    assert t % tm == 0 and n % tn == 0 and k1 % tk == 0 and k % tk == 0
    n_first = None if a2 is None else k1 // tk
    if a2 is None:
        in_specs = [pl.BlockSpec((tm, tk), lambda i, j, kk: (i, kk))]
        args = [a]
    else:
        in_specs = [pl.BlockSpec((tm, tk), lambda i, j, kk: (i, jnp.minimum(kk, n_first - 1))),
                    pl.BlockSpec((tm, tk), lambda i, j, kk: (i, jnp.maximum(kk - n_first, 0)))]
        args = [a, a2]
    in_specs += [
        pl.BlockSpec((None, tk, tn), lambda i, j, kk: (layer, kk, j)),
        pl.BlockSpec((tm, tn), lambda i, j, kk: (i, j)),
    ]
    return pl.pallas_call(
        functools.partial(_mm_res_body, n_first=n_first),
        grid=(t // tm, n // tn, k // tk),
        in_specs=in_specs,
        out_specs=pl.BlockSpec((tm, tn), lambda i, j, kk: (i, j)),
        out_shape=jax.ShapeDtypeStruct((t, n), F32),
        compiler_params=_params("parallel", "parallel", "arbitrary"),
        name="mm_res",
    )(*args, w, res)


def _xattn_body(x_ref, g_ref, wq_ref, kv_ref, wo_ref, qn_ref, kn_ref, g2_ref, o_ref, hn_ref):
    x = x_ref[...]
    xn = x * lax.rsqrt(jnp.mean(x * x, axis=-1, keepdims=True) + RMS_EPS) * g_ref[...]
    q = jnp.dot(xn.astype(BF16), wq_ref[...], preferred_element_type=F32)
    kv = kv_ref[0]
    outs = []
    for h in range(XA_HEADS):
        sl = slice(h * XA_HEAD_DIM, (h + 1) * XA_HEAD_DIM)
        qh = q[:, sl]
        qh = qh * lax.rsqrt(jnp.mean(qh * qh, axis=-1, keepdims=True) + RMS_EPS) * qn_ref[...]
        kh = kv[:, sl]
        kh = kh * lax.rsqrt(jnp.mean(kh * kh, axis=-1, keepdims=True) + RMS_EPS) * kn_ref[...]
        vh = kv[:, XA_WIDTH + h * XA_HEAD_DIM:XA_WIDTH + (h + 1) * XA_HEAD_DIM]
        logits = _dot1(qh, kh, _NT) * (XA_HEAD_DIM ** -0.5)
        m = jnp.max(logits, axis=-1, keepdims=True)
        p = jnp.exp(logits - m)
        s = jnp.sum(p, axis=-1, keepdims=True)
        outs.append(_dot1(p, vh) / s)
    o = jnp.concatenate(outs, axis=-1).astype(BF16)
    xo = x + jnp.dot(o, wo_ref[...], preferred_element_type=F32)
    o_ref[...] = xo
    hn_ref[...] = (xo * lax.rsqrt(jnp.mean(xo * xo, axis=-1, keepdims=True) + RMS_EPS) * g2_ref[...]).astype(BF16)


def xattn(x, g, wq, kv, wo, layer, qn, kn, g_next, batch, ts=1024):
    t, d = x.shape
    s = t // batch
    ts = min(ts, s)
    nst = s // ts
    m = kv.shape[1]
    return pl.pallas_call(
        _xattn_body,
        grid=(batch, nst),
        in_specs=[
            pl.BlockSpec((ts, d), lambda b, i: (b * nst + i, 0)),
            pl.BlockSpec((1, d), lambda b, i: (0, 0)),
            pl.BlockSpec((None, d, XA_WIDTH), lambda b, i: (layer, 0, 0)),
            pl.BlockSpec((1, m, 2 * XA_WIDTH), lambda b, i: (b, 0, 0)),
            pl.BlockSpec((None, XA_WIDTH, d), lambda b, i: (layer, 0, 0)),
            pl.BlockSpec((1, XA_HEAD_DIM), lambda b, i: (0, 0)),
            pl.BlockSpec((1, XA_HEAD_DIM), lambda b, i: (0, 0)),
            pl.BlockSpec((1, d), lambda b, i: (0, 0)),
        ],
        out_specs=[pl.BlockSpec((ts, d), lambda b, i: (b * nst + i, 0))] * 2,
        out_shape=[jax.ShapeDtypeStruct((t, d), F32), jax.ShapeDtypeStruct((t, d), BF16)],
        compiler_params=_params("parallel", "parallel"),
        name="xattn",
    )(x, g.reshape(1, d), wq, kv, wo, qn.reshape(1, -1), kn.reshape(1, -1), g_next.reshape(1, d))


def _tri_masks(n, blk):
    ii = lax.broadcasted_iota(jnp.int32, (n, n), 0)
    jj = lax.broadcasted_iota(jnp.int32, (n, n), 1)
    same = (ii // blk) == (jj // blk) if n != blk else None

    def m(c):
        return c if same is None else (c & same)

    return {
        False: (m(ii >= jj), m(ii > jj)),
        True: (m(ii <= jj), m(ii < jj)),
        "eye": ii == jj,
    }


def _seg_cumsum(x, pos, reverse):
    n = x.shape[0]
    sh = 1
    while sh < CHUNK:
        if reverse:
            x = x + jnp.where(pos < CHUNK - sh, pltpu.roll(x, n - sh, 0), 0.0)
        else:
            x = x + jnp.where(pos >= sh, pltpu.roll(x, sh, 0), 0.0)
        sh *= 2
    return x


def _dn_body(q_ref, k_ref, v_ref, gate_ref, ab_ref, cq_ref, ck_ref, cv_ref, alog_ref, dt_ref, gain_ref,
             y_ref, qs_ref, ks_ref, vs_ref, kt_ref, bg_ref, gt_ref, of_ref, ob_ref,
             u_ref, w_ref, qg_ref, qk_ref, kdt_ref, eg_ref, st_ref):
    s = q_ref.shape[0]
    hd = DN_HEAD_DIM
    nh = DN_HEADS_PER_STEP
    row = lax.broadcasted_iota(jnp.int32, (s, 1), 0)
    c = CHUNK
    lane32 = lax.broadcasted_iota(jnp.int32, (1, 4 * DN_HEADS), 1)
    lane128 = lax.broadcasted_iota(jnp.int32, (1, LANES), 1)
    pos_t = lax.broadcasted_iota(jnp.int32, (1, s), 1) % c

    def conv_silu(x_ref, cw_ref, lanes):
        x = x_ref[:, lanes].astype(F32)
        w = cw_ref[:, lanes]
        y = _shift_down(x, row) * w[0:1] + x * w[1:2] + _shift_up(x, row) * w[2:3]
        return y * _sigmoid(y)

    for hh in range(nh):
        h = pl.program_id(1) * nh + hh
        lanes = slice(hh * hd, (hh + 1) * hd)
        q = conv_silu(q_ref, cq_ref, lanes)
        q = q * lax.rsqrt(jnp.sum(q * q, axis=-1, keepdims=True) + L2_EPS) * (hd ** -0.5)
        qs_ref[hh] = q
        k = conv_silu(k_ref, ck_ref, lanes)
        k = k * lax.rsqrt(jnp.sum(k * k, axis=-1, keepdims=True) + L2_EPS)
        ks_ref[hh] = k
        kt_ref[hh] = k.T
        vs_ref[hh] = conv_silu(v_ref, cv_ref, lanes)

        def decay_row(d, h=h):
            idx = (2 + d) * DN_HEADS + h
            a_log = jnp.sum(jnp.where(lane32 == idx, alog_ref[...], 0.0), axis=-1, keepdims=True)
            dt = jnp.sum(jnp.where(lane32 == idx, dt_ref[...], 0.0), axis=-1, keepdims=True)
            g = -jnp.exp(a_log) * _softplus(ab_ref[pl.ds(idx, 1), :] + dt)
            sh = 1
            while sh < c:
                if d:
                    g = g + jnp.where(pos_t < c - sh, pltpu.roll(g, s - sh, 1), 0.0)
                else:
                    g = g + jnp.where(pos_t >= sh, pltpu.roll(g, sh, 1), 0.0)
                sh *= 2
            return g

        rows = [_sigmoid(ab_ref[pl.ds(h, 1), :]), _sigmoid(ab_ref[pl.ds(DN_HEADS + h, 1), :]),
                decay_row(0), decay_row(1)]
        gt = jnp.concatenate(rows + [jnp.zeros((LANES - len(rows), s), F32)], axis=0)
        gt_ref[hh] = gt[0:SUBLANES, :]
        bg_ref[hh] = gt.T
        for d in range(2):
            st_ref[hh, d] = jnp.zeros((hd, hd), F32)

    masks = _tri_masks(PAIR, c)
    merge = {rev: _merge_masks(PAIR, rev) for rev in (False, True)}
    eye_f = jnp.where(masks["eye"], 1.0, 0.0)
    npair = s // PAIR
    group = min(DN_PREP_PAIRS, npair)

    def prep(hh, pidx, d):
        reverse = d == 1
        rows = pl.ds(pl.multiple_of(pidx * PAIR, PAIR), PAIR)
        qp, kp, vp = qs_ref[hh, rows, :], ks_ref[hh, rows, :], vs_ref[hh, rows, :]
        bgp = bg_ref[hh, rows, :]
        beta = bgp[:, d:d + 1]
        gcol = bgp[:, 2 + d:3 + d]
        grow = gt_ref[hh, 2 + d:3 + d, rows]
        incl, strict = masks[reverse]
        decay = jnp.where(incl, jnp.exp(jnp.where(incl, gcol - grow, 0.0)), 0.0)
        kb = kp * beta
        kq = _dot1(jnp.concatenate([kb, qp], axis=0), kp, _NT)
        yield
        qk_ref[hh, d, rows, :] = jnp.where(incl, kq[PAIR:] * decay, 0.0).astype(BF16)
        t_inv = []
        yield from _tri_inverse(jnp.where(strict, kq[:PAIR] * decay, 0.0), eye_f, merge[reverse], t_inv)
        egc = jnp.exp(gcol)
        uw = _dot1(t_inv[0], jnp.concatenate([vp * beta, kb * egc], axis=1))
        yield
        u_ref[hh, d, rows, :] = uw[:, :hd]
        w_ref[hh, d, rows, :] = uw[:, hd:].astype(BF16)
        qg_ref[hh, d, rows, :] = (qp * egc).astype(BF16)
        last0, last1 = (0, c) if reverse else (c - 1, PAIR - 1)
        glast = jnp.where(lane128 < c, grow[:, last0:last0 + 1], grow[:, last1:last1 + 1])
        kdt_ref[hh, d, :, rows] = (kt_ref[hh, :, rows] * jnp.exp(glast - grow)).astype(BF16)
        eg_ref[hh, d, :, rows] = jnp.broadcast_to(jnp.exp(glast), (SUBLANES, PAIR))

    def seq(hh, pidx, d):
        pair_rows = pl.ds(pl.multiple_of(pidx * PAIR, PAIR), PAIR)
        kdt = kdt_ref[hh, d, :, pair_rows]
        eg = eg_ref[hh, d, 0:1, pair_rows]
        for half in ((1, 0) if d else (0, 1)):
            r0 = half * c
            rows = pl.ds(pl.multiple_of(pidx * PAIR + r0, c), c)
            state = st_ref[hh, d]
            ws = _dot1(jnp.concatenate([w_ref[hh, d, rows, :], qg_ref[hh, d, rows, :]], axis=0), state)
            yield
            v_new = u_ref[hh, d, rows, :] - ws[:c]
            o = ws[c:] + _dot1(qk_ref[hh, d, rows, r0:r0 + c], v_new)
            if d:
                ob_ref[hh, rows, :] = o
            else:
                of_ref[hh, rows, :] = o
            st_ref[hh, d] = state * eg[:, r0:r0 + 1] + _dot1(kdt[:, r0:r0 + c], v_new)
            yield

    ngroups = npair // group

    def pair_of(g, j, d):
        return npair - 1 - (g * group + j) if d else g * group + j

    def prep_chains(g):
        return [prep(hh, pair_of(g, j, d), d) for j in range(group) for hh in range(nh) for d in range(2)]

    def seq_chain(g, hh, d):
        for j in range(group):
            yield from seq(hh, pair_of(g, j, d), d)

    def seq_chains(g):
        return [seq_chain(g, hh, d) for hh in range(nh) for d in range(2)]

    def step(g, _):
        _interleave(prep_chains(g + 1) + seq_chains(g))
        return 0

    _interleave(prep_chains(0))
    lax.fori_loop(0, ngroups - 1, step, 0)
    _interleave(seq_chains(ngroups - 1))

    for hh in range(nh):
        lanes = slice(hh * hd, (hh + 1) * hd)
        o = of_ref[hh] + ob_ref[hh]
        o = o * lax.rsqrt(jnp.mean(o * o, axis=-1, keepdims=True) + RMS_EPS) * gain_ref[...]
        gate = gate_ref[:, lanes].astype(F32)
        y_ref[:, lanes] = (o * (gate * _sigmoid(gate))).astype(y_ref.dtype)


def deltanet(p, ab, conv_t, alog32, dt32, gain, batch):
    t = p.shape[0]
    s = t // batch
    assert s % PAIR == 0 and (s // PAIR) % min(DN_PREP_PAIRS, s // PAIR) == 0
    hd = DN_HEAD_DIM
    nh = DN_HEADS_PER_STEP
    nsteps = DN_HEADS // nh
    col = lambda off: pl.BlockSpec((s, nh * hd), lambda b, h: (b, off * nsteps + h))
    cw = lambda off: pl.BlockSpec((3, nh * hd), lambda b, h: (0, off * nsteps + h))
    gate_col = pl.BlockSpec((s, nh * hd), lambda b, h: (b, 3 * nsteps + h), pipeline_mode=pl.Buffered(1))
    small = lambda n: pl.BlockSpec((1, n), lambda b, h: (0, 0))
    tok = pltpu.VMEM((nh, s, hd), F32)
    per_dir = pltpu.VMEM((nh, 2, s, hd), BF16)
    return pl.pallas_call(
        _dn_body,
        grid=(batch, nsteps),
        in_specs=[col(0), col(1), col(2), gate_col,
                  pl.BlockSpec((4 * DN_HEADS, s), lambda b, h: (0, b)),
                  cw(0), cw(1), cw(2),
                  small(4 * DN_HEADS), small(4 * DN_HEADS), small(hd)],
        out_specs=pl.BlockSpec((s, nh * hd), lambda b, h: (b, h)),
        out_shape=jax.ShapeDtypeStruct((t, DN_WIDTH), BF16),
        scratch_shapes=[tok, tok, tok, pltpu.VMEM((nh, hd, s), F32), tok, pltpu.VMEM((nh, SUBLANES, s), F32), tok, tok,
                        pltpu.VMEM((nh, 2, s, hd), F32), per_dir, per_dir, per_dir,
                        pltpu.VMEM((nh, 2, hd, s), BF16), pltpu.VMEM((nh, 2, SUBLANES, s), F32),
                        pltpu.VMEM((nh, 2, hd, hd), F32)],
        compiler_params=_params("parallel", "parallel"),
        name="deltanet",
    )(p, p, p, p, ab, conv_t, conv_t, conv_t, alog32, dt32, gain.reshape(1, hd))


def _sconv_body(b_ref, c_ref, u_ref, w_ref, y_ref):
    s = b_ref.shape[0]
    row = lax.broadcasted_iota(jnp.int32, (s, 1), 0)
    cu = c_ref[...].astype(F32) * u_ref[...].astype(F32)
    w = w_ref[...]
    y = _shift_down(cu, row) * w[0:1] + cu * w[1:2] + _shift_up(cu, row) * w[2:3]
    y_ref[...] = (b_ref[...].astype(F32) * y).astype(y_ref.dtype)


def short_conv(p, conv_t, col0, batch, tc=256):
    t = p.shape[0]
    s = t // batch
    nct = SC_WIDTH // tc
    base = col0 // tc
    col = lambda off: pl.BlockSpec((s, tc), lambda b, c: (b, base + off * nct + c))
    return pl.pallas_call(
        _sconv_body,
        grid=(batch, nct),
        in_specs=[col(0), col(1), col(2), pl.BlockSpec((3, tc), lambda b, c: (0, c))],
        out_specs=pl.BlockSpec((s, tc), lambda b, c: (b, c)),
        out_shape=jax.ShapeDtypeStruct((t, SC_WIDTH), BF16),
        compiler_params=_params("parallel", "parallel"),
        name="short_conv",
    )(p, p, p, conv_t)


def _dsa_body(*refs, seq):
    q_refs = refs[0:3]
    k_refs = refs[3:6]
    v_refs = refs[6:9]
    qn_ref, kn_ref, bias_ref, y_ref, qs_ref, kpad_ref, vpad_ref, og_ref, lse_ref = refs[9:]
    s = seq
    qb = DSA_QBLK
    side = DSA_SIDE
    width = qb + 2 * side
    kj = lax.broadcasted_iota(jnp.int32, (1, width), 1)
    zpad = jnp.zeros((DSA_PAD, DSA_HEAD_DIM), F32)
    for ref in (kpad_ref, vpad_ref):
        ref[0:DSA_PAD, :] = zpad
        ref[DSA_PAD + s:2 * DSA_PAD + s, :] = zpad

    for gi, (_, dil) in enumerate(DSA_PATTERNS):
        sub = s // dil
        nblk = sub // qb
        q = q_refs[gi][...].astype(F32)
        qs_ref[...] = (q * lax.rsqrt(jnp.mean(q * q, axis=-1, keepdims=True) + RMS_EPS) * qn_ref[...]
                       * (DSA_HEAD_DIM ** -0.5))
        k = k_refs[gi][...].astype(F32)
        kpad_ref[DSA_PAD:DSA_PAD + s, :] = k * lax.rsqrt(jnp.mean(k * k, axis=-1, keepdims=True) + RMS_EPS) * kn_ref[...]
        vpad_ref[DSA_PAD:DSA_PAD + s, :] = v_refs[gi][...].astype(F32)
        bias = bias_ref[0, gi]

        def block(t, gi=gi, dil=dil, sub=sub, nblk=nblk, bias=bias):
            r = t // nblk
            n = t % nblk
            rows = pl.ds(r + n * (qb * dil), qb, stride=dil)
            win = pl.ds(DSA_PAD + r + (n * qb - side) * dil, width, stride=dil)
            logits = _dot1(qs_ref[rows, :], kpad_ref[win, :], _NT) + bias
            yield
            pos = n * qb - side + kj
            logits = jnp.where((pos >= 0) & (pos < sub), logits, NEG_INF)
            m = jnp.max(logits, axis=-1, keepdims=True)
            yield
            p = jnp.exp(logits - m)
            ssum = jnp.sum(p, axis=-1, keepdims=True)
            o = _dot1(p, vpad_ref[win, :])
            yield
            og_ref[gi, rows, :] = o / ssum
            lse_ref[gi, rows, :] = jnp.broadcast_to(m + jnp.log(ssum), (qb, DSA_HEAD_DIM))

        nblocks = dil * nblk

        def blocks_step(i, _, block=block):
            _interleave(block(i * DSA_INTERLEAVE + u) for u in range(DSA_INTERLEAVE))
            return 0

        lax.fori_loop(0, nblocks // DSA_INTERLEAVE, blocks_step, 0)

    lse = [lse_ref[gi] for gi in range(DSA_GROUPS)]
    mx = jnp.maximum(jnp.maximum(lse[0], lse[1]), lse[2])
    ws = [jnp.exp(l - mx) for l in lse]
    num = ws[0] * og_ref[0] + ws[1] * og_ref[1] + ws[2] * og_ref[2]
    y_ref[...] = (num / (ws[0] + ws[1] + ws[2])).astype(y_ref.dtype)


def dilated_attention(p, qn, kn, bias_tab, batch):
    t = p.shape[0]
    s = t // batch
    assert (s // DSA_QBLK) % DSA_INTERLEAVE == 0
    hd = DSA_HEAD_DIM
    nh = DSA_HEADS

    def col(part, gi):
        return pl.BlockSpec((s, hd), lambda b, j: (b, part * nh + gi * DSA_HPG + j))

    in_specs = [col(part, gi) for part in range(3) for gi in range(DSA_GROUPS)]
    in_specs += [pl.BlockSpec((1, hd), lambda b, j: (0, 0)), pl.BlockSpec((1, hd), lambda b, j: (0, 0)),
                 pl.BlockSpec((1, DSA_GROUPS, DSA_QBLK, DSA_QBLK + 2 * DSA_SIDE), lambda b, j: (j, 0, 0, 0))]
    big = pltpu.VMEM((s, hd), F32)
    pad = pltpu.VMEM((s + 2 * DSA_PAD, hd), F32)
    grp = pltpu.VMEM((DSA_GROUPS, s, hd), F32)
    return pl.pallas_call(
        functools.partial(_dsa_body, seq=s),
        grid=(batch, DSA_HPG),
        in_specs=in_specs,
        out_specs=pl.BlockSpec((s, hd), lambda b, j: (b, j)),
        out_shape=jax.ShapeDtypeStruct((t, DSA_HPG * hd), BF16),
        scratch_shapes=[big, pad, pad, grp, grp],
        compiler_params=_params("parallel", "parallel"),
        name="dilated_attention",
    )(*([p] * 9), qn.reshape(1, hd), kn.reshape(1, hd), bias_tab)


def _t5_bucket(rel):
    half = REL_BUCKETS // 2
    max_exact = half // 2
    n = np.abs(rel)
    scaled = (np.log(np.maximum(n, max_exact).astype(np.float32) / np.float32(max_exact))
              / np.float32(math.log(REL_MAX_DIST / max_exact)))
    large = np.minimum(max_exact + (scaled * np.float32(half - max_exact)).astype(np.int32), half - 1)
    return np.where(rel > 0, half, 0) + np.where(n < max_exact, n, large)


def _dsa_bias_table(rel_bias):
    width = DSA_QBLK + 2 * DSA_SIDE
    tabs = []
    for gi, (_, dil) in enumerate(DSA_PATTERNS):
        offs = np.arange(-DSA_SIDE, DSA_SIDE + 1, dtype=np.int32) * dil
        band = rel_bias[_t5_bucket(offs)][:, gi * DSA_HPG:(gi + 1) * DSA_HPG].astype(F32)
        fill = jnp.full((DSA_QBLK - 1, DSA_HPG), NEG_INF, F32)
        line = jnp.concatenate([fill, band, fill], axis=0)
        tabs.append(jnp.stack([line[DSA_QBLK - 1 - q:DSA_QBLK - 1 - q + width] for q in range(DSA_QBLK)]))
    return jnp.transpose(jnp.stack(tabs), (3, 0, 1, 2))


def _head_block_diag():
    ii = lax.broadcasted_iota(jnp.int32, (LANES, LANES), 0)
    jj = lax.broadcasted_iota(jnp.int32, (LANES, LANES), 1)
    return (ii // RW_HEAD_DIM) == (jj // RW_HEAD_DIM)


def _rw_prep_body(r_ref, k_ref, v_ref, lo_ref, mur_ref, muk_ref, muv_ref, mulo_ref, w0_ref, w2_ref, a0_ref,
                  a2_ref, g2_ref, kk_ref, ka_ref, rk_ref,
                  ro_ref, vo_ref, kko_ref, bon_ref, gate_ref, lwf_ref, lwb_ref, kdf_ref, kdb_ref, bbf_ref, bbb_ref):
    s = r_ref.shape[0]
    row = lax.broadcasted_iota(jnp.int32, (s, 1), 0)

    def mix(t, mu):
        return t + mu * (0.5 * (_shift_down(t, row) + _shift_up(t, row)) - t)

    r = mix(r_ref[...].astype(F32), mur_ref[...])
    kr = mix(k_ref[...].astype(F32), muk_ref[...])
    v = mix(v_ref[...].astype(F32), muv_ref[...])
    lo = mix(lo_ref[...], mulo_ref[...])
    bd = jnp.where(_head_block_diag(), 1.0, 0.0)

    def head_sum(t):
        return _dot_sum(t, bd)

    kk = kr * kk_ref[...]
    kk = kk * lax.rsqrt(head_sum(kk * kk) + L2_EPS)
    gd = lo[:, 4 * RW_LORA:]
    gate_ref[...] = _dot1(_sigmoid(gd), g2_ref[...])
    ro_ref[...] = r
    vo_ref[...] = v
    kko_ref[...] = kk
    bonus = jnp.zeros_like(r)
    outs = ((lwf_ref, kdf_ref, bbf_ref), (lwb_ref, kdb_ref, bbb_ref))
    for d in range(2):
        wd = lo[:, d * RW_LORA:(d + 1) * RW_LORA]
        ad = lo[:, (2 + d) * RW_LORA:(3 + d) * RW_LORA]
        w_log = -_softplus(-(w0_ref[d:d + 1, :] + _dot1(jnp.tanh(wd), w2_ref[d]))) - 0.5
        a = _sigmoid(a0_ref[d:d + 1, :] + _dot1(ad, a2_ref[d]))
        kd = kr * (1.0 + (a - 1.0) * ka_ref[...])
        lw_ref, kd_ref, bb_ref = outs[d]
        lw_ref[...] = -jnp.exp(w_log)
        kd_ref[...] = kd
        bb_ref[...] = kk * a
        bonus = bonus + head_sum(r * kd * rk_ref[...]) * v
    bon_ref[...] = bonus


def rwkv_prep(p, lora_in, mu, w0, w2, a0, a2, g2, k_k, k_a, r_k, col0, batch):
    t = p.shape[0]
    s = t // batch
    nct = RW_WIDTH // LANES
    base = col0 // LANES
    nlo = RW_LORA_IN
    col = lambda off: pl.BlockSpec((s, LANES), lambda b, c: (b, base + off * nct + c))
    vec = lambda off: pl.BlockSpec((1, LANES), lambda b, c: (0, off * nct + c))
    mu_main = mu[:RW_MAIN].reshape(1, RW_MAIN)
    mu_lo = mu[RW_MAIN:].reshape(1, nlo)
    out_spec = pl.BlockSpec((s, LANES), lambda b, c: (b, c))
    n_out = 11
    return pl.pallas_call(
        _rw_prep_body,
        grid=(batch, nct),
        in_specs=[col(0), col(1), col(2),
                  pl.BlockSpec((s, nlo), lambda b, c: (b, 0)),
                  vec(0), vec(1), vec(2),
                  pl.BlockSpec((1, nlo), lambda b, c: (0, 0)),
                  pl.BlockSpec((2, LANES), lambda b, c: (0, c)),
                  pl.BlockSpec((2, RW_LORA, LANES), lambda b, c: (0, 0, c)),
                  pl.BlockSpec((2, LANES), lambda b, c: (0, c)),
                  pl.BlockSpec((2, RW_LORA, LANES), lambda b, c: (0, 0, c)),
                  pl.BlockSpec((RW_GATE_LORA, LANES), lambda b, c: (0, c)),
                  vec(0), vec(0), vec(0)],
        out_specs=[out_spec] * n_out,
        out_shape=[jax.ShapeDtypeStruct((t, RW_WIDTH), F32)] * n_out,
        compiler_params=_params("parallel", "parallel"),
        name="rwkv_prep",
    )(p, p, p, lora_in, mu_main, mu_main, mu_main, mu_lo, w0, w2, a0, a2, g2,
      k_k.reshape(1, RW_WIDTH), k_a.reshape(1, RW_WIDTH), r_k.reshape(1, RW_WIDTH))


def _rw_scan_body(r_ref, v_ref, kk_ref, bon_ref, gate_ref, lwf_ref, lwb_ref, kdf_ref, kdb_ref, bbf_ref, bbb_ref,
                  lnw_ref, lnb_ref, y_ref, cum_ref, cumt_ref, kdt_ref, bbt_ref, yf_ref, yb_ref,
                  tr_ref, tc_ref, rbk_ref, bkt_ref):
    s = r_ref.shape[0]
    c = CHUNK
    c2 = 2 * c
    row = lax.broadcasted_iota(jnp.int32, (s, 1), 0)
    pos = row % c
    lw_refs, kd_refs, bb_refs = (lwf_ref, lwb_ref), (kdf_ref, kdb_ref), (bbf_ref, bbb_ref)
    for d in range(2):
        cum = _seg_cumsum(lw_refs[d][...], pos, d == 1)
        cum_ref[d] = cum
        cumt_ref[d] = cum.T
        kdt_ref[d] = kd_refs[d][...].T
        bbt_ref[d] = bb_refs[d][...].T

    masks = _tri_masks(c2, c)
    merge = {rev: _merge_masks(c2, rev) for rev in (False, True)}
    eye2_f = jnp.where(masks["eye"], 1.0, 0.0)
    head0 = lax.broadcasted_iota(jnp.int32, (1, LANES), 1) < RW_HEAD_DIM
    head_bd = _head_block_diag()
    npair = s // PAIR
    group = min(RW_PREP_PAIRS, npair)

    def stack(x):
        return jnp.concatenate([jnp.where(head0, x, 0.0), jnp.where(head0, 0.0, x)], axis=0)

    def prep(pidx, d, half):
        reverse = d == 1
        incl2, strict2 = masks[reverse]
        cidx = 2 * pidx + half
        cols = pl.ds(pl.multiple_of(pidx * PAIR, PAIR), PAIR)
        rows = pl.ds(pl.multiple_of(cidx * c, c), c)
        hs = slice(half * c, (half + 1) * c)
        r, v, kk = r_ref[rows, :], v_ref[rows, :], kk_ref[rows, :]
        lw, kd, bb = lw_refs[d][rows, :], kd_refs[d][rows, :], bb_refs[d][rows, :]
        cum = cum_ref[d, rows, :]
        e_neg = jnp.exp(-cum)
        a_s = stack(-kk * jnp.exp(cum - lw))
        r_s = stack(r * jnp.exp(cum))
        ar = jnp.concatenate([a_s, r_s], axis=0)
        bk = jnp.concatenate([stack(bb * e_neg), stack(kd * e_neg)], axis=0)
        g = _dot1(ar, bk, _NT)
        yield
        rbk_ref[d, cidx] = jnp.concatenate([jnp.where(incl2, g[c2:, :c2], 0.0),
                                            jnp.where(incl2, g[c2:, c2:], 0.0)], axis=1).astype(BF16)
        av = _dot1(jnp.where(strict2, g[:c2, c2:], 0.0), stack(v))
        yield
        t_inv = []
        yield from _tri_inverse(-jnp.where(strict2, g[:c2, :c2], 0.0), eye2_f, merge[reverse], t_inv)
        tt = _dot1(t_inv[0], jnp.concatenate([a_s, av], axis=1))
        yield
        tr_ref[d, cidx] = jnp.concatenate([tt[:, :LANES], r_s], axis=0).astype(BF16)
        tc_ref[d, cidx] = tt[:, LANES:].astype(BF16)
        cum_t = cumt_ref[d, :, cols][:, hs]
        last = 0 if reverse else c - 1
        e_out_t = jnp.exp(cum_t[:, last:last + 1] - cum_t)
        bkt_ref[d, cidx] = jnp.concatenate([bbt_ref[d, :, cols][:, hs] * e_out_t,
                                            kdt_ref[d, :, cols][:, hs] * e_out_t], axis=1).astype(BF16)

    def seq(states, pidx, d):
        cum_t2 = cumt_ref[d, :, pl.ds(pl.multiple_of(pidx * PAIR, PAIR), PAIR)]
        for half in ((1, 0) if d else (0, 1)):
            cidx = 2 * pidx + half
            rows = pl.ds(pl.multiple_of(cidx * c, c), c)
            v = v_ref[rows, :]
            x = _dot1(tr_ref[d, cidx], states[d])
            yield
            ps = x[:c2] + tc_ref[d, cidx].astype(F32)
            os_ = x[c2:] + _dot1(rbk_ref[d, cidx], jnp.concatenate([ps, stack(v)], axis=0))
            y = os_[:c] + os_[c:]
            if d:
                yb_ref[rows, :] = y
            else:
                yf_ref[rows, :] = y
            pv = jnp.concatenate([ps[:c] + ps[c:], v], axis=0)
            last = half * c + (0 if d else c - 1)
            e_tot = jnp.exp(cum_t2[:, last:last + 1])
            states[d] = states[d] * e_tot + jnp.where(head_bd, _dot1(bkt_ref[d, cidx], pv), 0.0)
            yield

    ngroups = npair // group

    def prep_chains(g):
        return [prep(npair - 1 - (g * group + j) if d else g * group + j, d, half)
                for j in range(group) for d in range(2) for half in range(2)]

    def seq_chain(states, g, d):
        for j in range(group):
            yield from seq(states, npair - 1 - (g * group + j) if d else g * group + j, d)

    def step(g, carry):
        states = list(carry)
        _interleave(prep_chains(g + 1) + [seq_chain(states, g, 0), seq_chain(states, g, 1)])
        return tuple(states)

    zero = jnp.zeros((LANES, LANES), F32)
    _interleave(prep_chains(0))
    states = list(lax.fori_loop(0, ngroups - 1, step, (zero, zero)))
    _interleave([seq_chain(states, ngroups - 1, 0), seq_chain(states, ngroups - 1, 1)])

    bd = jnp.where(head_bd, 1.0, 0.0)
    y = yf_ref[...] + yb_ref[...]
    mean = _dot_sum(y, bd) * (1.0 / RW_HEAD_DIM)
    yc = y - mean
    var = _dot_sum(yc * yc, bd) * (1.0 / RW_HEAD_DIM)
    yn = yc * lax.rsqrt(var + RW_GN_EPS) * lnw_ref[...] + lnb_ref[...]
    y_ref[...] = ((yn + bon_ref[...]) * gate_ref[...]).astype(y_ref.dtype)


def rwkv_scan(prep, ln_w, ln_b, batch):
    t = prep[0].shape[0]
    s = t // batch
    assert s % PAIR == 0 and (s // PAIR) % min(RW_PREP_PAIRS, s // PAIR) == 0
    nct = RW_WIDTH // LANES
    nchunk = s // CHUNK
    blk = pl.BlockSpec((s, LANES), lambda b, c: (b, c))
    vec = pl.BlockSpec((1, LANES), lambda b, c: (0, c))
    tsp = pltpu.VMEM((2, LANES, s), F32)
    big = pltpu.VMEM((s, LANES), F32)
    per_chunk = lambda rows, cols: pltpu.VMEM((2, nchunk, rows, cols), BF16)
    return pl.pallas_call(
        _rw_scan_body,
        grid=(batch, nct),
        in_specs=[blk] * 11 + [vec, vec],
        out_specs=blk,
        out_shape=jax.ShapeDtypeStruct((t, RW_WIDTH), BF16),
        scratch_shapes=[pltpu.VMEM((2, s, LANES), F32), tsp, tsp, tsp, big, big,
                        per_chunk(2 * PAIR, LANES), per_chunk(PAIR, LANES), per_chunk(PAIR, 2 * LANES),
                        per_chunk(PAIR, LANES)],
        compiler_params=_params("parallel", "parallel"),
        name="rwkv_scan",
    )(*prep, ln_w.reshape(1, RW_WIDTH), ln_b.reshape(1, RW_WIDTH))


def even_mixer(x, g, w_in, w_sc, w_ab, w_out, layer, conv_qkv, a_log, dt_bias, out_gain, conv_sc, batch):
    p, ab = norm_mm(x, g, w_in, layer, ncols=4 * DN_WIDTH, w2=w_sc, wp_t=w_ab.T, out_dtype=BF16)
    zeros = jnp.zeros((2 * DN_HEADS,), F32)
    alog32 = jnp.concatenate([zeros, a_log.reshape(-1)]).reshape(1, -1)
    dt32 = jnp.concatenate([zeros, dt_bias.reshape(-1)]).reshape(1, -1)
    y_dn = deltanet(p, ab, conv_qkv.T, alog32, dt32, out_gain, batch)
    y_sc = short_conv(p, conv_sc.T, 4 * DN_WIDTH, batch)
    return mm_res(y_dn, w_out, layer, x, a2=y_sc, tn=D_MODEL)


def odd_mixer(x, g, w_in, w_lora, w_out, layer, qn, kn, bias_tab, mu, w0, w2, a0, a2, g2, k_k, k_a, r_k, ln_w, ln_b,
              batch):
    p, lora_in = norm_mm(x, g, w_in, layer, ncols=3 * DSA_QKV + RW_MAIN, wp=w_lora, out_dtype=BF16)
    y_c = dilated_attention(p, qn, kn, bias_tab, batch)
    prep = rwkv_prep(p, lora_in, mu, w0, w2, a0, a2, g2, k_k, k_a, r_k, 3 * DSA_QKV, batch)
    y_d = rwkv_scan(prep, ln_w, ln_b, batch)
    return mm_res(y_c, w_out, layer, x, a2=y_d, tn=D_MODEL)


def kernel(x, mem, rel_bias, norm_mix, norm_xattn, norm_mem, norm_ffn, xa_wq, xa_wk, xa_wv, xa_wo, xa_qn, xa_kn, ffn_w1, ffn_w2, ev_w_in, ev_w_out, dn_conv, dn_a_log, dn_dt_bias, dn_norm, sc_conv, od_w_in, od_w_out, ca_qn, ca_kn, rw_mu, rw_w0, rw_w2, rw_a0, rw_a2, rw_g2, rw_k_k, rw_k_a, rw_r_k, rw_ln_w, rw_ln_b):
    batch, seq, d = x.shape
    n_mem = mem.shape[1]
    xf = x.reshape(batch * seq, d)
    memf = mem.reshape(batch * n_mem, d)
    bias_tab = _dsa_bias_table(rel_bias)
    c_ab = 4 * DN_WIDTH
    c_sc = c_ab + 4 * DN_HEADS
    c_lo = 3 * DSA_QKV + RW_MAIN
    ev_in, od_in = ev_w_in.astype(BF16), od_w_in.astype(BF16)
    ev_sc = ev_in[:, :, c_sc:]
    ev_out, od_out = ev_w_out.astype(BF16), od_w_out.astype(BF16)
    w_q, w_o = xa_wq.astype(BF16), xa_wo.astype(BF16)
    w_kv = jnp.concatenate([xa_wk, xa_wv], axis=2).astype(BF16)
    for layer in range(DEPTH):
        i = layer // 2
        if layer % 2 == 0:
            xf = even_mixer(xf, norm_mix[layer], ev_in, ev_sc, ev_w_in[i, :, c_ab:c_sc], ev_out, i, dn_conv[i],
                            dn_a_log[i], dn_dt_bias[i], dn_norm[i], sc_conv[i], batch)
        else:
            xf = odd_mixer(xf, norm_mix[layer], od_in, od_w_in[i, :, c_lo:], od_out, i, ca_qn[i], ca_kn[i], bias_tab,
                           rw_mu[i], rw_w0[i], rw_w2[i], rw_a0[i], rw_a2[i], rw_g2[i], rw_k_k[i], rw_k_a[i], rw_r_k[i],
                           rw_ln_w[i], rw_ln_b[i], batch)
        kv = norm_mm(memf, norm_mem[layer], w_kv, layer).reshape(batch, n_mem, 2 * XA_WIDTH)
        xf, hn = xattn(xf, norm_xattn[layer], w_q, kv, w_o, layer, xa_qn[layer], xa_kn[layer], norm_ffn[layer], batch)
        h1 = act_mm(hn, ffn_w1, layer, act="relu2", out_dtype=BF16)
        xf = mm_res(h1, ffn_w2, layer, xf)
    return xf.reshape(batch, seq, d)
```

```python
import functools
import math

import jax
import jax.numpy as jnp
import numpy as np
from jax import lax
from jax.experimental import pallas as pl
from jax.experimental.pallas import tpu as pltpu

F32 = jnp.float32
BF16 = jnp.bfloat16

D_MODEL = 2048
DEPTH = 4
RMS_EPS = 1e-6
L2_EPS = 1e-6

DN_HEADS = 8
DN_HEAD_DIM = 128
DN_WIDTH = DN_HEADS * DN_HEAD_DIM
SC_WIDTH = D_MODEL - DN_WIDTH
CHUNK = 64
PAIR = 2 * CHUNK
DN_PREP_PAIRS = 4
DN_HEADS_PER_STEP = 2
RW_PREP_PAIRS = 4

DSA_PATTERNS = ((128, 1), (512, 4), (2048, 16))
DSA_GROUPS = len(DSA_PATTERNS)
DSA_HPG = 4
DSA_HEAD_DIM = 128
DSA_HEADS = DSA_GROUPS * DSA_HPG
DSA_QKV = DSA_HEADS * DSA_HEAD_DIM
DSA_SIDE = 64
DSA_QBLK = 128
DSA_PAD = DSA_SIDE * max(d for _, d in DSA_PATTERNS)
DSA_INTERLEAVE = 8
REL_BUCKETS = 32
REL_MAX_DIST = 1024
NEG_INF = -1e30

RW_HEADS = 8
RW_HEAD_DIM = 64
RW_WIDTH = RW_HEADS * RW_HEAD_DIM
RW_LORA = 64
RW_GATE_LORA = 128
RW_MAIN = 3 * RW_WIDTH
RW_LORA_IN = 4 * RW_LORA + RW_GATE_LORA
RW_GN_EPS = 64e-5

XA_HEADS = 4
XA_HEAD_DIM = 128
XA_WIDTH = XA_HEADS * XA_HEAD_DIM

LANES = 128
SUBLANES = 8
NORM_ROWS = 256
VMEM_LIMIT_BYTES = 58 * 1024 * 1024


def _params(*sem):
    return pltpu.CompilerParams(dimension_semantics=sem, vmem_limit_bytes=VMEM_LIMIT_BYTES)


_NN = (((1,), (0,)), ((), ()))
_NT = (((1,), (1,)), ((), ()))


def _dot1(a, b, dims=_NN):
    return lax.dot_general(a.astype(BF16), b.astype(BF16), dims, preferred_element_type=F32)


def _dot_sum(a, b01):
    b = b01.astype(BF16)
    hi = a.astype(BF16)
    lo = (a - hi.astype(F32)).astype(BF16)
    dg = functools.partial(lax.dot_general, dimension_numbers=_NN, preferred_element_type=F32)
    return dg(hi, b) + dg(lo, b)


def _interleave(chains):
    chains = list(chains)
    while chains:
        alive = []
        for ch in chains:
            try:
                next(ch)
                alive.append(ch)
            except StopIteration:
                pass
        chains = alive


def _merge_masks(n, reverse):
    ii = lax.broadcasted_iota(jnp.int32, (n, n), 0)
    jj = lax.broadcasted_iota(jnp.int32, (n, n), 1)
    tri = (ii < jj) if reverse else (ii > jj)
    out = []
    size = 1
    while size < CHUNK:
        out.append(tri & ((ii // (2 * size)) == (jj // (2 * size))) & ((ii // size) != (jj // size)))
        size *= 2
    return out


def _tri_inverse(a, eye_f, level_masks, out):
    t = eye_f - jnp.where(level_masks[0], a, 0.0)
    for m in level_masks[1:]:
        x = _dot1(jnp.where(m, a, 0.0), t)
        yield
        t = t - _dot1(t, x)
        yield
    out.append(t)


def _sigmoid(x):
    return 1.0 / (1.0 + jnp.exp(-x))


def _softplus(x):
    return jnp.maximum(x, 0.0) + jnp.log1p(jnp.exp(-jnp.abs(x)))


def _shift_down(x, row):
    return jnp.where(row == 0, 0.0, pltpu.roll(x, 1, 0))


def _shift_up(x, row):
    n = x.shape[0]
    return jnp.where(row == n - 1, 0.0, pltpu.roll(x, n - 1, 0))


def _norm_mm_body(x_ref, g_ref, w_ref, *rest, act, extra, extra_t, n_first):
    if n_first is not None:
        w2_ref, rest = rest[0], rest[1:]
    if extra:
        wp_ref, o_ref, op_ref, xn_ref = rest
    else:
        o_ref, xn_ref = rest

    j = pl.program_id(1)

    def project(wt_ref, rows=slice(None)):
        acc = jnp.dot(xn_ref[rows, :], wt_ref[...].astype(BF16), preferred_element_type=F32)
        if act == "relu2":
            acc = jnp.square(jnp.maximum(acc, 0.0))
        o_ref[rows, :] = acc.astype(o_ref.dtype)

    @pl.when(j == 0)
    def _():
        for r in range(x_ref.shape[0] // NORM_ROWS):
            rows = slice(r * NORM_ROWS, (r + 1) * NORM_ROWS)
            x = x_ref[rows, :]
            xn = x * lax.rsqrt(jnp.mean(x * x, axis=-1, keepdims=True) + RMS_EPS) * g_ref[...]
            xn_ref[rows, :] = xn.astype(BF16)
            if extra and extra_t:
                op_ref[:, rows] = _dot1(wp_ref[...], xn, _NT)
            elif extra:
                op_ref[rows, :] = _dot1(xn, wp_ref[...])
            project(w_ref, rows)

    if n_first is None:
        pl.when(j > 0)(lambda: project(w_ref))
    else:
        pl.when((j > 0) & (j < n_first))(lambda: project(w_ref))
        pl.when(j >= n_first)(lambda: project(w2_ref))


def norm_mm(x, g, w, layer, ncols=None, w2=None, wp=None, wp_t=None, act=None, out_dtype=F32, tm=1024, tn=1024):
    t, d = x.shape
    n1 = w.shape[2] if ncols is None else ncols
    n = n1 + (0 if w2 is None else w2.shape[2])
    tm = min(tm, t)
    tn = min(tn, n1)
    assert t % tm == 0 and n1 % tn == 0 and n % tn == 0 and tm % NORM_ROWS == 0
    extra = wp is not None or wp_t is not None
    n_first = None if w2 is None else n1 // tn
    in_specs = [
        pl.BlockSpec((tm, d), lambda i, j: (i, 0)),
        pl.BlockSpec((1, d), lambda i, j: (0, 0)),
    ]
    args = [x, g.reshape(1, d), w]
    if w2 is None:
        in_specs.append(pl.BlockSpec((None, d, tn), lambda i, j: (layer, 0, j)))
    else:
        in_specs.append(pl.BlockSpec((None, d, tn), lambda i, j: (layer, 0, jnp.minimum(j, n_first - 1))))
        in_specs.append(pl.BlockSpec((None, d, tn), lambda i, j: (layer, 0, jnp.maximum(j - n_first, 0))))
        args.append(w2)
    out_specs = pl.BlockSpec((tm, tn), lambda i, j: (i, j))
    out_shape = jax.ShapeDtypeStruct((t, n), out_dtype)
    if wp is not None:
        npc = wp.shape[1]
        in_specs.append(pl.BlockSpec((d, npc), lambda i, j: (0, 0)))
        out_specs = [out_specs, pl.BlockSpec((tm, npc), lambda i, j: (i, 0))]
        out_shape = [out_shape, jax.ShapeDtypeStruct((t, npc), F32)]
        args.append(wp)
    elif wp_t is not None:
        npc = wp_t.shape[0]
        in_specs.append(pl.BlockSpec((npc, d), lambda i, j: (0, 0)))
        out_specs = [out_specs, pl.BlockSpec((npc, tm), lambda i, j: (0, i))]
        out_shape = [out_shape, jax.ShapeDtypeStruct((npc, t), F32)]
        args.append(wp_t)
    return pl.pallas_call(
        functools.partial(_norm_mm_body, act=act, extra=extra, extra_t=wp_t is not None, n_first=n_first),
        grid=(t // tm, n // tn),
        in_specs=in_specs,
        out_specs=out_specs,
        out_shape=out_shape,
        scratch_shapes=[pltpu.VMEM((tm, d), BF16)],
        compiler_params=_params("parallel", "arbitrary"),
        name="norm_mm",
    )(*args)


def _act_mm_body(h_ref, w_ref, o_ref, *, act):
    acc = jnp.dot(h_ref[...], w_ref[...].astype(BF16), preferred_element_type=F32)
    if act == "relu2":
        acc = jnp.square(jnp.maximum(acc, 0.0))
    o_ref[...] = acc.astype(o_ref.dtype)


def act_mm(h, w, layer, act=None, out_dtype=F32, tm=2048, tn=1024):
    t, d = h.shape
    n = w.shape[2]
    tm, tn = min(tm, t), min(tn, n)
    assert t % tm == 0 and n % tn == 0
    return pl.pallas_call(
        functools.partial(_act_mm_body, act=act),
        grid=(t // tm, n // tn),
        in_specs=[pl.BlockSpec((tm, d), lambda i, j: (i, 0)),
                  pl.BlockSpec((None, d, tn), lambda i, j: (layer, 0, j))],
        out_specs=pl.BlockSpec((tm, tn), lambda i, j: (i, j)),
        out_shape=jax.ShapeDtypeStruct((t, n), out_dtype),
        compiler_params=_params("parallel", "arbitrary"),
        name="act_mm",
    )(h, w)


def _mm_res_body(a_ref, *rest, n_first):
    if n_first is not None:
        a2_ref, rest = rest[0], rest[1:]
    w_ref, r_ref, o_ref = rest
    kk = pl.program_id(2)

    def accumulate(lhs_ref):
        def product():
            return jnp.dot(lhs_ref[...], w_ref[...].astype(BF16), preferred_element_type=F32)

        @pl.when(kk == 0)
        def _():
            o_ref[...] = r_ref[...] + product()

        @pl.when(kk > 0)
        def _():
            o_ref[...] += product()

    if n_first is None:
        accumulate(a_ref)
    else:
        pl.when(kk < n_first)(lambda: accumulate(a_ref))
        pl.when(kk >= n_first)(lambda: accumulate(a2_ref))


def mm_res(a, w, layer, res, a2=None, tm=1024, tn=1024, tk=2048):
    t, k1 = a.shape
    k = k1 + (0 if a2 is None else a2.shape[1])
    n = w.shape[2]
    tm, tn, tk = min(tm, t), min(tn, n), min(tk, k1)
    assert t % tm == 0 and n % tn == 0 and k1 % tk == 0 and k % tk == 0
    n_first = None if a2 is None else k1 // tk
    if a2 is None:
        in_specs = [pl.BlockSpec((tm, tk), lambda i, j, kk: (i, kk))]
        args = [a]
    else:
        in_specs = [pl.BlockSpec((tm, tk), lambda i, j, kk: (i, jnp.minimum(kk, n_first - 1))),
                    pl.BlockSpec((tm, tk), lambda i, j, kk: (i, jnp.maximum(kk - n_first, 0)))]
        args = [a, a2]
    in_specs += [
        pl.BlockSpec((None, tk, tn), lambda i, j, kk: (layer, kk, j)),
        pl.BlockSpec((tm, tn), lambda i, j, kk: (i, j)),
    ]
    return pl.pallas_call(
        functools.partial(_mm_res_body, n_first=n_first),
        grid=(t // tm, n // tn, k // tk),
        in_specs=in_specs,
        out_specs=pl.BlockSpec((tm, tn), lambda i, j, kk: (i, j)),
        out_shape=jax.ShapeDtypeStruct((t, n), F32),
        compiler_params=_params("parallel", "parallel", "arbitrary"),
        name="mm_res",
    )(*args, w, res)


def _xattn_body(x_ref, g_ref, wq_ref, kv_ref, wo_ref, qn_ref, kn_ref, g2_ref, o_ref, hn_ref):
    x = x_ref[...]
    xn = x * lax.rsqrt(jnp.mean(x * x, axis=-1, keepdims=True) + RMS_EPS) * g_ref[...]
    q = jnp.dot(xn.astype(BF16), wq_ref[...], preferred_element_type=F32)
    kv = kv_ref[0]
    outs = []
    for h in range(XA_HEADS):
        sl = slice(h * XA_HEAD_DIM, (h + 1) * XA_HEAD_DIM)
        qh = q[:, sl]
        qh = qh * lax.rsqrt(jnp.mean(qh * qh, axis=-1, keepdims=True) + RMS_EPS) * qn_ref[...]
        kh = kv[:, sl]
        kh = kh * lax.rsqrt(jnp.mean(kh * kh, axis=-1, keepdims=True) + RMS_EPS) * kn_ref[...]
        vh = kv[:, XA_WIDTH + h * XA_HEAD_DIM:XA_WIDTH + (h + 1) * XA_HEAD_DIM]
        logits = _dot1(qh, kh, _NT) * (XA_HEAD_DIM ** -0.5)
        m = jnp.max(logits, axis=-1, keepdims=True)
        p = jnp.exp(logits - m)
        s = jnp.sum(p, axis=-1, keepdims=True)
        outs.append(_dot1(p, vh) / s)
    o = jnp.concatenate(outs, axis=-1).astype(BF16)
    xo = x + jnp.dot(o, wo_ref[...], preferred_element_type=F32)
    o_ref[...] = xo
    hn_ref[...] = (xo * lax.rsqrt(jnp.mean(xo * xo, axis=-1, keepdims=True) + RMS_EPS) * g2_ref[...]).astype(BF16)


def xattn(x, g, wq, kv, wo, layer, qn, kn, g_next, batch, ts=1024):
    t, d = x.shape
    s = t // batch
    ts = min(ts, s)
    nst = s // ts
    m = kv.shape[1]
    return pl.pallas_call(
        _xattn_body,
        grid=(batch, nst),
        in_specs=[
            pl.BlockSpec((ts, d), lambda b, i: (b * nst + i, 0)),
            pl.BlockSpec((1, d), lambda b, i: (0, 0)),
            pl.BlockSpec((None, d, XA_WIDTH), lambda b, i: (layer, 0, 0)),
            pl.BlockSpec((1, m, 2 * XA_WIDTH), lambda b, i: (b, 0, 0)),
            pl.BlockSpec((None, XA_WIDTH, d), lambda b, i: (layer, 0, 0)),
            pl.BlockSpec((1, XA_HEAD_DIM), lambda b, i: (0, 0)),
            pl.BlockSpec((1, XA_HEAD_DIM), lambda b, i: (0, 0)),
            pl.BlockSpec((1, d), lambda b, i: (0, 0)),
        ],
        out_specs=[pl.BlockSpec((ts, d), lambda b, i: (b * nst + i, 0))] * 2,
        out_shape=[jax.ShapeDtypeStruct((t, d), F32), jax.ShapeDtypeStruct((t, d), BF16)],
        compiler_params=_params("parallel", "parallel"),
        name="xattn",
    )(x, g.reshape(1, d), wq, kv, wo, qn.reshape(1, -1), kn.reshape(1, -1), g_next.reshape(1, d))


def _tri_masks(n, blk):
    ii = lax.broadcasted_iota(jnp.int32, (n, n), 0)
    jj = lax.broadcasted_iota(jnp.int32, (n, n), 1)
    same = (ii // blk) == (jj // blk) if n != blk else None

    def m(c):
        return c if same is None else (c & same)

    return {
        False: (m(ii >= jj), m(ii > jj)),
        True: (m(ii <= jj), m(ii < jj)),
        "eye": ii == jj,
    }


def _seg_cumsum(x, pos, reverse):
    n = x.shape[0]
    sh = 1
    while sh < CHUNK:
        if reverse:
            x = x + jnp.where(pos < CHUNK - sh, pltpu.roll(x, n - sh, 0), 0.0)
        else:
            x = x + jnp.where(pos >= sh, pltpu.roll(x, sh, 0), 0.0)
        sh *= 2
    return x


def _dn_body(q_ref, k_ref, v_ref, gate_ref, ab_ref, cq_ref, ck_ref, cv_ref, alog_ref, dt_ref, gain_ref,
             y_ref, qs_ref, ks_ref, vs_ref, kt_ref, bg_ref, gt_ref, of_ref, ob_ref,
             u_ref, w_ref, qg_ref, qk_ref, kdt_ref, eg_ref, st_ref):
    s = q_ref.shape[0]
    hd = DN_HEAD_DIM
    nh = DN_HEADS_PER_STEP
    row = lax.broadcasted_iota(jnp.int32, (s, 1), 0)
    c = CHUNK
    lane32 = lax.broadcasted_iota(jnp.int32, (1, 4 * DN_HEADS), 1)
    lane128 = lax.broadcasted_iota(jnp.int32, (1, LANES), 1)
    pos_t = lax.broadcasted_iota(jnp.int32, (1, s), 1) % c

    def conv_silu(x_ref, cw_ref, lanes):
        x = x_ref[:, lanes].astype(F32)
        w = cw_ref[:, lanes]
        y = _shift_down(x, row) * w[0:1] + x * w[1:2] + _shift_up(x, row) * w[2:3]
        return y * _sigmoid(y)

    for hh in range(nh):
        h = pl.program_id(1) * nh + hh
        lanes = slice(hh * hd, (hh + 1) * hd)
        q = conv_silu(q_ref, cq_ref, lanes)
        q = q * lax.rsqrt(jnp.sum(q * q, axis=-1, keepdims=True) + L2_EPS) * (hd ** -0.5)
        qs_ref[hh] = q
        k = conv_silu(k_ref, ck_ref, lanes)
        k = k * lax.rsqrt(jnp.sum(k * k, axis=-1, keepdims=True) + L2_EPS)
        ks_ref[hh] = k
        kt_ref[hh] = k.T
        vs_ref[hh] = conv_silu(v_ref, cv_ref, lanes)

        def decay_row(d, h=h):
            idx = (2 + d) * DN_HEADS + h
            a_log = jnp.sum(jnp.where(lane32 == idx, alog_ref[...], 0.0), axis=-1, keepdims=True)
            dt = jnp.sum(jnp.where(lane32 == idx, dt_ref[...], 0.0), axis=-1, keepdims=True)
            g = -jnp.exp(a_log) * _softplus(ab_ref[pl.ds(idx, 1), :] + dt)
            sh = 1
            while sh < c:
                if d:
                    g = g + jnp.where(pos_t < c - sh, pltpu.roll(g, s - sh, 1), 0.0)
                else:
                    g = g + jnp.where(pos_t >= sh, pltpu.roll(g, sh, 1), 0.0)
                sh *= 2
            return g

        rows = [_sigmoid(ab_ref[pl.ds(h, 1), :]), _sigmoid(ab_ref[pl.ds(DN_HEADS + h, 1), :]),
                decay_row(0), decay_row(1)]
        gt = jnp.concatenate(rows + [jnp.zeros((LANES - len(rows), s), F32)], axis=0)
        gt_ref[hh] = gt[0:SUBLANES, :]
        bg_ref[hh] = gt.T
        for d in range(2):
            st_ref[hh, d] = jnp.zeros((hd, hd), F32)

    masks = _tri_masks(PAIR, c)
    merge = {rev: _merge_masks(PAIR, rev) for rev in (False, True)}
    eye_f = jnp.where(masks["eye"], 1.0, 0.0)
    npair = s // PAIR
    group = min(DN_PREP_PAIRS, npair)

    def prep(hh, pidx, d):
        reverse = d == 1
        rows = pl.ds(pl.multiple_of(pidx * PAIR, PAIR), PAIR)
        qp, kp, vp = qs_ref[hh, rows, :], ks_ref[hh, rows, :], vs_ref[hh, rows, :]
        bgp = bg_ref[hh, rows, :]
        beta = bgp[:, d:d + 1]
        gcol = bgp[:, 2 + d:3 + d]
        grow = gt_ref[hh, 2 + d:3 + d, rows]
        incl, strict = masks[reverse]
        decay = jnp.where(incl, jnp.exp(jnp.where(incl, gcol - grow, 0.0)), 0.0)
        kb = kp * beta
        kq = _dot1(jnp.concatenate([kb, qp], axis=0), kp, _NT)
        yield
        qk_ref[hh, d, rows, :] = jnp.where(incl, kq[PAIR:] * decay, 0.0).astype(BF16)
        t_inv = []
        yield from _tri_inverse(jnp.where(strict, kq[:PAIR] * decay, 0.0), eye_f, merge[reverse], t_inv)
        egc = jnp.exp(gcol)
        uw = _dot1(t_inv[0], jnp.concatenate([vp * beta, kb * egc], axis=1))
        yield
        u_ref[hh, d, rows, :] = uw[:, :hd]
        w_ref[hh, d, rows, :] = uw[:, hd:].astype(BF16)
        qg_ref[hh, d, rows, :] = (qp * egc).astype(BF16)
        last0, last1 = (0, c) if reverse else (c - 1, PAIR - 1)
        glast = jnp.where(lane128 < c, grow[:, last0:last0 + 1], grow[:, last1:last1 + 1])
        kdt_ref[hh, d, :, rows] = (kt_ref[hh, :, rows] * jnp.exp(glast - grow)).astype(BF16)
        eg_ref[hh, d, :, rows] = jnp.broadcast_to(jnp.exp(glast), (SUBLANES, PAIR))

    def seq(hh, pidx, d):
        pair_rows = pl.ds(pl.multiple_of(pidx * PAIR, PAIR), PAIR)
        kdt = kdt_ref[hh, d, :, pair_rows]
        eg = eg_ref[hh, d, 0:1, pair_rows]
        for half in ((1, 0) if d else (0, 1)):
            r0 = half * c
            rows = pl.ds(pl.multiple_of(pidx * PAIR + r0, c), c)
            state = st_ref[hh, d]
            ws = _dot1(jnp.concatenate([w_ref[hh, d, rows, :], qg_ref[hh, d, rows, :]], axis=0), state)
            yield
            v_new = u_ref[hh, d, rows, :] - ws[:c]
            o = ws[c:] + _dot1(qk_ref[hh, d, rows, r0:r0 + c], v_new)
            if d:
                ob_ref[hh, rows, :] = o
            else:
                of_ref[hh, rows, :] = o
            st_ref[hh, d] = state * eg[:, r0:r0 + 1] + _dot1(kdt[:, r0:r0 + c], v_new)
            yield

    ngroups = npair // group

    def pair_of(g, j, d):
        return npair - 1 - (g * group + j) if d else g * group + j

    def prep_chains(g):
        return [prep(hh, pair_of(g, j, d), d) for j in range(group) for hh in range(nh) for d in range(2)]

    def seq_chain(g, hh, d):
        for j in range(group):
            yield from seq(hh, pair_of(g, j, d), d)

    def seq_chains(g):
        return [seq_chain(g, hh, d) for hh in range(nh) for d in range(2)]

    def step(g, _):
        _interleave(prep_chains(g + 1) + seq_chains(g))
        return 0

    _interleave(prep_chains(0))
    lax.fori_loop(0, ngroups - 1, step, 0)
    _interleave(seq_chains(ngroups - 1))

    for hh in range(nh):
        lanes = slice(hh * hd, (hh + 1) * hd)
        o = of_ref[hh] + ob_ref[hh]
        o = o * lax.rsqrt(jnp.mean(o * o, axis=-1, keepdims=True) + RMS_EPS) * gain_ref[...]
        gate = gate_ref[:, lanes].astype(F32)
        y_ref[:, lanes] = (o * (gate * _sigmoid(gate))).astype(y_ref.dtype)


def deltanet(p, ab, conv_t, alog32, dt32, gain, batch):
    t = p.shape[0]
    s = t // batch
    assert s % PAIR == 0 and (s // PAIR) % min(DN_PREP_PAIRS, s // PAIR) == 0
    hd = DN_HEAD_DIM
    nh = DN_HEADS_PER_STEP
    nsteps = DN_HEADS // nh
    col = lambda off: pl.BlockSpec((s, nh * hd), lambda b, h: (b, off * nsteps + h))
    cw = lambda off: pl.BlockSpec((3, nh * hd), lambda b, h: (0, off * nsteps + h))
    gate_col = pl.BlockSpec((s, nh * hd), lambda b, h: (b, 3 * nsteps + h), pipeline_mode=pl.Buffered(1))
    small = lambda n: pl.BlockSpec((1, n), lambda b, h: (0, 0))
    tok = pltpu.VMEM((nh, s, hd), F32)
    per_dir = pltpu.VMEM((nh, 2, s, hd), BF16)
    return pl.pallas_call(
        _dn_body,
        grid=(batch, nsteps),
        in_specs=[col(0), col(1), col(2), gate_col,
                  pl.BlockSpec((4 * DN_HEADS, s), lambda b, h: (0, b)),
                  cw(0), cw(1), cw(2),
                  small(4 * DN_HEADS), small(4 * DN_HEADS), small(hd)],
        out_specs=pl.BlockSpec((s, nh * hd), lambda b, h: (b, h)),
        out_shape=jax.ShapeDtypeStruct((t, DN_WIDTH), BF16),
        scratch_shapes=[tok, tok, tok, pltpu.VMEM((nh, hd, s), F32), tok, pltpu.VMEM((nh, SUBLANES, s), F32), tok, tok,
                        pltpu.VMEM((nh, 2, s, hd), F32), per_dir, per_dir, per_dir,
                        pltpu.VMEM((nh, 2, hd, s), BF16), pltpu.VMEM((nh, 2, SUBLANES, s), F32),
                        pltpu.VMEM((nh, 2, hd, hd), F32)],
        compiler_params=_params("parallel", "parallel"),
        name="deltanet",
    )(p, p, p, p, ab, conv_t, conv_t, conv_t, alog32, dt32, gain.reshape(1, hd))


def _sconv_body(b_ref, c_ref, u_ref, w_ref, y_ref):
    s = b_ref.shape[0]
    row = lax.broadcasted_iota(jnp.int32, (s, 1), 0)
    cu = c_ref[...].astype(F32) * u_ref[...].astype(F32)
    w = w_ref[...]
    y = _shift_down(cu, row) * w[0:1] + cu * w[1:2] + _shift_up(cu, row) * w[2:3]
    y_ref[...] = (b_ref[...].astype(F32) * y).astype(y_ref.dtype)


def short_conv(p, conv_t, col0, batch, tc=256):
    t = p.shape[0]
    s = t // batch
    nct = SC_WIDTH // tc
    base = col0 // tc
    col = lambda off: pl.BlockSpec((s, tc), lambda b, c: (b, base + off * nct + c))
    return pl.pallas_call(
        _sconv_body,
        grid=(batch, nct),
        in_specs=[col(0), col(1), col(2), pl.BlockSpec((3, tc), lambda b, c: (0, c))],
        out_specs=pl.BlockSpec((s, tc), lambda b, c: (b, c)),
        out_shape=jax.ShapeDtypeStruct((t, SC_WIDTH), BF16),
        compiler_params=_params("parallel", "parallel"),
        name="short_conv",
    )(p, p, p, conv_t)


def _dsa_body(*refs, seq):
    q_refs = refs[0:3]
    k_refs = refs[3:6]
    v_refs = refs[6:9]
    qn_ref, kn_ref, bias_ref, y_ref, qs_ref, kpad_ref, vpad_ref, og_ref, lse_ref = refs[9:]
    s = seq
    qb = DSA_QBLK
    side = DSA_SIDE
    width = qb + 2 * side
    kj = lax.broadcasted_iota(jnp.int32, (1, width), 1)
    zpad = jnp.zeros((DSA_PAD, DSA_HEAD_DIM), F32)
    for ref in (kpad_ref, vpad_ref):
        ref[0:DSA_PAD, :] = zpad
        ref[DSA_PAD + s:2 * DSA_PAD + s, :] = zpad

    for gi, (_, dil) in enumerate(DSA_PATTERNS):
        sub = s // dil
        nblk = sub // qb
        q = q_refs[gi][...].astype(F32)
        qs_ref[...] = (q * lax.rsqrt(jnp.mean(q * q, axis=-1, keepdims=True) + RMS_EPS) * qn_ref[...]
                       * (DSA_HEAD_DIM ** -0.5))
        k = k_refs[gi][...].astype(F32)
        kpad_ref[DSA_PAD:DSA_PAD + s, :] = k * lax.rsqrt(jnp.mean(k * k, axis=-1, keepdims=True) + RMS_EPS) * kn_ref[...]
        vpad_ref[DSA_PAD:DSA_PAD + s, :] = v_refs[gi][...].astype(F32)
        bias = bias_ref[0, gi]

        def block(t, gi=gi, dil=dil, sub=sub, nblk=nblk, bias=bias):
            r = t // nblk
            n = t % nblk
            rows = pl.ds(r + n * (qb * dil), qb, stride=dil)
            win = pl.ds(DSA_PAD + r + (n * qb - side) * dil, width, stride=dil)
            logits = _dot1(qs_ref[rows, :], kpad_ref[win, :], _NT) + bias
            yield
            pos = n * qb - side + kj
            logits = jnp.where((pos >= 0) & (pos < sub), logits, NEG_INF)
            m = jnp.max(logits, axis=-1, keepdims=True)
            yield
            p = jnp.exp(logits - m)
            ssum = jnp.sum(p, axis=-1, keepdims=True)
            o = _dot1(p, vpad_ref[win, :])
            yield
            og_ref[gi, rows, :] = o / ssum
            lse_ref[gi, rows, :] = jnp.broadcast_to(m + jnp.log(ssum), (qb, DSA_HEAD_DIM))

        nblocks = dil * nblk

        def blocks_step(i, _, block=block):
            _interleave(block(i * DSA_INTERLEAVE + u) for u in range(DSA_INTERLEAVE))
            return 0

        lax.fori_loop(0, nblocks // DSA_INTERLEAVE, blocks_step, 0)

    lse = [lse_ref[gi] for gi in range(DSA_GROUPS)]
    mx = jnp.maximum(jnp.maximum(lse[0], lse[1]), lse[2])
    ws = [jnp.exp(l - mx) for l in lse]
    num = ws[0] * og_ref[0] + ws[1] * og_ref[1] + ws[2] * og_ref[2]
    y_ref[...] = (num / (ws[0] + ws[1] + ws[2])).astype(y_ref.dtype)


def dilated_attention(p, qn, kn, bias_tab, batch):
    t = p.shape[0]
    s = t // batch
    assert (s // DSA_QBLK) % DSA_INTERLEAVE == 0
    hd = DSA_HEAD_DIM
    nh = DSA_HEADS

    def col(part, gi):
        return pl.BlockSpec((s, hd), lambda b, j: (b, part * nh + gi * DSA_HPG + j))

    in_specs = [col(part, gi) for part in range(3) for gi in range(DSA_GROUPS)]
    in_specs += [pl.BlockSpec((1, hd), lambda b, j: (0, 0)), pl.BlockSpec((1, hd), lambda b, j: (0, 0)),
                 pl.BlockSpec((1, DSA_GROUPS, DSA_QBLK, DSA_QBLK + 2 * DSA_SIDE), lambda b, j: (j, 0, 0, 0))]
    big = pltpu.VMEM((s, hd), F32)
    pad = pltpu.VMEM((s + 2 * DSA_PAD, hd), F32)
    grp = pltpu.VMEM((DSA_GROUPS, s, hd), F32)
    return pl.pallas_call(
        functools.partial(_dsa_body, seq=s),
        grid=(batch, DSA_HPG),
        in_specs=in_specs,
        out_specs=pl.BlockSpec((s, hd), lambda b, j: (b, j)),
        out_shape=jax.ShapeDtypeStruct((t, DSA_HPG * hd), BF16),
        scratch_shapes=[big, pad, pad, grp, grp],
        compiler_params=_params("parallel", "parallel"),
        name="dilated_attention",
    )(*([p] * 9), qn.reshape(1, hd), kn.reshape(1, hd), bias_tab)


def _t5_bucket(rel):
    half = REL_BUCKETS // 2
    max_exact = half // 2
    n = np.abs(rel)
    scaled = (np.log(np.maximum(n, max_exact).astype(np.float32) / np.float32(max_exact))
              / np.float32(math.log(REL_MAX_DIST / max_exact)))
    large = np.minimum(max_exact + (scaled * np.float32(half - max_exact)).astype(np.int32), half - 1)
    return np.where(rel > 0, half, 0) + np.where(n < max_exact, n, large)


def _dsa_bias_table(rel_bias):
    width = DSA_QBLK + 2 * DSA_SIDE
    tabs = []
    for gi, (_, dil) in enumerate(DSA_PATTERNS):
        offs = np.arange(-DSA_SIDE, DSA_SIDE + 1, dtype=np.int32) * dil
        band = rel_bias[_t5_bucket(offs)][:, gi * DSA_HPG:(gi + 1) * DSA_HPG].astype(F32)
        fill = jnp.full((DSA_QBLK - 1, DSA_HPG), NEG_INF, F32)
        line = jnp.concatenate([fill, band, fill], axis=0)
        tabs.append(jnp.stack([line[DSA_QBLK - 1 - q:DSA_QBLK - 1 - q + width] for q in range(DSA_QBLK)]))
    return jnp.transpose(jnp.stack(tabs), (3, 0, 1, 2))


def _head_block_diag():
    ii = lax.broadcasted_iota(jnp.int32, (LANES, LANES), 0)
    jj = lax.broadcasted_iota(jnp.int32, (LANES, LANES), 1)
    return (ii // RW_HEAD_DIM) == (jj // RW_HEAD_DIM)


def _rw_prep_body(r_ref, k_ref, v_ref, lo_ref, mur_ref, muk_ref, muv_ref, mulo_ref, w0_ref, w2_ref, a0_ref,
                  a2_ref, g2_ref, kk_ref, ka_ref, rk_ref,
                  ro_ref, vo_ref, kko_ref, bon_ref, gate_ref, lwf_ref, lwb_ref, kdf_ref, kdb_ref, bbf_ref, bbb_ref):
    s = r_ref.shape[0]
    row = lax.broadcasted_iota(jnp.int32, (s, 1), 0)

    def mix(t, mu):
        return t + mu * (0.5 * (_shift_down(t, row) + _shift_up(t, row)) - t)

    r = mix(r_ref[...].astype(F32), mur_ref[...])
    kr = mix(k_ref[...].astype(F32), muk_ref[...])
    v = mix(v_ref[...].astype(F32), muv_ref[...])
    lo = mix(lo_ref[...], mulo_ref[...])
    bd = jnp.where(_head_block_diag(), 1.0, 0.0)

    def head_sum(t):
        return _dot_sum(t, bd)

    kk = kr * kk_ref[...]
    kk = kk * lax.rsqrt(head_sum(kk * kk) + L2_EPS)
    gd = lo[:, 4 * RW_LORA:]
    gate_ref[...] = _dot1(_sigmoid(gd), g2_ref[...])
    ro_ref[...] = r
    vo_ref[...] = v
    kko_ref[...] = kk
    bonus = jnp.zeros_like(r)
    outs = ((lwf_ref, kdf_ref, bbf_ref), (lwb_ref, kdb_ref, bbb_ref))
    for d in range(2):
        wd = lo[:, d * RW_LORA:(d + 1) * RW_LORA]
        ad = lo[:, (2 + d) * RW_LORA:(3 + d) * RW_LORA]
        w_log = -_softplus(-(w0_ref[d:d + 1, :] + _dot1(jnp.tanh(wd), w2_ref[d]))) - 0.5
        a = _sigmoid(a0_ref[d:d + 1, :] + _dot1(ad, a2_ref[d]))
        kd = kr * (1.0 + (a - 1.0) * ka_ref[...])
        lw_ref, kd_ref, bb_ref = outs[d]
        lw_ref[...] = -jnp.exp(w_log)
        kd_ref[...] = kd
        bb_ref[...] = kk * a
        bonus = bonus + head_sum(r * kd * rk_ref[...]) * v
    bon_ref[...] = bonus


def rwkv_prep(p, lora_in, mu, w0, w2, a0, a2, g2, k_k, k_a, r_k, col0, batch):
    t = p.shape[0]
    s = t // batch
    nct = RW_WIDTH // LANES
    base = col0 // LANES
    nlo = RW_LORA_IN
    col = lambda off: pl.BlockSpec((s, LANES), lambda b, c: (b, base + off * nct + c))
    vec = lambda off: pl.BlockSpec((1, LANES), lambda b, c: (0, off * nct + c))
    mu_main = mu[:RW_MAIN].reshape(1, RW_MAIN)
    mu_lo = mu[RW_MAIN:].reshape(1, nlo)
    out_spec = pl.BlockSpec((s, LANES), lambda b, c: (b, c))
    n_out = 11
    return pl.pallas_call(
        _rw_prep_body,
        grid=(batch, nct),
        in_specs=[col(0), col(1), col(2),
                  pl.BlockSpec((s, nlo), lambda b, c: (b, 0)),
                  vec(0), vec(1), vec(2),
                  pl.BlockSpec((1, nlo), lambda b, c: (0, 0)),
                  pl.BlockSpec((2, LANES), lambda b, c: (0, c)),
                  pl.BlockSpec((2, RW_LORA, LANES), lambda b, c: (0, 0, c)),
                  pl.BlockSpec((2, LANES), lambda b, c: (0, c)),
                  pl.BlockSpec((2, RW_LORA, LANES), lambda b, c: (0, 0, c)),
                  pl.BlockSpec((RW_GATE_LORA, LANES), lambda b, c: (0, c)),
                  vec(0), vec(0), vec(0)],
        out_specs=[out_spec] * n_out,
        out_shape=[jax.ShapeDtypeStruct((t, RW_WIDTH), F32)] * n_out,
        compiler_params=_params("parallel", "parallel"),
        name="rwkv_prep",
    )(p, p, p, lora_in, mu_main, mu_main, mu_main, mu_lo, w0, w2, a0, a2, g2,
      k_k.reshape(1, RW_WIDTH), k_a.reshape(1, RW_WIDTH), r_k.reshape(1, RW_WIDTH))


def _rw_scan_body(r_ref, v_ref, kk_ref, bon_ref, gate_ref, lwf_ref, lwb_ref, kdf_ref, kdb_ref, bbf_ref, bbb_ref,
                  lnw_ref, lnb_ref, y_ref, cum_ref, cumt_ref, kdt_ref, bbt_ref, yf_ref, yb_ref,
                  tr_ref, tc_ref, rbk_ref, bkt_ref):
    s = r_ref.shape[0]
    c = CHUNK
    c2 = 2 * c
    row = lax.broadcasted_iota(jnp.int32, (s, 1), 0)
    pos = row % c
    lw_refs, kd_refs, bb_refs = (lwf_ref, lwb_ref), (kdf_ref, kdb_ref), (bbf_ref, bbb_ref)
    for d in range(2):
        cum = _seg_cumsum(lw_refs[d][...], pos, d == 1)
        cum_ref[d] = cum
        cumt_ref[d] = cum.T
        kdt_ref[d] = kd_refs[d][...].T
        bbt_ref[d] = bb_refs[d][...].T

    masks = _tri_masks(c2, c)
    merge = {rev: _merge_masks(c2, rev) for rev in (False, True)}
    eye2_f = jnp.where(masks["eye"], 1.0, 0.0)
    head0 = lax.broadcasted_iota(jnp.int32, (1, LANES), 1) < RW_HEAD_DIM
    head_bd = _head_block_diag()
    npair = s // PAIR
    group = min(RW_PREP_PAIRS, npair)

    def stack(x):
        return jnp.concatenate([jnp.where(head0, x, 0.0), jnp.where(head0, 0.0, x)], axis=0)

    def prep(pidx, d, half):
        reverse = d == 1
        incl2, strict2 = masks[reverse]
        cidx = 2 * pidx + half
        cols = pl.ds(pl.multiple_of(pidx * PAIR, PAIR), PAIR)
        rows = pl.ds(pl.multiple_of(cidx * c, c), c)
        hs = slice(half * c, (half + 1) * c)
        r, v, kk = r_ref[rows, :], v_ref[rows, :], kk_ref[rows, :]
        lw, kd, bb = lw_refs[d][rows, :], kd_refs[d][rows, :], bb_refs[d][rows, :]
        cum = cum_ref[d, rows, :]
        e_neg = jnp.exp(-cum)
        a_s = stack(-kk * jnp.exp(cum - lw))
        r_s = stack(r * jnp.exp(cum))
        ar = jnp.concatenate([a_s, r_s], axis=0)
        bk = jnp.concatenate([stack(bb * e_neg), stack(kd * e_neg)], axis=0)
        g = _dot1(ar, bk, _NT)
        yield
        rbk_ref[d, cidx] = jnp.concatenate([jnp.where(incl2, g[c2:, :c2], 0.0),
                                            jnp.where(incl2, g[c2:, c2:], 0.0)], axis=1).astype(BF16)
        av = _dot1(jnp.where(strict2, g[:c2, c2:], 0.0), stack(v))
        yield
        t_inv = []
        yield from _tri_inverse(-jnp.where(strict2, g[:c2, :c2], 0.0), eye2_f, merge[reverse], t_inv)
        tt = _dot1(t_inv[0], jnp.concatenate([a_s, av], axis=1))
        yield
        tr_ref[d, cidx] = jnp.concatenate([tt[:, :LANES], r_s], axis=0).astype(BF16)
        tc_ref[d, cidx] = tt[:, LANES:].astype(BF16)
        cum_t = cumt_ref[d, :, cols][:, hs]
        last = 0 if reverse else c - 1
        e_out_t = jnp.exp(cum_t[:, last:last + 1] - cum_t)
        bkt_ref[d, cidx] = jnp.concatenate([bbt_ref[d, :, cols][:, hs] * e_out_t,
                                            kdt_ref[d, :, cols][:, hs] * e_out_t], axis=1).astype(BF16)

    def seq(states, pidx, d):
        cum_t2 = cumt_ref[d, :, pl.ds(pl.multiple_of(pidx * PAIR, PAIR), PAIR)]
        for half in ((1, 0) if d else (0, 1)):
            cidx = 2 * pidx + half
            rows = pl.ds(pl.multiple_of(cidx * c, c), c)
            v = v_ref[rows, :]
            x = _dot1(tr_ref[d, cidx], states[d])
            yield
            ps = x[:c2] + tc_ref[d, cidx].astype(F32)
            os_ = x[c2:] + _dot1(rbk_ref[d, cidx], jnp.concatenate([ps, stack(v)], axis=0))
            y = os_[:c] + os_[c:]
            if d:
                yb_ref[rows, :] = y
            else:
                yf_ref[rows, :] = y
            pv = jnp.concatenate([ps[:c] + ps[c:], v], axis=0)
            last = half * c + (0 if d else c - 1)
            e_tot = jnp.exp(cum_t2[:, last:last + 1])
            states[d] = states[d] * e_tot + jnp.where(head_bd, _dot1(bkt_ref[d, cidx], pv), 0.0)
            yield

    ngroups = npair // group

    def prep_chains(g):
        return [prep(npair - 1 - (g * group + j) if d else g * group + j, d, half)
                for j in range(group) for d in range(2) for half in range(2)]

    def seq_chain(states, g, d):
        for j in range(group):
            yield from seq(states, npair - 1 - (g * group + j) if d else g * group + j, d)

    def step(g, carry):
        states = list(carry)
        _interleave(prep_chains(g + 1) + [seq_chain(states, g, 0), seq_chain(states, g, 1)])
        return tuple(states)

    zero = jnp.zeros((LANES, LANES), F32)
    _interleave(prep_chains(0))
    states = list(lax.fori_loop(0, ngroups - 1, step, (zero, zero)))
    _interleave([seq_chain(states, ngroups - 1, 0), seq_chain(states, ngroups - 1, 1)])

    bd = jnp.where(head_bd, 1.0, 0.0)
    y = yf_ref[...] + yb_ref[...]
    mean = _dot_sum(y, bd) * (1.0 / RW_HEAD_DIM)
    yc = y - mean
    var = _dot_sum(yc * yc, bd) * (1.0 / RW_HEAD_DIM)
    yn = yc * lax.rsqrt(var + RW_GN_EPS) * lnw_ref[...] + lnb_ref[...]
    y_ref[...] = ((yn + bon_ref[...]) * gate_ref[...]).astype(y_ref.dtype)


def rwkv_scan(prep, ln_w, ln_b, batch):
    t = prep[0].shape[0]
    s = t // batch
    assert s % PAIR == 0 and (s // PAIR) % min(RW_PREP_PAIRS, s // PAIR) == 0
    nct = RW_WIDTH // LANES
    nchunk = s // CHUNK
    blk = pl.BlockSpec((s, LANES), lambda b, c: (b, c))
    vec = pl.BlockSpec((1, LANES), lambda b, c: (0, c))
    tsp = pltpu.VMEM((2, LANES, s), F32)
    big = pltpu.VMEM((s, LANES), F32)
    per_chunk = lambda rows, cols: pltpu.VMEM((2, nchunk, rows, cols), BF16)
    return pl.pallas_call(
        _rw_scan_body,
        grid=(batch, nct),
        in_specs=[blk] * 11 + [vec, vec],
        out_specs=blk,
        out_shape=jax.ShapeDtypeStruct((t, RW_WIDTH), BF16),
        scratch_shapes=[pltpu.VMEM((2, s, LANES), F32), tsp, tsp, tsp, big, big,
                        per_chunk(2 * PAIR, LANES), per_chunk(PAIR, LANES), per_chunk(PAIR, 2 * LANES),
                        per_chunk(PAIR, LANES)],
        compiler_params=_params("parallel", "parallel"),
        name="rwkv_scan",
    )(*prep, ln_w.reshape(1, RW_WIDTH), ln_b.reshape(1, RW_WIDTH))


def even_mixer(x, g, w_in, w_sc, w_ab, w_out, layer, conv_qkv, a_log, dt_bias, out_gain, conv_sc, batch):
    p, ab = norm_mm(x, g, w_in, layer, ncols=4 * DN_WIDTH, w2=w_sc, wp_t=w_ab.T, out_dtype=BF16)
    zeros = jnp.zeros((2 * DN_HEADS,), F32)
    alog32 = jnp.concatenate([zeros, a_log.reshape(-1)]).reshape(1, -1)
    dt32 = jnp.concatenate([zeros, dt_bias.reshape(-1)]).reshape(1, -1)
    y_dn = deltanet(p, ab, conv_qkv.T, alog32, dt32, out_gain, batch)
    y_sc = short_conv(p, conv_sc.T, 4 * DN_WIDTH, batch)
    return mm_res(y_dn, w_out, layer, x, a2=y_sc, tn=D_MODEL)


def odd_mixer(x, g, w_in, w_lora, w_out, layer, qn, kn, bias_tab, mu, w0, w2, a0, a2, g2, k_k, k_a, r_k, ln_w, ln_b,
              batch):
    p, lora_in = norm_mm(x, g, w_in, layer, ncols=3 * DSA_QKV + RW_MAIN, wp=w_lora, out_dtype=BF16)
    y_c = dilated_attention(p, qn, kn, bias_tab, batch)
    prep = rwkv_prep(p, lora_in, mu, w0, w2, a0, a2, g2, k_k, k_a, r_k, 3 * DSA_QKV, batch)
    y_d = rwkv_scan(prep, ln_w, ln_b, batch)
    return mm_res(y_c, w_out, layer, x, a2=y_d, tn=D_MODEL)


def kernel(x, mem, rel_bias, norm_mix, norm_xattn, norm_mem, norm_ffn, xa_wq, xa_wk, xa_wv, xa_wo, xa_qn, xa_kn, ffn_w1, ffn_w2, ev_w_in, ev_w_out, dn_conv, dn_a_log, dn_dt_bias, dn_norm, sc_conv, od_w_in, od_w_out, ca_qn, ca_kn, rw_mu, rw_w0, rw_w2, rw_a0, rw_a2, rw_g2, rw_k_k, rw_k_a, rw_r_k, rw_ln_w, rw_ln_b):
    batch, seq, d = x.shape
    n_mem = mem.shape[1]
    xf = x.reshape(batch * seq, d)
    memf = mem.reshape(batch * n_mem, d)
    bias_tab = _dsa_bias_table(rel_bias)
    c_ab = 4 * DN_WIDTH
    c_sc = c_ab + 4 * DN_HEADS
    c_lo = 3 * DSA_QKV + RW_MAIN
    ev_in, od_in = ev_w_in.astype(BF16), od_w_in.astype(BF16)
    ev_sc = ev_in[:, :, c_sc:]
    ev_out, od_out = ev_w_out.astype(BF16), od_w_out.astype(BF16)
    w_q, w_o = xa_wq.astype(BF16), xa_wo.astype(BF16)
    w_kv = jnp.concatenate([xa_wk, xa_wv], axis=2).astype(BF16)
    for layer in range(DEPTH):
        i = layer // 2
        if layer % 2 == 0:
            xf = even_mixer(xf, norm_mix[layer], ev_in, ev_sc, ev_w_in[i, :, c_ab:c_sc], ev_out, i, dn_conv[i],
                            dn_a_log[i], dn_dt_bias[i], dn_norm[i], sc_conv[i], batch)
        else:
            xf = odd_mixer(xf, norm_mix[layer], od_in, od_w_in[i, :, c_lo:], od_out, i, ca_qn[i], ca_kn[i], bias_tab,
                           rw_mu[i], rw_w0[i], rw_w2[i], rw_a0[i], rw_a2[i], rw_g2[i], rw_k_k[i], rw_k_a[i], rw_r_k[i],
                           rw_ln_w[i], rw_ln_b[i], batch)
        kv = norm_mm(memf, norm_mem[layer], w_kv, layer).reshape(batch, n_mem, 2 * XA_WIDTH)
        xf, hn = xattn(xf, norm_xattn[layer], w_q, kv, w_o, layer, xa_qn[layer], xa_kn[layer], norm_ffn[layer], batch)
        h1 = act_mm(hn, ffn_w1, layer, act="relu2", out_dtype=BF16)
        xf = mm_res(h1, ffn_w2, layer, xf)
    return xf.reshape(batch, seq, d)
```

```python
import functools
import math

import jax
import jax.numpy as jnp
import numpy as np
from jax import lax
from jax.experimental import pallas as pl
from jax.experimental.pallas import tpu as pltpu

F32 = jnp.float32
BF16 = jnp.bfloat16

D_MODEL = 2048
DEPTH = 4
RMS_EPS = 1e-6
L2_EPS = 1e-6

DN_HEADS = 8
DN_HEAD_DIM = 128
DN_WIDTH = DN_HEADS * DN_HEAD_DIM
SC_WIDTH = D_MODEL - DN_WIDTH
CHUNK = 64
PAIR = 2 * CHUNK
DN_PREP_PAIRS = 4
DN_HEADS_PER_STEP = 2
RW_PREP_PAIRS = 4

DSA_PATTERNS = ((128, 1), (512, 4), (2048, 16))
DSA_GROUPS = len(DSA_PATTERNS)
DSA_HPG = 4
DSA_HEAD_DIM = 128
DSA_HEADS = DSA_GROUPS * DSA_HPG
DSA_QKV = DSA_HEADS * DSA_HEAD_DIM
DSA_SIDE = 64
DSA_QBLK = 128
DSA_PAD = DSA_SIDE * max(d for _, d in DSA_PATTERNS)
DSA_INTERLEAVE = 8
REL_BUCKETS = 32
REL_MAX_DIST = 1024
NEG_INF = -1e30

RW_HEADS = 8
RW_HEAD_DIM = 64
RW_WIDTH = RW_HEADS * RW_HEAD_DIM
RW_LORA = 64
RW_GATE_LORA = 128
RW_MAIN = 3 * RW_WIDTH
RW_LORA_IN = 4 * RW_LORA + RW_GATE_LORA
RW_GN_EPS = 64e-5

XA_HEADS = 4
XA_HEAD_DIM = 128
XA_WIDTH = XA_HEADS * XA_HEAD_DIM

LANES = 128
SUBLANES = 8
NORM_ROWS = 256
VMEM_LIMIT_BYTES = 58 * 1024 * 1024


def _params(*sem):
    return pltpu.CompilerParams(dimension_semantics=sem, vmem_limit_bytes=VMEM_LIMIT_BYTES)


_NN = (((1,), (0,)), ((), ()))
_NT = (((1,), (1,)), ((), ()))


def _dot1(a, b, dims=_NN):
    return lax.dot_general(a.astype(BF16), b.astype(BF16), dims, preferred_element_type=F32)


def _dot_sum(a, b01):
    b = b01.astype(BF16)
    hi = a.astype(BF16)
    lo = (a - hi.astype(F32)).astype(BF16)
    dg = functools.partial(lax.dot_general, dimension_numbers=_NN, preferred_element_type=F32)
    return dg(hi, b) + dg(lo, b)


def _interleave(chains):
    chains = list(chains)
    while chains:
        alive = []
        for ch in chains:
            try:
                next(ch)
                alive.append(ch)
            except StopIteration:
                pass
        chains = alive


def _merge_masks(n, reverse):
    ii = lax.broadcasted_iota(jnp.int32, (n, n), 0)
    jj = lax.broadcasted_iota(jnp.int32, (n, n), 1)
    tri = (ii < jj) if reverse else (ii > jj)
    out = []
    size = 1
    while size < CHUNK:
        out.append(tri & ((ii // (2 * size)) == (jj // (2 * size))) & ((ii // size) != (jj // size)))
        size *= 2
    return out


def _tri_inverse(a, eye_f, level_masks, out):
    t = eye_f - jnp.where(level_masks[0], a, 0.0)
    for m in level_masks[1:]:
        x = _dot1(jnp.where(m, a, 0.0), t)
        yield
        t = t - _dot1(t, x)
        yield
    out.append(t)


def _sigmoid(x):
    return 1.0 / (1.0 + jnp.exp(-x))


def _softplus(x):
    return jnp.maximum(x, 0.0) + jnp.log1p(jnp.exp(-jnp.abs(x)))


def _shift_down(x, row):
    return jnp.where(row == 0, 0.0, pltpu.roll(x, 1, 0))


def _shift_up(x, row):
    n = x.shape[0]
    return jnp.where(row == n - 1, 0.0, pltpu.roll(x, n - 1, 0))


def _norm_mm_body(x_ref, g_ref, w_ref, *rest, act, extra, extra_t, n_first):
    if n_first is not None:
        w2_ref, rest = rest[0], rest[1:]
    if extra:
        wp_ref, o_ref, op_ref, xn_ref = rest
    else:
        o_ref, xn_ref = rest

    j = pl.program_id(1)

    def project(wt_ref, rows=slice(None)):
        acc = jnp.dot(xn_ref[rows, :], wt_ref[...].astype(BF16), preferred_element_type=F32)
        if act == "relu2":
            acc = jnp.square(jnp.maximum(acc, 0.0))
        o_ref[rows, :] = acc.astype(o_ref.dtype)

    @pl.when(j == 0)
    def _():
        for r in range(x_ref.shape[0] // NORM_ROWS):
            rows = slice(r * NORM_ROWS, (r + 1) * NORM_ROWS)
            x = x_ref[rows, :]
            xn = x * lax.rsqrt(jnp.mean(x * x, axis=-1, keepdims=True) + RMS_EPS) * g_ref[...]
            xn_ref[rows, :] = xn.astype(BF16)
            if extra and extra_t:
                op_ref[:, rows] = _dot1(wp_ref[...], xn, _NT)
            elif extra:
                op_ref[rows, :] = _dot1(xn, wp_ref[...])
            project(w_ref, rows)

    if n_first is None:
        pl.when(j > 0)(lambda: project(w_ref))
    else:
        pl.when((j > 0) & (j < n_first))(lambda: project(w_ref))
        pl.when(j >= n_first)(lambda: project(w2_ref))


def norm_mm(x, g, w, layer, ncols=None, w2=None, wp=None, wp_t=None, act=None, out_dtype=F32, tm=1024, tn=1024):
    t, d = x.shape
    n1 = w.shape[2] if ncols is None else ncols
    n = n1 + (0 if w2 is None else w2.shape[2])
    tm = min(tm, t)
    tn = min(tn, n1)
    assert t % tm == 0 and n1 % tn == 0 and n % tn == 0 and tm % NORM_ROWS == 0
    extra = wp is not None or wp_t is not None
    n_first = None if w2 is None else n1 // tn
    in_specs = [
        pl.BlockSpec((tm, d), lambda i, j: (i, 0)),
        pl.BlockSpec((1, d), lambda i, j: (0, 0)),
    ]
    args = [x, g.reshape(1, d), w]
    if w2 is None:
        in_specs.append(pl.BlockSpec((None, d, tn), lambda i, j: (layer, 0, j)))
    else:
        in_specs.append(pl.BlockSpec((None, d, tn), lambda i, j: (layer, 0, jnp.minimum(j, n_first - 1))))
        in_specs.append(pl.BlockSpec((None, d, tn), lambda i, j: (layer, 0, jnp.maximum(j - n_first, 0))))
        args.append(w2)
    out_specs = pl.BlockSpec((tm, tn), lambda i, j: (i, j))
    out_shape = jax.ShapeDtypeStruct((t, n), out_dtype)
    if wp is not None:
        npc = wp.shape[1]
        in_specs.append(pl.BlockSpec((d, npc), lambda i, j: (0, 0)))
        out_specs = [out_specs, pl.BlockSpec((tm, npc), lambda i, j: (i, 0))]
        out_shape = [out_shape, jax.ShapeDtypeStruct((t, npc), F32)]
        args.append(wp)
    elif wp_t is not None:
        npc = wp_t.shape[0]
        in_specs.append(pl.BlockSpec((npc, d), lambda i, j: (0, 0)))
        out_specs = [out_specs, pl.BlockSpec((npc, tm), lambda i, j: (0, i))]
        out_shape = [out_shape, jax.ShapeDtypeStruct((npc, t), F32)]
        args.append(wp_t)
    return pl.pallas_call(
        functools.partial(_norm_mm_body, act=act, extra=extra, extra_t=wp_t is not None, n_first=n_first),
        grid=(t // tm, n // tn),
        in_specs=in_specs,
        out_specs=out_specs,
        out_shape=out_shape,
        scratch_shapes=[pltpu.VMEM((tm, d), BF16)],
        compiler_params=_params("parallel", "arbitrary"),
        name="norm_mm",
    )(*args)


def _act_mm_body(h_ref, w_ref, o_ref, *, act):
    acc = jnp.dot(h_ref[...], w_ref[...].astype(BF16), preferred_element_type=F32)
    if act == "relu2":
        acc = jnp.square(jnp.maximum(acc, 0.0))
    o_ref[...] = acc.astype(o_ref.dtype)


def act_mm(h, w, layer, act=None, out_dtype=F32, tm=2048, tn=1024):
    t, d = h.shape
    n = w.shape[2]
    tm, tn = min(tm, t), min(tn, n)
    assert t % tm == 0 and n % tn == 0
    return pl.pallas_call(
        functools.partial(_act_mm_body, act=act),
        grid=(t // tm, n // tn),
        in_specs=[pl.BlockSpec((tm, d), lambda i, j: (i, 0)),
                  pl.BlockSpec((None, d, tn), lambda i, j: (layer, 0, j))],
        out_specs=pl.BlockSpec((tm, tn), lambda i, j: (i, j)),
        out_shape=jax.ShapeDtypeStruct((t, n), out_dtype),
        compiler_params=_params("parallel", "arbitrary"),
        name="act_mm",
    )(h, w)


def _mm_res_body(a_ref, *rest, n_first):
    if n_first is not None:
        a2_ref, rest = rest[0], rest[1:]
    w_ref, r_ref, o_ref = rest
    kk = pl.program_id(2)

    def accumulate(lhs_ref):
        def product():
            return jnp.dot(lhs_ref[...], w_ref[...].astype(BF16), preferred_element_type=F32)

        @pl.when(kk == 0)
        def _():
            o_ref[...] = r_ref[...] + product()

        @pl.when(kk > 0)
        def _():
            o_ref[...] += product()

    if n_first is None:
        accumulate(a_ref)
    else:
        pl.when(kk < n_first)(lambda: accumulate(a_ref))
        pl.when(kk >= n_first)(lambda: accumulate(a2_ref))


def mm_res(a, w, layer, res, a2=None, tm=1024, tn=1024, tk=2048):
    t, k1 = a.shape
    k = k1 + (0 if a2 is None else a2.shape[1])
    n = w.shape[2]
    tm, tn, tk = min(tm, t), min(tn, n), min(tk, k1)
    assert t % tm == 0 and n % tn == 0 and k1 % tk == 0 and k % tk == 0
    n_first = None if a2 is None else k1 // tk
    if a2 is None:
        in_specs = [pl.BlockSpec((tm, tk), lambda i, j, kk: (i, kk))]
        args = [a]
    else:
        in_specs = [pl.BlockSpec((tm, tk), lambda i, j, kk: (i, jnp.minimum(kk, n_first - 1))),
                    pl.BlockSpec((tm, tk), lambda i, j, kk: (i, jnp.maximum(kk - n_first, 0)))]
        args = [a, a2]
    in_specs += [
        pl.BlockSpec((None, tk, tn), lambda i, j, kk: (layer, kk, j)),
        pl.BlockSpec((tm, tn), lambda i, j, kk: (i, j)),
    ]
    return pl.pallas_call(
        functools.partial(_mm_res_body, n_first=n_first),
        grid=(t // tm, n // tn, k // tk),
        in_specs=in_specs,
        out_specs=pl.BlockSpec((tm, tn), lambda i, j, kk: (i, j)),
        out_shape=jax.ShapeDtypeStruct((t, n), F32),
        compiler_params=_params("parallel", "parallel", "arbitrary"),
        name="mm_res",
    )(*args, w, res)


def _xattn_body(x_ref, g_ref, wq_ref, kv_ref, wo_ref, qn_ref, kn_ref, g2_ref, o_ref, hn_ref):
    x = x_ref[...]
    xn = x * lax.rsqrt(jnp.mean(x * x, axis=-1, keepdims=True) + RMS_EPS) * g_ref[...]
    q = jnp.dot(xn.astype(BF16), wq_ref[...], preferred_element_type=F32)
    kv = kv_ref[0]
    outs = []
    for h in range(XA_HEADS):
        sl = slice(h * XA_HEAD_DIM, (h + 1) * XA_HEAD_DIM)
        qh = q[:, sl]
        qh = qh * lax.rsqrt(jnp.mean(qh * qh, axis=-1, keepdims=True) + RMS_EPS) * qn_ref[...]
        kh = kv[:, sl]
        kh = kh * lax.rsqrt(jnp.mean(kh * kh, axis=-1, keepdims=True) + RMS_EPS) * kn_ref[...]
        vh = kv[:, XA_WIDTH + h * XA_HEAD_DIM:XA_WIDTH + (h + 1) * XA_HEAD_DIM]
        logits = _dot1(qh, kh, _NT) * (XA_HEAD_DIM ** -0.5)
        m = jnp.max(logits, axis=-1, keepdims=True)
        p = jnp.exp(logits - m)
        s = jnp.sum(p, axis=-1, keepdims=True)
        outs.append(_dot1(p, vh) / s)
    o = jnp.concatenate(outs, axis=-1).astype(BF16)
    xo = x + jnp.dot(o, wo_ref[...], preferred_element_type=F32)
    o_ref[...] = xo
    hn_ref[...] = (xo * lax.rsqrt(jnp.mean(xo * xo, axis=-1, keepdims=True) + RMS_EPS) * g2_ref[...]).astype(BF16)


def xattn(x, g, wq, kv, wo, layer, qn, kn, g_next, batch, ts=1024):
    t, d = x.shape
    s = t // batch
    ts = min(ts, s)
    nst = s // ts
    m = kv.shape[1]
    return pl.pallas_call(
        _xattn_body,
        grid=(batch, nst),
        in_specs=[
            pl.BlockSpec((ts, d), lambda b, i: (b * nst + i, 0)),
            pl.BlockSpec((1, d), lambda b, i: (0, 0)),
            pl.BlockSpec((None, d, XA_WIDTH), lambda b, i: (layer, 0, 0)),
            pl.BlockSpec((1, m, 2 * XA_WIDTH), lambda b, i: (b, 0, 0)),
            pl.BlockSpec((None, XA_WIDTH, d), lambda b, i: (layer, 0, 0)),
            pl.BlockSpec((1, XA_HEAD_DIM), lambda b, i: (0, 0)),
            pl.BlockSpec((1, XA_HEAD_DIM), lambda b, i: (0, 0)),
            pl.BlockSpec((1, d), lambda b, i: (0, 0)),
        ],
        out_specs=[pl.BlockSpec((ts, d), lambda b, i: (b * nst + i, 0))] * 2,
        out_shape=[jax.ShapeDtypeStruct((t, d), F32), jax.ShapeDtypeStruct((t, d), BF16)],
        compiler_params=_params("parallel", "parallel"),
        name="xattn",
    )(x, g.reshape(1, d), wq, kv, wo, qn.reshape(1, -1), kn.reshape(1, -1), g_next.reshape(1, d))


def _tri_masks(n, blk):
    ii = lax.broadcasted_iota(jnp.int32, (n, n), 0)
    jj = lax.broadcasted_iota(jnp.int32, (n, n), 1)
    same = (ii // blk) == (jj // blk) if n != blk else None

    def m(c):
        return c if same is None else (c & same)

    return {
        False: (m(ii >= jj), m(ii > jj)),
        True: (m(ii <= jj), m(ii < jj)),
        "eye": ii == jj,
    }


def _seg_cumsum(x, pos, reverse):
    n = x.shape[0]
    sh = 1
    while sh < CHUNK:
        if reverse:
            x = x + jnp.where(pos < CHUNK - sh, pltpu.roll(x, n - sh, 0), 0.0)
        else:
            x = x + jnp.where(pos >= sh, pltpu.roll(x, sh, 0), 0.0)
        sh *= 2
    return x


def _dn_body(q_ref, k_ref, v_ref, gate_ref, ab_ref, cq_ref, ck_ref, cv_ref, alog_ref, dt_ref, gain_ref,
             y_ref, qs_ref, ks_ref, vs_ref, kt_ref, bg_ref, gt_ref, of_ref, ob_ref,
             u_ref, w_ref, qg_ref, qk_ref, kdt_ref, eg_ref, st_ref):
    s = q_ref.shape[0]
    hd = DN_HEAD_DIM
    nh = DN_HEADS_PER_STEP
    row = lax.broadcasted_iota(jnp.int32, (s, 1), 0)
    c = CHUNK
    lane32 = lax.broadcasted_iota(jnp.int32, (1, 4 * DN_HEADS), 1)
    lane128 = lax.broadcasted_iota(jnp.int32, (1, LANES), 1)
    pos_t = lax.broadcasted_iota(jnp.int32, (1, s), 1) % c

    def conv_silu(x_ref, cw_ref, lanes):
        x = x_ref[:, lanes].astype(F32)
        w = cw_ref[:, lanes]
        y = _shift_down(x, row) * w[0:1] + x * w[1:2] + _shift_up(x, row) * w[2:3]
        return y * _sigmoid(y)

    for hh in range(nh):
        h = pl.program_id(1) * nh + hh
        lanes = slice(hh * hd, (hh + 1) * hd)
        q = conv_silu(q_ref, cq_ref, lanes)
        q = q * lax.rsqrt(jnp.sum(q * q, axis=-1, keepdims=True) + L2_EPS) * (hd ** -0.5)
        qs_ref[hh] = q
        k = conv_silu(k_ref, ck_ref, lanes)
        k = k * lax.rsqrt(jnp.sum(k * k, axis=-1, keepdims=True) + L2_EPS)
        ks_ref[hh] = k
        kt_ref[hh] = k.T
        vs_ref[hh] = conv_silu(v_ref, cv_ref, lanes)

        def decay_row(d, h=h):
            idx = (2 + d) * DN_HEADS + h
            a_log = jnp.sum(jnp.where(lane32 == idx, alog_ref[...], 0.0), axis=-1, keepdims=True)
            dt = jnp.sum(jnp.where(lane32 == idx, dt_ref[...], 0.0), axis=-1, keepdims=True)
            g = -jnp.exp(a_log) * _softplus(ab_ref[pl.ds(idx, 1), :] + dt)
            sh = 1
            while sh < c:
                if d:
                    g = g + jnp.where(pos_t < c - sh, pltpu.roll(g, s - sh, 1), 0.0)
                else:
                    g = g + jnp.where(pos_t >= sh, pltpu.roll(g, sh, 1), 0.0)
                sh *= 2
            return g

        rows = [_sigmoid(ab_ref[pl.ds(h, 1), :]), _sigmoid(ab_ref[pl.ds(DN_HEADS + h, 1), :]),
                decay_row(0), decay_row(1)]
        gt = jnp.concatenate(rows + [jnp.zeros((LANES - len(rows), s), F32)], axis=0)
        gt_ref[hh] = gt[0:SUBLANES, :]
        bg_ref[hh] = gt.T
        for d in range(2):
            st_ref[hh, d] = jnp.zeros((hd, hd), F32)

    masks = _tri_masks(PAIR, c)
    merge = {rev: _merge_masks(PAIR, rev) for rev in (False, True)}
    eye_f = jnp.where(masks["eye"], 1.0, 0.0)
    npair = s // PAIR
    group = min(DN_PREP_PAIRS, npair)

    def prep(hh, pidx, d):
        reverse = d == 1
        rows = pl.ds(pl.multiple_of(pidx * PAIR, PAIR), PAIR)
        qp, kp, vp = qs_ref[hh, rows, :], ks_ref[hh, rows, :], vs_ref[hh, rows, :]
        bgp = bg_ref[hh, rows, :]
        beta = bgp[:, d:d + 1]
        gcol = bgp[:, 2 + d:3 + d]
        grow = gt_ref[hh, 2 + d:3 + d, rows]
        incl, strict = masks[reverse]
        decay = jnp.where(incl, jnp.exp(jnp.where(incl, gcol - grow, 0.0)), 0.0)
        kb = kp * beta
        kq = _dot1(jnp.concatenate([kb, qp], axis=0), kp, _NT)
        yield
        qk_ref[hh, d, rows, :] = jnp.where(incl, kq[PAIR:] * decay, 0.0).astype(BF16)
        t_inv = []
        yield from _tri_inverse(jnp.where(strict, kq[:PAIR] * decay, 0.0), eye_f, merge[reverse], t_inv)
        egc = jnp.exp(gcol)
        uw = _dot1(t_inv[0], jnp.concatenate([vp * beta, kb * egc], axis=1))
        yield
        u_ref[hh, d, rows, :] = uw[:, :hd]
        w_ref[hh, d, rows, :] = uw[:, hd:].astype(BF16)
        qg_ref[hh, d, rows, :] = (qp * egc).astype(BF16)
        last0, last1 = (0, c) if reverse else (c - 1, PAIR - 1)
        glast = jnp.where(lane128 < c, grow[:, last0:last0 + 1], grow[:, last1:last1 + 1])
        kdt_ref[hh, d, :, rows] = (kt_ref[hh, :, rows] * jnp.exp(glast - grow)).astype(BF16)
        eg_ref[hh, d, :, rows] = jnp.broadcast_to(jnp.exp(glast), (SUBLANES, PAIR))

    def seq(hh, pidx, d):
        pair_rows = pl.ds(pl.multiple_of(pidx * PAIR, PAIR), PAIR)
        kdt = kdt_ref[hh, d, :, pair_rows]
        eg = eg_ref[hh, d, 0:1, pair_rows]
        for half in ((1, 0) if d else (0, 1)):
            r0 = half * c
            rows = pl.ds(pl.multiple_of(pidx * PAIR + r0, c), c)
            state = st_ref[hh, d]
            ws = _dot1(jnp.concatenate([w_ref[hh, d, rows, :], qg_ref[hh, d, rows, :]], axis=0), state)
            yield
            v_new = u_ref[hh, d, rows, :] - ws[:c]
            o = ws[c:] + _dot1(qk_ref[hh, d, rows, r0:r0 + c], v_new)
            if d:
                ob_ref[hh, rows, :] = o
            else:
                of_ref[hh, rows, :] = o
            st_ref[hh, d] = state * eg[:, r0:r0 + 1] + _dot1(kdt[:, r0:r0 + c], v_new)
            yield

    ngroups = npair // group

    def pair_of(g, j, d):
        return npair - 1 - (g * group + j) if d else g * group + j

    def prep_chains(g):
        return [prep(hh, pair_of(g, j, d), d) for j in range(group) for hh in range(nh) for d in range(2)]

    def seq_chain(g, hh, d):
        for j in range(group):
            yield from seq(hh, pair_of(g, j, d), d)

    def seq_chains(g):
        return [seq_chain(g, hh, d) for hh in range(nh) for d in range(2)]

    def step(g, _):
        _interleave(prep_chains(g + 1) + seq_chains(g))
        return 0

    _interleave(prep_chains(0))
    lax.fori_loop(0, ngroups - 1, step, 0)
    _interleave(seq_chains(ngroups - 1))

    for hh in range(nh):
        lanes = slice(hh * hd, (hh + 1) * hd)
        o = of_ref[hh] + ob_ref[hh]
        o = o * lax.rsqrt(jnp.mean(o * o, axis=-1, keepdims=True) + RMS_EPS) * gain_ref[...]
        gate = gate_ref[:, lanes].astype(F32)
        y_ref[:, lanes] = (o * (gate * _sigmoid(gate))).astype(y_ref.dtype)


def deltanet(p, ab, conv_t, alog32, dt32, gain, batch):
    t = p.shape[0]
    s = t // batch
    assert s % PAIR == 0 and (s // PAIR) % min(DN_PREP_PAIRS, s // PAIR) == 0
    hd = DN_HEAD_DIM
    nh = DN_HEADS_PER_STEP
    nsteps = DN_HEADS // nh
    col = lambda off: pl.BlockSpec((s, nh * hd), lambda b, h: (b, off * nsteps + h))
    cw = lambda off: pl.BlockSpec((3, nh * hd), lambda b, h: (0, off * nsteps + h))
    gate_col = pl.BlockSpec((s, nh * hd), lambda b, h: (b, 3 * nsteps + h), pipeline_mode=pl.Buffered(1))
    small = lambda n: pl.BlockSpec((1, n), lambda b, h: (0, 0))
    tok = pltpu.VMEM((nh, s, hd), F32)
    per_dir = pltpu.VMEM((nh, 2, s, hd), BF16)
    return pl.pallas_call(
        _dn_body,
        grid=(batch, nsteps),
        in_specs=[col(0), col(1), col(2), gate_col,
                  pl.BlockSpec((4 * DN_HEADS, s), lambda b, h: (0, b)),
                  cw(0), cw(1), cw(2),
                  small(4 * DN_HEADS), small(4 * DN_HEADS), small(hd)],
        out_specs=pl.BlockSpec((s, nh * hd), lambda b, h: (b, h)),
        out_shape=jax.ShapeDtypeStruct((t, DN_WIDTH), BF16),
        scratch_shapes=[tok, tok, tok, pltpu.VMEM((nh, hd, s), F32), tok, pltpu.VMEM((nh, SUBLANES, s), F32), tok, tok,
                        pltpu.VMEM((nh, 2, s, hd), F32), per_dir, per_dir, per_dir,
                        pltpu.VMEM((nh, 2, hd, s), BF16), pltpu.VMEM((nh, 2, SUBLANES, s), F32),
                        pltpu.VMEM((nh, 2, hd, hd), F32)],
        compiler_params=_params("parallel", "parallel"),
        name="deltanet",
    )(p, p, p, p, ab, conv_t, conv_t, conv_t, alog32, dt32, gain.reshape(1, hd))


def _sconv_body(b_ref, c_ref, u_ref, w_ref, y_ref):
    s = b_ref.shape[0]
    row = lax.broadcasted_iota(jnp.int32, (s, 1), 0)
    cu = c_ref[...].astype(F32) * u_ref[...].astype(F32)
    w = w_ref[...]
    y = _shift_down(cu, row) * w[0:1] + cu * w[1:2] + _shift_up(cu, row) * w[2:3]
    y_ref[...] = (b_ref[...].astype(F32) * y).astype(y_ref.dtype)


def short_conv(p, conv_t, col0, batch, tc=512):
    t = p.shape[0]
    s = t // batch
    nct = SC_WIDTH // tc
    base = col0 // tc
    col = lambda off: pl.BlockSpec((s, tc), lambda b, c: (b, base + off * nct + c))
    return pl.pallas_call(
        _sconv_body,
        grid=(batch, nct),
        in_specs=[col(0), col(1), col(2), pl.BlockSpec((3, tc), lambda b, c: (0, c))],
        out_specs=pl.BlockSpec((s, tc), lambda b, c: (b, c)),
        out_shape=jax.ShapeDtypeStruct((t, SC_WIDTH), BF16),
        compiler_params=_params("parallel", "parallel"),
        name="short_conv",
    )(p, p, p, conv_t)


def _dsa_body(*refs, seq):
    q_refs = refs[0:3]
    k_refs = refs[3:6]
    v_refs = refs[6:9]
    qn_ref, kn_ref, bias_ref, y_ref, qs_ref, kpad_ref, vpad_ref, og_ref, lse_ref = refs[9:]
    s = seq
    qb = DSA_QBLK
    side = DSA_SIDE
    width = qb + 2 * side
    kj = lax.broadcasted_iota(jnp.int32, (1, width), 1)
    zpad = jnp.zeros((DSA_PAD, DSA_HEAD_DIM), F32)
    for ref in (kpad_ref, vpad_ref):
        ref[0:DSA_PAD, :] = zpad
        ref[DSA_PAD + s:2 * DSA_PAD + s, :] = zpad

    for gi, (_, dil) in enumerate(DSA_PATTERNS):
        sub = s // dil
        nblk = sub // qb
        q = q_refs[gi][...].astype(F32)
        qs_ref[...] = (q * lax.rsqrt(jnp.mean(q * q, axis=-1, keepdims=True) + RMS_EPS) * qn_ref[...]
                       * (DSA_HEAD_DIM ** -0.5))
        k = k_refs[gi][...].astype(F32)
        kpad_ref[DSA_PAD:DSA_PAD + s, :] = k * lax.rsqrt(jnp.mean(k * k, axis=-1, keepdims=True) + RMS_EPS) * kn_ref[...]
        vpad_ref[DSA_PAD:DSA_PAD + s, :] = v_refs[gi][...].astype(F32)
        bias = bias_ref[0, gi]

        def block(t, gi=gi, dil=dil, sub=sub, nblk=nblk, bias=bias):
            r = t // nblk
            n = t % nblk
            rows = pl.ds(r + n * (qb * dil), qb, stride=dil)
            win = pl.ds(DSA_PAD + r + (n * qb - side) * dil, width, stride=dil)
            logits = _dot1(qs_ref[rows, :], kpad_ref[win, :], _NT) + bias
            yield
            pos = n * qb - side + kj
            logits = jnp.where((pos >= 0) & (pos < sub), logits, NEG_INF)
            m = jnp.max(logits, axis=-1, keepdims=True)
            yield
            p = jnp.exp(logits - m)
            ssum = jnp.sum(p, axis=-1, keepdims=True)
            o = _dot1(p, vpad_ref[win, :])
            yield
            og_ref[gi, rows, :] = o / ssum
            lse_ref[gi, rows, :] = jnp.broadcast_to(m + jnp.log(ssum), (qb, DSA_HEAD_DIM))

        nblocks = dil * nblk

        def blocks_step(i, _, block=block):
            _interleave(block(i * DSA_INTERLEAVE + u) for u in range(DSA_INTERLEAVE))
            return 0

        lax.fori_loop(0, nblocks // DSA_INTERLEAVE, blocks_step, 0)

    lse = [lse_ref[gi] for gi in range(DSA_GROUPS)]
    mx = jnp.maximum(jnp.maximum(lse[0], lse[1]), lse[2])
    ws = [jnp.exp(l - mx) for l in lse]
    num = ws[0] * og_ref[0] + ws[1] * og_ref[1] + ws[2] * og_ref[2]
    y_ref[...] = (num / (ws[0] + ws[1] + ws[2])).astype(y_ref.dtype)


def dilated_attention(p, qn, kn, bias_tab, batch):
    t = p.shape[0]
    s = t // batch
    assert (s // DSA_QBLK) % DSA_INTERLEAVE == 0
    hd = DSA_HEAD_DIM
    nh = DSA_HEADS

    def col(part, gi):
        return pl.BlockSpec((s, hd), lambda b, j: (b, part * nh + gi * DSA_HPG + j))

    in_specs = [col(part, gi) for part in range(3) for gi in range(DSA_GROUPS)]
    in_specs += [pl.BlockSpec((1, hd), lambda b, j: (0, 0)), pl.BlockSpec((1, hd), lambda b, j: (0, 0)),
                 pl.BlockSpec((1, DSA_GROUPS, DSA_QBLK, DSA_QBLK + 2 * DSA_SIDE), lambda b, j: (j, 0, 0, 0))]
    big = pltpu.VMEM((s, hd), F32)
    pad = pltpu.VMEM((s + 2 * DSA_PAD, hd), F32)
    grp = pltpu.VMEM((DSA_GROUPS, s, hd), F32)
    return pl.pallas_call(
        functools.partial(_dsa_body, seq=s),
        grid=(batch, DSA_HPG),
        in_specs=in_specs,
        out_specs=pl.BlockSpec((s, hd), lambda b, j: (b, j)),
        out_shape=jax.ShapeDtypeStruct((t, DSA_HPG * hd), BF16),
        scratch_shapes=[big, pad, pad, grp, grp],
        compiler_params=_params("parallel", "parallel"),
        name="dilated_attention",
    )(*([p] * 9), qn.reshape(1, hd), kn.reshape(1, hd), bias_tab)


def _t5_bucket(rel):
    half = REL_BUCKETS // 2
    max_exact = half // 2
    n = np.abs(rel)
    scaled = (np.log(np.maximum(n, max_exact).astype(np.float32) / np.float32(max_exact))
              / np.float32(math.log(REL_MAX_DIST / max_exact)))
    large = np.minimum(max_exact + (scaled * np.float32(half - max_exact)).astype(np.int32), half - 1)
    return np.where(rel > 0, half, 0) + np.where(n < max_exact, n, large)


def _dsa_bias_table(rel_bias):
    width = DSA_QBLK + 2 * DSA_SIDE
    tabs = []
    for gi, (_, dil) in enumerate(DSA_PATTERNS):
        offs = np.arange(-DSA_SIDE, DSA_SIDE + 1, dtype=np.int32) * dil
        band = rel_bias[_t5_bucket(offs)][:, gi * DSA_HPG:(gi + 1) * DSA_HPG].astype(F32)
        fill = jnp.full((DSA_QBLK - 1, DSA_HPG), NEG_INF, F32)
        line = jnp.concatenate([fill, band, fill], axis=0)
        tabs.append(jnp.stack([line[DSA_QBLK - 1 - q:DSA_QBLK - 1 - q + width] for q in range(DSA_QBLK)]))
    return jnp.transpose(jnp.stack(tabs), (3, 0, 1, 2))


def _head_block_diag():
    ii = lax.broadcasted_iota(jnp.int32, (LANES, LANES), 0)
    jj = lax.broadcasted_iota(jnp.int32, (LANES, LANES), 1)
    return (ii // RW_HEAD_DIM) == (jj // RW_HEAD_DIM)


def _rw_prep_body(r_ref, k_ref, v_ref, lo_ref, mur_ref, muk_ref, muv_ref, mulo_ref, w0_ref, w2_ref, a0_ref,
                  a2_ref, g2_ref, kk_ref, ka_ref, rk_ref,
                  ro_ref, vo_ref, kko_ref, bon_ref, gate_ref, lwf_ref, lwb_ref, kdf_ref, kdb_ref, bbf_ref, bbb_ref):
    s = r_ref.shape[0]
    row = lax.broadcasted_iota(jnp.int32, (s, 1), 0)

    def mix(t, mu):
        return t + mu * (0.5 * (_shift_down(t, row) + _shift_up(t, row)) - t)

    r = mix(r_ref[...].astype(F32), mur_ref[...])
    kr = mix(k_ref[...].astype(F32), muk_ref[...])
    v = mix(v_ref[...].astype(F32), muv_ref[...])
    lo = mix(lo_ref[...], mulo_ref[...])
    bd = jnp.where(_head_block_diag(), 1.0, 0.0)

    def head_sum(t):
        return _dot_sum(t, bd)

    kk = kr * kk_ref[...]
    kk = kk * lax.rsqrt(head_sum(kk * kk) + L2_EPS)
    gd = lo[:, 4 * RW_LORA:]
    gate_ref[...] = _dot1(_sigmoid(gd), g2_ref[...])
    ro_ref[...] = r
    vo_ref[...] = v
    kko_ref[...] = kk
    bonus = jnp.zeros_like(r)
    outs = ((lwf_ref, kdf_ref, bbf_ref), (lwb_ref, kdb_ref, bbb_ref))
    for d in range(2):
        wd = lo[:, d * RW_LORA:(d + 1) * RW_LORA]
        ad = lo[:, (2 + d) * RW_LORA:(3 + d) * RW_LORA]
        w_log = -_softplus(-(w0_ref[d:d + 1, :] + _dot1(jnp.tanh(wd), w2_ref[d]))) - 0.5
        a = _sigmoid(a0_ref[d:d + 1, :] + _dot1(ad, a2_ref[d]))
        kd = kr * (1.0 + (a - 1.0) * ka_ref[...])
        lw_ref, kd_ref, bb_ref = outs[d]
        lw_ref[...] = -jnp.exp(w_log)
        kd_ref[...] = kd
        bb_ref[...] = kk * a
        bonus = bonus + head_sum(r * kd * rk_ref[...]) * v
    bon_ref[...] = bonus


def rwkv_prep(p, lora_in, mu, w0, w2, a0, a2, g2, k_k, k_a, r_k, col0, batch):
    t = p.shape[0]
    s = t // batch
    nct = RW_WIDTH // LANES
    base = col0 // LANES
    nlo = RW_LORA_IN
    col = lambda off: pl.BlockSpec((s, LANES), lambda b, c: (b, base + off * nct + c))
    vec = lambda off: pl.BlockSpec((1, LANES), lambda b, c: (0, off * nct + c))
    mu_main = mu[:RW_MAIN].reshape(1, RW_MAIN)
    mu_lo = mu[RW_MAIN:].reshape(1, nlo)
    out_spec = pl.BlockSpec((s, LANES), lambda b, c: (b, c))
    n_out = 11
    return pl.pallas_call(
        _rw_prep_body,
        grid=(batch, nct),
        in_specs=[col(0), col(1), col(2),
                  pl.BlockSpec((s, nlo), lambda b, c: (b, 0)),
                  vec(0), vec(1), vec(2),
                  pl.BlockSpec((1, nlo), lambda b, c: (0, 0)),
                  pl.BlockSpec((2, LANES), lambda b, c: (0, c)),
                  pl.BlockSpec((2, RW_LORA, LANES), lambda b, c: (0, 0, c)),
                  pl.BlockSpec((2, LANES), lambda b, c: (0, c)),
                  pl.BlockSpec((2, RW_LORA, LANES), lambda b, c: (0, 0, c)),
                  pl.BlockSpec((RW_GATE_LORA, LANES), lambda b, c: (0, c)),
                  vec(0), vec(0), vec(0)],
        out_specs=[out_spec] * n_out,
        out_shape=[jax.ShapeDtypeStruct((t, RW_WIDTH), F32)] * n_out,
        compiler_params=_params("parallel", "parallel"),
        name="rwkv_prep",
    )(p, p, p, lora_in, mu_main, mu_main, mu_main, mu_lo, w0, w2, a0, a2, g2,
      k_k.reshape(1, RW_WIDTH), k_a.reshape(1, RW_WIDTH), r_k.reshape(1, RW_WIDTH))


def _rw_scan_body(r_ref, v_ref, kk_ref, bon_ref, gate_ref, lwf_ref, lwb_ref, kdf_ref, kdb_ref, bbf_ref, bbb_ref,
                  lnw_ref, lnb_ref, y_ref, cum_ref, cumt_ref, kdt_ref, bbt_ref, yf_ref, yb_ref,
                  tr_ref, tc_ref, rbk_ref, bkt_ref):
    s = r_ref.shape[0]
    c = CHUNK
    c2 = 2 * c
    row = lax.broadcasted_iota(jnp.int32, (s, 1), 0)
    pos = row % c
    lw_refs, kd_refs, bb_refs = (lwf_ref, lwb_ref), (kdf_ref, kdb_ref), (bbf_ref, bbb_ref)
    for d in range(2):
        cum = _seg_cumsum(lw_refs[d][...], pos, d == 1)
        cum_ref[d] = cum
        cumt_ref[d] = cum.T
        kdt_ref[d] = kd_refs[d][...].T
        bbt_ref[d] = bb_refs[d][...].T

    masks = _tri_masks(c2, c)
    merge = {rev: _merge_masks(c2, rev) for rev in (False, True)}
    eye2_f = jnp.where(masks["eye"], 1.0, 0.0)
    head0 = lax.broadcasted_iota(jnp.int32, (1, LANES), 1) < RW_HEAD_DIM
    head_bd = _head_block_diag()
    npair = s // PAIR
    group = min(RW_PREP_PAIRS, npair)

    def stack(x):
        return jnp.concatenate([jnp.where(head0, x, 0.0), jnp.where(head0, 0.0, x)], axis=0)

    def prep(pidx, d, half):
        reverse = d == 1
        incl2, strict2 = masks[reverse]
        cidx = 2 * pidx + half
        cols = pl.ds(pl.multiple_of(pidx * PAIR, PAIR), PAIR)
        rows = pl.ds(pl.multiple_of(cidx * c, c), c)
        hs = slice(half * c, (half + 1) * c)
        r, v, kk = r_ref[rows, :], v_ref[rows, :], kk_ref[rows, :]
        lw, kd, bb = lw_refs[d][rows, :], kd_refs[d][rows, :], bb_refs[d][rows, :]
        cum = cum_ref[d, rows, :]
        e_neg = jnp.exp(-cum)
        a_s = stack(-kk * jnp.exp(cum - lw))
        r_s = stack(r * jnp.exp(cum))
        ar = jnp.concatenate([a_s, r_s], axis=0)
        bk = jnp.concatenate([stack(bb * e_neg), stack(kd * e_neg)], axis=0)
        g = _dot1(ar, bk, _NT)
        yield
        rbk_ref[d, cidx] = jnp.concatenate([jnp.where(incl2, g[c2:, :c2], 0.0),
                                            jnp.where(incl2, g[c2:, c2:], 0.0)], axis=1).astype(BF16)
        av = _dot1(jnp.where(strict2, g[:c2, c2:], 0.0), stack(v))
        yield
        t_inv = []
        yield from _tri_inverse(-jnp.where(strict2, g[:c2, :c2], 0.0), eye2_f, merge[reverse], t_inv)
        tt = _dot1(t_inv[0], jnp.concatenate([a_s, av], axis=1))
        yield
        tr_ref[d, cidx] = jnp.concatenate([tt[:, :LANES], r_s], axis=0).astype(BF16)
        tc_ref[d, cidx] = tt[:, LANES:].astype(BF16)
        cum_t = cumt_ref[d, :, cols][:, hs]
        last = 0 if reverse else c - 1
        e_out_t = jnp.exp(cum_t[:, last:last + 1] - cum_t)
        bkt_ref[d, cidx] = jnp.concatenate([bbt_ref[d, :, cols][:, hs] * e_out_t,
                                            kdt_ref[d, :, cols][:, hs] * e_out_t], axis=1).astype(BF16)

    def seq(states, pidx, d):
        cum_t2 = cumt_ref[d, :, pl.ds(pl.multiple_of(pidx * PAIR, PAIR), PAIR)]
        for half in ((1, 0) if d else (0, 1)):
            cidx = 2 * pidx + half
            rows = pl.ds(pl.multiple_of(cidx * c, c), c)
            v = v_ref[rows, :]
            x = _dot1(tr_ref[d, cidx], states[d])
            yield
            ps = x[:c2] + tc_ref[d, cidx].astype(F32)
            os_ = x[c2:] + _dot1(rbk_ref[d, cidx], jnp.concatenate([ps, stack(v)], axis=0))
            y = os_[:c] + os_[c:]
            if d:
                yb_ref[rows, :] = y
            else:
                yf_ref[rows, :] = y
            pv = jnp.concatenate([ps[:c] + ps[c:], v], axis=0)
            last = half * c + (0 if d else c - 1)
            e_tot = jnp.exp(cum_t2[:, last:last + 1])
            states[d] = states[d] * e_tot + jnp.where(head_bd, _dot1(bkt_ref[d, cidx], pv), 0.0)
            yield

    ngroups = npair // group

    def prep_chains(g):
        return [prep(npair - 1 - (g * group + j) if d else g * group + j, d, half)
                for j in range(group) for d in range(2) for half in range(2)]

    def seq_chain(states, g, d):
        for j in range(group):
            yield from seq(states, npair - 1 - (g * group + j) if d else g * group + j, d)

    def step(g, carry):
        states = list(carry)
        _interleave(prep_chains(g + 1) + [seq_chain(states, g, 0), seq_chain(states, g, 1)])
        return tuple(states)

    zero = jnp.zeros((LANES, LANES), F32)
    _interleave(prep_chains(0))
    states = list(lax.fori_loop(0, ngroups - 1, step, (zero, zero)))
    _interleave([seq_chain(states, ngroups - 1, 0), seq_chain(states, ngroups - 1, 1)])

    bd = jnp.where(head_bd, 1.0, 0.0)
    y = yf_ref[...] + yb_ref[...]
    mean = _dot_sum(y, bd) * (1.0 / RW_HEAD_DIM)
    yc = y - mean
    var = _dot_sum(yc * yc, bd) * (1.0 / RW_HEAD_DIM)
    yn = yc * lax.rsqrt(var + RW_GN_EPS) * lnw_ref[...] + lnb_ref[...]
    y_ref[...] = ((yn + bon_ref[...]) * gate_ref[...]).astype(y_ref.dtype)


def rwkv_scan(prep, ln_w, ln_b, batch):
    t = prep[0].shape[0]
    s = t // batch
    assert s % PAIR == 0 and (s // PAIR) % min(RW_PREP_PAIRS, s // PAIR) == 0
    nct = RW_WIDTH // LANES
    nchunk = s // CHUNK
    blk = pl.BlockSpec((s, LANES), lambda b, c: (b, c))
    vec = pl.BlockSpec((1, LANES), lambda b, c: (0, c))
    tsp = pltpu.VMEM((2, LANES, s), F32)
    big = pltpu.VMEM((s, LANES), F32)
    per_chunk = lambda rows, cols: pltpu.VMEM((2, nchunk, rows, cols), BF16)
    return pl.pallas_call(
        _rw_scan_body,
        grid=(batch, nct),
        in_specs=[blk] * 11 + [vec, vec],
        out_specs=blk,
        out_shape=jax.ShapeDtypeStruct((t, RW_WIDTH), BF16),
        scratch_shapes=[pltpu.VMEM((2, s, LANES), F32), tsp, tsp, tsp, big, big,
                        per_chunk(2 * PAIR, LANES), per_chunk(PAIR, LANES), per_chunk(PAIR, 2 * LANES),
                        per_chunk(PAIR, LANES)],
        compiler_params=_params("parallel", "parallel"),
        name="rwkv_scan",
    )(*prep, ln_w.reshape(1, RW_WIDTH), ln_b.reshape(1, RW_WIDTH))


def even_mixer(x, g, w_in, w_sc, w_ab, w_out, layer, conv_qkv, a_log, dt_bias, out_gain, conv_sc, batch):
    p, ab = norm_mm(x, g, w_in, layer, ncols=4 * DN_WIDTH, w2=w_sc, wp_t=w_ab.T, out_dtype=BF16)
    zeros = jnp.zeros((2 * DN_HEADS,), F32)
    alog32 = jnp.concatenate([zeros, a_log.reshape(-1)]).reshape(1, -1)
    dt32 = jnp.concatenate([zeros, dt_bias.reshape(-1)]).reshape(1, -1)
    y_dn = deltanet(p, ab, conv_qkv.T, alog32, dt32, out_gain, batch)
    y_sc = short_conv(p, conv_sc.T, 4 * DN_WIDTH, batch)
    return mm_res(y_dn, w_out, layer, x, a2=y_sc, tn=D_MODEL)


def odd_mixer(x, g, w_in, w_lora, w_out, layer, qn, kn, bias_tab, mu, w0, w2, a0, a2, g2, k_k, k_a, r_k, ln_w, ln_b,
              batch):
    p, lora_in = norm_mm(x, g, w_in, layer, ncols=3 * DSA_QKV + RW_MAIN, wp=w_lora, out_dtype=BF16)
    y_c = dilated_attention(p, qn, kn, bias_tab, batch)
    prep = rwkv_prep(p, lora_in, mu, w0, w2, a0, a2, g2, k_k, k_a, r_k, 3 * DSA_QKV, batch)
    y_d = rwkv_scan(prep, ln_w, ln_b, batch)
    return mm_res(y_c, w_out, layer, x, a2=y_d, tn=D_MODEL)


def kernel(x, mem, rel_bias, norm_mix, norm_xattn, norm_mem, norm_ffn, xa_wq, xa_wk, xa_wv, xa_wo, xa_qn, xa_kn, ffn_w1, ffn_w2, ev_w_in, ev_w_out, dn_conv, dn_a_log, dn_dt_bias, dn_norm, sc_conv, od_w_in, od_w_out, ca_qn, ca_kn, rw_mu, rw_w0, rw_w2, rw_a0, rw_a2, rw_g2, rw_k_k, rw_k_a, rw_r_k, rw_ln_w, rw_ln_b):
    batch, seq, d = x.shape
    n_mem = mem.shape[1]
    xf = x.reshape(batch * seq, d)
    memf = mem.reshape(batch * n_mem, d)
    bias_tab = _dsa_bias_table(rel_bias)
    c_ab = 4 * DN_WIDTH
    c_sc = c_ab + 4 * DN_HEADS
    c_lo = 3 * DSA_QKV + RW_MAIN
    ev_in, od_in = ev_w_in.astype(BF16), od_w_in.astype(BF16)
    ev_sc = ev_in[:, :, c_sc:]
    ev_out, od_out = ev_w_out.astype(BF16), od_w_out.astype(BF16)
    w_q, w_o = xa_wq.astype(BF16), xa_wo.astype(BF16)
    w_kv = jnp.concatenate([xa_wk, xa_wv], axis=2).astype(BF16)
    for layer in range(DEPTH):
        i = layer // 2
        if layer % 2 == 0:
            xf = even_mixer(xf, norm_mix[layer], ev_in, ev_sc, ev_w_in[i, :, c_ab:c_sc], ev_out, i, dn_conv[i],
                            dn_a_log[i], dn_dt_bias[i], dn_norm[i], sc_conv[i], batch)
        else:
            xf = odd_mixer(xf, norm_mix[layer], od_in, od_w_in[i, :, c_lo:], od_out, i, ca_qn[i], ca_kn[i], bias_tab,
                           rw_mu[i], rw_w0[i], rw_w2[i], rw_a0[i], rw_a2[i], rw_g2[i], rw_k_k[i], rw_k_a[i], rw_r_k[i],
                           rw_ln_w[i], rw_ln_b[i], batch)
        kv = norm_mm(memf, norm_mem[layer], w_kv, layer).reshape(batch, n_mem, 2 * XA_WIDTH)
        xf, hn = xattn(xf, norm_xattn[layer], w_q, kv, w_o, layer, xa_qn[layer], xa_kn[layer], norm_ffn[layer], batch)
        h1 = act_mm(hn, ffn_w1, layer, act="relu2", out_dtype=BF16)
        xf = mm_res(h1, ffn_w2, layer, xf)
    return xf.reshape(batch, seq, d)
```
